```python
import jax, jax.numpy as jnp
from jax import lax
import numpy as np

D_MODEL = 1024
BATCH = 32
SEQ = 256
DEPTH = 2
DEC_BATCH = 8
DEC_SEQ = 1024
PAST_LEN = 512

GRID_W = 64
N_MIXERS = 2
N_GLA_LAYERS = (DEPTH + 1) // 2
N_RWKV_LAYERS = DEPTH // 2
EPS = 1e-6
GLA_HEADS = 4
GLA_DK = D_MODEL // 2
GLA_DV = D_MODEL
GLA_DKH = GLA_DK // GLA_HEADS
GLA_DVH = GLA_DV // GLA_HEADS
GLA_IN = 2 * GLA_DK + 2 * GLA_DV
GLA_GATE_RANK = 16
GLA_GATE_NORM = 16.0
GLA_CHUNK = 64
RWKV_HEAD = 64
RWKV_HEADS = D_MODEL // RWKV_HEAD
RWKV_W_RANK = 64
RWKV_A_RANK = 64
RWKV_N_MIX = 6
LNX_EPS = 64e-5

kernel_name = "bidir_gla_rwkv7_flow_trunk"

F32 = jnp.float32


def rmsnorm(x, w):
    x32 = x.astype(F32)
    y = x32 * lax.rsqrt(jnp.mean(x32 * x32, axis=-1, keepdims=True) + EPS)
    return (y * w.astype(F32)).astype(x.dtype)


def shift_ctx(x):
    p = jnp.pad(x, ((0, 0), (1, 1), (0, 0)))
    return 0.5 * (p[:, :-2] + p[:, 2:])


def shift_grid(x):
    B, T, D = x.shape
    rows = T // GRID_W
    g = x.reshape(B, rows, GRID_W, D)
    p = jnp.pad(g, ((0, 0), (1, 1), (1, 1), (0, 0)))
    nb = 0.25 * (p[:, :-2, 1:-1] + p[:, 2:, 1:-1] + p[:, 1:-1, :-2] + p[:, 1:-1, 2:])
    return nb.reshape(B, T, D)


def gla_chunked(q, k, v, log_a, s0):
    B, T, H, _ = q.shape
    dv = v.shape[-1]
    n = T // GLA_CHUNK
    ch = lambda t: t.astype(F32).reshape(B, n, GLA_CHUNK, H, t.shape[-1])
    qc, kc, vc, gc = ch(q), ch(k), ch(v), ch(log_a)
    b = jnp.cumsum(gc, axis=2)
    b_last = b[:, :, -1:]
    qb = qc * jnp.exp(b)
    kb = kc * jnp.exp(-b)
    kd = kc * jnp.exp(b_last - b)
    scores = jnp.einsum('bnchd,bnshd->bnhcs', qb, kb)
    causal = jnp.tril(jnp.ones((GLA_CHUNK, GLA_CHUNK), dtype=bool))
    scores = jnp.where(causal, scores, 0.0)
    o_intra = jnp.einsum('bnhcs,bnshe->bnche', scores, vc)
    u = jnp.einsum('bnchd,bnche->bnhde', kd, vc)
    decay = jnp.exp(b_last[:, :, 0])

    def step(s, inp):
        dec, uu = inp
        return dec[..., None] * s + uu, s

    s_fin, s_start = lax.scan(step, s0.astype(F32), (jnp.moveaxis(decay, 1, 0), jnp.moveaxis(u, 1, 0)))
    s_start = jnp.moveaxis(s_start, 0, 1)
    o_inter = jnp.einsum('bnchd,bnhde->bnche', qb, s_start)
    return (o_intra + o_inter).reshape(B, T, H, dv), s_fin


def gla_mixer(h, w_in, w_a1, w_a2, b_a, w_norm, w_out, s0_fwd, s0_bwd):
    B, T, _ = h.shape
    q, k, v, g = jnp.split(h @ w_in, [GLA_DK, 2 * GLA_DK, 2 * GLA_DK + GLA_DV], axis=-1)
    q = q.reshape(B, T, GLA_HEADS, GLA_DKH) * (GLA_DKH ** -0.5)
    k = k.reshape(B, T, GLA_HEADS, GLA_DKH)
    v = v.reshape(B, T, GLA_HEADS, GLA_DVH)

    def log_gate(d):
        z = (h @ w_a1[d]) @ w_a2[d] + b_a[d]
        return (jax.nn.log_sigmoid(z.astype(F32)) / GLA_GATE_NORM).reshape(B, T, GLA_HEADS, GLA_DKH)

    flip = lambda t: jnp.flip(t, axis=1)
    o_f, s_f = gla_chunked(q, k, v, log_gate(0), s0_fwd)
    o_b, s_b = gla_chunked(flip(q), flip(k), flip(v), flip(log_gate(1)), s0_bwd)
    o = o_f + flip(o_b)
    o = o * lax.rsqrt(jnp.mean(o * o, axis=-1, keepdims=True) + EPS) * w_norm.astype(F32)
    o = o.reshape(B, T, GLA_DV).astype(h.dtype) * jax.nn.silu(g)
    return o @ w_out, jnp.stack([s_f, s_b], axis=1).astype(h.dtype)


def rwkv7_scan(r, decay, k, v, a_vec, b_vec, s0):
    def step(s, inp):
        rt, wt, kt, vt, at, bt = inp
        sa = jnp.einsum('bhij,bhj->bhi', s, at)
        s = s * wt[:, :, None, :] + sa[..., None] * bt[:, :, None, :] + vt[..., None] * kt[:, :, None, :]
        return s, jnp.einsum('bhij,bhj->bhi', s, rt)

    xs = tuple(jnp.moveaxis(t.astype(F32), 1, 0) for t in (r, decay, k, v, a_vec, b_vec))
    s_fin, ys = lax.scan(step, s0.astype(F32), xs)
    return jnp.moveaxis(ys, 0, 1), s_fin


def rwkv7_mixer(h, h_shift, mu, w_rkvg, w0, w1, w2, a0, a1, a2, k_k, k_a, r_k, lnx_w, lnx_b, w_out,
                s0_fwd, s0_bwd):
    B, T, D = h.shape
    xm = h[None] + (h_shift - h)[None] * mu[:, None, None, :]
    r, k, v, g = jnp.einsum('pbtd,pde->pbte', xm[:4], w_rkvg)
    xw, xa = xm[4], xm[5]
    heads = lambda t: t.reshape(B, T, RWKV_HEADS, RWKV_HEAD)
    flip = lambda t: jnp.flip(t, axis=1)
    rh, vh = heads(r), heads(v)
    y_sum = 0.0
    bonus = 0.0
    states = []
    for d in range(2):
        w_log = -jax.nn.softplus(-(w0[d] + jnp.tanh(xw @ w1[d]) @ w2[d])) - 0.5
        decay = jnp.exp(-jnp.exp(w_log.astype(F32)))
        a = jax.nn.sigmoid(a0[d] + (xa @ a1[d]) @ a2[d])
        kk = heads(k * k_k[d]).astype(F32)
        kk = kk / jnp.maximum(jnp.sqrt(jnp.sum(kk * kk, axis=-1, keepdims=True)), 1e-12)
        kd = heads(k * (1.0 + (a - 1.0) * k_a[d]))
        ins = (rh, heads(decay), kd, vh, -kk, kk * heads(a).astype(F32))
        s0 = s0_fwd if d == 0 else s0_bwd
        if d == 1:
            ins = tuple(flip(t) for t in ins)
        y, s = rwkv7_scan(*ins, s0)
        y_sum = y_sum + (flip(y) if d == 1 else y)
        states.append(s)
        bonus = bonus + jnp.sum((rh * kd * r_k[d]).astype(F32), axis=-1, keepdims=True) * vh.astype(F32)
    mean = jnp.mean(y_sum, axis=-1, keepdims=True)
    var = jnp.mean(jnp.square(y_sum - mean), axis=-1, keepdims=True)
    yn = ((y_sum - mean) * lax.rsqrt(var + LNX_EPS)).reshape(B, T, D) * lnx_w.astype(F32) + lnx_b.astype(F32)
    o = (yn + bonus.reshape(B, T, D)).astype(h.dtype) * jax.nn.silu(g)
    return o @ w_out, jnp.stack(states, axis=1).astype(h.dtype)


def run_trunk(x, cond, shift_fn, gla_s0, rwkv_s0, p):
    gla_states, rwkv_states = [], []
    for i in range(DEPTH):
        mod = (jax.nn.silu(cond) @ p['w_mod'][i] + p['b_mod'][i])[:, None, :]
        shift, scale, gate = jnp.split(mod, 3, axis=-1)
        h = rmsnorm(x, p['norm_w'][i]) * (1.0 + scale) + shift
        j = i // N_MIXERS
        if i % N_MIXERS == 0:
            out, s = gla_mixer(h, p['gla_w_in'][j], p['gla_w_a1'][j], p['gla_w_a2'][j], p['gla_b_a'][j],
                               p['gla_norm'][j], p['gla_w_out'][j], gla_s0[:, j, 0], gla_s0[:, j, 1])
            gla_states.append(s)
        else:
            out, s = rwkv7_mixer(h, shift_fn(h), p['rwkv_mu'][j], p['rwkv_w_rkvg'][j], p['rwkv_w0'][j],
                                 p['rwkv_w1'][j], p['rwkv_w2'][j], p['rwkv_a0'][j], p['rwkv_a1'][j],
                                 p['rwkv_a2'][j], p['rwkv_k_k'][j], p['rwkv_k_a'][j], p['rwkv_r_k'][j],
                                 p['rwkv_lnx_w'][j], p['rwkv_lnx_b'][j], p['rwkv_w_out'][j],
                                 rwkv_s0[:, j, 0], rwkv_s0[:, j, 1])
            rwkv_states.append(s)
        x = x + gate * out
    return rmsnorm(x, p['norm_f']), jnp.stack(gla_states, axis=1), jnp.stack(rwkv_states, axis=1)


def setup_inputs(seed: int = 0) -> dict:
    key = jax.random.key(seed)
    ks = iter(list(jax.random.split(key, 40)))
    nrm = lambda shape, s: jax.random.normal(next(ks), shape, F32) * s
    uni = lambda shape, lo, hi: jax.random.uniform(next(ks), shape, F32, minval=lo, maxval=hi)
    D, Lg, Lr = D_MODEL, N_GLA_LAYERS, N_RWKV_LAYERS
    return {
        "x_prompt": nrm((BATCH, SEQ, D), 1.0),
        "x_sample": nrm((DEC_BATCH, DEC_SEQ, D), 1.0),
        "state_gla": nrm((DEC_BATCH, Lg, 2, GLA_HEADS, GLA_DKH, GLA_DVH), 1.0),
        "state_rwkv": nrm((DEC_BATCH, Lr, 2, RWKV_HEADS, RWKV_HEAD, RWKV_HEAD), 0.3),
        "c": nrm((DEC_BATCH, D), 1.0),
        "c_ctx": nrm((D,), 1.0),
        "w_mod": nrm((DEPTH, D, 3 * D), 0.5 * D ** -0.5),
        "b_mod": nrm((DEPTH, 3 * D), 0.02),
        "norm_w": 1.0 + nrm((DEPTH, D), 0.02),
        "gla_w_in": nrm((Lg, D, GLA_IN), D ** -0.5),
        "gla_w_a1": nrm((Lg, 2, D, GLA_GATE_RANK), D ** -0.5),
        "gla_w_a2": nrm((Lg, 2, GLA_GATE_RANK, GLA_DK), GLA_GATE_RANK ** -0.5),
        "gla_b_a": 1.0 + nrm((Lg, 2, GLA_DK), 0.5),
        "gla_norm": 1.0 + nrm((Lg, GLA_DVH), 0.02),
        "gla_w_out": nrm((Lg, GLA_DV, D), GLA_DV ** -0.5),
        "rwkv_mu": uni((Lr, RWKV_N_MIX, D), 0.0, 1.0),
        "rwkv_w_rkvg": nrm((Lr, 4, D, D), D ** -0.5),
        "rwkv_w0": uni((Lr, 2, D), -6.0, 1.0),
        "rwkv_w1": nrm((Lr, 2, D, RWKV_W_RANK), D ** -0.5),
        "rwkv_w2": nrm((Lr, 2, RWKV_W_RANK, D), 0.1 * RWKV_W_RANK ** -0.5),
        "rwkv_a0": nrm((Lr, 2, D), 0.1),
        "rwkv_a1": nrm((Lr, 2, D, RWKV_A_RANK), D ** -0.5),
        "rwkv_a2": nrm((Lr, 2, RWKV_A_RANK, D), 0.1 * RWKV_A_RANK ** -0.5),
        "rwkv_k_k": 0.85 + nrm((Lr, 2, D), 0.02),
        "rwkv_k_a": 1.0 + nrm((Lr, 2, D), 0.02),
        "rwkv_r_k": nrm((Lr, 2, RWKV_HEADS, RWKV_HEAD), 0.1),
        "rwkv_lnx_w": 1.0 + nrm((Lr, D), 0.02),
        "rwkv_lnx_b": nrm((Lr, D), 0.02),
        "rwkv_w_out": nrm((Lr, D, D), D ** -0.5),
        "norm_f": 1.0 + nrm((D,), 0.02),
    }


def reference(x_prompt, x_sample, state_gla, state_rwkv, c, c_ctx, w_mod, b_mod, norm_w,
              gla_w_in, gla_w_a1, gla_w_a2, gla_b_a, gla_norm, gla_w_out,
              rwkv_mu, rwkv_w_rkvg, rwkv_w0, rwkv_w1, rwkv_w2, rwkv_a0, rwkv_a1, rwkv_a2,
              rwkv_k_k, rwkv_k_a, rwkv_r_k, rwkv_lnx_w, rwkv_lnx_b, rwkv_w_out, norm_f):
    p = dict(w_mod=w_mod, b_mod=b_mod, norm_w=norm_w, gla_w_in=gla_w_in, gla_w_a1=gla_w_a1,
             gla_w_a2=gla_w_a2, gla_b_a=gla_b_a, gla_norm=gla_norm, gla_w_out=gla_w_out,
             rwkv_mu=rwkv_mu, rwkv_w_rkvg=rwkv_w_rkvg, rwkv_w0=rwkv_w0, rwkv_w1=rwkv_w1,
             rwkv_w2=rwkv_w2, rwkv_a0=rwkv_a0, rwkv_a1=rwkv_a1, rwkv_a2=rwkv_a2, rwkv_k_k=rwkv_k_k,
             rwkv_k_a=rwkv_k_a, rwkv_r_k=rwkv_r_k, rwkv_lnx_w=rwkv_lnx_w, rwkv_lnx_b=rwkv_lnx_b,
             rwkv_w_out=rwkv_w_out, norm_f=norm_f)
    Bp = x_prompt.shape[0]
    zero_gla = jnp.zeros((Bp, N_GLA_LAYERS, 2, GLA_HEADS, GLA_DKH, GLA_DVH), x_prompt.dtype)
    zero_rwkv = jnp.zeros((Bp, N_RWKV_LAYERS, 2, RWKV_HEADS, RWKV_HEAD, RWKV_HEAD), x_prompt.dtype)
    y_prompt, new_state_gla, new_state_rwkv = run_trunk(x_prompt, c_ctx[None, :], shift_ctx,
                                                        zero_gla, zero_rwkv, p)
    y_sample, _, _ = run_trunk(x_sample, c, shift_grid, state_gla, state_rwkv, p)
    return (y_prompt, y_sample, new_state_gla, new_state_rwkv)
```

```python
import functools

import jax
import jax.numpy as jnp
from jax import lax
from jax.experimental import pallas as pl
from jax.experimental.pallas import tpu as pltpu

F32 = jnp.float32
BF16 = jnp.bfloat16

D_MODEL = 1024
EPS = 1e-6
GRID_W = 64
GLA_HEADS = 4
GLA_DK = 512
GLA_DV = 1024
GLA_DKH = 128
GLA_DVH = 256
GLA_GATE_RANK = 16
GLA_GATE_NORM = 16.0
RWKV_HEAD = 64
RWKV_HEADS = 16
RWKV_PAIRS = 8
RWKV_RANK = 64
LNX_EPS = 64e-5
CHUNK = 64
LANES = 128
VMEM_LIMIT = 56 * 1024 * 1024


def _dot(a, b):
    return jnp.dot(a.astype(BF16), b.astype(BF16), preferred_element_type=F32)


def _dot_nt(a, b):
    return lax.dot_general(a.astype(BF16), b.astype(BF16), (((1,), (1,)), ((), ())),
                           preferred_element_type=F32)


def _dot_tn(a, b):
    return lax.dot_general(a.astype(BF16), b.astype(BF16), (((0,), (0,)), ((), ())),
                           preferred_element_type=F32)


def _tri_dot(tri, x):
    hi = x.astype(BF16)
    r1 = x - hi.astype(F32)
    mid = r1.astype(BF16)
    lo = (r1 - mid.astype(F32)).astype(BF16)
    t = tri.astype(BF16)
    d = lambda p: jnp.dot(t, p, preferred_element_type=F32)
    return d(hi) + d(mid) + d(lo)


def _log_sigmoid(z):
    return jnp.minimum(z, 0.0) - jnp.log(1.0 + jnp.exp(-jnp.abs(z)))


def _sigmoid(z):
    return 1.0 / (1.0 + jnp.exp(-z))


def _silu(z):
    return z * _sigmoid(z)


def _rms(x):
    return x * lax.rsqrt(jnp.mean(x * x, axis=-1, keepdims=True) + EPS)


def _params(n_grid_dims):
    return pltpu.CompilerParams(dimension_semantics=("arbitrary",) * n_grid_dims,
                                vmem_limit_bytes=VMEM_LIMIT)


def _mm_kernel(x_ref, w_ref, b_ref, o_ref):
    o_ref[...] = _dot(x_ref[...], w_ref[...]) + b_ref[...]


def _matmul_bias(x, w, b):
    m, k = x.shape
    n = w.shape[1]
    return pl.pallas_call(
        _mm_kernel,
        grid=(1,),
        in_specs=[pl.BlockSpec((m, k), lambda i: (0, 0)),
                  pl.BlockSpec((k, n), lambda i: (0, 0)),
                  pl.BlockSpec((1, n), lambda i: (0, 0))],
        out_specs=pl.BlockSpec((m, n), lambda i: (0, 0)),
        out_shape=jax.ShapeDtypeStruct((m, n), F32),
        compiler_params=_params(1),
        name="mod_matmul",
    )(x, w.astype(BF16), b)


def _gla_front_kernel(x_ref, mod_ref, nw_ref, win_ref, wa1_ref, wa2_ref, ba_ref,
                      q_ref, k_ref, v_ref, g_ref, lg_ref):
    d = D_MODEL
    mod = mod_ref[0]
    shift, scale = mod[:, :d], mod[:, d:2 * d]
    h = _rms(x_ref[...]) * nw_ref[...] * (1.0 + scale) + shift
    hb = h.astype(BF16)
    q_ref[...] = jnp.dot(hb, win_ref[:, 0:GLA_DK], preferred_element_type=F32) * (GLA_DKH ** -0.5)
    k_ref[...] = jnp.dot(hb, win_ref[:, GLA_DK:2 * GLA_DK], preferred_element_type=F32)
    v_ref[...] = jnp.dot(hb, win_ref[:, 2 * GLA_DK:2 * GLA_DK + GLA_DV], preferred_element_type=F32)
    g_ref[...] = jnp.dot(hb, win_ref[:, 2 * GLA_DK + GLA_DV:], preferred_element_type=F32)
    t = jnp.dot(hb, wa1_ref[...], preferred_element_type=F32)
    z = _dot(t, wa2_ref[...]) + ba_ref[...]
    lg_ref[...] = _log_sigmoid(z) * (1.0 / GLA_GATE_NORM)


def _gla_front(x, mod, mod_idx, tm, norm_w, w_in, w_a1, w_a2, b_a):
    n, d = x.shape
    nmod = mod.shape[0]
    wa1 = jnp.zeros((d, LANES), F32).at[:, :GLA_GATE_RANK].set(w_a1[0]).at[:, GLA_GATE_RANK:2 * GLA_GATE_RANK].set(w_a1[1])
    wa2 = jnp.zeros((LANES, 2 * GLA_DK), F32).at[:GLA_GATE_RANK, :GLA_DK].set(w_a2[0])
    wa2 = wa2.at[GLA_GATE_RANK:2 * GLA_GATE_RANK, GLA_DK:].set(w_a2[1])
    ba = b_a.reshape(1, 2 * GLA_DK)
    row = lambda i: (i, 0)
    const = lambda i: (0, 0)
    n_in = w_in.shape[1]
    outs = pl.pallas_call(
        _gla_front_kernel,
        grid=(n // tm,),
        in_specs=[pl.BlockSpec((tm, d), row),
                  pl.BlockSpec((1, 1, 3 * d), lambda i: (mod_idx(i), 0, 0)),
                  pl.BlockSpec((1, d), const),
                  pl.BlockSpec((d, n_in), const),
                  pl.BlockSpec((d, LANES), const),
                  pl.BlockSpec((LANES, 2 * GLA_DK), const),
                  pl.BlockSpec((1, 2 * GLA_DK), const)],
        out_specs=[pl.BlockSpec((tm, GLA_DK), row), pl.BlockSpec((tm, GLA_DK), row),
                   pl.BlockSpec((tm, GLA_DV), row), pl.BlockSpec((tm, GLA_DV), row),
                   pl.BlockSpec((tm, 2 * GLA_DK), row)],
        out_shape=[jax.ShapeDtypeStruct((n, GLA_DK), F32), jax.ShapeDtypeStruct((n, GLA_DK), F32),
                   jax.ShapeDtypeStruct((n, GLA_DV), F32), jax.ShapeDtypeStruct((n, GLA_DV), F32),
                   jax.ShapeDtypeStruct((n, 2 * GLA_DK), F32)],
        compiler_params=_params(1),
        name="gla_front",
    )(x, mod.reshape(nmod, 1, 3 * d), norm_w.reshape(1, d), w_in.astype(BF16), wa1.astype(BF16),
      wa2.astype(BF16), ba)
    return outs


def _gla_chunk(q, k, v, g, st, reverse):
    c = q.shape[0]
    row = lax.broadcasted_iota(jnp.int32, (c, c), 0)
    col = lax.broadcasted_iota(jnp.int32, (c, c), 1)
    keep = (col >= row) if reverse else (col <= row)
    b = _tri_dot(keep.astype(F32), g)
    b_end = b[0:1] if reverse else b[c - 1:c]
    qb = q * jnp.exp(b)
    kb = k * jnp.exp(-b)
    kd = k * jnp.exp(b_end - b)
    scores = jnp.where(keep, _dot_nt(qb, kb), 0.0)
    o = _dot(scores, v) + _dot_nt(qb, st)
    st = st * jnp.exp(b_end) + _dot_tn(v, kd)
    return o, st


def _gla_scan_kernel(q_ref, k_ref, v_ref, lf_ref, lb_ref, s0f_ref, s0b_ref, o_ref, sf_ref, sb_ref):
    t = q_ref.shape[1]
    nc = t // CHUNK

    def fwd(i, st):
        sl = pl.ds(pl.multiple_of(i * CHUNK, CHUNK), CHUNK)
        o, st = _gla_chunk(q_ref[0, sl, :], k_ref[0, sl, :], v_ref[0, sl, :], lf_ref[0, sl, :], st, False)
        o_ref[0, sl, :] = o
        return st

    sf_ref[0, 0] = lax.fori_loop(0, nc, fwd, s0f_ref[0, 0])

    def bwd(j, st):
        i = nc - 1 - j
        sl = pl.ds(pl.multiple_of(i * CHUNK, CHUNK), CHUNK)
        o, st = _gla_chunk(q_ref[0, sl, :], k_ref[0, sl, :], v_ref[0, sl, :], lb_ref[0, sl, :], st, True)
        o_ref[0, sl, :] += o
        return st

    sb_ref[0, 0] = lax.fori_loop(0, nc, bwd, s0b_ref[0, 0])


def _gla_scan(q, k, v, lg, s0f, s0b):
    bsz, t, _ = q.shape
    hd = lambda b, h: (b, 0, h)
    st = lambda b, h: (b, h, 0, 0)
    return pl.pallas_call(
        _gla_scan_kernel,
        grid=(bsz, GLA_HEADS),
        in_specs=[pl.BlockSpec((1, t, GLA_DKH), hd), pl.BlockSpec((1, t, GLA_DKH), hd),
                  pl.BlockSpec((1, t, GLA_DVH), hd),
                  pl.BlockSpec((1, t, GLA_DKH), hd),
                  pl.BlockSpec((1, t, GLA_DKH), lambda b, h: (b, 0, GLA_HEADS + h)),
                  pl.BlockSpec((1, 1, GLA_DVH, GLA_DKH), st), pl.BlockSpec((1, 1, GLA_DVH, GLA_DKH), st)],
        out_specs=[pl.BlockSpec((1, t, GLA_DVH), hd),
                   pl.BlockSpec((1, 1, GLA_DVH, GLA_DKH), st), pl.BlockSpec((1, 1, GLA_DVH, GLA_DKH), st)],
        out_shape=[jax.ShapeDtypeStruct((bsz, t, GLA_DV), F32),
                   jax.ShapeDtypeStruct((bsz, GLA_HEADS, GLA_DVH, GLA_DKH), F32),
                   jax.ShapeDtypeStruct((bsz, GLA_HEADS, GLA_DVH, GLA_DKH), F32)],
        compiler_params=_params(2),
        name="gla_scan",
    )(q, k, v, lg, lg, s0f, s0b)


def _gla_back_kernel(o_ref, g_ref, x_ref, mod0_ref, mod1_ref, gn_ref, wout_ref, nw1_ref, x1_ref, h1_ref):
    d = D_MODEL
    o = o_ref[...]
    gn = gn_ref[...]
    parts = [_rms(o[:, h * GLA_DVH:(h + 1) * GLA_DVH]) * gn for h in range(GLA_HEADS)]
    on = jnp.concatenate(parts, axis=-1) * _silu(g_ref[...])
    out = _dot(on, wout_ref[...])
    gate = mod0_ref[0][:, 2 * d:]
    x1 = x_ref[...] + gate * out
    x1_ref[...] = x1
    mod1 = mod1_ref[0]
    h1_ref[...] = _rms(x1) * nw1_ref[...] * (1.0 + mod1[:, d:2 * d]) + mod1[:, :d]


def _gla_back(o, g, x, mod0, mod1, mod_idx, tm, gla_norm, w_out, norm_w1):
    n, d = x.shape
    nmod = mod0.shape[0]
    row = lambda i: (i, 0)
    const = lambda i: (0, 0)
    modspec = pl.BlockSpec((1, 1, 3 * d), lambda i: (mod_idx(i), 0, 0))
    return pl.pallas_call(
        _gla_back_kernel,
        grid=(n // tm,),
        in_specs=[pl.BlockSpec((tm, d), row), pl.BlockSpec((tm, d), row), pl.BlockSpec((tm, d), row),
                  modspec, modspec,
                  pl.BlockSpec((1, GLA_DVH), const), pl.BlockSpec((d, d), const), pl.BlockSpec((1, d), const)],
        out_specs=[pl.BlockSpec((tm, d), row), pl.BlockSpec((tm, d), row)],
        out_shape=[jax.ShapeDtypeStruct((n, d), F32), jax.ShapeDtypeStruct((n, d), F32)],
        compiler_params=_params(1),
        name="gla_back",
    )(o, g, x, mod0.reshape(nmod, 1, 3 * d), mod1.reshape(nmod, 1, 3 * d), gla_norm.reshape(1, GLA_DVH),
      w_out.astype(BF16), norm_w1.reshape(1, d))


def _rwkv_front_kernel(h_ref, hs_ref, mu_ref, wrkvg_ref, w1_ref, w2_ref, w0_ref, a1_ref, a2_ref, a0_ref,
                       r_ref, k_ref, v_ref, g_ref, lw_ref, a_ref):
    h = h_ref[...]
    dh = hs_ref[...] - h
    mix = lambda p: (h + dh * mu_ref[p:p + 1, :]).astype(BF16)
    r_ref[...] = jnp.dot(mix(0), wrkvg_ref[0], preferred_element_type=F32)
    k_ref[...] = jnp.dot(mix(1), wrkvg_ref[1], preferred_element_type=F32)
    v_ref[...] = jnp.dot(mix(2), wrkvg_ref[2], preferred_element_type=F32)
    g_ref[...] = jnp.dot(mix(3), wrkvg_ref[3], preferred_element_type=F32)
    tw = jnp.tanh(jnp.dot(mix(4), w1_ref[...], preferred_element_type=F32))
    w_log = _log_sigmoid(w0_ref[...] + _dot(tw, w2_ref[...])) - 0.5
    lw_ref[...] = -jnp.exp(w_log)
    ta = jnp.dot(mix(5), a1_ref[...], preferred_element_type=F32)
    a_ref[...] = _sigmoid(a0_ref[...] + _dot(ta, a2_ref[...]))


def _rwkv_front(h, hs, tm, mu, w_rkvg, w0, w1, w2, a0, a1, a2):
    n, d = h.shape
    rk = RWKV_RANK
    cat1 = lambda w: jnp.concatenate([w[0], w[1]], axis=1)
    bd2 = lambda w: jnp.zeros((2 * rk, 2 * d), F32).at[:rk, :d].set(w[0]).at[rk:, d:].set(w[1])
    row = lambda i: (i, 0)
    const = lambda i: (0, 0)
    return pl.pallas_call(
        _rwkv_front_kernel,
        grid=(n // tm,),
        in_specs=[pl.BlockSpec((tm, d), row), pl.BlockSpec((tm, d), row),
                  pl.BlockSpec((8, d), const),
                  pl.BlockSpec((4, d, d), lambda i: (0, 0, 0)),
                  pl.BlockSpec((d, 2 * rk), const), pl.BlockSpec((2 * rk, 2 * d), const), pl.BlockSpec((1, 2 * d), const),
                  pl.BlockSpec((d, 2 * rk), const), pl.BlockSpec((2 * rk, 2 * d), const), pl.BlockSpec((1, 2 * d), const)],
        out_specs=[pl.BlockSpec((tm, d), row)] * 4 + [pl.BlockSpec((tm, 2 * d), row)] * 2,
        out_shape=[jax.ShapeDtypeStruct((n, d), F32)] * 4 + [jax.ShapeDtypeStruct((n, 2 * d), F32)] * 2,
        compiler_params=_params(1),
        name="rwkv_front",
    )(h, hs, jnp.zeros((8, d), F32).at[:6].set(mu), w_rkvg.astype(BF16),
      cat1(w1).astype(BF16), bd2(w2).astype(BF16), w0.reshape(1, 2 * d),
      cat1(a1).astype(BF16), bd2(a2).astype(BF16), a0.reshape(1, 2 * d))


def _seg_sum(x, head0):
    s0 = jnp.sum(jnp.where(head0, x, 0.0), axis=-1, keepdims=True)
    s1 = jnp.sum(jnp.where(head0, 0.0, x), axis=-1, keepdims=True)
    return jnp.where(head0, s0, s1)


def _stack(x, head0):
    return jnp.concatenate([jnp.where(head0, x, 0.0), jnp.where(head0, 0.0, x)], axis=0)


def _unstack(xs):
    c = xs.shape[0] // 2
    return xs[:c] + xs[c:]


def _rwkv_chunk_terms(r, k, v, lw, a, k_k, k_a, r_k, reverse):
    c = r.shape[0]
    lane = lax.broadcasted_iota(jnp.int32, (c, LANES), 1)
    head0 = lane < RWKV_HEAD
    kk = k * k_k
    kk = kk / jnp.maximum(jnp.sqrt(_seg_sum(kk * kk, head0)), 1e-12)
    kd = k * (1.0 + (a - 1.0) * k_a)
    av = -kk
    bv = kk * a
    bonus = _seg_sum(r * kd * r_k, head0) * v

    row = lax.broadcasted_iota(jnp.int32, (c, c), 0)
    col = lax.broadcasted_iota(jnp.int32, (c, c), 1)
    keep = (col >= row) if reverse else (col <= row)
    cum = _tri_dot(keep.astype(F32), lw)
    tot = cum[0:1] if reverse else cum[c - 1:c]
    e_inc = jnp.exp(cum)
    e_neg = jnp.exp(-cum)
    e_tail = jnp.exp(tot - cum)
    a_t = _stack(av * jnp.exp(cum - lw), head0)
    r_t = _stack(r * e_inc, head0)
    b_t = _stack(bv * e_neg, head0)
    k_t = _stack(kd * e_neg, head0)
    b_h = _stack(bv * e_tail, head0)
    k_h = _stack(kd * e_tail, head0)
    v_s = _stack(v, head0)

    srow = lax.broadcasted_iota(jnp.int32, (2 * c, 2 * c), 0)
    scol = lax.broadcasted_iota(jnp.int32, (2 * c, 2 * c), 1)
    same = (srow >= c) == (scol >= c)
    ti, si = srow & (c - 1), scol & (c - 1)
    strict = same & ((si > ti) if reverse else (si < ti))
    incl = same & ((si >= ti) if reverse else (si <= ti))
    n_ab = jnp.where(strict, _dot_nt(a_t, b_t), 0.0)
    l_ak = jnp.where(strict, _dot_nt(a_t, k_t), 0.0)
    p_rb = jnp.where(incl, _dot_nt(r_t, b_t), 0.0)
    p_rk = jnp.where(incl, _dot_nt(r_t, k_t), 0.0)

    tinv, q = n_ab, n_ab
    steps = max(1, (c - 1).bit_length()) - 1
    for _ in range(steps):
        q = _dot(q, q)
        tinv = tinv + q + _dot(tinv, q)
    a_bar = a_t + _dot(tinv, a_t)
    w = _dot(l_ak, v_s)
    u0 = w + _dot(tinv, w)
    r_bar = _unstack(r_t + _dot(p_rb, a_bar))
    y0 = _unstack(_dot(p_rb, u0) + _dot(p_rk, v_s))
    m = _dot_tn(a_bar, b_h)
    s0p = _dot_tn(u0, b_h) + _dot_tn(v_s, k_h)
    return r_bar, y0, m, s0p, jnp.exp(tot), bonus


def _rwkv_scan_kernel(r_ref, k_ref, v_ref, lwf_ref, lwb_ref, af_ref, ab_ref,
                      kkf_ref, kkb_ref, kaf_ref, kab_ref, rkf_ref, rkb_ref, lnw_ref, lnb_ref,
                      s0f_ref, s0b_ref, o_ref, sf_ref, sb_ref,
                      rbar_ref, y0_ref, m_ref, s0p_ref, dec_ref, bonus_ref):
    t = r_ref.shape[1]
    nc = t // CHUNK
    dirs = ((lwf_ref, af_ref, kkf_ref, kaf_ref, rkf_ref, False), (lwb_ref, ab_ref, kkb_ref, kab_ref, rkb_ref, True))

    def prep(i, carry):
        sl = pl.ds(pl.multiple_of(i * CHUNK, CHUNK), CHUNK)
        r, k, v = r_ref[0, sl, :], k_ref[0, sl, :], v_ref[0, sl, :]
        bonus = None
        for d, (lw_ref, a_ref, kk_ref, ka_ref, rk_ref, reverse) in enumerate(dirs):
            r_bar, y0, m, s0p, dec, bon = _rwkv_chunk_terms(
                r, k, v, lw_ref[0, sl, :], a_ref[0, sl, :], kk_ref[...], ka_ref[...], rk_ref[...], reverse)
            rbar_ref[d, i] = r_bar
            y0_ref[d, i] = y0
            m_ref[d, i] = m
            s0p_ref[d, i] = s0p
            dec_ref[d, i] = jnp.broadcast_to(dec, (8, LANES))
            bonus = bon if bonus is None else bonus + bon
        bonus_ref[sl, :] = bonus
        return carry

    lax.fori_loop(0, nc, prep, 0)

    def step(d, i, s):
        y = _dot_nt(rbar_ref[d, i], s) + y0_ref[d, i]
        s = s * dec_ref[d, i][0:1] + _dot(s, m_ref[d, i]) + s0p_ref[d, i]
        return y, s

    def fwd(i, s):
        sl = pl.ds(pl.multiple_of(i * CHUNK, CHUNK), CHUNK)
        y, s = step(0, i, s)
        o_ref[0, sl, :] = y
        return s

    sf_ref[0, 0] = lax.fori_loop(0, nc, fwd, s0f_ref[0, 0])

    lane = lax.broadcasted_iota(jnp.int32, (CHUNK, LANES), 1)
    head0 = lane < RWKV_HEAD
    inv_n = 1.0 / RWKV_HEAD

    def bwd(j, s):
        i = nc - 1 - j
        sl = pl.ds(pl.multiple_of(i * CHUNK, CHUNK), CHUNK)
        y, s = step(1, i, s)
        y = o_ref[0, sl, :] + y
        mean = _seg_sum(y, head0) * inv_n
        yc = y - mean
        var = _seg_sum(yc * yc, head0) * inv_n
        o_ref[0, sl, :] = yc * lax.rsqrt(var + LNX_EPS) * lnw_ref[...] + lnb_ref[...] + bonus_ref[sl, :]
        return s

    sb_ref[0, 0] = lax.fori_loop(0, nc, bwd, s0b_ref[0, 0])


def _rwkv_scan(r, k, v, lw, a, k_k, k_a, r_k, lnx_w, lnx_b, s0f, s0b):
    bsz, t, d = r.shape
    nc = t // CHUNK
    seq = lambda b, p: (b, 0, p)
    seq_b = lambda b, p: (b, 0, RWKV_PAIRS + p)
    par = lambda b, p: (0, p)
    par_b = lambda b, p: (0, RWKV_PAIRS + p)
    st = lambda b, p: (b, p, 0, 0)
    sblk = pl.BlockSpec((1, t, LANES), seq)
    sblk_b = pl.BlockSpec((1, t, LANES), seq_b)
    pblk = pl.BlockSpec((1, LANES), par)
    pblk_b = pl.BlockSpec((1, LANES), par_b)
    stblk = pl.BlockSpec((1, 1, LANES, LANES), st)
    k_k, k_a, r_k = (p.reshape(1, 2 * d) for p in (k_k, k_a, r_k))
    return pl.pallas_call(
        _rwkv_scan_kernel,
        grid=(bsz, RWKV_PAIRS),
        in_specs=[sblk, sblk, sblk, sblk, sblk_b, sblk, sblk_b,
                  pblk, pblk_b, pblk, pblk_b, pblk, pblk_b, pblk, pblk, stblk, stblk],
        out_specs=[sblk, stblk, stblk],
        out_shape=[jax.ShapeDtypeStruct((bsz, t, d), F32),
                   jax.ShapeDtypeStruct((bsz, RWKV_PAIRS, LANES, LANES), F32),
                   jax.ShapeDtypeStruct((bsz, RWKV_PAIRS, LANES, LANES), F32)],
        scratch_shapes=[pltpu.VMEM((2, nc, CHUNK, LANES), F32), pltpu.VMEM((2, nc, CHUNK, LANES), F32),
                        pltpu.VMEM((2, nc, LANES, LANES), F32), pltpu.VMEM((2, nc, LANES, LANES), F32),
                        pltpu.VMEM((2, nc, 8, LANES), F32), pltpu.VMEM((t, LANES), F32)],
        compiler_params=_params(2),
        name="rwkv_scan",
    )(r, k, v, lw, lw, a, a, k_k, k_k, k_a, k_a, r_k, r_k,
      lnx_w.reshape(1, d), lnx_b.reshape(1, d), s0f, s0b)


def _rwkv_back_kernel(o_ref, g_ref, x_ref, mod_ref, wout_ref, nf_ref, y_ref):
    out = _dot(o_ref[...] * _silu(g_ref[...]), wout_ref[...])
    x2 = x_ref[...] + mod_ref[0][:, 2 * D_MODEL:] * out
    y_ref[...] = _rms(x2) * nf_ref[...]


def _rwkv_back(o, g, x, mod1, mod_idx, tm, w_out, norm_f):
    n, d = x.shape
    nmod = mod1.shape[0]
    row = lambda i: (i, 0)
    const = lambda i: (0, 0)
    return pl.pallas_call(
        _rwkv_back_kernel,
        grid=(n // tm,),
        in_specs=[pl.BlockSpec((tm, d), row), pl.BlockSpec((tm, d), row), pl.BlockSpec((tm, d), row),
                  pl.BlockSpec((1, 1, 3 * d), lambda i: (mod_idx(i), 0, 0)),
                  pl.BlockSpec((d, d), const), pl.BlockSpec((1, d), const)],
        out_specs=pl.BlockSpec((tm, d), row),
        out_shape=jax.ShapeDtypeStruct((n, d), F32),
        compiler_params=_params(1),
        name="rwkv_back",
    )(o, g, x, mod1.reshape(nmod, 1, 3 * d), w_out.astype(BF16), norm_f.reshape(1, d))


def _shift_ctx(x):
    p = jnp.pad(x, ((0, 0), (1, 1), (0, 0)))
    return 0.5 * (p[:, :-2] + p[:, 2:])


def _shift_grid(x):
    b, t, d = x.shape
    g = x.reshape(b, t // GRID_W, GRID_W, d)
    p = jnp.pad(g, ((0, 0), (1, 1), (1, 1), (0, 0)))
    nb = 0.25 * (p[:, :-2, 1:-1] + p[:, 2:, 1:-1] + p[:, 1:-1, :-2] + p[:, 1:-1, 2:])
    return nb.reshape(b, t, d)


def _pair_blockdiag(s):
    b = s.shape[0]
    s = s.reshape(b, RWKV_PAIRS, 2, RWKV_HEAD, RWKV_HEAD)
    z = jnp.zeros_like(s[:, :, 0])
    top = jnp.concatenate([s[:, :, 0], z], axis=-1)
    bot = jnp.concatenate([z, s[:, :, 1]], axis=-1)
    return jnp.concatenate([top, bot], axis=-2)


def _pair_diag_blocks(s):
    b = s.shape[0]
    h = RWKV_HEAD
    return jnp.stack([s[:, :, :h, :h], s[:, :, h:, h:]], axis=2).reshape(b, RWKV_HEADS, h, h)


def kernel(x_prompt, x_sample, state_gla, state_rwkv, c, c_ctx, w_mod, b_mod, norm_w, gla_w_in, gla_w_a1,
           gla_w_a2, gla_b_a, gla_norm, gla_w_out, rwkv_mu, rwkv_w_rkvg, rwkv_w0, rwkv_w1, rwkv_w2, rwkv_a0,
           rwkv_a1, rwkv_a2, rwkv_k_k, rwkv_k_a, rwkv_r_k, rwkv_lnx_w, rwkv_lnx_b, rwkv_w_out, norm_f):
    d = D_MODEL
    bp, tp, _ = x_prompt.shape
    bs, ts, _ = x_sample.shape
    np_, ns = bp * tp, bs * ts
    tm = min(256, tp, ts)
    assert tp % tm == 0 and ts % tm == 0 and tp % CHUNK == 0 and ts % CHUNK == 0 and ts % GRID_W == 0
    n_ptiles, tiles_per_seq = np_ // tm, ts // tm

    def mod_idx(i):
        return jnp.where(i < n_ptiles, 0, 1 + (i - n_ptiles) // tiles_per_seq)

    nrows = -(-(1 + bs) // 8) * 8
    cond = jnp.zeros((nrows, d), F32).at[0].set(c_ctx).at[1:1 + bs].set(c)
    cond = cond * (1.0 / (1.0 + jnp.exp(-cond)))
    mod0 = _matmul_bias(cond, w_mod[0], b_mod[0].reshape(1, 3 * d))
    mod1 = _matmul_bias(cond, w_mod[1], b_mod[1].reshape(1, 3 * d))

    x = jnp.concatenate([x_prompt.reshape(np_, d), x_sample.reshape(ns, d)], axis=0)

    q, k, v, g, lg = _gla_front(x, mod0, mod_idx, tm, norm_w[0], gla_w_in[0], gla_w_a1[0], gla_w_a2[0], gla_b_a[0])
    split = lambda a, w: (a[:np_].reshape(bp, tp, w), a[np_:].reshape(bs, ts, w))
    (qp, qs), (kp, ks), (vp, vs), (lgp, lgs) = split(q, GLA_DK), split(k, GLA_DK), split(v, GLA_DV), split(lg, 2 * GLA_DK)
    zero_gla = jnp.zeros((bp, GLA_HEADS, GLA_DVH, GLA_DKH), F32)
    op, gsf_p, gsb_p = _gla_scan(qp, kp, vp, lgp, zero_gla, zero_gla)
    sg = jnp.swapaxes(state_gla[:, 0], -1, -2)
    os_, _, _ = _gla_scan(qs, ks, vs, lgs, sg[:, 0], sg[:, 1])
    new_state_gla = jnp.swapaxes(jnp.stack([gsf_p, gsb_p], axis=1), -1, -2)[:, None]
    o = jnp.concatenate([op.reshape(np_, d), os_.reshape(ns, d)], axis=0)
    x1, h1 = _gla_back(o, g, x, mod0, mod1, mod_idx, tm, gla_norm[0], gla_w_out[0], norm_w[1])

    hs = jnp.concatenate([_shift_ctx(h1[:np_].reshape(bp, tp, d)).reshape(np_, d),
                          _shift_grid(h1[np_:].reshape(bs, ts, d)).reshape(ns, d)], axis=0)
    r, k, v, g, lw, a = _rwkv_front(h1, hs, tm, rwkv_mu[0], rwkv_w_rkvg[0], rwkv_w0[0], rwkv_w1[0], rwkv_w2[0],
                                    rwkv_a0[0], rwkv_a1[0], rwkv_a2[0])
    (rp, rs), (kp, ks), (vp, vs) = split(r, d), split(k, d), split(v, d)
    (lwp, lws), (ap, as_) = split(lw, 2 * d), split(a, 2 * d)
    scan = functools.partial(_rwkv_scan, k_k=rwkv_k_k[0], k_a=rwkv_k_a[0], r_k=rwkv_r_k[0],
                             lnx_w=rwkv_lnx_w[0], lnx_b=rwkv_lnx_b[0])
    zero_rwkv = jnp.zeros((bp, RWKV_PAIRS, LANES, LANES), F32)
    yp, rsf_p, rsb_p = scan(rp, kp, vp, lwp, ap, s0f=zero_rwkv, s0b=zero_rwkv)
    ys, _, _ = scan(rs, ks, vs, lws, as_, s0f=_pair_blockdiag(state_rwkv[:, 0, 0]), s0b=_pair_blockdiag(state_rwkv[:, 0, 1]))
    new_state_rwkv = jnp.stack([_pair_diag_blocks(rsf_p), _pair_diag_blocks(rsb_p)], axis=1)[:, None]
    o = jnp.concatenate([yp.reshape(np_, d), ys.reshape(ns, d)], axis=0)
    y = _rwkv_back(o, g, x1, mod1, mod_idx, tm, rwkv_w_out[0], norm_f)

    return (y[:np_].reshape(bp, tp, d), y[np_:].reshape(bs, ts, d), new_state_gla, new_state_rwkv)
```

```python
import functools

import jax
import jax.numpy as jnp
from jax import lax
from jax.experimental import pallas as pl
from jax.experimental.pallas import tpu as pltpu

F32 = jnp.float32
BF16 = jnp.bfloat16

D_MODEL = 1024
EPS = 1e-6
GRID_W = 64
GLA_HEADS = 4
GLA_DK = 512
GLA_DV = 1024
GLA_DKH = 128
GLA_DVH = 256
GLA_GATE_RANK = 16
GLA_GATE_NORM = 16.0
RWKV_HEAD = 64
RWKV_HEADS = 16
RWKV_PAIRS = 8
RWKV_RANK = 64
LNX_EPS = 64e-5
CHUNK = 64
PREP_GROUP = 4
LANES = 128
VMEM_LIMIT = 56 * 1024 * 1024


def _dot(a, b):
    return jnp.dot(a.astype(BF16), b.astype(BF16), preferred_element_type=F32)


def _dot_nt(a, b):
    return lax.dot_general(a.astype(BF16), b.astype(BF16), (((1,), (1,)), ((), ())),
                           preferred_element_type=F32)


def _dot_tn(a, b):
    return lax.dot_general(a.astype(BF16), b.astype(BF16), (((0,), (0,)), ((), ())),
                           preferred_element_type=F32)


def _tri_dot(tri, x):
    hi = x.astype(BF16)
    r1 = x - hi.astype(F32)
    mid = r1.astype(BF16)
    lo = (r1 - mid.astype(F32)).astype(BF16)
    t = tri.astype(BF16)
    d = lambda p: jnp.dot(t, p, preferred_element_type=F32)
    return d(hi) + d(mid) + d(lo)


def _log_sigmoid(z):
    return jnp.minimum(z, 0.0) - jnp.log(1.0 + jnp.exp(-jnp.abs(z)))


def _sigmoid(z):
    return 1.0 / (1.0 + jnp.exp(-z))


def _silu(z):
    return z * _sigmoid(z)


def _rms(x):
    return x * lax.rsqrt(jnp.mean(x * x, axis=-1, keepdims=True) + EPS)


def _params(n_grid_dims):
    return pltpu.CompilerParams(dimension_semantics=("arbitrary",) * n_grid_dims,
                                vmem_limit_bytes=VMEM_LIMIT)


def _mm_kernel(x_ref, w_ref, b_ref, o_ref):
    o_ref[...] = _dot(x_ref[...], w_ref[...]) + b_ref[...]


def _matmul_bias(x, w, b):
    m, k = x.shape
    n = w.shape[1]
    return pl.pallas_call(
        _mm_kernel,
        grid=(1,),
        in_specs=[pl.BlockSpec((m, k), lambda i: (0, 0)),
                  pl.BlockSpec((k, n), lambda i: (0, 0)),
                  pl.BlockSpec((1, n), lambda i: (0, 0))],
        out_specs=pl.BlockSpec((m, n), lambda i: (0, 0)),
        out_shape=jax.ShapeDtypeStruct((m, n), F32),
        compiler_params=_params(1),
        name="mod_matmul",
    )(x, w.astype(BF16), b)


def _gla_front_kernel(x_ref, mod_ref, nw_ref, win_ref, wa1_ref, wa2_ref, ba_ref,
                      q_ref, k_ref, v_ref, g_ref, lg_ref):
    d = D_MODEL
    mod = mod_ref[0]
    shift, scale = mod[:, :d], mod[:, d:2 * d]
    h = _rms(x_ref[...]) * nw_ref[...] * (1.0 + scale) + shift
    hb = h.astype(BF16)
    q_ref[...] = jnp.dot(hb, win_ref[:, 0:GLA_DK], preferred_element_type=F32) * (GLA_DKH ** -0.5)
    k_ref[...] = jnp.dot(hb, win_ref[:, GLA_DK:2 * GLA_DK], preferred_element_type=F32)
    v_ref[...] = jnp.dot(hb, win_ref[:, 2 * GLA_DK:2 * GLA_DK + GLA_DV], preferred_element_type=F32)
    g_ref[...] = jnp.dot(hb, win_ref[:, 2 * GLA_DK + GLA_DV:], preferred_element_type=F32)
    t = jnp.dot(hb, wa1_ref[...], preferred_element_type=F32)
    z = _dot(t, wa2_ref[...]) + ba_ref[...]
    lg_ref[...] = _log_sigmoid(z) * (1.0 / GLA_GATE_NORM)


def _gla_front(x, mod, mod_idx, tm, norm_w, w_in, w_a1, w_a2, b_a):
    n, d = x.shape
    nmod = mod.shape[0]
    wa1 = jnp.zeros((d, LANES), F32).at[:, :GLA_GATE_RANK].set(w_a1[0]).at[:, GLA_GATE_RANK:2 * GLA_GATE_RANK].set(w_a1[1])
    wa2 = jnp.zeros((LANES, 2 * GLA_DK), F32).at[:GLA_GATE_RANK, :GLA_DK].set(w_a2[0])
    wa2 = wa2.at[GLA_GATE_RANK:2 * GLA_GATE_RANK, GLA_DK:].set(w_a2[1])
    ba = b_a.reshape(1, 2 * GLA_DK)
    row = lambda i: (i, 0)
    const = lambda i: (0, 0)
    n_in = w_in.shape[1]
    outs = pl.pallas_call(
        _gla_front_kernel,
        grid=(n // tm,),
        in_specs=[pl.BlockSpec((tm, d), row),
                  pl.BlockSpec((1, 1, 3 * d), lambda i: (mod_idx(i), 0, 0)),
                  pl.BlockSpec((1, d), const),
                  pl.BlockSpec((d, n_in), const),
                  pl.BlockSpec((d, LANES), const),
                  pl.BlockSpec((LANES, 2 * GLA_DK), const),
                  pl.BlockSpec((1, 2 * GLA_DK), const)],
        out_specs=[pl.BlockSpec((tm, GLA_DK), row), pl.BlockSpec((tm, GLA_DK), row),
                   pl.BlockSpec((tm, GLA_DV), row), pl.BlockSpec((tm, GLA_DV), row),
                   pl.BlockSpec((tm, 2 * GLA_DK), row)],
        out_shape=[jax.ShapeDtypeStruct((n, GLA_DK), F32), jax.ShapeDtypeStruct((n, GLA_DK), F32),
                   jax.ShapeDtypeStruct((n, GLA_DV), F32), jax.ShapeDtypeStruct((n, GLA_DV), F32),
                   jax.ShapeDtypeStruct((n, 2 * GLA_DK), F32)],
        compiler_params=_params(1),
        name="gla_front",
    )(x, mod.reshape(nmod, 1, 3 * d), norm_w.reshape(1, d), w_in.astype(BF16), wa1.astype(BF16),
      wa2.astype(BF16), ba)
    return outs


def _gla_chunks(insts):
    c = insts[0][0].shape[0]
    row = lax.broadcasted_iota(jnp.int32, (c, c), 0)
    col = lax.broadcasted_iota(jnp.int32, (c, c), 1)
    keep = [(col >= row) if x[5] else (col <= row) for x in insts]
    b = [_tri_dot(kp.astype(F32), x[3]) for kp, x in zip(keep, insts)]
    b_end = [bb[0:1] if x[5] else bb[c - 1:c] for bb, x in zip(b, insts)]
    qb = [x[0] * jnp.exp(bb) for x, bb in zip(insts, b)]
    kb = [x[1] * jnp.exp(-bb) for x, bb in zip(insts, b)]
    kd = [x[1] * jnp.exp(be - bb) for x, bb, be in zip(insts, b, b_end)]
    scores = [jnp.where(kp, _dot_nt(x, y), 0.0) for kp, x, y in zip(keep, qb, kb)]
    o_inter = [_dot_nt(x, ins[4]) for x, ins in zip(qb, insts)]
    upd = [_dot_tn(ins[2], x) for ins, x in zip(insts, kd)]
    o = [_dot(sc, ins[2]) + oi for sc, ins, oi in zip(scores, insts, o_inter)]
    st = [ins[4] * jnp.exp(be) + u for ins, be, u in zip(insts, b_end, upd)]
    return list(zip(o, st))


def _gla_scan_kernel(q_ref, k_ref, v_ref, lf_ref, lb_ref, s0f_ref, s0b_ref, o_ref, sf_ref, sb_ref):
    t = q_ref.shape[1]
    nc = t // CHUNK
    half = nc // 2

    def both(accumulate):
        def body(i, carry):
            sls = [pl.ds(pl.multiple_of(ci * CHUNK, CHUNK), CHUNK) for ci in (i, nc - 1 - i)]
            insts = [(q_ref[0, sl, :], k_ref[0, sl, :], v_ref[0, sl, :], g_ref[0, sl, :], st, reverse)
                     for sl, g_ref, st, reverse in zip(sls, (lf_ref, lb_ref), carry, (False, True))]
            res = _gla_chunks(insts)
            for sl, (o, _) in zip(sls, res):
                if accumulate:
                    o_ref[0, sl, :] += o
                else:
                    o_ref[0, sl, :] = o
            return tuple(st for _, st in res)
        return body

    carry = lax.fori_loop(0, half, both(False), (s0f_ref[0, 0], s0b_ref[0, 0]))
    sf, sb = lax.fori_loop(half, nc, both(True), carry)
    sf_ref[0, 0] = sf
    sb_ref[0, 0] = sb


def _gla_scan(q, k, v, lg, s0f, s0b):
    bsz, t, _ = q.shape
    hd = lambda b, h: (b, 0, h)
    st = lambda b, h: (b, h, 0, 0)
    return pl.pallas_call(
        _gla_scan_kernel,
        grid=(bsz, GLA_HEADS),
        in_specs=[pl.BlockSpec((1, t, GLA_DKH), hd), pl.BlockSpec((1, t, GLA_DKH), hd),
                  pl.BlockSpec((1, t, GLA_DVH), hd),
                  pl.BlockSpec((1, t, GLA_DKH), hd),
                  pl.BlockSpec((1, t, GLA_DKH), lambda b, h: (b, 0, GLA_HEADS + h)),
                  pl.BlockSpec((1, 1, GLA_DVH, GLA_DKH), st), pl.BlockSpec((1, 1, GLA_DVH, GLA_DKH), st)],
        out_specs=[pl.BlockSpec((1, t, GLA_DVH), hd),
                   pl.BlockSpec((1, 1, GLA_DVH, GLA_DKH), st), pl.BlockSpec((1, 1, GLA_DVH, GLA_DKH), st)],
        out_shape=[jax.ShapeDtypeStruct((bsz, t, GLA_DV), F32),
                   jax.ShapeDtypeStruct((bsz, GLA_HEADS, GLA_DVH, GLA_DKH), F32),
                   jax.ShapeDtypeStruct((bsz, GLA_HEADS, GLA_DVH, GLA_DKH), F32)],
        compiler_params=_params(2),
        name="gla_scan",
    )(q, k, v, lg, lg, s0f, s0b)


def _gla_back_kernel(o_ref, g_ref, x_ref, mod0_ref, mod1_ref, gn_ref, wout_ref, nw1_ref, x1_ref, h1_ref):
    d = D_MODEL
    o = o_ref[...]
    gn = gn_ref[...]
    parts = [_rms(o[:, h * GLA_DVH:(h + 1) * GLA_DVH]) * gn for h in range(GLA_HEADS)]
    on = jnp.concatenate(parts, axis=-1) * _silu(g_ref[...])
    out = _dot(on, wout_ref[...])
    gate = mod0_ref[0][:, 2 * d:]
    x1 = x_ref[...] + gate * out
    x1_ref[...] = x1
    mod1 = mod1_ref[0]
    h1_ref[...] = _rms(x1) * nw1_ref[...] * (1.0 + mod1[:, d:2 * d]) + mod1[:, :d]


def _gla_back(o, g, x, mod0, mod1, mod_idx, tm, gla_norm, w_out, norm_w1):
    n, d = x.shape
    nmod = mod0.shape[0]
    row = lambda i: (i, 0)
    const = lambda i: (0, 0)
    modspec = pl.BlockSpec((1, 1, 3 * d), lambda i: (mod_idx(i), 0, 0))
    return pl.pallas_call(
        _gla_back_kernel,
        grid=(n // tm,),
        in_specs=[pl.BlockSpec((tm, d), row), pl.BlockSpec((tm, d), row), pl.BlockSpec((tm, d), row),
                  modspec, modspec,
                  pl.BlockSpec((1, GLA_DVH), const), pl.BlockSpec((d, d), const), pl.BlockSpec((1, d), const)],
        out_specs=[pl.BlockSpec((tm, d), row), pl.BlockSpec((tm, d), row)],
        out_shape=[jax.ShapeDtypeStruct((n, d), F32), jax.ShapeDtypeStruct((n, d), F32)],
        compiler_params=_params(1),
        name="gla_back",
    )(o, g, x, mod0.reshape(nmod, 1, 3 * d), mod1.reshape(nmod, 1, 3 * d), gla_norm.reshape(1, GLA_DVH),
      w_out.astype(BF16), norm_w1.reshape(1, d))


def _rwkv_front_kernel(h_ref, hs_ref, mu_ref, wrkvg_ref, w1_ref, w2_ref, w0_ref, a1_ref, a2_ref, a0_ref,
                       r_ref, k_ref, v_ref, g_ref, lw_ref, a_ref):
    h = h_ref[...]
    dh = hs_ref[...] - h
    mix = lambda p: (h + dh * mu_ref[p:p + 1, :]).astype(BF16)
    r_ref[...] = jnp.dot(mix(0), wrkvg_ref[0], preferred_element_type=F32)
    k_ref[...] = jnp.dot(mix(1), wrkvg_ref[1], preferred_element_type=F32)
    v_ref[...] = jnp.dot(mix(2), wrkvg_ref[2], preferred_element_type=F32)
    g_ref[...] = jnp.dot(mix(3), wrkvg_ref[3], preferred_element_type=F32)
    tw = jnp.tanh(jnp.dot(mix(4), w1_ref[...], preferred_element_type=F32))
    w_log = _log_sigmoid(w0_ref[...] + _dot(tw, w2_ref[...])) - 0.5
    lw_ref[...] = -jnp.exp(w_log)
    ta = jnp.dot(mix(5), a1_ref[...], preferred_element_type=F32)
    a_ref[...] = _sigmoid(a0_ref[...] + _dot(ta, a2_ref[...]))


def _rwkv_front(h, hs, tm, mu, w_rkvg, w0, w1, w2, a0, a1, a2):
    n, d = h.shape
    rk = RWKV_RANK
    cat1 = lambda w: jnp.concatenate([w[0], w[1]], axis=1)
    bd2 = lambda w: jnp.zeros((2 * rk, 2 * d), F32).at[:rk, :d].set(w[0]).at[rk:, d:].set(w[1])
    row = lambda i: (i, 0)
    const = lambda i: (0, 0)
    return pl.pallas_call(
        _rwkv_front_kernel,
        grid=(n // tm,),
        in_specs=[pl.BlockSpec((tm, d), row), pl.BlockSpec((tm, d), row),
                  pl.BlockSpec((8, d), const),
                  pl.BlockSpec((4, d, d), lambda i: (0, 0, 0)),
                  pl.BlockSpec((d, 2 * rk), const), pl.BlockSpec((2 * rk, 2 * d), const), pl.BlockSpec((1, 2 * d), const),
                  pl.BlockSpec((d, 2 * rk), const), pl.BlockSpec((2 * rk, 2 * d), const), pl.BlockSpec((1, 2 * d), const)],
        out_specs=[pl.BlockSpec((tm, d), row)] * 4 + [pl.BlockSpec((tm, 2 * d), row)] * 2,
        out_shape=[jax.ShapeDtypeStruct((n, d), F32)] * 4 + [jax.ShapeDtypeStruct((n, 2 * d), F32)] * 2,
        compiler_params=_params(1),
        name="rwkv_front",
    )(h, hs, jnp.zeros((8, d), F32).at[:6].set(mu), w_rkvg.astype(BF16),
      cat1(w1).astype(BF16), bd2(w2).astype(BF16), w0.reshape(1, 2 * d),
      cat1(a1).astype(BF16), bd2(a2).astype(BF16), a0.reshape(1, 2 * d))


def _seg_sum(x, head0):
    s0 = jnp.sum(jnp.where(head0, x, 0.0), axis=-1, keepdims=True)
    s1 = jnp.sum(jnp.where(head0, 0.0, x), axis=-1, keepdims=True)
    return jnp.where(head0, s0, s1)


def _stack(x, head0):
    return jnp.concatenate([jnp.where(head0, x, 0.0), jnp.where(head0, 0.0, x)], axis=0)


def _unstack(xs):
    c = xs.shape[0] // 2
    return xs[:c] + xs[c:]


def _rwkv_chunk_terms(insts):
    n = len(insts)
    c = insts[0][0].shape[0]
    lane = lax.broadcasted_iota(jnp.int32, (c, LANES), 1)
    head0 = lane < RWKV_HEAD
    row = lax.broadcasted_iota(jnp.int32, (c, c), 0)
    col = lax.broadcasted_iota(jnp.int32, (c, c), 1)
    srow = lax.broadcasted_iota(jnp.int32, (2 * c, 2 * c), 0)
    scol = lax.broadcasted_iota(jnp.int32, (2 * c, 2 * c), 1)
    same = (srow >= c) == (scol >= c)
    ti, si = srow & (c - 1), scol & (c - 1)
    each = lambda f, *ls: [f(*xs) for xs in zip(*ls)]
    rev = [x[8] for x in insts]

    cum = [_tri_dot(((col >= row) if x[8] else (col <= row)).astype(F32), x[3]) for x in insts]
    tot = [cm[0:1] if rv else cm[c - 1:c] for cm, rv in zip(cum, rev)]

    a_t, r_t, b_t, k_t, b_h, k_h, v_s, bonus = [], [], [], [], [], [], [], []
    for (r, k, v, lw, a, k_k, k_a, r_k, _), cm, tt in zip(insts, cum, tot):
        kk = k * k_k
        kk = kk / jnp.maximum(jnp.sqrt(_seg_sum(kk * kk, head0)), 1e-12)
        kd = k * (1.0 + (a - 1.0) * k_a)
        bv = kk * a
        bonus.append(_seg_sum(r * kd * r_k, head0) * v)
        e_neg = jnp.exp(-cm)
        e_tail = jnp.exp(tt - cm)
        a_t.append(_stack(-kk * jnp.exp(cm - lw), head0))
        r_t.append(_stack(r * jnp.exp(cm), head0))
        b_t.append(_stack(bv * e_neg, head0))
        k_t.append(_stack(kd * e_neg, head0))
        b_h.append(_stack(bv * e_tail, head0))
        k_h.append(_stack(kd * e_tail, head0))
        v_s.append(_stack(v, head0))

    strict = [same & ((si > ti) if rv else (si < ti)) for rv in rev]
    incl = [same & ((si >= ti) if rv else (si <= ti)) for rv in rev]
    n_ab = each(lambda m_, x, y: jnp.where(m_, _dot_nt(x, y), 0.0), strict, a_t, b_t)
    l_ak = each(lambda m_, x, y: jnp.where(m_, _dot_nt(x, y), 0.0), strict, a_t, k_t)
    p_rb = each(lambda m_, x, y: jnp.where(m_, _dot_nt(x, y), 0.0), incl, r_t, b_t)
    p_rk = each(lambda m_, x, y: jnp.where(m_, _dot_nt(x, y), 0.0), incl, r_t, k_t)

    tinv, q = n_ab, n_ab
    steps = max(1, (c - 1).bit_length()) - 1
    for _ in range(steps):
        q = each(lambda x: _dot(x, x), q)
        tinv = each(lambda t_, x: t_ + x + _dot(t_, x), tinv, q)
    a_bar = each(lambda t_, x: x + _dot(t_, x), tinv, a_t)
    w = each(_dot, l_ak, v_s)
    u0 = each(lambda t_, x: x + _dot(t_, x), tinv, w)
    r_bar = each(lambda x, p, y: _unstack(x + _dot(p, y)), r_t, p_rb, a_bar)
    y0 = each(lambda p, x, p2, y: _unstack(_dot(p, x) + _dot(p2, y)), p_rb, u0, p_rk, v_s)
    m = each(_dot_tn, a_bar, b_h)
    s0p = each(lambda x, y, x2, y2: _dot_tn(x, y) + _dot_tn(x2, y2), u0, b_h, v_s, k_h)
    return [(r_bar[i], y0[i], m[i], s0p[i], jnp.exp(tot[i]), bonus[i]) for i in range(n)]


def _rwkv_scan_kernel(r_ref, k_ref, v_ref, lwf_ref, lwb_ref, af_ref, ab_ref,
                      kkf_ref, kkb_ref, kaf_ref, kab_ref, rkf_ref, rkb_ref, lnw_ref, lnb_ref,
                      s0f_ref, s0b_ref, o_ref, sf_ref, sb_ref,
                      rbar_ref, y0_ref, m_ref, s0p_ref, dec_ref, bonus_ref):
    t = r_ref.shape[1]
    nc = t // CHUNK
    dirs = ((lwf_ref, af_ref, kkf_ref, kaf_ref, rkf_ref, False), (lwb_ref, ab_ref, kkb_ref, kab_ref, rkb_ref, True))

    group = min(nc, PREP_GROUP)

    def prep(gi, carry):
        insts, where = [], []
        for j in range(group):
            ci = gi * group + j
            sl = pl.ds(pl.multiple_of(ci * CHUNK, CHUNK), CHUNK)
            r, k, v = r_ref[0, sl, :], k_ref[0, sl, :], v_ref[0, sl, :]
            for d, (lw_ref, a_ref, kk_ref, ka_ref, rk_ref, reverse) in enumerate(dirs):
                insts.append((r, k, v, lw_ref[0, sl, :], a_ref[0, sl, :], kk_ref[...], ka_ref[...], rk_ref[...], reverse))
                where.append((d, ci, sl))
        terms = _rwkv_chunk_terms(insts)
        for (d, ci, sl), (r_bar, y0, m, s0p, dec, bon) in zip(where, terms):
            rbar_ref[d, ci] = r_bar
            y0_ref[d, ci] = y0
            m_ref[d, ci] = m
            s0p_ref[d, ci] = s0p
            dec_ref[d, ci] = jnp.broadcast_to(dec, (8, LANES))
        for j in range(group):
            bonus_ref[where[2 * j][2], :] = terms[2 * j][5] + terms[2 * j + 1][5]
        return carry

    lax.fori_loop(0, nc // group, prep, 0)

    lane = lax.broadcasted_iota(jnp.int32, (CHUNK, LANES), 1)
    head0 = lane < RWKV_HEAD
    inv_n = 1.0 / RWKV_HEAD
    half = nc // 2

    def both(finish):
        def body(i, carry):
            cis = (i, nc - 1 - i)
            sb16 = [s.astype(BF16) for s in carry]
            sm = [_dot(sb16[d], m_ref[d, cis[d]]) for d in range(2)]
            ys = [_dot_nt(rbar_ref[d, cis[d]], sb16[d]) + y0_ref[d, cis[d]] for d in range(2)]
            outs = tuple(carry[d] * dec_ref[d, cis[d]][0:1] + sm[d] + s0p_ref[d, cis[d]] for d in range(2))
            for d in range(2):
                sl = pl.ds(pl.multiple_of(cis[d] * CHUNK, CHUNK), CHUNK)
                y = ys[d]
                if finish:
                    y = o_ref[0, sl, :] + y
                    yc = y - _seg_sum(y, head0) * inv_n
                    var = _seg_sum(yc * yc, head0) * inv_n
                    y = yc * lax.rsqrt(var + LNX_EPS) * lnw_ref[...] + lnb_ref[...] + bonus_ref[sl, :]
                o_ref[0, sl, :] = y
            return outs
        return body

    carry = lax.fori_loop(0, half, both(False), (s0f_ref[0, 0], s0b_ref[0, 0]))
    sf, sb = lax.fori_loop(half, nc, both(True), carry)
    sf_ref[0, 0] = sf
    sb_ref[0, 0] = sb


def _rwkv_scan(r, k, v, lw, a, k_k, k_a, r_k, lnx_w, lnx_b, s0f, s0b):
    bsz, t, d = r.shape
    nc = t // CHUNK
    seq = lambda b, p: (b, 0, p)
    seq_b = lambda b, p: (b, 0, RWKV_PAIRS + p)
    par = lambda b, p: (0, p)
    par_b = lambda b, p: (0, RWKV_PAIRS + p)
    st = lambda b, p: (b, p, 0, 0)
    sblk = pl.BlockSpec((1, t, LANES), seq)
    sblk_b = pl.BlockSpec((1, t, LANES), seq_b)
    pblk = pl.BlockSpec((1, LANES), par)
    pblk_b = pl.BlockSpec((1, LANES), par_b)
    stblk = pl.BlockSpec((1, 1, LANES, LANES), st)
    k_k, k_a, r_k = (p.reshape(1, 2 * d) for p in (k_k, k_a, r_k))
    return pl.pallas_call(
        _rwkv_scan_kernel,
        grid=(bsz, RWKV_PAIRS),
        in_specs=[sblk, sblk, sblk, sblk, sblk_b, sblk, sblk_b,
                  pblk, pblk_b, pblk, pblk_b, pblk, pblk_b, pblk, pblk, stblk, stblk],
        out_specs=[sblk, stblk, stblk],
        out_shape=[jax.ShapeDtypeStruct((bsz, t, d), F32),
                   jax.ShapeDtypeStruct((bsz, RWKV_PAIRS, LANES, LANES), F32),
                   jax.ShapeDtypeStruct((bsz, RWKV_PAIRS, LANES, LANES), F32)],
        scratch_shapes=[pltpu.VMEM((2, nc, CHUNK, LANES), F32), pltpu.VMEM((2, nc, CHUNK, LANES), F32),
                        pltpu.VMEM((2, nc, LANES, LANES), F32), pltpu.VMEM((2, nc, LANES, LANES), F32),
                        pltpu.VMEM((2, nc, 8, LANES), F32), pltpu.VMEM((t, LANES), F32)],
        compiler_params=_params(2),
        name="rwkv_scan",
    )(r, k, v, lw, lw, a, a, k_k, k_k, k_a, k_a, r_k, r_k,
      lnx_w.reshape(1, d), lnx_b.reshape(1, d), s0f, s0b)


def _rwkv_back_kernel(o_ref, g_ref, x_ref, mod_ref, wout_ref, nf_ref, y_ref):
    out = _dot(o_ref[...] * _silu(g_ref[...]), wout_ref[...])
    x2 = x_ref[...] + mod_ref[0][:, 2 * D_MODEL:] * out
    y_ref[...] = _rms(x2) * nf_ref[...]


def _rwkv_back(o, g, x, mod1, mod_idx, tm, w_out, norm_f):
    n, d = x.shape
    nmod = mod1.shape[0]
    row = lambda i: (i, 0)
    const = lambda i: (0, 0)
    return pl.pallas_call(
        _rwkv_back_kernel,
        grid=(n // tm,),
        in_specs=[pl.BlockSpec((tm, d), row), pl.BlockSpec((tm, d), row), pl.BlockSpec((tm, d), row),
                  pl.BlockSpec((1, 1, 3 * d), lambda i: (mod_idx(i), 0, 0)),
                  pl.BlockSpec((d, d), const), pl.BlockSpec((1, d), const)],
        out_specs=pl.BlockSpec((tm, d), row),
        out_shape=jax.ShapeDtypeStruct((n, d), F32),
        compiler_params=_params(1),
        name="rwkv_back",
    )(o, g, x, mod1.reshape(nmod, 1, 3 * d), w_out.astype(BF16), norm_f.reshape(1, d))


def _shift_ctx(x):
    p = jnp.pad(x, ((0, 0), (1, 1), (0, 0)))
    return 0.5 * (p[:, :-2] + p[:, 2:])


def _shift_grid(x):
    b, t, d = x.shape
    g = x.reshape(b, t // GRID_W, GRID_W, d)
    p = jnp.pad(g, ((0, 0), (1, 1), (1, 1), (0, 0)))
    nb = 0.25 * (p[:, :-2, 1:-1] + p[:, 2:, 1:-1] + p[:, 1:-1, :-2] + p[:, 1:-1, 2:])
    return nb.reshape(b, t, d)


def _pair_blockdiag(s):
    b = s.shape[0]
    s = s.reshape(b, RWKV_PAIRS, 2, RWKV_HEAD, RWKV_HEAD)
    z = jnp.zeros_like(s[:, :, 0])
    top = jnp.concatenate([s[:, :, 0], z], axis=-1)
    bot = jnp.concatenate([z, s[:, :, 1]], axis=-1)
    return jnp.concatenate([top, bot], axis=-2)


def _pair_diag_blocks(s):
    b = s.shape[0]
    h = RWKV_HEAD
    return jnp.stack([s[:, :, :h, :h], s[:, :, h:, h:]], axis=2).reshape(b, RWKV_HEADS, h, h)


def kernel(x_prompt, x_sample, state_gla, state_rwkv, c, c_ctx, w_mod, b_mod, norm_w, gla_w_in, gla_w_a1,
           gla_w_a2, gla_b_a, gla_norm, gla_w_out, rwkv_mu, rwkv_w_rkvg, rwkv_w0, rwkv_w1, rwkv_w2, rwkv_a0,
           rwkv_a1, rwkv_a2, rwkv_k_k, rwkv_k_a, rwkv_r_k, rwkv_lnx_w, rwkv_lnx_b, rwkv_w_out, norm_f):
    d = D_MODEL
    bp, tp, _ = x_prompt.shape
    bs, ts, _ = x_sample.shape
    np_, ns = bp * tp, bs * ts
    tm = min(256, tp, ts)
    assert tp % tm == 0 and ts % tm == 0 and tp % CHUNK == 0 and ts % CHUNK == 0 and ts % GRID_W == 0
    n_ptiles, tiles_per_seq = np_ // tm, ts // tm

    def mod_idx(i):
        return jnp.where(i < n_ptiles, 0, 1 + (i - n_ptiles) // tiles_per_seq)

    nrows = -(-(1 + bs) // 8) * 8
    cond = jnp.zeros((nrows, d), F32).at[0].set(c_ctx).at[1:1 + bs].set(c)
    cond = cond * (1.0 / (1.0 + jnp.exp(-cond)))
    mod0 = _matmul_bias(cond, w_mod[0], b_mod[0].reshape(1, 3 * d))
    mod1 = _matmul_bias(cond, w_mod[1], b_mod[1].reshape(1, 3 * d))

    x = jnp.concatenate([x_prompt.reshape(np_, d), x_sample.reshape(ns, d)], axis=0)

    q, k, v, g, lg = _gla_front(x, mod0, mod_idx, tm, norm_w[0], gla_w_in[0], gla_w_a1[0], gla_w_a2[0], gla_b_a[0])
    split = lambda a, w: (a[:np_].reshape(bp, tp, w), a[np_:].reshape(bs, ts, w))
    (qp, qs), (kp, ks), (vp, vs), (lgp, lgs) = split(q, GLA_DK), split(k, GLA_DK), split(v, GLA_DV), split(lg, 2 * GLA_DK)
    zero_gla = jnp.zeros((bp, GLA_HEADS, GLA_DVH, GLA_DKH), F32)
    op, gsf_p, gsb_p = _gla_scan(qp, kp, vp, lgp, zero_gla, zero_gla)
    sg = jnp.swapaxes(state_gla[:, 0], -1, -2)
    os_, _, _ = _gla_scan(qs, ks, vs, lgs, sg[:, 0], sg[:, 1])
    new_state_gla = jnp.swapaxes(jnp.stack([gsf_p, gsb_p], axis=1), -1, -2)[:, None]
    o = jnp.concatenate([op.reshape(np_, d), os_.reshape(ns, d)], axis=0)
    x1, h1 = _gla_back(o, g, x, mod0, mod1, mod_idx, tm, gla_norm[0], gla_w_out[0], norm_w[1])

    hs = jnp.concatenate([_shift_ctx(h1[:np_].reshape(bp, tp, d)).reshape(np_, d),
                          _shift_grid(h1[np_:].reshape(bs, ts, d)).reshape(ns, d)], axis=0)
    r, k, v, g, lw, a = _rwkv_front(h1, hs, tm, rwkv_mu[0], rwkv_w_rkvg[0], rwkv_w0[0], rwkv_w1[0], rwkv_w2[0],
                                    rwkv_a0[0], rwkv_a1[0], rwkv_a2[0])
    (rp, rs), (kp, ks), (vp, vs) = split(r, d), split(k, d), split(v, d)
    (lwp, lws), (ap, as_) = split(lw, 2 * d), split(a, 2 * d)
    scan = functools.partial(_rwkv_scan, k_k=rwkv_k_k[0], k_a=rwkv_k_a[0], r_k=rwkv_r_k[0],
                             lnx_w=rwkv_lnx_w[0], lnx_b=rwkv_lnx_b[0])
    zero_rwkv = jnp.zeros((bp, RWKV_PAIRS, LANES, LANES), F32)
    yp, rsf_p, rsb_p = scan(rp, kp, vp, lwp, ap, s0f=zero_rwkv, s0b=zero_rwkv)
    ys, _, _ = scan(rs, ks, vs, lws, as_, s0f=_pair_blockdiag(state_rwkv[:, 0, 0]), s0b=_pair_blockdiag(state_rwkv[:, 0, 1]))
    new_state_rwkv = jnp.stack([_pair_diag_blocks(rsf_p), _pair_diag_blocks(rsb_p)], axis=1)[:, None]
    o = jnp.concatenate([yp.reshape(np_, d), ys.reshape(ns, d)], axis=0)
    y = _rwkv_back(o, g, x1, mod1, mod_idx, tm, rwkv_w_out[0], norm_f)

    return (y[:np_].reshape(bp, tp, d), y[np_:].reshape(bs, ts, d), new_state_gla, new_state_rwkv)
```

```python
import functools

import jax
import jax.numpy as jnp
from jax import lax
from jax.experimental import pallas as pl
from jax.experimental.pallas import tpu as pltpu

F32 = jnp.float32
BF16 = jnp.bfloat16

D_MODEL = 1024
EPS = 1e-6
GRID_W = 64
GLA_HEADS = 4
GLA_DK = 512
GLA_DV = 1024
GLA_DKH = 128
GLA_DVH = 256
GLA_GATE_RANK = 16
GLA_GATE_NORM = 16.0
RWKV_HEAD = 64
RWKV_HEADS = 16
RWKV_PAIRS = 8
RWKV_RANK = 64
LNX_EPS = 64e-5
CHUNK = 64
INV_BASE = 8
TOKEN_TILE = 256
PREP_GROUP = 8
LANES = 128
VMEM_LIMIT = 56 * 1024 * 1024


def _dot(a, b):
    return jnp.dot(a.astype(BF16), b.astype(BF16), preferred_element_type=F32)


def _dot_nt(a, b):
    return lax.dot_general(a.astype(BF16), b.astype(BF16), (((1,), (1,)), ((), ())),
                           preferred_element_type=F32)


def _dot_tn(a, b):
    return lax.dot_general(a.astype(BF16), b.astype(BF16), (((0,), (0,)), ((), ())),
                           preferred_element_type=F32)


def _tri_dot(tri, x):
    hi = x.astype(BF16)
    r1 = x - hi.astype(F32)
    mid = r1.astype(BF16)
    lo = (r1 - mid.astype(F32)).astype(BF16)
    t = tri.astype(BF16)
    d = lambda p: jnp.dot(t, p, preferred_element_type=F32)
    return d(hi) + d(mid) + d(lo)


def _log_sigmoid(z):
    return jnp.minimum(z, 0.0) - jnp.log(1.0 + jnp.exp(-jnp.abs(z)))


def _sigmoid(z):
    return 1.0 / (1.0 + jnp.exp(-z))


def _silu(z):
    return z * _sigmoid(z)


def _rms(x):
    return x * lax.rsqrt(jnp.mean(x * x, axis=-1, keepdims=True) + EPS)


def _params(n_grid_dims):
    return pltpu.CompilerParams(dimension_semantics=("arbitrary",) * n_grid_dims,
                                vmem_limit_bytes=VMEM_LIMIT)


def _mm_kernel(x_ref, w_ref, b_ref, o_ref):
    o_ref[...] = _dot(x_ref[...], w_ref[...]) + b_ref[...]


def _matmul_bias(x, w, b):
    m, k = x.shape
    n = w.shape[1]
    return pl.pallas_call(
        _mm_kernel,
        grid=(1,),
        in_specs=[pl.BlockSpec((m, k), lambda i: (0, 0)),
                  pl.BlockSpec((k, n), lambda i: (0, 0)),
                  pl.BlockSpec((1, n), lambda i: (0, 0))],
        out_specs=pl.BlockSpec((m, n), lambda i: (0, 0)),
        out_shape=jax.ShapeDtypeStruct((m, n), F32),
        compiler_params=_params(1),
        name="mod_matmul",
    )(x, w.astype(BF16), b)


def _gla_front_kernel(x_ref, mod_ref, nw_ref, win_ref, wa1_ref, wa2_ref, ba_ref,
                      q_ref, k_ref, v_ref, g_ref, lg_ref):
    d = D_MODEL
    mod = mod_ref[0]
    shift, scale = mod[:, :d], mod[:, d:2 * d]
    h = _rms(x_ref[...]) * nw_ref[...] * (1.0 + scale) + shift
    hb = h.astype(BF16)
    q_ref[...] = jnp.dot(hb, win_ref[:, 0:GLA_DK], preferred_element_type=F32) * (GLA_DKH ** -0.5)
    k_ref[...] = jnp.dot(hb, win_ref[:, GLA_DK:2 * GLA_DK], preferred_element_type=F32)
    v_ref[...] = jnp.dot(hb, win_ref[:, 2 * GLA_DK:2 * GLA_DK + GLA_DV], preferred_element_type=F32)
    g_ref[...] = jnp.dot(hb, win_ref[:, 2 * GLA_DK + GLA_DV:], preferred_element_type=F32)
    t = jnp.dot(hb, wa1_ref[...], preferred_element_type=F32)
    z = _dot(t, wa2_ref[...]) + ba_ref[...]
    lg_ref[...] = _log_sigmoid(z) * (1.0 / GLA_GATE_NORM)


def _gla_front(x, mod, mod_idx, tm, norm_w, w_in, w_a1, w_a2, b_a):
    n, d = x.shape
    nmod = mod.shape[0]
    wa1 = jnp.zeros((d, LANES), F32).at[:, :GLA_GATE_RANK].set(w_a1[0]).at[:, GLA_GATE_RANK:2 * GLA_GATE_RANK].set(w_a1[1])
    wa2 = jnp.zeros((LANES, 2 * GLA_DK), F32).at[:GLA_GATE_RANK, :GLA_DK].set(w_a2[0])
    wa2 = wa2.at[GLA_GATE_RANK:2 * GLA_GATE_RANK, GLA_DK:].set(w_a2[1])
    ba = b_a.reshape(1, 2 * GLA_DK)
    row = lambda i: (i, 0)
    const = lambda i: (0, 0)
    n_in = w_in.shape[1]
    outs = pl.pallas_call(
        _gla_front_kernel,
        grid=(n // tm,),
        in_specs=[pl.BlockSpec((tm, d), row),
                  pl.BlockSpec((1, 1, 3 * d), lambda i: (mod_idx(i), 0, 0)),
                  pl.BlockSpec((1, d), const),
                  pl.BlockSpec((d, n_in), const),
                  pl.BlockSpec((d, LANES), const),
                  pl.BlockSpec((LANES, 2 * GLA_DK), const),
                  pl.BlockSpec((1, 2 * GLA_DK), const)],
        out_specs=[pl.BlockSpec((tm, GLA_DK), row), pl.BlockSpec((tm, GLA_DK), row),
                   pl.BlockSpec((tm, GLA_DV), row), pl.BlockSpec((tm, GLA_DV), row),
                   pl.BlockSpec((tm, 2 * GLA_DK), row)],
        out_shape=[jax.ShapeDtypeStruct((n, GLA_DK), F32), jax.ShapeDtypeStruct((n, GLA_DK), F32),
                   jax.ShapeDtypeStruct((n, GLA_DV), F32), jax.ShapeDtypeStruct((n, GLA_DV), F32),
                   jax.ShapeDtypeStruct((n, 2 * GLA_DK), F32)],
        compiler_params=_params(1),
        name="gla_front",
    )(x, mod.reshape(nmod, 1, 3 * d), norm_w.reshape(1, d), w_in.astype(BF16), wa1.astype(BF16),
      wa2.astype(BF16), ba)
    return outs


def _gla_chunks(insts):
    c = insts[0][0].shape[0]
    row = lax.broadcasted_iota(jnp.int32, (c, c), 0)
    col = lax.broadcasted_iota(jnp.int32, (c, c), 1)
    keep = [(col >= row) if x[5] else (col <= row) for x in insts]
    b = [_tri_dot(kp.astype(F32), x[3]) for kp, x in zip(keep, insts)]
    b_end = [bb[0:1] if x[5] else bb[c - 1:c] for bb, x in zip(b, insts)]
    qb = [x[0] * jnp.exp(bb) for x, bb in zip(insts, b)]
    kb = [x[1] * jnp.exp(-bb) for x, bb in zip(insts, b)]
    kd = [x[1] * jnp.exp(be - bb) for x, bb, be in zip(insts, b, b_end)]
    scores = [jnp.where(kp, _dot_nt(x, y), 0.0) for kp, x, y in zip(keep, qb, kb)]
    o_inter = [_dot_nt(x, ins[4]) for x, ins in zip(qb, insts)]
    upd = [_dot_tn(ins[2], x) for ins, x in zip(insts, kd)]
    o = [_dot(sc, ins[2]) + oi for sc, ins, oi in zip(scores, insts, o_inter)]
    st = [ins[4] * jnp.exp(be) + u for ins, be, u in zip(insts, b_end, upd)]
    return list(zip(o, st))


def _gla_scan_kernel(q_ref, k_ref, v_ref, lf_ref, lb_ref, s0f_ref, s0b_ref, o_ref, sf_ref, sb_ref):
    t = q_ref.shape[1]
    hb = q_ref.shape[2] // GLA_DKH
    nc = t // CHUNK
    half = nc // 2
    kh = lambda h: slice(h * GLA_DKH, (h + 1) * GLA_DKH)
    vh = lambda h: slice(h * GLA_DVH, (h + 1) * GLA_DVH)

    def both(accumulate):
        def body(i, carry):
            sls = [pl.ds(pl.multiple_of(ci * CHUNK, CHUNK), CHUNK) for ci in (i, nc - 1 - i)]
            insts = [(q_ref[0, sls[d], kh(h)], k_ref[0, sls[d], kh(h)], v_ref[0, sls[d], vh(h)],
                      (lf_ref, lb_ref)[d][0, sls[d], kh(h)], carry[d * hb + h], d == 1)
                     for d in range(2) for h in range(hb)]
            res = _gla_chunks(insts)
            for d in range(2):
                for h in range(hb):
                    o = res[d * hb + h][0]
                    if accumulate:
                        o_ref[0, sls[d], vh(h)] += o
                    else:
                        o_ref[0, sls[d], vh(h)] = o
            return tuple(st for _, st in res)
        return body

    init = tuple(s0f_ref[0, h] for h in range(hb)) + tuple(s0b_ref[0, h] for h in range(hb))
    carry = lax.fori_loop(0, half, both(False), init)
    carry = lax.fori_loop(half, nc, both(True), carry)
    for h in range(hb):
        sf_ref[0, h] = carry[h]
        sb_ref[0, h] = carry[hb + h]


def _gla_scan(q, k, v, lg, s0f, s0b, hb):
    bsz, t, _ = q.shape
    ng = GLA_HEADS // hb
    hd = lambda b, h: (b, 0, h)
    st = lambda b, h: (b, h, 0, 0)
    kblk = pl.BlockSpec((1, t, hb * GLA_DKH), hd)
    vblk = pl.BlockSpec((1, t, hb * GLA_DVH), hd)
    sblk = pl.BlockSpec((1, hb, GLA_DVH, GLA_DKH), st)
    return pl.pallas_call(
        _gla_scan_kernel,
        grid=(bsz, ng),
        in_specs=[kblk, kblk, vblk, kblk, pl.BlockSpec((1, t, hb * GLA_DKH), lambda b, h: (b, 0, ng + h)),
                  sblk, sblk],
        out_specs=[vblk, sblk, sblk],
        out_shape=[jax.ShapeDtypeStruct((bsz, t, GLA_DV), F32),
                   jax.ShapeDtypeStruct((bsz, GLA_HEADS, GLA_DVH, GLA_DKH), F32),
                   jax.ShapeDtypeStruct((bsz, GLA_HEADS, GLA_DVH, GLA_DKH), F32)],
        compiler_params=_params(2),
        name="gla_scan",
    )(q, k, v, lg, lg, s0f, s0b)


def _gla_back_kernel(o_ref, g_ref, x_ref, mod0_ref, mod1_ref, gn_ref, wout_ref, nw1_ref, x1_ref, h1_ref):
    d = D_MODEL
    o = o_ref[...]
    gn = gn_ref[...]
    parts = [_rms(o[:, h * GLA_DVH:(h + 1) * GLA_DVH]) * gn for h in range(GLA_HEADS)]
    on = jnp.concatenate(parts, axis=-1) * _silu(g_ref[...])
    out = _dot(on, wout_ref[...])
    gate = mod0_ref[0][:, 2 * d:]
    x1 = x_ref[...] + gate * out
    x1_ref[...] = x1
    mod1 = mod1_ref[0]
    h1_ref[...] = _rms(x1) * nw1_ref[...] * (1.0 + mod1[:, d:2 * d]) + mod1[:, :d]


def _gla_back(o, g, x, mod0, mod1, mod_idx, tm, gla_norm, w_out, norm_w1):
    n, d = x.shape
    nmod = mod0.shape[0]
    row = lambda i: (i, 0)
    const = lambda i: (0, 0)
    modspec = pl.BlockSpec((1, 1, 3 * d), lambda i: (mod_idx(i), 0, 0))
    return pl.pallas_call(
        _gla_back_kernel,
        grid=(n // tm,),
        in_specs=[pl.BlockSpec((tm, d), row), pl.BlockSpec((tm, d), row), pl.BlockSpec((tm, d), row),
                  modspec, modspec,
                  pl.BlockSpec((1, GLA_DVH), const), pl.BlockSpec((d, d), const), pl.BlockSpec((1, d), const)],
        out_specs=[pl.BlockSpec((tm, d), row), pl.BlockSpec((tm, d), row)],
        out_shape=[jax.ShapeDtypeStruct((n, d), F32), jax.ShapeDtypeStruct((n, d), F32)],
        compiler_params=_params(1),
        name="gla_back",
    )(o, g, x, mod0.reshape(nmod, 1, 3 * d), mod1.reshape(nmod, 1, 3 * d), gla_norm.reshape(1, GLA_DVH),
      w_out.astype(BF16), norm_w1.reshape(1, d))


def _rwkv_front_kernel(h_ref, hs_ref, mu_ref, wrkvg_ref, w1_ref, w2_ref, w0_ref, a1_ref, a2_ref, a0_ref,
                       r_ref, k_ref, v_ref, g_ref, lw_ref, a_ref):
    h = h_ref[...]
    dh = hs_ref[...] - h
    mix = lambda p: (h + dh * mu_ref[p:p + 1, :]).astype(BF16)
    r_ref[...] = jnp.dot(mix(0), wrkvg_ref[0], preferred_element_type=F32)
    k_ref[...] = jnp.dot(mix(1), wrkvg_ref[1], preferred_element_type=F32)
    v_ref[...] = jnp.dot(mix(2), wrkvg_ref[2], preferred_element_type=F32)
    g_ref[...] = jnp.dot(mix(3), wrkvg_ref[3], preferred_element_type=F32)
    tw = jnp.tanh(jnp.dot(mix(4), w1_ref[...], preferred_element_type=F32))
    w_log = _log_sigmoid(w0_ref[...] + _dot(tw, w2_ref[...])) - 0.5
    lw_ref[...] = -jnp.exp(w_log)
    ta = jnp.dot(mix(5), a1_ref[...], preferred_element_type=F32)
    a_ref[...] = _sigmoid(a0_ref[...] + _dot(ta, a2_ref[...]))


def _rwkv_front(h, hs, tm, mu, w_rkvg, w0, w1, w2, a0, a1, a2):
    n, d = h.shape
    rk = RWKV_RANK
    cat1 = lambda w: jnp.concatenate([w[0], w[1]], axis=1)
    bd2 = lambda w: jnp.zeros((2 * rk, 2 * d), F32).at[:rk, :d].set(w[0]).at[rk:, d:].set(w[1])
    row = lambda i: (i, 0)
    const = lambda i: (0, 0)
    return pl.pallas_call(
        _rwkv_front_kernel,
        grid=(n // tm,),
        in_specs=[pl.BlockSpec((tm, d), row), pl.BlockSpec((tm, d), row),
                  pl.BlockSpec((8, d), const),
                  pl.BlockSpec((4, d, d), lambda i: (0, 0, 0)),
                  pl.BlockSpec((d, 2 * rk), const), pl.BlockSpec((2 * rk, 2 * d), const), pl.BlockSpec((1, 2 * d), const),
                  pl.BlockSpec((d, 2 * rk), const), pl.BlockSpec((2 * rk, 2 * d), const), pl.BlockSpec((1, 2 * d), const)],
        out_specs=[pl.BlockSpec((tm, d), row)] * 4 + [pl.BlockSpec((tm, 2 * d), row)] * 2,
        out_shape=[jax.ShapeDtypeStruct((n, d), F32)] * 4 + [jax.ShapeDtypeStruct((n, 2 * d), F32)] * 2,
        compiler_params=_params(1),
        name="rwkv_front",
    )(h, hs, jnp.zeros((8, d), F32).at[:6].set(mu), w_rkvg.astype(BF16),
      cat1(w1).astype(BF16), bd2(w2).astype(BF16), w0.reshape(1, 2 * d),
      cat1(a1).astype(BF16), bd2(a2).astype(BF16), a0.reshape(1, 2 * d))


def _seg_sum(x, head0):
    s0 = jnp.sum(jnp.where(head0, x, 0.0), axis=-1, keepdims=True)
    s1 = jnp.sum(jnp.where(head0, 0.0, x), axis=-1, keepdims=True)
    return jnp.where(head0, s0, s1)


def _stack(x, head0):
    return jnp.concatenate([jnp.where(head0, x, 0.0), jnp.where(head0, 0.0, x)], axis=0)


def _rwkv_chunk_terms(insts):
    n = len(insts)
    c = insts[0][0].shape[0]
    lane = lax.broadcasted_iota(jnp.int32, (c, LANES), 1)
    head0 = lane < RWKV_HEAD
    row = lax.broadcasted_iota(jnp.int32, (c, c), 0)
    col = lax.broadcasted_iota(jnp.int32, (c, c), 1)
    ti = lax.broadcasted_iota(jnp.int32, (c, LANES), 0)
    si = lane & (c - 1)
    prow = lax.broadcasted_iota(jnp.int32, (LANES, LANES), 0)
    pcol = lax.broadcasted_iota(jnp.int32, (LANES, LANES), 1)
    same_head = (prow >= RWKV_HEAD) == (pcol >= RWKV_HEAD)
    each = lambda f, *ls: [f(*xs) for xs in zip(*ls)]
    rev = [x[8] for x in insts]
    stk = lambda x: _stack(x, head0).astype(BF16)

    cum = [_tri_dot(((col >= row) if x[8] else (col <= row)).astype(F32), x[3]) for x in insts]
    tot = [cm[0:1] if rv else cm[c - 1:c] for cm, rv in zip(cum, rev)]

    a_t, r_t, b_ts, k_ts, b_h, k_h, v_s, vv, bonus = [], [], [], [], [], [], [], [], []
    for (r, k, v, lw, a, k_k, k_a, r_k, _), cm, tt in zip(insts, cum, tot):
        kk = k * k_k
        kk = kk / jnp.maximum(jnp.sqrt(_seg_sum(kk * kk, head0)), 1e-12)
        kd = k * (1.0 + (a - 1.0) * k_a)
        bv = kk * a
        bonus.append(_seg_sum(r * kd * r_k, head0) * v)
        e_neg = jnp.exp(-cm)
        e_tail = jnp.exp(tt - cm)
        a_t.append(-kk * jnp.exp(cm - lw))
        r_t.append(r * jnp.exp(cm))
        b_ts.append(stk(bv * e_neg))
        k_ts.append(stk(kd * e_neg))
        b_h.append(bv * e_tail)
        k_h.append(kd * e_tail)
        v_s.append(stk(v))
        vv.append(v)

    strict = [(si > ti) if rv else (si < ti) for rv in rev]
    incl = [(si >= ti) if rv else (si <= ti) for rv in rev]
    ar = each(lambda x, y: jnp.concatenate([x, y], axis=0).astype(BF16), a_t, r_t)
    g_b = each(_dot_nt, ar, b_ts)
    g_k = each(_dot_nt, ar, k_ts)
    n_ab = each(lambda m_, x: jnp.where(m_, x[:c], 0.0), strict, g_b)
    l_ak = each(lambda m_, x: jnp.where(m_, x[:c], 0.0), strict, g_k)
    p_rb = each(lambda m_, x: jnp.where(m_, x[c:], 0.0), incl, g_b)
    p_rk = each(lambda m_, x: jnp.where(m_, x[c:], 0.0), incl, g_k)

    same_blk = lambda h: (ti // h) == (si // h)
    n_d = each(lambda x: jnp.where(same_blk(INV_BASE), x, 0.0), n_ab)
    steps = max(1, (INV_BASE - 1).bit_length())
    q = n_d
    tinv = n_d
    q = each(lambda x: _dot(x, stk(x)), q) if steps > 1 else q
    for j in range(1, steps):
        last = j == steps - 1
        qs = each(stk, q)
        if last:
            z = each(_dot, tinv, qs)
            tinv = each(lambda t_, x, y: t_ + x + y, tinv, q, z)
        else:
            both = each(lambda x, t_, w_: _dot(jnp.concatenate([x, t_], axis=0), w_), q, tinv, qs)
            tinv = each(lambda t_, x, y: t_ + x + y[c:], tinv, q, both)
            q = each(lambda y: y[:c], both)
    h = INV_BASE
    while h < c:
        link = same_blk(2 * h) & jnp.logical_not(same_blk(h))
        n_off = each(lambda x: jnp.where(link, x, 0.0), n_ab)
        y = each(lambda t_, x: x + _dot(t_, stk(x)), tinv, n_off)
        tinv = each(lambda t_, x: t_ + x + _dot(x, stk(t_)), tinv, y)
        h *= 2
    a_bar = each(lambda t_, x: x + _dot(t_, stk(x)), tinv, a_t)
    w = each(_dot, l_ak, v_s)
    u0 = each(lambda t_, x: x + _dot(t_, stk(x)), tinv, w)
    r_bar = each(lambda x, p, y: x + _dot(p, stk(y)), r_t, p_rb, a_bar)
    y0 = each(lambda p, p2, x, y: _dot(jnp.concatenate([p, p2], axis=1), jnp.concatenate([stk(x), y], axis=0)),
              p_rb, p_rk, u0, v_s)
    m = each(lambda x, y: jnp.where(same_head, _dot_tn(x, y), 0.0), a_bar, b_h)
    s0p = each(lambda x, y, x2, y2: jnp.where(same_head, _dot_tn(jnp.concatenate([x, x2], axis=0),
                                                                  jnp.concatenate([y, y2], axis=0)), 0.0),
               u0, b_h, vv, k_h)
    return [(r_bar[i], y0[i], m[i], s0p[i], jnp.exp(tot[i]), bonus[i]) for i in range(n)]


def _rwkv_scan_kernel(r_ref, k_ref, v_ref, lwf_ref, lwb_ref, af_ref, ab_ref,
                      kkf_ref, kkb_ref, kaf_ref, kab_ref, rkf_ref, rkb_ref, lnw_ref, lnb_ref,
                      s0f_ref, s0b_ref, o_ref, sf_ref, sb_ref,
                      rbar_ref, y0_ref, m_ref, s0p_ref, dec_ref, bonus_ref, yb_ref):
    bb, t = r_ref.shape[0], r_ref.shape[1]
    nc = t // CHUNK
    dirs = ((lwf_ref, af_ref, kkf_ref, kaf_ref, rkf_ref, False), (lwb_ref, ab_ref, kkb_ref, kab_ref, rkb_ref, True))
    group = min(bb * nc, PREP_GROUP)
    assert (bb * nc) % group == 0
    chunk_rows = lambda ci: pl.ds(pl.multiple_of(ci * CHUNK, CHUNK), CHUNK)

    def prep(gi, carry):
        insts, where = [], []
        for j in range(group):
            fi = gi * group + j
            e, sl = fi // nc, chunk_rows(fi % nc)
            r, k, v = r_ref[e, sl, :], k_ref[e, sl, :], v_ref[e, sl, :]
            for d, (lw_ref, a_ref, kk_ref, ka_ref, rk_ref, reverse) in enumerate(dirs):
                insts.append((r, k, v, lw_ref[e, sl, :], a_ref[e, sl, :], kk_ref[...], ka_ref[...], rk_ref[...], reverse))
                where.append((d, fi, e, sl))
        terms = _rwkv_chunk_terms(insts)
        for (d, fi, e, sl), (r_bar, y0, m, s0p, dec, bon) in zip(where, terms):
            rbar_ref[d, fi] = r_bar.astype(BF16)
            y0_ref[d, fi] = y0
            m_ref[d, fi] = m.astype(BF16)
            s0p_ref[d, fi] = s0p
            dec_ref[d, fi] = jnp.broadcast_to(dec, (8, LANES))
        for j in range(group):
            _, _, e, sl = where[2 * j]
            bonus_ref[e, sl, :] = terms[2 * j][5] + terms[2 * j + 1][5]
        return carry

    lax.fori_loop(0, bb * nc // group, prep, 0)

    def seq(i, carry):
        cis = (i, nc - 1 - i)
        idx = [(d, e) for d in range(2) for e in range(bb)]
        sb16 = [s.astype(BF16) for s in carry]
        sm = [jnp.dot(sb16[n], m_ref[d, e * nc + cis[d]], preferred_element_type=F32) for n, (d, e) in enumerate(idx)]
        ys = [_dot_nt(rbar_ref[d, e * nc + cis[d]], sb16[n]) + y0_ref[d, e * nc + cis[d]] for n, (d, e) in enumerate(idx)]
        for n, (d, e) in enumerate(idx):
            (o_ref, yb_ref)[d][e, chunk_rows(cis[d]), :] = ys[n]
        return tuple(carry[n] * dec_ref[d, e * nc + cis[d]][0:1] + sm[n] + s0p_ref[d, e * nc + cis[d]]
                     for n, (d, e) in enumerate(idx))

    init = tuple(s0f_ref[e, 0] for e in range(bb)) + tuple(s0b_ref[e, 0] for e in range(bb))
    final = lax.fori_loop(0, nc, seq, init)
    for e in range(bb):
        sf_ref[e, 0] = final[e]
        sb_ref[e, 0] = final[bb + e]

    lane = lax.broadcasted_iota(jnp.int32, (CHUNK, LANES), 1)
    head0 = lane < RWKV_HEAD
    inv_n = 1.0 / RWKV_HEAD

    def finish(gi, carry):
        at = [((gi * group + j) // nc, chunk_rows((gi * group + j) % nc)) for j in range(group)]
        y = [o_ref[e, sl, :] + yb_ref[e, sl, :] for e, sl in at]
        yc = [x - _seg_sum(x, head0) * inv_n for x in y]
        var = [_seg_sum(x * x, head0) * inv_n for x in yc]
        for (e, sl), x, vr in zip(at, yc, var):
            o_ref[e, sl, :] = x * lax.rsqrt(vr + LNX_EPS) * lnw_ref[...] + lnb_ref[...] + bonus_ref[e, sl, :]
        return carry

    lax.fori_loop(0, bb * nc // group, finish, 0)


def _rwkv_scan(r, k, v, lw, a, k_k, k_a, r_k, lnx_w, lnx_b, s0f, s0b, bb):
    bsz, t, d = r.shape
    nc = t // CHUNK
    seq = lambda b, p: (b, 0, p)
    seq_b = lambda b, p: (b, 0, RWKV_PAIRS + p)
    par = lambda b, p: (0, p)
    par_b = lambda b, p: (0, RWKV_PAIRS + p)
    st = lambda b, p: (b, p, 0, 0)
    sblk = pl.BlockSpec((bb, t, LANES), seq)
    sblk_b = pl.BlockSpec((bb, t, LANES), seq_b)
    pblk = pl.BlockSpec((1, LANES), par)
    pblk_b = pl.BlockSpec((1, LANES), par_b)
    stblk = pl.BlockSpec((bb, 1, LANES, LANES), st)
    k_k, k_a, r_k = (p.reshape(1, 2 * d) for p in (k_k, k_a, r_k))
    return pl.pallas_call(
        _rwkv_scan_kernel,
        grid=(bsz // bb, RWKV_PAIRS),
        in_specs=[sblk, sblk, sblk, sblk, sblk_b, sblk, sblk_b,
                  pblk, pblk_b, pblk, pblk_b, pblk, pblk_b, pblk, pblk, stblk, stblk],
        out_specs=[sblk, stblk, stblk],
        out_shape=[jax.ShapeDtypeStruct((bsz, t, d), F32),
                   jax.ShapeDtypeStruct((bsz, RWKV_PAIRS, LANES, LANES), F32),
                   jax.ShapeDtypeStruct((bsz, RWKV_PAIRS, LANES, LANES), F32)],
        scratch_shapes=[pltpu.VMEM((2, bb * nc, CHUNK, LANES), BF16), pltpu.VMEM((2, bb * nc, CHUNK, LANES), F32),
                        pltpu.VMEM((2, bb * nc, LANES, LANES), BF16), pltpu.VMEM((2, bb * nc, LANES, LANES), F32),
                        pltpu.VMEM((2, bb * nc, 8, LANES), F32), pltpu.VMEM((bb, t, LANES), F32),
                        pltpu.VMEM((bb, t, LANES), F32)],
        compiler_params=_params(2),
        name="rwkv_scan",
    )(r, k, v, lw, lw, a, a, k_k, k_k, k_a, k_a, r_k, r_k,
      lnx_w.reshape(1, d), lnx_b.reshape(1, d), s0f, s0b)


def _rwkv_back_kernel(o_ref, g_ref, x_ref, mod_ref, wout_ref, nf_ref, y_ref):
    out = _dot(o_ref[...] * _silu(g_ref[...]), wout_ref[...])
    x2 = x_ref[...] + mod_ref[0][:, 2 * D_MODEL:] * out
    y_ref[...] = _rms(x2) * nf_ref[...]


def _rwkv_back(o, g, x, mod1, mod_idx, tm, w_out, norm_f):
    n, d = x.shape
    nmod = mod1.shape[0]
    row = lambda i: (i, 0)
    const = lambda i: (0, 0)
    return pl.pallas_call(
        _rwkv_back_kernel,
        grid=(n // tm,),
        in_specs=[pl.BlockSpec((tm, d), row), pl.BlockSpec((tm, d), row), pl.BlockSpec((tm, d), row),
                  pl.BlockSpec((1, 1, 3 * d), lambda i: (mod_idx(i), 0, 0)),
                  pl.BlockSpec((d, d), const), pl.BlockSpec((1, d), const)],
        out_specs=pl.BlockSpec((tm, d), row),
        out_shape=jax.ShapeDtypeStruct((n, d), F32),
        compiler_params=_params(1),
        name="rwkv_back",
    )(o, g, x, mod1.reshape(nmod, 1, 3 * d), w_out.astype(BF16), norm_f.reshape(1, d))


def _shift_ctx(x):
    p = jnp.pad(x, ((0, 0), (1, 1), (0, 0)))
    return 0.5 * (p[:, :-2] + p[:, 2:])


def _shift_grid(x):
    b, t, d = x.shape
    g = x.reshape(b, t // GRID_W, GRID_W, d)
    p = jnp.pad(g, ((0, 0), (1, 1), (1, 1), (0, 0)))
    nb = 0.25 * (p[:, :-2, 1:-1] + p[:, 2:, 1:-1] + p[:, 1:-1, :-2] + p[:, 1:-1, 2:])
    return nb.reshape(b, t, d)


def _pair_blockdiag(s):
    b = s.shape[0]
    s = s.reshape(b, RWKV_PAIRS, 2, RWKV_HEAD, RWKV_HEAD)
    z = jnp.zeros_like(s[:, :, 0])
    top = jnp.concatenate([s[:, :, 0], z], axis=-1)
    bot = jnp.concatenate([z, s[:, :, 1]], axis=-1)
    return jnp.concatenate([top, bot], axis=-2)


def _pair_diag_blocks(s):
    b = s.shape[0]
    h = RWKV_HEAD
    return jnp.stack([s[:, :, :h, :h], s[:, :, h:, h:]], axis=2).reshape(b, RWKV_HEADS, h, h)


def kernel(x_prompt, x_sample, state_gla, state_rwkv, c, c_ctx, w_mod, b_mod, norm_w, gla_w_in, gla_w_a1,
           gla_w_a2, gla_b_a, gla_norm, gla_w_out, rwkv_mu, rwkv_w_rkvg, rwkv_w0, rwkv_w1, rwkv_w2, rwkv_a0,
           rwkv_a1, rwkv_a2, rwkv_k_k, rwkv_k_a, rwkv_r_k, rwkv_lnx_w, rwkv_lnx_b, rwkv_w_out, norm_f):
    d = D_MODEL
    bp, tp, _ = x_prompt.shape
    bs, ts, _ = x_sample.shape
    assert tp % CHUNK == 0 and ts % CHUNK == 0 and ts % GRID_W == 0

    nrows = -(-(1 + bs) // 8) * 8
    cond = jnp.zeros((nrows, d), F32).at[0].set(c_ctx).at[1:1 + bs].set(c)
    cond = cond * (1.0 / (1.0 + jnp.exp(-cond)))
    mod0 = _matmul_bias(cond, w_mod[0], b_mod[0].reshape(1, 3 * d))
    mod1 = _matmul_bias(cond, w_mod[1], b_mod[1].reshape(1, 3 * d))

    def trunk(x3, mod_idx, shift_fn, gla_s0, rwkv_s0, gla_hb, rwkv_bb):
        b, t, _ = x3.shape
        n = b * t
        tm = min(TOKEN_TILE, t)
        x = x3.reshape(n, d)
        seq = lambda arr: arr.reshape(b, t, arr.shape[-1])
        q, k, v, g, lg = _gla_front(x, mod0, mod_idx, tm, norm_w[0], gla_w_in[0], gla_w_a1[0], gla_w_a2[0], gla_b_a[0])
        o, gsf, gsb = _gla_scan(seq(q), seq(k), seq(v), seq(lg), gla_s0[0], gla_s0[1], gla_hb)
        x1, h1 = _gla_back(o.reshape(n, d), g, x, mod0, mod1, mod_idx, tm, gla_norm[0], gla_w_out[0], norm_w[1])
        hs = shift_fn(seq(h1)).reshape(n, d)
        r, k, v, g, lw, a = _rwkv_front(h1, hs, tm, rwkv_mu[0], rwkv_w_rkvg[0], rwkv_w0[0], rwkv_w1[0], rwkv_w2[0],
                                        rwkv_a0[0], rwkv_a1[0], rwkv_a2[0])
        o, rsf, rsb = _rwkv_scan(seq(r), seq(k), seq(v), seq(lw), seq(a), rwkv_k_k[0], rwkv_k_a[0], rwkv_r_k[0],
                                 rwkv_lnx_w[0], rwkv_lnx_b[0], rwkv_s0[0], rwkv_s0[1], rwkv_bb)
        y = _rwkv_back(o.reshape(n, d), g, x1, mod1, mod_idx, tm, rwkv_w_out[0], norm_f)
        return y.reshape(b, t, d), (gsf, gsb), (rsf, rsb)

    zero_gla = jnp.zeros((bp, GLA_HEADS, GLA_DVH, GLA_DKH), F32)
    zero_rwkv = jnp.zeros((bp, RWKV_PAIRS, LANES, LANES), F32)
    y_prompt, (gsf, gsb), (rsf, rsb) = trunk(x_prompt, lambda i: 0, _shift_ctx, (zero_gla, zero_gla),
                                             (zero_rwkv, zero_rwkv), GLA_HEADS, min(bp, 4))
    new_state_gla = jnp.swapaxes(jnp.stack([gsf, gsb], axis=1), -1, -2)[:, None]
    new_state_rwkv = jnp.stack([_pair_diag_blocks(rsf), _pair_diag_blocks(rsb)], axis=1)[:, None]

    tiles_per_seq = ts // min(TOKEN_TILE, ts)
    sg = jnp.swapaxes(state_gla[:, 0], -1, -2)
    y_sample, _, _ = trunk(x_sample, lambda i: 1 + i // tiles_per_seq, _shift_grid, (sg[:, 0], sg[:, 1]),
                           (_pair_blockdiag(state_rwkv[:, 0, 0]), _pair_blockdiag(state_rwkv[:, 0, 1])),
                           GLA_HEADS // 2, min(bs, 2))
    return (y_prompt, y_sample, new_state_gla, new_state_rwkv)
```

```python
import functools

import jax
import jax.numpy as jnp
from jax import lax
from jax.experimental import pallas as pl
from jax.experimental.pallas import tpu as pltpu

F32 = jnp.float32
BF16 = jnp.bfloat16

D_MODEL = 1024
EPS = 1e-6
GRID_W = 64
GLA_HEADS = 4
GLA_DK = 512
GLA_DV = 1024
GLA_DKH = 128
GLA_DVH = 256
GLA_GATE_RANK = 16
GLA_GATE_NORM = 16.0
RWKV_HEAD = 64
RWKV_HEADS = 16
RWKV_PAIRS = 8
RWKV_RANK = 64
LNX_EPS = 64e-5
CHUNK = 64
INV_BASE = 8
TOKEN_TILE = 256
PREP_GROUP = 8
LANES = 128
VMEM_LIMIT = 56 * 1024 * 1024


def _dot(a, b):
    return jnp.dot(a.astype(BF16), b.astype(BF16), preferred_element_type=F32)


def _dot_nt(a, b):
    return lax.dot_general(a.astype(BF16), b.astype(BF16), (((1,), (1,)), ((), ())),
                           preferred_element_type=F32)


def _dot_tn(a, b):
    return lax.dot_general(a.astype(BF16), b.astype(BF16), (((0,), (0,)), ((), ())),
                           preferred_element_type=F32)


def _tri_dot(tri, x):
    hi = x.astype(BF16)
    lo = (x - hi.astype(F32)).astype(BF16)
    t = tri.astype(BF16)
    return jnp.dot(t, hi, preferred_element_type=F32) + jnp.dot(t, lo, preferred_element_type=F32)


def _log_sigmoid(z):
    return jnp.minimum(z, 0.0) - jnp.log(1.0 + jnp.exp(-jnp.abs(z)))


def _sigmoid(z):
    return 1.0 / (1.0 + jnp.exp(-z))


def _silu(z):
    return z * _sigmoid(z)


def _rms(x):
    return x * lax.rsqrt(jnp.mean(x * x, axis=-1, keepdims=True) + EPS)


def _params(n_grid_dims):
    return pltpu.CompilerParams(dimension_semantics=("arbitrary",) * n_grid_dims,
                                vmem_limit_bytes=VMEM_LIMIT)


def _mm_kernel(x_ref, w_ref, b_ref, o_ref):
    o_ref[...] = _dot(x_ref[...], w_ref[...]) + b_ref[...]


def _matmul_bias(x, w, b):
    m, k = x.shape
    n = w.shape[1]
    return pl.pallas_call(
        _mm_kernel,
        grid=(1,),
        in_specs=[pl.BlockSpec((m, k), lambda i: (0, 0)),
                  pl.BlockSpec((k, n), lambda i: (0, 0)),
                  pl.BlockSpec((1, n), lambda i: (0, 0))],
        out_specs=pl.BlockSpec((m, n), lambda i: (0, 0)),
        out_shape=jax.ShapeDtypeStruct((m, n), F32),
        compiler_params=_params(1),
        name="mod_matmul",
    )(x, w.astype(BF16), b)


def _gla_front_kernel(x_ref, mod_ref, nw_ref, win_ref, wa1_ref, wa2_ref, ba_ref,
                      q_ref, k_ref, v_ref, g_ref, lg_ref):
    d = D_MODEL
    mod = mod_ref[0]
    shift, scale = mod[:, :d], mod[:, d:2 * d]
    h = _rms(x_ref[...]) * nw_ref[...] * (1.0 + scale) + shift
    hb = h.astype(BF16)
    proj = lambda lo, hi: jnp.dot(hb, win_ref[:, lo:hi], preferred_element_type=F32)
    q_ref[...] = (proj(0, GLA_DK) * (GLA_DKH ** -0.5)).astype(BF16)
    k_ref[...] = proj(GLA_DK, 2 * GLA_DK).astype(BF16)
    v_ref[...] = proj(2 * GLA_DK, 2 * GLA_DK + GLA_DV).astype(BF16)
    g_ref[...] = proj(2 * GLA_DK + GLA_DV, 2 * GLA_DK + 2 * GLA_DV).astype(BF16)
    t = jnp.dot(hb, wa1_ref[...], preferred_element_type=F32)
    z = _dot(t, wa2_ref[...]) + ba_ref[...]
    lg_ref[...] = _log_sigmoid(z) * (1.0 / GLA_GATE_NORM)


def _gla_front(x, mod, mod_idx, tm, norm_w, w_in, w_a1, w_a2, b_a):
    n, d = x.shape
    nmod = mod.shape[0]
    wa1 = jnp.zeros((d, LANES), F32).at[:, :GLA_GATE_RANK].set(w_a1[0]).at[:, GLA_GATE_RANK:2 * GLA_GATE_RANK].set(w_a1[1])
    wa2 = jnp.zeros((LANES, 2 * GLA_DK), F32).at[:GLA_GATE_RANK, :GLA_DK].set(w_a2[0])
    wa2 = wa2.at[GLA_GATE_RANK:2 * GLA_GATE_RANK, GLA_DK:].set(w_a2[1])
    ba = b_a.reshape(1, 2 * GLA_DK)
    row = lambda i: (i, 0)
    const = lambda i: (0, 0)
    n_in = w_in.shape[1]
    outs = pl.pallas_call(
        _gla_front_kernel,
        grid=(n // tm,),
        in_specs=[pl.BlockSpec((tm, d), row),
                  pl.BlockSpec((1, 1, 3 * d), lambda i: (mod_idx(i), 0, 0)),
                  pl.BlockSpec((1, d), const),
                  pl.BlockSpec((d, n_in), const),
                  pl.BlockSpec((d, LANES), const),
                  pl.BlockSpec((LANES, 2 * GLA_DK), const),
                  pl.BlockSpec((1, 2 * GLA_DK), const)],
        out_specs=[pl.BlockSpec((tm, GLA_DK), row), pl.BlockSpec((tm, GLA_DK), row),
                   pl.BlockSpec((tm, GLA_DV), row), pl.BlockSpec((tm, GLA_DV), row),
                   pl.BlockSpec((tm, 2 * GLA_DK), row)],
        out_shape=[jax.ShapeDtypeStruct((n, GLA_DK), BF16), jax.ShapeDtypeStruct((n, GLA_DK), BF16),
                   jax.ShapeDtypeStruct((n, GLA_DV), BF16), jax.ShapeDtypeStruct((n, GLA_DV), BF16),
                   jax.ShapeDtypeStruct((n, 2 * GLA_DK), F32)],
        compiler_params=_params(1),
        name="gla_front",
    )(x, mod.reshape(nmod, 1, 3 * d), norm_w.reshape(1, d), w_in.astype(BF16), wa1.astype(BF16),
      wa2.astype(BF16), ba)
    return outs


def _gla_chunks(insts):
    c = insts[0][0].shape[0]
    row = lax.broadcasted_iota(jnp.int32, (c, c), 0)
    col = lax.broadcasted_iota(jnp.int32, (c, c), 1)
    keep = [(col >= row) if x[5] else (col <= row) for x in insts]
    b = [_tri_dot(kp.astype(F32), x[3]) for kp, x in zip(keep, insts)]
    b_end = [bb[0:1] if x[5] else bb[c - 1:c] for bb, x in zip(b, insts)]
    qb = [x[0] * jnp.exp(bb) for x, bb in zip(insts, b)]
    kb = [x[1] * jnp.exp(-bb) for x, bb in zip(insts, b)]
    kd = [x[1] * jnp.exp(be - bb) for x, bb, be in zip(insts, b, b_end)]
    scores = [jnp.where(kp, _dot_nt(x, y), 0.0) for kp, x, y in zip(keep, qb, kb)]
    o_inter = [_dot_nt(x, ins[4]) for x, ins in zip(qb, insts)]
    upd = [_dot_tn(ins[2], x) for ins, x in zip(insts, kd)]
    o = [_dot(sc, ins[2]) + oi for sc, ins, oi in zip(scores, insts, o_inter)]
    st = [ins[4] * jnp.exp(be) + u for ins, be, u in zip(insts, b_end, upd)]
    return list(zip(o, st))


def _gla_scan_kernel(q_ref, k_ref, v_ref, lf_ref, lb_ref, s0f_ref, s0b_ref, o_ref, sf_ref, sb_ref):
    t = q_ref.shape[1]
    hb = q_ref.shape[2] // GLA_DKH
    nc = t // CHUNK
    half = nc // 2
    kh = lambda h: slice(h * GLA_DKH, (h + 1) * GLA_DKH)
    vh = lambda h: slice(h * GLA_DVH, (h + 1) * GLA_DVH)

    def both(accumulate):
        def body(i, carry):
            sls = [pl.ds(pl.multiple_of(ci * CHUNK, CHUNK), CHUNK) for ci in (i, nc - 1 - i)]
            insts = [(q_ref[0, sls[d], kh(h)].astype(F32), k_ref[0, sls[d], kh(h)].astype(F32), v_ref[0, sls[d], vh(h)],
                      (lf_ref, lb_ref)[d][0, sls[d], kh(h)], carry[d * hb + h], d == 1)
                     for d in range(2) for h in range(hb)]
            res = _gla_chunks(insts)
            for d in range(2):
                for h in range(hb):
                    o = res[d * hb + h][0]
                    if accumulate:
                        o_ref[0, sls[d], vh(h)] += o
                    else:
                        o_ref[0, sls[d], vh(h)] = o
            return tuple(st for _, st in res)
        return body

    init = tuple(s0f_ref[0, h] for h in range(hb)) + tuple(s0b_ref[0, h] for h in range(hb))
    carry = lax.fori_loop(0, half, both(False), init)
    carry = lax.fori_loop(half, nc, both(True), carry)
    for h in range(hb):
        sf_ref[0, h] = carry[h]
        sb_ref[0, h] = carry[hb + h]


def _gla_scan(q, k, v, lg, s0f, s0b, hb):
    bsz, t, _ = q.shape
    ng = GLA_HEADS // hb
    hd = lambda b, h: (b, 0, h)
    st = lambda b, h: (b, h, 0, 0)
    kblk = pl.BlockSpec((1, t, hb * GLA_DKH), hd)
    vblk = pl.BlockSpec((1, t, hb * GLA_DVH), hd)
    sblk = pl.BlockSpec((1, hb, GLA_DVH, GLA_DKH), st)
    return pl.pallas_call(
        _gla_scan_kernel,
        grid=(bsz, ng),
        in_specs=[kblk, kblk, vblk, kblk, pl.BlockSpec((1, t, hb * GLA_DKH), lambda b, h: (b, 0, ng + h)),
                  sblk, sblk],
        out_specs=[vblk, sblk, sblk],
        out_shape=[jax.ShapeDtypeStruct((bsz, t, GLA_DV), F32),
                   jax.ShapeDtypeStruct((bsz, GLA_HEADS, GLA_DVH, GLA_DKH), F32),
                   jax.ShapeDtypeStruct((bsz, GLA_HEADS, GLA_DVH, GLA_DKH), F32)],
        compiler_params=_params(2),
        name="gla_scan",
    )(q, k, v, lg, lg, s0f, s0b)


def _gla_back_kernel(o_ref, g_ref, x_ref, mod0_ref, mod1_ref, gn_ref, wout_ref, nw1_ref, x1_ref, h1_ref):
    d = D_MODEL
    o = o_ref[...]
    gn = gn_ref[...]
    parts = [_rms(o[:, h * GLA_DVH:(h + 1) * GLA_DVH]) * gn for h in range(GLA_HEADS)]
    on = jnp.concatenate(parts, axis=-1) * _silu(g_ref[...].astype(F32))
    out = _dot(on, wout_ref[...])
    gate = mod0_ref[0][:, 2 * d:]
    x1 = x_ref[...] + gate * out
    x1_ref[...] = x1
    mod1 = mod1_ref[0]
    h1_ref[...] = _rms(x1) * nw1_ref[...] * (1.0 + mod1[:, d:2 * d]) + mod1[:, :d]


def _gla_back(o, g, x, mod0, mod1, mod_idx, tm, gla_norm, w_out, norm_w1):
    n, d = x.shape
    nmod = mod0.shape[0]
    row = lambda i: (i, 0)
    const = lambda i: (0, 0)
    modspec = pl.BlockSpec((1, 1, 3 * d), lambda i: (mod_idx(i), 0, 0))
    return pl.pallas_call(
        _gla_back_kernel,
        grid=(n // tm,),
        in_specs=[pl.BlockSpec((tm, d), row), pl.BlockSpec((tm, d), row), pl.BlockSpec((tm, d), row),
                  modspec, modspec,
                  pl.BlockSpec((1, GLA_DVH), const), pl.BlockSpec((d, d), const), pl.BlockSpec((1, d), const)],
        out_specs=[pl.BlockSpec((tm, d), row), pl.BlockSpec((tm, d), row)],
        out_shape=[jax.ShapeDtypeStruct((n, d), F32), jax.ShapeDtypeStruct((n, d), F32)],
        compiler_params=_params(1),
        name="gla_back",
    )(o, g, x, mod0.reshape(nmod, 1, 3 * d), mod1.reshape(nmod, 1, 3 * d), gla_norm.reshape(1, GLA_DVH),
      w_out.astype(BF16), norm_w1.reshape(1, d))


def _token_shift(h, up_ref, dn_ref, seq_len, grid_w):
    tm = h.shape[0]
    pos = lax.broadcasted_iota(jnp.int32, (tm, 1), 0) + pl.program_id(0) * tm
    prev = pltpu.roll(h, 1, 0)
    nxt = pltpu.roll(h, tm - 1, 0)
    t = pos % seq_len
    if grid_w is None:
        return 0.5 * (jnp.where(t == 0, 0.0, prev) + jnp.where(t == seq_len - 1, 0.0, nxt))
    col = pos % grid_w
    up = jnp.concatenate([up_ref[...], h[:tm - grid_w]], axis=0)
    dn = jnp.concatenate([h[grid_w:], dn_ref[...]], axis=0)
    up = jnp.where(t < grid_w, 0.0, up)
    dn = jnp.where(t >= seq_len - grid_w, 0.0, dn)
    left = jnp.where(col == 0, 0.0, prev)
    right = jnp.where(col == grid_w - 1, 0.0, nxt)
    return 0.25 * (up + dn + left + right)


def _rwkv_front_kernel(*refs, seq_len, grid_w):
    n_h = 1 if grid_w is None else 3
    h_ref, up_ref, dn_ref = (tuple(refs[:n_h]) + (None, None))[:3]
    (mu_ref, wrkvg_ref, w1_ref, w2_ref, w0_ref, a1_ref, a2_ref, a0_ref,
     r_ref, k_ref, v_ref, g_ref, lw_ref, a_ref) = refs[n_h:]
    h = h_ref[...]
    dh = _token_shift(h, up_ref, dn_ref, seq_len, grid_w) - h
    mix = lambda p: (h + dh * mu_ref[p:p + 1, :]).astype(BF16)
    r_ref[...] = jnp.dot(mix(0), wrkvg_ref[0], preferred_element_type=F32).astype(BF16)
    k_ref[...] = jnp.dot(mix(1), wrkvg_ref[1], preferred_element_type=F32).astype(BF16)
    v_ref[...] = jnp.dot(mix(2), wrkvg_ref[2], preferred_element_type=F32).astype(BF16)
    g_ref[...] = jnp.dot(mix(3), wrkvg_ref[3], preferred_element_type=F32).astype(BF16)
    tw = jnp.tanh(jnp.dot(mix(4), w1_ref[...], preferred_element_type=F32))
    w_log = _log_sigmoid(w0_ref[...] + _dot(tw, w2_ref[...])) - 0.5
    lw_ref[...] = -jnp.exp(w_log)
    ta = jnp.dot(mix(5), a1_ref[...], preferred_element_type=F32)
    a_ref[...] = _sigmoid(a0_ref[...] + _dot(ta, a2_ref[...]))


def _rwkv_front(h, tm, seq_len, grid_w, mu, w_rkvg, w0, w1, w2, a0, a1, a2):
    n, d = h.shape
    rk = RWKV_RANK
    cat1 = lambda w: jnp.concatenate([w[0], w[1]], axis=1)
    bd2 = lambda w: jnp.zeros((2 * rk, 2 * d), F32).at[:rk, :d].set(w[0]).at[rk:, d:].set(w[1])
    row = lambda i: (i, 0)
    const = lambda i: (0, 0)
    if grid_w is None:
        assert tm % seq_len == 0
        h_specs, h_args = [pl.BlockSpec((tm, d), row)], [h]
    else:
        assert tm % grid_w == 0 and seq_len % tm == 0
        rows_per_tile, last = tm // grid_w, n // grid_w - 1
        h_specs = [pl.BlockSpec((tm, d), row),
                   pl.BlockSpec((grid_w, d), lambda i: (jnp.maximum(i * rows_per_tile - 1, 0), 0)),
                   pl.BlockSpec((grid_w, d), lambda i: (jnp.minimum((i + 1) * rows_per_tile, last), 0))]
        h_args = [h, h, h]
    return pl.pallas_call(
        functools.partial(_rwkv_front_kernel, seq_len=seq_len, grid_w=grid_w),
        grid=(n // tm,),
        in_specs=h_specs + [
            pl.BlockSpec((8, d), const),
            pl.BlockSpec((4, d, d), lambda i: (0, 0, 0)),
            pl.BlockSpec((d, 2 * rk), const), pl.BlockSpec((2 * rk, 2 * d), const), pl.BlockSpec((1, 2 * d), const),
            pl.BlockSpec((d, 2 * rk), const), pl.BlockSpec((2 * rk, 2 * d), const), pl.BlockSpec((1, 2 * d), const)],
        out_specs=[pl.BlockSpec((tm, d), row)] * 4 + [pl.BlockSpec((tm, 2 * d), row)] * 2,
        out_shape=[jax.ShapeDtypeStruct((n, d), BF16)] * 4 + [jax.ShapeDtypeStruct((n, 2 * d), F32)] * 2,
        compiler_params=_params(1),
        name="rwkv_front",
    )(*h_args, jnp.zeros((8, d), F32).at[:6].set(mu), w_rkvg.astype(BF16),
      cat1(w1).astype(BF16), bd2(w2).astype(BF16), w0.reshape(1, 2 * d),
      cat1(a1).astype(BF16), bd2(a2).astype(BF16), a0.reshape(1, 2 * d))


def _seg_sum(x, head0):
    s0 = jnp.sum(jnp.where(head0, x, 0.0), axis=-1, keepdims=True)
    s1 = jnp.sum(jnp.where(head0, 0.0, x), axis=-1, keepdims=True)
    return jnp.where(head0, s0, s1)


def _stack(x, head0):
    return jnp.concatenate([jnp.where(head0, x, 0.0), jnp.where(head0, 0.0, x)], axis=0)


def _rwkv_chunk_terms(insts):
    n = len(insts)
    c = insts[0][0].shape[0]
    lane = lax.broadcasted_iota(jnp.int32, (c, LANES), 1)
    head0 = lane < RWKV_HEAD
    row = lax.broadcasted_iota(jnp.int32, (c, c), 0)
    col = lax.broadcasted_iota(jnp.int32, (c, c), 1)
    ti = lax.broadcasted_iota(jnp.int32, (c, LANES), 0)
    si = lane & (c - 1)
    prow = lax.broadcasted_iota(jnp.int32, (LANES, LANES), 0)
    pcol = lax.broadcasted_iota(jnp.int32, (LANES, LANES), 1)
    same_head = (prow >= RWKV_HEAD) == (pcol >= RWKV_HEAD)
    each = lambda f, *ls: [f(*xs) for xs in zip(*ls)]
    rev = [x[8] for x in insts]
    stk = lambda x: _stack(x, head0).astype(BF16)

    cum = [_tri_dot(((col >= row) if x[8] else (col <= row)).astype(F32), x[3]) for x in insts]
    tot = [cm[0:1] if rv else cm[c - 1:c] for cm, rv in zip(cum, rev)]

    a_t, r_t, b_ts, k_ts, b_h, k_h, v_s, vv, bonus = [], [], [], [], [], [], [], [], []
    for (r, k, v, lw, a, k_k, k_a, r_k, _), cm, tt in zip(insts, cum, tot):
        kk = k * k_k
        kk = kk / jnp.maximum(jnp.sqrt(_seg_sum(kk * kk, head0)), 1e-12)
        kd = k * (1.0 + (a - 1.0) * k_a)
        bv = kk * a
        bonus.append(_seg_sum(r * kd * r_k, head0) * v)
        e_neg = jnp.exp(-cm)
        e_tail = jnp.exp(tt - cm)
        a_t.append(-kk * jnp.exp(cm - lw))
        r_t.append(r * jnp.exp(cm))
        b_ts.append(stk(bv * e_neg))
        k_ts.append(stk(kd * e_neg))
        b_h.append(bv * e_tail)
        k_h.append(kd * e_tail)
        v_s.append(stk(v))
        vv.append(v)

    strict = [(si > ti) if rv else (si < ti) for rv in rev]
    incl = [(si >= ti) if rv else (si <= ti) for rv in rev]
    ar = each(lambda x, y: jnp.concatenate([x, y], axis=0).astype(BF16), a_t, r_t)
    g_b = each(_dot_nt, ar, b_ts)
    g_k = each(_dot_nt, ar, k_ts)
    n_ab = each(lambda m_, x: jnp.where(m_, x[:c], 0.0), strict, g_b)
    l_ak = each(lambda m_, x: jnp.where(m_, x[:c], 0.0), strict, g_k)
    p_rb = each(lambda m_, x: jnp.where(m_, x[c:], 0.0), incl, g_b)
    p_rk = each(lambda m_, x: jnp.where(m_, x[c:], 0.0), incl, g_k)

    same_blk = lambda h: (ti // h) == (si // h)
    n_d = each(lambda x: jnp.where(same_blk(INV_BASE), x, 0.0), n_ab)
    steps = max(1, (INV_BASE - 1).bit_length())
    q = n_d
    tinv = n_d
    q = each(lambda x: _dot(x, stk(x)), q) if steps > 1 else q
    for j in range(1, steps):
        last = j == steps - 1
        qs = each(stk, q)
        if last:
            z = each(_dot, tinv, qs)
            tinv = each(lambda t_, x, y: t_ + x + y, tinv, q, z)
        else:
            both = each(lambda x, t_, w_: _dot(jnp.concatenate([x, t_], axis=0), w_), q, tinv, qs)
            tinv = each(lambda t_, x, y: t_ + x + y[c:], tinv, q, both)
            q = each(lambda y: y[:c], both)
    h = INV_BASE
    while h < c:
        link = same_blk(2 * h) & jnp.logical_not(same_blk(h))
        n_off = each(lambda x: jnp.where(link, x, 0.0), n_ab)
        y = each(lambda t_, x: x + _dot(t_, stk(x)), tinv, n_off)
        tinv = each(lambda t_, x: t_ + x + _dot(x, stk(t_)), tinv, y)
        h *= 2
    a_bar = each(lambda t_, x: x + _dot(t_, stk(x)), tinv, a_t)
    w = each(_dot, l_ak, v_s)
    u0 = each(lambda t_, x: x + _dot(t_, stk(x)), tinv, w)
    r_bar = each(lambda x, p, y: x + _dot(p, stk(y)), r_t, p_rb, a_bar)
    y0 = each(lambda p, p2, x, y: _dot(jnp.concatenate([p, p2], axis=1), jnp.concatenate([stk(x), y], axis=0)),
              p_rb, p_rk, u0, v_s)
    m = each(lambda x, y: jnp.where(same_head, _dot_tn(x, y), 0.0), a_bar, b_h)
    s0p = each(lambda x, y, x2, y2: jnp.where(same_head, _dot_tn(jnp.concatenate([x, x2], axis=0),
                                                                  jnp.concatenate([y, y2], axis=0)), 0.0),
               u0, b_h, vv, k_h)
    return [(r_bar[i], y0[i], m[i], s0p[i], jnp.exp(tot[i]), bonus[i]) for i in range(n)]


def _rwkv_scan_kernel(r_ref, k_ref, v_ref, lwf_ref, lwb_ref, af_ref, ab_ref,
                      kkf_ref, kkb_ref, kaf_ref, kab_ref, rkf_ref, rkb_ref, lnw_ref, lnb_ref,
                      s0f_ref, s0b_ref, o_ref, sf_ref, sb_ref,
                      rbar_ref, y0_ref, m_ref, s0p_ref, dec_ref, bonus_ref, yb_ref):
    bb, t = r_ref.shape[0], r_ref.shape[1]
    nc = t // CHUNK
    dirs = ((lwf_ref, af_ref, kkf_ref, kaf_ref, rkf_ref, False), (lwb_ref, ab_ref, kkb_ref, kab_ref, rkb_ref, True))
    group = min(bb * nc, PREP_GROUP)
    assert (bb * nc) % group == 0
    chunk_rows = lambda ci: pl.ds(pl.multiple_of(ci * CHUNK, CHUNK), CHUNK)

    def prep(gi, carry):
        insts, where = [], []
        for j in range(group):
            fi = gi * group + j
            e, sl = fi // nc, chunk_rows(fi % nc)
            r, k, v = (ref[e, sl, :].astype(F32) for ref in (r_ref, k_ref, v_ref))
            for d, (lw_ref, a_ref, kk_ref, ka_ref, rk_ref, reverse) in enumerate(dirs):
                insts.append((r, k, v, lw_ref[e, sl, :], a_ref[e, sl, :], kk_ref[...], ka_ref[...], rk_ref[...], reverse))
                where.append((d, fi, e, sl))
        terms = _rwkv_chunk_terms(insts)
        for (d, fi, e, sl), (r_bar, y0, m, s0p, dec, bon) in zip(where, terms):
            rbar_ref[d, fi] = r_bar.astype(BF16)
            y0_ref[d, fi] = y0
            m_ref[d, fi] = m.astype(BF16)
            s0p_ref[d, fi] = s0p
            dec_ref[d, fi] = jnp.broadcast_to(dec, (8, LANES))
        for j in range(group):
            _, _, e, sl = where[2 * j]
            bonus_ref[e, sl, :] = terms[2 * j][5] + terms[2 * j + 1][5]
        return carry

    lax.fori_loop(0, bb * nc // group, prep, 0)

    def seq(i, carry):
        cis = (i, nc - 1 - i)
        idx = [(d, e) for d in range(2) for e in range(bb)]
        sb16 = [s.astype(BF16) for s in carry]
        sm = [jnp.dot(sb16[n], m_ref[d, e * nc + cis[d]], preferred_element_type=F32) for n, (d, e) in enumerate(idx)]
        ys = [_dot_nt(rbar_ref[d, e * nc + cis[d]], sb16[n]) + y0_ref[d, e * nc + cis[d]] for n, (d, e) in enumerate(idx)]
        for n, (d, e) in enumerate(idx):
            (o_ref, yb_ref)[d][e, chunk_rows(cis[d]), :] = ys[n]
        return tuple(carry[n] * dec_ref[d, e * nc + cis[d]][0:1] + sm[n] + s0p_ref[d, e * nc + cis[d]]
                     for n, (d, e) in enumerate(idx))

    init = tuple(s0f_ref[e, 0] for e in range(bb)) + tuple(s0b_ref[e, 0] for e in range(bb))
    final = lax.fori_loop(0, nc, seq, init)
    for e in range(bb):
        sf_ref[e, 0] = final[e]
        sb_ref[e, 0] = final[bb + e]

    lane = lax.broadcasted_iota(jnp.int32, (CHUNK, LANES), 1)
    head0 = lane < RWKV_HEAD
    inv_n = 1.0 / RWKV_HEAD

    def finish(gi, carry):
        at = [((gi * group + j) // nc, chunk_rows((gi * group + j) % nc)) for j in range(group)]
        y = [o_ref[e, sl, :] + yb_ref[e, sl, :] for e, sl in at]
        yc = [x - _seg_sum(x, head0) * inv_n for x in y]
        var = [_seg_sum(x * x, head0) * inv_n for x in yc]
        for (e, sl), x, vr in zip(at, yc, var):
            o_ref[e, sl, :] = x * lax.rsqrt(vr + LNX_EPS) * lnw_ref[...] + lnb_ref[...] + bonus_ref[e, sl, :]
        return carry

    lax.fori_loop(0, bb * nc // group, finish, 0)


def _rwkv_scan(r, k, v, lw, a, k_k, k_a, r_k, lnx_w, lnx_b, s0f, s0b, bb):
    bsz, t, d = r.shape
    nc = t // CHUNK
    seq = lambda b, p: (b, 0, p)
    seq_b = lambda b, p: (b, 0, RWKV_PAIRS + p)
    par = lambda b, p: (0, p)
    par_b = lambda b, p: (0, RWKV_PAIRS + p)
    st = lambda b, p: (b, p, 0, 0)
    sblk = pl.BlockSpec((bb, t, LANES), seq)
    sblk_b = pl.BlockSpec((bb, t, LANES), seq_b)
    pblk = pl.BlockSpec((1, LANES), par)
    pblk_b = pl.BlockSpec((1, LANES), par_b)
    stblk = pl.BlockSpec((bb, 1, LANES, LANES), st)
    k_k, k_a, r_k = (p.reshape(1, 2 * d) for p in (k_k, k_a, r_k))
    return pl.pallas_call(
        _rwkv_scan_kernel,
        grid=(bsz // bb, RWKV_PAIRS),
        in_specs=[sblk, sblk, sblk, sblk, sblk_b, sblk, sblk_b,
                  pblk, pblk_b, pblk, pblk_b, pblk, pblk_b, pblk, pblk, stblk, stblk],
        out_specs=[sblk, stblk, stblk],
        out_shape=[jax.ShapeDtypeStruct((bsz, t, d), F32),
                   jax.ShapeDtypeStruct((bsz, RWKV_PAIRS, LANES, LANES), F32),
                   jax.ShapeDtypeStruct((bsz, RWKV_PAIRS, LANES, LANES), F32)],
        scratch_shapes=[pltpu.VMEM((2, bb * nc, CHUNK, LANES), BF16), pltpu.VMEM((2, bb * nc, CHUNK, LANES), F32),
                        pltpu.VMEM((2, bb * nc, LANES, LANES), BF16), pltpu.VMEM((2, bb * nc, LANES, LANES), F32),
                        pltpu.VMEM((2, bb * nc, 8, LANES), F32), pltpu.VMEM((bb, t, LANES), F32),
                        pltpu.VMEM((bb, t, LANES), F32)],
        compiler_params=_params(2),
        name="rwkv_scan",
    )(r, k, v, lw, lw, a, a, k_k, k_k, k_a, k_a, r_k, r_k,
      lnx_w.reshape(1, d), lnx_b.reshape(1, d), s0f, s0b)


def _rwkv_back_kernel(o_ref, g_ref, x_ref, mod_ref, wout_ref, nf_ref, y_ref):
    out = _dot(o_ref[...] * _silu(g_ref[...].astype(F32)), wout_ref[...])
    x2 = x_ref[...] + mod_ref[0][:, 2 * D_MODEL:] * out
    y_ref[...] = _rms(x2) * nf_ref[...]


def _rwkv_back(o, g, x, mod1, mod_idx, tm, w_out, norm_f):
    n, d = x.shape
    nmod = mod1.shape[0]
    row = lambda i: (i, 0)
    const = lambda i: (0, 0)
    return pl.pallas_call(
        _rwkv_back_kernel,
        grid=(n // tm,),
        in_specs=[pl.BlockSpec((tm, d), row), pl.BlockSpec((tm, d), row), pl.BlockSpec((tm, d), row),
                  pl.BlockSpec((1, 1, 3 * d), lambda i: (mod_idx(i), 0, 0)),
                  pl.BlockSpec((d, d), const), pl.BlockSpec((1, d), const)],
        out_specs=pl.BlockSpec((tm, d), row),
        out_shape=jax.ShapeDtypeStruct((n, d), F32),
        compiler_params=_params(1),
        name="rwkv_back",
    )(o, g, x, mod1.reshape(nmod, 1, 3 * d), w_out.astype(BF16), norm_f.reshape(1, d))


def _pair_blockdiag(s):
    b = s.shape[0]
    s = s.reshape(b, RWKV_PAIRS, 2, RWKV_HEAD, RWKV_HEAD)
    z = jnp.zeros_like(s[:, :, 0])
    top = jnp.concatenate([s[:, :, 0], z], axis=-1)
    bot = jnp.concatenate([z, s[:, :, 1]], axis=-1)
    return jnp.concatenate([top, bot], axis=-2)


def _pair_diag_blocks(s):
    b = s.shape[0]
    h = RWKV_HEAD
    return jnp.stack([s[:, :, :h, :h], s[:, :, h:, h:]], axis=2).reshape(b, RWKV_HEADS, h, h)


def kernel(x_prompt, x_sample, state_gla, state_rwkv, c, c_ctx, w_mod, b_mod, norm_w, gla_w_in, gla_w_a1,
           gla_w_a2, gla_b_a, gla_norm, gla_w_out, rwkv_mu, rwkv_w_rkvg, rwkv_w0, rwkv_w1, rwkv_w2, rwkv_a0,
           rwkv_a1, rwkv_a2, rwkv_k_k, rwkv_k_a, rwkv_r_k, rwkv_lnx_w, rwkv_lnx_b, rwkv_w_out, norm_f):
    d = D_MODEL
    bp, tp, _ = x_prompt.shape
    bs, ts, _ = x_sample.shape
    assert tp % CHUNK == 0 and ts % CHUNK == 0 and ts % GRID_W == 0

    nrows = -(-(1 + bs) // 8) * 8
    cond = jnp.zeros((nrows, d), F32).at[0].set(c_ctx).at[1:1 + bs].set(c)
    cond = cond * (1.0 / (1.0 + jnp.exp(-cond)))
    mod0 = _matmul_bias(cond, w_mod[0], b_mod[0].reshape(1, 3 * d))
    mod1 = _matmul_bias(cond, w_mod[1], b_mod[1].reshape(1, 3 * d))

    def trunk(x3, mod_idx, grid_w, gla_s0, rwkv_s0, gla_hb, rwkv_bb):
        b, t, _ = x3.shape
        n = b * t
        tm = min(TOKEN_TILE, t)
        x = x3.reshape(n, d)
        seq = lambda arr: arr.reshape(b, t, arr.shape[-1])
        q, k, v, g, lg = _gla_front(x, mod0, mod_idx, tm, norm_w[0], gla_w_in[0], gla_w_a1[0], gla_w_a2[0], gla_b_a[0])
        o, gsf, gsb = _gla_scan(seq(q), seq(k), seq(v), seq(lg), gla_s0[0], gla_s0[1], gla_hb)
        x1, h1 = _gla_back(o.reshape(n, d), g, x, mod0, mod1, mod_idx, tm, gla_norm[0], gla_w_out[0], norm_w[1])
        r, k, v, g, lw, a = _rwkv_front(h1, tm, t, grid_w, rwkv_mu[0], rwkv_w_rkvg[0], rwkv_w0[0], rwkv_w1[0], rwkv_w2[0],
                                        rwkv_a0[0], rwkv_a1[0], rwkv_a2[0])
        o, rsf, rsb = _rwkv_scan(seq(r), seq(k), seq(v), seq(lw), seq(a), rwkv_k_k[0], rwkv_k_a[0], rwkv_r_k[0],
                                 rwkv_lnx_w[0], rwkv_lnx_b[0], rwkv_s0[0], rwkv_s0[1], rwkv_bb)
        y = _rwkv_back(o.reshape(n, d), g, x1, mod1, mod_idx, tm, rwkv_w_out[0], norm_f)
        return y.reshape(b, t, d), (gsf, gsb), (rsf, rsb)

    zero_gla = jnp.zeros((bp, GLA_HEADS, GLA_DVH, GLA_DKH), F32)
    zero_rwkv = jnp.zeros((bp, RWKV_PAIRS, LANES, LANES), F32)
    y_prompt, (gsf, gsb), (rsf, rsb) = trunk(x_prompt, lambda i: 0, None, (zero_gla, zero_gla),
                                             (zero_rwkv, zero_rwkv), GLA_HEADS, min(bp, 4))
    new_state_gla = jnp.swapaxes(jnp.stack([gsf, gsb], axis=1), -1, -2)[:, None]
    new_state_rwkv = jnp.stack([_pair_diag_blocks(rsf), _pair_diag_blocks(rsb)], axis=1)[:, None]

    tiles_per_seq = ts // min(TOKEN_TILE, ts)
    sg = jnp.swapaxes(state_gla[:, 0], -1, -2)
    y_sample, _, _ = trunk(x_sample, lambda i: 1 + i // tiles_per_seq, GRID_W, (sg[:, 0], sg[:, 1]),
                           (_pair_blockdiag(state_rwkv[:, 0, 0]), _pair_blockdiag(state_rwkv[:, 0, 1])),
                           GLA_HEADS, min(bs, 2))
    return (y_prompt, y_sample, new_state_gla, new_state_rwkv)
```

```python
import functools

import jax
import jax.numpy as jnp
from jax import lax
from jax.experimental import pallas as pl
from jax.experimental.pallas import tpu as pltpu

F32 = jnp.float32
BF16 = jnp.bfloat16

D_MODEL = 1024
EPS = 1e-6
GRID_W = 64
GLA_HEADS = 4
GLA_DK = 512
GLA_DV = 1024
GLA_DKH = 128
GLA_DVH = 256
GLA_GATE_RANK = 16
GLA_GATE_NORM = 16.0
RWKV_HEAD = 64
RWKV_HEADS = 16
RWKV_PAIRS = 8
RWKV_RANK = 64
LNX_EPS = 64e-5
RWKV_DECAY_SCALE = 0.6065306597126334
CHUNK = 64
INV_BASE = 8
TOKEN_TILE = 256
PREP_GROUP = 8
LANES = 128
VMEM_LIMIT = 56 * 1024 * 1024


def _dot(a, b):
    return jnp.dot(a.astype(BF16), b.astype(BF16), preferred_element_type=F32)


def _dot_nt(a, b):
    return lax.dot_general(a.astype(BF16), b.astype(BF16), (((1,), (1,)), ((), ())),
                           preferred_element_type=F32)


def _dot_tn(a, b):
    return lax.dot_general(a.astype(BF16), b.astype(BF16), (((0,), (0,)), ((), ())),
                           preferred_element_type=F32)


def _tri_dot(tri, x):
    hi = x.astype(BF16)
    lo = (x - hi.astype(F32)).astype(BF16)
    t = tri.astype(BF16)
    return jnp.dot(t, hi, preferred_element_type=F32) + jnp.dot(t, lo, preferred_element_type=F32)


def _log_sigmoid(z):
    return jnp.minimum(z, 0.0) - jnp.log(1.0 + jnp.exp(-jnp.abs(z)))


def _sigmoid(z):
    return 1.0 / (1.0 + jnp.exp(-z))


def _silu(z):
    return z * _sigmoid(z)


def _rms(x):
    return x * lax.rsqrt(jnp.mean(x * x, axis=-1, keepdims=True) + EPS)


def _params(n_grid_dims):
    return pltpu.CompilerParams(dimension_semantics=("arbitrary",) * n_grid_dims,
                                vmem_limit_bytes=VMEM_LIMIT)


def _mm_kernel(x_ref, w_ref, b_ref, o_ref):
    o_ref[...] = _dot(x_ref[...], w_ref[...]) + b_ref[...]


def _matmul_bias(x, w, b):
    m, k = x.shape
    n = w.shape[1]
    return pl.pallas_call(
        _mm_kernel,
        grid=(1,),
        in_specs=[pl.BlockSpec((m, k), lambda i: (0, 0)),
                  pl.BlockSpec((k, n), lambda i: (0, 0)),
                  pl.BlockSpec((1, n), lambda i: (0, 0))],
        out_specs=pl.BlockSpec((m, n), lambda i: (0, 0)),
        out_shape=jax.ShapeDtypeStruct((m, n), F32),
        compiler_params=_params(1),
        name="mod_matmul",
    )(x, w.astype(BF16), b)


def _gla_front_kernel(x_ref, mod_ref, nw_ref, win_ref, wa1_ref, wa2_ref, ba_ref,
                      q_ref, k_ref, v_ref, g_ref, lg_ref):
    d = D_MODEL
    mod = mod_ref[0]
    shift, scale = mod[:, :d], mod[:, d:2 * d]
    h = _rms(x_ref[...]) * nw_ref[...] * (1.0 + scale) + shift
    hb = h.astype(BF16)
    proj = lambda lo, hi: jnp.dot(hb, win_ref[:, lo:hi], preferred_element_type=F32)
    q_ref[...] = (proj(0, GLA_DK) * (GLA_DKH ** -0.5)).astype(BF16)
    k_ref[...] = proj(GLA_DK, 2 * GLA_DK).astype(BF16)
    v_ref[...] = proj(2 * GLA_DK, 2 * GLA_DK + GLA_DV).astype(BF16)
    g_ref[...] = proj(2 * GLA_DK + GLA_DV, 2 * GLA_DK + 2 * GLA_DV).astype(BF16)
    t = jnp.dot(hb, wa1_ref[...], preferred_element_type=F32)
    z = _dot(t, wa2_ref[...]) + ba_ref[...]
    lg_ref[...] = _log_sigmoid(z) * (1.0 / GLA_GATE_NORM)


def _gla_front(x, mod, mod_idx, tm, norm_w, w_in, w_a1, w_a2, b_a):
    n, d = x.shape
    nmod = mod.shape[0]
    wa1 = jnp.zeros((d, LANES), F32).at[:, :GLA_GATE_RANK].set(w_a1[0]).at[:, GLA_GATE_RANK:2 * GLA_GATE_RANK].set(w_a1[1])
    wa2 = jnp.zeros((LANES, 2 * GLA_DK), F32).at[:GLA_GATE_RANK, :GLA_DK].set(w_a2[0])
    wa2 = wa2.at[GLA_GATE_RANK:2 * GLA_GATE_RANK, GLA_DK:].set(w_a2[1])
    ba = b_a.reshape(1, 2 * GLA_DK)
    row = lambda i: (i, 0)
    const = lambda i: (0, 0)
    n_in = w_in.shape[1]
    outs = pl.pallas_call(
        _gla_front_kernel,
        grid=(n // tm,),
        in_specs=[pl.BlockSpec((tm, d), row),
                  pl.BlockSpec((1, 1, 3 * d), lambda i: (mod_idx(i), 0, 0)),
                  pl.BlockSpec((1, d), const),
                  pl.BlockSpec((d, n_in), const),
                  pl.BlockSpec((d, LANES), const),
                  pl.BlockSpec((LANES, 2 * GLA_DK), const),
                  pl.BlockSpec((1, 2 * GLA_DK), const)],
        out_specs=[pl.BlockSpec((tm, GLA_DK), row), pl.BlockSpec((tm, GLA_DK), row),
                   pl.BlockSpec((tm, GLA_DV), row), pl.BlockSpec((tm, GLA_DV), row),
                   pl.BlockSpec((tm, 2 * GLA_DK), row)],
        out_shape=[jax.ShapeDtypeStruct((n, GLA_DK), BF16), jax.ShapeDtypeStruct((n, GLA_DK), BF16),
                   jax.ShapeDtypeStruct((n, GLA_DV), BF16), jax.ShapeDtypeStruct((n, GLA_DV), BF16),
                   jax.ShapeDtypeStruct((n, 2 * GLA_DK), F32)],
        compiler_params=_params(1),
        name="gla_front",
    )(x, mod.reshape(nmod, 1, 3 * d), norm_w.reshape(1, d), w_in.astype(BF16), wa1.astype(BF16),
      wa2.astype(BF16), ba)
    return outs


def _gla_chunks(insts):
    c = insts[0][0].shape[0]
    row = lax.broadcasted_iota(jnp.int32, (c, c), 0)
    col = lax.broadcasted_iota(jnp.int32, (c, c), 1)
    keep = [(col >= row) if x[5] else (col <= row) for x in insts]
    b = [_tri_dot(kp.astype(F32), x[3]) for kp, x in zip(keep, insts)]
    b_end = [bb[0:1] if x[5] else bb[c - 1:c] for bb, x in zip(b, insts)]
    qb = [x[0] * jnp.exp(bb) for x, bb in zip(insts, b)]
    kb = [x[1] * jnp.exp(-bb) for x, bb in zip(insts, b)]
    kd = [x[1] * jnp.exp(be - bb) for x, bb, be in zip(insts, b, b_end)]
    scores = [jnp.where(kp, _dot_nt(x, y), 0.0) for kp, x, y in zip(keep, qb, kb)]
    o_inter = [_dot_nt(x, ins[4]) for x, ins in zip(qb, insts)]
    upd = [_dot_tn(ins[2], x) for ins, x in zip(insts, kd)]
    o = [_dot(sc, ins[2]) + oi for sc, ins, oi in zip(scores, insts, o_inter)]
    st = [ins[4] * jnp.exp(be) + u for ins, be, u in zip(insts, b_end, upd)]
    return list(zip(o, st))


def _gla_scan_kernel(*refs, has_s0, emit_state):
    q_ref, k_ref, v_ref, lf_ref, lb_ref = refs[:5]
    s0_ref = refs[5] if has_s0 else None
    o_ref = refs[5 + has_s0]
    s_ref = refs[6 + has_s0] if emit_state else None
    t = q_ref.shape[1]
    hb = q_ref.shape[2] // GLA_DKH
    nc = t // CHUNK
    half = nc // 2
    kh = lambda h: slice(h * GLA_DKH, (h + 1) * GLA_DKH)
    vh = lambda h: slice(h * GLA_DVH, (h + 1) * GLA_DVH)

    def both(accumulate):
        def body(i, carry):
            sls = [pl.ds(pl.multiple_of(ci * CHUNK, CHUNK), CHUNK) for ci in (i, nc - 1 - i)]
            insts = [(q_ref[0, sls[d], kh(h)].astype(F32), k_ref[0, sls[d], kh(h)].astype(F32), v_ref[0, sls[d], vh(h)],
                      (lf_ref, lb_ref)[d][0, sls[d], kh(h)], carry[d * hb + h], d == 1)
                     for d in range(2) for h in range(hb)]
            res = _gla_chunks(insts)
            for d in range(2):
                for h in range(hb):
                    o = res[d * hb + h][0]
                    if accumulate:
                        o_ref[0, sls[d], vh(h)] += o
                    else:
                        o_ref[0, sls[d], vh(h)] = o
            return tuple(st for _, st in res)
        return body

    if has_s0:
        init = tuple(s0_ref[0, d, h].T for d in range(2) for h in range(hb))
    else:
        init = tuple(jnp.zeros((GLA_DVH, GLA_DKH), F32) for _ in range(2 * hb))
    carry = lax.fori_loop(0, half, both(False), init)
    carry = lax.fori_loop(half, nc, both(True), carry)
    if emit_state:
        for d in range(2):
            for h in range(hb):
                s_ref[0, d, h] = carry[d * hb + h].T


def _gla_scan(q, k, v, lg, s0, hb, emit_state):
    bsz, t, _ = q.shape
    ng = GLA_HEADS // hb
    hd = lambda b, h: (b, 0, h)
    kblk = pl.BlockSpec((1, t, hb * GLA_DKH), hd)
    vblk = pl.BlockSpec((1, t, hb * GLA_DVH), hd)
    sblk = pl.BlockSpec((1, 2, hb, GLA_DKH, GLA_DVH), lambda b, h: (b, 0, h, 0, 0))
    s_shape = jax.ShapeDtypeStruct((bsz, 2, GLA_HEADS, GLA_DKH, GLA_DVH), F32)
    has_s0 = s0 is not None
    return pl.pallas_call(
        functools.partial(_gla_scan_kernel, has_s0=has_s0, emit_state=emit_state),
        grid=(bsz, ng),
        in_specs=[kblk, kblk, vblk, kblk, pl.BlockSpec((1, t, hb * GLA_DKH), lambda b, h: (b, 0, ng + h))]
        + [sblk] * has_s0,
        out_specs=[vblk] + [sblk] * emit_state,
        out_shape=[jax.ShapeDtypeStruct((bsz, t, GLA_DV), F32)] + [s_shape] * emit_state,
        compiler_params=_params(2),
        name="gla_scan",
    )(q, k, v, lg, lg, *([s0] * has_s0))


def _gla_back_kernel(o_ref, g_ref, x_ref, mod0_ref, mod1_ref, gn_ref, wout_ref, nw1_ref, x1_ref, h1_ref):
    d = D_MODEL
    o = o_ref[...]
    gn = gn_ref[...]
    parts = [_rms(o[:, h * GLA_DVH:(h + 1) * GLA_DVH]) * gn for h in range(GLA_HEADS)]
    on = jnp.concatenate(parts, axis=-1) * _silu(g_ref[...].astype(F32))
    out = _dot(on, wout_ref[...])
    gate = mod0_ref[0][:, 2 * d:]
    x1 = x_ref[...] + gate * out
    x1_ref[...] = x1
    mod1 = mod1_ref[0]
    h1_ref[...] = _rms(x1) * nw1_ref[...] * (1.0 + mod1[:, d:2 * d]) + mod1[:, :d]


def _gla_back(o, g, x, mod0, mod1, mod_idx, tm, gla_norm, w_out, norm_w1):
    n, d = x.shape
    nmod = mod0.shape[0]
    row = lambda i: (i, 0)
    const = lambda i: (0, 0)
    modspec = pl.BlockSpec((1, 1, 3 * d), lambda i: (mod_idx(i), 0, 0))
    return pl.pallas_call(
        _gla_back_kernel,
        grid=(n // tm,),
        in_specs=[pl.BlockSpec((tm, d), row), pl.BlockSpec((tm, d), row), pl.BlockSpec((tm, d), row),
                  modspec, modspec,
                  pl.BlockSpec((1, GLA_DVH), const), pl.BlockSpec((d, d), const), pl.BlockSpec((1, d), const)],
        out_specs=[pl.BlockSpec((tm, d), row), pl.BlockSpec((tm, d), row)],
        out_shape=[jax.ShapeDtypeStruct((n, d), F32), jax.ShapeDtypeStruct((n, d), F32)],
        compiler_params=_params(1),
        name="gla_back",
    )(o, g, x, mod0.reshape(nmod, 1, 3 * d), mod1.reshape(nmod, 1, 3 * d), gla_norm.reshape(1, GLA_DVH),
      w_out.astype(BF16), norm_w1.reshape(1, d))


def _token_shift(h, up_ref, dn_ref, seq_len, grid_w):
    tm = h.shape[0]
    pos = lax.broadcasted_iota(jnp.int32, (tm, 1), 0) + pl.program_id(0) * tm
    prev = pltpu.roll(h, 1, 0)
    nxt = pltpu.roll(h, tm - 1, 0)
    t = pos % seq_len
    if grid_w is None:
        return 0.5 * (jnp.where(t == 0, 0.0, prev) + jnp.where(t == seq_len - 1, 0.0, nxt))
    col = pos % grid_w
    up = jnp.concatenate([up_ref[...], h[:tm - grid_w]], axis=0)
    dn = jnp.concatenate([h[grid_w:], dn_ref[...]], axis=0)
    up = jnp.where(t < grid_w, 0.0, up)
    dn = jnp.where(t >= seq_len - grid_w, 0.0, dn)
    left = jnp.where(col == 0, 0.0, prev)
    right = jnp.where(col == grid_w - 1, 0.0, nxt)
    return 0.25 * (up + dn + left + right)


def _rwkv_front_kernel(*refs, seq_len, grid_w):
    n_h = 1 if grid_w is None else 3
    h_ref, up_ref, dn_ref = (tuple(refs[:n_h]) + (None, None))[:3]
    (mu_ref, wrkvg_ref, w1_ref, w2_ref, w0_ref, a1_ref, a2_ref, a0_ref,
     r_ref, k_ref, v_ref, g_ref, lw_ref, a_ref) = refs[n_h:]
    h = h_ref[...]
    dh = _token_shift(h, up_ref, dn_ref, seq_len, grid_w) - h
    mix = lambda p: (h + dh * mu_ref[p:p + 1, :]).astype(BF16)
    r_ref[...] = jnp.dot(mix(0), wrkvg_ref[0], preferred_element_type=F32).astype(BF16)
    k_ref[...] = jnp.dot(mix(1), wrkvg_ref[1], preferred_element_type=F32).astype(BF16)
    v_ref[...] = jnp.dot(mix(2), wrkvg_ref[2], preferred_element_type=F32).astype(BF16)
    g_ref[...] = jnp.dot(mix(3), wrkvg_ref[3], preferred_element_type=F32).astype(BF16)
    tw = jnp.tanh(jnp.dot(mix(4), w1_ref[...], preferred_element_type=F32))
    lw_ref[...] = -RWKV_DECAY_SCALE * _sigmoid(w0_ref[...] + _dot(tw, w2_ref[...]))
    ta = jnp.dot(mix(5), a1_ref[...], preferred_element_type=F32)
    a_ref[...] = _sigmoid(a0_ref[...] + _dot(ta, a2_ref[...]))


def _rwkv_front(h, tm, seq_len, grid_w, mu, w_rkvg, w0, w1, w2, a0, a1, a2):
    n, d = h.shape
    rk = RWKV_RANK
    cat1 = lambda w: jnp.concatenate([w[0], w[1]], axis=1)
    bd2 = lambda w: jnp.zeros((2 * rk, 2 * d), F32).at[:rk, :d].set(w[0]).at[rk:, d:].set(w[1])
    row = lambda i: (i, 0)
    const = lambda i: (0, 0)
    if grid_w is None:
        assert tm % seq_len == 0
        h_specs, h_args = [pl.BlockSpec((tm, d), row)], [h]
    else:
        assert tm % grid_w == 0 and seq_len % tm == 0
        rows_per_tile, last = tm // grid_w, n // grid_w - 1
        h_specs = [pl.BlockSpec((tm, d), row),
                   pl.BlockSpec((grid_w, d), lambda i: (jnp.maximum(i * rows_per_tile - 1, 0), 0)),
                   pl.BlockSpec((grid_w, d), lambda i: (jnp.minimum((i + 1) * rows_per_tile, last), 0))]
        h_args = [h, h, h]
    return pl.pallas_call(
        functools.partial(_rwkv_front_kernel, seq_len=seq_len, grid_w=grid_w),
        grid=(n // tm,),
        in_specs=h_specs + [
            pl.BlockSpec((8, d), const),
            pl.BlockSpec((4, d, d), lambda i: (0, 0, 0)),
            pl.BlockSpec((d, 2 * rk), const), pl.BlockSpec((2 * rk, 2 * d), const), pl.BlockSpec((1, 2 * d), const),
            pl.BlockSpec((d, 2 * rk), const), pl.BlockSpec((2 * rk, 2 * d), const), pl.BlockSpec((1, 2 * d), const)],
        out_specs=[pl.BlockSpec((tm, d), row)] * 4 + [pl.BlockSpec((tm, 2 * d), row)] * 2,
        out_shape=[jax.ShapeDtypeStruct((n, d), BF16)] * 4 + [jax.ShapeDtypeStruct((n, 2 * d), F32)] * 2,
        compiler_params=_params(1),
        name="rwkv_front",
    )(*h_args, jnp.zeros((8, d), F32).at[:6].set(mu), w_rkvg.astype(BF16),
      cat1(w1).astype(BF16), bd2(w2).astype(BF16), w0.reshape(1, 2 * d),
      cat1(a1).astype(BF16), bd2(a2).astype(BF16), a0.reshape(1, 2 * d))


def _seg_sum(x, head0):
    s0 = jnp.sum(jnp.where(head0, x, 0.0), axis=-1, keepdims=True)
    s1 = jnp.sum(jnp.where(head0, 0.0, x), axis=-1, keepdims=True)
    return jnp.where(head0, s0, s1)


def _stack(x, head0):
    return jnp.concatenate([jnp.where(head0, x, 0.0), jnp.where(head0, 0.0, x)], axis=0)


def _rwkv_chunk_terms(insts):
    n = len(insts)
    c = insts[0][0].shape[0]
    lane = lax.broadcasted_iota(jnp.int32, (c, LANES), 1)
    head0 = lane < RWKV_HEAD
    row = lax.broadcasted_iota(jnp.int32, (c, c), 0)
    col = lax.broadcasted_iota(jnp.int32, (c, c), 1)
    ti = lax.broadcasted_iota(jnp.int32, (c, LANES), 0)
    si = lane & (c - 1)
    prow = lax.broadcasted_iota(jnp.int32, (LANES, LANES), 0)
    pcol = lax.broadcasted_iota(jnp.int32, (LANES, LANES), 1)
    same_head = (prow >= RWKV_HEAD) == (pcol >= RWKV_HEAD)
    each = lambda f, *ls: [f(*xs) for xs in zip(*ls)]
    rev = [x[8] for x in insts]
    stk = lambda x: _stack(x, head0).astype(BF16)

    cum = [_tri_dot(((col >= row) if x[8] else (col <= row)).astype(F32), x[3]) for x in insts]
    tot = [cm[0:1] if rv else cm[c - 1:c] for cm, rv in zip(cum, rev)]

    a_t, r_t, b_ts, k_ts, b_h, k_h, v_s, vv, bonus = [], [], [], [], [], [], [], [], []
    for (r, k, v, lw, a, k_k, k_a, r_k, _), cm, tt in zip(insts, cum, tot):
        kk = k * k_k
        kk = kk / jnp.maximum(jnp.sqrt(_seg_sum(kk * kk, head0)), 1e-12)
        kd = k * (1.0 + (a - 1.0) * k_a)
        bv = kk * a
        bonus.append(_seg_sum(r * kd * r_k, head0) * v)
        e_neg = jnp.exp(-cm)
        e_tail = jnp.exp(tt - cm)
        a_t.append(-kk * jnp.exp(cm - lw))
        r_t.append(r * jnp.exp(cm))
        b_ts.append(stk(bv * e_neg))
        k_ts.append(stk(kd * e_neg))
        b_h.append(bv * e_tail)
        k_h.append(kd * e_tail)
        v_s.append(stk(v))
        vv.append(v)

    strict = [(si > ti) if rv else (si < ti) for rv in rev]
    incl = [(si >= ti) if rv else (si <= ti) for rv in rev]
    ar = each(lambda x, y: jnp.concatenate([x, y], axis=0).astype(BF16), a_t, r_t)
    g_b = each(_dot_nt, ar, b_ts)
    g_k = each(_dot_nt, ar, k_ts)
    n_ab = each(lambda m_, x: jnp.where(m_, x[:c], 0.0), strict, g_b)
    l_ak = each(lambda m_, x: jnp.where(m_, x[:c], 0.0), strict, g_k)
    p_rb = each(lambda m_, x: jnp.where(m_, x[c:], 0.0), incl, g_b)
    p_rk = each(lambda m_, x: jnp.where(m_, x[c:], 0.0), incl, g_k)

    same_blk = lambda h: (ti // h) == (si // h)
    n_d = each(lambda x: jnp.where(same_blk(INV_BASE), x, 0.0), n_ab)
    steps = max(1, (INV_BASE - 1).bit_length())
    q = n_d
    tinv = n_d
    q = each(lambda x: _dot(x, stk(x)), q) if steps > 1 else q
    for j in range(1, steps):
        last = j == steps - 1
        qs = each(stk, q)
        if last:
            z = each(_dot, tinv, qs)
            tinv = each(lambda t_, x, y: t_ + x + y, tinv, q, z)
        else:
            both = each(lambda x, t_, w_: _dot(jnp.concatenate([x, t_], axis=0), w_), q, tinv, qs)
            tinv = each(lambda t_, x, y: t_ + x + y[c:], tinv, q, both)
            q = each(lambda y: y[:c], both)
    h = INV_BASE
    while h < c:
        link = same_blk(2 * h) & jnp.logical_not(same_blk(h))
        n_off = each(lambda x: jnp.where(link, x, 0.0), n_ab)
        y = each(lambda t_, x: x + _dot(t_, stk(x)), tinv, n_off)
        tinv = each(lambda t_, x: t_ + x + _dot(x, stk(t_)), tinv, y)
        h *= 2
    a_bar = each(lambda t_, x: x + _dot(t_, stk(x)), tinv, a_t)
    lv = each(lambda l_, p, y: _dot(jnp.concatenate([l_, p], axis=0), y), l_ak, p_rk, v_s)
    w = [x[:c] for x in lv]
    u0 = each(lambda t_, x: x + _dot(t_, stk(x)), tinv, w)
    r_bar = each(lambda x, p, y: x + _dot(p, stk(y)), r_t, p_rb, a_bar)
    y0 = each(lambda p, x, z: _dot(p, stk(x)) + z[c:], p_rb, u0, lv)
    zero = jnp.zeros((c, LANES), F32)
    ms = each(lambda ab, u, v_, bh, kh_: _dot_tn(
        jnp.concatenate([jnp.concatenate([ab, u], axis=1), jnp.concatenate([zero, v_], axis=1)], axis=0),
        jnp.concatenate([bh, kh_], axis=0)), a_bar, u0, vv, b_h, k_h)
    m = [jnp.where(same_head, x[:LANES], 0.0) for x in ms]
    s0p = [jnp.where(same_head, x[LANES:], 0.0) for x in ms]
    return [(r_bar[i], y0[i], m[i], s0p[i], jnp.exp(tot[i]), bonus[i]) for i in range(n)]


def _rwkv_scan_kernel(*refs, has_s0, emit_state):
    (r_ref, k_ref, v_ref, lwf_ref, lwb_ref, af_ref, ab_ref,
     kkf_ref, kkb_ref, kaf_ref, kab_ref, rkf_ref, rkb_ref, lnw_ref, lnb_ref) = refs[:15]
    s0_ref = refs[15] if has_s0 else None
    o_ref = refs[15 + has_s0]
    s_ref = refs[16 + has_s0] if emit_state else None
    rbar_ref, y0_ref, m_ref, s0p_ref, dec_ref, bonus_ref, yb_ref = refs[16 + has_s0 + emit_state:]
    bb, t = r_ref.shape[0], r_ref.shape[1]
    nc = t // CHUNK
    dirs = ((lwf_ref, af_ref, kkf_ref, kaf_ref, rkf_ref, False), (lwb_ref, ab_ref, kkb_ref, kab_ref, rkb_ref, True))
    group = min(bb * nc, PREP_GROUP)
    assert (bb * nc) % group == 0
    chunk_rows = lambda ci: pl.ds(pl.multiple_of(ci * CHUNK, CHUNK), CHUNK)

    def prep(gi, carry):
        insts, where = [], []
        for j in range(group):
            fi = gi * group + j
            e, sl = fi // nc, chunk_rows(fi % nc)
            r, k, v = (ref[e, sl, :].astype(F32) for ref in (r_ref, k_ref, v_ref))
            for d, (lw_ref, a_ref, kk_ref, ka_ref, rk_ref, reverse) in enumerate(dirs):
                insts.append((r, k, v, lw_ref[e, sl, :], a_ref[e, sl, :], kk_ref[...], ka_ref[...], rk_ref[...], reverse))
                where.append((d, fi, e, sl))
        terms = _rwkv_chunk_terms(insts)
        for (d, fi, e, sl), (r_bar, y0, m, s0p, dec, bon) in zip(where, terms):
            rbar_ref[d, fi] = r_bar.astype(BF16)
            y0_ref[d, fi] = y0
            m_ref[d, fi] = m.astype(BF16)
            s0p_ref[d, fi] = s0p
            dec_ref[d, fi] = jnp.broadcast_to(dec, (8, LANES))
        for j in range(group):
            _, _, e, sl = where[2 * j]
            bonus_ref[e, sl, :] = terms[2 * j][5] + terms[2 * j + 1][5]
        return carry

    lax.fori_loop(0, bb * nc // group, prep, 0)

    def seq(i, carry):
        cis = (i, nc - 1 - i)
        idx = [(d, e) for d in range(2) for e in range(bb)]
        sb16 = [s.astype(BF16) for s in carry]
        sm = [jnp.dot(sb16[n], m_ref[d, e * nc + cis[d]], preferred_element_type=F32) for n, (d, e) in enumerate(idx)]
        ys = [_dot_nt(rbar_ref[d, e * nc + cis[d]], sb16[n]) + y0_ref[d, e * nc + cis[d]] for n, (d, e) in enumerate(idx)]
        for n, (d, e) in enumerate(idx):
            (o_ref, yb_ref)[d][e, chunk_rows(cis[d]), :] = ys[n]
        return tuple(carry[n] * dec_ref[d, e * nc + cis[d]][0:1] + sm[n] + s0p_ref[d, e * nc + cis[d]]
                     for n, (d, e) in enumerate(idx))

    if has_s0:
        init = tuple(s0_ref[e, d, 0] for d in range(2) for e in range(bb))
    else:
        init = tuple(jnp.zeros((LANES, LANES), F32) for _ in range(2 * bb))
    final = lax.fori_loop(0, nc, seq, init)
    if emit_state:
        for d in range(2):
            for e in range(bb):
                s = final[d * bb + e]
                s_ref[e, d, 0] = s[:RWKV_HEAD, :RWKV_HEAD]
                s_ref[e, d, 1] = pltpu.roll(s, RWKV_HEAD, 1)[RWKV_HEAD:, :RWKV_HEAD]

    lane = lax.broadcasted_iota(jnp.int32, (CHUNK, LANES), 1)
    head0 = lane < RWKV_HEAD
    inv_n = 1.0 / RWKV_HEAD

    def finish(gi, carry):
        at = [((gi * group + j) // nc, chunk_rows((gi * group + j) % nc)) for j in range(group)]
        y = [o_ref[e, sl, :] + yb_ref[e, sl, :] for e, sl in at]
        yc = [x - _seg_sum(x, head0) * inv_n for x in y]
        var = [_seg_sum(x * x, head0) * inv_n for x in yc]
        for (e, sl), x, vr in zip(at, yc, var):
            o_ref[e, sl, :] = x * lax.rsqrt(vr + LNX_EPS) * lnw_ref[...] + lnb_ref[...] + bonus_ref[e, sl, :]
        return carry

    lax.fori_loop(0, bb * nc // group, finish, 0)


def _rwkv_scan(r, k, v, lw, a, k_k, k_a, r_k, lnx_w, lnx_b, s0, bb, emit_state):
    bsz, t, d = r.shape
    nc = t // CHUNK
    seq = lambda b, p: (b, 0, p)
    seq_b = lambda b, p: (b, 0, RWKV_PAIRS + p)
    par = lambda b, p: (0, p)
    par_b = lambda b, p: (0, RWKV_PAIRS + p)
    sblk = pl.BlockSpec((bb, t, LANES), seq)
    sblk_b = pl.BlockSpec((bb, t, LANES), seq_b)
    pblk = pl.BlockSpec((1, LANES), par)
    pblk_b = pl.BlockSpec((1, LANES), par_b)
    s0blk = pl.BlockSpec((bb, 2, 1, LANES, LANES), lambda b, p: (b, 0, p, 0, 0))
    s_blk = pl.BlockSpec((bb, 2, 2, RWKV_HEAD, RWKV_HEAD), lambda b, p: (b, 0, p, 0, 0))
    s_shape = jax.ShapeDtypeStruct((bsz, 2, RWKV_HEADS, RWKV_HEAD, RWKV_HEAD), F32)
    has_s0 = s0 is not None
    k_k, k_a, r_k = (p.reshape(1, 2 * d) for p in (k_k, k_a, r_k))
    return pl.pallas_call(
        functools.partial(_rwkv_scan_kernel, has_s0=has_s0, emit_state=emit_state),
        grid=(bsz // bb, RWKV_PAIRS),
        in_specs=[sblk, sblk, sblk, sblk, sblk_b, sblk, sblk_b,
                  pblk, pblk_b, pblk, pblk_b, pblk, pblk_b, pblk, pblk] + [s0blk] * has_s0,
        out_specs=[sblk] + [s_blk] * emit_state,
        out_shape=[jax.ShapeDtypeStruct((bsz, t, d), F32)] + [s_shape] * emit_state,
        scratch_shapes=[pltpu.VMEM((2, bb * nc, CHUNK, LANES), BF16), pltpu.VMEM((2, bb * nc, CHUNK, LANES), F32),
                        pltpu.VMEM((2, bb * nc, LANES, LANES), BF16), pltpu.VMEM((2, bb * nc, LANES, LANES), F32),
                        pltpu.VMEM((2, bb * nc, 8, LANES), F32), pltpu.VMEM((bb, t, LANES), F32),
                        pltpu.VMEM((bb, t, LANES), F32)],
        compiler_params=_params(2),
        name="rwkv_scan",
    )(r, k, v, lw, lw, a, a, k_k, k_k, k_a, k_a, r_k, r_k,
      lnx_w.reshape(1, d), lnx_b.reshape(1, d), *([s0] * has_s0))


def _rwkv_back_kernel(o_ref, g_ref, x_ref, mod_ref, wout_ref, nf_ref, y_ref):
    out = _dot(o_ref[...] * _silu(g_ref[...].astype(F32)), wout_ref[...])
    x2 = x_ref[...] + mod_ref[0][:, 2 * D_MODEL:] * out
    y_ref[...] = _rms(x2) * nf_ref[...]


def _rwkv_back(o, g, x, mod1, mod_idx, tm, w_out, norm_f):
    n, d = x.shape
    nmod = mod1.shape[0]
    row = lambda i: (i, 0)
    const = lambda i: (0, 0)
    return pl.pallas_call(
        _rwkv_back_kernel,
        grid=(n // tm,),
        in_specs=[pl.BlockSpec((tm, d), row), pl.BlockSpec((tm, d), row), pl.BlockSpec((tm, d), row),
                  pl.BlockSpec((1, 1, 3 * d), lambda i: (mod_idx(i), 0, 0)),
                  pl.BlockSpec((d, d), const), pl.BlockSpec((1, d), const)],
        out_specs=pl.BlockSpec((tm, d), row),
        out_shape=jax.ShapeDtypeStruct((n, d), F32),
        compiler_params=_params(1),
        name="rwkv_back",
    )(o, g, x, mod1.reshape(nmod, 1, 3 * d), w_out.astype(BF16), norm_f.reshape(1, d))


def _pair_blockdiag(s):
    b = s.shape[0]
    s = s.reshape(b, 2, RWKV_PAIRS, 2, RWKV_HEAD, RWKV_HEAD)
    z = jnp.zeros_like(s[:, :, :, 0])
    top = jnp.concatenate([s[:, :, :, 0], z], axis=-1)
    bot = jnp.concatenate([z, s[:, :, :, 1]], axis=-1)
    return jnp.concatenate([top, bot], axis=-2)


def kernel(x_prompt, x_sample, state_gla, state_rwkv, c, c_ctx, w_mod, b_mod, norm_w, gla_w_in, gla_w_a1,
           gla_w_a2, gla_b_a, gla_norm, gla_w_out, rwkv_mu, rwkv_w_rkvg, rwkv_w0, rwkv_w1, rwkv_w2, rwkv_a0,
           rwkv_a1, rwkv_a2, rwkv_k_k, rwkv_k_a, rwkv_r_k, rwkv_lnx_w, rwkv_lnx_b, rwkv_w_out, norm_f):
    d = D_MODEL
    bp, tp, _ = x_prompt.shape
    bs, ts, _ = x_sample.shape
    assert tp % CHUNK == 0 and ts % CHUNK == 0 and ts % GRID_W == 0

    nrows = -(-(1 + bs) // 8) * 8
    cond = jnp.zeros((nrows, d), F32).at[0].set(c_ctx).at[1:1 + bs].set(c)
    cond = cond * (1.0 / (1.0 + jnp.exp(-cond)))
    mod0 = _matmul_bias(cond, w_mod[0], b_mod[0].reshape(1, 3 * d))
    mod1 = _matmul_bias(cond, w_mod[1], b_mod[1].reshape(1, 3 * d))

    def trunk(x3, mod_idx, grid_w, gla_s0, rwkv_s0, rwkv_bb, emit_state):
        b, t, _ = x3.shape
        n = b * t
        tm = min(TOKEN_TILE, t)
        x = x3.reshape(n, d)
        seq = lambda arr: arr.reshape(b, t, arr.shape[-1])
        q, k, v, g, lg = _gla_front(x, mod0, mod_idx, tm, norm_w[0], gla_w_in[0], gla_w_a1[0], gla_w_a2[0], gla_b_a[0])
        o, *gla_s = _gla_scan(seq(q), seq(k), seq(v), seq(lg), gla_s0, GLA_HEADS, emit_state)
        x1, h1 = _gla_back(o.reshape(n, d), g, x, mod0, mod1, mod_idx, tm, gla_norm[0], gla_w_out[0], norm_w[1])
        r, k, v, g, lw, a = _rwkv_front(h1, tm, t, grid_w, rwkv_mu[0], rwkv_w_rkvg[0], rwkv_w0[0], rwkv_w1[0], rwkv_w2[0],
                                        rwkv_a0[0], rwkv_a1[0], rwkv_a2[0])
        o, *rwkv_s = _rwkv_scan(seq(r), seq(k), seq(v), seq(lw), seq(a), rwkv_k_k[0], rwkv_k_a[0], rwkv_r_k[0],
                                rwkv_lnx_w[0], rwkv_lnx_b[0], rwkv_s0, rwkv_bb, emit_state)
        y = _rwkv_back(o.reshape(n, d), g, x1, mod1, mod_idx, tm, rwkv_w_out[0], norm_f)
        return y.reshape(b, t, d), gla_s, rwkv_s

    y_prompt, (gla_s,), (rwkv_s,) = trunk(x_prompt, lambda i: 0, None, None, None, min(bp, 4), True)

    tiles_per_seq = ts // min(TOKEN_TILE, ts)
    y_sample, _, _ = trunk(x_sample, lambda i: 1 + i // tiles_per_seq, GRID_W, state_gla[:, 0],
                           _pair_blockdiag(state_rwkv[:, 0]), min(bs, 2), False)
    return (y_prompt, y_sample, gla_s[:, None], rwkv_s[:, None])
```

```python
import functools

import jax
import jax.numpy as jnp
from jax import lax
from jax.experimental import pallas as pl
from jax.experimental.pallas import tpu as pltpu

F32 = jnp.float32
BF16 = jnp.bfloat16

D_MODEL = 1024
EPS = 1e-6
GRID_W = 64
GLA_HEADS = 4
GLA_DK = 512
GLA_DV = 1024
GLA_DKH = 128
GLA_DVH = 256
GLA_GATE_RANK = 16
GLA_GATE_NORM = 16.0
RWKV_HEAD = 64
RWKV_HEADS = 16
RWKV_PAIRS = 8
RWKV_RANK = 64
LNX_EPS = 64e-5
RWKV_DECAY_SCALE = 0.6065306597126334
CHUNK = 64
INV_BASE = 8
TOKEN_TILE = 256
PREP_GROUP = 8
LANES = 128
VMEM_LIMIT = 56 * 1024 * 1024


def _dot(a, b):
    return jnp.dot(a.astype(BF16), b.astype(BF16), preferred_element_type=F32)


def _dot_nt(a, b):
    return lax.dot_general(a.astype(BF16), b.astype(BF16), (((1,), (1,)), ((), ())),
                           preferred_element_type=F32)


def _dot_tn(a, b):
    return lax.dot_general(a.astype(BF16), b.astype(BF16), (((0,), (0,)), ((), ())),
                           preferred_element_type=F32)


def _tri_dot(tri, x):
    hi = x.astype(BF16)
    lo = (x - hi.astype(F32)).astype(BF16)
    t = tri.astype(BF16)
    return jnp.dot(t, hi, preferred_element_type=F32) + jnp.dot(t, lo, preferred_element_type=F32)


def _log_sigmoid(z):
    return jnp.minimum(z, 0.0) - jnp.log(1.0 + jnp.exp(-jnp.abs(z)))


def _sigmoid(z):
    return 1.0 / (1.0 + jnp.exp(-z))


def _silu(z):
    return z * _sigmoid(z)


def _rms(x):
    return x * lax.rsqrt(jnp.mean(x * x, axis=-1, keepdims=True) + EPS)


def _params(n_grid_dims):
    return pltpu.CompilerParams(dimension_semantics=("arbitrary",) * n_grid_dims,
                                vmem_limit_bytes=VMEM_LIMIT)


def _mm_kernel(x_ref, w_ref, b_ref, o_ref):
    o_ref[...] = _dot(x_ref[...], w_ref[...]) + b_ref[...]


def _matmul_bias(x, w, b):
    m, k = x.shape
    n = w.shape[1]
    return pl.pallas_call(
        _mm_kernel,
        grid=(1,),
        in_specs=[pl.BlockSpec((m, k), lambda i: (0, 0)),
                  pl.BlockSpec((k, n), lambda i: (0, 0)),
                  pl.BlockSpec((1, n), lambda i: (0, 0))],
        out_specs=pl.BlockSpec((m, n), lambda i: (0, 0)),
        out_shape=jax.ShapeDtypeStruct((m, n), F32),
        compiler_params=_params(1),
        name="mod_matmul",
    )(x, w.astype(BF16), b)


def _gla_front_kernel(x_ref, mod_ref, nw_ref, win_ref, wa1_ref, wa2_ref, ba_ref,
                      q_ref, k_ref, v_ref, g_ref, lg_ref):
    d = D_MODEL
    mod = mod_ref[0]
    shift, scale = mod[:, :d], mod[:, d:2 * d]
    h = _rms(x_ref[...]) * nw_ref[...] * (1.0 + scale) + shift
    hb = h.astype(BF16)
    proj = lambda lo, hi: jnp.dot(hb, win_ref[:, lo:hi], preferred_element_type=F32)
    q_ref[...] = (proj(0, GLA_DK) * (GLA_DKH ** -0.5)).astype(BF16)
    k_ref[...] = proj(GLA_DK, 2 * GLA_DK).astype(BF16)
    v_ref[...] = proj(2 * GLA_DK, 2 * GLA_DK + GLA_DV).astype(BF16)
    g_ref[...] = proj(2 * GLA_DK + GLA_DV, 2 * GLA_DK + 2 * GLA_DV).astype(BF16)
    t = jnp.dot(hb, wa1_ref[...], preferred_element_type=F32)
    z = _dot(t, wa2_ref[...]) + ba_ref[...]
    lg_ref[...] = _log_sigmoid(z) * (1.0 / GLA_GATE_NORM)


def _gla_front(x, mod, mod_idx, tm, norm_w, w_in, w_a1, w_a2, b_a):
    n, d = x.shape
    nmod = mod.shape[0]
    wa1 = jnp.zeros((d, LANES), F32).at[:, :GLA_GATE_RANK].set(w_a1[0]).at[:, GLA_GATE_RANK:2 * GLA_GATE_RANK].set(w_a1[1])
    wa2 = jnp.zeros((LANES, 2 * GLA_DK), F32).at[:GLA_GATE_RANK, :GLA_DK].set(w_a2[0])
    wa2 = wa2.at[GLA_GATE_RANK:2 * GLA_GATE_RANK, GLA_DK:].set(w_a2[1])
    ba = b_a.reshape(1, 2 * GLA_DK)
    row = lambda i: (i, 0)
    const = lambda i: (0, 0)
    n_in = w_in.shape[1]
    outs = pl.pallas_call(
        _gla_front_kernel,
        grid=(n // tm,),
        in_specs=[pl.BlockSpec((tm, d), row),
                  pl.BlockSpec((1, 1, 3 * d), lambda i: (mod_idx(i), 0, 0)),
                  pl.BlockSpec((1, d), const),
                  pl.BlockSpec((d, n_in), const),
                  pl.BlockSpec((d, LANES), const),
                  pl.BlockSpec((LANES, 2 * GLA_DK), const),
                  pl.BlockSpec((1, 2 * GLA_DK), const)],
        out_specs=[pl.BlockSpec((tm, GLA_DK), row), pl.BlockSpec((tm, GLA_DK), row),
                   pl.BlockSpec((tm, GLA_DV), row), pl.BlockSpec((tm, GLA_DV), row),
                   pl.BlockSpec((tm, 2 * GLA_DK), row)],
        out_shape=[jax.ShapeDtypeStruct((n, GLA_DK), BF16), jax.ShapeDtypeStruct((n, GLA_DK), BF16),
                   jax.ShapeDtypeStruct((n, GLA_DV), BF16), jax.ShapeDtypeStruct((n, GLA_DV), BF16),
                   jax.ShapeDtypeStruct((n, 2 * GLA_DK), F32)],
        compiler_params=_params(1),
        name="gla_front",
    )(x, mod.reshape(nmod, 1, 3 * d), norm_w.reshape(1, d), w_in.astype(BF16), wa1.astype(BF16),
      wa2.astype(BF16), ba)
    return outs


def _gla_chunks(insts):
    c = insts[0][0].shape[0]
    row = lax.broadcasted_iota(jnp.int32, (c, c), 0)
    col = lax.broadcasted_iota(jnp.int32, (c, c), 1)
    keep = [(col >= row) if x[5] else (col <= row) for x in insts]
    b = [_tri_dot(kp.astype(F32), x[3]) for kp, x in zip(keep, insts)]
    b_end = [bb[0:1] if x[5] else bb[c - 1:c] for bb, x in zip(b, insts)]
    qb = [x[0] * jnp.exp(bb) for x, bb in zip(insts, b)]
    kb = [x[1] * jnp.exp(-bb) for x, bb in zip(insts, b)]
    kd = [x[1] * jnp.exp(be - bb) for x, bb, be in zip(insts, b, b_end)]
    scores = [jnp.where(kp, _dot_nt(x, y), 0.0) for kp, x, y in zip(keep, qb, kb)]
    o_inter = [_dot_nt(x, ins[4]) for x, ins in zip(qb, insts)]
    upd = [_dot_tn(ins[2], x) for ins, x in zip(insts, kd)]
    o = [_dot(sc, ins[2]) + oi for sc, ins, oi in zip(scores, insts, o_inter)]
    st = [ins[4] * jnp.exp(be) + u for ins, be, u in zip(insts, b_end, upd)]
    return list(zip(o, st))


def _gla_scan_kernel(*refs, has_s0, emit_state):
    q_ref, k_ref, v_ref, lf_ref, lb_ref = refs[:5]
    s0_ref = refs[5] if has_s0 else None
    o_ref = refs[5 + has_s0]
    s_ref = refs[6 + has_s0] if emit_state else None
    acc_ref = refs[-1]
    t = q_ref.shape[1]
    hb = q_ref.shape[2] // GLA_DKH
    nc = t // CHUNK
    half = nc // 2
    kh = lambda h: slice(h * GLA_DKH, (h + 1) * GLA_DKH)
    vh = lambda h: slice(h * GLA_DVH, (h + 1) * GLA_DVH)

    def both(accumulate):
        def body(i, carry):
            sls = [pl.ds(pl.multiple_of(ci * CHUNK, CHUNK), CHUNK) for ci in (i, nc - 1 - i)]
            insts = [(q_ref[0, sls[d], kh(h)].astype(F32), k_ref[0, sls[d], kh(h)].astype(F32), v_ref[0, sls[d], vh(h)],
                      (lf_ref, lb_ref)[d][0, sls[d], kh(h)], carry[d * hb + h], d == 1)
                     for d in range(2) for h in range(hb)]
            res = _gla_chunks(insts)
            for d in range(2):
                for h in range(hb):
                    o = res[d * hb + h][0]
                    if accumulate:
                        o_ref[0, sls[d], vh(h)] = (acc_ref[sls[d], vh(h)] + o).astype(BF16)
                    else:
                        acc_ref[sls[d], vh(h)] = o
            return tuple(st for _, st in res)
        return body

    if has_s0:
        init = tuple(s0_ref[0, d, h].T for d in range(2) for h in range(hb))
    else:
        init = tuple(jnp.zeros((GLA_DVH, GLA_DKH), F32) for _ in range(2 * hb))
    carry = lax.fori_loop(0, half, both(False), init)
    carry = lax.fori_loop(half, nc, both(True), carry)
    if emit_state:
        for d in range(2):
            for h in range(hb):
                s_ref[0, d, h] = carry[d * hb + h].T


def _gla_scan(q, k, v, lg, s0, hb, emit_state):
    bsz, t, _ = q.shape
    ng = GLA_HEADS // hb
    hd = lambda b, h: (b, 0, h)
    kblk = pl.BlockSpec((1, t, hb * GLA_DKH), hd)
    vblk = pl.BlockSpec((1, t, hb * GLA_DVH), hd)
    sblk = pl.BlockSpec((1, 2, hb, GLA_DKH, GLA_DVH), lambda b, h: (b, 0, h, 0, 0))
    s_shape = jax.ShapeDtypeStruct((bsz, 2, GLA_HEADS, GLA_DKH, GLA_DVH), F32)
    has_s0 = s0 is not None
    return pl.pallas_call(
        functools.partial(_gla_scan_kernel, has_s0=has_s0, emit_state=emit_state),
        grid=(bsz, ng),
        in_specs=[kblk, kblk, vblk, kblk, pl.BlockSpec((1, t, hb * GLA_DKH), lambda b, h: (b, 0, ng + h))]
        + [sblk] * has_s0,
        out_specs=[vblk] + [sblk] * emit_state,
        out_shape=[jax.ShapeDtypeStruct((bsz, t, GLA_DV), BF16)] + [s_shape] * emit_state,
        scratch_shapes=[pltpu.VMEM((t, hb * GLA_DVH), F32)],
        compiler_params=_params(2),
        name="gla_scan",
    )(q, k, v, lg, lg, *([s0] * has_s0))


def _gla_back_kernel(o_ref, g_ref, x_ref, mod0_ref, mod1_ref, gn_ref, wout_ref, nw1_ref, x1_ref, h1_ref):
    d = D_MODEL
    o = o_ref[...].astype(F32)
    gn = gn_ref[...]
    parts = [_rms(o[:, h * GLA_DVH:(h + 1) * GLA_DVH]) * gn for h in range(GLA_HEADS)]
    on = jnp.concatenate(parts, axis=-1) * _silu(g_ref[...].astype(F32))
    out = _dot(on, wout_ref[...])
    gate = mod0_ref[0][:, 2 * d:]
    x1 = x_ref[...] + gate * out
    x1_ref[...] = x1
    mod1 = mod1_ref[0]
    h1_ref[...] = (_rms(x1) * nw1_ref[...] * (1.0 + mod1[:, d:2 * d]) + mod1[:, :d]).astype(BF16)


def _gla_back(o, g, x, mod0, mod1, mod_idx, tm, gla_norm, w_out, norm_w1):
    n, d = x.shape
    nmod = mod0.shape[0]
    row = lambda i: (i, 0)
    const = lambda i: (0, 0)
    modspec = pl.BlockSpec((1, 1, 3 * d), lambda i: (mod_idx(i), 0, 0))
    return pl.pallas_call(
        _gla_back_kernel,
        grid=(n // tm,),
        in_specs=[pl.BlockSpec((tm, d), row), pl.BlockSpec((tm, d), row), pl.BlockSpec((tm, d), row),
                  modspec, modspec,
                  pl.BlockSpec((1, GLA_DVH), const), pl.BlockSpec((d, d), const), pl.BlockSpec((1, d), const)],
        out_specs=[pl.BlockSpec((tm, d), row), pl.BlockSpec((tm, d), row)],
        out_shape=[jax.ShapeDtypeStruct((n, d), F32), jax.ShapeDtypeStruct((n, d), BF16)],
        compiler_params=_params(1),
        name="gla_back",
    )(o, g, x, mod0.reshape(nmod, 1, 3 * d), mod1.reshape(nmod, 1, 3 * d), gla_norm.reshape(1, GLA_DVH),
      w_out.astype(BF16), norm_w1.reshape(1, d))


def _token_shift(h, up_ref, dn_ref, seq_len, grid_w):
    tm = h.shape[0]
    pos = lax.broadcasted_iota(jnp.int32, (tm, 1), 0) + pl.program_id(0) * tm
    prev = pltpu.roll(h, 1, 0)
    nxt = pltpu.roll(h, tm - 1, 0)
    t = pos % seq_len
    if grid_w is None:
        return 0.5 * (jnp.where(t == 0, 0.0, prev) + jnp.where(t == seq_len - 1, 0.0, nxt))
    col = pos % grid_w
    up = jnp.concatenate([up_ref[...].astype(F32), h[:tm - grid_w]], axis=0)
    dn = jnp.concatenate([h[grid_w:], dn_ref[...].astype(F32)], axis=0)
    up = jnp.where(t < grid_w, 0.0, up)
    dn = jnp.where(t >= seq_len - grid_w, 0.0, dn)
    left = jnp.where(col == 0, 0.0, prev)
    right = jnp.where(col == grid_w - 1, 0.0, nxt)
    return 0.25 * (up + dn + left + right)


def _rwkv_front_kernel(*refs, seq_len, grid_w):
    n_h = 1 if grid_w is None else 3
    h_ref, up_ref, dn_ref = (tuple(refs[:n_h]) + (None, None))[:3]
    mu_ref, wrkvg_ref, w1_ref, a1_ref, r_ref, k_ref, v_ref, g_ref, tw_ref, ta_ref = refs[n_h:]
    h = h_ref[...].astype(F32)
    dh = _token_shift(h, up_ref, dn_ref, seq_len, grid_w) - h
    mix = lambda p: (h + dh * mu_ref[p:p + 1, :]).astype(BF16)
    r_ref[...] = jnp.dot(mix(0), wrkvg_ref[0], preferred_element_type=F32).astype(BF16)
    k_ref[...] = jnp.dot(mix(1), wrkvg_ref[1], preferred_element_type=F32).astype(BF16)
    v_ref[...] = jnp.dot(mix(2), wrkvg_ref[2], preferred_element_type=F32).astype(BF16)
    g_ref[...] = jnp.dot(mix(3), wrkvg_ref[3], preferred_element_type=F32).astype(BF16)
    tw_ref[...] = jnp.tanh(jnp.dot(mix(4), w1_ref[...], preferred_element_type=F32)).astype(BF16)
    ta_ref[...] = jnp.dot(mix(5), a1_ref[...], preferred_element_type=F32).astype(BF16)


def _rwkv_front(h, tm, seq_len, grid_w, mu, w_rkvg, w1, a1):
    n, d = h.shape
    rk = RWKV_RANK
    cat1 = lambda w: jnp.concatenate([w[0], w[1]], axis=1)
    row = lambda i: (i, 0)
    const = lambda i: (0, 0)
    if grid_w is None:
        assert tm % seq_len == 0
        h_specs, h_args = [pl.BlockSpec((tm, d), row)], [h]
    else:
        assert tm % grid_w == 0 and seq_len % tm == 0
        rows_per_tile, last = tm // grid_w, n // grid_w - 1
        h_specs = [pl.BlockSpec((tm, d), row),
                   pl.BlockSpec((grid_w, d), lambda i: (jnp.maximum(i * rows_per_tile - 1, 0), 0)),
                   pl.BlockSpec((grid_w, d), lambda i: (jnp.minimum((i + 1) * rows_per_tile, last), 0))]
        h_args = [h, h, h]
    return pl.pallas_call(
        functools.partial(_rwkv_front_kernel, seq_len=seq_len, grid_w=grid_w),
        grid=(n // tm,),
        in_specs=h_specs + [
            pl.BlockSpec((8, d), const),
            pl.BlockSpec((4, d, d), lambda i: (0, 0, 0)),
            pl.BlockSpec((d, 2 * rk), const), pl.BlockSpec((d, 2 * rk), const)],
        out_specs=[pl.BlockSpec((tm, d), row)] * 4 + [pl.BlockSpec((tm, 2 * rk), row)] * 2,
        out_shape=[jax.ShapeDtypeStruct((n, d), BF16)] * 4 + [jax.ShapeDtypeStruct((n, 2 * rk), BF16)] * 2,
        compiler_params=_params(1),
        name="rwkv_front",
    )(*h_args, jnp.zeros((8, d), F32).at[:6].set(mu), w_rkvg.astype(BF16),
      cat1(w1).astype(BF16), cat1(a1).astype(BF16))


def _seg_sum(x, head0):
    s0 = jnp.sum(jnp.where(head0, x, 0.0), axis=-1, keepdims=True)
    s1 = jnp.sum(jnp.where(head0, 0.0, x), axis=-1, keepdims=True)
    return jnp.where(head0, s0, s1)


def _stack(x, head0):
    return jnp.concatenate([jnp.where(head0, x, 0.0), jnp.where(head0, 0.0, x)], axis=0)


def _rwkv_chunk_terms(insts):
    n = len(insts)
    c = insts[0][0].shape[0]
    lane = lax.broadcasted_iota(jnp.int32, (c, LANES), 1)
    head0 = lane < RWKV_HEAD
    row = lax.broadcasted_iota(jnp.int32, (c, c), 0)
    col = lax.broadcasted_iota(jnp.int32, (c, c), 1)
    ti = lax.broadcasted_iota(jnp.int32, (c, LANES), 0)
    si = lane & (c - 1)
    prow = lax.broadcasted_iota(jnp.int32, (LANES, LANES), 0)
    pcol = lax.broadcasted_iota(jnp.int32, (LANES, LANES), 1)
    same_head = (prow >= RWKV_HEAD) == (pcol >= RWKV_HEAD)
    each = lambda f, *ls: [f(*xs) for xs in zip(*ls)]
    rev = [x[8] for x in insts]
    stk = lambda x: _stack(x, head0).astype(BF16)

    cum = [_tri_dot(((col >= row) if x[8] else (col <= row)).astype(F32), x[3]) for x in insts]
    tot = [cm[0:1] if rv else cm[c - 1:c] for cm, rv in zip(cum, rev)]

    a_t, r_t, b_ts, k_ts, b_h, k_h, v_s, vv, bonus = [], [], [], [], [], [], [], [], []
    for (r, k, v, lw, a, k_k, k_a, r_k, _), cm, tt in zip(insts, cum, tot):
        kk = k * k_k
        kk = kk / jnp.maximum(jnp.sqrt(_seg_sum(kk * kk, head0)), 1e-12)
        kd = k * (1.0 + (a - 1.0) * k_a)
        bv = kk * a
        bonus.append(_seg_sum(r * kd * r_k, head0) * v)
        e_neg = jnp.exp(-cm)
        e_tail = jnp.exp(tt - cm)
        a_t.append(-kk * jnp.exp(cm - lw))
        r_t.append(r * jnp.exp(cm))
        b_ts.append(stk(bv * e_neg))
        k_ts.append(stk(kd * e_neg))
        b_h.append(bv * e_tail)
        k_h.append(kd * e_tail)
        v_s.append(stk(v))
        vv.append(v)

    strict = [(si > ti) if rv else (si < ti) for rv in rev]
    incl = [(si >= ti) if rv else (si <= ti) for rv in rev]
    ar = each(lambda x, y: jnp.concatenate([x, y], axis=0).astype(BF16), a_t, r_t)
    g_b = each(_dot_nt, ar, b_ts)
    g_k = each(_dot_nt, ar, k_ts)
    n_ab = each(lambda m_, x: jnp.where(m_, x[:c], 0.0), strict, g_b)
    l_ak = each(lambda m_, x: jnp.where(m_, x[:c], 0.0), strict, g_k)
    p_rb = each(lambda m_, x: jnp.where(m_, x[c:], 0.0), incl, g_b)
    p_rk = each(lambda m_, x: jnp.where(m_, x[c:], 0.0), incl, g_k)

    same_blk = lambda h: (ti // h) == (si // h)
    n_d = each(lambda x: jnp.where(same_blk(INV_BASE), x, 0.0), n_ab)
    steps = max(1, (INV_BASE - 1).bit_length())
    q = n_d
    tinv = n_d
    q = each(lambda x: _dot(x, stk(x)), q) if steps > 1 else q
    for j in range(1, steps):
        last = j == steps - 1
        qs = each(stk, q)
        if last:
            z = each(_dot, tinv, qs)
            tinv = each(lambda t_, x, y: t_ + x + y, tinv, q, z)
        else:
            both = each(lambda x, t_, w_: _dot(jnp.concatenate([x, t_], axis=0), w_), q, tinv, qs)
            tinv = each(lambda t_, x, y: t_ + x + y[c:], tinv, q, both)
            q = each(lambda y: y[:c], both)
    h = INV_BASE
    while h < c:
        link = same_blk(2 * h) & jnp.logical_not(same_blk(h))
        n_off = each(lambda x: jnp.where(link, x, 0.0), n_ab)
        y = each(lambda t_, x: x + _dot(t_, stk(x)), tinv, n_off)
        tinv = each(lambda t_, x: t_ + x + _dot(x, stk(t_)), tinv, y)
        h *= 2
    a_bar = each(lambda t_, x: x + _dot(t_, stk(x)), tinv, a_t)
    lv = each(lambda l_, p, y: _dot(jnp.concatenate([l_, p], axis=0), y), l_ak, p_rk, v_s)
    w = [x[:c] for x in lv]
    u0 = each(lambda t_, x: x + _dot(t_, stk(x)), tinv, w)
    r_bar = each(lambda x, p, y: x + _dot(p, stk(y)), r_t, p_rb, a_bar)
    y0 = each(lambda p, x, z: _dot(p, stk(x)) + z[c:], p_rb, u0, lv)
    zero = jnp.zeros((c, LANES), F32)
    ms = each(lambda ab, u, v_, bh, kh_: _dot_tn(
        jnp.concatenate([jnp.concatenate([ab, u], axis=1), jnp.concatenate([zero, v_], axis=1)], axis=0),
        jnp.concatenate([bh, kh_], axis=0)), a_bar, u0, vv, b_h, k_h)
    m = [jnp.where(same_head, x[:LANES], 0.0) for x in ms]
    s0p = [jnp.where(same_head, x[LANES:], 0.0) for x in ms]
    return [(r_bar[i], y0[i], m[i], s0p[i], jnp.exp(tot[i]), bonus[i]) for i in range(n)]


def _rwkv_scan_kernel(*refs, has_s0, emit_state):
    (r_ref, k_ref, v_ref, tw_ref, ta_ref, w2_ref, w0_ref, a2_ref, a0_ref,
     kkf_ref, kkb_ref, kaf_ref, kab_ref, rkf_ref, rkb_ref, lnw_ref, lnb_ref) = refs[:17]
    s0_ref = refs[17] if has_s0 else None
    o_ref = refs[17 + has_s0]
    s_ref = refs[18 + has_s0] if emit_state else None
    rbar_ref, y0_ref, m_ref, s0p_ref, dec_ref, bonus_ref, yf_ref, yb_ref = refs[18 + has_s0 + emit_state:]
    bb, t = r_ref.shape[0], r_ref.shape[1]
    nc = t // CHUNK
    dirs = ((kkf_ref, kaf_ref, rkf_ref, False), (kkb_ref, kab_ref, rkb_ref, True))
    group = min(bb * nc, PREP_GROUP)
    assert (bb * nc) % group == 0
    chunk_rows = lambda ci: pl.ds(pl.multiple_of(ci * CHUNK, CHUNK), CHUNK)

    def prep(gi, carry):
        insts, where = [], []
        at = [((gi * group + j) // nc, chunk_rows((gi * group + j) % nc)) for j in range(group)]
        zw = [jnp.dot(tw_ref[e, sl, :], w2_ref[0], preferred_element_type=F32) for e, sl in at]
        za = [jnp.dot(ta_ref[e, sl, :], a2_ref[0], preferred_element_type=F32) for e, sl in at]
        lw2 = [-RWKV_DECAY_SCALE * _sigmoid(z + w0_ref[0]) for z in zw]
        a2 = [_sigmoid(z + a0_ref[0]) for z in za]
        for j, (e, sl) in enumerate(at):
            r, k, v = (ref[e, sl, :].astype(F32) for ref in (r_ref, k_ref, v_ref))
            for d, (kk_ref, ka_ref, rk_ref, reverse) in enumerate(dirs):
                lanes = slice(d * LANES, (d + 1) * LANES)
                insts.append((r, k, v, lw2[j][:, lanes], a2[j][:, lanes], kk_ref[...], ka_ref[...], rk_ref[...], reverse))
                where.append((d, gi * group + j, e, sl))
        terms = _rwkv_chunk_terms(insts)
        for (d, fi, e, sl), (r_bar, y0, m, s0p, dec, bon) in zip(where, terms):
            rbar_ref[d, fi] = r_bar.astype(BF16)
            y0_ref[d, fi] = y0
            m_ref[d, fi] = m.astype(BF16)
            s0p_ref[d, fi] = s0p
            dec_ref[d, fi] = jnp.broadcast_to(dec, (8, LANES))
        for j in range(group):
            _, _, e, sl = where[2 * j]
            bonus_ref[e, sl, :] = terms[2 * j][5] + terms[2 * j + 1][5]
        return carry

    lax.fori_loop(0, bb * nc // group, prep, 0)

    def seq(i, carry):
        cis = (i, nc - 1 - i)
        idx = [(d, e) for d in range(2) for e in range(bb)]
        sb16 = [s.astype(BF16) for s in carry]
        sm = [jnp.dot(sb16[n], m_ref[d, e * nc + cis[d]], preferred_element_type=F32) for n, (d, e) in enumerate(idx)]
        ys = [_dot_nt(rbar_ref[d, e * nc + cis[d]], sb16[n]) + y0_ref[d, e * nc + cis[d]] for n, (d, e) in enumerate(idx)]
        for n, (d, e) in enumerate(idx):
            (yf_ref, yb_ref)[d][e, chunk_rows(cis[d]), :] = ys[n]
        return tuple(carry[n] * dec_ref[d, e * nc + cis[d]][0:1] + sm[n] + s0p_ref[d, e * nc + cis[d]]
                     for n, (d, e) in enumerate(idx))

    if has_s0:
        init = tuple(s0_ref[e, d, 0] for d in range(2) for e in range(bb))
    else:
        init = tuple(jnp.zeros((LANES, LANES), F32) for _ in range(2 * bb))
    final = lax.fori_loop(0, nc, seq, init)
    if emit_state:
        for d in range(2):
            for e in range(bb):
                s = final[d * bb + e]
                s_ref[e, d, 0] = s[:RWKV_HEAD, :RWKV_HEAD]
                s_ref[e, d, 1] = pltpu.roll(s, RWKV_HEAD, 1)[RWKV_HEAD:, :RWKV_HEAD]

    prow = lax.broadcasted_iota(jnp.int32, (LANES, LANES), 0)
    pcol = lax.broadcasted_iota(jnp.int32, (LANES, LANES), 1)
    head_ones = ((prow >= RWKV_HEAD) == (pcol >= RWKV_HEAD)).astype(BF16)
    inv_n = 1.0 / RWKV_HEAD

    def head_mean(x):
        hi = x.astype(BF16)
        lo = (x - hi.astype(F32)).astype(BF16)
        both = jnp.dot(jnp.concatenate([hi, lo], axis=0), head_ones, preferred_element_type=F32)
        return (both[:x.shape[0]] + both[x.shape[0]:]) * inv_n

    def finish(gi, carry):
        at = [((gi * group + j) // nc, chunk_rows((gi * group + j) % nc)) for j in range(group)]
        y = [yf_ref[e, sl, :] + yb_ref[e, sl, :] for e, sl in at]
        yc = [x - head_mean(x) for x in y]
        var = [head_mean(x * x) for x in yc]
        for (e, sl), x, vr in zip(at, yc, var):
            o_ref[e, sl, :] = (x * lax.rsqrt(vr + LNX_EPS) * lnw_ref[...] + lnb_ref[...]
                               + bonus_ref[e, sl, :]).astype(BF16)
        return carry

    lax.fori_loop(0, bb * nc // group, finish, 0)


def _rwkv_scan(r, k, v, tw, ta, w0, w2, a0, a2, k_k, k_a, r_k, lnx_w, lnx_b, s0, bb, emit_state):
    bsz, t, d = r.shape
    nc = t // CHUNK
    seq = lambda b, p: (b, 0, p)
    seq_b = lambda b, p: (b, 0, RWKV_PAIRS + p)
    par = lambda b, p: (0, p)
    par_b = lambda b, p: (0, RWKV_PAIRS + p)
    sblk = pl.BlockSpec((bb, t, LANES), seq)
    sblk_b = pl.BlockSpec((bb, t, LANES), seq_b)
    pblk = pl.BlockSpec((1, LANES), par)
    pblk_b = pl.BlockSpec((1, LANES), par_b)
    s0blk = pl.BlockSpec((bb, 2, 1, LANES, LANES), lambda b, p: (b, 0, p, 0, 0))
    s_blk = pl.BlockSpec((bb, 2, 2, RWKV_HEAD, RWKV_HEAD), lambda b, p: (b, 0, p, 0, 0))
    s_shape = jax.ShapeDtypeStruct((bsz, 2, RWKV_HEADS, RWKV_HEAD, RWKV_HEAD), F32)
    has_s0 = s0 is not None
    k_k, k_a, r_k = (p.reshape(1, 2 * d) for p in (k_k, k_a, r_k))
    rk = RWKV_RANK

    def pair_w(w):
        wp = w.reshape(2, rk, RWKV_PAIRS, LANES)
        z = jnp.zeros_like(wp[0])
        top = jnp.concatenate([wp[0], z], axis=-1)
        bot = jnp.concatenate([z, wp[1]], axis=-1)
        return jnp.transpose(jnp.concatenate([top, bot], axis=0), (1, 0, 2)).astype(BF16)

    def pair_b(b_):
        return jnp.transpose(b_.reshape(2, RWKV_PAIRS, 1, LANES), (1, 2, 0, 3)).reshape(RWKV_PAIRS, 1, 2 * LANES)

    lblk = pl.BlockSpec((bb, t, 2 * rk), lambda b, p: (b, 0, 0))
    wblk = pl.BlockSpec((1, 2 * rk, 2 * LANES), lambda b, p: (p, 0, 0))
    bblk = pl.BlockSpec((1, 1, 2 * LANES), lambda b, p: (p, 0, 0))
    return pl.pallas_call(
        functools.partial(_rwkv_scan_kernel, has_s0=has_s0, emit_state=emit_state),
        grid=(bsz // bb, RWKV_PAIRS),
        in_specs=[sblk, sblk, sblk, lblk, lblk, wblk, bblk, wblk, bblk,
                  pblk, pblk_b, pblk, pblk_b, pblk, pblk_b, pblk, pblk] + [s0blk] * has_s0,
        out_specs=[sblk] + [s_blk] * emit_state,
        out_shape=[jax.ShapeDtypeStruct((bsz, t, d), BF16)] + [s_shape] * emit_state,
        scratch_shapes=[pltpu.VMEM((2, bb * nc, CHUNK, LANES), BF16), pltpu.VMEM((2, bb * nc, CHUNK, LANES), F32),
                        pltpu.VMEM((2, bb * nc, LANES, LANES), BF16), pltpu.VMEM((2, bb * nc, LANES, LANES), F32),
                        pltpu.VMEM((2, bb * nc, 8, LANES), F32), pltpu.VMEM((bb, t, LANES), F32),
                        pltpu.VMEM((bb, t, LANES), F32), pltpu.VMEM((bb, t, LANES), F32)],
        compiler_params=_params(2),
        name="rwkv_scan",
    )(r, k, v, tw, ta, pair_w(w2), pair_b(w0), pair_w(a2), pair_b(a0), k_k, k_k, k_a, k_a, r_k, r_k,
      lnx_w.reshape(1, d), lnx_b.reshape(1, d), *([s0] * has_s0))


def _rwkv_back_kernel(o_ref, g_ref, x_ref, mod_ref, wout_ref, nf_ref, y_ref):
    out = _dot(o_ref[...].astype(F32) * _silu(g_ref[...].astype(F32)), wout_ref[...])
    x2 = x_ref[...] + mod_ref[0][:, 2 * D_MODEL:] * out
    y_ref[...] = _rms(x2) * nf_ref[...]


def _rwkv_back(o, g, x, mod1, mod_idx, tm, w_out, norm_f):
    n, d = x.shape
    nmod = mod1.shape[0]
    row = lambda i: (i, 0)
    const = lambda i: (0, 0)
    return pl.pallas_call(
        _rwkv_back_kernel,
        grid=(n // tm,),
        in_specs=[pl.BlockSpec((tm, d), row), pl.BlockSpec((tm, d), row), pl.BlockSpec((tm, d), row),
                  pl.BlockSpec((1, 1, 3 * d), lambda i: (mod_idx(i), 0, 0)),
                  pl.BlockSpec((d, d), const), pl.BlockSpec((1, d), const)],
        out_specs=pl.BlockSpec((tm, d), row),
        out_shape=jax.ShapeDtypeStruct((n, d), F32),
        compiler_params=_params(1),
        name="rwkv_back",
    )(o, g, x, mod1.reshape(nmod, 1, 3 * d), w_out.astype(BF16), norm_f.reshape(1, d))


def _pair_blockdiag(s):
    b = s.shape[0]
    s = s.reshape(b, 2, RWKV_PAIRS, 2, RWKV_HEAD, RWKV_HEAD)
    z = jnp.zeros_like(s[:, :, :, 0])
    top = jnp.concatenate([s[:, :, :, 0], z], axis=-1)
    bot = jnp.concatenate([z, s[:, :, :, 1]], axis=-1)
    return jnp.concatenate([top, bot], axis=-2)


def kernel(x_prompt, x_sample, state_gla, state_rwkv, c, c_ctx, w_mod, b_mod, norm_w, gla_w_in, gla_w_a1,
           gla_w_a2, gla_b_a, gla_norm, gla_w_out, rwkv_mu, rwkv_w_rkvg, rwkv_w0, rwkv_w1, rwkv_w2, rwkv_a0,
           rwkv_a1, rwkv_a2, rwkv_k_k, rwkv_k_a, rwkv_r_k, rwkv_lnx_w, rwkv_lnx_b, rwkv_w_out, norm_f):
    d = D_MODEL
    bp, tp, _ = x_prompt.shape
    bs, ts, _ = x_sample.shape
    assert tp % CHUNK == 0 and ts % CHUNK == 0 and ts % GRID_W == 0

    nrows = -(-(1 + bs) // 8) * 8
    cond = jnp.zeros((nrows, d), F32).at[0].set(c_ctx).at[1:1 + bs].set(c)
    cond = cond * (1.0 / (1.0 + jnp.exp(-cond)))
    mod0 = _matmul_bias(cond, w_mod[0], b_mod[0].reshape(1, 3 * d))
    mod1 = _matmul_bias(cond, w_mod[1], b_mod[1].reshape(1, 3 * d))

    def trunk(x3, mod_idx, grid_w, gla_s0, rwkv_s0, rwkv_bb, emit_state):
        b, t, _ = x3.shape
        n = b * t
        tm = min(TOKEN_TILE, t)
        x = x3.reshape(n, d)
        seq = lambda arr: arr.reshape(b, t, arr.shape[-1])
        q, k, v, g, lg = _gla_front(x, mod0, mod_idx, tm, norm_w[0], gla_w_in[0], gla_w_a1[0], gla_w_a2[0], gla_b_a[0])
        o, *gla_s = _gla_scan(seq(q), seq(k), seq(v), seq(lg), gla_s0, GLA_HEADS, emit_state)
        x1, h1 = _gla_back(o.reshape(n, d), g, x, mod0, mod1, mod_idx, tm, gla_norm[0], gla_w_out[0], norm_w[1])
        r, k, v, g, tw, ta = _rwkv_front(h1, tm, t, grid_w, rwkv_mu[0], rwkv_w_rkvg[0], rwkv_w1[0], rwkv_a1[0])
        o, *rwkv_s = _rwkv_scan(seq(r), seq(k), seq(v), seq(tw), seq(ta), rwkv_w0[0], rwkv_w2[0], rwkv_a0[0],
                                rwkv_a2[0], rwkv_k_k[0], rwkv_k_a[0], rwkv_r_k[0], rwkv_lnx_w[0], rwkv_lnx_b[0],
                                rwkv_s0, rwkv_bb, emit_state)
        y = _rwkv_back(o.reshape(n, d), g, x1, mod1, mod_idx, tm, rwkv_w_out[0], norm_f)
        return y.reshape(b, t, d), gla_s, rwkv_s

    y_prompt, (gla_s,), (rwkv_s,) = trunk(x_prompt, lambda i: 0, None, None, None, min(bp, 4), True)

    tiles_per_seq = ts // min(TOKEN_TILE, ts)
    y_sample, _, _ = trunk(x_sample, lambda i: 1 + i // tiles_per_seq, GRID_W, state_gla[:, 0],
                           _pair_blockdiag(state_rwkv[:, 0]), min(bs, 2), False)
    return (y_prompt, y_sample, gla_s[:, None], rwkv_s[:, None])
```

```python
import functools

import jax
import jax.numpy as jnp
from jax import lax
from jax.experimental import pallas as pl
from jax.experimental.pallas import tpu as pltpu

F32 = jnp.float32
BF16 = jnp.bfloat16

D_MODEL = 1024
EPS = 1e-6
GRID_W = 64
GLA_HEADS = 4
GLA_DK = 512
GLA_DV = 1024
GLA_DKH = 128
GLA_DVH = 256
GLA_GATE_RANK = 16
GLA_GATE_NORM = 16.0
RWKV_HEAD = 64
RWKV_HEADS = 16
RWKV_PAIRS = 8
RWKV_RANK = 64
LNX_EPS = 64e-5
RWKV_DECAY_SCALE = 0.6065306597126334
CHUNK = 64
INV_BASE = 8
TOKEN_TILE = 256
PREP_GROUP = 8
LANES = 128
VMEM_LIMIT = 56 * 1024 * 1024


def _dot(a, b):
    return jnp.dot(a.astype(BF16), b.astype(BF16), preferred_element_type=F32)


def _dot_nt(a, b):
    return lax.dot_general(a.astype(BF16), b.astype(BF16), (((1,), (1,)), ((), ())),
                           preferred_element_type=F32)


def _dot_tn(a, b):
    return lax.dot_general(a.astype(BF16), b.astype(BF16), (((0,), (0,)), ((), ())),
                           preferred_element_type=F32)


def _tri_dot(tri, x):
    hi = x.astype(BF16)
    lo = (x - hi.astype(F32)).astype(BF16)
    t = tri.astype(BF16)
    return jnp.dot(t, hi, preferred_element_type=F32) + jnp.dot(t, lo, preferred_element_type=F32)


def _log_sigmoid(z):
    return jnp.minimum(z, 0.0) - jnp.log(1.0 + jnp.exp(-jnp.abs(z)))


def _sigmoid(z):
    return 1.0 / (1.0 + jnp.exp(-z))


def _silu(z):
    return z * _sigmoid(z)


def _rms(x):
    return x * lax.rsqrt(jnp.mean(x * x, axis=-1, keepdims=True) + EPS)


def _params(n_grid_dims):
    return pltpu.CompilerParams(dimension_semantics=("arbitrary",) * n_grid_dims,
                                vmem_limit_bytes=VMEM_LIMIT)


def _mm_kernel(x_ref, w_ref, b_ref, o_ref):
    o_ref[...] = _dot(x_ref[...], w_ref[...]) + b_ref[...]


def _matmul_bias(x, w, b):
    m, k = x.shape
    n = w.shape[1]
    return pl.pallas_call(
        _mm_kernel,
        grid=(1,),
        in_specs=[pl.BlockSpec((m, k), lambda i: (0, 0)),
                  pl.BlockSpec((k, n), lambda i: (0, 0)),
                  pl.BlockSpec((1, n), lambda i: (0, 0))],
        out_specs=pl.BlockSpec((m, n), lambda i: (0, 0)),
        out_shape=jax.ShapeDtypeStruct((m, n), F32),
        compiler_params=_params(1),
        name="mod_matmul",
    )(x, w.astype(BF16), b)


def _gla_front_kernel(x_ref, mod_ref, nw_ref, win_ref, wa1_ref, wa2_ref, ba_ref,
                      q_ref, k_ref, v_ref, g_ref, lg_ref):
    d = D_MODEL
    mod = mod_ref[0]
    shift, scale = mod[:, :d], mod[:, d:2 * d]
    h = _rms(x_ref[...]) * nw_ref[...] * (1.0 + scale) + shift
    hb = h.astype(BF16)
    proj = lambda lo, hi: jnp.dot(hb, win_ref[:, lo:hi], preferred_element_type=F32)
    q_ref[...] = (proj(0, GLA_DK) * (GLA_DKH ** -0.5)).astype(BF16)
    k_ref[...] = proj(GLA_DK, 2 * GLA_DK).astype(BF16)
    v_ref[...] = proj(2 * GLA_DK, 2 * GLA_DK + GLA_DV).astype(BF16)
    g_ref[...] = proj(2 * GLA_DK + GLA_DV, 2 * GLA_DK + 2 * GLA_DV).astype(BF16)
    t = jnp.dot(hb, wa1_ref[...], preferred_element_type=F32)
    z = _dot(t, wa2_ref[...]) + ba_ref[...]
    lg_ref[...] = _log_sigmoid(z) * (1.0 / GLA_GATE_NORM)


def _gla_front(x, mod, mod_idx, tm, norm_w, w_in, w_a1, w_a2, b_a):
    n, d = x.shape
    nmod = mod.shape[0]
    wa1 = jnp.zeros((d, LANES), F32).at[:, :GLA_GATE_RANK].set(w_a1[0]).at[:, GLA_GATE_RANK:2 * GLA_GATE_RANK].set(w_a1[1])
    wa2 = jnp.zeros((LANES, 2 * GLA_DK), F32).at[:GLA_GATE_RANK, :GLA_DK].set(w_a2[0])
    wa2 = wa2.at[GLA_GATE_RANK:2 * GLA_GATE_RANK, GLA_DK:].set(w_a2[1])
    ba = b_a.reshape(1, 2 * GLA_DK)
    row = lambda i: (i, 0)
    const = lambda i: (0, 0)
    n_in = w_in.shape[1]
    outs = pl.pallas_call(
        _gla_front_kernel,
        grid=(n // tm,),
        in_specs=[pl.BlockSpec((tm, d), row),
                  pl.BlockSpec((1, 1, 3 * d), lambda i: (mod_idx(i), 0, 0)),
                  pl.BlockSpec((1, d), const),
                  pl.BlockSpec((d, n_in), const),
                  pl.BlockSpec((d, LANES), const),
                  pl.BlockSpec((LANES, 2 * GLA_DK), const),
                  pl.BlockSpec((1, 2 * GLA_DK), const)],
        out_specs=[pl.BlockSpec((tm, GLA_DK), row), pl.BlockSpec((tm, GLA_DK), row),
                   pl.BlockSpec((tm, GLA_DV), row), pl.BlockSpec((tm, GLA_DV), row),
                   pl.BlockSpec((tm, 2 * GLA_DK), row)],
        out_shape=[jax.ShapeDtypeStruct((n, GLA_DK), BF16), jax.ShapeDtypeStruct((n, GLA_DK), BF16),
                   jax.ShapeDtypeStruct((n, GLA_DV), BF16), jax.ShapeDtypeStruct((n, GLA_DV), BF16),
                   jax.ShapeDtypeStruct((n, 2 * GLA_DK), F32)],
        compiler_params=_params(1),
        name="gla_front",
    )(x, mod.reshape(nmod, 1, 3 * d), norm_w.reshape(1, d), w_in.astype(BF16), wa1.astype(BF16),
      wa2.astype(BF16), ba)
    return outs


def _gla_chunks(insts):
    c = insts[0][0].shape[0]
    row = lax.broadcasted_iota(jnp.int32, (c, c), 0)
    col = lax.broadcasted_iota(jnp.int32, (c, c), 1)
    keep = [(col >= row) if x[5] else (col <= row) for x in insts]
    b = [_tri_dot(kp.astype(F32), x[3]) for kp, x in zip(keep, insts)]
    b_end = [bb[0:1] if x[5] else bb[c - 1:c] for bb, x in zip(b, insts)]
    qb = [x[0] * jnp.exp(bb) for x, bb in zip(insts, b)]
    kb = [x[1] * jnp.exp(-bb) for x, bb in zip(insts, b)]
    kd = [x[1] * jnp.exp(be - bb) for x, bb, be in zip(insts, b, b_end)]
    scores = [jnp.where(kp, _dot_nt(x, y), 0.0) for kp, x, y in zip(keep, qb, kb)]
    o_inter = [_dot_nt(x, ins[4]) for x, ins in zip(qb, insts)]
    upd = [_dot_tn(ins[2], x) for ins, x in zip(insts, kd)]
    o = [_dot(sc, ins[2]) + oi for sc, ins, oi in zip(scores, insts, o_inter)]
    st = [ins[4] * jnp.exp(be) + u for ins, be, u in zip(insts, b_end, upd)]
    return list(zip(o, st))


def _gla_scan_kernel(*refs, has_s0, emit_state):
    q_ref, k_ref, v_ref, lf_ref, lb_ref = refs[:5]
    s0_ref = refs[5] if has_s0 else None
    o_ref = refs[5 + has_s0]
    s_ref = refs[6 + has_s0] if emit_state else None
    acc_ref = refs[-1]
    t = q_ref.shape[1]
    hb = q_ref.shape[2] // GLA_DKH
    nc = t // CHUNK
    half = nc // 2
    kh = lambda h: slice(h * GLA_DKH, (h + 1) * GLA_DKH)
    vh = lambda h: slice(h * GLA_DVH, (h + 1) * GLA_DVH)

    def both(accumulate):
        def body(i, carry):
            sls = [pl.ds(pl.multiple_of(ci * CHUNK, CHUNK), CHUNK) for ci in (i, nc - 1 - i)]
            insts = [(q_ref[0, sls[d], kh(h)].astype(F32), k_ref[0, sls[d], kh(h)].astype(F32), v_ref[0, sls[d], vh(h)],
                      (lf_ref, lb_ref)[d][0, sls[d], kh(h)], carry[d * hb + h], d == 1)
                     for d in range(2) for h in range(hb)]
            res = _gla_chunks(insts)
            for d in range(2):
                for h in range(hb):
                    o = res[d * hb + h][0]
                    if accumulate:
                        o_ref[0, sls[d], vh(h)] = (acc_ref[sls[d], vh(h)] + o).astype(BF16)
                    else:
                        acc_ref[sls[d], vh(h)] = o
            return tuple(st for _, st in res)
        return body

    if has_s0:
        init = tuple(s0_ref[0, d, h].T for d in range(2) for h in range(hb))
    else:
        init = tuple(jnp.zeros((GLA_DVH, GLA_DKH), F32) for _ in range(2 * hb))
    carry = lax.fori_loop(0, half, both(False), init)
    carry = lax.fori_loop(half, nc, both(True), carry)
    if emit_state:
        for d in range(2):
            for h in range(hb):
                s_ref[0, d, h] = carry[d * hb + h].T


def _gla_scan(q, k, v, lg, s0, hb, emit_state):
    bsz, t, _ = q.shape
    ng = GLA_HEADS // hb
    hd = lambda b, h: (b, 0, h)
    kblk = pl.BlockSpec((1, t, hb * GLA_DKH), hd)
    vblk = pl.BlockSpec((1, t, hb * GLA_DVH), hd)
    sblk = pl.BlockSpec((1, 2, hb, GLA_DKH, GLA_DVH), lambda b, h: (b, 0, h, 0, 0))
    s_shape = jax.ShapeDtypeStruct((bsz, 2, GLA_HEADS, GLA_DKH, GLA_DVH), F32)
    has_s0 = s0 is not None
    return pl.pallas_call(
        functools.partial(_gla_scan_kernel, has_s0=has_s0, emit_state=emit_state),
        grid=(bsz, ng),
        in_specs=[kblk, kblk, vblk, kblk, pl.BlockSpec((1, t, hb * GLA_DKH), lambda b, h: (b, 0, ng + h))]
        + [sblk] * has_s0,
        out_specs=[vblk] + [sblk] * emit_state,
        out_shape=[jax.ShapeDtypeStruct((bsz, t, GLA_DV), BF16)] + [s_shape] * emit_state,
        scratch_shapes=[pltpu.VMEM((t, hb * GLA_DVH), F32)],
        compiler_params=_params(2),
        name="gla_scan",
    )(q, k, v, lg, lg, *([s0] * has_s0))


def _gla_back_kernel(o_ref, g_ref, x_ref, mod0_ref, mod1_ref, gn_ref, wout_ref, nw1_ref, x1_ref, h1_ref):
    d = D_MODEL
    o = o_ref[...].astype(F32)
    gn = gn_ref[...]
    parts = [_rms(o[:, h * GLA_DVH:(h + 1) * GLA_DVH]) * gn for h in range(GLA_HEADS)]
    on = jnp.concatenate(parts, axis=-1) * _silu(g_ref[...].astype(F32))
    out = _dot(on, wout_ref[...])
    gate = mod0_ref[0][:, 2 * d:]
    x1 = x_ref[...] + gate * out
    x1_ref[...] = x1
    mod1 = mod1_ref[0]
    h1_ref[...] = (_rms(x1) * nw1_ref[...] * (1.0 + mod1[:, d:2 * d]) + mod1[:, :d]).astype(BF16)


def _gla_back(o, g, x, mod0, mod1, mod_idx, tm, gla_norm, w_out, norm_w1):
    n, d = x.shape
    nmod = mod0.shape[0]
    row = lambda i: (i, 0)
    const = lambda i: (0, 0)
    modspec = pl.BlockSpec((1, 1, 3 * d), lambda i: (mod_idx(i), 0, 0))
    return pl.pallas_call(
        _gla_back_kernel,
        grid=(n // tm,),
        in_specs=[pl.BlockSpec((tm, d), row), pl.BlockSpec((tm, d), row), pl.BlockSpec((tm, d), row),
                  modspec, modspec,
                  pl.BlockSpec((1, GLA_DVH), const), pl.BlockSpec((d, d), const), pl.BlockSpec((1, d), const)],
        out_specs=[pl.BlockSpec((tm, d), row), pl.BlockSpec((tm, d), row)],
        out_shape=[jax.ShapeDtypeStruct((n, d), F32), jax.ShapeDtypeStruct((n, d), BF16)],
        compiler_params=_params(1),
        name="gla_back",
    )(o, g, x, mod0.reshape(nmod, 1, 3 * d), mod1.reshape(nmod, 1, 3 * d), gla_norm.reshape(1, GLA_DVH),
      w_out.astype(BF16), norm_w1.reshape(1, d))


def _token_shift(h, up_ref, dn_ref, seq_len, grid_w):
    tm = h.shape[0]
    pos = lax.broadcasted_iota(jnp.int32, (tm, 1), 0) + pl.program_id(0) * tm
    prev = pltpu.roll(h, 1, 0)
    nxt = pltpu.roll(h, tm - 1, 0)
    t = pos % seq_len
    if grid_w is None:
        return 0.5 * (jnp.where(t == 0, 0.0, prev) + jnp.where(t == seq_len - 1, 0.0, nxt))
    col = pos % grid_w
    up = jnp.concatenate([up_ref[...].astype(F32), h[:tm - grid_w]], axis=0)
    dn = jnp.concatenate([h[grid_w:], dn_ref[...].astype(F32)], axis=0)
    up = jnp.where(t < grid_w, 0.0, up)
    dn = jnp.where(t >= seq_len - grid_w, 0.0, dn)
    left = jnp.where(col == 0, 0.0, prev)
    right = jnp.where(col == grid_w - 1, 0.0, nxt)
    return 0.25 * (up + dn + left + right)


def _rwkv_front_kernel(*refs, seq_len, grid_w):
    n_h = 1 if grid_w is None else 3
    h_ref, up_ref, dn_ref = (tuple(refs[:n_h]) + (None, None))[:3]
    (mu_ref, wrkvg_ref, w1_ref, w2_ref, w0_ref, a1_ref, a2_ref, a0_ref,
     r_ref, k_ref, v_ref, g_ref, lw_ref, a_ref) = refs[n_h:]
    h = h_ref[...].astype(F32)
    dh = _token_shift(h, up_ref, dn_ref, seq_len, grid_w) - h
    mix = lambda p: (h + dh * mu_ref[p:p + 1, :]).astype(BF16)
    r_ref[...] = jnp.dot(mix(0), wrkvg_ref[0], preferred_element_type=F32).astype(BF16)
    k_ref[...] = jnp.dot(mix(1), wrkvg_ref[1], preferred_element_type=F32).astype(BF16)
    v_ref[...] = jnp.dot(mix(2), wrkvg_ref[2], preferred_element_type=F32).astype(BF16)
    g_ref[...] = jnp.dot(mix(3), wrkvg_ref[3], preferred_element_type=F32).astype(BF16)
    tw = jnp.tanh(jnp.dot(mix(4), w1_ref[...], preferred_element_type=F32))
    lw_ref[...] = -RWKV_DECAY_SCALE * _sigmoid(w0_ref[...] + _dot(tw, w2_ref[...]))
    ta = jnp.dot(mix(5), a1_ref[...], preferred_element_type=F32)
    a_ref[...] = _sigmoid(a0_ref[...] + _dot(ta, a2_ref[...])).astype(BF16)


def _rwkv_front(h, tm, seq_len, grid_w, mu, w_rkvg, w0, w1, w2, a0, a1, a2):
    n, d = h.shape
    rk = RWKV_RANK
    cat1 = lambda w: jnp.concatenate([w[0], w[1]], axis=1)
    bd2 = lambda w: jnp.zeros((2 * rk, 2 * d), F32).at[:rk, :d].set(w[0]).at[rk:, d:].set(w[1])
    row = lambda i: (i, 0)
    const = lambda i: (0, 0)
    if grid_w is None:
        assert tm % seq_len == 0
        h_specs, h_args = [pl.BlockSpec((tm, d), row)], [h]
    else:
        assert tm % grid_w == 0 and seq_len % tm == 0
        rows_per_tile, last = tm // grid_w, n // grid_w - 1
        h_specs = [pl.BlockSpec((tm, d), row),
                   pl.BlockSpec((grid_w, d), lambda i: (jnp.maximum(i * rows_per_tile - 1, 0), 0)),
                   pl.BlockSpec((grid_w, d), lambda i: (jnp.minimum((i + 1) * rows_per_tile, last), 0))]
        h_args = [h, h, h]
    return pl.pallas_call(
        functools.partial(_rwkv_front_kernel, seq_len=seq_len, grid_w=grid_w),
        grid=(n // tm,),
        in_specs=h_specs + [
            pl.BlockSpec((8, d), const),
            pl.BlockSpec((4, d, d), lambda i: (0, 0, 0)),
            pl.BlockSpec((d, 2 * rk), const), pl.BlockSpec((2 * rk, 2 * d), const), pl.BlockSpec((1, 2 * d), const),
            pl.BlockSpec((d, 2 * rk), const), pl.BlockSpec((2 * rk, 2 * d), const), pl.BlockSpec((1, 2 * d), const)],
        out_specs=[pl.BlockSpec((tm, d), row)] * 4 + [pl.BlockSpec((tm, 2 * d), row)] * 2,
        out_shape=[jax.ShapeDtypeStruct((n, d), BF16)] * 4
        + [jax.ShapeDtypeStruct((n, 2 * d), F32), jax.ShapeDtypeStruct((n, 2 * d), BF16)],
        compiler_params=_params(1),
        name="rwkv_front",
    )(*h_args, jnp.zeros((8, d), F32).at[:6].set(mu), w_rkvg.astype(BF16),
      cat1(w1).astype(BF16), bd2(w2).astype(BF16), w0.reshape(1, 2 * d),
      cat1(a1).astype(BF16), bd2(a2).astype(BF16), a0.reshape(1, 2 * d))


def _seg_sum(x, head0):
    s0 = jnp.sum(jnp.where(head0, x, 0.0), axis=-1, keepdims=True)
    s1 = jnp.sum(jnp.where(head0, 0.0, x), axis=-1, keepdims=True)
    return jnp.where(head0, s0, s1)


def _stack(x, head0):
    return jnp.concatenate([jnp.where(head0, x, 0.0), jnp.where(head0, 0.0, x)], axis=0)


def _rwkv_chunk_terms(insts):
    n = len(insts)
    c = insts[0][0].shape[0]
    lane = lax.broadcasted_iota(jnp.int32, (c, LANES), 1)
    head0 = lane < RWKV_HEAD
    row = lax.broadcasted_iota(jnp.int32, (c, c), 0)
    col = lax.broadcasted_iota(jnp.int32, (c, c), 1)
    ti = lax.broadcasted_iota(jnp.int32, (c, LANES), 0)
    si = lane & (c - 1)
    prow = lax.broadcasted_iota(jnp.int32, (LANES, LANES), 0)
    pcol = lax.broadcasted_iota(jnp.int32, (LANES, LANES), 1)
    same_head = (prow >= RWKV_HEAD) == (pcol >= RWKV_HEAD)
    each = lambda f, *ls: [f(*xs) for xs in zip(*ls)]
    rev = [x[8] for x in insts]
    stk = lambda x: _stack(x, head0).astype(BF16)

    cum = [_tri_dot(((col >= row) if x[8] else (col <= row)).astype(F32), x[3]) for x in insts]
    tot = [cm[0:1] if rv else cm[c - 1:c] for cm, rv in zip(cum, rev)]

    a_t, r_t, b_ts, k_ts, b_h, k_h, v_s, vv, bonus = [], [], [], [], [], [], [], [], []
    for (r, k, v, lw, a, k_k, k_a, r_k, _), cm, tt in zip(insts, cum, tot):
        kk = k * k_k
        kk = kk / jnp.maximum(jnp.sqrt(_seg_sum(kk * kk, head0)), 1e-12)
        kd = k * (1.0 + (a - 1.0) * k_a)
        bv = kk * a
        bonus.append(_seg_sum(r * kd * r_k, head0) * v)
        e_neg = jnp.exp(-cm)
        e_tail = jnp.exp(tt - cm)
        a_t.append(-kk * jnp.exp(cm - lw))
        r_t.append(r * jnp.exp(cm))
        b_ts.append(stk(bv * e_neg))
        k_ts.append(stk(kd * e_neg))
        b_h.append(bv * e_tail)
        k_h.append(kd * e_tail)
        v_s.append(stk(v))
        vv.append(v)

    strict = [(si > ti) if rv else (si < ti) for rv in rev]
    incl = [(si >= ti) if rv else (si <= ti) for rv in rev]
    ar = each(lambda x, y: jnp.concatenate([x, y], axis=0).astype(BF16), a_t, r_t)
    g_b = each(_dot_nt, ar, b_ts)
    g_k = each(_dot_nt, ar, k_ts)
    n_ab = each(lambda m_, x: jnp.where(m_, x[:c], 0.0), strict, g_b)
    l_ak = each(lambda m_, x: jnp.where(m_, x[:c], 0.0), strict, g_k)
    p_rb = each(lambda m_, x: jnp.where(m_, x[c:], 0.0), incl, g_b)
    p_rk = each(lambda m_, x: jnp.where(m_, x[c:], 0.0), incl, g_k)

    same_blk = lambda h: (ti // h) == (si // h)
    n_d = each(lambda x: jnp.where(same_blk(INV_BASE), x, 0.0), n_ab)
    steps = max(1, (INV_BASE - 1).bit_length())
    q = n_d
    tinv = n_d
    q = each(lambda x: _dot(x, stk(x)), q) if steps > 1 else q
    for j in range(1, steps):
        last = j == steps - 1
        qs = each(stk, q)
        if last:
            z = each(_dot, tinv, qs)
            tinv = each(lambda t_, x, y: t_ + x + y, tinv, q, z)
        else:
            both = each(lambda x, t_, w_: _dot(jnp.concatenate([x, t_], axis=0), w_), q, tinv, qs)
            tinv = each(lambda t_, x, y: t_ + x + y[c:], tinv, q, both)
            q = each(lambda y: y[:c], both)
    h = INV_BASE
    while h < c:
        link = same_blk(2 * h) & jnp.logical_not(same_blk(h))
        n_off = each(lambda x: jnp.where(link, x, 0.0), n_ab)
        y = each(lambda t_, x: x + _dot(t_, stk(x)), tinv, n_off)
        tinv = each(lambda t_, x: t_ + x + _dot(x, stk(t_)), tinv, y)
        h *= 2
    a_bar = each(lambda t_, x: x + _dot(t_, stk(x)), tinv, a_t)
    lv = each(lambda l_, p, y: _dot(jnp.concatenate([l_, p], axis=0), y), l_ak, p_rk, v_s)
    w = [x[:c] for x in lv]
    u0 = each(lambda t_, x: x + _dot(t_, stk(x)), tinv, w)
    r_bar = each(lambda x, p, y: x + _dot(p, stk(y)), r_t, p_rb, a_bar)
    y0 = each(lambda p, x, z: _dot(p, stk(x)) + z[c:], p_rb, u0, lv)
    zero = jnp.zeros((c, LANES), F32)
    ms = each(lambda ab, u, v_, bh, kh_: _dot_tn(
        jnp.concatenate([jnp.concatenate([ab, u], axis=1), jnp.concatenate([zero, v_], axis=1)], axis=0),
        jnp.concatenate([bh, kh_], axis=0)), a_bar, u0, vv, b_h, k_h)
    m = [jnp.where(same_head, x[:LANES], 0.0) for x in ms]
    s0p = [jnp.where(same_head, x[LANES:], 0.0) for x in ms]
    return [(r_bar[i], y0[i], m[i], s0p[i], jnp.exp(tot[i]), bonus[i]) for i in range(n)]


def _rwkv_scan_kernel(*refs, has_s0, emit_state):
    (r_ref, k_ref, v_ref, lwf_ref, lwb_ref, af_ref, ab_ref,
     kkf_ref, kkb_ref, kaf_ref, kab_ref, rkf_ref, rkb_ref, lnw_ref, lnb_ref) = refs[:15]
    s0_ref = refs[15] if has_s0 else None
    o_ref = refs[15 + has_s0]
    s_ref = refs[16 + has_s0] if emit_state else None
    rbar_ref, y0_ref, m_ref, s0p_ref, dec_ref, bonus_ref, yf_ref, yb_ref = refs[16 + has_s0 + emit_state:]
    bb, t = r_ref.shape[0], r_ref.shape[1]
    nc = t // CHUNK
    dirs = ((lwf_ref, af_ref, kkf_ref, kaf_ref, rkf_ref, False), (lwb_ref, ab_ref, kkb_ref, kab_ref, rkb_ref, True))
    group = min(bb * nc, PREP_GROUP)
    assert (bb * nc) % group == 0
    chunk_rows = lambda ci: pl.ds(pl.multiple_of(ci * CHUNK, CHUNK), CHUNK)

    def prep(gi, carry):
        insts, where = [], []
        at = [((gi * group + j) // nc, chunk_rows((gi * group + j) % nc)) for j in range(group)]
        for j, (e, sl) in enumerate(at):
            r, k, v = (ref[e, sl, :].astype(F32) for ref in (r_ref, k_ref, v_ref))
            for d, (lw_ref, a_ref, kk_ref, ka_ref, rk_ref, reverse) in enumerate(dirs):
                insts.append((r, k, v, lw_ref[e, sl, :], a_ref[e, sl, :].astype(F32), kk_ref[...], ka_ref[...],
                              rk_ref[...], reverse))
                where.append((d, gi * group + j, e, sl))
        terms = _rwkv_chunk_terms(insts)
        for (d, fi, e, sl), (r_bar, y0, m, s0p, dec, bon) in zip(where, terms):
            rbar_ref[d, fi] = r_bar.astype(BF16)
            y0_ref[d, fi] = y0
            m_ref[d, fi] = m.astype(BF16)
            s0p_ref[d, fi] = s0p
            dec_ref[d, fi] = jnp.broadcast_to(dec, (8, LANES))
        for j in range(group):
            _, _, e, sl = where[2 * j]
            bonus_ref[e, sl, :] = terms[2 * j][5] + terms[2 * j + 1][5]
        return carry

    lax.fori_loop(0, bb * nc // group, prep, 0)

    def seq(i, carry):
        cis = (i, nc - 1 - i)
        idx = [(d, e) for d in range(2) for e in range(bb)]
        sb16 = [s.astype(BF16) for s in carry]
        sm = [jnp.dot(sb16[n], m_ref[d, e * nc + cis[d]], preferred_element_type=F32) for n, (d, e) in enumerate(idx)]
        ys = [_dot_nt(rbar_ref[d, e * nc + cis[d]], sb16[n]) + y0_ref[d, e * nc + cis[d]] for n, (d, e) in enumerate(idx)]
        for n, (d, e) in enumerate(idx):
            (yf_ref, yb_ref)[d][e, chunk_rows(cis[d]), :] = ys[n]
        return tuple(carry[n] * dec_ref[d, e * nc + cis[d]][0:1] + sm[n] + s0p_ref[d, e * nc + cis[d]]
                     for n, (d, e) in enumerate(idx))

    if has_s0:
        init = tuple(s0_ref[e, d, 0] for d in range(2) for e in range(bb))
    else:
        init = tuple(jnp.zeros((LANES, LANES), F32) for _ in range(2 * bb))
    final = lax.fori_loop(0, nc, seq, init)
    if emit_state:
        for d in range(2):
            for e in range(bb):
                s = final[d * bb + e]
                s_ref[e, d, 0] = s[:RWKV_HEAD, :RWKV_HEAD]
                s_ref[e, d, 1] = pltpu.roll(s, RWKV_HEAD, 1)[RWKV_HEAD:, :RWKV_HEAD]

    prow = lax.broadcasted_iota(jnp.int32, (LANES, LANES), 0)
    pcol = lax.broadcasted_iota(jnp.int32, (LANES, LANES), 1)
    head_ones = ((prow >= RWKV_HEAD) == (pcol >= RWKV_HEAD)).astype(BF16)
    inv_n = 1.0 / RWKV_HEAD

    def head_mean(x):
        hi = x.astype(BF16)
        lo = (x - hi.astype(F32)).astype(BF16)
        both = jnp.dot(jnp.concatenate([hi, lo], axis=0), head_ones, preferred_element_type=F32)
        return (both[:x.shape[0]] + both[x.shape[0]:]) * inv_n

    def finish(gi, carry):
        at = [((gi * group + j) // nc, chunk_rows((gi * group + j) % nc)) for j in range(group)]
        y = [yf_ref[e, sl, :] + yb_ref[e, sl, :] for e, sl in at]
        yc = [x - head_mean(x) for x in y]
        var = [head_mean(x * x) for x in yc]
        for (e, sl), x, vr in zip(at, yc, var):
            o_ref[e, sl, :] = (x * lax.rsqrt(vr + LNX_EPS) * lnw_ref[...] + lnb_ref[...]
                               + bonus_ref[e, sl, :]).astype(BF16)
        return carry

    lax.fori_loop(0, bb * nc // group, finish, 0)


def _rwkv_scan(r, k, v, lw, a, k_k, k_a, r_k, lnx_w, lnx_b, s0, bb, emit_state):
    bsz, t, d = r.shape
    nc = t // CHUNK
    seq = lambda b, p: (b, 0, p)
    seq_b = lambda b, p: (b, 0, RWKV_PAIRS + p)
    par = lambda b, p: (0, p)
    par_b = lambda b, p: (0, RWKV_PAIRS + p)
    sblk = pl.BlockSpec((bb, t, LANES), seq)
    sblk_b = pl.BlockSpec((bb, t, LANES), seq_b)
    pblk = pl.BlockSpec((1, LANES), par)
    pblk_b = pl.BlockSpec((1, LANES), par_b)
    s0blk = pl.BlockSpec((bb, 2, 1, LANES, LANES), lambda b, p: (b, 0, p, 0, 0))
    s_blk = pl.BlockSpec((bb, 2, 2, RWKV_HEAD, RWKV_HEAD), lambda b, p: (b, 0, p, 0, 0))
    s_shape = jax.ShapeDtypeStruct((bsz, 2, RWKV_HEADS, RWKV_HEAD, RWKV_HEAD), F32)
    has_s0 = s0 is not None
    k_k, k_a, r_k = (p.reshape(1, 2 * d) for p in (k_k, k_a, r_k))
    return pl.pallas_call(
        functools.partial(_rwkv_scan_kernel, has_s0=has_s0, emit_state=emit_state),
        grid=(bsz // bb, RWKV_PAIRS),
        in_specs=[sblk, sblk, sblk, sblk, sblk_b, sblk, sblk_b,
                  pblk, pblk_b, pblk, pblk_b, pblk, pblk_b, pblk, pblk] + [s0blk] * has_s0,
        out_specs=[sblk] + [s_blk] * emit_state,
        out_shape=[jax.ShapeDtypeStruct((bsz, t, d), BF16)] + [s_shape] * emit_state,
        scratch_shapes=[pltpu.VMEM((2, bb * nc, CHUNK, LANES), BF16), pltpu.VMEM((2, bb * nc, CHUNK, LANES), F32),
                        pltpu.VMEM((2, bb * nc, LANES, LANES), BF16), pltpu.VMEM((2, bb * nc, LANES, LANES), F32),
                        pltpu.VMEM((2, bb * nc, 8, LANES), F32), pltpu.VMEM((bb, t, LANES), F32),
                        pltpu.VMEM((bb, t, LANES), F32), pltpu.VMEM((bb, t, LANES), F32)],
        compiler_params=_params(2),
        name="rwkv_scan",
    )(r, k, v, lw, lw, a, a, k_k, k_k, k_a, k_a, r_k, r_k,
      lnx_w.reshape(1, d), lnx_b.reshape(1, d), *([s0] * has_s0))


def _rwkv_back_kernel(o_ref, g_ref, x_ref, mod_ref, wout_ref, nf_ref, y_ref):
    out = _dot(o_ref[...].astype(F32) * _silu(g_ref[...].astype(F32)), wout_ref[...])
    x2 = x_ref[...] + mod_ref[0][:, 2 * D_MODEL:] * out
    y_ref[...] = _rms(x2) * nf_ref[...]


def _rwkv_back(o, g, x, mod1, mod_idx, tm, w_out, norm_f):
    n, d = x.shape
    nmod = mod1.shape[0]
    row = lambda i: (i, 0)
    const = lambda i: (0, 0)
    return pl.pallas_call(
        _rwkv_back_kernel,
        grid=(n // tm,),
        in_specs=[pl.BlockSpec((tm, d), row), pl.BlockSpec((tm, d), row), pl.BlockSpec((tm, d), row),
                  pl.BlockSpec((1, 1, 3 * d), lambda i: (mod_idx(i), 0, 0)),
                  pl.BlockSpec((d, d), const), pl.BlockSpec((1, d), const)],
        out_specs=pl.BlockSpec((tm, d), row),
        out_shape=jax.ShapeDtypeStruct((n, d), F32),
        compiler_params=_params(1),
        name="rwkv_back",
    )(o, g, x, mod1.reshape(nmod, 1, 3 * d), w_out.astype(BF16), norm_f.reshape(1, d))


def _pair_blockdiag(s):
    b = s.shape[0]
    s = s.reshape(b, 2, RWKV_PAIRS, 2, RWKV_HEAD, RWKV_HEAD)
    z = jnp.zeros_like(s[:, :, :, 0])
    top = jnp.concatenate([s[:, :, :, 0], z], axis=-1)
    bot = jnp.concatenate([z, s[:, :, :, 1]], axis=-1)
    return jnp.concatenate([top, bot], axis=-2)


def kernel(x_prompt, x_sample, state_gla, state_rwkv, c, c_ctx, w_mod, b_mod, norm_w, gla_w_in, gla_w_a1,
           gla_w_a2, gla_b_a, gla_norm, gla_w_out, rwkv_mu, rwkv_w_rkvg, rwkv_w0, rwkv_w1, rwkv_w2, rwkv_a0,
           rwkv_a1, rwkv_a2, rwkv_k_k, rwkv_k_a, rwkv_r_k, rwkv_lnx_w, rwkv_lnx_b, rwkv_w_out, norm_f):
    d = D_MODEL
    bp, tp, _ = x_prompt.shape
    bs, ts, _ = x_sample.shape
    assert tp % CHUNK == 0 and ts % CHUNK == 0 and ts % GRID_W == 0

    nrows = -(-(1 + bs) // 8) * 8
    cond = jnp.zeros((nrows, d), F32).at[0].set(c_ctx).at[1:1 + bs].set(c)
    cond = cond * (1.0 / (1.0 + jnp.exp(-cond)))
    mod0 = _matmul_bias(cond, w_mod[0], b_mod[0].reshape(1, 3 * d))
    mod1 = _matmul_bias(cond, w_mod[1], b_mod[1].reshape(1, 3 * d))

    def trunk(x3, mod_idx, grid_w, gla_s0, rwkv_s0, rwkv_bb, emit_state):
        b, t, _ = x3.shape
        n = b * t
        tm = min(TOKEN_TILE, t)
        x = x3.reshape(n, d)
        seq = lambda arr: arr.reshape(b, t, arr.shape[-1])
        q, k, v, g, lg = _gla_front(x, mod0, mod_idx, tm, norm_w[0], gla_w_in[0], gla_w_a1[0], gla_w_a2[0], gla_b_a[0])
        o, *gla_s = _gla_scan(seq(q), seq(k), seq(v), seq(lg), gla_s0, GLA_HEADS, emit_state)
        x1, h1 = _gla_back(o.reshape(n, d), g, x, mod0, mod1, mod_idx, tm, gla_norm[0], gla_w_out[0], norm_w[1])
        r, k, v, g, lw, a = _rwkv_front(h1, tm, t, grid_w, rwkv_mu[0], rwkv_w_rkvg[0], rwkv_w0[0], rwkv_w1[0], rwkv_w2[0],
                                        rwkv_a0[0], rwkv_a1[0], rwkv_a2[0])
        o, *rwkv_s = _rwkv_scan(seq(r), seq(k), seq(v), seq(lw), seq(a), rwkv_k_k[0], rwkv_k_a[0], rwkv_r_k[0],
                                rwkv_lnx_w[0], rwkv_lnx_b[0], rwkv_s0, rwkv_bb, emit_state)
        y = _rwkv_back(o.reshape(n, d), g, x1, mod1, mod_idx, tm, rwkv_w_out[0], norm_f)
        return y.reshape(b, t, d), gla_s, rwkv_s

    y_prompt, (gla_s,), (rwkv_s,) = trunk(x_prompt, lambda i: 0, None, None, None, min(bp, 4), True)

    tiles_per_seq = ts // min(TOKEN_TILE, ts)
    y_sample, _, _ = trunk(x_sample, lambda i: 1 + i // tiles_per_seq, GRID_W, state_gla[:, 0],
                           _pair_blockdiag(state_rwkv[:, 0]), min(bs, 4), False)
    return (y_prompt, y_sample, gla_s[:, None], rwkv_s[:, None])
```

```python
import functools

import jax
import jax.numpy as jnp
from jax import lax
from jax.experimental import pallas as pl
from jax.experimental.pallas import tpu as pltpu

F32 = jnp.float32
BF16 = jnp.bfloat16

D_MODEL = 1024
EPS = 1e-6
GRID_W = 64
GLA_HEADS = 4
GLA_DK = 512
GLA_DV = 1024
GLA_DKH = 128
GLA_DVH = 256
GLA_GATE_RANK = 16
GLA_GATE_NORM = 16.0
RWKV_HEAD = 64
RWKV_HEADS = 16
RWKV_PAIRS = 8
RWKV_RANK = 64
LNX_EPS = 64e-5
RWKV_DECAY_SCALE = 0.6065306597126334
CHUNK = 64
INV_BASE = 8
TOKEN_TILE = 512
PREP_GROUP = 8
LANES = 128
VMEM_LIMIT = 56 * 1024 * 1024


def _dot(a, b):
    return jnp.dot(a.astype(BF16), b.astype(BF16), preferred_element_type=F32)


def _dot_nt(a, b):
    return lax.dot_general(a.astype(BF16), b.astype(BF16), (((1,), (1,)), ((), ())),
                           preferred_element_type=F32)


def _dot_tn(a, b):
    return lax.dot_general(a.astype(BF16), b.astype(BF16), (((0,), (0,)), ((), ())),
                           preferred_element_type=F32)


def _tri_dot(tri, x):
    hi = x.astype(BF16)
    lo = (x - hi.astype(F32)).astype(BF16)
    t = tri.astype(BF16)
    return jnp.dot(t, hi, preferred_element_type=F32) + jnp.dot(t, lo, preferred_element_type=F32)


def _log_sigmoid(z):
    return jnp.minimum(z, 0.0) - jnp.log(1.0 + jnp.exp(-jnp.abs(z)))


def _sigmoid(z):
    return 1.0 / (1.0 + jnp.exp(-z))


def _silu(z):
    return z * _sigmoid(z)


def _rms(x):
    return x * lax.rsqrt(jnp.mean(x * x, axis=-1, keepdims=True) + EPS)


def _params(n_grid_dims):
    return pltpu.CompilerParams(dimension_semantics=("arbitrary",) * n_grid_dims,
                                vmem_limit_bytes=VMEM_LIMIT)


def _mm_kernel(x_ref, w_ref, b_ref, o_ref):
    o_ref[...] = _dot(x_ref[...], w_ref[...]) + b_ref[...]


def _matmul_bias(x, w, b):
    m, k = x.shape
    n = w.shape[1]
    return pl.pallas_call(
        _mm_kernel,
        grid=(1,),
        in_specs=[pl.BlockSpec((m, k), lambda i: (0, 0)),
                  pl.BlockSpec((k, n), lambda i: (0, 0)),
                  pl.BlockSpec((1, n), lambda i: (0, 0))],
        out_specs=pl.BlockSpec((m, n), lambda i: (0, 0)),
        out_shape=jax.ShapeDtypeStruct((m, n), F32),
        compiler_params=_params(1),
        name="mod_matmul",
    )(x, w.astype(BF16), b)


def _gla_front_kernel(x_ref, mod_ref, nw_ref, win_ref, wa1_ref, wa2_ref, ba_ref,
                      q_ref, k_ref, v_ref, g_ref, lg_ref):
    d = D_MODEL
    mod = mod_ref[0]
    shift, scale = mod[:, :d], mod[:, d:2 * d]
    h = _rms(x_ref[...]) * nw_ref[...] * (1.0 + scale) + shift
    hb = h.astype(BF16)
    proj = lambda lo, hi: jnp.dot(hb, win_ref[:, lo:hi], preferred_element_type=F32)
    q_ref[...] = (proj(0, GLA_DK) * (GLA_DKH ** -0.5)).astype(BF16)
    k_ref[...] = proj(GLA_DK, 2 * GLA_DK).astype(BF16)
    v_ref[...] = proj(2 * GLA_DK, 2 * GLA_DK + GLA_DV).astype(BF16)
    g_ref[...] = proj(2 * GLA_DK + GLA_DV, 2 * GLA_DK + 2 * GLA_DV).astype(BF16)
    t = jnp.dot(hb, wa1_ref[...], preferred_element_type=F32)
    z = _dot(t, wa2_ref[...]) + ba_ref[...]
    lg_ref[...] = _log_sigmoid(z) * (1.0 / GLA_GATE_NORM)


def _gla_front(x, mod, mod_idx, tm, norm_w, w_in, w_a1, w_a2, b_a):
    n, d = x.shape
    nmod = mod.shape[0]
    wa1 = jnp.zeros((d, LANES), F32).at[:, :GLA_GATE_RANK].set(w_a1[0]).at[:, GLA_GATE_RANK:2 * GLA_GATE_RANK].set(w_a1[1])
    wa2 = jnp.zeros((LANES, 2 * GLA_DK), F32).at[:GLA_GATE_RANK, :GLA_DK].set(w_a2[0])
    wa2 = wa2.at[GLA_GATE_RANK:2 * GLA_GATE_RANK, GLA_DK:].set(w_a2[1])
    ba = b_a.reshape(1, 2 * GLA_DK)
    row = lambda i: (i, 0)
    const = lambda i: (0, 0)
    n_in = w_in.shape[1]
    outs = pl.pallas_call(
        _gla_front_kernel,
        grid=(n // tm,),
        in_specs=[pl.BlockSpec((tm, d), row),
                  pl.BlockSpec((1, 1, 3 * d), lambda i: (mod_idx(i), 0, 0)),
                  pl.BlockSpec((1, d), const),
                  pl.BlockSpec((d, n_in), const),
                  pl.BlockSpec((d, LANES), const),
                  pl.BlockSpec((LANES, 2 * GLA_DK), const),
                  pl.BlockSpec((1, 2 * GLA_DK), const)],
        out_specs=[pl.BlockSpec((tm, GLA_DK), row), pl.BlockSpec((tm, GLA_DK), row),
                   pl.BlockSpec((tm, GLA_DV), row), pl.BlockSpec((tm, GLA_DV), row),
                   pl.BlockSpec((tm, 2 * GLA_DK), row)],
        out_shape=[jax.ShapeDtypeStruct((n, GLA_DK), BF16), jax.ShapeDtypeStruct((n, GLA_DK), BF16),
                   jax.ShapeDtypeStruct((n, GLA_DV), BF16), jax.ShapeDtypeStruct((n, GLA_DV), BF16),
                   jax.ShapeDtypeStruct((n, 2 * GLA_DK), F32)],
        compiler_params=_params(1),
        name="gla_front",
    )(x, mod.reshape(nmod, 1, 3 * d), norm_w.reshape(1, d), w_in.astype(BF16), wa1.astype(BF16),
      wa2.astype(BF16), ba)
    return outs


def _gla_chunks(insts):
    c = insts[0][0].shape[0]
    row = lax.broadcasted_iota(jnp.int32, (c, c), 0)
    col = lax.broadcasted_iota(jnp.int32, (c, c), 1)
    keep = [(col >= row) if x[5] else (col <= row) for x in insts]
    b = [_tri_dot(kp.astype(F32), x[3]) for kp, x in zip(keep, insts)]
    b_end = [bb[0:1] if x[5] else bb[c - 1:c] for bb, x in zip(b, insts)]
    qb = [x[0] * jnp.exp(bb) for x, bb in zip(insts, b)]
    kb = [x[1] * jnp.exp(-bb) for x, bb in zip(insts, b)]
    kd = [x[1] * jnp.exp(be - bb) for x, bb, be in zip(insts, b, b_end)]
    scores = [jnp.where(kp, _dot_nt(x, y), 0.0) for kp, x, y in zip(keep, qb, kb)]
    o_inter = [_dot_nt(x, ins[4]) for x, ins in zip(qb, insts)]
    upd = [_dot_tn(ins[2], x) for ins, x in zip(insts, kd)]
    o = [_dot(sc, ins[2]) + oi for sc, ins, oi in zip(scores, insts, o_inter)]
    st = [ins[4] * jnp.exp(be) + u for ins, be, u in zip(insts, b_end, upd)]
    return list(zip(o, st))


def _gla_scan_kernel(*refs, has_s0, emit_state):
    q_ref, k_ref, v_ref, lf_ref, lb_ref = refs[:5]
    s0_ref = refs[5] if has_s0 else None
    o_ref = refs[5 + has_s0]
    s_ref = refs[6 + has_s0] if emit_state else None
    acc_ref = refs[-1]
    t = q_ref.shape[1]
    hb = q_ref.shape[2] // GLA_DKH
    nc = t // CHUNK
    half = nc // 2
    kh = lambda h: slice(h * GLA_DKH, (h + 1) * GLA_DKH)
    vh = lambda h: slice(h * GLA_DVH, (h + 1) * GLA_DVH)

    def both(accumulate):
        def body(i, carry):
            sls = [pl.ds(pl.multiple_of(ci * CHUNK, CHUNK), CHUNK) for ci in (i, nc - 1 - i)]
            insts = [(q_ref[0, sls[d], kh(h)].astype(F32), k_ref[0, sls[d], kh(h)].astype(F32), v_ref[0, sls[d], vh(h)],
                      (lf_ref, lb_ref)[d][0, sls[d], kh(h)], carry[d * hb + h], d == 1)
                     for d in range(2) for h in range(hb)]
            res = _gla_chunks(insts)
            for d in range(2):
                for h in range(hb):
                    o = res[d * hb + h][0]
                    if accumulate:
                        o_ref[0, sls[d], vh(h)] = (acc_ref[sls[d], vh(h)] + o).astype(BF16)
                    else:
                        acc_ref[sls[d], vh(h)] = o
            return tuple(st for _, st in res)
        return body

    if has_s0:
        init = tuple(s0_ref[0, d, h].T for d in range(2) for h in range(hb))
    else:
        init = tuple(jnp.zeros((GLA_DVH, GLA_DKH), F32) for _ in range(2 * hb))
    carry = lax.fori_loop(0, half, both(False), init)
    carry = lax.fori_loop(half, nc, both(True), carry)
    if emit_state:
        for d in range(2):
            for h in range(hb):
                s_ref[0, d, h] = carry[d * hb + h].T


def _gla_scan(q, k, v, lg, s0, hb, emit_state):
    bsz, t, _ = q.shape
    ng = GLA_HEADS // hb
    hd = lambda b, h: (b, 0, h)
    kblk = pl.BlockSpec((1, t, hb * GLA_DKH), hd)
    vblk = pl.BlockSpec((1, t, hb * GLA_DVH), hd)
    sblk = pl.BlockSpec((1, 2, hb, GLA_DKH, GLA_DVH), lambda b, h: (b, 0, h, 0, 0))
    s_shape = jax.ShapeDtypeStruct((bsz, 2, GLA_HEADS, GLA_DKH, GLA_DVH), F32)
    has_s0 = s0 is not None
    return pl.pallas_call(
        functools.partial(_gla_scan_kernel, has_s0=has_s0, emit_state=emit_state),
        grid=(bsz, ng),
        in_specs=[kblk, kblk, vblk, kblk, pl.BlockSpec((1, t, hb * GLA_DKH), lambda b, h: (b, 0, ng + h))]
        + [sblk] * has_s0,
        out_specs=[vblk] + [sblk] * emit_state,
        out_shape=[jax.ShapeDtypeStruct((bsz, t, GLA_DV), BF16)] + [s_shape] * emit_state,
        scratch_shapes=[pltpu.VMEM((t, hb * GLA_DVH), F32)],
        compiler_params=_params(2),
        name="gla_scan",
    )(q, k, v, lg, lg, *([s0] * has_s0))


def _gla_back_kernel(o_ref, g_ref, x_ref, mod0_ref, mod1_ref, gn_ref, wout_ref, nw1_ref, x1_ref, h1_ref):
    d = D_MODEL
    o = o_ref[...].astype(F32)
    gn = gn_ref[...]
    parts = [_rms(o[:, h * GLA_DVH:(h + 1) * GLA_DVH]) * gn for h in range(GLA_HEADS)]
    on = jnp.concatenate(parts, axis=-1) * _silu(g_ref[...].astype(F32))
    out = _dot(on, wout_ref[...])
    gate = mod0_ref[0][:, 2 * d:]
    x1 = x_ref[...] + gate * out
    x1_ref[...] = x1
    mod1 = mod1_ref[0]
    h1_ref[...] = (_rms(x1) * nw1_ref[...] * (1.0 + mod1[:, d:2 * d]) + mod1[:, :d]).astype(BF16)


def _gla_back(o, g, x, mod0, mod1, mod_idx, tm, gla_norm, w_out, norm_w1):
    n, d = x.shape
    nmod = mod0.shape[0]
    row = lambda i: (i, 0)
    const = lambda i: (0, 0)
    modspec = pl.BlockSpec((1, 1, 3 * d), lambda i: (mod_idx(i), 0, 0))
    return pl.pallas_call(
        _gla_back_kernel,
        grid=(n // tm,),
        in_specs=[pl.BlockSpec((tm, d), row), pl.BlockSpec((tm, d), row), pl.BlockSpec((tm, d), row),
                  modspec, modspec,
                  pl.BlockSpec((1, GLA_DVH), const), pl.BlockSpec((d, d), const), pl.BlockSpec((1, d), const)],
        out_specs=[pl.BlockSpec((tm, d), row), pl.BlockSpec((tm, d), row)],
        out_shape=[jax.ShapeDtypeStruct((n, d), F32), jax.ShapeDtypeStruct((n, d), BF16)],
        compiler_params=_params(1),
        name="gla_back",
    )(o, g, x, mod0.reshape(nmod, 1, 3 * d), mod1.reshape(nmod, 1, 3 * d), gla_norm.reshape(1, GLA_DVH),
      w_out.astype(BF16), norm_w1.reshape(1, d))


def _token_shift(h, up_ref, dn_ref, seq_len, grid_w):
    tm = h.shape[0]
    pos = lax.broadcasted_iota(jnp.int32, (tm, 1), 0) + pl.program_id(0) * tm
    prev = pltpu.roll(h, 1, 0)
    nxt = pltpu.roll(h, tm - 1, 0)
    t = pos % seq_len
    if grid_w is None:
        return 0.5 * (jnp.where(t == 0, 0.0, prev) + jnp.where(t == seq_len - 1, 0.0, nxt))
    col = pos % grid_w
    up = jnp.concatenate([up_ref[...].astype(F32), h[:tm - grid_w]], axis=0)
    dn = jnp.concatenate([h[grid_w:], dn_ref[...].astype(F32)], axis=0)
    up = jnp.where(t < grid_w, 0.0, up)
    dn = jnp.where(t >= seq_len - grid_w, 0.0, dn)
    left = jnp.where(col == 0, 0.0, prev)
    right = jnp.where(col == grid_w - 1, 0.0, nxt)
    return 0.25 * (up + dn + left + right)


def _rwkv_front_kernel(*refs, seq_len, grid_w):
    n_h = 1 if grid_w is None else 3
    h_ref, up_ref, dn_ref = (tuple(refs[:n_h]) + (None, None))[:3]
    (mu_ref, wrkvg_ref, w1_ref, w2_ref, w0_ref, a1_ref, a2_ref, a0_ref,
     r_ref, k_ref, v_ref, g_ref, lw_ref, a_ref) = refs[n_h:]
    h = h_ref[...].astype(F32)
    dh = _token_shift(h, up_ref, dn_ref, seq_len, grid_w) - h
    mix = lambda p: (h + dh * mu_ref[p:p + 1, :]).astype(BF16)
    r_ref[...] = jnp.dot(mix(0), wrkvg_ref[0], preferred_element_type=F32).astype(BF16)
    k_ref[...] = jnp.dot(mix(1), wrkvg_ref[1], preferred_element_type=F32).astype(BF16)
    v_ref[...] = jnp.dot(mix(2), wrkvg_ref[2], preferred_element_type=F32).astype(BF16)
    g_ref[...] = jnp.dot(mix(3), wrkvg_ref[3], preferred_element_type=F32).astype(BF16)
    tw = jnp.tanh(jnp.dot(mix(4), w1_ref[...], preferred_element_type=F32))
    lw_ref[...] = -RWKV_DECAY_SCALE * _sigmoid(w0_ref[...] + _dot(tw, w2_ref[...]))
    ta = jnp.dot(mix(5), a1_ref[...], preferred_element_type=F32)
    a_ref[...] = _sigmoid(a0_ref[...] + _dot(ta, a2_ref[...])).astype(BF16)


def _rwkv_front(h, tm, seq_len, grid_w, mu, w_rkvg, w0, w1, w2, a0, a1, a2):
    n, d = h.shape
    rk = RWKV_RANK
    cat1 = lambda w: jnp.concatenate([w[0], w[1]], axis=1)
    bd2 = lambda w: jnp.zeros((2 * rk, 2 * d), F32).at[:rk, :d].set(w[0]).at[rk:, d:].set(w[1])
    row = lambda i: (i, 0)
    const = lambda i: (0, 0)
    if grid_w is None:
        assert tm % seq_len == 0
        h_specs, h_args = [pl.BlockSpec((tm, d), row)], [h]
    else:
        assert tm % grid_w == 0 and seq_len % tm == 0
        rows_per_tile, last = tm // grid_w, n // grid_w - 1
        h_specs = [pl.BlockSpec((tm, d), row),
                   pl.BlockSpec((grid_w, d), lambda i: (jnp.maximum(i * rows_per_tile - 1, 0), 0)),
                   pl.BlockSpec((grid_w, d), lambda i: (jnp.minimum((i + 1) * rows_per_tile, last), 0))]
        h_args = [h, h, h]
    return pl.pallas_call(
        functools.partial(_rwkv_front_kernel, seq_len=seq_len, grid_w=grid_w),
        grid=(n // tm,),
        in_specs=h_specs + [
            pl.BlockSpec((8, d), const),
            pl.BlockSpec((4, d, d), lambda i: (0, 0, 0)),
            pl.BlockSpec((d, 2 * rk), const), pl.BlockSpec((2 * rk, 2 * d), const), pl.BlockSpec((1, 2 * d), const),
            pl.BlockSpec((d, 2 * rk), const), pl.BlockSpec((2 * rk, 2 * d), const), pl.BlockSpec((1, 2 * d), const)],
        out_specs=[pl.BlockSpec((tm, d), row)] * 4 + [pl.BlockSpec((tm, 2 * d), row)] * 2,
        out_shape=[jax.ShapeDtypeStruct((n, d), BF16)] * 4
        + [jax.ShapeDtypeStruct((n, 2 * d), F32), jax.ShapeDtypeStruct((n, 2 * d), BF16)],
        compiler_params=_params(1),
        name="rwkv_front",
    )(*h_args, jnp.zeros((8, d), F32).at[:6].set(mu), w_rkvg.astype(BF16),
      cat1(w1).astype(BF16), bd2(w2).astype(BF16), w0.reshape(1, 2 * d),
      cat1(a1).astype(BF16), bd2(a2).astype(BF16), a0.reshape(1, 2 * d))


def _seg_sum(x, head0):
    s0 = jnp.sum(jnp.where(head0, x, 0.0), axis=-1, keepdims=True)
    s1 = jnp.sum(jnp.where(head0, 0.0, x), axis=-1, keepdims=True)
    return jnp.where(head0, s0, s1)


def _stack(x, head0):
    return jnp.concatenate([jnp.where(head0, x, 0.0), jnp.where(head0, 0.0, x)], axis=0)


def _rwkv_chunk_terms(insts):
    n = len(insts)
    c = insts[0][0].shape[0]
    lane = lax.broadcasted_iota(jnp.int32, (c, LANES), 1)
    head0 = lane < RWKV_HEAD
    row = lax.broadcasted_iota(jnp.int32, (c, c), 0)
    col = lax.broadcasted_iota(jnp.int32, (c, c), 1)
    ti = lax.broadcasted_iota(jnp.int32, (c, LANES), 0)
    si = lane & (c - 1)
    prow = lax.broadcasted_iota(jnp.int32, (LANES, LANES), 0)
    pcol = lax.broadcasted_iota(jnp.int32, (LANES, LANES), 1)
    same_head = (prow >= RWKV_HEAD) == (pcol >= RWKV_HEAD)
    each = lambda f, *ls: [f(*xs) for xs in zip(*ls)]
    rev = [x[8] for x in insts]
    stk = lambda x: _stack(x, head0).astype(BF16)

    cum = [_tri_dot(((col >= row) if x[8] else (col <= row)).astype(F32), x[3]) for x in insts]
    tot = [cm[0:1] if rv else cm[c - 1:c] for cm, rv in zip(cum, rev)]

    a_t, r_t, b_ts, k_ts, b_h, k_h, v_s, vv, bonus = [], [], [], [], [], [], [], [], []
    for (r, k, v, lw, a, k_k, k_a, r_k, _), cm, tt in zip(insts, cum, tot):
        kk = k * k_k
        kk = kk / jnp.maximum(jnp.sqrt(_seg_sum(kk * kk, head0)), 1e-12)
        kd = k * (1.0 + (a - 1.0) * k_a)
        bv = kk * a
        bonus.append(_seg_sum(r * kd * r_k, head0) * v)
        e_neg = jnp.exp(-cm)
        e_tail = jnp.exp(tt - cm)
        a_t.append(-kk * jnp.exp(cm - lw))
        r_t.append(r * jnp.exp(cm))
        b_ts.append(stk(bv * e_neg))
        k_ts.append(stk(kd * e_neg))
        b_h.append(bv * e_tail)
        k_h.append(kd * e_tail)
        v_s.append(stk(v))
        vv.append(v)

    strict = [(si > ti) if rv else (si < ti) for rv in rev]
    incl = [(si >= ti) if rv else (si <= ti) for rv in rev]
    ar = each(lambda x, y: jnp.concatenate([x, y], axis=0).astype(BF16), a_t, r_t)
    g_b = each(_dot_nt, ar, b_ts)
    g_k = each(_dot_nt, ar, k_ts)
    n_ab = each(lambda m_, x: jnp.where(m_, x[:c], 0.0), strict, g_b)
    l_ak = each(lambda m_, x: jnp.where(m_, x[:c], 0.0), strict, g_k)
    p_rb = each(lambda m_, x: jnp.where(m_, x[c:], 0.0), incl, g_b)
    p_rk = each(lambda m_, x: jnp.where(m_, x[c:], 0.0), incl, g_k)

    same_blk = lambda h: (ti // h) == (si // h)
    n_d = each(lambda x: jnp.where(same_blk(INV_BASE), x, 0.0), n_ab)
    steps = max(1, (INV_BASE - 1).bit_length())
    q = n_d
    tinv = n_d
    q = each(lambda x: _dot(x, stk(x)), q) if steps > 1 else q
    for j in range(1, steps):
        last = j == steps - 1
        qs = each(stk, q)
        if last:
            z = each(_dot, tinv, qs)
            tinv = each(lambda t_, x, y: t_ + x + y, tinv, q, z)
        else:
            both = each(lambda x, t_, w_: _dot(jnp.concatenate([x, t_], axis=0), w_), q, tinv, qs)
            tinv = each(lambda t_, x, y: t_ + x + y[c:], tinv, q, both)
            q = each(lambda y: y[:c], both)
    h = INV_BASE
    while h < c:
        link = same_blk(2 * h) & jnp.logical_not(same_blk(h))
        n_off = each(lambda x: jnp.where(link, x, 0.0), n_ab)
        y = each(lambda t_, x: x + _dot(t_, stk(x)), tinv, n_off)
        tinv = each(lambda t_, x: t_ + x + _dot(x, stk(t_)), tinv, y)
        h *= 2
    a_bar = each(lambda t_, x: x + _dot(t_, stk(x)), tinv, a_t)
    lv = each(lambda l_, p, y: _dot(jnp.concatenate([l_, p], axis=0), y), l_ak, p_rk, v_s)
    w = [x[:c] for x in lv]
    u0 = each(lambda t_, x: x + _dot(t_, stk(x)), tinv, w)
    r_bar = each(lambda x, p, y: x + _dot(p, stk(y)), r_t, p_rb, a_bar)
    y0 = each(lambda p, x, z: _dot(p, stk(x)) + z[c:], p_rb, u0, lv)
    zero = jnp.zeros((c, LANES), F32)
    ms = each(lambda ab, u, v_, bh, kh_: _dot_tn(
        jnp.concatenate([jnp.concatenate([ab, u], axis=1), jnp.concatenate([zero, v_], axis=1)], axis=0),
        jnp.concatenate([bh, kh_], axis=0)), a_bar, u0, vv, b_h, k_h)
    m = [jnp.where(same_head, x[:LANES], 0.0) for x in ms]
    s0p = [jnp.where(same_head, x[LANES:], 0.0) for x in ms]
    return [(r_bar[i], y0[i], m[i], s0p[i], jnp.exp(tot[i]), bonus[i]) for i in range(n)]


def _rwkv_scan_kernel(*refs, has_s0, emit_state):
    (r_ref, k_ref, v_ref, lwf_ref, lwb_ref, af_ref, ab_ref,
     kkf_ref, kkb_ref, kaf_ref, kab_ref, rkf_ref, rkb_ref, lnw_ref, lnb_ref) = refs[:15]
    s0_ref = refs[15] if has_s0 else None
    o_ref = refs[15 + has_s0]
    s_ref = refs[16 + has_s0] if emit_state else None
    rbar_ref, y0_ref, m_ref, s0p_ref, dec_ref, bonus_ref, yf_ref, yb_ref = refs[16 + has_s0 + emit_state:]
    bb, t = r_ref.shape[0], r_ref.shape[1]
    nc = t // CHUNK
    dirs = ((lwf_ref, af_ref, kkf_ref, kaf_ref, rkf_ref, False), (lwb_ref, ab_ref, kkb_ref, kab_ref, rkb_ref, True))
    group = min(bb * nc, PREP_GROUP)
    assert (bb * nc) % group == 0
    chunk_rows = lambda ci: pl.ds(pl.multiple_of(ci * CHUNK, CHUNK), CHUNK)

    def prep(gi, carry):
        insts, where = [], []
        at = [((gi * group + j) // nc, chunk_rows((gi * group + j) % nc)) for j in range(group)]
        for j, (e, sl) in enumerate(at):
            r, k, v = (ref[e, sl, :].astype(F32) for ref in (r_ref, k_ref, v_ref))
            for d, (lw_ref, a_ref, kk_ref, ka_ref, rk_ref, reverse) in enumerate(dirs):
                insts.append((r, k, v, lw_ref[e, sl, :], a_ref[e, sl, :].astype(F32), kk_ref[...], ka_ref[...],
                              rk_ref[...], reverse))
                where.append((d, gi * group + j, e, sl))
        terms = _rwkv_chunk_terms(insts)
        for (d, fi, e, sl), (r_bar, y0, m, s0p, dec, bon) in zip(where, terms):
            rbar_ref[d, fi] = r_bar.astype(BF16)
            y0_ref[d, fi] = y0
            m_ref[d, fi] = m.astype(BF16)
            s0p_ref[d, fi] = s0p
            dec_ref[d, fi] = jnp.broadcast_to(dec, (8, LANES))
        for j in range(group):
            _, _, e, sl = where[2 * j]
            bonus_ref[e, sl, :] = terms[2 * j][5] + terms[2 * j + 1][5]
        return carry

    lax.fori_loop(0, bb * nc // group, prep, 0)

    def seq(i, carry):
        cis = (i, nc - 1 - i)
        idx = [(d, e) for d in range(2) for e in range(bb)]
        sb16 = [s.astype(BF16) for s in carry]
        sm = [jnp.dot(sb16[n], m_ref[d, e * nc + cis[d]], preferred_element_type=F32) for n, (d, e) in enumerate(idx)]
        ys = [_dot_nt(rbar_ref[d, e * nc + cis[d]], sb16[n]) + y0_ref[d, e * nc + cis[d]] for n, (d, e) in enumerate(idx)]
        for n, (d, e) in enumerate(idx):
            (yf_ref, yb_ref)[d][e, chunk_rows(cis[d]), :] = ys[n]
        return tuple(carry[n] * dec_ref[d, e * nc + cis[d]][0:1] + sm[n] + s0p_ref[d, e * nc + cis[d]]
                     for n, (d, e) in enumerate(idx))

    if has_s0:
        init = tuple(s0_ref[e, d, 0] for d in range(2) for e in range(bb))
    else:
        init = tuple(jnp.zeros((LANES, LANES), F32) for _ in range(2 * bb))
    final = lax.fori_loop(0, nc, seq, init)
    if emit_state:
        for d in range(2):
            for e in range(bb):
                s = final[d * bb + e]
                s_ref[e, d, 0] = s[:RWKV_HEAD, :RWKV_HEAD]
                s_ref[e, d, 1] = pltpu.roll(s, RWKV_HEAD, 1)[RWKV_HEAD:, :RWKV_HEAD]

    prow = lax.broadcasted_iota(jnp.int32, (LANES, LANES), 0)
    pcol = lax.broadcasted_iota(jnp.int32, (LANES, LANES), 1)
    head_ones = ((prow >= RWKV_HEAD) == (pcol >= RWKV_HEAD)).astype(BF16)
    inv_n = 1.0 / RWKV_HEAD

    def head_mean(x):
        hi = x.astype(BF16)
        lo = (x - hi.astype(F32)).astype(BF16)
        both = jnp.dot(jnp.concatenate([hi, lo], axis=0), head_ones, preferred_element_type=F32)
        return (both[:x.shape[0]] + both[x.shape[0]:]) * inv_n

    def finish(gi, carry):
        at = [((gi * group + j) // nc, chunk_rows((gi * group + j) % nc)) for j in range(group)]
        y = [yf_ref[e, sl, :] + yb_ref[e, sl, :] for e, sl in at]
        yc = [x - head_mean(x) for x in y]
        var = [head_mean(x * x) for x in yc]
        for (e, sl), x, vr in zip(at, yc, var):
            o_ref[e, sl, :] = (x * lax.rsqrt(vr + LNX_EPS) * lnw_ref[...] + lnb_ref[...]
                               + bonus_ref[e, sl, :]).astype(BF16)
        return carry

    lax.fori_loop(0, bb * nc // group, finish, 0)


def _rwkv_scan(r, k, v, lw, a, k_k, k_a, r_k, lnx_w, lnx_b, s0, bb, emit_state):
    bsz, t, d = r.shape
    nc = t // CHUNK
    seq = lambda b, p: (b, 0, p)
    seq_b = lambda b, p: (b, 0, RWKV_PAIRS + p)
    par = lambda b, p: (0, p)
    par_b = lambda b, p: (0, RWKV_PAIRS + p)
    sblk = pl.BlockSpec((bb, t, LANES), seq)
    sblk_b = pl.BlockSpec((bb, t, LANES), seq_b)
    pblk = pl.BlockSpec((1, LANES), par)
    pblk_b = pl.BlockSpec((1, LANES), par_b)
    s0blk = pl.BlockSpec((bb, 2, 1, LANES, LANES), lambda b, p: (b, 0, p, 0, 0))
    s_blk = pl.BlockSpec((bb, 2, 2, RWKV_HEAD, RWKV_HEAD), lambda b, p: (b, 0, p, 0, 0))
    s_shape = jax.ShapeDtypeStruct((bsz, 2, RWKV_HEADS, RWKV_HEAD, RWKV_HEAD), F32)
    has_s0 = s0 is not None
    k_k, k_a, r_k = (p.reshape(1, 2 * d) for p in (k_k, k_a, r_k))
    return pl.pallas_call(
        functools.partial(_rwkv_scan_kernel, has_s0=has_s0, emit_state=emit_state),
        grid=(bsz // bb, RWKV_PAIRS),
        in_specs=[sblk, sblk, sblk, sblk, sblk_b, sblk, sblk_b,
                  pblk, pblk_b, pblk, pblk_b, pblk, pblk_b, pblk, pblk] + [s0blk] * has_s0,
        out_specs=[sblk] + [s_blk] * emit_state,
        out_shape=[jax.ShapeDtypeStruct((bsz, t, d), BF16)] + [s_shape] * emit_state,
        scratch_shapes=[pltpu.VMEM((2, bb * nc, CHUNK, LANES), BF16), pltpu.VMEM((2, bb * nc, CHUNK, LANES), F32),
                        pltpu.VMEM((2, bb * nc, LANES, LANES), BF16), pltpu.VMEM((2, bb * nc, LANES, LANES), F32),
                        pltpu.VMEM((2, bb * nc, 8, LANES), F32), pltpu.VMEM((bb, t, LANES), F32),
                        pltpu.VMEM((bb, t, LANES), F32), pltpu.VMEM((bb, t, LANES), F32)],
        compiler_params=_params(2),
        name="rwkv_scan",
    )(r, k, v, lw, lw, a, a, k_k, k_k, k_a, k_a, r_k, r_k,
      lnx_w.reshape(1, d), lnx_b.reshape(1, d), *([s0] * has_s0))


def _rwkv_back_kernel(o_ref, g_ref, x_ref, mod_ref, wout_ref, nf_ref, y_ref):
    out = _dot(o_ref[...].astype(F32) * _silu(g_ref[...].astype(F32)), wout_ref[...])
    x2 = x_ref[...] + mod_ref[0][:, 2 * D_MODEL:] * out
    y_ref[...] = _rms(x2) * nf_ref[...]


def _rwkv_back(o, g, x, mod1, mod_idx, tm, w_out, norm_f):
    n, d = x.shape
    nmod = mod1.shape[0]
    row = lambda i: (i, 0)
    const = lambda i: (0, 0)
    return pl.pallas_call(
        _rwkv_back_kernel,
        grid=(n // tm,),
        in_specs=[pl.BlockSpec((tm, d), row), pl.BlockSpec((tm, d), row), pl.BlockSpec((tm, d), row),
                  pl.BlockSpec((1, 1, 3 * d), lambda i: (mod_idx(i), 0, 0)),
                  pl.BlockSpec((d, d), const), pl.BlockSpec((1, d), const)],
        out_specs=pl.BlockSpec((tm, d), row),
        out_shape=jax.ShapeDtypeStruct((n, d), F32),
        compiler_params=_params(1),
        name="rwkv_back",
    )(o, g, x, mod1.reshape(nmod, 1, 3 * d), w_out.astype(BF16), norm_f.reshape(1, d))


def _pair_blockdiag(s):
    b = s.shape[0]
    s = s.reshape(b, 2, RWKV_PAIRS, 2, RWKV_HEAD, RWKV_HEAD)
    z = jnp.zeros_like(s[:, :, :, 0])
    top = jnp.concatenate([s[:, :, :, 0], z], axis=-1)
    bot = jnp.concatenate([z, s[:, :, :, 1]], axis=-1)
    return jnp.concatenate([top, bot], axis=-2)


def _tile_rows(n_seq, t, may_span):
    if t % TOKEN_TILE == 0:
        return TOKEN_TILE
    if may_span and TOKEN_TILE % t == 0 and (n_seq * t) % TOKEN_TILE == 0:
        return TOKEN_TILE
    return t


def kernel(x_prompt, x_sample, state_gla, state_rwkv, c, c_ctx, w_mod, b_mod, norm_w, gla_w_in, gla_w_a1,
           gla_w_a2, gla_b_a, gla_norm, gla_w_out, rwkv_mu, rwkv_w_rkvg, rwkv_w0, rwkv_w1, rwkv_w2, rwkv_a0,
           rwkv_a1, rwkv_a2, rwkv_k_k, rwkv_k_a, rwkv_r_k, rwkv_lnx_w, rwkv_lnx_b, rwkv_w_out, norm_f):
    d = D_MODEL
    bp, tp, _ = x_prompt.shape
    bs, ts, _ = x_sample.shape
    assert tp % CHUNK == 0 and ts % CHUNK == 0 and ts % GRID_W == 0

    nrows = -(-(1 + bs) // 8) * 8
    cond = jnp.zeros((nrows, d), F32).at[0].set(c_ctx).at[1:1 + bs].set(c)
    cond = cond * (1.0 / (1.0 + jnp.exp(-cond)))
    mod0 = _matmul_bias(cond, w_mod[0], b_mod[0].reshape(1, 3 * d))
    mod1 = _matmul_bias(cond, w_mod[1], b_mod[1].reshape(1, 3 * d))

    def trunk(x3, tm, mod_idx, grid_w, gla_s0, rwkv_s0, rwkv_bb, emit_state):
        b, t, _ = x3.shape
        n = b * t
        x = x3.reshape(n, d)
        seq = lambda arr: arr.reshape(b, t, arr.shape[-1])
        q, k, v, g, lg = _gla_front(x, mod0, mod_idx, tm, norm_w[0], gla_w_in[0], gla_w_a1[0], gla_w_a2[0], gla_b_a[0])
        o, *gla_s = _gla_scan(seq(q), seq(k), seq(v), seq(lg), gla_s0, GLA_HEADS, emit_state)
        x1, h1 = _gla_back(o.reshape(n, d), g, x, mod0, mod1, mod_idx, tm, gla_norm[0], gla_w_out[0], norm_w[1])
        r, k, v, g, lw, a = _rwkv_front(h1, tm, t, grid_w, rwkv_mu[0], rwkv_w_rkvg[0], rwkv_w0[0], rwkv_w1[0], rwkv_w2[0],
                                        rwkv_a0[0], rwkv_a1[0], rwkv_a2[0])
        o, *rwkv_s = _rwkv_scan(seq(r), seq(k), seq(v), seq(lw), seq(a), rwkv_k_k[0], rwkv_k_a[0], rwkv_r_k[0],
                                rwkv_lnx_w[0], rwkv_lnx_b[0], rwkv_s0, rwkv_bb, emit_state)
        y = _rwkv_back(o.reshape(n, d), g, x1, mod1, mod_idx, tm, rwkv_w_out[0], norm_f)
        return y.reshape(b, t, d), gla_s, rwkv_s

    y_prompt, (gla_s,), (rwkv_s,) = trunk(x_prompt, _tile_rows(bp, tp, True), lambda i: 0, None, None, None,
                                          min(bp, 4), True)

    tm_s = _tile_rows(bs, ts, False)
    y_sample, _, _ = trunk(x_sample, tm_s, lambda i: 1 + i // (ts // tm_s), GRID_W, state_gla[:, 0],
                           _pair_blockdiag(state_rwkv[:, 0]), min(bs, 4), False)
    return (y_prompt, y_sample, gla_s[:, None], rwkv_s[:, None])
```

```python
import functools

import jax
import jax.numpy as jnp
from jax import lax
from jax.experimental import pallas as pl
from jax.experimental.pallas import tpu as pltpu

F32 = jnp.float32
BF16 = jnp.bfloat16

D_MODEL = 1024
EPS = 1e-6
GRID_W = 64
GLA_HEADS = 4
GLA_DK = 512
GLA_DV = 1024
GLA_DKH = 128
GLA_DVH = 256
GLA_GATE_RANK = 16
GLA_GATE_NORM = 16.0
RWKV_HEAD = 64
RWKV_HEADS = 16
RWKV_PAIRS = 8
RWKV_RANK = 64
LNX_EPS = 64e-5
RWKV_DECAY_SCALE = 0.6065306597126334
CHUNK = 64
GLA_STEPS = 2
INV_BASE = 8
TOKEN_TILE = 512
PREP_GROUP = 8
LANES = 128
VMEM_LIMIT = 56 * 1024 * 1024


def _dot(a, b):
    return jnp.dot(a.astype(BF16), b.astype(BF16), preferred_element_type=F32)


def _dot_nt(a, b):
    return lax.dot_general(a.astype(BF16), b.astype(BF16), (((1,), (1,)), ((), ())),
                           preferred_element_type=F32)


def _dot_tn(a, b):
    return lax.dot_general(a.astype(BF16), b.astype(BF16), (((0,), (0,)), ((), ())),
                           preferred_element_type=F32)


def _tri_dot(tri, x):
    hi = x.astype(BF16)
    lo = (x - hi.astype(F32)).astype(BF16)
    t = tri.astype(BF16)
    return jnp.dot(t, hi, preferred_element_type=F32) + jnp.dot(t, lo, preferred_element_type=F32)


def _log_sigmoid(z):
    return jnp.minimum(z, 0.0) - jnp.log(1.0 + jnp.exp(-jnp.abs(z)))


def _sigmoid(z):
    return 1.0 / (1.0 + jnp.exp(-z))


def _silu(z):
    return z * _sigmoid(z)


def _rms(x):
    return x * lax.rsqrt(jnp.mean(x * x, axis=-1, keepdims=True) + EPS)


def _params(n_grid_dims):
    return pltpu.CompilerParams(dimension_semantics=("arbitrary",) * n_grid_dims,
                                vmem_limit_bytes=VMEM_LIMIT)


def _mm_kernel(x_ref, w_ref, b_ref, o_ref):
    o_ref[...] = _dot(x_ref[...], w_ref[...]) + b_ref[...]


def _matmul_bias(x, w, b):
    m, k = x.shape
    n = w.shape[1]
    return pl.pallas_call(
        _mm_kernel,
        grid=(1,),
        in_specs=[pl.BlockSpec((m, k), lambda i: (0, 0)),
                  pl.BlockSpec((k, n), lambda i: (0, 0)),
                  pl.BlockSpec((1, n), lambda i: (0, 0))],
        out_specs=pl.BlockSpec((m, n), lambda i: (0, 0)),
        out_shape=jax.ShapeDtypeStruct((m, n), F32),
        compiler_params=_params(1),
        name="mod_matmul",
    )(x, w.astype(BF16), b)


def _gla_front_kernel(x_ref, mod_ref, nw_ref, win_ref, wa1_ref, wa2_ref, ba_ref,
                      q_ref, k_ref, v_ref, g_ref, lg_ref):
    d = D_MODEL
    mod = mod_ref[0]
    shift, scale = mod[:, :d], mod[:, d:2 * d]
    h = _rms(x_ref[...]) * nw_ref[...] * (1.0 + scale) + shift
    hb = h.astype(BF16)
    proj = lambda lo, hi: jnp.dot(hb, win_ref[:, lo:hi], preferred_element_type=F32)
    q_ref[...] = (proj(0, GLA_DK) * (GLA_DKH ** -0.5)).astype(BF16)
    k_ref[...] = proj(GLA_DK, 2 * GLA_DK).astype(BF16)
    v_ref[...] = proj(2 * GLA_DK, 2 * GLA_DK + GLA_DV).astype(BF16)
    g_ref[...] = proj(2 * GLA_DK + GLA_DV, 2 * GLA_DK + 2 * GLA_DV).astype(BF16)
    t = jnp.dot(hb, wa1_ref[...], preferred_element_type=F32)
    z = _dot(t, wa2_ref[...]) + ba_ref[...]
    lg_ref[...] = _log_sigmoid(z) * (1.0 / GLA_GATE_NORM)


def _gla_front(x, mod, mod_idx, tm, norm_w, w_in, w_a1, w_a2, b_a):
    n, d = x.shape
    nmod = mod.shape[0]
    wa1 = jnp.zeros((d, LANES), F32).at[:, :GLA_GATE_RANK].set(w_a1[0]).at[:, GLA_GATE_RANK:2 * GLA_GATE_RANK].set(w_a1[1])
    wa2 = jnp.zeros((LANES, 2 * GLA_DK), F32).at[:GLA_GATE_RANK, :GLA_DK].set(w_a2[0])
    wa2 = wa2.at[GLA_GATE_RANK:2 * GLA_GATE_RANK, GLA_DK:].set(w_a2[1])
    ba = b_a.reshape(1, 2 * GLA_DK)
    row = lambda i: (i, 0)
    const = lambda i: (0, 0)
    n_in = w_in.shape[1]
    outs = pl.pallas_call(
        _gla_front_kernel,
        grid=(n // tm,),
        in_specs=[pl.BlockSpec((tm, d), row),
                  pl.BlockSpec((1, 1, 3 * d), lambda i: (mod_idx(i), 0, 0)),
                  pl.BlockSpec((1, d), const),
                  pl.BlockSpec((d, n_in), const),
                  pl.BlockSpec((d, LANES), const),
                  pl.BlockSpec((LANES, 2 * GLA_DK), const),
                  pl.BlockSpec((1, 2 * GLA_DK), const)],
        out_specs=[pl.BlockSpec((tm, GLA_DK), row), pl.BlockSpec((tm, GLA_DK), row),
                   pl.BlockSpec((tm, GLA_DV), row), pl.BlockSpec((tm, GLA_DV), row),
                   pl.BlockSpec((tm, 2 * GLA_DK), row)],
        out_shape=[jax.ShapeDtypeStruct((n, GLA_DK), BF16), jax.ShapeDtypeStruct((n, GLA_DK), BF16),
                   jax.ShapeDtypeStruct((n, GLA_DV), BF16), jax.ShapeDtypeStruct((n, GLA_DV), BF16),
                   jax.ShapeDtypeStruct((n, 2 * GLA_DK), F32)],
        compiler_params=_params(1),
        name="gla_front",
    )(x, mod.reshape(nmod, 1, 3 * d), norm_w.reshape(1, d), w_in.astype(BF16), wa1.astype(BF16),
      wa2.astype(BF16), ba)
    return outs


def _gla_chunks(chains):
    flat = [(x, ch[2]) for ch in chains for x in ch[0]]
    c = flat[0][0][0].shape[0]
    dk = flat[0][0][1].shape[1]
    row = lax.broadcasted_iota(jnp.int32, (c, c), 0)
    col = lax.broadcasted_iota(jnp.int32, (c, c), 1)
    keep = [(col >= row) if rv else (col <= row) for _, rv in flat]
    b = [_tri_dot(kp.astype(F32), x[3]) for kp, (x, _) in zip(keep, flat)]
    b_end = [bb[0:1] if rv else bb[c - 1:c] for bb, (_, rv) in zip(b, flat)]
    qb = [(x[0] * jnp.exp(bb)).astype(BF16) for (x, _), bb in zip(flat, b)]
    kb = [x[1] * jnp.exp(-bb) for (x, _), bb in zip(flat, b)]
    kd = [x[1] * jnp.exp(be - bb) for (x, _), bb, be in zip(flat, b, b_end)]
    scores = [jnp.where(kp, _dot_nt(x, y), 0.0).astype(BF16) for kp, x, y in zip(keep, qb, kb)]
    upd = [_dot_tn(x, y[2]) for (y, _), x in zip(flat, kd)]
    decay = [jnp.broadcast_to(jnp.exp(be), (dk, dk)).T for be in b_end]
    n_steps = len(chains[0][0])
    st = [ch[1] for ch in chains]
    outs = [[] for _ in chains]
    for s in range(n_steps):
        at = [ci * n_steps + s for ci in range(len(chains))]
        for ci, f in enumerate(at):
            outs[ci].append(_dot(jnp.concatenate([scores[f], qb[f]], axis=1),
                                 jnp.concatenate([flat[f][0][2], st[ci].astype(BF16)], axis=0)))
        st = [st[ci] * jnp.concatenate([decay[f]] * (st[ci].shape[1] // dk), axis=1) + upd[f]
              for ci, f in enumerate(at)]
    return list(zip(outs, st))


def _gla_scan_kernel(*refs, has_s0, emit_state):
    q_ref, k_ref, v_ref, lf_ref, lb_ref = refs[:5]
    s0_ref = refs[5] if has_s0 else None
    o_ref = refs[5 + has_s0]
    s_ref = refs[6 + has_s0] if emit_state else None
    acc_ref = refs[-1]
    t = q_ref.shape[1]
    hb = q_ref.shape[2] // GLA_DKH
    nc = t // CHUNK
    half = nc // 2
    kh = lambda h: slice(h * GLA_DKH, (h + 1) * GLA_DKH)
    vh = lambda h: slice(h * GLA_DVH, (h + 1) * GLA_DVH)

    steps = GLA_STEPS if half % GLA_STEPS == 0 else 1

    def both(accumulate):
        def body(i, carry):
            sls = [[pl.ds(pl.multiple_of(ci * CHUNK, CHUNK), CHUNK)
                    for ci in ((i * steps + s, nc - 1 - i * steps - s)[d] for s in range(steps))] for d in range(2)]
            chains = [([(q_ref[0, sl, kh(h)].astype(F32), k_ref[0, sl, kh(h)].astype(F32), v_ref[0, sl, vh(h)],
                         (lf_ref, lb_ref)[d][0, sl, kh(h)]) for sl in sls[d]], carry[d * hb + h], d == 1)
                      for d in range(2) for h in range(hb)]
            res = _gla_chunks(chains)
            for d in range(2):
                for h in range(hb):
                    for sl, o in zip(sls[d], res[d * hb + h][0]):
                        if accumulate:
                            o_ref[0, sl, vh(h)] = (acc_ref[sl, vh(h)] + o).astype(BF16)
                        else:
                            acc_ref[sl, vh(h)] = o
            return tuple(st for _, st in res)
        return body

    if has_s0:
        init = tuple(s0_ref[0, d, h] for d in range(2) for h in range(hb))
    else:
        init = tuple(jnp.zeros((GLA_DKH, GLA_DVH), F32) for _ in range(2 * hb))
    carry = lax.fori_loop(0, half // steps, both(False), init)
    carry = lax.fori_loop(half // steps, nc // steps, both(True), carry)
    if emit_state:
        for d in range(2):
            for h in range(hb):
                s_ref[0, d, h] = carry[d * hb + h]


def _gla_scan(q, k, v, lg, s0, hb, emit_state):
    bsz, t, _ = q.shape
    ng = GLA_HEADS // hb
    hd = lambda b, h: (b, 0, h)
    kblk = pl.BlockSpec((1, t, hb * GLA_DKH), hd)
    vblk = pl.BlockSpec((1, t, hb * GLA_DVH), hd)
    sblk = pl.BlockSpec((1, 2, hb, GLA_DKH, GLA_DVH), lambda b, h: (b, 0, h, 0, 0))
    s_shape = jax.ShapeDtypeStruct((bsz, 2, GLA_HEADS, GLA_DKH, GLA_DVH), F32)
    has_s0 = s0 is not None
    return pl.pallas_call(
        functools.partial(_gla_scan_kernel, has_s0=has_s0, emit_state=emit_state),
        grid=(bsz, ng),
        in_specs=[kblk, kblk, vblk, kblk, pl.BlockSpec((1, t, hb * GLA_DKH), lambda b, h: (b, 0, ng + h))]
        + [sblk] * has_s0,
        out_specs=[vblk] + [sblk] * emit_state,
        out_shape=[jax.ShapeDtypeStruct((bsz, t, GLA_DV), BF16)] + [s_shape] * emit_state,
        scratch_shapes=[pltpu.VMEM((t, hb * GLA_DVH), F32)],
        compiler_params=_params(2),
        name="gla_scan",
    )(q, k, v, lg, lg, *([s0] * has_s0))


def _gla_back_kernel(o_ref, g_ref, x_ref, mod0_ref, mod1_ref, gn_ref, wout_ref, nw1_ref, x1_ref, h1_ref):
    d = D_MODEL
    o = o_ref[...].astype(F32)
    gn = gn_ref[...]
    parts = [_rms(o[:, h * GLA_DVH:(h + 1) * GLA_DVH]) * gn for h in range(GLA_HEADS)]
    on = jnp.concatenate(parts, axis=-1) * _silu(g_ref[...].astype(F32))
    out = _dot(on, wout_ref[...])
    gate = mod0_ref[0][:, 2 * d:]
    x1 = x_ref[...] + gate * out
    x1_ref[...] = x1
    mod1 = mod1_ref[0]
    h1_ref[...] = (_rms(x1) * nw1_ref[...] * (1.0 + mod1[:, d:2 * d]) + mod1[:, :d]).astype(BF16)


def _gla_back(o, g, x, mod0, mod1, mod_idx, tm, gla_norm, w_out, norm_w1):
    n, d = x.shape
    nmod = mod0.shape[0]
    row = lambda i: (i, 0)
    const = lambda i: (0, 0)
    modspec = pl.BlockSpec((1, 1, 3 * d), lambda i: (mod_idx(i), 0, 0))
    return pl.pallas_call(
        _gla_back_kernel,
        grid=(n // tm,),
        in_specs=[pl.BlockSpec((tm, d), row), pl.BlockSpec((tm, d), row), pl.BlockSpec((tm, d), row),
                  modspec, modspec,
                  pl.BlockSpec((1, GLA_DVH), const), pl.BlockSpec((d, d), const), pl.BlockSpec((1, d), const)],
        out_specs=[pl.BlockSpec((tm, d), row), pl.BlockSpec((tm, d), row)],
        out_shape=[jax.ShapeDtypeStruct((n, d), F32), jax.ShapeDtypeStruct((n, d), BF16)],
        compiler_params=_params(1),
        name="gla_back",
    )(o, g, x, mod0.reshape(nmod, 1, 3 * d), mod1.reshape(nmod, 1, 3 * d), gla_norm.reshape(1, GLA_DVH),
      w_out.astype(BF16), norm_w1.reshape(1, d))


def _token_shift(h, up_ref, dn_ref, seq_len, grid_w):
    tm = h.shape[0]
    pos = lax.broadcasted_iota(jnp.int32, (tm, 1), 0) + pl.program_id(0) * tm
    prev = pltpu.roll(h, 1, 0)
    nxt = pltpu.roll(h, tm - 1, 0)
    t = pos % seq_len
    if grid_w is None:
        return 0.5 * (jnp.where(t == 0, 0.0, prev) + jnp.where(t == seq_len - 1, 0.0, nxt))
    col = pos % grid_w
    up = jnp.concatenate([up_ref[...].astype(F32), h[:tm - grid_w]], axis=0)
    dn = jnp.concatenate([h[grid_w:], dn_ref[...].astype(F32)], axis=0)
    up = jnp.where(t < grid_w, 0.0, up)
    dn = jnp.where(t >= seq_len - grid_w, 0.0, dn)
    left = jnp.where(col == 0, 0.0, prev)
    right = jnp.where(col == grid_w - 1, 0.0, nxt)
    return 0.25 * (up + dn + left + right)


def _rwkv_front_kernel(*refs, seq_len, grid_w):
    n_h = 1 if grid_w is None else 3
    h_ref, up_ref, dn_ref = (tuple(refs[:n_h]) + (None, None))[:3]
    (mu_ref, wrkvg_ref, w1_ref, w2_ref, w0_ref, a1_ref, a2_ref, a0_ref,
     r_ref, k_ref, v_ref, g_ref, lw_ref, a_ref) = refs[n_h:]
    h = h_ref[...].astype(F32)
    dh = _token_shift(h, up_ref, dn_ref, seq_len, grid_w) - h
    mix = lambda p: (h + dh * mu_ref[p:p + 1, :]).astype(BF16)
    r_ref[...] = jnp.dot(mix(0), wrkvg_ref[0], preferred_element_type=F32).astype(BF16)
    k_ref[...] = jnp.dot(mix(1), wrkvg_ref[1], preferred_element_type=F32).astype(BF16)
    v_ref[...] = jnp.dot(mix(2), wrkvg_ref[2], preferred_element_type=F32).astype(BF16)
    g_ref[...] = jnp.dot(mix(3), wrkvg_ref[3], preferred_element_type=F32).astype(BF16)
    tw = jnp.tanh(jnp.dot(mix(4), w1_ref[...], preferred_element_type=F32))
    lw_ref[...] = -RWKV_DECAY_SCALE * _sigmoid(w0_ref[...] + _dot(tw, w2_ref[...]))
    ta = jnp.dot(mix(5), a1_ref[...], preferred_element_type=F32)
    a_ref[...] = _sigmoid(a0_ref[...] + _dot(ta, a2_ref[...])).astype(BF16)


def _rwkv_front(h, tm, seq_len, grid_w, mu, w_rkvg, w0, w1, w2, a0, a1, a2):
    n, d = h.shape
    rk = RWKV_RANK
    cat1 = lambda w: jnp.concatenate([w[0], w[1]], axis=1)
    bd2 = lambda w: jnp.zeros((2 * rk, 2 * d), F32).at[:rk, :d].set(w[0]).at[rk:, d:].set(w[1])
    row = lambda i: (i, 0)
    const = lambda i: (0, 0)
    if grid_w is None:
        assert tm % seq_len == 0
        h_specs, h_args = [pl.BlockSpec((tm, d), row)], [h]
    else:
        assert tm % grid_w == 0 and seq_len % tm == 0
        rows_per_tile, last = tm // grid_w, n // grid_w - 1
        h_specs = [pl.BlockSpec((tm, d), row),
                   pl.BlockSpec((grid_w, d), lambda i: (jnp.maximum(i * rows_per_tile - 1, 0), 0)),
                   pl.BlockSpec((grid_w, d), lambda i: (jnp.minimum((i + 1) * rows_per_tile, last), 0))]
        h_args = [h, h, h]
    return pl.pallas_call(
        functools.partial(_rwkv_front_kernel, seq_len=seq_len, grid_w=grid_w),
        grid=(n // tm,),
        in_specs=h_specs + [
            pl.BlockSpec((8, d), const),
            pl.BlockSpec((4, d, d), lambda i: (0, 0, 0)),
            pl.BlockSpec((d, 2 * rk), const), pl.BlockSpec((2 * rk, 2 * d), const), pl.BlockSpec((1, 2 * d), const),
            pl.BlockSpec((d, 2 * rk), const), pl.BlockSpec((2 * rk, 2 * d), const), pl.BlockSpec((1, 2 * d), const)],
        out_specs=[pl.BlockSpec((tm, d), row)] * 4 + [pl.BlockSpec((tm, 2 * d), row)] * 2,
        out_shape=[jax.ShapeDtypeStruct((n, d), BF16)] * 4
        + [jax.ShapeDtypeStruct((n, 2 * d), F32), jax.ShapeDtypeStruct((n, 2 * d), BF16)],
        compiler_params=_params(1),
        name="rwkv_front",
    )(*h_args, jnp.zeros((8, d), F32).at[:6].set(mu), w_rkvg.astype(BF16),
      cat1(w1).astype(BF16), bd2(w2).astype(BF16), w0.reshape(1, 2 * d),
      cat1(a1).astype(BF16), bd2(a2).astype(BF16), a0.reshape(1, 2 * d))


def _seg_sum(x, head0):
    s0 = jnp.sum(jnp.where(head0, x, 0.0), axis=-1, keepdims=True)
    s1 = jnp.sum(jnp.where(head0, 0.0, x), axis=-1, keepdims=True)
    return jnp.where(head0, s0, s1)


def _stack(x, head0):
    return jnp.concatenate([jnp.where(head0, x, 0.0), jnp.where(head0, 0.0, x)], axis=0)


def _rwkv_chunk_terms(insts):
    n = len(insts)
    c = insts[0][0].shape[0]
    lane = lax.broadcasted_iota(jnp.int32, (c, LANES), 1)
    head0 = lane < RWKV_HEAD
    row = lax.broadcasted_iota(jnp.int32, (c, c), 0)
    col = lax.broadcasted_iota(jnp.int32, (c, c), 1)
    ti = lax.broadcasted_iota(jnp.int32, (c, LANES), 0)
    si = lane & (c - 1)
    prow = lax.broadcasted_iota(jnp.int32, (LANES, LANES), 0)
    pcol = lax.broadcasted_iota(jnp.int32, (LANES, LANES), 1)
    same_head = (prow >= RWKV_HEAD) == (pcol >= RWKV_HEAD)
    each = lambda f, *ls: [f(*xs) for xs in zip(*ls)]
    rev = [x[8] for x in insts]
    stk = lambda x: _stack(x, head0).astype(BF16)

    cum = [_tri_dot(((col >= row) if x[8] else (col <= row)).astype(F32), x[3]) for x in insts]
    tot = [cm[0:1] if rv else cm[c - 1:c] for cm, rv in zip(cum, rev)]

    a_t, r_t, b_ts, k_ts, b_h, k_h, v_s, vv, bonus = [], [], [], [], [], [], [], [], []
    for (r, k, v, lw, a, k_k, k_a, r_k, _), cm, tt in zip(insts, cum, tot):
        kk = k * k_k
        kk = kk / jnp.maximum(jnp.sqrt(_seg_sum(kk * kk, head0)), 1e-12)
        kd = k * (1.0 + (a - 1.0) * k_a)
        bv = kk * a
        bonus.append(_seg_sum(r * kd * r_k, head0) * v)
        e_neg = jnp.exp(-cm)
        e_tail = jnp.exp(tt - cm)
        a_t.append(-kk * jnp.exp(cm - lw))
        r_t.append(r * jnp.exp(cm))
        b_ts.append(stk(bv * e_neg))
        k_ts.append(stk(kd * e_neg))
        b_h.append(bv * e_tail)
        k_h.append(kd * e_tail)
        v_s.append(stk(v))
        vv.append(v)

    strict = [(si > ti) if rv else (si < ti) for rv in rev]
    incl = [(si >= ti) if rv else (si <= ti) for rv in rev]
    ar = each(lambda x, y: jnp.concatenate([x, y], axis=0).astype(BF16), a_t, r_t)
    g_b = each(_dot_nt, ar, b_ts)
    g_k = each(_dot_nt, ar, k_ts)
    n_ab = each(lambda m_, x: jnp.where(m_, x[:c], 0.0), strict, g_b)
    l_ak = each(lambda m_, x: jnp.where(m_, x[:c], 0.0), strict, g_k)
    p_rb = each(lambda m_, x: jnp.where(m_, x[c:], 0.0), incl, g_b)
    p_rk = each(lambda m_, x: jnp.where(m_, x[c:], 0.0), incl, g_k)

    same_blk = lambda h: (ti // h) == (si // h)
    n_d = each(lambda x: jnp.where(same_blk(INV_BASE), x, 0.0), n_ab)
    steps = max(1, (INV_BASE - 1).bit_length())
    q = n_d
    tinv = n_d
    q = each(lambda x: _dot(x, stk(x)), q) if steps > 1 else q
    for j in range(1, steps):
        last = j == steps - 1
        qs = each(stk, q)
        if last:
            z = each(_dot, tinv, qs)
            tinv = each(lambda t_, x, y: t_ + x + y, tinv, q, z)
        else:
            both = each(lambda x, t_, w_: _dot(jnp.concatenate([x, t_], axis=0), w_), q, tinv, qs)
            tinv = each(lambda t_, x, y: t_ + x + y[c:], tinv, q, both)
            q = each(lambda y: y[:c], both)
    h = INV_BASE
    while h < c:
        link = same_blk(2 * h) & jnp.logical_not(same_blk(h))
        n_off = each(lambda x: jnp.where(link, x, 0.0), n_ab)
        y = each(lambda t_, x: x + _dot(t_, stk(x)), tinv, n_off)
        tinv = each(lambda t_, x: t_ + x + _dot(x, stk(t_)), tinv, y)
        h *= 2
    a_bar = each(lambda t_, x: x + _dot(t_, stk(x)), tinv, a_t)
    lv = each(lambda l_, p, y: _dot(jnp.concatenate([l_, p], axis=0), y), l_ak, p_rk, v_s)
    w = [x[:c] for x in lv]
    u0 = each(lambda t_, x: x + _dot(t_, stk(x)), tinv, w)
    r_bar = each(lambda x, p, y: x + _dot(p, stk(y)), r_t, p_rb, a_bar)
    y0 = each(lambda p, x, z: _dot(p, stk(x)) + z[c:], p_rb, u0, lv)
    zero = jnp.zeros((c, LANES), F32)
    ms = each(lambda ab, u, v_, bh, kh_: _dot_tn(
        jnp.concatenate([jnp.concatenate([ab, u], axis=1), jnp.concatenate([zero, v_], axis=1)], axis=0),
        jnp.concatenate([bh, kh_], axis=0)), a_bar, u0, vv, b_h, k_h)
    m = [jnp.where(same_head, x[:LANES], 0.0) for x in ms]
    s0p = [jnp.where(same_head, x[LANES:], 0.0) for x in ms]
    return [(r_bar[i], y0[i], m[i], s0p[i], jnp.exp(tot[i]), bonus[i]) for i in range(n)]


def _rwkv_scan_kernel(*refs, has_s0, emit_state):
    (r_ref, k_ref, v_ref, lwf_ref, lwb_ref, af_ref, ab_ref,
     kkf_ref, kkb_ref, kaf_ref, kab_ref, rkf_ref, rkb_ref, lnw_ref, lnb_ref) = refs[:15]
    s0_ref = refs[15] if has_s0 else None
    o_ref = refs[15 + has_s0]
    s_ref = refs[16 + has_s0] if emit_state else None
    rbar_ref, y0_ref, m_ref, s0p_ref, dec_ref, bonus_ref, yf_ref, yb_ref = refs[16 + has_s0 + emit_state:]
    bb, t = r_ref.shape[0], r_ref.shape[1]
    nc = t // CHUNK
    dirs = ((lwf_ref, af_ref, kkf_ref, kaf_ref, rkf_ref, False), (lwb_ref, ab_ref, kkb_ref, kab_ref, rkb_ref, True))
    group = min(bb * nc, PREP_GROUP)
    assert (bb * nc) % group == 0
    chunk_rows = lambda ci: pl.ds(pl.multiple_of(ci * CHUNK, CHUNK), CHUNK)

    def prep(gi, carry):
        insts, where = [], []
        at = [((gi * group + j) // nc, chunk_rows((gi * group + j) % nc)) for j in range(group)]
        for j, (e, sl) in enumerate(at):
            r, k, v = (ref[e, sl, :].astype(F32) for ref in (r_ref, k_ref, v_ref))
            for d, (lw_ref, a_ref, kk_ref, ka_ref, rk_ref, reverse) in enumerate(dirs):
                insts.append((r, k, v, lw_ref[e, sl, :], a_ref[e, sl, :].astype(F32), kk_ref[...], ka_ref[...],
                              rk_ref[...], reverse))
                where.append((d, gi * group + j, e, sl))
        terms = _rwkv_chunk_terms(insts)
        for (d, fi, e, sl), (r_bar, y0, m, s0p, dec, bon) in zip(where, terms):
            rbar_ref[d, fi] = r_bar.astype(BF16)
            y0_ref[d, fi] = y0
            m_ref[d, fi] = m.astype(BF16)
            s0p_ref[d, fi] = s0p
            dec_ref[d, fi] = jnp.broadcast_to(dec, (8, LANES))
        for j in range(group):
            _, _, e, sl = where[2 * j]
            bonus_ref[e, sl, :] = terms[2 * j][5] + terms[2 * j + 1][5]
        return carry

    lax.fori_loop(0, bb * nc // group, prep, 0)

    def seq(i, carry):
        cis = (i, nc - 1 - i)
        idx = [(d, e) for d in range(2) for e in range(bb)]
        sb16 = [s.astype(BF16) for s in carry]
        sm = [jnp.dot(sb16[n], m_ref[d, e * nc + cis[d]], preferred_element_type=F32) for n, (d, e) in enumerate(idx)]
        ys = [_dot_nt(rbar_ref[d, e * nc + cis[d]], sb16[n]) + y0_ref[d, e * nc + cis[d]] for n, (d, e) in enumerate(idx)]
        for n, (d, e) in enumerate(idx):
            (yf_ref, yb_ref)[d][e, chunk_rows(cis[d]), :] = ys[n]
        return tuple(carry[n] * dec_ref[d, e * nc + cis[d]][0:1] + sm[n] + s0p_ref[d, e * nc + cis[d]]
                     for n, (d, e) in enumerate(idx))

    if has_s0:
        init = tuple(s0_ref[e, d, 0] for d in range(2) for e in range(bb))
    else:
        init = tuple(jnp.zeros((LANES, LANES), F32) for _ in range(2 * bb))
    final = lax.fori_loop(0, nc, seq, init)
    if emit_state:
        for d in range(2):
            for e in range(bb):
                s = final[d * bb + e]
                s_ref[e, d, 0] = s[:RWKV_HEAD, :RWKV_HEAD]
                s_ref[e, d, 1] = pltpu.roll(s, RWKV_HEAD, 1)[RWKV_HEAD:, :RWKV_HEAD]

    prow = lax.broadcasted_iota(jnp.int32, (LANES, LANES), 0)
    pcol = lax.broadcasted_iota(jnp.int32, (LANES, LANES), 1)
    head_ones = ((prow >= RWKV_HEAD) == (pcol >= RWKV_HEAD)).astype(BF16)
    inv_n = 1.0 / RWKV_HEAD

    def head_mean(x):
        hi = x.astype(BF16)
        lo = (x - hi.astype(F32)).astype(BF16)
        both = jnp.dot(jnp.concatenate([hi, lo], axis=0), head_ones, preferred_element_type=F32)
        return (both[:x.shape[0]] + both[x.shape[0]:]) * inv_n

    def finish(gi, carry):
        at = [((gi * group + j) // nc, chunk_rows((gi * group + j) % nc)) for j in range(group)]
        y = [yf_ref[e, sl, :] + yb_ref[e, sl, :] for e, sl in at]
        yc = [x - head_mean(x) for x in y]
        var = [head_mean(x * x) for x in yc]
        for (e, sl), x, vr in zip(at, yc, var):
            o_ref[e, sl, :] = (x * lax.rsqrt(vr + LNX_EPS) * lnw_ref[...] + lnb_ref[...]
                               + bonus_ref[e, sl, :]).astype(BF16)
        return carry

    lax.fori_loop(0, bb * nc // group, finish, 0)


def _rwkv_scan(r, k, v, lw, a, k_k, k_a, r_k, lnx_w, lnx_b, s0, bb, emit_state):
    bsz, t, d = r.shape
    nc = t // CHUNK
    seq = lambda b, p: (b, 0, p)
    seq_b = lambda b, p: (b, 0, RWKV_PAIRS + p)
    par = lambda b, p: (0, p)
    par_b = lambda b, p: (0, RWKV_PAIRS + p)
    sblk = pl.BlockSpec((bb, t, LANES), seq)
    sblk_b = pl.BlockSpec((bb, t, LANES), seq_b)
    pblk = pl.BlockSpec((1, LANES), par)
    pblk_b = pl.BlockSpec((1, LANES), par_b)
    s0blk = pl.BlockSpec((bb, 2, 1, LANES, LANES), lambda b, p: (b, 0, p, 0, 0))
    s_blk = pl.BlockSpec((bb, 2, 2, RWKV_HEAD, RWKV_HEAD), lambda b, p: (b, 0, p, 0, 0))
    s_shape = jax.ShapeDtypeStruct((bsz, 2, RWKV_HEADS, RWKV_HEAD, RWKV_HEAD), F32)
    has_s0 = s0 is not None
    k_k, k_a, r_k = (p.reshape(1, 2 * d) for p in (k_k, k_a, r_k))
    return pl.pallas_call(
        functools.partial(_rwkv_scan_kernel, has_s0=has_s0, emit_state=emit_state),
        grid=(bsz // bb, RWKV_PAIRS),
        in_specs=[sblk, sblk, sblk, sblk, sblk_b, sblk, sblk_b,
                  pblk, pblk_b, pblk, pblk_b, pblk, pblk_b, pblk, pblk] + [s0blk] * has_s0,
        out_specs=[sblk] + [s_blk] * emit_state,
        out_shape=[jax.ShapeDtypeStruct((bsz, t, d), BF16)] + [s_shape] * emit_state,
        scratch_shapes=[pltpu.VMEM((2, bb * nc, CHUNK, LANES), BF16), pltpu.VMEM((2, bb * nc, CHUNK, LANES), F32),
                        pltpu.VMEM((2, bb * nc, LANES, LANES), BF16), pltpu.VMEM((2, bb * nc, LANES, LANES), F32),
                        pltpu.VMEM((2, bb * nc, 8, LANES), F32), pltpu.VMEM((bb, t, LANES), F32),
                        pltpu.VMEM((bb, t, LANES), F32), pltpu.VMEM((bb, t, LANES), F32)],
        compiler_params=_params(2),
        name="rwkv_scan",
    )(r, k, v, lw, lw, a, a, k_k, k_k, k_a, k_a, r_k, r_k,
      lnx_w.reshape(1, d), lnx_b.reshape(1, d), *([s0] * has_s0))


def _rwkv_back_kernel(o_ref, g_ref, x_ref, mod_ref, wout_ref, nf_ref, y_ref):
    out = _dot(o_ref[...].astype(F32) * _silu(g_ref[...].astype(F32)), wout_ref[...])
    x2 = x_ref[...] + mod_ref[0][:, 2 * D_MODEL:] * out
    y_ref[...] = _rms(x2) * nf_ref[...]


def _rwkv_back(o, g, x, mod1, mod_idx, tm, w_out, norm_f):
    n, d = x.shape
    nmod = mod1.shape[0]
    row = lambda i: (i, 0)
    const = lambda i: (0, 0)
    return pl.pallas_call(
        _rwkv_back_kernel,
        grid=(n // tm,),
        in_specs=[pl.BlockSpec((tm, d), row), pl.BlockSpec((tm, d), row), pl.BlockSpec((tm, d), row),
                  pl.BlockSpec((1, 1, 3 * d), lambda i: (mod_idx(i), 0, 0)),
                  pl.BlockSpec((d, d), const), pl.BlockSpec((1, d), const)],
        out_specs=pl.BlockSpec((tm, d), row),
        out_shape=jax.ShapeDtypeStruct((n, d), F32),
        compiler_params=_params(1),
        name="rwkv_back",
    )(o, g, x, mod1.reshape(nmod, 1, 3 * d), w_out.astype(BF16), norm_f.reshape(1, d))


def _pair_blockdiag(s):
    b = s.shape[0]
    s = s.reshape(b, 2, RWKV_PAIRS, 2, RWKV_HEAD, RWKV_HEAD)
    z = jnp.zeros_like(s[:, :, :, 0])
    top = jnp.concatenate([s[:, :, :, 0], z], axis=-1)
    bot = jnp.concatenate([z, s[:, :, :, 1]], axis=-1)
    return jnp.concatenate([top, bot], axis=-2)


def _tile_rows(n_seq, t, may_span):
    if t % TOKEN_TILE == 0:
        return TOKEN_TILE
    if may_span and TOKEN_TILE % t == 0 and (n_seq * t) % TOKEN_TILE == 0:
        return TOKEN_TILE
    return t


def kernel(x_prompt, x_sample, state_gla, state_rwkv, c, c_ctx, w_mod, b_mod, norm_w, gla_w_in, gla_w_a1,
           gla_w_a2, gla_b_a, gla_norm, gla_w_out, rwkv_mu, rwkv_w_rkvg, rwkv_w0, rwkv_w1, rwkv_w2, rwkv_a0,
           rwkv_a1, rwkv_a2, rwkv_k_k, rwkv_k_a, rwkv_r_k, rwkv_lnx_w, rwkv_lnx_b, rwkv_w_out, norm_f):
    d = D_MODEL
    bp, tp, _ = x_prompt.shape
    bs, ts, _ = x_sample.shape
    assert tp % CHUNK == 0 and ts % CHUNK == 0 and ts % GRID_W == 0

    nrows = -(-(1 + bs) // 8) * 8
    cond = jnp.zeros((nrows, d), F32).at[0].set(c_ctx).at[1:1 + bs].set(c)
    cond = cond * (1.0 / (1.0 + jnp.exp(-cond)))
    mod0 = _matmul_bias(cond, w_mod[0], b_mod[0].reshape(1, 3 * d))
    mod1 = _matmul_bias(cond, w_mod[1], b_mod[1].reshape(1, 3 * d))

    def trunk(x3, tm, mod_idx, grid_w, gla_s0, rwkv_s0, rwkv_bb, emit_state):
        b, t, _ = x3.shape
        n = b * t
        x = x3.reshape(n, d)
        seq = lambda arr: arr.reshape(b, t, arr.shape[-1])
        q, k, v, g, lg = _gla_front(x, mod0, mod_idx, tm, norm_w[0], gla_w_in[0], gla_w_a1[0], gla_w_a2[0], gla_b_a[0])
        o, *gla_s = _gla_scan(seq(q), seq(k), seq(v), seq(lg), gla_s0, GLA_HEADS, emit_state)
        x1, h1 = _gla_back(o.reshape(n, d), g, x, mod0, mod1, mod_idx, tm, gla_norm[0], gla_w_out[0], norm_w[1])
        r, k, v, g, lw, a = _rwkv_front(h1, tm, t, grid_w, rwkv_mu[0], rwkv_w_rkvg[0], rwkv_w0[0], rwkv_w1[0], rwkv_w2[0],
                                        rwkv_a0[0], rwkv_a1[0], rwkv_a2[0])
        o, *rwkv_s = _rwkv_scan(seq(r), seq(k), seq(v), seq(lw), seq(a), rwkv_k_k[0], rwkv_k_a[0], rwkv_r_k[0],
                                rwkv_lnx_w[0], rwkv_lnx_b[0], rwkv_s0, rwkv_bb, emit_state)
        y = _rwkv_back(o.reshape(n, d), g, x1, mod1, mod_idx, tm, rwkv_w_out[0], norm_f)
        return y.reshape(b, t, d), gla_s, rwkv_s

    y_prompt, (gla_s,), (rwkv_s,) = trunk(x_prompt, _tile_rows(bp, tp, True), lambda i: 0, None, None, None,
                                          min(bp, 4), True)

    tm_s = _tile_rows(bs, ts, False)
    y_sample, _, _ = trunk(x_sample, tm_s, lambda i: 1 + i // (ts // tm_s), GRID_W, state_gla[:, 0],
                           _pair_blockdiag(state_rwkv[:, 0]), min(bs, 4), False)
    return (y_prompt, y_sample, gla_s[:, None], rwkv_s[:, None])
```

```python
import functools

import jax
import jax.numpy as jnp
from jax import lax
from jax.experimental import pallas as pl
from jax.experimental.pallas import tpu as pltpu

F32 = jnp.float32
BF16 = jnp.bfloat16

D_MODEL = 1024
EPS = 1e-6
GRID_W = 64
GLA_HEADS = 4
GLA_DK = 512
GLA_DV = 1024
GLA_DKH = 128
GLA_DVH = 256
GLA_GATE_RANK = 16
GLA_GATE_NORM = 16.0
RWKV_HEAD = 64
RWKV_HEADS = 16
RWKV_PAIRS = 8
RWKV_RANK = 64
LNX_EPS = 64e-5
RWKV_DECAY_SCALE = 0.6065306597126334
CHUNK = 64
SEQ_STEPS = 4
GLA_STEPS = 2
INV_BASE = 8
TOKEN_TILE = 512
PREP_GROUP = 8
LANES = 128
VMEM_LIMIT = 56 * 1024 * 1024


def _dot(a, b):
    return jnp.dot(a.astype(BF16), b.astype(BF16), preferred_element_type=F32)


def _dot_nt(a, b):
    return lax.dot_general(a.astype(BF16), b.astype(BF16), (((1,), (1,)), ((), ())),
                           preferred_element_type=F32)


def _dot_tn(a, b):
    return lax.dot_general(a.astype(BF16), b.astype(BF16), (((0,), (0,)), ((), ())),
                           preferred_element_type=F32)


def _tri_dot(tri, x):
    hi = x.astype(BF16)
    lo = (x - hi.astype(F32)).astype(BF16)
    t = tri.astype(BF16)
    return jnp.dot(t, hi, preferred_element_type=F32) + jnp.dot(t, lo, preferred_element_type=F32)


def _log_sigmoid(z):
    return jnp.minimum(z, 0.0) - jnp.log(1.0 + jnp.exp(-jnp.abs(z)))


def _sigmoid(z):
    return 1.0 / (1.0 + jnp.exp(-z))


def _silu(z):
    return z * _sigmoid(z)


def _rms(x):
    return x * lax.rsqrt(jnp.mean(x * x, axis=-1, keepdims=True) + EPS)


def _params(n_grid_dims):
    return pltpu.CompilerParams(dimension_semantics=("arbitrary",) * n_grid_dims,
                                vmem_limit_bytes=VMEM_LIMIT)


def _mm_kernel(x_ref, w_ref, b_ref, o_ref):
    o_ref[...] = _dot(x_ref[...], w_ref[...]) + b_ref[...]


def _matmul_bias(x, w, b):
    m, k = x.shape
    n = w.shape[1]
    return pl.pallas_call(
        _mm_kernel,
        grid=(1,),
        in_specs=[pl.BlockSpec((m, k), lambda i: (0, 0)),
                  pl.BlockSpec((k, n), lambda i: (0, 0)),
                  pl.BlockSpec((1, n), lambda i: (0, 0))],
        out_specs=pl.BlockSpec((m, n), lambda i: (0, 0)),
        out_shape=jax.ShapeDtypeStruct((m, n), F32),
        compiler_params=_params(1),
        name="mod_matmul",
    )(x, w.astype(BF16), b)


def _gla_front_kernel(x_ref, mod_ref, nw_ref, win_ref, wa1_ref, wa2_ref, ba_ref,
                      q_ref, k_ref, v_ref, g_ref, lg_ref):
    d = D_MODEL
    mod = mod_ref[0]
    shift, scale = mod[:, :d], mod[:, d:2 * d]
    h = _rms(x_ref[...]) * nw_ref[...] * (1.0 + scale) + shift
    hb = h.astype(BF16)
    proj = lambda lo, hi: jnp.dot(hb, win_ref[:, lo:hi], preferred_element_type=F32)
    q_ref[...] = (proj(0, GLA_DK) * (GLA_DKH ** -0.5)).astype(BF16)
    k_ref[...] = proj(GLA_DK, 2 * GLA_DK).astype(BF16)
    v_ref[...] = proj(2 * GLA_DK, 2 * GLA_DK + GLA_DV).astype(BF16)
    g_ref[...] = proj(2 * GLA_DK + GLA_DV, 2 * GLA_DK + 2 * GLA_DV).astype(BF16)
    t = jnp.dot(hb, wa1_ref[...], preferred_element_type=F32)
    z = _dot(t, wa2_ref[...]) + ba_ref[...]
    lg_ref[...] = _log_sigmoid(z) * (1.0 / GLA_GATE_NORM)


def _gla_front(x, mod, mod_idx, tm, norm_w, w_in, w_a1, w_a2, b_a):
    n, d = x.shape
    nmod = mod.shape[0]
    wa1 = jnp.zeros((d, LANES), F32).at[:, :GLA_GATE_RANK].set(w_a1[0]).at[:, GLA_GATE_RANK:2 * GLA_GATE_RANK].set(w_a1[1])
    wa2 = jnp.zeros((LANES, 2 * GLA_DK), F32).at[:GLA_GATE_RANK, :GLA_DK].set(w_a2[0])
    wa2 = wa2.at[GLA_GATE_RANK:2 * GLA_GATE_RANK, GLA_DK:].set(w_a2[1])
    ba = b_a.reshape(1, 2 * GLA_DK)
    row = lambda i: (i, 0)
    const = lambda i: (0, 0)
    n_in = w_in.shape[1]
    outs = pl.pallas_call(
        _gla_front_kernel,
        grid=(n // tm,),
        in_specs=[pl.BlockSpec((tm, d), row),
                  pl.BlockSpec((1, 1, 3 * d), lambda i: (mod_idx(i), 0, 0)),
                  pl.BlockSpec((1, d), const),
                  pl.BlockSpec((d, n_in), const),
                  pl.BlockSpec((d, LANES), const),
                  pl.BlockSpec((LANES, 2 * GLA_DK), const),
                  pl.BlockSpec((1, 2 * GLA_DK), const)],
        out_specs=[pl.BlockSpec((tm, GLA_DK), row), pl.BlockSpec((tm, GLA_DK), row),
                   pl.BlockSpec((tm, GLA_DV), row), pl.BlockSpec((tm, GLA_DV), row),
                   pl.BlockSpec((tm, 2 * GLA_DK), row)],
        out_shape=[jax.ShapeDtypeStruct((n, GLA_DK), BF16), jax.ShapeDtypeStruct((n, GLA_DK), BF16),
                   jax.ShapeDtypeStruct((n, GLA_DV), BF16), jax.ShapeDtypeStruct((n, GLA_DV), BF16),
                   jax.ShapeDtypeStruct((n, 2 * GLA_DK), F32)],
        compiler_params=_params(1),
        name="gla_front",
    )(x, mod.reshape(nmod, 1, 3 * d), norm_w.reshape(1, d), w_in.astype(BF16), wa1.astype(BF16),
      wa2.astype(BF16), ba)
    return outs


def _gla_chunks(chains):
    flat = [(x, ch[2]) for ch in chains for x in ch[0]]
    c = flat[0][0][0].shape[0]
    dk = flat[0][0][1].shape[1]
    row = lax.broadcasted_iota(jnp.int32, (c, c), 0)
    col = lax.broadcasted_iota(jnp.int32, (c, c), 1)
    keep = [(col >= row) if rv else (col <= row) for _, rv in flat]
    b = [_tri_dot(kp.astype(F32), x[3]) for kp, (x, _) in zip(keep, flat)]
    b_end = [bb[0:1] if rv else bb[c - 1:c] for bb, (_, rv) in zip(b, flat)]
    qb = [(x[0] * jnp.exp(bb)).astype(BF16) for (x, _), bb in zip(flat, b)]
    kb = [x[1] * jnp.exp(-bb) for (x, _), bb in zip(flat, b)]
    kd = [x[1] * jnp.exp(be - bb) for (x, _), bb, be in zip(flat, b, b_end)]
    scores = [jnp.where(kp, _dot_nt(x, y), 0.0).astype(BF16) for kp, x, y in zip(keep, qb, kb)]
    upd = [_dot_tn(x, y[2]) for (y, _), x in zip(flat, kd)]
    decay = [jnp.broadcast_to(jnp.exp(be), (dk, dk)).T for be in b_end]
    n_steps = len(chains[0][0])
    st = [ch[1] for ch in chains]
    outs = [[] for _ in chains]
    for s in range(n_steps):
        at = [ci * n_steps + s for ci in range(len(chains))]
        for ci, f in enumerate(at):
            outs[ci].append(_dot(jnp.concatenate([scores[f], qb[f]], axis=1),
                                 jnp.concatenate([flat[f][0][2], st[ci].astype(BF16)], axis=0)))
        st = [st[ci] * jnp.concatenate([decay[f]] * (st[ci].shape[1] // dk), axis=1) + upd[f]
              for ci, f in enumerate(at)]
    return list(zip(outs, st))


def _gla_scan_kernel(*refs, has_s0, emit_state):
    q_ref, k_ref, v_ref, lf_ref, lb_ref = refs[:5]
    s0_ref = refs[5] if has_s0 else None
    o_ref = refs[5 + has_s0]
    s_ref = refs[6 + has_s0] if emit_state else None
    acc_ref = refs[-1]
    t = q_ref.shape[1]
    hb = q_ref.shape[2] // GLA_DKH
    nc = t // CHUNK
    half = nc // 2
    kh = lambda h: slice(h * GLA_DKH, (h + 1) * GLA_DKH)
    vh = lambda h: slice(h * GLA_DVH, (h + 1) * GLA_DVH)

    steps = GLA_STEPS if half % GLA_STEPS == 0 else 1

    def both(accumulate):
        def body(i, carry):
            sls = [[pl.ds(pl.multiple_of(ci * CHUNK, CHUNK), CHUNK)
                    for ci in ((i * steps + s, nc - 1 - i * steps - s)[d] for s in range(steps))] for d in range(2)]
            chains = [([(q_ref[0, sl, kh(h)].astype(F32), k_ref[0, sl, kh(h)].astype(F32), v_ref[0, sl, vh(h)],
                         (lf_ref, lb_ref)[d][0, sl, kh(h)]) for sl in sls[d]], carry[d * hb + h], d == 1)
                      for d in range(2) for h in range(hb)]
            res = _gla_chunks(chains)
            for d in range(2):
                for h in range(hb):
                    for sl, o in zip(sls[d], res[d * hb + h][0]):
                        if accumulate:
                            o_ref[0, sl, vh(h)] = (acc_ref[sl, vh(h)] + o).astype(BF16)
                        else:
                            acc_ref[sl, vh(h)] = o
            return tuple(st for _, st in res)
        return body

    if has_s0:
        init = tuple(s0_ref[0, d, h] for d in range(2) for h in range(hb))
    else:
        init = tuple(jnp.zeros((GLA_DKH, GLA_DVH), F32) for _ in range(2 * hb))
    carry = lax.fori_loop(0, half // steps, both(False), init)
    carry = lax.fori_loop(half // steps, nc // steps, both(True), carry)
    if emit_state:
        for d in range(2):
            for h in range(hb):
                s_ref[0, d, h] = carry[d * hb + h]


def _gla_scan(q, k, v, lg, s0, hb, emit_state):
    bsz, t, _ = q.shape
    ng = GLA_HEADS // hb
    hd = lambda b, h: (b, 0, h)
    kblk = pl.BlockSpec((1, t, hb * GLA_DKH), hd)
    vblk = pl.BlockSpec((1, t, hb * GLA_DVH), hd)
    sblk = pl.BlockSpec((1, 2, hb, GLA_DKH, GLA_DVH), lambda b, h: (b, 0, h, 0, 0))
    s_shape = jax.ShapeDtypeStruct((bsz, 2, GLA_HEADS, GLA_DKH, GLA_DVH), F32)
    has_s0 = s0 is not None
    return pl.pallas_call(
        functools.partial(_gla_scan_kernel, has_s0=has_s0, emit_state=emit_state),
        grid=(bsz, ng),
        in_specs=[kblk, kblk, vblk, kblk, pl.BlockSpec((1, t, hb * GLA_DKH), lambda b, h: (b, 0, ng + h))]
        + [sblk] * has_s0,
        out_specs=[vblk] + [sblk] * emit_state,
        out_shape=[jax.ShapeDtypeStruct((bsz, t, GLA_DV), BF16)] + [s_shape] * emit_state,
        scratch_shapes=[pltpu.VMEM((t, hb * GLA_DVH), F32)],
        compiler_params=_params(2),
        name="gla_scan",
    )(q, k, v, lg, lg, *([s0] * has_s0))


def _gla_back_kernel(o_ref, g_ref, x_ref, mod0_ref, mod1_ref, gn_ref, wout_ref, nw1_ref, x1_ref, h1_ref):
    d = D_MODEL
    o = o_ref[...].astype(F32)
    gn = gn_ref[...]
    parts = [_rms(o[:, h * GLA_DVH:(h + 1) * GLA_DVH]) * gn for h in range(GLA_HEADS)]
    on = jnp.concatenate(parts, axis=-1) * _silu(g_ref[...].astype(F32))
    out = _dot(on, wout_ref[...])
    gate = mod0_ref[0][:, 2 * d:]
    x1 = x_ref[...] + gate * out
    x1_ref[...] = x1
    mod1 = mod1_ref[0]
    h1_ref[...] = (_rms(x1) * nw1_ref[...] * (1.0 + mod1[:, d:2 * d]) + mod1[:, :d]).astype(BF16)


def _gla_back(o, g, x, mod0, mod1, mod_idx, tm, gla_norm, w_out, norm_w1):
    n, d = x.shape
    nmod = mod0.shape[0]
    row = lambda i: (i, 0)
    const = lambda i: (0, 0)
    modspec = pl.BlockSpec((1, 1, 3 * d), lambda i: (mod_idx(i), 0, 0))
    return pl.pallas_call(
        _gla_back_kernel,
        grid=(n // tm,),
        in_specs=[pl.BlockSpec((tm, d), row), pl.BlockSpec((tm, d), row), pl.BlockSpec((tm, d), row),
                  modspec, modspec,
                  pl.BlockSpec((1, GLA_DVH), const), pl.BlockSpec((d, d), const), pl.BlockSpec((1, d), const)],
        out_specs=[pl.BlockSpec((tm, d), row), pl.BlockSpec((tm, d), row)],
        out_shape=[jax.ShapeDtypeStruct((n, d), F32), jax.ShapeDtypeStruct((n, d), BF16)],
        compiler_params=_params(1),
        name="gla_back",
    )(o, g, x, mod0.reshape(nmod, 1, 3 * d), mod1.reshape(nmod, 1, 3 * d), gla_norm.reshape(1, GLA_DVH),
      w_out.astype(BF16), norm_w1.reshape(1, d))


def _token_shift(h, up_ref, dn_ref, seq_len, grid_w):
    tm = h.shape[0]
    pos = lax.broadcasted_iota(jnp.int32, (tm, 1), 0) + pl.program_id(0) * tm
    prev = pltpu.roll(h, 1, 0)
    nxt = pltpu.roll(h, tm - 1, 0)
    t = pos % seq_len
    if grid_w is None:
        return 0.5 * (jnp.where(t == 0, 0.0, prev) + jnp.where(t == seq_len - 1, 0.0, nxt))
    col = pos % grid_w
    up = jnp.concatenate([up_ref[...].astype(F32), h[:tm - grid_w]], axis=0)
    dn = jnp.concatenate([h[grid_w:], dn_ref[...].astype(F32)], axis=0)
    up = jnp.where(t < grid_w, 0.0, up)
    dn = jnp.where(t >= seq_len - grid_w, 0.0, dn)
    left = jnp.where(col == 0, 0.0, prev)
    right = jnp.where(col == grid_w - 1, 0.0, nxt)
    return 0.25 * (up + dn + left + right)


def _rwkv_front_kernel(*refs, seq_len, grid_w):
    n_h = 1 if grid_w is None else 3
    h_ref, up_ref, dn_ref = (tuple(refs[:n_h]) + (None, None))[:3]
    (mu_ref, wrkvg_ref, w1_ref, w2_ref, w0_ref, a1_ref, a2_ref, a0_ref,
     r_ref, k_ref, v_ref, g_ref, lw_ref, a_ref) = refs[n_h:]
    h = h_ref[...].astype(F32)
    dh = _token_shift(h, up_ref, dn_ref, seq_len, grid_w) - h
    mix = lambda p: (h + dh * mu_ref[p:p + 1, :]).astype(BF16)
    r_ref[...] = jnp.dot(mix(0), wrkvg_ref[0], preferred_element_type=F32).astype(BF16)
    k_ref[...] = jnp.dot(mix(1), wrkvg_ref[1], preferred_element_type=F32).astype(BF16)
    v_ref[...] = jnp.dot(mix(2), wrkvg_ref[2], preferred_element_type=F32).astype(BF16)
    g_ref[...] = jnp.dot(mix(3), wrkvg_ref[3], preferred_element_type=F32).astype(BF16)
    tw = jnp.tanh(jnp.dot(mix(4), w1_ref[...], preferred_element_type=F32))
    lw_ref[...] = -RWKV_DECAY_SCALE * _sigmoid(w0_ref[...] + _dot(tw, w2_ref[...]))
    ta = jnp.dot(mix(5), a1_ref[...], preferred_element_type=F32)
    a_ref[...] = _sigmoid(a0_ref[...] + _dot(ta, a2_ref[...])).astype(BF16)


def _rwkv_front(h, tm, seq_len, grid_w, mu, w_rkvg, w0, w1, w2, a0, a1, a2):
    n, d = h.shape
    rk = RWKV_RANK
    cat1 = lambda w: jnp.concatenate([w[0], w[1]], axis=1)
    bd2 = lambda w: jnp.zeros((2 * rk, 2 * d), F32).at[:rk, :d].set(w[0]).at[rk:, d:].set(w[1])
    row = lambda i: (i, 0)
    const = lambda i: (0, 0)
    if grid_w is None:
        assert tm % seq_len == 0
        h_specs, h_args = [pl.BlockSpec((tm, d), row)], [h]
    else:
        assert tm % grid_w == 0 and seq_len % tm == 0
        rows_per_tile, last = tm // grid_w, n // grid_w - 1
        h_specs = [pl.BlockSpec((tm, d), row),
                   pl.BlockSpec((grid_w, d), lambda i: (jnp.maximum(i * rows_per_tile - 1, 0), 0)),
                   pl.BlockSpec((grid_w, d), lambda i: (jnp.minimum((i + 1) * rows_per_tile, last), 0))]
        h_args = [h, h, h]
    return pl.pallas_call(
        functools.partial(_rwkv_front_kernel, seq_len=seq_len, grid_w=grid_w),
        grid=(n // tm,),
        in_specs=h_specs + [
            pl.BlockSpec((8, d), const),
            pl.BlockSpec((4, d, d), lambda i: (0, 0, 0)),
            pl.BlockSpec((d, 2 * rk), const), pl.BlockSpec((2 * rk, 2 * d), const), pl.BlockSpec((1, 2 * d), const),
            pl.BlockSpec((d, 2 * rk), const), pl.BlockSpec((2 * rk, 2 * d), const), pl.BlockSpec((1, 2 * d), const)],
        out_specs=[pl.BlockSpec((tm, d), row)] * 4 + [pl.BlockSpec((tm, 2 * d), row)] * 2,
        out_shape=[jax.ShapeDtypeStruct((n, d), BF16)] * 4
        + [jax.ShapeDtypeStruct((n, 2 * d), F32), jax.ShapeDtypeStruct((n, 2 * d), BF16)],
        compiler_params=_params(1),
        name="rwkv_front",
    )(*h_args, jnp.zeros((8, d), F32).at[:6].set(mu), w_rkvg.astype(BF16),
      cat1(w1).astype(BF16), bd2(w2).astype(BF16), w0.reshape(1, 2 * d),
      cat1(a1).astype(BF16), bd2(a2).astype(BF16), a0.reshape(1, 2 * d))


def _seg_sum(x, head0):
    s0 = jnp.sum(jnp.where(head0, x, 0.0), axis=-1, keepdims=True)
    s1 = jnp.sum(jnp.where(head0, 0.0, x), axis=-1, keepdims=True)
    return jnp.where(head0, s0, s1)


def _stack(x, head0):
    return jnp.concatenate([jnp.where(head0, x, 0.0), jnp.where(head0, 0.0, x)], axis=0)


def _rwkv_chunk_terms(insts):
    n = len(insts)
    c = insts[0][0].shape[0]
    lane = lax.broadcasted_iota(jnp.int32, (c, LANES), 1)
    head0 = lane < RWKV_HEAD
    row = lax.broadcasted_iota(jnp.int32, (c, c), 0)
    col = lax.broadcasted_iota(jnp.int32, (c, c), 1)
    ti = lax.broadcasted_iota(jnp.int32, (c, LANES), 0)
    si = lane & (c - 1)
    prow = lax.broadcasted_iota(jnp.int32, (LANES, LANES), 0)
    pcol = lax.broadcasted_iota(jnp.int32, (LANES, LANES), 1)
    same_head = (prow >= RWKV_HEAD) == (pcol >= RWKV_HEAD)
    each = lambda f, *ls: [f(*xs) for xs in zip(*ls)]
    rev = [x[8] for x in insts]
    stk = lambda x: _stack(x, head0).astype(BF16)

    cum = [_tri_dot(((col >= row) if x[8] else (col <= row)).astype(F32), x[3]) for x in insts]
    tot = [cm[0:1] if rv else cm[c - 1:c] for cm, rv in zip(cum, rev)]

    a_t, r_t, b_ts, k_ts, b_h, k_h, v_s, vv, bonus = [], [], [], [], [], [], [], [], []
    for (r, k, v, lw, a, k_k, k_a, r_k, _), cm, tt in zip(insts, cum, tot):
        kk = k * k_k
        kk = kk / jnp.maximum(jnp.sqrt(_seg_sum(kk * kk, head0)), 1e-12)
        kd = k * (1.0 + (a - 1.0) * k_a)
        bv = kk * a
        bonus.append(_seg_sum(r * kd * r_k, head0) * v)
        e_neg = jnp.exp(-cm)
        e_tail = jnp.exp(tt - cm)
        a_t.append(-kk * jnp.exp(cm - lw))
        r_t.append(r * jnp.exp(cm))
        b_ts.append(stk(bv * e_neg))
        k_ts.append(stk(kd * e_neg))
        b_h.append(bv * e_tail)
        k_h.append(kd * e_tail)
        v_s.append(stk(v))
        vv.append(v)

    strict = [(si > ti) if rv else (si < ti) for rv in rev]
    incl = [(si >= ti) if rv else (si <= ti) for rv in rev]
    ar = each(lambda x, y: jnp.concatenate([x, y], axis=0).astype(BF16), a_t, r_t)
    g_b = each(_dot_nt, ar, b_ts)
    g_k = each(_dot_nt, ar, k_ts)
    n_ab = each(lambda m_, x: jnp.where(m_, x[:c], 0.0), strict, g_b)
    l_ak = each(lambda m_, x: jnp.where(m_, x[:c], 0.0), strict, g_k)
    p_rb = each(lambda m_, x: jnp.where(m_, x[c:], 0.0), incl, g_b)
    p_rk = each(lambda m_, x: jnp.where(m_, x[c:], 0.0), incl, g_k)

    same_blk = lambda h: (ti // h) == (si // h)
    n_d = each(lambda x: jnp.where(same_blk(INV_BASE), x, 0.0), n_ab)
    steps = max(1, (INV_BASE - 1).bit_length())
    q = n_d
    tinv = n_d
    q = each(lambda x: _dot(x, stk(x)), q) if steps > 1 else q
    for j in range(1, steps):
        last = j == steps - 1
        qs = each(stk, q)
        if last:
            z = each(_dot, tinv, qs)
            tinv = each(lambda t_, x, y: t_ + x + y, tinv, q, z)
        else:
            both = each(lambda x, t_, w_: _dot(jnp.concatenate([x, t_], axis=0), w_), q, tinv, qs)
            tinv = each(lambda t_, x, y: t_ + x + y[c:], tinv, q, both)
            q = each(lambda y: y[:c], both)
    h = INV_BASE
    while h < c:
        link = same_blk(2 * h) & jnp.logical_not(same_blk(h))
        n_off = each(lambda x: jnp.where(link, x, 0.0), n_ab)
        y = each(lambda t_, x: x + _dot(t_, stk(x)), tinv, n_off)
        tinv = each(lambda t_, x: t_ + x + _dot(x, stk(t_)), tinv, y)
        h *= 2
    a_bar = each(lambda t_, x: x + _dot(t_, stk(x)), tinv, a_t)
    lv = each(lambda l_, p, y: _dot(jnp.concatenate([l_, p], axis=0), y), l_ak, p_rk, v_s)
    w = [x[:c] for x in lv]
    u0 = each(lambda t_, x: x + _dot(t_, stk(x)), tinv, w)
    r_bar = each(lambda x, p, y: x + _dot(p, stk(y)), r_t, p_rb, a_bar)
    y0 = each(lambda p, x, z: _dot(p, stk(x)) + z[c:], p_rb, u0, lv)
    zero = jnp.zeros((c, LANES), F32)
    ms = each(lambda ab, u, v_, bh, kh_: _dot_tn(
        jnp.concatenate([jnp.concatenate([ab, u], axis=1), jnp.concatenate([zero, v_], axis=1)], axis=0),
        jnp.concatenate([bh, kh_], axis=0)), a_bar, u0, vv, b_h, k_h)
    m = [jnp.where(same_head, x[:LANES], 0.0) for x in ms]
    s0p = [jnp.where(same_head, x[LANES:], 0.0) for x in ms]
    return [(r_bar[i], y0[i], m[i], s0p[i], jnp.exp(tot[i]), bonus[i]) for i in range(n)]


def _rwkv_scan_kernel(*refs, has_s0, emit_state):
    (r_ref, k_ref, v_ref, lwf_ref, lwb_ref, af_ref, ab_ref,
     kkf_ref, kkb_ref, kaf_ref, kab_ref, rkf_ref, rkb_ref, lnw_ref, lnb_ref) = refs[:15]
    s0_ref = refs[15] if has_s0 else None
    o_ref = refs[15 + has_s0]
    s_ref = refs[16 + has_s0] if emit_state else None
    rbar_ref, y0_ref, m_ref, s0p_ref, dec_ref, bonus_ref, yf_ref, yb_ref = refs[16 + has_s0 + emit_state:]
    bb, t = r_ref.shape[0], r_ref.shape[1]
    nc = t // CHUNK
    dirs = ((lwf_ref, af_ref, kkf_ref, kaf_ref, rkf_ref, False), (lwb_ref, ab_ref, kkb_ref, kab_ref, rkb_ref, True))
    group = min(bb * nc, PREP_GROUP)
    assert (bb * nc) % group == 0
    chunk_rows = lambda ci: pl.ds(pl.multiple_of(ci * CHUNK, CHUNK), CHUNK)

    def prep(gi, carry):
        insts, where = [], []
        at = [((gi * group + j) // nc, chunk_rows((gi * group + j) % nc)) for j in range(group)]
        for j, (e, sl) in enumerate(at):
            r, k, v = (ref[e, sl, :].astype(F32) for ref in (r_ref, k_ref, v_ref))
            for d, (lw_ref, a_ref, kk_ref, ka_ref, rk_ref, reverse) in enumerate(dirs):
                insts.append((r, k, v, lw_ref[e, sl, :], a_ref[e, sl, :].astype(F32), kk_ref[...], ka_ref[...],
                              rk_ref[...], reverse))
                where.append((d, gi * group + j, e, sl))
        terms = _rwkv_chunk_terms(insts)
        for (d, fi, e, sl), (r_bar, y0, m, s0p, dec, bon) in zip(where, terms):
            rbar_ref[d, fi] = r_bar.astype(BF16)
            y0_ref[d, fi] = y0
            m_ref[d, fi] = m.astype(BF16)
            s0p_ref[d, fi] = s0p
            dec_ref[d, fi] = jnp.broadcast_to(dec, (8, LANES))
        for j in range(group):
            _, _, e, sl = where[2 * j]
            bonus_ref[e, sl, :] = terms[2 * j][5] + terms[2 * j + 1][5]
        return carry

    lax.fori_loop(0, bb * nc // group, prep, 0)

    seq_steps = SEQ_STEPS if nc % SEQ_STEPS == 0 else 1
    idx = [(d, e) for d in range(2) for e in range(bb)]

    def seq(i, carry):
        for s in range(seq_steps):
            cis = (i * seq_steps + s, nc - 1 - i * seq_steps - s)
            sb16 = [st.astype(BF16) for st in carry]
            sm = [jnp.dot(sb16[n], m_ref[d, e * nc + cis[d]], preferred_element_type=F32)
                  for n, (d, e) in enumerate(idx)]
            ys = [_dot_nt(rbar_ref[d, e * nc + cis[d]], sb16[n]) + y0_ref[d, e * nc + cis[d]]
                  for n, (d, e) in enumerate(idx)]
            for n, (d, e) in enumerate(idx):
                (yf_ref, yb_ref)[d][e, chunk_rows(cis[d]), :] = ys[n]
            carry = tuple(carry[n] * dec_ref[d, e * nc + cis[d]][0:1] + sm[n] + s0p_ref[d, e * nc + cis[d]]
                          for n, (d, e) in enumerate(idx))
        return carry

    if has_s0:
        init = tuple(s0_ref[e, d, 0] for d in range(2) for e in range(bb))
    else:
        init = tuple(jnp.zeros((LANES, LANES), F32) for _ in range(2 * bb))
    final = lax.fori_loop(0, nc // seq_steps, seq, init)
    if emit_state:
        for d in range(2):
            for e in range(bb):
                s = final[d * bb + e]
                s_ref[e, d, 0] = s[:RWKV_HEAD, :RWKV_HEAD]
                s_ref[e, d, 1] = pltpu.roll(s, RWKV_HEAD, 1)[RWKV_HEAD:, :RWKV_HEAD]

    prow = lax.broadcasted_iota(jnp.int32, (LANES, LANES), 0)
    pcol = lax.broadcasted_iota(jnp.int32, (LANES, LANES), 1)
    head_ones = ((prow >= RWKV_HEAD) == (pcol >= RWKV_HEAD)).astype(BF16)
    inv_n = 1.0 / RWKV_HEAD

    def head_mean(x):
        hi = x.astype(BF16)
        lo = (x - hi.astype(F32)).astype(BF16)
        both = jnp.dot(jnp.concatenate([hi, lo], axis=0), head_ones, preferred_element_type=F32)
        return (both[:x.shape[0]] + both[x.shape[0]:]) * inv_n

    def finish(gi, carry):
        at = [((gi * group + j) // nc, chunk_rows((gi * group + j) % nc)) for j in range(group)]
        y = [yf_ref[e, sl, :] + yb_ref[e, sl, :] for e, sl in at]
        yc = [x - head_mean(x) for x in y]
        var = [head_mean(x * x) for x in yc]
        for (e, sl), x, vr in zip(at, yc, var):
            o_ref[e, sl, :] = (x * lax.rsqrt(vr + LNX_EPS) * lnw_ref[...] + lnb_ref[...]
                               + bonus_ref[e, sl, :]).astype(BF16)
        return carry

    lax.fori_loop(0, bb * nc // group, finish, 0)


def _rwkv_scan(r, k, v, lw, a, k_k, k_a, r_k, lnx_w, lnx_b, s0, bb, emit_state):
    bsz, t, d = r.shape
    nc = t // CHUNK
    seq = lambda b, p: (b, 0, p)
    seq_b = lambda b, p: (b, 0, RWKV_PAIRS + p)
    par = lambda b, p: (0, p)
    par_b = lambda b, p: (0, RWKV_PAIRS + p)
    sblk = pl.BlockSpec((bb, t, LANES), seq)
    sblk_b = pl.BlockSpec((bb, t, LANES), seq_b)
    pblk = pl.BlockSpec((1, LANES), par)
    pblk_b = pl.BlockSpec((1, LANES), par_b)
    s0blk = pl.BlockSpec((bb, 2, 1, LANES, LANES), lambda b, p: (b, 0, p, 0, 0))
    s_blk = pl.BlockSpec((bb, 2, 2, RWKV_HEAD, RWKV_HEAD), lambda b, p: (b, 0, p, 0, 0))
    s_shape = jax.ShapeDtypeStruct((bsz, 2, RWKV_HEADS, RWKV_HEAD, RWKV_HEAD), F32)
    has_s0 = s0 is not None
    k_k, k_a, r_k = (p.reshape(1, 2 * d) for p in (k_k, k_a, r_k))
    return pl.pallas_call(
        functools.partial(_rwkv_scan_kernel, has_s0=has_s0, emit_state=emit_state),
        grid=(bsz // bb, RWKV_PAIRS),
        in_specs=[sblk, sblk, sblk, sblk, sblk_b, sblk, sblk_b,
                  pblk, pblk_b, pblk, pblk_b, pblk, pblk_b, pblk, pblk] + [s0blk] * has_s0,
        out_specs=[sblk] + [s_blk] * emit_state,
        out_shape=[jax.ShapeDtypeStruct((bsz, t, d), BF16)] + [s_shape] * emit_state,
        scratch_shapes=[pltpu.VMEM((2, bb * nc, CHUNK, LANES), BF16), pltpu.VMEM((2, bb * nc, CHUNK, LANES), F32),
                        pltpu.VMEM((2, bb * nc, LANES, LANES), BF16), pltpu.VMEM((2, bb * nc, LANES, LANES), F32),
                        pltpu.VMEM((2, bb * nc, 8, LANES), F32), pltpu.VMEM((bb, t, LANES), F32),
                        pltpu.VMEM((bb, t, LANES), F32), pltpu.VMEM((bb, t, LANES), F32)],
        compiler_params=_params(2),
        name="rwkv_scan",
    )(r, k, v, lw, lw, a, a, k_k, k_k, k_a, k_a, r_k, r_k,
      lnx_w.reshape(1, d), lnx_b.reshape(1, d), *([s0] * has_s0))


def _rwkv_back_kernel(o_ref, g_ref, x_ref, mod_ref, wout_ref, nf_ref, y_ref):
    out = _dot(o_ref[...].astype(F32) * _silu(g_ref[...].astype(F32)), wout_ref[...])
    x2 = x_ref[...] + mod_ref[0][:, 2 * D_MODEL:] * out
    y_ref[...] = _rms(x2) * nf_ref[...]


def _rwkv_back(o, g, x, mod1, mod_idx, tm, w_out, norm_f):
    n, d = x.shape
    nmod = mod1.shape[0]
    row = lambda i: (i, 0)
    const = lambda i: (0, 0)
    return pl.pallas_call(
        _rwkv_back_kernel,
        grid=(n // tm,),
        in_specs=[pl.BlockSpec((tm, d), row), pl.BlockSpec((tm, d), row), pl.BlockSpec((tm, d), row),
                  pl.BlockSpec((1, 1, 3 * d), lambda i: (mod_idx(i), 0, 0)),
                  pl.BlockSpec((d, d), const), pl.BlockSpec((1, d), const)],
        out_specs=pl.BlockSpec((tm, d), row),
        out_shape=jax.ShapeDtypeStruct((n, d), F32),
        compiler_params=_params(1),
        name="rwkv_back",
    )(o, g, x, mod1.reshape(nmod, 1, 3 * d), w_out.astype(BF16), norm_f.reshape(1, d))


def _pair_blockdiag(s):
    b = s.shape[0]
    s = s.reshape(b, 2, RWKV_PAIRS, 2, RWKV_HEAD, RWKV_HEAD)
    z = jnp.zeros_like(s[:, :, :, 0])
    top = jnp.concatenate([s[:, :, :, 0], z], axis=-1)
    bot = jnp.concatenate([z, s[:, :, :, 1]], axis=-1)
    return jnp.concatenate([top, bot], axis=-2)


def _tile_rows(n_seq, t, may_span):
    if t % TOKEN_TILE == 0:
        return TOKEN_TILE
    if may_span and TOKEN_TILE % t == 0 and (n_seq * t) % TOKEN_TILE == 0:
        return TOKEN_TILE
    return t


def kernel(x_prompt, x_sample, state_gla, state_rwkv, c, c_ctx, w_mod, b_mod, norm_w, gla_w_in, gla_w_a1,
           gla_w_a2, gla_b_a, gla_norm, gla_w_out, rwkv_mu, rwkv_w_rkvg, rwkv_w0, rwkv_w1, rwkv_w2, rwkv_a0,
           rwkv_a1, rwkv_a2, rwkv_k_k, rwkv_k_a, rwkv_r_k, rwkv_lnx_w, rwkv_lnx_b, rwkv_w_out, norm_f):
    d = D_MODEL
    bp, tp, _ = x_prompt.shape
    bs, ts, _ = x_sample.shape
    assert tp % CHUNK == 0 and ts % CHUNK == 0 and ts % GRID_W == 0

    nrows = -(-(1 + bs) // 8) * 8
    cond = jnp.zeros((nrows, d), F32).at[0].set(c_ctx).at[1:1 + bs].set(c)
    cond = cond * (1.0 / (1.0 + jnp.exp(-cond)))
    mod0 = _matmul_bias(cond, w_mod[0], b_mod[0].reshape(1, 3 * d))
    mod1 = _matmul_bias(cond, w_mod[1], b_mod[1].reshape(1, 3 * d))

    def trunk(x3, tm, mod_idx, grid_w, gla_s0, rwkv_s0, rwkv_bb, emit_state):
        b, t, _ = x3.shape
        n = b * t
        x = x3.reshape(n, d)
        seq = lambda arr: arr.reshape(b, t, arr.shape[-1])
        q, k, v, g, lg = _gla_front(x, mod0, mod_idx, tm, norm_w[0], gla_w_in[0], gla_w_a1[0], gla_w_a2[0], gla_b_a[0])
        o, *gla_s = _gla_scan(seq(q), seq(k), seq(v), seq(lg), gla_s0, GLA_HEADS, emit_state)
        x1, h1 = _gla_back(o.reshape(n, d), g, x, mod0, mod1, mod_idx, tm, gla_norm[0], gla_w_out[0], norm_w[1])
        r, k, v, g, lw, a = _rwkv_front(h1, tm, t, grid_w, rwkv_mu[0], rwkv_w_rkvg[0], rwkv_w0[0], rwkv_w1[0], rwkv_w2[0],
                                        rwkv_a0[0], rwkv_a1[0], rwkv_a2[0])
        o, *rwkv_s = _rwkv_scan(seq(r), seq(k), seq(v), seq(lw), seq(a), rwkv_k_k[0], rwkv_k_a[0], rwkv_r_k[0],
                                rwkv_lnx_w[0], rwkv_lnx_b[0], rwkv_s0, rwkv_bb, emit_state)
        y = _rwkv_back(o.reshape(n, d), g, x1, mod1, mod_idx, tm, rwkv_w_out[0], norm_f)
        return y.reshape(b, t, d), gla_s, rwkv_s

    y_prompt, (gla_s,), (rwkv_s,) = trunk(x_prompt, _tile_rows(bp, tp, True), lambda i: 0, None, None, None,
                                          min(bp, 4), True)

    tm_s = _tile_rows(bs, ts, False)
    y_sample, _, _ = trunk(x_sample, tm_s, lambda i: 1 + i // (ts // tm_s), GRID_W, state_gla[:, 0],
                           _pair_blockdiag(state_rwkv[:, 0]), min(bs, 4), False)
    return (y_prompt, y_sample, gla_s[:, None], rwkv_s[:, None])
```

```python
import functools

import jax
import jax.numpy as jnp
from jax import lax
from jax.experimental import pallas as pl
from jax.experimental.pallas import tpu as pltpu

F32 = jnp.float32
BF16 = jnp.bfloat16

D_MODEL = 1024
EPS = 1e-6
GRID_W = 64
GLA_HEADS = 4
GLA_DK = 512
GLA_DV = 1024
GLA_DKH = 128
GLA_DVH = 256
GLA_GATE_RANK = 16
GLA_GATE_NORM = 16.0
RWKV_HEAD = 64
RWKV_HEADS = 16
RWKV_PAIRS = 8
RWKV_RANK = 64
LNX_EPS = 64e-5
RWKV_DECAY_SCALE = 0.6065306597126334
CHUNK = 64
SEQ_STEPS = 4
GLA_STEPS = 2
INV_BASE = 8
TOKEN_TILE = 512
PREP_GROUP = 8
LANES = 128
VMEM_LIMIT = 56 * 1024 * 1024


def _dot(a, b):
    return jnp.dot(a.astype(BF16), b.astype(BF16), preferred_element_type=F32)


def _dot_nt(a, b):
    return lax.dot_general(a.astype(BF16), b.astype(BF16), (((1,), (1,)), ((), ())),
                           preferred_element_type=F32)


def _dot_tn(a, b):
    return lax.dot_general(a.astype(BF16), b.astype(BF16), (((0,), (0,)), ((), ())),
                           preferred_element_type=F32)


def _tri_dot(tri, x):
    hi = x.astype(BF16)
    lo = (x - hi.astype(F32)).astype(BF16)
    t = tri.astype(BF16)
    return jnp.dot(t, hi, preferred_element_type=F32) + jnp.dot(t, lo, preferred_element_type=F32)


def _log_sigmoid(z):
    return jnp.minimum(z, 0.0) - jnp.log(1.0 + jnp.exp(-jnp.abs(z)))


def _sigmoid(z):
    return 1.0 / (1.0 + jnp.exp(-z))


def _silu(z):
    return z * _sigmoid(z)


def _rms(x):
    return x * lax.rsqrt(jnp.mean(x * x, axis=-1, keepdims=True) + EPS)


def _params(n_grid_dims):
    return pltpu.CompilerParams(dimension_semantics=("arbitrary",) * n_grid_dims,
                                vmem_limit_bytes=VMEM_LIMIT)


def _mm_kernel(x_ref, w_ref, b_ref, o_ref):
    o_ref[...] = _dot(x_ref[...], w_ref[...]) + b_ref[...]


def _matmul_bias(x, w, b):
    m, k = x.shape
    n = w.shape[1]
    return pl.pallas_call(
        _mm_kernel,
        grid=(1,),
        in_specs=[pl.BlockSpec((m, k), lambda i: (0, 0)),
                  pl.BlockSpec((k, n), lambda i: (0, 0)),
                  pl.BlockSpec((1, n), lambda i: (0, 0))],
        out_specs=pl.BlockSpec((m, n), lambda i: (0, 0)),
        out_shape=jax.ShapeDtypeStruct((m, n), F32),
        compiler_params=_params(1),
        name="mod_matmul",
    )(x, w.astype(BF16), b)


def _gla_front_kernel(x_ref, mod_ref, nw_ref, win_ref, wa1_ref, wa2_ref, ba_ref,
                      q_ref, k_ref, v_ref, g_ref, lg_ref):
    d = D_MODEL
    mod = mod_ref[0]
    shift, scale = mod[:, :d], mod[:, d:2 * d]
    h = _rms(x_ref[...]) * nw_ref[...] * (1.0 + scale) + shift
    hb = h.astype(BF16)
    proj = lambda lo, hi: jnp.dot(hb, win_ref[:, lo:hi], preferred_element_type=F32)
    q_ref[...] = (proj(0, GLA_DK) * (GLA_DKH ** -0.5)).astype(BF16)
    k_ref[...] = proj(GLA_DK, 2 * GLA_DK).astype(BF16)
    v_ref[...] = proj(2 * GLA_DK, 2 * GLA_DK + GLA_DV).astype(BF16)
    g_ref[...] = proj(2 * GLA_DK + GLA_DV, 2 * GLA_DK + 2 * GLA_DV).astype(BF16)
    t = jnp.dot(hb, wa1_ref[...], preferred_element_type=F32)
    z = _dot(t, wa2_ref[...]) + ba_ref[...]
    lg_ref[...] = _log_sigmoid(z) * (1.0 / GLA_GATE_NORM)


def _gla_front(x, mod, mod_idx, tm, norm_w, w_in, w_a1, w_a2, b_a):
    n, d = x.shape
    nmod = mod.shape[0]
    wa1 = jnp.zeros((d, LANES), F32).at[:, :GLA_GATE_RANK].set(w_a1[0]).at[:, GLA_GATE_RANK:2 * GLA_GATE_RANK].set(w_a1[1])
    wa2 = jnp.zeros((LANES, 2 * GLA_DK), F32).at[:GLA_GATE_RANK, :GLA_DK].set(w_a2[0])
    wa2 = wa2.at[GLA_GATE_RANK:2 * GLA_GATE_RANK, GLA_DK:].set(w_a2[1])
    ba = b_a.reshape(1, 2 * GLA_DK)
    row = lambda i: (i, 0)
    const = lambda i: (0, 0)
    n_in = w_in.shape[1]
    outs = pl.pallas_call(
        _gla_front_kernel,
        grid=(n // tm,),
        in_specs=[pl.BlockSpec((tm, d), row),
                  pl.BlockSpec((1, 1, 3 * d), lambda i: (mod_idx(i), 0, 0)),
                  pl.BlockSpec((1, d), const),
                  pl.BlockSpec((d, n_in), const),
                  pl.BlockSpec((d, LANES), const),
                  pl.BlockSpec((LANES, 2 * GLA_DK), const),
                  pl.BlockSpec((1, 2 * GLA_DK), const)],
        out_specs=[pl.BlockSpec((tm, GLA_DK), row), pl.BlockSpec((tm, GLA_DK), row),
                   pl.BlockSpec((tm, GLA_DV), row), pl.BlockSpec((tm, GLA_DV), row),
                   pl.BlockSpec((tm, 2 * GLA_DK), row)],
        out_shape=[jax.ShapeDtypeStruct((n, GLA_DK), BF16), jax.ShapeDtypeStruct((n, GLA_DK), BF16),
                   jax.ShapeDtypeStruct((n, GLA_DV), BF16), jax.ShapeDtypeStruct((n, GLA_DV), BF16),
                   jax.ShapeDtypeStruct((n, 2 * GLA_DK), F32)],
        compiler_params=_params(1),
        name="gla_front",
    )(x, mod.reshape(nmod, 1, 3 * d), norm_w.reshape(1, d), w_in.astype(BF16), wa1.astype(BF16),
      wa2.astype(BF16), ba)
    return outs


def _gla_chunks(chains):
    flat = [(x, ch[2]) for ch in chains for x in ch[0]]
    c = flat[0][0][0].shape[0]
    dk = flat[0][0][1].shape[1]
    row = lax.broadcasted_iota(jnp.int32, (c, c), 0)
    col = lax.broadcasted_iota(jnp.int32, (c, c), 1)
    keep = [(col >= row) if rv else (col <= row) for _, rv in flat]
    b = [_tri_dot(kp.astype(F32), x[3]) for kp, (x, _) in zip(keep, flat)]
    b_end = [bb[0:1] if rv else bb[c - 1:c] for bb, (_, rv) in zip(b, flat)]
    qb = [(x[0] * jnp.exp(bb)).astype(BF16) for (x, _), bb in zip(flat, b)]
    kb = [x[1] * jnp.exp(-bb) for (x, _), bb in zip(flat, b)]
    kd = [x[1] * jnp.exp(be - bb) for (x, _), bb, be in zip(flat, b, b_end)]
    scores = [jnp.where(kp, _dot_nt(x, y), 0.0).astype(BF16) for kp, x, y in zip(keep, qb, kb)]
    upd = [_dot_tn(x, y[2]) for (y, _), x in zip(flat, kd)]
    decay = [jnp.broadcast_to(jnp.exp(be), (dk, dk)).T for be in b_end]
    n_steps = len(chains[0][0])
    st = [ch[1] for ch in chains]
    outs = [[] for _ in chains]
    for s in range(n_steps):
        at = [ci * n_steps + s for ci in range(len(chains))]
        for ci, f in enumerate(at):
            outs[ci].append(_dot(jnp.concatenate([scores[f], qb[f]], axis=1),
                                 jnp.concatenate([flat[f][0][2], st[ci].astype(BF16)], axis=0)))
        st = [st[ci] * jnp.concatenate([decay[f]] * (st[ci].shape[1] // dk), axis=1) + upd[f]
              for ci, f in enumerate(at)]
    return list(zip(outs, st))


def _gla_scan_kernel(*refs, has_s0, emit_state):
    q_ref, k_ref, v_ref, lf_ref, lb_ref = refs[:5]
    s0_ref = refs[5] if has_s0 else None
    o_ref = refs[5 + has_s0]
    s_ref = refs[6 + has_s0] if emit_state else None
    acc_ref = refs[-1]
    t = q_ref.shape[1]
    hb = q_ref.shape[2] // GLA_DKH
    nc = t // CHUNK
    half = nc // 2
    kh = lambda h: slice(h * GLA_DKH, (h + 1) * GLA_DKH)
    vh = lambda h: slice(h * GLA_DVH, (h + 1) * GLA_DVH)

    steps = GLA_STEPS if half % GLA_STEPS == 0 else 1

    def both(accumulate):
        def body(i, carry):
            sls = [[pl.ds(pl.multiple_of(ci * CHUNK, CHUNK), CHUNK)
                    for ci in ((i * steps + s, nc - 1 - i * steps - s)[d] for s in range(steps))] for d in range(2)]
            chains = [([(q_ref[0, sl, kh(h)].astype(F32), k_ref[0, sl, kh(h)].astype(F32), v_ref[0, sl, vh(h)],
                         (lf_ref, lb_ref)[d][0, sl, kh(h)]) for sl in sls[d]], carry[d * hb + h], d == 1)
                      for d in range(2) for h in range(hb)]
            res = _gla_chunks(chains)
            for d in range(2):
                for h in range(hb):
                    for sl, o in zip(sls[d], res[d * hb + h][0]):
                        if accumulate:
                            o_ref[0, sl, vh(h)] = (acc_ref[sl, vh(h)] + o).astype(BF16)
                        else:
                            acc_ref[sl, vh(h)] = o
            return tuple(st for _, st in res)
        return body

    if has_s0:
        init = tuple(s0_ref[0, d, h] for d in range(2) for h in range(hb))
    else:
        init = tuple(jnp.zeros((GLA_DKH, GLA_DVH), F32) for _ in range(2 * hb))
    carry = lax.fori_loop(0, half // steps, both(False), init)
    carry = lax.fori_loop(half // steps, nc // steps, both(True), carry)
    if emit_state:
        for d in range(2):
            for h in range(hb):
                s_ref[0, d, h] = carry[d * hb + h]


def _gla_scan(q, k, v, lg, s0, hb, emit_state):
    bsz, t, _ = q.shape
    ng = GLA_HEADS // hb
    hd = lambda b, h: (b, 0, h)
    kblk = pl.BlockSpec((1, t, hb * GLA_DKH), hd)
    vblk = pl.BlockSpec((1, t, hb * GLA_DVH), hd)
    sblk = pl.BlockSpec((1, 2, hb, GLA_DKH, GLA_DVH), lambda b, h: (b, 0, h, 0, 0))
    s_shape = jax.ShapeDtypeStruct((bsz, 2, GLA_HEADS, GLA_DKH, GLA_DVH), F32)
    has_s0 = s0 is not None
    return pl.pallas_call(
        functools.partial(_gla_scan_kernel, has_s0=has_s0, emit_state=emit_state),
        grid=(bsz, ng),
        in_specs=[kblk, kblk, vblk, kblk, pl.BlockSpec((1, t, hb * GLA_DKH), lambda b, h: (b, 0, ng + h))]
        + [sblk] * has_s0,
        out_specs=[vblk] + [sblk] * emit_state,
        out_shape=[jax.ShapeDtypeStruct((bsz, t, GLA_DV), BF16)] + [s_shape] * emit_state,
        scratch_shapes=[pltpu.VMEM((t, hb * GLA_DVH), F32)],
        compiler_params=_params(2),
        name="gla_scan",
    )(q, k, v, lg, lg, *([s0] * has_s0))


def _gla_back_kernel(o_ref, g_ref, x_ref, mod0_ref, mod1_ref, gn_ref, wout_ref, nw1_ref, x1_ref, h1_ref):
    d = D_MODEL
    o = o_ref[...].astype(F32)
    gn = gn_ref[...]
    parts = [_rms(o[:, h * GLA_DVH:(h + 1) * GLA_DVH]) * gn for h in range(GLA_HEADS)]
    on = jnp.concatenate(parts, axis=-1) * _silu(g_ref[...].astype(F32))
    out = _dot(on, wout_ref[...])
    gate = mod0_ref[0][:, 2 * d:]
    x1 = x_ref[...] + gate * out
    x1_ref[...] = x1
    mod1 = mod1_ref[0]
    h1_ref[...] = (_rms(x1) * nw1_ref[...] * (1.0 + mod1[:, d:2 * d]) + mod1[:, :d]).astype(BF16)


def _gla_back(o, g, x, mod0, mod1, mod_idx, tm, gla_norm, w_out, norm_w1):
    n, d = x.shape
    nmod = mod0.shape[0]
    row = lambda i: (i, 0)
    const = lambda i: (0, 0)
    modspec = pl.BlockSpec((1, 1, 3 * d), lambda i: (mod_idx(i), 0, 0))
    return pl.pallas_call(
        _gla_back_kernel,
        grid=(n // tm,),
        in_specs=[pl.BlockSpec((tm, d), row), pl.BlockSpec((tm, d), row), pl.BlockSpec((tm, d), row),
                  modspec, modspec,
                  pl.BlockSpec((1, GLA_DVH), const), pl.BlockSpec((d, d), const), pl.BlockSpec((1, d), const)],
        out_specs=[pl.BlockSpec((tm, d), row), pl.BlockSpec((tm, d), row)],
        out_shape=[jax.ShapeDtypeStruct((n, d), F32), jax.ShapeDtypeStruct((n, d), BF16)],
        compiler_params=_params(1),
        name="gla_back",
    )(o, g, x, mod0.reshape(nmod, 1, 3 * d), mod1.reshape(nmod, 1, 3 * d), gla_norm.reshape(1, GLA_DVH),
      w_out.astype(BF16), norm_w1.reshape(1, d))


def _token_shift(h, up_ref, dn_ref, seq_len, grid_w):
    tm = h.shape[0]
    pos = lax.broadcasted_iota(jnp.int32, (tm, 1), 0) + pl.program_id(0) * tm
    prev = pltpu.roll(h, 1, 0)
    nxt = pltpu.roll(h, tm - 1, 0)
    t = pos % seq_len
    if grid_w is None:
        return 0.5 * (jnp.where(t == 0, 0.0, prev) + jnp.where(t == seq_len - 1, 0.0, nxt))
    col = pos % grid_w
    up = jnp.concatenate([up_ref[...].astype(F32), h[:tm - grid_w]], axis=0)
    dn = jnp.concatenate([h[grid_w:], dn_ref[...].astype(F32)], axis=0)
    up = jnp.where(t < grid_w, 0.0, up)
    dn = jnp.where(t >= seq_len - grid_w, 0.0, dn)
    left = jnp.where(col == 0, 0.0, prev)
    right = jnp.where(col == grid_w - 1, 0.0, nxt)
    return 0.25 * (up + dn + left + right)


def _rwkv_front_kernel(*refs, seq_len, grid_w):
    n_h = 1 if grid_w is None else 3
    h_ref, up_ref, dn_ref = (tuple(refs[:n_h]) + (None, None))[:3]
    (mu_ref, wrkvg_ref, w1_ref, w2_ref, w0_ref, a1_ref, a2_ref, a0_ref,
     r_ref, k_ref, v_ref, g_ref, lw_ref, a_ref) = refs[n_h:]
    h = h_ref[...].astype(F32)
    dh = _token_shift(h, up_ref, dn_ref, seq_len, grid_w) - h
    mix = lambda p: (h + dh * mu_ref[p:p + 1, :]).astype(BF16)
    r_ref[...] = jnp.dot(mix(0), wrkvg_ref[0], preferred_element_type=F32).astype(BF16)
    k_ref[...] = jnp.dot(mix(1), wrkvg_ref[1], preferred_element_type=F32).astype(BF16)
    v_ref[...] = jnp.dot(mix(2), wrkvg_ref[2], preferred_element_type=F32).astype(BF16)
    g_ref[...] = jnp.dot(mix(3), wrkvg_ref[3], preferred_element_type=F32).astype(BF16)
    tw = jnp.tanh(jnp.dot(mix(4), w1_ref[...], preferred_element_type=F32))
    lw_ref[...] = -RWKV_DECAY_SCALE * _sigmoid(w0_ref[...] + _dot(tw, w2_ref[...]))
    ta = jnp.dot(mix(5), a1_ref[...], preferred_element_type=F32)
    a_ref[...] = _sigmoid(a0_ref[...] + _dot(ta, a2_ref[...])).astype(BF16)


def _rwkv_front(h, tm, seq_len, grid_w, mu, w_rkvg, w0, w1, w2, a0, a1, a2):
    n, d = h.shape
    rk = RWKV_RANK
    cat1 = lambda w: jnp.concatenate([w[0], w[1]], axis=1)
    bd2 = lambda w: jnp.zeros((2 * rk, 2 * d), F32).at[:rk, :d].set(w[0]).at[rk:, d:].set(w[1])
    row = lambda i: (i, 0)
    const = lambda i: (0, 0)
    if grid_w is None:
        assert tm % seq_len == 0
        h_specs, h_args = [pl.BlockSpec((tm, d), row)], [h]
    else:
        assert tm % grid_w == 0 and seq_len % tm == 0
        rows_per_tile, last = tm // grid_w, n // grid_w - 1
        h_specs = [pl.BlockSpec((tm, d), row),
                   pl.BlockSpec((grid_w, d), lambda i: (jnp.maximum(i * rows_per_tile - 1, 0), 0)),
                   pl.BlockSpec((grid_w, d), lambda i: (jnp.minimum((i + 1) * rows_per_tile, last), 0))]
        h_args = [h, h, h]
    return pl.pallas_call(
        functools.partial(_rwkv_front_kernel, seq_len=seq_len, grid_w=grid_w),
        grid=(n // tm,),
        in_specs=h_specs + [
            pl.BlockSpec((8, d), const),
            pl.BlockSpec((4, d, d), lambda i: (0, 0, 0)),
            pl.BlockSpec((d, 2 * rk), const), pl.BlockSpec((2 * rk, 2 * d), const), pl.BlockSpec((1, 2 * d), const),
            pl.BlockSpec((d, 2 * rk), const), pl.BlockSpec((2 * rk, 2 * d), const), pl.BlockSpec((1, 2 * d), const)],
        out_specs=[pl.BlockSpec((tm, d), row)] * 4 + [pl.BlockSpec((tm, 2 * d), row)] * 2,
        out_shape=[jax.ShapeDtypeStruct((n, d), BF16)] * 4
        + [jax.ShapeDtypeStruct((n, 2 * d), F32), jax.ShapeDtypeStruct((n, 2 * d), BF16)],
        compiler_params=_params(1),
        name="rwkv_front",
    )(*h_args, jnp.zeros((8, d), F32).at[:6].set(mu), w_rkvg.astype(BF16),
      cat1(w1).astype(BF16), bd2(w2).astype(BF16), w0.reshape(1, 2 * d),
      cat1(a1).astype(BF16), bd2(a2).astype(BF16), a0.reshape(1, 2 * d))


def _seg_sum(x, head0):
    s0 = jnp.sum(jnp.where(head0, x, 0.0), axis=-1, keepdims=True)
    s1 = jnp.sum(jnp.where(head0, 0.0, x), axis=-1, keepdims=True)
    return jnp.where(head0, s0, s1)


def _stack(x, head0):
    return jnp.concatenate([jnp.where(head0, x, 0.0), jnp.where(head0, 0.0, x)], axis=0)


def _rwkv_chunk_terms(insts):
    n = len(insts)
    c = insts[0][0].shape[0]
    lane = lax.broadcasted_iota(jnp.int32, (c, LANES), 1)
    head0 = lane < RWKV_HEAD
    row = lax.broadcasted_iota(jnp.int32, (c, c), 0)
    col = lax.broadcasted_iota(jnp.int32, (c, c), 1)
    ti = lax.broadcasted_iota(jnp.int32, (c, LANES), 0)
    si = lane & (c - 1)
    prow = lax.broadcasted_iota(jnp.int32, (LANES, LANES), 0)
    pcol = lax.broadcasted_iota(jnp.int32, (LANES, LANES), 1)
    same_head = (prow >= RWKV_HEAD) == (pcol >= RWKV_HEAD)
    each = lambda f, *ls: [f(*xs) for xs in zip(*ls)]
    rev = [x[8] for x in insts]
    stk = lambda x: _stack(x, head0).astype(BF16)

    cum = [_tri_dot(((col >= row) if x[8] else (col <= row)).astype(F32), x[3]) for x in insts]
    tot = [cm[0:1] if rv else cm[c - 1:c] for cm, rv in zip(cum, rev)]

    a_t, r_t, b_ts, k_ts, b_h, k_h, v_s, vv, bonus = [], [], [], [], [], [], [], [], []
    for (r, k, v, lw, a, k_k, k_a, r_k, _), cm, tt in zip(insts, cum, tot):
        kk = k * k_k
        kk = kk * lax.rsqrt(jnp.maximum(_seg_sum(kk * kk, head0), 1e-24))
        kd = k * (1.0 + (a - 1.0) * k_a)
        bv = kk * a
        bonus.append(_seg_sum(r * kd * r_k, head0) * v)
        e_neg = jnp.exp(-cm)
        e_tot = jnp.exp(tt)
        b_neg, k_neg = bv * e_neg, kd * e_neg
        a_t.append(-kk * jnp.exp(cm - lw))
        r_t.append(r * jnp.exp(cm))
        b_ts.append(stk(b_neg))
        k_ts.append(stk(k_neg))
        b_h.append(b_neg * e_tot)
        k_h.append(k_neg * e_tot)
        v_s.append(stk(v))
        vv.append(v)

    strict = [(si > ti) if rv else (si < ti) for rv in rev]
    incl = [(si >= ti) if rv else (si <= ti) for rv in rev]
    ar = each(lambda x, y: jnp.concatenate([x, y], axis=0).astype(BF16), a_t, r_t)
    g_b = each(_dot_nt, ar, b_ts)
    g_k = each(_dot_nt, ar, k_ts)
    n_ab = each(lambda m_, x: jnp.where(m_, x[:c], 0.0), strict, g_b)
    l_ak = each(lambda m_, x: jnp.where(m_, x[:c], 0.0), strict, g_k)
    p_rb = each(lambda m_, x: jnp.where(m_, x[c:], 0.0), incl, g_b)
    p_rk = each(lambda m_, x: jnp.where(m_, x[c:], 0.0), incl, g_k)

    same_blk = lambda h: (ti // h) == (si // h)
    n_d = each(lambda x: jnp.where(same_blk(INV_BASE), x, 0.0), n_ab)
    steps = max(1, (INV_BASE - 1).bit_length())
    q = n_d
    tinv = n_d
    q = each(lambda x: _dot(x, stk(x)), q) if steps > 1 else q
    for j in range(1, steps):
        last = j == steps - 1
        qs = each(stk, q)
        if last:
            z = each(_dot, tinv, qs)
            tinv = each(lambda t_, x, y: t_ + x + y, tinv, q, z)
        else:
            both = each(lambda x, t_, w_: _dot(jnp.concatenate([x, t_], axis=0), w_), q, tinv, qs)
            tinv = each(lambda t_, x, y: t_ + x + y[c:], tinv, q, both)
            q = each(lambda y: y[:c], both)
    h = INV_BASE
    while h < c:
        link = same_blk(2 * h) & jnp.logical_not(same_blk(h))
        n_off = each(lambda x: jnp.where(link, x, 0.0), n_ab)
        y = each(lambda t_, x: x + _dot(t_, stk(x)), tinv, n_off)
        tinv = each(lambda t_, x: t_ + x + _dot(x, stk(t_)), tinv, y)
        h *= 2
    a_bar = each(lambda t_, x: x + _dot(t_, stk(x)), tinv, a_t)
    lv = each(lambda l_, p, y: _dot(jnp.concatenate([l_, p], axis=0), y), l_ak, p_rk, v_s)
    w = [x[:c] for x in lv]
    u0 = each(lambda t_, x: x + _dot(t_, stk(x)), tinv, w)
    r_bar = each(lambda x, p, y: x + _dot(p, stk(y)), r_t, p_rb, a_bar)
    y0 = each(lambda p, x, z: _dot(p, stk(x)) + z[c:], p_rb, u0, lv)
    zero = jnp.zeros((c, LANES), F32)
    ms = each(lambda ab, u, v_, bh, kh_: _dot_tn(
        jnp.concatenate([jnp.concatenate([ab, u], axis=1), jnp.concatenate([zero, v_], axis=1)], axis=0),
        jnp.concatenate([bh, kh_], axis=0)), a_bar, u0, vv, b_h, k_h)
    m = [jnp.where(same_head, x[:LANES], 0.0) for x in ms]
    s0p = [jnp.where(same_head, x[LANES:], 0.0) for x in ms]
    return [(r_bar[i], y0[i], m[i], s0p[i], jnp.exp(tot[i]), bonus[i]) for i in range(n)]


def _rwkv_scan_kernel(*refs, has_s0, emit_state):
    (r_ref, k_ref, v_ref, lwf_ref, lwb_ref, af_ref, ab_ref,
     kkf_ref, kkb_ref, kaf_ref, kab_ref, rkf_ref, rkb_ref, lnw_ref, lnb_ref) = refs[:15]
    s0_ref = refs[15] if has_s0 else None
    o_ref = refs[15 + has_s0]
    s_ref = refs[16 + has_s0] if emit_state else None
    rbar_ref, y0_ref, m_ref, s0p_ref, dec_ref, bonus_ref, yf_ref, yb_ref = refs[16 + has_s0 + emit_state:]
    bb, t = r_ref.shape[0], r_ref.shape[1]
    nc = t // CHUNK
    dirs = ((lwf_ref, af_ref, kkf_ref, kaf_ref, rkf_ref, False), (lwb_ref, ab_ref, kkb_ref, kab_ref, rkb_ref, True))
    group = min(bb * nc, PREP_GROUP)
    assert (bb * nc) % group == 0
    chunk_rows = lambda ci: pl.ds(pl.multiple_of(ci * CHUNK, CHUNK), CHUNK)

    def prep(gi, carry):
        insts, where = [], []
        at = [((gi * group + j) // nc, chunk_rows((gi * group + j) % nc)) for j in range(group)]
        for j, (e, sl) in enumerate(at):
            r, k, v = (ref[e, sl, :].astype(F32) for ref in (r_ref, k_ref, v_ref))
            for d, (lw_ref, a_ref, kk_ref, ka_ref, rk_ref, reverse) in enumerate(dirs):
                insts.append((r, k, v, lw_ref[e, sl, :], a_ref[e, sl, :].astype(F32), kk_ref[...], ka_ref[...],
                              rk_ref[...], reverse))
                where.append((d, gi * group + j, e, sl))
        terms = _rwkv_chunk_terms(insts)
        for (d, fi, e, sl), (r_bar, y0, m, s0p, dec, bon) in zip(where, terms):
            rbar_ref[d, fi] = r_bar.astype(BF16)
            y0_ref[d, fi] = y0
            m_ref[d, fi] = m.astype(BF16)
            s0p_ref[d, fi] = s0p
            dec_ref[d, fi] = jnp.broadcast_to(dec, (8, LANES))
        for j in range(group):
            _, _, e, sl = where[2 * j]
            bonus_ref[e, sl, :] = terms[2 * j][5] + terms[2 * j + 1][5]
        return carry

    lax.fori_loop(0, bb * nc // group, prep, 0)

    seq_steps = SEQ_STEPS if nc % SEQ_STEPS == 0 else 1
    idx = [(d, e) for d in range(2) for e in range(bb)]

    def seq(i, carry):
        for s in range(seq_steps):
            cis = (i * seq_steps + s, nc - 1 - i * seq_steps - s)
            sb16 = [st.astype(BF16) for st in carry]
            sm = [jnp.dot(sb16[n], m_ref[d, e * nc + cis[d]], preferred_element_type=F32)
                  for n, (d, e) in enumerate(idx)]
            ys = [_dot_nt(rbar_ref[d, e * nc + cis[d]], sb16[n]) + y0_ref[d, e * nc + cis[d]]
                  for n, (d, e) in enumerate(idx)]
            for n, (d, e) in enumerate(idx):
                (yf_ref, yb_ref)[d][e, chunk_rows(cis[d]), :] = ys[n]
            carry = tuple(carry[n] * dec_ref[d, e * nc + cis[d]][0:1] + sm[n] + s0p_ref[d, e * nc + cis[d]]
                          for n, (d, e) in enumerate(idx))
        return carry

    if has_s0:
        init = tuple(s0_ref[e, d, 0] for d in range(2) for e in range(bb))
    else:
        init = tuple(jnp.zeros((LANES, LANES), F32) for _ in range(2 * bb))
    final = lax.fori_loop(0, nc // seq_steps, seq, init)
    if emit_state:
        for d in range(2):
            for e in range(bb):
                s = final[d * bb + e]
                s_ref[e, d, 0] = s[:RWKV_HEAD, :RWKV_HEAD]
                s_ref[e, d, 1] = pltpu.roll(s, RWKV_HEAD, 1)[RWKV_HEAD:, :RWKV_HEAD]

    prow = lax.broadcasted_iota(jnp.int32, (LANES, LANES), 0)
    pcol = lax.broadcasted_iota(jnp.int32, (LANES, LANES), 1)
    head_ones = ((prow >= RWKV_HEAD) == (pcol >= RWKV_HEAD)).astype(BF16)
    inv_n = 1.0 / RWKV_HEAD

    def head_mean(x):
        hi = x.astype(BF16)
        lo = (x - hi.astype(F32)).astype(BF16)
        both = jnp.dot(jnp.concatenate([hi, lo], axis=0), head_ones, preferred_element_type=F32)
        return (both[:x.shape[0]] + both[x.shape[0]:]) * inv_n

    def finish(gi, carry):
        at = [((gi * group + j) // nc, chunk_rows((gi * group + j) % nc)) for j in range(group)]
        y = [yf_ref[e, sl, :] + yb_ref[e, sl, :] for e, sl in at]
        yc = [x - head_mean(x) for x in y]
        var = [head_mean(x * x) for x in yc]
        for (e, sl), x, vr in zip(at, yc, var):
            o_ref[e, sl, :] = (x * lax.rsqrt(vr + LNX_EPS) * lnw_ref[...] + lnb_ref[...]
                               + bonus_ref[e, sl, :]).astype(BF16)
        return carry

    lax.fori_loop(0, bb * nc // group, finish, 0)


def _rwkv_scan(r, k, v, lw, a, k_k, k_a, r_k, lnx_w, lnx_b, s0, bb, emit_state):
    bsz, t, d = r.shape
    nc = t // CHUNK
    seq = lambda b, p: (b, 0, p)
    seq_b = lambda b, p: (b, 0, RWKV_PAIRS + p)
    par = lambda b, p: (0, p)
    par_b = lambda b, p: (0, RWKV_PAIRS + p)
    sblk = pl.BlockSpec((bb, t, LANES), seq)
    sblk_b = pl.BlockSpec((bb, t, LANES), seq_b)
    pblk = pl.BlockSpec((1, LANES), par)
    pblk_b = pl.BlockSpec((1, LANES), par_b)
    s0blk = pl.BlockSpec((bb, 2, 1, LANES, LANES), lambda b, p: (b, 0, p, 0, 0))
    s_blk = pl.BlockSpec((bb, 2, 2, RWKV_HEAD, RWKV_HEAD), lambda b, p: (b, 0, p, 0, 0))
    s_shape = jax.ShapeDtypeStruct((bsz, 2, RWKV_HEADS, RWKV_HEAD, RWKV_HEAD), F32)
    has_s0 = s0 is not None
    k_k, k_a, r_k = (p.reshape(1, 2 * d) for p in (k_k, k_a, r_k))
    return pl.pallas_call(
        functools.partial(_rwkv_scan_kernel, has_s0=has_s0, emit_state=emit_state),
        grid=(bsz // bb, RWKV_PAIRS),
        in_specs=[sblk, sblk, sblk, sblk, sblk_b, sblk, sblk_b,
                  pblk, pblk_b, pblk, pblk_b, pblk, pblk_b, pblk, pblk] + [s0blk] * has_s0,
        out_specs=[sblk] + [s_blk] * emit_state,
        out_shape=[jax.ShapeDtypeStruct((bsz, t, d), BF16)] + [s_shape] * emit_state,
        scratch_shapes=[pltpu.VMEM((2, bb * nc, CHUNK, LANES), BF16), pltpu.VMEM((2, bb * nc, CHUNK, LANES), F32),
                        pltpu.VMEM((2, bb * nc, LANES, LANES), BF16), pltpu.VMEM((2, bb * nc, LANES, LANES), F32),
                        pltpu.VMEM((2, bb * nc, 8, LANES), F32), pltpu.VMEM((bb, t, LANES), F32),
                        pltpu.VMEM((bb, t, LANES), F32), pltpu.VMEM((bb, t, LANES), F32)],
        compiler_params=_params(2),
        name="rwkv_scan",
    )(r, k, v, lw, lw, a, a, k_k, k_k, k_a, k_a, r_k, r_k,
      lnx_w.reshape(1, d), lnx_b.reshape(1, d), *([s0] * has_s0))


def _rwkv_back_kernel(o_ref, g_ref, x_ref, mod_ref, wout_ref, nf_ref, y_ref):
    out = _dot(o_ref[...].astype(F32) * _silu(g_ref[...].astype(F32)), wout_ref[...])
    x2 = x_ref[...] + mod_ref[0][:, 2 * D_MODEL:] * out
    y_ref[...] = _rms(x2) * nf_ref[...]


def _rwkv_back(o, g, x, mod1, mod_idx, tm, w_out, norm_f):
    n, d = x.shape
    nmod = mod1.shape[0]
    row = lambda i: (i, 0)
    const = lambda i: (0, 0)
    return pl.pallas_call(
        _rwkv_back_kernel,
        grid=(n // tm,),
        in_specs=[pl.BlockSpec((tm, d), row), pl.BlockSpec((tm, d), row), pl.BlockSpec((tm, d), row),
                  pl.BlockSpec((1, 1, 3 * d), lambda i: (mod_idx(i), 0, 0)),
                  pl.BlockSpec((d, d), const), pl.BlockSpec((1, d), const)],
        out_specs=pl.BlockSpec((tm, d), row),
        out_shape=jax.ShapeDtypeStruct((n, d), F32),
        compiler_params=_params(1),
        name="rwkv_back",
    )(o, g, x, mod1.reshape(nmod, 1, 3 * d), w_out.astype(BF16), norm_f.reshape(1, d))


def _pair_blockdiag(s):
    b = s.shape[0]
    s = s.reshape(b, 2, RWKV_PAIRS, 2, RWKV_HEAD, RWKV_HEAD)
    z = jnp.zeros_like(s[:, :, :, 0])
    top = jnp.concatenate([s[:, :, :, 0], z], axis=-1)
    bot = jnp.concatenate([z, s[:, :, :, 1]], axis=-1)
    return jnp.concatenate([top, bot], axis=-2)


def _tile_rows(n_seq, t, may_span):
    if t % TOKEN_TILE == 0:
        return TOKEN_TILE
    if may_span and TOKEN_TILE % t == 0 and (n_seq * t) % TOKEN_TILE == 0:
        return TOKEN_TILE
    return t


def kernel(x_prompt, x_sample, state_gla, state_rwkv, c, c_ctx, w_mod, b_mod, norm_w, gla_w_in, gla_w_a1,
           gla_w_a2, gla_b_a, gla_norm, gla_w_out, rwkv_mu, rwkv_w_rkvg, rwkv_w0, rwkv_w1, rwkv_w2, rwkv_a0,
           rwkv_a1, rwkv_a2, rwkv_k_k, rwkv_k_a, rwkv_r_k, rwkv_lnx_w, rwkv_lnx_b, rwkv_w_out, norm_f):
    d = D_MODEL
    bp, tp, _ = x_prompt.shape
    bs, ts, _ = x_sample.shape
    assert tp % CHUNK == 0 and ts % CHUNK == 0 and ts % GRID_W == 0

    nrows = -(-(1 + bs) // 8) * 8
    cond = jnp.zeros((nrows, d), F32).at[0].set(c_ctx).at[1:1 + bs].set(c)
    cond = cond * (1.0 / (1.0 + jnp.exp(-cond)))
    mod0 = _matmul_bias(cond, w_mod[0], b_mod[0].reshape(1, 3 * d))
    mod1 = _matmul_bias(cond, w_mod[1], b_mod[1].reshape(1, 3 * d))

    def trunk(x3, tm, mod_idx, grid_w, gla_s0, rwkv_s0, rwkv_bb, emit_state):
        b, t, _ = x3.shape
        n = b * t
        x = x3.reshape(n, d)
        seq = lambda arr: arr.reshape(b, t, arr.shape[-1])
        q, k, v, g, lg = _gla_front(x, mod0, mod_idx, tm, norm_w[0], gla_w_in[0], gla_w_a1[0], gla_w_a2[0], gla_b_a[0])
        o, *gla_s = _gla_scan(seq(q), seq(k), seq(v), seq(lg), gla_s0, GLA_HEADS, emit_state)
        x1, h1 = _gla_back(o.reshape(n, d), g, x, mod0, mod1, mod_idx, tm, gla_norm[0], gla_w_out[0], norm_w[1])
        r, k, v, g, lw, a = _rwkv_front(h1, tm, t, grid_w, rwkv_mu[0], rwkv_w_rkvg[0], rwkv_w0[0], rwkv_w1[0], rwkv_w2[0],
                                        rwkv_a0[0], rwkv_a1[0], rwkv_a2[0])
        o, *rwkv_s = _rwkv_scan(seq(r), seq(k), seq(v), seq(lw), seq(a), rwkv_k_k[0], rwkv_k_a[0], rwkv_r_k[0],
                                rwkv_lnx_w[0], rwkv_lnx_b[0], rwkv_s0, rwkv_bb, emit_state)
        y = _rwkv_back(o.reshape(n, d), g, x1, mod1, mod_idx, tm, rwkv_w_out[0], norm_f)
        return y.reshape(b, t, d), gla_s, rwkv_s

    y_prompt, (gla_s,), (rwkv_s,) = trunk(x_prompt, _tile_rows(bp, tp, True), lambda i: 0, None, None, None,
                                          min(bp, 4), True)

    tm_s = _tile_rows(bs, ts, False)
    y_sample, _, _ = trunk(x_sample, tm_s, lambda i: 1 + i // (ts // tm_s), GRID_W, state_gla[:, 0],
                           _pair_blockdiag(state_rwkv[:, 0]), min(bs, 4), False)
    return (y_prompt, y_sample, gla_s[:, None], rwkv_s[:, None])
```

```python
import functools

import jax
import jax.numpy as jnp
from jax import lax
from jax.experimental import pallas as pl
from jax.experimental.pallas import tpu as pltpu

F32 = jnp.float32
BF16 = jnp.bfloat16

D_MODEL = 1024
EPS = 1e-6
GRID_W = 64
GLA_HEADS = 4
GLA_DK = 512
GLA_DV = 1024
GLA_DKH = 128
GLA_DVH = 256
GLA_GATE_RANK = 16
GLA_GATE_NORM = 16.0
RWKV_HEAD = 64
RWKV_HEADS = 16
RWKV_PAIRS = 8
RWKV_RANK = 64
LNX_EPS = 64e-5
RWKV_DECAY_SCALE = 0.6065306597126334
CHUNK = 64
SEQ_STEPS = 4
GLA_STEPS = 2
INV_BASE = 8
TOKEN_TILE = 512
WIDE_TILE = 1024
PREP_GROUP = 8
LANES = 128
VMEM_LIMIT = 56 * 1024 * 1024


def _dot(a, b):
    return jnp.dot(a.astype(BF16), b.astype(BF16), preferred_element_type=F32)


def _dot_nt(a, b):
    return lax.dot_general(a.astype(BF16), b.astype(BF16), (((1,), (1,)), ((), ())),
                           preferred_element_type=F32)


def _dot_tn(a, b):
    return lax.dot_general(a.astype(BF16), b.astype(BF16), (((0,), (0,)), ((), ())),
                           preferred_element_type=F32)


def _tri_dot(tri, x):
    hi = x.astype(BF16)
    lo = (x - hi.astype(F32)).astype(BF16)
    t = tri.astype(BF16)
    return jnp.dot(t, hi, preferred_element_type=F32) + jnp.dot(t, lo, preferred_element_type=F32)


def _log_sigmoid(z):
    return jnp.minimum(z, 0.0) - jnp.log(1.0 + jnp.exp(-jnp.abs(z)))


def _sigmoid(z):
    return 1.0 / (1.0 + jnp.exp(-z))


def _silu(z):
    return z * _sigmoid(z)


def _rms(x):
    return x * lax.rsqrt(jnp.mean(x * x, axis=-1, keepdims=True) + EPS)


def _params(n_grid_dims):
    return pltpu.CompilerParams(dimension_semantics=("arbitrary",) * n_grid_dims,
                                vmem_limit_bytes=VMEM_LIMIT)


def _mm_kernel(x_ref, w_ref, b_ref, o_ref):
    o_ref[...] = _dot(x_ref[...], w_ref[...]) + b_ref[...]


def _matmul_bias(x, w, b):
    m, k = x.shape
    n = w.shape[1]
    return pl.pallas_call(
        _mm_kernel,
        grid=(1,),
        in_specs=[pl.BlockSpec((m, k), lambda i: (0, 0)),
                  pl.BlockSpec((k, n), lambda i: (0, 0)),
                  pl.BlockSpec((1, n), lambda i: (0, 0))],
        out_specs=pl.BlockSpec((m, n), lambda i: (0, 0)),
        out_shape=jax.ShapeDtypeStruct((m, n), F32),
        compiler_params=_params(1),
        name="mod_matmul",
    )(x, w.astype(BF16), b)


def _gla_front_kernel(x_ref, mod_ref, nw_ref, win_ref, wa1_ref, wa2_ref, ba_ref,
                      q_ref, k_ref, v_ref, g_ref, lg_ref):
    d = D_MODEL
    mod = mod_ref[0]
    shift, scale = mod[:, :d], mod[:, d:2 * d]
    h = _rms(x_ref[...]) * nw_ref[...] * (1.0 + scale) + shift
    hb = h.astype(BF16)
    proj = lambda lo, hi: jnp.dot(hb, win_ref[:, lo:hi], preferred_element_type=F32)
    q_ref[...] = (proj(0, GLA_DK) * (GLA_DKH ** -0.5)).astype(BF16)
    k_ref[...] = proj(GLA_DK, 2 * GLA_DK).astype(BF16)
    v_ref[...] = proj(2 * GLA_DK, 2 * GLA_DK + GLA_DV).astype(BF16)
    g_ref[...] = proj(2 * GLA_DK + GLA_DV, 2 * GLA_DK + 2 * GLA_DV).astype(BF16)
    t = jnp.dot(hb, wa1_ref[...], preferred_element_type=F32)
    z = _dot(t, wa2_ref[...]) + ba_ref[...]
    lg_ref[...] = _log_sigmoid(z) * (1.0 / GLA_GATE_NORM)


def _gla_front(x, mod, mod_idx, tm, norm_w, w_in, w_a1, w_a2, b_a):
    n, d = x.shape
    nmod = mod.shape[0]
    wa1 = jnp.zeros((d, LANES), F32).at[:, :GLA_GATE_RANK].set(w_a1[0]).at[:, GLA_GATE_RANK:2 * GLA_GATE_RANK].set(w_a1[1])
    wa2 = jnp.zeros((LANES, 2 * GLA_DK), F32).at[:GLA_GATE_RANK, :GLA_DK].set(w_a2[0])
    wa2 = wa2.at[GLA_GATE_RANK:2 * GLA_GATE_RANK, GLA_DK:].set(w_a2[1])
    ba = b_a.reshape(1, 2 * GLA_DK)
    row = lambda i: (i, 0)
    const = lambda i: (0, 0)
    n_in = w_in.shape[1]
    outs = pl.pallas_call(
        _gla_front_kernel,
        grid=(n // tm,),
        in_specs=[pl.BlockSpec((tm, d), row),
                  pl.BlockSpec((1, 1, 3 * d), lambda i: (mod_idx(i), 0, 0)),
                  pl.BlockSpec((1, d), const),
                  pl.BlockSpec((d, n_in), const),
                  pl.BlockSpec((d, LANES), const),
                  pl.BlockSpec((LANES, 2 * GLA_DK), const),
                  pl.BlockSpec((1, 2 * GLA_DK), const)],
        out_specs=[pl.BlockSpec((tm, GLA_DK), row), pl.BlockSpec((tm, GLA_DK), row),
                   pl.BlockSpec((tm, GLA_DV), row), pl.BlockSpec((tm, GLA_DV), row),
                   pl.BlockSpec((tm, 2 * GLA_DK), row)],
        out_shape=[jax.ShapeDtypeStruct((n, GLA_DK), BF16), jax.ShapeDtypeStruct((n, GLA_DK), BF16),
                   jax.ShapeDtypeStruct((n, GLA_DV), BF16), jax.ShapeDtypeStruct((n, GLA_DV), BF16),
                   jax.ShapeDtypeStruct((n, 2 * GLA_DK), F32)],
        compiler_params=_params(1),
        name="gla_front",
    )(x, mod.reshape(nmod, 1, 3 * d), norm_w.reshape(1, d), w_in.astype(BF16), wa1.astype(BF16),
      wa2.astype(BF16), ba)
    return outs


def _gla_chunks(chains):
    flat = [(x, ch[2]) for ch in chains for x in ch[0]]
    c = flat[0][0][0].shape[0]
    dk = flat[0][0][1].shape[1]
    row = lax.broadcasted_iota(jnp.int32, (c, c), 0)
    col = lax.broadcasted_iota(jnp.int32, (c, c), 1)
    keep = [(col >= row) if rv else (col <= row) for _, rv in flat]
    b = [_tri_dot(kp.astype(F32), x[3]) for kp, (x, _) in zip(keep, flat)]
    b_end = [bb[0:1] if rv else bb[c - 1:c] for bb, (_, rv) in zip(b, flat)]
    qb = [(x[0] * jnp.exp(bb)).astype(BF16) for (x, _), bb in zip(flat, b)]
    kb = [x[1] * jnp.exp(-bb) for (x, _), bb in zip(flat, b)]
    kd = [x[1] * jnp.exp(be - bb) for (x, _), bb, be in zip(flat, b, b_end)]
    scores = [jnp.where(kp, _dot_nt(x, y), 0.0).astype(BF16) for kp, x, y in zip(keep, qb, kb)]
    upd = [_dot_tn(x, y[2]) for (y, _), x in zip(flat, kd)]
    decay = [jnp.broadcast_to(jnp.exp(be), (dk, dk)).T for be in b_end]
    n_steps = len(chains[0][0])
    st = [ch[1] for ch in chains]
    outs = [[] for _ in chains]
    for s in range(n_steps):
        at = [ci * n_steps + s for ci in range(len(chains))]
        for ci, f in enumerate(at):
            outs[ci].append(_dot(jnp.concatenate([scores[f], qb[f]], axis=1),
                                 jnp.concatenate([flat[f][0][2], st[ci].astype(BF16)], axis=0)))
        st = [st[ci] * jnp.concatenate([decay[f]] * (st[ci].shape[1] // dk), axis=1) + upd[f]
              for ci, f in enumerate(at)]
    return list(zip(outs, st))


def _gla_scan_kernel(*refs, has_s0, emit_state):
    q_ref, k_ref, v_ref, lf_ref, lb_ref = refs[:5]
    s0_ref = refs[5] if has_s0 else None
    o_ref = refs[5 + has_s0]
    s_ref = refs[6 + has_s0] if emit_state else None
    acc_ref = refs[-1]
    t = q_ref.shape[1]
    hb = q_ref.shape[2] // GLA_DKH
    nc = t // CHUNK
    half = nc // 2
    kh = lambda h: slice(h * GLA_DKH, (h + 1) * GLA_DKH)
    vh = lambda h: slice(h * GLA_DVH, (h + 1) * GLA_DVH)

    steps = GLA_STEPS if half % GLA_STEPS == 0 else 1

    def both(accumulate):
        def body(i, carry):
            sls = [[pl.ds(pl.multiple_of(ci * CHUNK, CHUNK), CHUNK)
                    for ci in ((i * steps + s, nc - 1 - i * steps - s)[d] for s in range(steps))] for d in range(2)]
            chains = [([(q_ref[0, sl, kh(h)].astype(F32), k_ref[0, sl, kh(h)].astype(F32), v_ref[0, sl, vh(h)],
                         (lf_ref, lb_ref)[d][0, sl, kh(h)]) for sl in sls[d]], carry[d * hb + h], d == 1)
                      for d in range(2) for h in range(hb)]
            res = _gla_chunks(chains)
            for d in range(2):
                for h in range(hb):
                    for sl, o in zip(sls[d], res[d * hb + h][0]):
                        if accumulate:
                            o_ref[0, sl, vh(h)] = (acc_ref[sl, vh(h)] + o).astype(BF16)
                        else:
                            acc_ref[sl, vh(h)] = o
            return tuple(st for _, st in res)
        return body

    if has_s0:
        init = tuple(s0_ref[0, d, h] for d in range(2) for h in range(hb))
    else:
        init = tuple(jnp.zeros((GLA_DKH, GLA_DVH), F32) for _ in range(2 * hb))
    carry = lax.fori_loop(0, half // steps, both(False), init)
    carry = lax.fori_loop(half // steps, nc // steps, both(True), carry)
    if emit_state:
        for d in range(2):
            for h in range(hb):
                s_ref[0, d, h] = carry[d * hb + h]


def _gla_scan(q, k, v, lg, s0, hb, emit_state):
    bsz, t, _ = q.shape
    ng = GLA_HEADS // hb
    hd = lambda b, h: (b, 0, h)
    kblk = pl.BlockSpec((1, t, hb * GLA_DKH), hd)
    vblk = pl.BlockSpec((1, t, hb * GLA_DVH), hd)
    sblk = pl.BlockSpec((1, 2, hb, GLA_DKH, GLA_DVH), lambda b, h: (b, 0, h, 0, 0))
    s_shape = jax.ShapeDtypeStruct((bsz, 2, GLA_HEADS, GLA_DKH, GLA_DVH), F32)
    has_s0 = s0 is not None
    return pl.pallas_call(
        functools.partial(_gla_scan_kernel, has_s0=has_s0, emit_state=emit_state),
        grid=(bsz, ng),
        in_specs=[kblk, kblk, vblk, kblk, pl.BlockSpec((1, t, hb * GLA_DKH), lambda b, h: (b, 0, ng + h))]
        + [sblk] * has_s0,
        out_specs=[vblk] + [sblk] * emit_state,
        out_shape=[jax.ShapeDtypeStruct((bsz, t, GLA_DV), BF16)] + [s_shape] * emit_state,
        scratch_shapes=[pltpu.VMEM((t, hb * GLA_DVH), F32)],
        compiler_params=_params(2),
        name="gla_scan",
    )(q, k, v, lg, lg, *([s0] * has_s0))


def _gla_back_kernel(o_ref, g_ref, x_ref, mod0_ref, mod1_ref, gn_ref, wout_ref, nw1_ref, x1_ref, h1_ref):
    d = D_MODEL
    o = o_ref[...].astype(F32)
    gn = gn_ref[...]
    parts = [_rms(o[:, h * GLA_DVH:(h + 1) * GLA_DVH]) * gn for h in range(GLA_HEADS)]
    on = jnp.concatenate(parts, axis=-1) * _silu(g_ref[...].astype(F32))
    out = _dot(on, wout_ref[...])
    gate = mod0_ref[0][:, 2 * d:]
    x1 = x_ref[...] + gate * out
    x1_ref[...] = x1
    mod1 = mod1_ref[0]
    h1_ref[...] = (_rms(x1) * nw1_ref[...] * (1.0 + mod1[:, d:2 * d]) + mod1[:, :d]).astype(BF16)


def _gla_back(o, g, x, mod0, mod1, mod_idx, tm, gla_norm, w_out, norm_w1):
    n, d = x.shape
    nmod = mod0.shape[0]
    row = lambda i: (i, 0)
    const = lambda i: (0, 0)
    modspec = pl.BlockSpec((1, 1, 3 * d), lambda i: (mod_idx(i), 0, 0))
    return pl.pallas_call(
        _gla_back_kernel,
        grid=(n // tm,),
        in_specs=[pl.BlockSpec((tm, d), row), pl.BlockSpec((tm, d), row), pl.BlockSpec((tm, d), row),
                  modspec, modspec,
                  pl.BlockSpec((1, GLA_DVH), const), pl.BlockSpec((d, d), const), pl.BlockSpec((1, d), const)],
        out_specs=[pl.BlockSpec((tm, d), row), pl.BlockSpec((tm, d), row)],
        out_shape=[jax.ShapeDtypeStruct((n, d), F32), jax.ShapeDtypeStruct((n, d), BF16)],
        compiler_params=_params(1),
        name="gla_back",
    )(o, g, x, mod0.reshape(nmod, 1, 3 * d), mod1.reshape(nmod, 1, 3 * d), gla_norm.reshape(1, GLA_DVH),
      w_out.astype(BF16), norm_w1.reshape(1, d))


def _token_shift(h, up_ref, dn_ref, seq_len, grid_w):
    tm = h.shape[0]
    pos = lax.broadcasted_iota(jnp.int32, (tm, 1), 0) + pl.program_id(0) * tm
    prev = pltpu.roll(h, 1, 0)
    nxt = pltpu.roll(h, tm - 1, 0)
    t = pos % seq_len
    if grid_w is None:
        return 0.5 * (jnp.where(t == 0, 0.0, prev) + jnp.where(t == seq_len - 1, 0.0, nxt))
    col = pos % grid_w
    up = jnp.concatenate([up_ref[...].astype(F32), h[:tm - grid_w]], axis=0)
    dn = jnp.concatenate([h[grid_w:], dn_ref[...].astype(F32)], axis=0)
    up = jnp.where(t < grid_w, 0.0, up)
    dn = jnp.where(t >= seq_len - grid_w, 0.0, dn)
    left = jnp.where(col == 0, 0.0, prev)
    right = jnp.where(col == grid_w - 1, 0.0, nxt)
    return 0.25 * (up + dn + left + right)


def _rwkv_front_kernel(*refs, seq_len, grid_w):
    n_h = 1 if grid_w is None else 3
    h_ref, up_ref, dn_ref = (tuple(refs[:n_h]) + (None, None))[:3]
    (mu_ref, wrkvg_ref, w1_ref, w2_ref, w0_ref, a1_ref, a2_ref, a0_ref,
     r_ref, k_ref, v_ref, g_ref, lw_ref, a_ref) = refs[n_h:]
    h = h_ref[...].astype(F32)
    dh = _token_shift(h, up_ref, dn_ref, seq_len, grid_w) - h
    mix = lambda p: (h + dh * mu_ref[p:p + 1, :]).astype(BF16)
    r_ref[...] = jnp.dot(mix(0), wrkvg_ref[0], preferred_element_type=F32).astype(BF16)
    k_ref[...] = jnp.dot(mix(1), wrkvg_ref[1], preferred_element_type=F32).astype(BF16)
    v_ref[...] = jnp.dot(mix(2), wrkvg_ref[2], preferred_element_type=F32).astype(BF16)
    g_ref[...] = jnp.dot(mix(3), wrkvg_ref[3], preferred_element_type=F32).astype(BF16)
    tw = jnp.tanh(jnp.dot(mix(4), w1_ref[...], preferred_element_type=F32))
    lw_ref[...] = -RWKV_DECAY_SCALE * _sigmoid(w0_ref[...] + _dot(tw, w2_ref[...]))
    ta = jnp.dot(mix(5), a1_ref[...], preferred_element_type=F32)
    a_ref[...] = _sigmoid(a0_ref[...] + _dot(ta, a2_ref[...])).astype(BF16)


def _rwkv_front(h, tm, seq_len, grid_w, mu, w_rkvg, w0, w1, w2, a0, a1, a2):
    n, d = h.shape
    rk = RWKV_RANK
    cat1 = lambda w: jnp.concatenate([w[0], w[1]], axis=1)
    bd2 = lambda w: jnp.zeros((2 * rk, 2 * d), F32).at[:rk, :d].set(w[0]).at[rk:, d:].set(w[1])
    row = lambda i: (i, 0)
    const = lambda i: (0, 0)
    if grid_w is None:
        assert tm % seq_len == 0
        h_specs, h_args = [pl.BlockSpec((tm, d), row)], [h]
    else:
        assert tm % grid_w == 0 and seq_len % tm == 0
        rows_per_tile, last = tm // grid_w, n // grid_w - 1
        h_specs = [pl.BlockSpec((tm, d), row),
                   pl.BlockSpec((grid_w, d), lambda i: (jnp.maximum(i * rows_per_tile - 1, 0), 0)),
                   pl.BlockSpec((grid_w, d), lambda i: (jnp.minimum((i + 1) * rows_per_tile, last), 0))]
        h_args = [h, h, h]
    return pl.pallas_call(
        functools.partial(_rwkv_front_kernel, seq_len=seq_len, grid_w=grid_w),
        grid=(n // tm,),
        in_specs=h_specs + [
            pl.BlockSpec((8, d), const),
            pl.BlockSpec((4, d, d), lambda i: (0, 0, 0)),
            pl.BlockSpec((d, 2 * rk), const), pl.BlockSpec((2 * rk, 2 * d), const), pl.BlockSpec((1, 2 * d), const),
            pl.BlockSpec((d, 2 * rk), const), pl.BlockSpec((2 * rk, 2 * d), const), pl.BlockSpec((1, 2 * d), const)],
        out_specs=[pl.BlockSpec((tm, d), row)] * 4 + [pl.BlockSpec((tm, 2 * d), row)] * 2,
        out_shape=[jax.ShapeDtypeStruct((n, d), BF16)] * 4
        + [jax.ShapeDtypeStruct((n, 2 * d), F32), jax.ShapeDtypeStruct((n, 2 * d), BF16)],
        compiler_params=_params(1),
        name="rwkv_front",
    )(*h_args, jnp.zeros((8, d), F32).at[:6].set(mu), w_rkvg.astype(BF16),
      cat1(w1).astype(BF16), bd2(w2).astype(BF16), w0.reshape(1, 2 * d),
      cat1(a1).astype(BF16), bd2(a2).astype(BF16), a0.reshape(1, 2 * d))


def _seg_sum(x, head0):
    s0 = jnp.sum(jnp.where(head0, x, 0.0), axis=-1, keepdims=True)
    s1 = jnp.sum(jnp.where(head0, 0.0, x), axis=-1, keepdims=True)
    return jnp.where(head0, s0, s1)


def _stack(x, head0):
    return jnp.concatenate([jnp.where(head0, x, 0.0), jnp.where(head0, 0.0, x)], axis=0)


def _rwkv_chunk_terms(insts):
    n = len(insts)
    c = insts[0][0].shape[0]
    lane = lax.broadcasted_iota(jnp.int32, (c, LANES), 1)
    head0 = lane < RWKV_HEAD
    row = lax.broadcasted_iota(jnp.int32, (c, c), 0)
    col = lax.broadcasted_iota(jnp.int32, (c, c), 1)
    ti = lax.broadcasted_iota(jnp.int32, (c, LANES), 0)
    si = lane & (c - 1)
    prow = lax.broadcasted_iota(jnp.int32, (LANES, LANES), 0)
    pcol = lax.broadcasted_iota(jnp.int32, (LANES, LANES), 1)
    same_head = (prow >= RWKV_HEAD) == (pcol >= RWKV_HEAD)
    each = lambda f, *ls: [f(*xs) for xs in zip(*ls)]
    rev = [x[8] for x in insts]
    stk = lambda x: _stack(x, head0).astype(BF16)

    cum = [_tri_dot(((col >= row) if x[8] else (col <= row)).astype(F32), x[3]) for x in insts]
    tot = [cm[0:1] if rv else cm[c - 1:c] for cm, rv in zip(cum, rev)]

    a_t, r_t, b_ts, k_ts, b_h, k_h, v_s, vv, bonus = [], [], [], [], [], [], [], [], []
    for (r, k, v, lw, a, k_k, k_a, r_k, _), cm, tt in zip(insts, cum, tot):
        kk = k * k_k
        kk = kk * lax.rsqrt(jnp.maximum(_seg_sum(kk * kk, head0), 1e-24))
        kd = k * (1.0 + (a - 1.0) * k_a)
        bv = kk * a
        bonus.append(_seg_sum(r * kd * r_k, head0) * v)
        e_neg = jnp.exp(-cm)
        e_tot = jnp.exp(tt)
        b_neg, k_neg = bv * e_neg, kd * e_neg
        a_t.append(-kk * jnp.exp(cm - lw))
        r_t.append(r * jnp.exp(cm))
        b_ts.append(stk(b_neg))
        k_ts.append(stk(k_neg))
        b_h.append(b_neg * e_tot)
        k_h.append(k_neg * e_tot)
        v_s.append(stk(v))
        vv.append(v)

    strict = [(si > ti) if rv else (si < ti) for rv in rev]
    incl = [(si >= ti) if rv else (si <= ti) for rv in rev]
    ar = each(lambda x, y: jnp.concatenate([x, y], axis=0).astype(BF16), a_t, r_t)
    g_b = each(_dot_nt, ar, b_ts)
    g_k = each(_dot_nt, ar, k_ts)
    n_ab = each(lambda m_, x: jnp.where(m_, x[:c], 0.0), strict, g_b)
    l_ak = each(lambda m_, x: jnp.where(m_, x[:c], 0.0), strict, g_k)
    p_rb = each(lambda m_, x: jnp.where(m_, x[c:], 0.0), incl, g_b)
    p_rk = each(lambda m_, x: jnp.where(m_, x[c:], 0.0), incl, g_k)

    same_blk = lambda h: (ti // h) == (si // h)
    n_d = each(lambda x: jnp.where(same_blk(INV_BASE), x, 0.0), n_ab)
    steps = max(1, (INV_BASE - 1).bit_length())
    q = n_d
    tinv = n_d
    q = each(lambda x: _dot(x, stk(x)), q) if steps > 1 else q
    for j in range(1, steps):
        last = j == steps - 1
        qs = each(stk, q)
        if last:
            z = each(_dot, tinv, qs)
            tinv = each(lambda t_, x, y: t_ + x + y, tinv, q, z)
        else:
            both = each(lambda x, t_, w_: _dot(jnp.concatenate([x, t_], axis=0), w_), q, tinv, qs)
            tinv = each(lambda t_, x, y: t_ + x + y[c:], tinv, q, both)
            q = each(lambda y: y[:c], both)
    h = INV_BASE
    while h < c:
        link = same_blk(2 * h) & jnp.logical_not(same_blk(h))
        n_off = each(lambda x: jnp.where(link, x, 0.0), n_ab)
        y = each(lambda t_, x: x + _dot(t_, stk(x)), tinv, n_off)
        tinv = each(lambda t_, x: t_ + x + _dot(x, stk(t_)), tinv, y)
        h *= 2
    a_bar = each(lambda t_, x: x + _dot(t_, stk(x)), tinv, a_t)
    lv = each(lambda l_, p, y: _dot(jnp.concatenate([l_, p], axis=0), y), l_ak, p_rk, v_s)
    w = [x[:c] for x in lv]
    u0 = each(lambda t_, x: x + _dot(t_, stk(x)), tinv, w)
    r_bar = each(lambda x, p, y: x + _dot(p, stk(y)), r_t, p_rb, a_bar)
    y0 = each(lambda p, x, z: _dot(p, stk(x)) + z[c:], p_rb, u0, lv)
    zero = jnp.zeros((c, LANES), F32)
    ms = each(lambda ab, u, v_, bh, kh_: _dot_tn(
        jnp.concatenate([jnp.concatenate([ab, u], axis=1), jnp.concatenate([zero, v_], axis=1)], axis=0),
        jnp.concatenate([bh, kh_], axis=0)), a_bar, u0, vv, b_h, k_h)
    m = [jnp.where(same_head, x[:LANES], 0.0) for x in ms]
    s0p = [jnp.where(same_head, x[LANES:], 0.0) for x in ms]
    return [(r_bar[i], y0[i], m[i], s0p[i], jnp.exp(tot[i]), bonus[i]) for i in range(n)]


def _rwkv_scan_kernel(*refs, has_s0, emit_state):
    (r_ref, k_ref, v_ref, lwf_ref, lwb_ref, af_ref, ab_ref,
     kkf_ref, kkb_ref, kaf_ref, kab_ref, rkf_ref, rkb_ref, lnw_ref, lnb_ref) = refs[:15]
    s0_ref = refs[15] if has_s0 else None
    o_ref = refs[15 + has_s0]
    s_ref = refs[16 + has_s0] if emit_state else None
    rbar_ref, y0_ref, m_ref, s0p_ref, dec_ref, bonus_ref, yf_ref, yb_ref = refs[16 + has_s0 + emit_state:]
    bb, t = r_ref.shape[0], r_ref.shape[1]
    nc = t // CHUNK
    dirs = ((lwf_ref, af_ref, kkf_ref, kaf_ref, rkf_ref, False), (lwb_ref, ab_ref, kkb_ref, kab_ref, rkb_ref, True))
    group = min(bb * nc, PREP_GROUP)
    assert (bb * nc) % group == 0
    chunk_rows = lambda ci: pl.ds(pl.multiple_of(ci * CHUNK, CHUNK), CHUNK)

    def prep(gi, carry):
        insts, where = [], []
        at = [((gi * group + j) // nc, chunk_rows((gi * group + j) % nc)) for j in range(group)]
        for j, (e, sl) in enumerate(at):
            r, k, v = (ref[e, sl, :].astype(F32) for ref in (r_ref, k_ref, v_ref))
            for d, (lw_ref, a_ref, kk_ref, ka_ref, rk_ref, reverse) in enumerate(dirs):
                insts.append((r, k, v, lw_ref[e, sl, :], a_ref[e, sl, :].astype(F32), kk_ref[...], ka_ref[...],
                              rk_ref[...], reverse))
                where.append((d, gi * group + j, e, sl))
        terms = _rwkv_chunk_terms(insts)
        for (d, fi, e, sl), (r_bar, y0, m, s0p, dec, bon) in zip(where, terms):
            rbar_ref[d, fi] = r_bar.astype(BF16)
            y0_ref[d, fi] = y0
            m_ref[d, fi] = m.astype(BF16)
            s0p_ref[d, fi] = s0p
            dec_ref[d, fi] = jnp.broadcast_to(dec, (8, LANES))
        for j in range(group):
            _, _, e, sl = where[2 * j]
            bonus_ref[e, sl, :] = terms[2 * j][5] + terms[2 * j + 1][5]
        return carry

    lax.fori_loop(0, bb * nc // group, prep, 0)

    seq_steps = SEQ_STEPS if nc % SEQ_STEPS == 0 else 1
    idx = [(d, e) for d in range(2) for e in range(bb)]

    def seq(i, carry):
        for s in range(seq_steps):
            cis = (i * seq_steps + s, nc - 1 - i * seq_steps - s)
            sb16 = [st.astype(BF16) for st in carry]
            sm = [jnp.dot(sb16[n], m_ref[d, e * nc + cis[d]], preferred_element_type=F32)
                  for n, (d, e) in enumerate(idx)]
            ys = [_dot_nt(rbar_ref[d, e * nc + cis[d]], sb16[n]) + y0_ref[d, e * nc + cis[d]]
                  for n, (d, e) in enumerate(idx)]
            for n, (d, e) in enumerate(idx):
                (yf_ref, yb_ref)[d][e, chunk_rows(cis[d]), :] = ys[n]
            carry = tuple(carry[n] * dec_ref[d, e * nc + cis[d]][0:1] + sm[n] + s0p_ref[d, e * nc + cis[d]]
                          for n, (d, e) in enumerate(idx))
        return carry

    if has_s0:
        init = tuple(s0_ref[e, d, 0] for d in range(2) for e in range(bb))
    else:
        init = tuple(jnp.zeros((LANES, LANES), F32) for _ in range(2 * bb))
    final = lax.fori_loop(0, nc // seq_steps, seq, init)
    if emit_state:
        for d in range(2):
            for e in range(bb):
                s = final[d * bb + e]
                s_ref[e, d, 0] = s[:RWKV_HEAD, :RWKV_HEAD]
                s_ref[e, d, 1] = pltpu.roll(s, RWKV_HEAD, 1)[RWKV_HEAD:, :RWKV_HEAD]

    prow = lax.broadcasted_iota(jnp.int32, (LANES, LANES), 0)
    pcol = lax.broadcasted_iota(jnp.int32, (LANES, LANES), 1)
    head_ones = ((prow >= RWKV_HEAD) == (pcol >= RWKV_HEAD)).astype(BF16)
    inv_n = 1.0 / RWKV_HEAD

    def head_mean(x):
        hi = x.astype(BF16)
        lo = (x - hi.astype(F32)).astype(BF16)
        both = jnp.dot(jnp.concatenate([hi, lo], axis=0), head_ones, preferred_element_type=F32)
        return (both[:x.shape[0]] + both[x.shape[0]:]) * inv_n

    def finish(gi, carry):
        at = [((gi * group + j) // nc, chunk_rows((gi * group + j) % nc)) for j in range(group)]
        y = [yf_ref[e, sl, :] + yb_ref[e, sl, :] for e, sl in at]
        yc = [x - head_mean(x) for x in y]
        var = [head_mean(x * x) for x in yc]
        for (e, sl), x, vr in zip(at, yc, var):
            o_ref[e, sl, :] = (x * lax.rsqrt(vr + LNX_EPS) * lnw_ref[...] + lnb_ref[...]
                               + bonus_ref[e, sl, :]).astype(BF16)
        return carry

    lax.fori_loop(0, bb * nc // group, finish, 0)


def _rwkv_scan(r, k, v, lw, a, k_k, k_a, r_k, lnx_w, lnx_b, s0, bb, emit_state):
    bsz, t, d = r.shape
    nc = t // CHUNK
    seq = lambda b, p: (b, 0, p)
    seq_b = lambda b, p: (b, 0, RWKV_PAIRS + p)
    par = lambda b, p: (0, p)
    par_b = lambda b, p: (0, RWKV_PAIRS + p)
    sblk = pl.BlockSpec((bb, t, LANES), seq)
    sblk_b = pl.BlockSpec((bb, t, LANES), seq_b)
    pblk = pl.BlockSpec((1, LANES), par)
    pblk_b = pl.BlockSpec((1, LANES), par_b)
    s0blk = pl.BlockSpec((bb, 2, 1, LANES, LANES), lambda b, p: (b, 0, p, 0, 0))
    s_blk = pl.BlockSpec((bb, 2, 2, RWKV_HEAD, RWKV_HEAD), lambda b, p: (b, 0, p, 0, 0))
    s_shape = jax.ShapeDtypeStruct((bsz, 2, RWKV_HEADS, RWKV_HEAD, RWKV_HEAD), F32)
    has_s0 = s0 is not None
    k_k, k_a, r_k = (p.reshape(1, 2 * d) for p in (k_k, k_a, r_k))
    return pl.pallas_call(
        functools.partial(_rwkv_scan_kernel, has_s0=has_s0, emit_state=emit_state),
        grid=(bsz // bb, RWKV_PAIRS),
        in_specs=[sblk, sblk, sblk, sblk, sblk_b, sblk, sblk_b,
                  pblk, pblk_b, pblk, pblk_b, pblk, pblk_b, pblk, pblk] + [s0blk] * has_s0,
        out_specs=[sblk] + [s_blk] * emit_state,
        out_shape=[jax.ShapeDtypeStruct((bsz, t, d), BF16)] + [s_shape] * emit_state,
        scratch_shapes=[pltpu.VMEM((2, bb * nc, CHUNK, LANES), BF16), pltpu.VMEM((2, bb * nc, CHUNK, LANES), F32),
                        pltpu.VMEM((2, bb * nc, LANES, LANES), BF16), pltpu.VMEM((2, bb * nc, LANES, LANES), F32),
                        pltpu.VMEM((2, bb * nc, 8, LANES), F32), pltpu.VMEM((bb, t, LANES), F32),
                        pltpu.VMEM((bb, t, LANES), F32), pltpu.VMEM((bb, t, LANES), F32)],
        compiler_params=_params(2),
        name="rwkv_scan",
    )(r, k, v, lw, lw, a, a, k_k, k_k, k_a, k_a, r_k, r_k,
      lnx_w.reshape(1, d), lnx_b.reshape(1, d), *([s0] * has_s0))


def _rwkv_back_kernel(o_ref, g_ref, x_ref, mod_ref, wout_ref, nf_ref, y_ref):
    out = _dot(o_ref[...].astype(F32) * _silu(g_ref[...].astype(F32)), wout_ref[...])
    x2 = x_ref[...] + mod_ref[0][:, 2 * D_MODEL:] * out
    y_ref[...] = _rms(x2) * nf_ref[...]


def _rwkv_back(o, g, x, mod1, mod_idx, tm, w_out, norm_f):
    n, d = x.shape
    nmod = mod1.shape[0]
    row = lambda i: (i, 0)
    const = lambda i: (0, 0)
    return pl.pallas_call(
        _rwkv_back_kernel,
        grid=(n // tm,),
        in_specs=[pl.BlockSpec((tm, d), row), pl.BlockSpec((tm, d), row), pl.BlockSpec((tm, d), row),
                  pl.BlockSpec((1, 1, 3 * d), lambda i: (mod_idx(i), 0, 0)),
                  pl.BlockSpec((d, d), const), pl.BlockSpec((1, d), const)],
        out_specs=pl.BlockSpec((tm, d), row),
        out_shape=jax.ShapeDtypeStruct((n, d), F32),
        compiler_params=_params(1),
        name="rwkv_back",
    )(o, g, x, mod1.reshape(nmod, 1, 3 * d), w_out.astype(BF16), norm_f.reshape(1, d))


def _pair_blockdiag(s):
    b = s.shape[0]
    s = s.reshape(b, 2, RWKV_PAIRS, 2, RWKV_HEAD, RWKV_HEAD)
    z = jnp.zeros_like(s[:, :, :, 0])
    top = jnp.concatenate([s[:, :, :, 0], z], axis=-1)
    bot = jnp.concatenate([z, s[:, :, :, 1]], axis=-1)
    return jnp.concatenate([top, bot], axis=-2)


def _tile_rows(n_seq, t, may_span, rows):
    if t % rows == 0:
        return rows
    if may_span and rows % t == 0 and (n_seq * t) % rows == 0:
        return rows
    return t


def kernel(x_prompt, x_sample, state_gla, state_rwkv, c, c_ctx, w_mod, b_mod, norm_w, gla_w_in, gla_w_a1,
           gla_w_a2, gla_b_a, gla_norm, gla_w_out, rwkv_mu, rwkv_w_rkvg, rwkv_w0, rwkv_w1, rwkv_w2, rwkv_a0,
           rwkv_a1, rwkv_a2, rwkv_k_k, rwkv_k_a, rwkv_r_k, rwkv_lnx_w, rwkv_lnx_b, rwkv_w_out, norm_f):
    d = D_MODEL
    bp, tp, _ = x_prompt.shape
    bs, ts, _ = x_sample.shape
    assert tp % CHUNK == 0 and ts % CHUNK == 0 and ts % GRID_W == 0

    nrows = -(-(1 + bs) // 8) * 8
    cond = jnp.zeros((nrows, d), F32).at[0].set(c_ctx).at[1:1 + bs].set(c)
    cond = cond * (1.0 / (1.0 + jnp.exp(-cond)))
    mod0 = _matmul_bias(cond, w_mod[0], b_mod[0].reshape(1, 3 * d))
    mod1 = _matmul_bias(cond, w_mod[1], b_mod[1].reshape(1, 3 * d))

    def trunk(x3, shared_mod, grid_w, gla_s0, rwkv_s0, rwkv_bb, emit_state):
        b, t, _ = x3.shape
        n = b * t
        tm, tw = (_tile_rows(b, t, shared_mod, rows) for rows in (TOKEN_TILE, WIDE_TILE))
        mod_row = lambda rows: (lambda i: 0) if shared_mod else (lambda i: 1 + i // (t // rows))
        x = x3.reshape(n, d)
        seq = lambda arr: arr.reshape(b, t, arr.shape[-1])
        q, k, v, g, lg = _gla_front(x, mod0, mod_row(tw), tw, norm_w[0], gla_w_in[0], gla_w_a1[0], gla_w_a2[0], gla_b_a[0])
        o, *gla_s = _gla_scan(seq(q), seq(k), seq(v), seq(lg), gla_s0, GLA_HEADS, emit_state)
        x1, h1 = _gla_back(o.reshape(n, d), g, x, mod0, mod1, mod_row(tw), tw, gla_norm[0], gla_w_out[0], norm_w[1])
        r, k, v, g, lw, a = _rwkv_front(h1, tm, t, grid_w, rwkv_mu[0], rwkv_w_rkvg[0], rwkv_w0[0], rwkv_w1[0], rwkv_w2[0],
                                        rwkv_a0[0], rwkv_a1[0], rwkv_a2[0])
        o, *rwkv_s = _rwkv_scan(seq(r), seq(k), seq(v), seq(lw), seq(a), rwkv_k_k[0], rwkv_k_a[0], rwkv_r_k[0],
                                rwkv_lnx_w[0], rwkv_lnx_b[0], rwkv_s0, rwkv_bb, emit_state)
        y = _rwkv_back(o.reshape(n, d), g, x1, mod1, mod_row(tw), tw, rwkv_w_out[0], norm_f)
        return y.reshape(b, t, d), gla_s, rwkv_s

    y_prompt, (gla_s,), (rwkv_s,) = trunk(x_prompt, True, None, None, None, min(bp, 4), True)

    y_sample, _, _ = trunk(x_sample, False, GRID_W, state_gla[:, 0], _pair_blockdiag(state_rwkv[:, 0]),
                           min(bs, 4), False)
    return (y_prompt, y_sample, gla_s[:, None], rwkv_s[:, None])
```

```python
import functools

import jax
import jax.numpy as jnp
from jax import lax
from jax.experimental import pallas as pl
from jax.experimental.pallas import tpu as pltpu

F32 = jnp.float32
BF16 = jnp.bfloat16

D_MODEL = 1024
EPS = 1e-6
GRID_W = 64
GLA_HEADS = 4
GLA_DK = 512
GLA_DV = 1024
GLA_DKH = 128
GLA_DVH = 256
GLA_GATE_RANK = 16
GLA_GATE_NORM = 16.0
RWKV_HEAD = 64
RWKV_HEADS = 16
RWKV_PAIRS = 8
RWKV_RANK = 64
LNX_EPS = 64e-5
RWKV_DECAY_SCALE = 0.6065306597126334
CHUNK = 64
SEQ_STEPS = 4
GLA_STEPS = 2
INV_BASE = 8
TOKEN_TILE = 512
WIDE_TILE = 1024
PREP_GROUP = 8
LANES = 128
VMEM_LIMIT = 56 * 1024 * 1024


def _dot(a, b):
    return jnp.dot(a.astype(BF16), b.astype(BF16), preferred_element_type=F32)


def _dot_nt(a, b):
    return lax.dot_general(a.astype(BF16), b.astype(BF16), (((1,), (1,)), ((), ())),
                           preferred_element_type=F32)


def _dot_tn(a, b):
    return lax.dot_general(a.astype(BF16), b.astype(BF16), (((0,), (0,)), ((), ())),
                           preferred_element_type=F32)


def _tri_dot(tri, x):
    hi = x.astype(BF16)
    lo = (x - hi.astype(F32)).astype(BF16)
    t = tri.astype(BF16)
    return jnp.dot(t, hi, preferred_element_type=F32) + jnp.dot(t, lo, preferred_element_type=F32)


def _log_sigmoid(z):
    return jnp.minimum(z, 0.0) - jnp.log(1.0 + jnp.exp(-jnp.abs(z)))


def _sigmoid(z):
    return 1.0 / (1.0 + jnp.exp(-z))


def _silu(z):
    return z * _sigmoid(z)


def _rms(x):
    return x * lax.rsqrt(jnp.mean(x * x, axis=-1, keepdims=True) + EPS)


def _params(n_grid_dims):
    return pltpu.CompilerParams(dimension_semantics=("arbitrary",) * n_grid_dims,
                                vmem_limit_bytes=VMEM_LIMIT)


def _mm_kernel(x_ref, w_ref, b_ref, o_ref):
    o_ref[...] = _dot(x_ref[...], w_ref[...]) + b_ref[...]


def _matmul_bias(x, w, b):
    m, k = x.shape
    n = w.shape[1]
    return pl.pallas_call(
        _mm_kernel,
        grid=(1,),
        in_specs=[pl.BlockSpec((m, k), lambda i: (0, 0)),
                  pl.BlockSpec((k, n), lambda i: (0, 0)),
                  pl.BlockSpec((1, n), lambda i: (0, 0))],
        out_specs=pl.BlockSpec((m, n), lambda i: (0, 0)),
        out_shape=jax.ShapeDtypeStruct((m, n), F32),
        compiler_params=_params(1),
        name="mod_matmul",
    )(x, w.astype(BF16), b)


def _gla_front_kernel(x_ref, mod_ref, nw_ref, win_ref, wa1_ref, wa2_ref, ba_ref,
                      q_ref, k_ref, v_ref, g_ref, lg_ref):
    d = D_MODEL
    mod = mod_ref[0]
    shift, scale = mod[:, :d], mod[:, d:2 * d]
    h = _rms(x_ref[...]) * nw_ref[...] * (1.0 + scale) + shift
    hb = h.astype(BF16)
    proj = lambda lo, hi: jnp.dot(hb, win_ref[:, lo:hi], preferred_element_type=F32)
    q_ref[...] = (proj(0, GLA_DK) * (GLA_DKH ** -0.5)).astype(BF16)
    k_ref[...] = proj(GLA_DK, 2 * GLA_DK).astype(BF16)
    v_ref[...] = proj(2 * GLA_DK, 2 * GLA_DK + GLA_DV).astype(BF16)
    g_ref[...] = proj(2 * GLA_DK + GLA_DV, 2 * GLA_DK + 2 * GLA_DV).astype(BF16)
    t = jnp.dot(hb, wa1_ref[...], preferred_element_type=F32)
    z = _dot(t, wa2_ref[...]) + ba_ref[...]
    lg_ref[...] = _log_sigmoid(z) * (1.0 / GLA_GATE_NORM)


def _gla_front(x, mod, mod_idx, tm, norm_w, w_in, w_a1, w_a2, b_a):
    n, d = x.shape
    nmod = mod.shape[0]
    wa1 = jnp.zeros((d, LANES), F32).at[:, :GLA_GATE_RANK].set(w_a1[0]).at[:, GLA_GATE_RANK:2 * GLA_GATE_RANK].set(w_a1[1])
    wa2 = jnp.zeros((LANES, 2 * GLA_DK), F32).at[:GLA_GATE_RANK, :GLA_DK].set(w_a2[0])
    wa2 = wa2.at[GLA_GATE_RANK:2 * GLA_GATE_RANK, GLA_DK:].set(w_a2[1])
    ba = b_a.reshape(1, 2 * GLA_DK)
    row = lambda i: (i, 0)
    const = lambda i: (0, 0)
    n_in = w_in.shape[1]
    outs = pl.pallas_call(
        _gla_front_kernel,
        grid=(n // tm,),
        in_specs=[pl.BlockSpec((tm, d), row),
                  pl.BlockSpec((1, 1, 3 * d), lambda i: (mod_idx(i), 0, 0)),
                  pl.BlockSpec((1, d), const),
                  pl.BlockSpec((d, n_in), const),
                  pl.BlockSpec((d, LANES), const),
                  pl.BlockSpec((LANES, 2 * GLA_DK), const),
                  pl.BlockSpec((1, 2 * GLA_DK), const)],
        out_specs=[pl.BlockSpec((tm, GLA_DK), row), pl.BlockSpec((tm, GLA_DK), row),
                   pl.BlockSpec((tm, GLA_DV), row), pl.BlockSpec((tm, GLA_DV), row),
                   pl.BlockSpec((tm, 2 * GLA_DK), row)],
        out_shape=[jax.ShapeDtypeStruct((n, GLA_DK), BF16), jax.ShapeDtypeStruct((n, GLA_DK), BF16),
                   jax.ShapeDtypeStruct((n, GLA_DV), BF16), jax.ShapeDtypeStruct((n, GLA_DV), BF16),
                   jax.ShapeDtypeStruct((n, 2 * GLA_DK), F32)],
        compiler_params=_params(1),
        name="gla_front",
    )(x, mod.reshape(nmod, 1, 3 * d), norm_w.reshape(1, d), w_in.astype(BF16), wa1.astype(BF16),
      wa2.astype(BF16), ba)
    return outs


def _gla_chunks(chains):
    flat = [(x, ch[2]) for ch in chains for x in ch[0]]
    c = flat[0][0][0].shape[0]
    dk = flat[0][0][1].shape[1]
    row = lax.broadcasted_iota(jnp.int32, (c, c), 0)
    col = lax.broadcasted_iota(jnp.int32, (c, c), 1)
    keep = [(col >= row) if rv else (col <= row) for _, rv in flat]
    b = [_tri_dot(kp.astype(F32), x[3]) for kp, (x, _) in zip(keep, flat)]
    b_end = [bb[0:1] if rv else bb[c - 1:c] for bb, (_, rv) in zip(b, flat)]
    qb = [(x[0] * jnp.exp(bb)).astype(BF16) for (x, _), bb in zip(flat, b)]
    kb = [x[1] * jnp.exp(-bb) for (x, _), bb in zip(flat, b)]
    kd = [x[1] * jnp.exp(be - bb) for (x, _), bb, be in zip(flat, b, b_end)]
    scores = [jnp.where(kp, _dot_nt(x, y), 0.0).astype(BF16) for kp, x, y in zip(keep, qb, kb)]
    upd = [_dot_tn(x, y[2]) for (y, _), x in zip(flat, kd)]
    decay = [jnp.broadcast_to(jnp.exp(be), (dk, dk)).T for be in b_end]
    n_steps = len(chains[0][0])
    st = [ch[1] for ch in chains]
    outs = [[] for _ in chains]
    for s in range(n_steps):
        at = [ci * n_steps + s for ci in range(len(chains))]
        for ci, f in enumerate(at):
            outs[ci].append(_dot(jnp.concatenate([scores[f], qb[f]], axis=1),
                                 jnp.concatenate([flat[f][0][2], st[ci].astype(BF16)], axis=0)))
        st = [st[ci] * jnp.concatenate([decay[f]] * (st[ci].shape[1] // dk), axis=1) + upd[f]
              for ci, f in enumerate(at)]
    return list(zip(outs, st))


def _gla_scan_kernel(*refs, has_s0, emit_state):
    q_ref, k_ref, v_ref, lf_ref, lb_ref = refs[:5]
    s0_ref = refs[5] if has_s0 else None
    o_ref = refs[5 + has_s0]
    s_ref = refs[6 + has_s0] if emit_state else None
    acc_ref = refs[-1]
    t = q_ref.shape[1]
    hb = q_ref.shape[2] // GLA_DKH
    nc = t // CHUNK
    half = nc // 2
    kh = lambda h: slice(h * GLA_DKH, (h + 1) * GLA_DKH)
    vh = lambda h: slice(h * GLA_DVH, (h + 1) * GLA_DVH)

    steps = GLA_STEPS if half % GLA_STEPS == 0 else 1

    def both(accumulate):
        def body(i, carry):
            sls = [[pl.ds(pl.multiple_of(ci * CHUNK, CHUNK), CHUNK)
                    for ci in ((i * steps + s, nc - 1 - i * steps - s)[d] for s in range(steps))] for d in range(2)]
            chains = [([(q_ref[0, sl, kh(h)].astype(F32), k_ref[0, sl, kh(h)].astype(F32), v_ref[0, sl, vh(h)],
                         (lf_ref, lb_ref)[d][0, sl, kh(h)]) for sl in sls[d]], carry[d * hb + h], d == 1)
                      for d in range(2) for h in range(hb)]
            res = _gla_chunks(chains)
            for d in range(2):
                for h in range(hb):
                    for sl, o in zip(sls[d], res[d * hb + h][0]):
                        if accumulate:
                            o_ref[0, sl, vh(h)] = (acc_ref[sl, vh(h)] + o).astype(BF16)
                        else:
                            acc_ref[sl, vh(h)] = o
            return tuple(st for _, st in res)
        return body

    if has_s0:
        init = tuple(s0_ref[0, d, h] for d in range(2) for h in range(hb))
    else:
        init = tuple(jnp.zeros((GLA_DKH, GLA_DVH), F32) for _ in range(2 * hb))
    carry = lax.fori_loop(0, half // steps, both(False), init)
    carry = lax.fori_loop(half // steps, nc // steps, both(True), carry)
    if emit_state:
        for d in range(2):
            for h in range(hb):
                s_ref[0, d, h] = carry[d * hb + h]


def _gla_scan(q, k, v, lg, s0, hb, emit_state):
    bsz, t, _ = q.shape
    ng = GLA_HEADS // hb
    hd = lambda b, h: (b, 0, h)
    kblk = pl.BlockSpec((1, t, hb * GLA_DKH), hd)
    vblk = pl.BlockSpec((1, t, hb * GLA_DVH), hd)
    sblk = pl.BlockSpec((1, 2, hb, GLA_DKH, GLA_DVH), lambda b, h: (b, 0, h, 0, 0))
    s_shape = jax.ShapeDtypeStruct((bsz, 2, GLA_HEADS, GLA_DKH, GLA_DVH), F32)
    has_s0 = s0 is not None
    return pl.pallas_call(
        functools.partial(_gla_scan_kernel, has_s0=has_s0, emit_state=emit_state),
        grid=(bsz, ng),
        in_specs=[kblk, kblk, vblk, kblk, pl.BlockSpec((1, t, hb * GLA_DKH), lambda b, h: (b, 0, ng + h))]
        + [sblk] * has_s0,
        out_specs=[vblk] + [sblk] * emit_state,
        out_shape=[jax.ShapeDtypeStruct((bsz, t, GLA_DV), BF16)] + [s_shape] * emit_state,
        scratch_shapes=[pltpu.VMEM((t, hb * GLA_DVH), F32)],
        compiler_params=_params(2),
        name="gla_scan",
    )(q, k, v, lg, lg, *([s0] * has_s0))


def _gla_back_kernel(o_ref, g_ref, x_ref, mod0_ref, mod1_ref, gn_ref, wout_ref, nw1_ref, x1_ref, h1_ref):
    d = D_MODEL
    o = o_ref[...].astype(F32)
    gn = gn_ref[...]
    parts = [_rms(o[:, h * GLA_DVH:(h + 1) * GLA_DVH]) * gn for h in range(GLA_HEADS)]
    on = jnp.concatenate(parts, axis=-1) * _silu(g_ref[...].astype(F32))
    out = _dot(on, wout_ref[...])
    gate = mod0_ref[0][:, 2 * d:]
    x1 = x_ref[...] + gate * out
    x1_ref[...] = x1
    mod1 = mod1_ref[0]
    h1_ref[...] = (_rms(x1) * nw1_ref[...] * (1.0 + mod1[:, d:2 * d]) + mod1[:, :d]).astype(BF16)


def _gla_back(o, g, x, mod0, mod1, mod_idx, tm, gla_norm, w_out, norm_w1):
    n, d = x.shape
    nmod = mod0.shape[0]
    row = lambda i: (i, 0)
    const = lambda i: (0, 0)
    modspec = pl.BlockSpec((1, 1, 3 * d), lambda i: (mod_idx(i), 0, 0))
    return pl.pallas_call(
        _gla_back_kernel,
        grid=(n // tm,),
        in_specs=[pl.BlockSpec((tm, d), row), pl.BlockSpec((tm, d), row), pl.BlockSpec((tm, d), row),
                  modspec, modspec,
                  pl.BlockSpec((1, GLA_DVH), const), pl.BlockSpec((d, d), const), pl.BlockSpec((1, d), const)],
        out_specs=[pl.BlockSpec((tm, d), row), pl.BlockSpec((tm, d), row)],
        out_shape=[jax.ShapeDtypeStruct((n, d), F32), jax.ShapeDtypeStruct((n, d), BF16)],
        compiler_params=_params(1),
        name="gla_back",
    )(o, g, x, mod0.reshape(nmod, 1, 3 * d), mod1.reshape(nmod, 1, 3 * d), gla_norm.reshape(1, GLA_DVH),
      w_out.astype(BF16), norm_w1.reshape(1, d))


def _token_shift(h, up_ref, dn_ref, seq_len, grid_w):
    tm = h.shape[0]
    pos = lax.broadcasted_iota(jnp.int32, (tm, 1), 0) + pl.program_id(0) * tm
    prev = pltpu.roll(h, 1, 0)
    nxt = pltpu.roll(h, tm - 1, 0)
    t = pos % seq_len
    if grid_w is None:
        return 0.5 * (jnp.where(t == 0, 0.0, prev) + jnp.where(t == seq_len - 1, 0.0, nxt))
    col = pos % grid_w
    up = jnp.concatenate([up_ref[...].astype(F32), h[:tm - grid_w]], axis=0)
    dn = jnp.concatenate([h[grid_w:], dn_ref[...].astype(F32)], axis=0)
    up = jnp.where(t < grid_w, 0.0, up)
    dn = jnp.where(t >= seq_len - grid_w, 0.0, dn)
    left = jnp.where(col == 0, 0.0, prev)
    right = jnp.where(col == grid_w - 1, 0.0, nxt)
    return 0.25 * (up + dn + left + right)


def _rwkv_front_kernel(*refs, seq_len, grid_w):
    n_h = 1 if grid_w is None else 3
    h_ref, up_ref, dn_ref = (tuple(refs[:n_h]) + (None, None))[:3]
    (mu_ref, wrkvg_ref, w1_ref, w2_ref, w0_ref, a1_ref, a2_ref, a0_ref,
     r_ref, k_ref, v_ref, g_ref, lw_ref, a_ref) = refs[n_h:]
    h = h_ref[...].astype(F32)
    dh = _token_shift(h, up_ref, dn_ref, seq_len, grid_w) - h
    mix = lambda p: (h + dh * mu_ref[p:p + 1, :]).astype(BF16)

    def put_pairs(ref, val):
        for p in range(ref.shape[0]):
            ref[p] = val[:, p * LANES:(p + 1) * LANES].astype(ref.dtype)

    put_pairs(r_ref, jnp.dot(mix(0), wrkvg_ref[0], preferred_element_type=F32))
    put_pairs(k_ref, jnp.dot(mix(1), wrkvg_ref[1], preferred_element_type=F32))
    put_pairs(v_ref, jnp.dot(mix(2), wrkvg_ref[2], preferred_element_type=F32))
    g_ref[...] = jnp.dot(mix(3), wrkvg_ref[3], preferred_element_type=F32).astype(BF16)
    tw = jnp.tanh(jnp.dot(mix(4), w1_ref[...], preferred_element_type=F32))
    put_pairs(lw_ref, -RWKV_DECAY_SCALE * _sigmoid(w0_ref[...] + _dot(tw, w2_ref[...])))
    ta = jnp.dot(mix(5), a1_ref[...], preferred_element_type=F32)
    put_pairs(a_ref, _sigmoid(a0_ref[...] + _dot(ta, a2_ref[...])))


def _rwkv_front(h, tm, seq_len, grid_w, mu, w_rkvg, w0, w1, w2, a0, a1, a2):
    n, d = h.shape
    rk = RWKV_RANK
    cat1 = lambda w: jnp.concatenate([w[0], w[1]], axis=1)
    bd2 = lambda w: jnp.zeros((2 * rk, 2 * d), F32).at[:rk, :d].set(w[0]).at[rk:, d:].set(w[1])
    row = lambda i: (i, 0)
    const = lambda i: (0, 0)
    np_ = RWKV_PAIRS
    pair_blk = lambda pairs: pl.BlockSpec((pairs, tm, LANES), lambda i: (0, i, 0))
    if grid_w is None:
        assert tm % seq_len == 0
        h_specs, h_args = [pl.BlockSpec((tm, d), row)], [h]
    else:
        assert tm % grid_w == 0 and seq_len % tm == 0
        rows_per_tile, last = tm // grid_w, n // grid_w - 1
        h_specs = [pl.BlockSpec((tm, d), row),
                   pl.BlockSpec((grid_w, d), lambda i: (jnp.maximum(i * rows_per_tile - 1, 0), 0)),
                   pl.BlockSpec((grid_w, d), lambda i: (jnp.minimum((i + 1) * rows_per_tile, last), 0))]
        h_args = [h, h, h]
    return pl.pallas_call(
        functools.partial(_rwkv_front_kernel, seq_len=seq_len, grid_w=grid_w),
        grid=(n // tm,),
        in_specs=h_specs + [
            pl.BlockSpec((8, d), const),
            pl.BlockSpec((4, d, d), lambda i: (0, 0, 0)),
            pl.BlockSpec((d, 2 * rk), const), pl.BlockSpec((2 * rk, 2 * d), const), pl.BlockSpec((1, 2 * d), const),
            pl.BlockSpec((d, 2 * rk), const), pl.BlockSpec((2 * rk, 2 * d), const), pl.BlockSpec((1, 2 * d), const)],
        out_specs=[pair_blk(np_)] * 3 + [pl.BlockSpec((tm, d), row), pair_blk(2 * np_), pair_blk(2 * np_)],
        out_shape=[jax.ShapeDtypeStruct((np_, n, LANES), BF16)] * 3 + [jax.ShapeDtypeStruct((n, d), BF16)]
        + [jax.ShapeDtypeStruct((2 * np_, n, LANES), F32), jax.ShapeDtypeStruct((2 * np_, n, LANES), BF16)],
        compiler_params=_params(1),
        name="rwkv_front",
    )(*h_args, jnp.zeros((8, d), F32).at[:6].set(mu), w_rkvg.astype(BF16),
      cat1(w1).astype(BF16), bd2(w2).astype(BF16), w0.reshape(1, 2 * d),
      cat1(a1).astype(BF16), bd2(a2).astype(BF16), a0.reshape(1, 2 * d))


def _seg_sum(x, head0):
    s0 = jnp.sum(jnp.where(head0, x, 0.0), axis=-1, keepdims=True)
    s1 = jnp.sum(jnp.where(head0, 0.0, x), axis=-1, keepdims=True)
    return jnp.where(head0, s0, s1)


def _stack(x, head0):
    return jnp.concatenate([jnp.where(head0, x, 0.0), jnp.where(head0, 0.0, x)], axis=0)


def _rwkv_chunk_terms(insts):
    n = len(insts)
    c = insts[0][0].shape[0]
    lane = lax.broadcasted_iota(jnp.int32, (c, LANES), 1)
    head0 = lane < RWKV_HEAD
    row = lax.broadcasted_iota(jnp.int32, (c, c), 0)
    col = lax.broadcasted_iota(jnp.int32, (c, c), 1)
    ti = lax.broadcasted_iota(jnp.int32, (c, LANES), 0)
    si = lane & (c - 1)
    prow = lax.broadcasted_iota(jnp.int32, (LANES, LANES), 0)
    pcol = lax.broadcasted_iota(jnp.int32, (LANES, LANES), 1)
    same_head = (prow >= RWKV_HEAD) == (pcol >= RWKV_HEAD)
    each = lambda f, *ls: [f(*xs) for xs in zip(*ls)]
    rev = [x[8] for x in insts]
    stk = lambda x: _stack(x, head0).astype(BF16)

    cum = [_tri_dot(((col >= row) if x[8] else (col <= row)).astype(F32), x[3]) for x in insts]
    tot = [cm[0:1] if rv else cm[c - 1:c] for cm, rv in zip(cum, rev)]

    a_t, r_t, b_ts, k_ts, b_h, k_h, v_s, vv, bonus = [], [], [], [], [], [], [], [], []
    for (r, k, v, lw, a, k_k, k_a, r_k, _), cm, tt in zip(insts, cum, tot):
        kk = k * k_k
        kk = kk * lax.rsqrt(jnp.maximum(_seg_sum(kk * kk, head0), 1e-24))
        kd = k * (1.0 + (a - 1.0) * k_a)
        bv = kk * a
        bonus.append(_seg_sum(r * kd * r_k, head0) * v)
        e_neg = jnp.exp(-cm)
        e_tot = jnp.exp(tt)
        b_neg, k_neg = bv * e_neg, kd * e_neg
        a_t.append(-kk * jnp.exp(cm - lw))
        r_t.append(r * jnp.exp(cm))
        b_ts.append(stk(b_neg))
        k_ts.append(stk(k_neg))
        b_h.append(b_neg * e_tot)
        k_h.append(k_neg * e_tot)
        v_s.append(stk(v))
        vv.append(v)

    strict = [(si > ti) if rv else (si < ti) for rv in rev]
    incl = [(si >= ti) if rv else (si <= ti) for rv in rev]
    ar = each(lambda x, y: jnp.concatenate([x, y], axis=0).astype(BF16), a_t, r_t)
    g_b = each(_dot_nt, ar, b_ts)
    g_k = each(_dot_nt, ar, k_ts)
    n_ab = each(lambda m_, x: jnp.where(m_, x[:c], 0.0), strict, g_b)
    l_ak = each(lambda m_, x: jnp.where(m_, x[:c], 0.0), strict, g_k)
    p_rb = each(lambda m_, x: jnp.where(m_, x[c:], 0.0), incl, g_b)
    p_rk = each(lambda m_, x: jnp.where(m_, x[c:], 0.0), incl, g_k)

    same_blk = lambda h: (ti // h) == (si // h)
    n_d = each(lambda x: jnp.where(same_blk(INV_BASE), x, 0.0), n_ab)
    steps = max(1, (INV_BASE - 1).bit_length())
    q = n_d
    tinv = n_d
    q = each(lambda x: _dot(x, stk(x)), q) if steps > 1 else q
    for j in range(1, steps):
        last = j == steps - 1
        qs = each(stk, q)
        if last:
            z = each(_dot, tinv, qs)
            tinv = each(lambda t_, x, y: t_ + x + y, tinv, q, z)
        else:
            both = each(lambda x, t_, w_: _dot(jnp.concatenate([x, t_], axis=0), w_), q, tinv, qs)
            tinv = each(lambda t_, x, y: t_ + x + y[c:], tinv, q, both)
            q = each(lambda y: y[:c], both)
    h = INV_BASE
    while h < c:
        link = same_blk(2 * h) & jnp.logical_not(same_blk(h))
        n_off = each(lambda x: jnp.where(link, x, 0.0), n_ab)
        y = each(lambda t_, x: x + _dot(t_, stk(x)), tinv, n_off)
        tinv = each(lambda t_, x: t_ + x + _dot(x, stk(t_)), tinv, y)
        h *= 2
    a_bar = each(lambda t_, x: x + _dot(t_, stk(x)), tinv, a_t)
    lv = each(lambda l_, p, y: _dot(jnp.concatenate([l_, p], axis=0), y), l_ak, p_rk, v_s)
    w = [x[:c] for x in lv]
    u0 = each(lambda t_, x: x + _dot(t_, stk(x)), tinv, w)
    r_bar = each(lambda x, p, y: x + _dot(p, stk(y)), r_t, p_rb, a_bar)
    y0 = each(lambda p, x, z: _dot(p, stk(x)) + z[c:], p_rb, u0, lv)
    zero = jnp.zeros((c, LANES), F32)
    ms = each(lambda ab, u, v_, bh, kh_: _dot_tn(
        jnp.concatenate([jnp.concatenate([ab, u], axis=1), jnp.concatenate([zero, v_], axis=1)], axis=0),
        jnp.concatenate([bh, kh_], axis=0)), a_bar, u0, vv, b_h, k_h)
    m = [jnp.where(same_head, x[:LANES], 0.0) for x in ms]
    s0p = [jnp.where(same_head, x[LANES:], 0.0) for x in ms]
    return [(r_bar[i], y0[i], m[i], s0p[i], jnp.exp(tot[i]), bonus[i]) for i in range(n)]


def _rwkv_scan_kernel(*refs, has_s0, emit_state):
    (r_ref, k_ref, v_ref, lwf_ref, lwb_ref, af_ref, ab_ref,
     kkf_ref, kkb_ref, kaf_ref, kab_ref, rkf_ref, rkb_ref, lnw_ref, lnb_ref) = refs[:15]
    s0_ref = refs[15] if has_s0 else None
    o_ref = refs[15 + has_s0]
    s_ref = refs[16 + has_s0] if emit_state else None
    rbar_ref, y0_ref, m_ref, s0p_ref, dec_ref, bonus_ref, yf_ref, yb_ref = refs[16 + has_s0 + emit_state:]
    bb, t = r_ref.shape[1], r_ref.shape[2]
    nc = t // CHUNK
    dirs = ((lwf_ref, af_ref, kkf_ref, kaf_ref, rkf_ref, False), (lwb_ref, ab_ref, kkb_ref, kab_ref, rkb_ref, True))
    group = min(bb * nc, PREP_GROUP)
    assert (bb * nc) % group == 0
    chunk_rows = lambda ci: pl.ds(pl.multiple_of(ci * CHUNK, CHUNK), CHUNK)

    def prep(gi, carry):
        insts, where = [], []
        at = [((gi * group + j) // nc, chunk_rows((gi * group + j) % nc)) for j in range(group)]
        for j, (e, sl) in enumerate(at):
            r, k, v = (ref[0, e, sl, :].astype(F32) for ref in (r_ref, k_ref, v_ref))
            for d, (lw_ref, a_ref, kk_ref, ka_ref, rk_ref, reverse) in enumerate(dirs):
                insts.append((r, k, v, lw_ref[0, e, sl, :], a_ref[0, e, sl, :].astype(F32), kk_ref[...], ka_ref[...],
                              rk_ref[...], reverse))
                where.append((d, gi * group + j, e, sl))
        terms = _rwkv_chunk_terms(insts)
        for (d, fi, e, sl), (r_bar, y0, m, s0p, dec, bon) in zip(where, terms):
            rbar_ref[d, fi] = r_bar.astype(BF16)
            y0_ref[d, fi] = y0
            m_ref[d, fi] = m.astype(BF16)
            s0p_ref[d, fi] = s0p
            dec_ref[d, fi] = jnp.broadcast_to(dec, (8, LANES))
        for j in range(group):
            _, _, e, sl = where[2 * j]
            bonus_ref[e, sl, :] = terms[2 * j][5] + terms[2 * j + 1][5]
        return carry

    lax.fori_loop(0, bb * nc // group, prep, 0)

    seq_steps = SEQ_STEPS if nc % SEQ_STEPS == 0 else 1
    idx = [(d, e) for d in range(2) for e in range(bb)]

    def seq(i, carry):
        for s in range(seq_steps):
            cis = (i * seq_steps + s, nc - 1 - i * seq_steps - s)
            sb16 = [st.astype(BF16) for st in carry]
            sm = [jnp.dot(sb16[n], m_ref[d, e * nc + cis[d]], preferred_element_type=F32)
                  for n, (d, e) in enumerate(idx)]
            ys = [_dot_nt(rbar_ref[d, e * nc + cis[d]], sb16[n]) + y0_ref[d, e * nc + cis[d]]
                  for n, (d, e) in enumerate(idx)]
            for n, (d, e) in enumerate(idx):
                (yf_ref, yb_ref)[d][e, chunk_rows(cis[d]), :] = ys[n]
            carry = tuple(carry[n] * dec_ref[d, e * nc + cis[d]][0:1] + sm[n] + s0p_ref[d, e * nc + cis[d]]
                          for n, (d, e) in enumerate(idx))
        return carry

    if has_s0:
        init = tuple(s0_ref[e, d, 0] for d in range(2) for e in range(bb))
    else:
        init = tuple(jnp.zeros((LANES, LANES), F32) for _ in range(2 * bb))
    final = lax.fori_loop(0, nc // seq_steps, seq, init)
    if emit_state:
        for d in range(2):
            for e in range(bb):
                s = final[d * bb + e]
                s_ref[e, d, 0] = s[:RWKV_HEAD, :RWKV_HEAD]
                s_ref[e, d, 1] = pltpu.roll(s, RWKV_HEAD, 1)[RWKV_HEAD:, :RWKV_HEAD]

    prow = lax.broadcasted_iota(jnp.int32, (LANES, LANES), 0)
    pcol = lax.broadcasted_iota(jnp.int32, (LANES, LANES), 1)
    head_ones = ((prow >= RWKV_HEAD) == (pcol >= RWKV_HEAD)).astype(BF16)
    inv_n = 1.0 / RWKV_HEAD

    def head_mean(x):
        hi = x.astype(BF16)
        lo = (x - hi.astype(F32)).astype(BF16)
        both = jnp.dot(jnp.concatenate([hi, lo], axis=0), head_ones, preferred_element_type=F32)
        return (both[:x.shape[0]] + both[x.shape[0]:]) * inv_n

    def finish(gi, carry):
        at = [((gi * group + j) // nc, chunk_rows((gi * group + j) % nc)) for j in range(group)]
        y = [yf_ref[e, sl, :] + yb_ref[e, sl, :] for e, sl in at]
        yc = [x - head_mean(x) for x in y]
        var = [head_mean(x * x) for x in yc]
        for (e, sl), x, vr in zip(at, yc, var):
            o_ref[0, e, sl, :] = (x * lax.rsqrt(vr + LNX_EPS) * lnw_ref[...] + lnb_ref[...]
                               + bonus_ref[e, sl, :]).astype(BF16)
        return carry

    lax.fori_loop(0, bb * nc // group, finish, 0)


def _rwkv_scan(r, k, v, lw, a, k_k, k_a, r_k, lnx_w, lnx_b, s0, bb, emit_state):
    _, bsz, t, _ = r.shape
    d = D_MODEL
    nc = t // CHUNK
    seq = lambda b, p: (p, b, 0, 0)
    seq_b = lambda b, p: (RWKV_PAIRS + p, b, 0, 0)
    par = lambda b, p: (0, p)
    par_b = lambda b, p: (0, RWKV_PAIRS + p)
    sblk = pl.BlockSpec((1, bb, t, LANES), seq)
    sblk_b = pl.BlockSpec((1, bb, t, LANES), seq_b)
    pblk = pl.BlockSpec((1, LANES), par)
    pblk_b = pl.BlockSpec((1, LANES), par_b)
    s0blk = pl.BlockSpec((bb, 2, 1, LANES, LANES), lambda b, p: (b, 0, p, 0, 0))
    s_blk = pl.BlockSpec((bb, 2, 2, RWKV_HEAD, RWKV_HEAD), lambda b, p: (b, 0, p, 0, 0))
    s_shape = jax.ShapeDtypeStruct((bsz, 2, RWKV_HEADS, RWKV_HEAD, RWKV_HEAD), F32)
    has_s0 = s0 is not None
    k_k, k_a, r_k = (p.reshape(1, 2 * d) for p in (k_k, k_a, r_k))
    return pl.pallas_call(
        functools.partial(_rwkv_scan_kernel, has_s0=has_s0, emit_state=emit_state),
        grid=(bsz // bb, RWKV_PAIRS),
        in_specs=[sblk, sblk, sblk, sblk, sblk_b, sblk, sblk_b,
                  pblk, pblk_b, pblk, pblk_b, pblk, pblk_b, pblk, pblk] + [s0blk] * has_s0,
        out_specs=[sblk] + [s_blk] * emit_state,
        out_shape=[jax.ShapeDtypeStruct((RWKV_PAIRS, bsz, t, LANES), BF16)] + [s_shape] * emit_state,
        scratch_shapes=[pltpu.VMEM((2, bb * nc, CHUNK, LANES), BF16), pltpu.VMEM((2, bb * nc, CHUNK, LANES), F32),
                        pltpu.VMEM((2, bb * nc, LANES, LANES), BF16), pltpu.VMEM((2, bb * nc, LANES, LANES), F32),
                        pltpu.VMEM((2, bb * nc, 8, LANES), F32), pltpu.VMEM((bb, t, LANES), F32),
                        pltpu.VMEM((bb, t, LANES), F32), pltpu.VMEM((bb, t, LANES), F32)],
        compiler_params=_params(2),
        name="rwkv_scan",
    )(r, k, v, lw, lw, a, a, k_k, k_k, k_a, k_a, r_k, r_k,
      lnx_w.reshape(1, d), lnx_b.reshape(1, d), *([s0] * has_s0))


def _rwkv_back_kernel(o_ref, g_ref, x_ref, mod_ref, wout_ref, nf_ref, y_ref):
    o = jnp.concatenate([o_ref[p] for p in range(o_ref.shape[0])], axis=1)
    out = _dot(o.astype(F32) * _silu(g_ref[...].astype(F32)), wout_ref[...])
    x2 = x_ref[...] + mod_ref[0][:, 2 * D_MODEL:] * out
    y_ref[...] = _rms(x2) * nf_ref[...]


def _rwkv_back(o, g, x, mod1, mod_idx, tm, w_out, norm_f):
    n, d = x.shape
    nmod = mod1.shape[0]
    row = lambda i: (i, 0)
    const = lambda i: (0, 0)
    return pl.pallas_call(
        _rwkv_back_kernel,
        grid=(n // tm,),
        in_specs=[pl.BlockSpec((RWKV_PAIRS, tm, LANES), lambda i: (0, i, 0)),
                  pl.BlockSpec((tm, d), row), pl.BlockSpec((tm, d), row),
                  pl.BlockSpec((1, 1, 3 * d), lambda i: (mod_idx(i), 0, 0)),
                  pl.BlockSpec((d, d), const), pl.BlockSpec((1, d), const)],
        out_specs=pl.BlockSpec((tm, d), row),
        out_shape=jax.ShapeDtypeStruct((n, d), F32),
        compiler_params=_params(1),
        name="rwkv_back",
    )(o, g, x, mod1.reshape(nmod, 1, 3 * d), w_out.astype(BF16), norm_f.reshape(1, d))


def _pair_blockdiag(s):
    b = s.shape[0]
    s = s.reshape(b, 2, RWKV_PAIRS, 2, RWKV_HEAD, RWKV_HEAD)
    z = jnp.zeros_like(s[:, :, :, 0])
    top = jnp.concatenate([s[:, :, :, 0], z], axis=-1)
    bot = jnp.concatenate([z, s[:, :, :, 1]], axis=-1)
    return jnp.concatenate([top, bot], axis=-2)


def _tile_rows(n_seq, t, may_span, rows):
    if t % rows == 0:
        return rows
    if may_span and rows % t == 0 and (n_seq * t) % rows == 0:
        return rows
    return t


def kernel(x_prompt, x_sample, state_gla, state_rwkv, c, c_ctx, w_mod, b_mod, norm_w, gla_w_in, gla_w_a1,
           gla_w_a2, gla_b_a, gla_norm, gla_w_out, rwkv_mu, rwkv_w_rkvg, rwkv_w0, rwkv_w1, rwkv_w2, rwkv_a0,
           rwkv_a1, rwkv_a2, rwkv_k_k, rwkv_k_a, rwkv_r_k, rwkv_lnx_w, rwkv_lnx_b, rwkv_w_out, norm_f):
    d = D_MODEL
    bp, tp, _ = x_prompt.shape
    bs, ts, _ = x_sample.shape
    assert tp % CHUNK == 0 and ts % CHUNK == 0 and ts % GRID_W == 0

    nrows = -(-(1 + bs) // 8) * 8
    cond = jnp.zeros((nrows, d), F32).at[0].set(c_ctx).at[1:1 + bs].set(c)
    cond = cond * (1.0 / (1.0 + jnp.exp(-cond)))
    mod0 = _matmul_bias(cond, w_mod[0], b_mod[0].reshape(1, 3 * d))
    mod1 = _matmul_bias(cond, w_mod[1], b_mod[1].reshape(1, 3 * d))

    def trunk(x3, shared_mod, grid_w, gla_s0, rwkv_s0, rwkv_bb, emit_state):
        b, t, _ = x3.shape
        n = b * t
        tm, tw = (_tile_rows(b, t, shared_mod, rows) for rows in (TOKEN_TILE, WIDE_TILE))
        mod_row = lambda rows: (lambda i: 0) if shared_mod else (lambda i: 1 + i // (t // rows))
        x = x3.reshape(n, d)
        seq = lambda arr: arr.reshape(b, t, arr.shape[-1])
        q, k, v, g, lg = _gla_front(x, mod0, mod_row(tw), tw, norm_w[0], gla_w_in[0], gla_w_a1[0], gla_w_a2[0], gla_b_a[0])
        o, *gla_s = _gla_scan(seq(q), seq(k), seq(v), seq(lg), gla_s0, GLA_HEADS, emit_state)
        x1, h1 = _gla_back(o.reshape(n, d), g, x, mod0, mod1, mod_row(tw), tw, gla_norm[0], gla_w_out[0], norm_w[1])
        r, k, v, g, lw, a = _rwkv_front(h1, tm, t, grid_w, rwkv_mu[0], rwkv_w_rkvg[0], rwkv_w0[0], rwkv_w1[0], rwkv_w2[0],
                                        rwkv_a0[0], rwkv_a1[0], rwkv_a2[0])
        pseq = lambda arr: arr.reshape(arr.shape[0], b, t, LANES)
        o, *rwkv_s = _rwkv_scan(pseq(r), pseq(k), pseq(v), pseq(lw), pseq(a), rwkv_k_k[0], rwkv_k_a[0], rwkv_r_k[0],
                                rwkv_lnx_w[0], rwkv_lnx_b[0], rwkv_s0, rwkv_bb, emit_state)
        y = _rwkv_back(o.reshape(RWKV_PAIRS, n, LANES), g, x1, mod1, mod_row(tw), tw, rwkv_w_out[0], norm_f)
        return y.reshape(b, t, d), gla_s, rwkv_s

    y_prompt, (gla_s,), (rwkv_s,) = trunk(x_prompt, True, None, None, None, min(bp, 4), True)

    y_sample, _, _ = trunk(x_sample, False, GRID_W, state_gla[:, 0], _pair_blockdiag(state_rwkv[:, 0]),
                           min(bs, 4), False)
    return (y_prompt, y_sample, gla_s[:, None], rwkv_s[:, None])
```

```python
import functools

import jax
import jax.numpy as jnp
from jax import lax
from jax.experimental import pallas as pl
from jax.experimental.pallas import tpu as pltpu

F32 = jnp.float32
BF16 = jnp.bfloat16

D_MODEL = 1024
EPS = 1e-6
GRID_W = 64
GLA_HEADS = 4
GLA_DK = 512
GLA_DV = 1024
GLA_DKH = 128
GLA_DVH = 256
GLA_GATE_RANK = 16
GLA_GATE_NORM = 16.0
RWKV_HEAD = 64
RWKV_HEADS = 16
RWKV_PAIRS = 8
RWKV_RANK = 64
LNX_EPS = 64e-5
RWKV_DECAY_SCALE = 0.6065306597126334
CHUNK = 64
SEQ_STEPS = 4
GLA_STEPS = 2
INV_BASE = 8
TOKEN_TILE = 512
WIDE_TILE = 1024
FINISH_GROUP = 32
PREP_GROUP = 8
LANES = 128
VMEM_LIMIT = 56 * 1024 * 1024


def _dot(a, b):
    return jnp.dot(a.astype(BF16), b.astype(BF16), preferred_element_type=F32)


def _dot_nt(a, b):
    return lax.dot_general(a.astype(BF16), b.astype(BF16), (((1,), (1,)), ((), ())),
                           preferred_element_type=F32)


def _dot_tn(a, b):
    return lax.dot_general(a.astype(BF16), b.astype(BF16), (((0,), (0,)), ((), ())),
                           preferred_element_type=F32)


def _tri_dot(tri, x):
    hi = x.astype(BF16)
    lo = (x - hi.astype(F32)).astype(BF16)
    t = tri.astype(BF16)
    return jnp.dot(t, hi, preferred_element_type=F32) + jnp.dot(t, lo, preferred_element_type=F32)


def _log_sigmoid(z):
    return jnp.minimum(z, 0.0) - jnp.log(1.0 + jnp.exp(-jnp.abs(z)))


def _sigmoid(z):
    return 1.0 / (1.0 + jnp.exp(-z))


def _silu(z):
    return z * _sigmoid(z)


def _rms(x):
    return x * lax.rsqrt(jnp.mean(x * x, axis=-1, keepdims=True) + EPS)


def _params(n_grid_dims):
    return pltpu.CompilerParams(dimension_semantics=("arbitrary",) * n_grid_dims,
                                vmem_limit_bytes=VMEM_LIMIT)


def _mm_kernel(x_ref, w_ref, b_ref, o_ref):
    o_ref[...] = _dot(x_ref[...], w_ref[...]) + b_ref[...]


def _matmul_bias(x, w, b):
    m, k = x.shape
    n = w.shape[1]
    return pl.pallas_call(
        _mm_kernel,
        grid=(1,),
        in_specs=[pl.BlockSpec((m, k), lambda i: (0, 0)),
                  pl.BlockSpec((k, n), lambda i: (0, 0)),
                  pl.BlockSpec((1, n), lambda i: (0, 0))],
        out_specs=pl.BlockSpec((m, n), lambda i: (0, 0)),
        out_shape=jax.ShapeDtypeStruct((m, n), F32),
        compiler_params=_params(1),
        name="mod_matmul",
    )(x, w.astype(BF16), b)


def _gla_front_kernel(x_ref, mod_ref, nw_ref, win_ref, wa1_ref, wa2_ref, ba_ref,
                      q_ref, k_ref, v_ref, g_ref, lg_ref):
    d = D_MODEL
    mod = mod_ref[0]
    shift, scale = mod[:, :d], mod[:, d:2 * d]
    h = _rms(x_ref[...]) * nw_ref[...] * (1.0 + scale) + shift
    hb = h.astype(BF16)
    proj = lambda lo, hi: jnp.dot(hb, win_ref[:, lo:hi], preferred_element_type=F32)
    q_ref[...] = (proj(0, GLA_DK) * (GLA_DKH ** -0.5)).astype(BF16)
    k_ref[...] = proj(GLA_DK, 2 * GLA_DK).astype(BF16)
    v_ref[...] = proj(2 * GLA_DK, 2 * GLA_DK + GLA_DV).astype(BF16)
    g_ref[...] = proj(2 * GLA_DK + GLA_DV, 2 * GLA_DK + 2 * GLA_DV).astype(BF16)
    t = jnp.dot(hb, wa1_ref[...], preferred_element_type=F32)
    z = _dot(t, wa2_ref[...]) + ba_ref[...]
    lg_ref[...] = _log_sigmoid(z) * (1.0 / GLA_GATE_NORM)


def _gla_front(x, mod, mod_idx, tm, norm_w, w_in, w_a1, w_a2, b_a):
    n, d = x.shape
    nmod = mod.shape[0]
    wa1 = jnp.zeros((d, LANES), F32).at[:, :GLA_GATE_RANK].set(w_a1[0]).at[:, GLA_GATE_RANK:2 * GLA_GATE_RANK].set(w_a1[1])
    wa2 = jnp.zeros((LANES, 2 * GLA_DK), F32).at[:GLA_GATE_RANK, :GLA_DK].set(w_a2[0])
    wa2 = wa2.at[GLA_GATE_RANK:2 * GLA_GATE_RANK, GLA_DK:].set(w_a2[1])
    ba = b_a.reshape(1, 2 * GLA_DK)
    row = lambda i: (i, 0)
    const = lambda i: (0, 0)
    n_in = w_in.shape[1]
    outs = pl.pallas_call(
        _gla_front_kernel,
        grid=(n // tm,),
        in_specs=[pl.BlockSpec((tm, d), row),
                  pl.BlockSpec((1, 1, 3 * d), lambda i: (mod_idx(i), 0, 0)),
                  pl.BlockSpec((1, d), const),
                  pl.BlockSpec((d, n_in), const),
                  pl.BlockSpec((d, LANES), const),
                  pl.BlockSpec((LANES, 2 * GLA_DK), const),
                  pl.BlockSpec((1, 2 * GLA_DK), const)],
        out_specs=[pl.BlockSpec((tm, GLA_DK), row), pl.BlockSpec((tm, GLA_DK), row),
                   pl.BlockSpec((tm, GLA_DV), row), pl.BlockSpec((tm, GLA_DV), row),
                   pl.BlockSpec((tm, 2 * GLA_DK), row)],
        out_shape=[jax.ShapeDtypeStruct((n, GLA_DK), BF16), jax.ShapeDtypeStruct((n, GLA_DK), BF16),
                   jax.ShapeDtypeStruct((n, GLA_DV), BF16), jax.ShapeDtypeStruct((n, GLA_DV), BF16),
                   jax.ShapeDtypeStruct((n, 2 * GLA_DK), F32)],
        compiler_params=_params(1),
        name="gla_front",
    )(x, mod.reshape(nmod, 1, 3 * d), norm_w.reshape(1, d), w_in.astype(BF16), wa1.astype(BF16),
      wa2.astype(BF16), ba)
    return outs


def _gla_chunks(chains):
    flat = [(x, ch[2]) for ch in chains for x in ch[0]]
    c = flat[0][0][0].shape[0]
    dk = flat[0][0][1].shape[1]
    row = lax.broadcasted_iota(jnp.int32, (c, c), 0)
    col = lax.broadcasted_iota(jnp.int32, (c, c), 1)
    keep = [(col >= row) if rv else (col <= row) for _, rv in flat]
    b = [_tri_dot(kp.astype(F32), x[3]) for kp, (x, _) in zip(keep, flat)]
    b_end = [bb[0:1] if rv else bb[c - 1:c] for bb, (_, rv) in zip(b, flat)]
    qb = [(x[0] * jnp.exp(bb)).astype(BF16) for (x, _), bb in zip(flat, b)]
    kb = [x[1] * jnp.exp(-bb) for (x, _), bb in zip(flat, b)]
    kd = [x[1] * jnp.exp(be - bb) for (x, _), bb, be in zip(flat, b, b_end)]
    scores = [jnp.where(kp, _dot_nt(x, y), 0.0).astype(BF16) for kp, x, y in zip(keep, qb, kb)]
    upd = [_dot_tn(x, y[2]) for (y, _), x in zip(flat, kd)]
    decay = [jnp.broadcast_to(jnp.exp(be), (dk, dk)).T for be in b_end]
    n_steps = len(chains[0][0])
    st = [ch[1] for ch in chains]
    outs = [[] for _ in chains]
    for s in range(n_steps):
        at = [ci * n_steps + s for ci in range(len(chains))]
        for ci, f in enumerate(at):
            outs[ci].append(_dot(jnp.concatenate([scores[f], qb[f]], axis=1),
                                 jnp.concatenate([flat[f][0][2], st[ci].astype(BF16)], axis=0)))
        st = [st[ci] * jnp.concatenate([decay[f]] * (st[ci].shape[1] // dk), axis=1) + upd[f]
              for ci, f in enumerate(at)]
    return list(zip(outs, st))


def _gla_scan_kernel(*refs, has_s0, emit_state):
    q_ref, k_ref, v_ref, lf_ref, lb_ref = refs[:5]
    s0_ref = refs[5] if has_s0 else None
    o_ref = refs[5 + has_s0]
    s_ref = refs[6 + has_s0] if emit_state else None
    acc_ref = refs[-1]
    t = q_ref.shape[1]
    hb = q_ref.shape[2] // GLA_DKH
    nc = t // CHUNK
    half = nc // 2
    kh = lambda h: slice(h * GLA_DKH, (h + 1) * GLA_DKH)
    vh = lambda h: slice(h * GLA_DVH, (h + 1) * GLA_DVH)

    steps = GLA_STEPS if half % GLA_STEPS == 0 else 1

    def both(accumulate):
        def body(i, carry):
            sls = [[pl.ds(pl.multiple_of(ci * CHUNK, CHUNK), CHUNK)
                    for ci in ((i * steps + s, nc - 1 - i * steps - s)[d] for s in range(steps))] for d in range(2)]
            chains = [([(q_ref[0, sl, kh(h)].astype(F32), k_ref[0, sl, kh(h)].astype(F32), v_ref[0, sl, vh(h)],
                         (lf_ref, lb_ref)[d][0, sl, kh(h)]) for sl in sls[d]], carry[d * hb + h], d == 1)
                      for d in range(2) for h in range(hb)]
            res = _gla_chunks(chains)
            for d in range(2):
                for h in range(hb):
                    for sl, o in zip(sls[d], res[d * hb + h][0]):
                        if accumulate:
                            o_ref[0, sl, vh(h)] = (acc_ref[sl, vh(h)] + o).astype(BF16)
                        else:
                            acc_ref[sl, vh(h)] = o
            return tuple(st for _, st in res)
        return body

    if has_s0:
        init = tuple(s0_ref[0, d, h] for d in range(2) for h in range(hb))
    else:
        init = tuple(jnp.zeros((GLA_DKH, GLA_DVH), F32) for _ in range(2 * hb))
    carry = lax.fori_loop(0, half // steps, both(False), init)
    carry = lax.fori_loop(half // steps, nc // steps, both(True), carry)
    if emit_state:
        for d in range(2):
            for h in range(hb):
                s_ref[0, d, h] = carry[d * hb + h]


def _gla_scan(q, k, v, lg, s0, hb, emit_state):
    bsz, t, _ = q.shape
    ng = GLA_HEADS // hb
    hd = lambda b, h: (b, 0, h)
    kblk = pl.BlockSpec((1, t, hb * GLA_DKH), hd)
    vblk = pl.BlockSpec((1, t, hb * GLA_DVH), hd)
    sblk = pl.BlockSpec((1, 2, hb, GLA_DKH, GLA_DVH), lambda b, h: (b, 0, h, 0, 0))
    s_shape = jax.ShapeDtypeStruct((bsz, 2, GLA_HEADS, GLA_DKH, GLA_DVH), F32)
    has_s0 = s0 is not None
    return pl.pallas_call(
        functools.partial(_gla_scan_kernel, has_s0=has_s0, emit_state=emit_state),
        grid=(bsz, ng),
        in_specs=[kblk, kblk, vblk, kblk, pl.BlockSpec((1, t, hb * GLA_DKH), lambda b, h: (b, 0, ng + h))]
        + [sblk] * has_s0,
        out_specs=[vblk] + [sblk] * emit_state,
        out_shape=[jax.ShapeDtypeStruct((bsz, t, GLA_DV), BF16)] + [s_shape] * emit_state,
        scratch_shapes=[pltpu.VMEM((t, hb * GLA_DVH), F32)],
        compiler_params=_params(2),
        name="gla_scan",
    )(q, k, v, lg, lg, *([s0] * has_s0))


def _gla_back_kernel(o_ref, g_ref, x_ref, mod0_ref, mod1_ref, gn_ref, wout_ref, nw1_ref, x1_ref, h1_ref):
    d = D_MODEL
    o = o_ref[...].astype(F32)
    gn = gn_ref[...]
    parts = [_rms(o[:, h * GLA_DVH:(h + 1) * GLA_DVH]) * gn for h in range(GLA_HEADS)]
    on = jnp.concatenate(parts, axis=-1) * _silu(g_ref[...].astype(F32))
    out = _dot(on, wout_ref[...])
    gate = mod0_ref[0][:, 2 * d:]
    x1 = x_ref[...] + gate * out
    x1_ref[...] = x1
    mod1 = mod1_ref[0]
    h1_ref[...] = (_rms(x1) * nw1_ref[...] * (1.0 + mod1[:, d:2 * d]) + mod1[:, :d]).astype(BF16)


def _gla_back(o, g, x, mod0, mod1, mod_idx, tm, gla_norm, w_out, norm_w1):
    n, d = x.shape
    nmod = mod0.shape[0]
    row = lambda i: (i, 0)
    const = lambda i: (0, 0)
    modspec = pl.BlockSpec((1, 1, 3 * d), lambda i: (mod_idx(i), 0, 0))
    return pl.pallas_call(
        _gla_back_kernel,
        grid=(n // tm,),
        in_specs=[pl.BlockSpec((tm, d), row), pl.BlockSpec((tm, d), row), pl.BlockSpec((tm, d), row),
                  modspec, modspec,
                  pl.BlockSpec((1, GLA_DVH), const), pl.BlockSpec((d, d), const), pl.BlockSpec((1, d), const)],
        out_specs=[pl.BlockSpec((tm, d), row), pl.BlockSpec((tm, d), row)],
        out_shape=[jax.ShapeDtypeStruct((n, d), F32), jax.ShapeDtypeStruct((n, d), BF16)],
        compiler_params=_params(1),
        name="gla_back",
    )(o, g, x, mod0.reshape(nmod, 1, 3 * d), mod1.reshape(nmod, 1, 3 * d), gla_norm.reshape(1, GLA_DVH),
      w_out.astype(BF16), norm_w1.reshape(1, d))


def _token_shift(h, up_ref, dn_ref, seq_len, grid_w):
    tm = h.shape[0]
    pos = lax.broadcasted_iota(jnp.int32, (tm, 1), 0) + pl.program_id(0) * tm
    prev = pltpu.roll(h, 1, 0)
    nxt = pltpu.roll(h, tm - 1, 0)
    t = pos % seq_len
    if grid_w is None:
        return 0.5 * (jnp.where(t == 0, 0.0, prev) + jnp.where(t == seq_len - 1, 0.0, nxt))
    col = pos % grid_w
    up = jnp.concatenate([up_ref[...].astype(F32), h[:tm - grid_w]], axis=0)
    dn = jnp.concatenate([h[grid_w:], dn_ref[...].astype(F32)], axis=0)
    up = jnp.where(t < grid_w, 0.0, up)
    dn = jnp.where(t >= seq_len - grid_w, 0.0, dn)
    left = jnp.where(col == 0, 0.0, prev)
    right = jnp.where(col == grid_w - 1, 0.0, nxt)
    return 0.25 * (up + dn + left + right)


def _rwkv_front_kernel(*refs, seq_len, grid_w):
    n_h = 1 if grid_w is None else 3
    h_ref, up_ref, dn_ref = (tuple(refs[:n_h]) + (None, None))[:3]
    (mu_ref, wrkvg_ref, w1_ref, w2_ref, w0_ref, a1_ref, a2_ref, a0_ref,
     r_ref, k_ref, v_ref, g_ref, lw_ref, a_ref) = refs[n_h:]
    h = h_ref[...].astype(F32)
    dh = _token_shift(h, up_ref, dn_ref, seq_len, grid_w) - h
    mix = lambda p: (h + dh * mu_ref[p:p + 1, :]).astype(BF16)
    r_ref[...] = jnp.dot(mix(0), wrkvg_ref[0], preferred_element_type=F32).astype(BF16)
    k_ref[...] = jnp.dot(mix(1), wrkvg_ref[1], preferred_element_type=F32).astype(BF16)
    v_ref[...] = jnp.dot(mix(2), wrkvg_ref[2], preferred_element_type=F32).astype(BF16)
    g_ref[...] = jnp.dot(mix(3), wrkvg_ref[3], preferred_element_type=F32).astype(BF16)
    tw = jnp.tanh(jnp.dot(mix(4), w1_ref[...], preferred_element_type=F32))
    lw_ref[...] = -RWKV_DECAY_SCALE * _sigmoid(w0_ref[...] + _dot(tw, w2_ref[...]))
    ta = jnp.dot(mix(5), a1_ref[...], preferred_element_type=F32)
    a_ref[...] = _sigmoid(a0_ref[...] + _dot(ta, a2_ref[...])).astype(BF16)


def _rwkv_front(h, tm, seq_len, grid_w, mu, w_rkvg, w0, w1, w2, a0, a1, a2):
    n, d = h.shape
    rk = RWKV_RANK
    cat1 = lambda w: jnp.concatenate([w[0], w[1]], axis=1)
    bd2 = lambda w: jnp.zeros((2 * rk, 2 * d), F32).at[:rk, :d].set(w[0]).at[rk:, d:].set(w[1])
    row = lambda i: (i, 0)
    const = lambda i: (0, 0)
    if grid_w is None:
        assert tm % seq_len == 0
        h_specs, h_args = [pl.BlockSpec((tm, d), row)], [h]
    else:
        assert tm % grid_w == 0 and seq_len % tm == 0
        rows_per_tile, last = tm // grid_w, n // grid_w - 1
        h_specs = [pl.BlockSpec((tm, d), row),
                   pl.BlockSpec((grid_w, d), lambda i: (jnp.maximum(i * rows_per_tile - 1, 0), 0)),
                   pl.BlockSpec((grid_w, d), lambda i: (jnp.minimum((i + 1) * rows_per_tile, last), 0))]
        h_args = [h, h, h]
    return pl.pallas_call(
        functools.partial(_rwkv_front_kernel, seq_len=seq_len, grid_w=grid_w),
        grid=(n // tm,),
        in_specs=h_specs + [
            pl.BlockSpec((8, d), const),
            pl.BlockSpec((4, d, d), lambda i: (0, 0, 0)),
            pl.BlockSpec((d, 2 * rk), const), pl.BlockSpec((2 * rk, 2 * d), const), pl.BlockSpec((1, 2 * d), const),
            pl.BlockSpec((d, 2 * rk), const), pl.BlockSpec((2 * rk, 2 * d), const), pl.BlockSpec((1, 2 * d), const)],
        out_specs=[pl.BlockSpec((tm, d), row)] * 4 + [pl.BlockSpec((tm, 2 * d), row)] * 2,
        out_shape=[jax.ShapeDtypeStruct((n, d), BF16)] * 4
        + [jax.ShapeDtypeStruct((n, 2 * d), F32), jax.ShapeDtypeStruct((n, 2 * d), BF16)],
        compiler_params=_params(1),
        name="rwkv_front",
    )(*h_args, jnp.zeros((8, d), F32).at[:6].set(mu), w_rkvg.astype(BF16),
      cat1(w1).astype(BF16), bd2(w2).astype(BF16), w0.reshape(1, 2 * d),
      cat1(a1).astype(BF16), bd2(a2).astype(BF16), a0.reshape(1, 2 * d))


def _seg_sum(x, head0):
    s0 = jnp.sum(jnp.where(head0, x, 0.0), axis=-1, keepdims=True)
    s1 = jnp.sum(jnp.where(head0, 0.0, x), axis=-1, keepdims=True)
    return jnp.where(head0, s0, s1)


def _stack(x, head0):
    return jnp.concatenate([jnp.where(head0, x, 0.0), jnp.where(head0, 0.0, x)], axis=0)


def _rwkv_chunk_terms(insts):
    n = len(insts)
    c = insts[0][0].shape[0]
    lane = lax.broadcasted_iota(jnp.int32, (c, LANES), 1)
    head0 = lane < RWKV_HEAD
    row = lax.broadcasted_iota(jnp.int32, (c, c), 0)
    col = lax.broadcasted_iota(jnp.int32, (c, c), 1)
    ti = lax.broadcasted_iota(jnp.int32, (c, LANES), 0)
    si = lane & (c - 1)
    prow = lax.broadcasted_iota(jnp.int32, (LANES, LANES), 0)
    pcol = lax.broadcasted_iota(jnp.int32, (LANES, LANES), 1)
    same_head = (prow >= RWKV_HEAD) == (pcol >= RWKV_HEAD)
    each = lambda f, *ls: [f(*xs) for xs in zip(*ls)]
    rev = [x[8] for x in insts]
    stk = lambda x: _stack(x, head0).astype(BF16)

    cum = [_tri_dot(((col >= row) if x[8] else (col <= row)).astype(F32), x[3]) for x in insts]
    tot = [cm[0:1] if rv else cm[c - 1:c] for cm, rv in zip(cum, rev)]

    a_t, r_t, b_ts, k_ts, b_h, k_h, v_s, vv, bonus = [], [], [], [], [], [], [], [], []
    for (r, k, v, lw, a, k_k, k_a, r_k, _), cm, tt in zip(insts, cum, tot):
        kk = k * k_k
        kk = kk * lax.rsqrt(jnp.maximum(_seg_sum(kk * kk, head0), 1e-24))
        kd = k * (1.0 + (a - 1.0) * k_a)
        bv = kk * a
        bonus.append(_seg_sum(r * kd * r_k, head0) * v)
        e_neg = jnp.exp(-cm)
        e_tot = jnp.exp(tt)
        b_neg, k_neg = bv * e_neg, kd * e_neg
        a_t.append(-kk * jnp.exp(cm - lw))
        r_t.append(r * jnp.exp(cm))
        b_ts.append(stk(b_neg))
        k_ts.append(stk(k_neg))
        b_h.append(b_neg * e_tot)
        k_h.append(k_neg * e_tot)
        v_s.append(stk(v))
        vv.append(v)

    strict = [(si > ti) if rv else (si < ti) for rv in rev]
    incl = [(si >= ti) if rv else (si <= ti) for rv in rev]
    ar = each(lambda x, y: jnp.concatenate([x, y], axis=0).astype(BF16), a_t, r_t)
    g_b = each(_dot_nt, ar, b_ts)
    g_k = each(_dot_nt, ar, k_ts)
    n_ab = each(lambda m_, x: jnp.where(m_, x[:c], 0.0), strict, g_b)
    l_ak = each(lambda m_, x: jnp.where(m_, x[:c], 0.0), strict, g_k)
    p_rb = each(lambda m_, x: jnp.where(m_, x[c:], 0.0), incl, g_b)
    p_rk = each(lambda m_, x: jnp.where(m_, x[c:], 0.0), incl, g_k)

    same_blk = lambda h: (ti // h) == (si // h)
    n_d = each(lambda x: jnp.where(same_blk(INV_BASE), x, 0.0), n_ab)
    steps = max(1, (INV_BASE - 1).bit_length())
    q = n_d
    tinv = n_d
    q = each(lambda x: _dot(x, stk(x)), q) if steps > 1 else q
    for j in range(1, steps):
        last = j == steps - 1
        qs = each(stk, q)
        if last:
            z = each(_dot, tinv, qs)
            tinv = each(lambda t_, x, y: t_ + x + y, tinv, q, z)
        else:
            both = each(lambda x, t_, w_: _dot(jnp.concatenate([x, t_], axis=0), w_), q, tinv, qs)
            tinv = each(lambda t_, x, y: t_ + x + y[c:], tinv, q, both)
            q = each(lambda y: y[:c], both)
    h = INV_BASE
    while h < c:
        link = same_blk(2 * h) & jnp.logical_not(same_blk(h))
        n_off = each(lambda x: jnp.where(link, x, 0.0), n_ab)
        y = each(lambda t_, x: x + _dot(t_, stk(x)), tinv, n_off)
        tinv = each(lambda t_, x: t_ + x + _dot(x, stk(t_)), tinv, y)
        h *= 2
    a_bar = each(lambda t_, x: x + _dot(t_, stk(x)), tinv, a_t)
    lv = each(lambda l_, p, y: _dot(jnp.concatenate([l_, p], axis=0), y), l_ak, p_rk, v_s)
    w = [x[:c] for x in lv]
    u0 = each(lambda t_, x: x + _dot(t_, stk(x)), tinv, w)
    r_bar = each(lambda x, p, y: x + _dot(p, stk(y)), r_t, p_rb, a_bar)
    y0 = each(lambda p, x, z: _dot(p, stk(x)) + z[c:], p_rb, u0, lv)
    zero = jnp.zeros((c, LANES), F32)
    ms = each(lambda ab, u, v_, bh, kh_: _dot_tn(
        jnp.concatenate([jnp.concatenate([ab, u], axis=1), jnp.concatenate([zero, v_], axis=1)], axis=0),
        jnp.concatenate([bh, kh_], axis=0)), a_bar, u0, vv, b_h, k_h)
    m = [jnp.where(same_head, x[:LANES], 0.0) for x in ms]
    s0p = [jnp.where(same_head, x[LANES:], 0.0) for x in ms]
    return [(r_bar[i], y0[i], m[i], s0p[i], jnp.exp(tot[i]), bonus[i]) for i in range(n)]


def _rwkv_scan_kernel(*refs, has_s0, emit_state):
    (r_ref, k_ref, v_ref, lwf_ref, lwb_ref, af_ref, ab_ref,
     kkf_ref, kkb_ref, kaf_ref, kab_ref, rkf_ref, rkb_ref, lnw_ref, lnb_ref) = refs[:15]
    s0_ref = refs[15] if has_s0 else None
    o_ref = refs[15 + has_s0]
    s_ref = refs[16 + has_s0] if emit_state else None
    rbar_ref, y0_ref, m_ref, s0p_ref, dec_ref, bonus_ref, yf_ref, yb_ref = refs[16 + has_s0 + emit_state:]
    bb, t = r_ref.shape[0], r_ref.shape[1]
    nc = t // CHUNK
    dirs = ((lwf_ref, af_ref, kkf_ref, kaf_ref, rkf_ref, False), (lwb_ref, ab_ref, kkb_ref, kab_ref, rkb_ref, True))
    group = min(bb * nc, PREP_GROUP)
    assert (bb * nc) % group == 0
    chunk_rows = lambda ci: pl.ds(pl.multiple_of(ci * CHUNK, CHUNK), CHUNK)

    def prep(gi, carry):
        insts, where = [], []
        at = [((gi * group + j) // nc, chunk_rows((gi * group + j) % nc)) for j in range(group)]
        for j, (e, sl) in enumerate(at):
            r, k, v = (ref[e, sl, :].astype(F32) for ref in (r_ref, k_ref, v_ref))
            for d, (lw_ref, a_ref, kk_ref, ka_ref, rk_ref, reverse) in enumerate(dirs):
                insts.append((r, k, v, lw_ref[e, sl, :], a_ref[e, sl, :].astype(F32), kk_ref[...], ka_ref[...],
                              rk_ref[...], reverse))
                where.append((d, gi * group + j, e, sl))
        terms = _rwkv_chunk_terms(insts)
        for (d, fi, e, sl), (r_bar, y0, m, s0p, dec, bon) in zip(where, terms):
            rbar_ref[d, fi] = r_bar.astype(BF16)
            y0_ref[d, fi] = y0
            m_ref[d, fi] = m.astype(BF16)
            s0p_ref[d, fi] = s0p
            dec_ref[d, fi] = jnp.broadcast_to(dec, (8, LANES))
        for j in range(group):
            _, _, e, sl = where[2 * j]
            bonus_ref[e, sl, :] = terms[2 * j][5] + terms[2 * j + 1][5]
        return carry

    lax.fori_loop(0, bb * nc // group, prep, 0)

    seq_steps = SEQ_STEPS if nc % SEQ_STEPS == 0 else 1
    idx = [(d, e) for d in range(2) for e in range(bb)]

    def seq(i, carry):
        for s in range(seq_steps):
            cis = (i * seq_steps + s, nc - 1 - i * seq_steps - s)
            sb16 = [st.astype(BF16) for st in carry]
            sm = [jnp.dot(sb16[n], m_ref[d, e * nc + cis[d]], preferred_element_type=F32)
                  for n, (d, e) in enumerate(idx)]
            ys = [_dot_nt(rbar_ref[d, e * nc + cis[d]], sb16[n]) + y0_ref[d, e * nc + cis[d]]
                  for n, (d, e) in enumerate(idx)]
            for n, (d, e) in enumerate(idx):
                (yf_ref, yb_ref)[d][e, chunk_rows(cis[d]), :] = ys[n]
            carry = tuple(carry[n] * dec_ref[d, e * nc + cis[d]][0:1] + sm[n] + s0p_ref[d, e * nc + cis[d]]
                          for n, (d, e) in enumerate(idx))
        return carry

    if has_s0:
        init = tuple(s0_ref[e, d, 0] for d in range(2) for e in range(bb))
    else:
        init = tuple(jnp.zeros((LANES, LANES), F32) for _ in range(2 * bb))
    final = lax.fori_loop(0, nc // seq_steps, seq, init)
    if emit_state:
        for d in range(2):
            for e in range(bb):
                s = final[d * bb + e]
                s_ref[e, d, 0] = s[:RWKV_HEAD, :RWKV_HEAD]
                s_ref[e, d, 1] = pltpu.roll(s, RWKV_HEAD, 1)[RWKV_HEAD:, :RWKV_HEAD]

    prow = lax.broadcasted_iota(jnp.int32, (LANES, LANES), 0)
    pcol = lax.broadcasted_iota(jnp.int32, (LANES, LANES), 1)
    head_ones = ((prow >= RWKV_HEAD) == (pcol >= RWKV_HEAD)).astype(BF16)
    inv_n = 1.0 / RWKV_HEAD

    def head_mean(x):
        hi = x.astype(BF16)
        lo = (x - hi.astype(F32)).astype(BF16)
        both = jnp.dot(jnp.concatenate([hi, lo], axis=0), head_ones, preferred_element_type=F32)
        return (both[:x.shape[0]] + both[x.shape[0]:]) * inv_n

    fgroup = min(bb * nc, FINISH_GROUP)
    assert (bb * nc) % fgroup == 0

    def finish(gi, carry):
        at = [((gi * fgroup + j) // nc, chunk_rows((gi * fgroup + j) % nc)) for j in range(fgroup)]
        y = [yf_ref[e, sl, :] + yb_ref[e, sl, :] for e, sl in at]
        yc = [x - head_mean(x) for x in y]
        var = [head_mean(x * x) for x in yc]
        for (e, sl), x, vr in zip(at, yc, var):
            o_ref[e, sl, :] = (x * lax.rsqrt(vr + LNX_EPS) * lnw_ref[...] + lnb_ref[...]
                               + bonus_ref[e, sl, :]).astype(BF16)
        return carry

    lax.fori_loop(0, bb * nc // fgroup, finish, 0)


def _rwkv_scan(r, k, v, lw, a, k_k, k_a, r_k, lnx_w, lnx_b, s0, bb, emit_state):
    bsz, t, d = r.shape
    nc = t // CHUNK
    seq = lambda b, p: (b, 0, p)
    seq_b = lambda b, p: (b, 0, RWKV_PAIRS + p)
    par = lambda b, p: (0, p)
    par_b = lambda b, p: (0, RWKV_PAIRS + p)
    sblk = pl.BlockSpec((bb, t, LANES), seq)
    sblk_b = pl.BlockSpec((bb, t, LANES), seq_b)
    pblk = pl.BlockSpec((1, LANES), par)
    pblk_b = pl.BlockSpec((1, LANES), par_b)
    s0blk = pl.BlockSpec((bb, 2, 1, LANES, LANES), lambda b, p: (b, 0, p, 0, 0))
    s_blk = pl.BlockSpec((bb, 2, 2, RWKV_HEAD, RWKV_HEAD), lambda b, p: (b, 0, p, 0, 0))
    s_shape = jax.ShapeDtypeStruct((bsz, 2, RWKV_HEADS, RWKV_HEAD, RWKV_HEAD), F32)
    has_s0 = s0 is not None
    k_k, k_a, r_k = (p.reshape(1, 2 * d) for p in (k_k, k_a, r_k))
    return pl.pallas_call(
        functools.partial(_rwkv_scan_kernel, has_s0=has_s0, emit_state=emit_state),
        grid=(bsz // bb, RWKV_PAIRS),
        in_specs=[sblk, sblk, sblk, sblk, sblk_b, sblk, sblk_b,
                  pblk, pblk_b, pblk, pblk_b, pblk, pblk_b, pblk, pblk] + [s0blk] * has_s0,
        out_specs=[sblk] + [s_blk] * emit_state,
        out_shape=[jax.ShapeDtypeStruct((bsz, t, d), BF16)] + [s_shape] * emit_state,
        scratch_shapes=[pltpu.VMEM((2, bb * nc, CHUNK, LANES), BF16), pltpu.VMEM((2, bb * nc, CHUNK, LANES), F32),
                        pltpu.VMEM((2, bb * nc, LANES, LANES), BF16), pltpu.VMEM((2, bb * nc, LANES, LANES), F32),
                        pltpu.VMEM((2, bb * nc, 8, LANES), F32), pltpu.VMEM((bb, t, LANES), F32),
                        pltpu.VMEM((bb, t, LANES), F32), pltpu.VMEM((bb, t, LANES), F32)],
        compiler_params=_params(2),
        name="rwkv_scan",
    )(r, k, v, lw, lw, a, a, k_k, k_k, k_a, k_a, r_k, r_k,
      lnx_w.reshape(1, d), lnx_b.reshape(1, d), *([s0] * has_s0))


def _rwkv_back_kernel(o_ref, g_ref, x_ref, mod_ref, wout_ref, nf_ref, y_ref):
    out = _dot(o_ref[...].astype(F32) * _silu(g_ref[...].astype(F32)), wout_ref[...])
    x2 = x_ref[...] + mod_ref[0][:, 2 * D_MODEL:] * out
    y_ref[...] = _rms(x2) * nf_ref[...]


def _rwkv_back(o, g, x, mod1, mod_idx, tm, w_out, norm_f):
    n, d = x.shape
    nmod = mod1.shape[0]
    row = lambda i: (i, 0)
    const = lambda i: (0, 0)
    return pl.pallas_call(
        _rwkv_back_kernel,
        grid=(n // tm,),
        in_specs=[pl.BlockSpec((tm, d), row), pl.BlockSpec((tm, d), row), pl.BlockSpec((tm, d), row),
                  pl.BlockSpec((1, 1, 3 * d), lambda i: (mod_idx(i), 0, 0)),
                  pl.BlockSpec((d, d), const), pl.BlockSpec((1, d), const)],
        out_specs=pl.BlockSpec((tm, d), row),
        out_shape=jax.ShapeDtypeStruct((n, d), F32),
        compiler_params=_params(1),
        name="rwkv_back",
    )(o, g, x, mod1.reshape(nmod, 1, 3 * d), w_out.astype(BF16), norm_f.reshape(1, d))


def _pair_blockdiag(s):
    b = s.shape[0]
    s = s.reshape(b, 2, RWKV_PAIRS, 2, RWKV_HEAD, RWKV_HEAD)
    z = jnp.zeros_like(s[:, :, :, 0])
    top = jnp.concatenate([s[:, :, :, 0], z], axis=-1)
    bot = jnp.concatenate([z, s[:, :, :, 1]], axis=-1)
    return jnp.concatenate([top, bot], axis=-2)


def _tile_rows(n_seq, t, may_span, rows):
    if t % rows == 0:
        return rows
    if may_span and rows % t == 0 and (n_seq * t) % rows == 0:
        return rows
    return t


def kernel(x_prompt, x_sample, state_gla, state_rwkv, c, c_ctx, w_mod, b_mod, norm_w, gla_w_in, gla_w_a1,
           gla_w_a2, gla_b_a, gla_norm, gla_w_out, rwkv_mu, rwkv_w_rkvg, rwkv_w0, rwkv_w1, rwkv_w2, rwkv_a0,
           rwkv_a1, rwkv_a2, rwkv_k_k, rwkv_k_a, rwkv_r_k, rwkv_lnx_w, rwkv_lnx_b, rwkv_w_out, norm_f):
    d = D_MODEL
    bp, tp, _ = x_prompt.shape
    bs, ts, _ = x_sample.shape
    assert tp % CHUNK == 0 and ts % CHUNK == 0 and ts % GRID_W == 0

    nrows = -(-(1 + bs) // 8) * 8
    cond = jnp.zeros((nrows, d), F32).at[0].set(c_ctx).at[1:1 + bs].set(c)
    cond = cond * (1.0 / (1.0 + jnp.exp(-cond)))
    mod0 = _matmul_bias(cond, w_mod[0], b_mod[0].reshape(1, 3 * d))
    mod1 = _matmul_bias(cond, w_mod[1], b_mod[1].reshape(1, 3 * d))

    def trunk(x3, shared_mod, grid_w, gla_s0, rwkv_s0, rwkv_bb, emit_state):
        b, t, _ = x3.shape
        n = b * t
        tm, tw = (_tile_rows(b, t, shared_mod, rows) for rows in (TOKEN_TILE, WIDE_TILE))
        mod_row = lambda rows: (lambda i: 0) if shared_mod else (lambda i: 1 + i // (t // rows))
        x = x3.reshape(n, d)
        seq = lambda arr: arr.reshape(b, t, arr.shape[-1])
        q, k, v, g, lg = _gla_front(x, mod0, mod_row(tw), tw, norm_w[0], gla_w_in[0], gla_w_a1[0], gla_w_a2[0], gla_b_a[0])
        o, *gla_s = _gla_scan(seq(q), seq(k), seq(v), seq(lg), gla_s0, GLA_HEADS, emit_state)
        x1, h1 = _gla_back(o.reshape(n, d), g, x, mod0, mod1, mod_row(tw), tw, gla_norm[0], gla_w_out[0], norm_w[1])
        r, k, v, g, lw, a = _rwkv_front(h1, tm, t, grid_w, rwkv_mu[0], rwkv_w_rkvg[0], rwkv_w0[0], rwkv_w1[0], rwkv_w2[0],
                                        rwkv_a0[0], rwkv_a1[0], rwkv_a2[0])
        o, *rwkv_s = _rwkv_scan(seq(r), seq(k), seq(v), seq(lw), seq(a), rwkv_k_k[0], rwkv_k_a[0], rwkv_r_k[0],
                                rwkv_lnx_w[0], rwkv_lnx_b[0], rwkv_s0, rwkv_bb, emit_state)
        y = _rwkv_back(o.reshape(n, d), g, x1, mod1, mod_row(tw), tw, rwkv_w_out[0], norm_f)
        return y.reshape(b, t, d), gla_s, rwkv_s

    y_prompt, (gla_s,), (rwkv_s,) = trunk(x_prompt, True, None, None, None, min(bp, 4), True)

    y_sample, _, _ = trunk(x_sample, False, GRID_W, state_gla[:, 0], _pair_blockdiag(state_rwkv[:, 0]),
                           min(bs, 4), False)
    return (y_prompt, y_sample, gla_s[:, None], rwkv_s[:, None])
```

```python
import functools

import jax
import jax.numpy as jnp
from jax import lax
from jax.experimental import pallas as pl
from jax.experimental.pallas import tpu as pltpu

F32 = jnp.float32
BF16 = jnp.bfloat16

D_MODEL = 1024
EPS = 1e-6
GRID_W = 64
GLA_HEADS = 4
GLA_DK = 512
GLA_DV = 1024
GLA_DKH = 128
GLA_DVH = 256
GLA_GATE_RANK = 16
GLA_GATE_NORM = 16.0
RWKV_HEAD = 64
RWKV_HEADS = 16
RWKV_PAIRS = 8
RWKV_RANK = 64
LNX_EPS = 64e-5
RWKV_DECAY_SCALE = 0.6065306597126334
CHUNK = 64
SEQ_STEPS = 4
GLA_STEPS = 2
INV_BASE = 8
TOKEN_TILE = 512
WIDE_TILE = 1024
FINISH_GROUP = 32
PREP_GROUP = 8
LANES = 128
VMEM_LIMIT = 56 * 1024 * 1024


def _dot(a, b):
    return jnp.dot(a.astype(BF16), b.astype(BF16), preferred_element_type=F32)


def _dot_nt(a, b):
    return lax.dot_general(a.astype(BF16), b.astype(BF16), (((1,), (1,)), ((), ())),
                           preferred_element_type=F32)


def _dot_tn(a, b):
    return lax.dot_general(a.astype(BF16), b.astype(BF16), (((0,), (0,)), ((), ())),
                           preferred_element_type=F32)


def _tri_dot(tri, x):
    hi = x.astype(BF16)
    lo = (x - hi.astype(F32)).astype(BF16)
    t = tri.astype(BF16)
    return jnp.dot(t, hi, preferred_element_type=F32) + jnp.dot(t, lo, preferred_element_type=F32)


def _log_sigmoid(z):
    return jnp.minimum(z, 0.0) - jnp.log(1.0 + jnp.exp(-jnp.abs(z)))


def _sigmoid(z):
    return 1.0 / (1.0 + jnp.exp(-z))


def _silu(z):
    return z * _sigmoid(z)


def _rms(x):
    return x * lax.rsqrt(jnp.mean(x * x, axis=-1, keepdims=True) + EPS)


def _params(n_grid_dims):
    return pltpu.CompilerParams(dimension_semantics=("arbitrary",) * n_grid_dims,
                                vmem_limit_bytes=VMEM_LIMIT)


def _mm_kernel(x_ref, w_ref, b_ref, o_ref):
    o_ref[...] = _dot(x_ref[...], w_ref[...]) + b_ref[...]


def _matmul_bias(x, w, b):
    m, k = x.shape
    n = w.shape[1]
    return pl.pallas_call(
        _mm_kernel,
        grid=(1,),
        in_specs=[pl.BlockSpec((m, k), lambda i: (0, 0)),
                  pl.BlockSpec((k, n), lambda i: (0, 0)),
                  pl.BlockSpec((1, n), lambda i: (0, 0))],
        out_specs=pl.BlockSpec((m, n), lambda i: (0, 0)),
        out_shape=jax.ShapeDtypeStruct((m, n), F32),
        compiler_params=_params(1),
        name="mod_matmul",
    )(x, w.astype(BF16), b)


def _gla_front_kernel(x_ref, mod_ref, nw_ref, win_ref, wa1_ref, wa2_ref, ba_ref,
                      q_ref, k_ref, v_ref, g_ref, lg_ref):
    d = D_MODEL
    mod = mod_ref[0]
    shift, scale = mod[:, :d], mod[:, d:2 * d]
    h = _rms(x_ref[...]) * nw_ref[...] * (1.0 + scale) + shift
    hb = h.astype(BF16)
    proj = lambda lo, hi: jnp.dot(hb, win_ref[:, lo:hi], preferred_element_type=F32)
    q_ref[...] = (proj(0, GLA_DK) * (GLA_DKH ** -0.5)).astype(BF16)
    k_ref[...] = proj(GLA_DK, 2 * GLA_DK).astype(BF16)
    v_ref[...] = proj(2 * GLA_DK, 2 * GLA_DK + GLA_DV).astype(BF16)
    g_ref[...] = proj(2 * GLA_DK + GLA_DV, 2 * GLA_DK + 2 * GLA_DV).astype(BF16)
    t = jnp.dot(hb, wa1_ref[...], preferred_element_type=F32)
    z = _dot(t, wa2_ref[...]) + ba_ref[...]
    lg_ref[...] = _log_sigmoid(z) * (1.0 / GLA_GATE_NORM)


def _gla_front(x, mod, mod_idx, tm, norm_w, w_in, w_a1, w_a2, b_a):
    n, d = x.shape
    nmod = mod.shape[0]
    wa1 = jnp.zeros((d, LANES), F32).at[:, :GLA_GATE_RANK].set(w_a1[0]).at[:, GLA_GATE_RANK:2 * GLA_GATE_RANK].set(w_a1[1])
    wa2 = jnp.zeros((LANES, 2 * GLA_DK), F32).at[:GLA_GATE_RANK, :GLA_DK].set(w_a2[0])
    wa2 = wa2.at[GLA_GATE_RANK:2 * GLA_GATE_RANK, GLA_DK:].set(w_a2[1])
    ba = b_a.reshape(1, 2 * GLA_DK)
    row = lambda i: (i, 0)
    const = lambda i: (0, 0)
    n_in = w_in.shape[1]
    outs = pl.pallas_call(
        _gla_front_kernel,
        grid=(n // tm,),
        in_specs=[pl.BlockSpec((tm, d), row),
                  pl.BlockSpec((1, 1, 3 * d), lambda i: (mod_idx(i), 0, 0)),
                  pl.BlockSpec((1, d), const),
                  pl.BlockSpec((d, n_in), const),
                  pl.BlockSpec((d, LANES), const),
                  pl.BlockSpec((LANES, 2 * GLA_DK), const),
                  pl.BlockSpec((1, 2 * GLA_DK), const)],
        out_specs=[pl.BlockSpec((tm, GLA_DK), row), pl.BlockSpec((tm, GLA_DK), row),
                   pl.BlockSpec((tm, GLA_DV), row), pl.BlockSpec((tm, GLA_DV), row),
                   pl.BlockSpec((tm, 2 * GLA_DK), row)],
        out_shape=[jax.ShapeDtypeStruct((n, GLA_DK), BF16), jax.ShapeDtypeStruct((n, GLA_DK), BF16),
                   jax.ShapeDtypeStruct((n, GLA_DV), BF16), jax.ShapeDtypeStruct((n, GLA_DV), BF16),
                   jax.ShapeDtypeStruct((n, 2 * GLA_DK), F32)],
        compiler_params=_params(1),
        name="gla_front",
    )(x, mod.reshape(nmod, 1, 3 * d), norm_w.reshape(1, d), w_in.astype(BF16), wa1.astype(BF16),
      wa2.astype(BF16), ba)
    return outs


def _gla_chunks(chains):
    flat = [(x, ch[2]) for ch in chains for x in ch[0]]
    c = flat[0][0][0].shape[0]
    dk = flat[0][0][1].shape[1]
    row = lax.broadcasted_iota(jnp.int32, (c, c), 0)
    col = lax.broadcasted_iota(jnp.int32, (c, c), 1)
    keep = [(col >= row) if rv else (col <= row) for _, rv in flat]
    b = [_tri_dot(kp.astype(F32), x[3]) for kp, (x, _) in zip(keep, flat)]
    b_end = [bb[0:1] if rv else bb[c - 1:c] for bb, (_, rv) in zip(b, flat)]
    qb = [(x[0] * jnp.exp(bb)).astype(BF16) for (x, _), bb in zip(flat, b)]
    kb = [x[1] * jnp.exp(-bb) for (x, _), bb in zip(flat, b)]
    kd = [x[1] * jnp.exp(be - bb) for (x, _), bb, be in zip(flat, b, b_end)]
    scores = [jnp.where(kp, _dot_nt(x, y), 0.0).astype(BF16) for kp, x, y in zip(keep, qb, kb)]
    upd = [_dot_tn(x, y[2]) for (y, _), x in zip(flat, kd)]
    decay = [jnp.broadcast_to(jnp.exp(be), (dk, dk)).T for be in b_end]
    n_steps = len(chains[0][0])
    st = [ch[1] for ch in chains]
    outs = [[] for _ in chains]
    for s in range(n_steps):
        at = [ci * n_steps + s for ci in range(len(chains))]
        for ci, f in enumerate(at):
            outs[ci].append(_dot(jnp.concatenate([scores[f], qb[f]], axis=1),
                                 jnp.concatenate([flat[f][0][2], st[ci].astype(BF16)], axis=0)))
        st = [st[ci] * jnp.concatenate([decay[f]] * (st[ci].shape[1] // dk), axis=1) + upd[f]
              for ci, f in enumerate(at)]
    return list(zip(outs, st))


def _gla_scan_kernel(*refs, has_s0, emit_state):
    q_ref, k_ref, v_ref, lf_ref, lb_ref = refs[:5]
    s0_ref = refs[5] if has_s0 else None
    o_ref = refs[5 + has_s0]
    s_ref = refs[6 + has_s0] if emit_state else None
    acc_ref = refs[-1]
    bb, t = q_ref.shape[0], q_ref.shape[1]
    hb = q_ref.shape[2] // GLA_DKH
    nc = t // CHUNK
    half = nc // 2
    kh = lambda h: slice(h * GLA_DKH, (h + 1) * GLA_DKH)
    vh = lambda h: slice(h * GLA_DVH, (h + 1) * GLA_DVH)

    steps = GLA_STEPS if half % GLA_STEPS == 0 else 1

    def both(accumulate):
        def body(i, carry):
            sls = [[pl.ds(pl.multiple_of(ci * CHUNK, CHUNK), CHUNK)
                    for ci in ((i * steps + s, nc - 1 - i * steps - s)[d] for s in range(steps))] for d in range(2)]
            chains = [([(q_ref[e, sl, kh(h)].astype(F32), k_ref[e, sl, kh(h)].astype(F32), v_ref[e, sl, vh(h)],
                         (lf_ref, lb_ref)[d][e, sl, kh(h)]) for sl in sls[d]], carry[n], d == 1)
                      for n, (e, d, h) in enumerate(idx)]
            res = _gla_chunks(chains)
            for n, (e, d, h) in enumerate(idx):
                for sl, o in zip(sls[d], res[n][0]):
                    if accumulate:
                        o_ref[e, sl, vh(h)] = (acc_ref[e, sl, vh(h)] + o).astype(BF16)
                    else:
                        acc_ref[e, sl, vh(h)] = o
            return tuple(st for _, st in res)
        return body

    idx = [(e, d, h) for e in range(bb) for d in range(2) for h in range(hb)]
    if has_s0:
        init = tuple(s0_ref[e, d, h] for e, d, h in idx)
    else:
        init = tuple(jnp.zeros((GLA_DKH, GLA_DVH), F32) for _ in idx)
    carry = lax.fori_loop(0, half // steps, both(False), init)
    carry = lax.fori_loop(half // steps, nc // steps, both(True), carry)
    if emit_state:
        for n, (e, d, h) in enumerate(idx):
            s_ref[e, d, h] = carry[n]


def _gla_scan(q, k, v, lg, s0, bb, hb, emit_state):
    bsz, t, _ = q.shape
    ng = GLA_HEADS // hb
    hd = lambda b, h: (b, 0, h)
    kblk = pl.BlockSpec((bb, t, hb * GLA_DKH), hd)
    vblk = pl.BlockSpec((bb, t, hb * GLA_DVH), hd)
    sblk = pl.BlockSpec((bb, 2, hb, GLA_DKH, GLA_DVH), lambda b, h: (b, 0, h, 0, 0))
    s_shape = jax.ShapeDtypeStruct((bsz, 2, GLA_HEADS, GLA_DKH, GLA_DVH), F32)
    has_s0 = s0 is not None
    return pl.pallas_call(
        functools.partial(_gla_scan_kernel, has_s0=has_s0, emit_state=emit_state),
        grid=(bsz // bb, ng),
        in_specs=[kblk, kblk, vblk, kblk, pl.BlockSpec((bb, t, hb * GLA_DKH), lambda b, h: (b, 0, ng + h))]
        + [sblk] * has_s0,
        out_specs=[vblk] + [sblk] * emit_state,
        out_shape=[jax.ShapeDtypeStruct((bsz, t, GLA_DV), BF16)] + [s_shape] * emit_state,
        scratch_shapes=[pltpu.VMEM((bb, t, hb * GLA_DVH), F32)],
        compiler_params=_params(2),
        name="gla_scan",
    )(q, k, v, lg, lg, *([s0] * has_s0))


def _gla_back_kernel(o_ref, g_ref, x_ref, mod0_ref, mod1_ref, gn_ref, wout_ref, nw1_ref, x1_ref, h1_ref):
    d = D_MODEL
    o = o_ref[...].astype(F32)
    gn = gn_ref[...]
    parts = [_rms(o[:, h * GLA_DVH:(h + 1) * GLA_DVH]) * gn for h in range(GLA_HEADS)]
    on = jnp.concatenate(parts, axis=-1) * _silu(g_ref[...].astype(F32))
    out = _dot(on, wout_ref[...])
    gate = mod0_ref[0][:, 2 * d:]
    x1 = x_ref[...] + gate * out
    x1_ref[...] = x1
    mod1 = mod1_ref[0]
    h1_ref[...] = (_rms(x1) * nw1_ref[...] * (1.0 + mod1[:, d:2 * d]) + mod1[:, :d]).astype(BF16)


def _gla_back(o, g, x, mod0, mod1, mod_idx, tm, gla_norm, w_out, norm_w1):
    n, d = x.shape
    nmod = mod0.shape[0]
    row = lambda i: (i, 0)
    const = lambda i: (0, 0)
    modspec = pl.BlockSpec((1, 1, 3 * d), lambda i: (mod_idx(i), 0, 0))
    return pl.pallas_call(
        _gla_back_kernel,
        grid=(n // tm,),
        in_specs=[pl.BlockSpec((tm, d), row), pl.BlockSpec((tm, d), row), pl.BlockSpec((tm, d), row),
                  modspec, modspec,
                  pl.BlockSpec((1, GLA_DVH), const), pl.BlockSpec((d, d), const), pl.BlockSpec((1, d), const)],
        out_specs=[pl.BlockSpec((tm, d), row), pl.BlockSpec((tm, d), row)],
        out_shape=[jax.ShapeDtypeStruct((n, d), F32), jax.ShapeDtypeStruct((n, d), BF16)],
        compiler_params=_params(1),
        name="gla_back",
    )(o, g, x, mod0.reshape(nmod, 1, 3 * d), mod1.reshape(nmod, 1, 3 * d), gla_norm.reshape(1, GLA_DVH),
      w_out.astype(BF16), norm_w1.reshape(1, d))


def _token_shift(h, up_ref, dn_ref, seq_len, grid_w):
    tm = h.shape[0]
    pos = lax.broadcasted_iota(jnp.int32, (tm, 1), 0) + pl.program_id(0) * tm
    prev = pltpu.roll(h, 1, 0)
    nxt = pltpu.roll(h, tm - 1, 0)
    t = pos % seq_len
    if grid_w is None:
        return 0.5 * (jnp.where(t == 0, 0.0, prev) + jnp.where(t == seq_len - 1, 0.0, nxt))
    col = pos % grid_w
    up = jnp.concatenate([up_ref[...].astype(F32), h[:tm - grid_w]], axis=0)
    dn = jnp.concatenate([h[grid_w:], dn_ref[...].astype(F32)], axis=0)
    up = jnp.where(t < grid_w, 0.0, up)
    dn = jnp.where(t >= seq_len - grid_w, 0.0, dn)
    left = jnp.where(col == 0, 0.0, prev)
    right = jnp.where(col == grid_w - 1, 0.0, nxt)
    return 0.25 * (up + dn + left + right)


def _rwkv_front_kernel(*refs, seq_len, grid_w):
    n_h = 1 if grid_w is None else 3
    h_ref, up_ref, dn_ref = (tuple(refs[:n_h]) + (None, None))[:3]
    (mu_ref, wrkvg_ref, w1_ref, w2_ref, w0_ref, a1_ref, a2_ref, a0_ref,
     r_ref, k_ref, v_ref, g_ref, lw_ref, a_ref) = refs[n_h:]
    h = h_ref[...].astype(F32)
    dh = _token_shift(h, up_ref, dn_ref, seq_len, grid_w) - h
    mix = lambda p: (h + dh * mu_ref[p:p + 1, :]).astype(BF16)
    r_ref[...] = jnp.dot(mix(0), wrkvg_ref[0], preferred_element_type=F32).astype(BF16)
    k_ref[...] = jnp.dot(mix(1), wrkvg_ref[1], preferred_element_type=F32).astype(BF16)
    v_ref[...] = jnp.dot(mix(2), wrkvg_ref[2], preferred_element_type=F32).astype(BF16)
    g_ref[...] = jnp.dot(mix(3), wrkvg_ref[3], preferred_element_type=F32).astype(BF16)
    tw = jnp.tanh(jnp.dot(mix(4), w1_ref[...], preferred_element_type=F32))
    lw_ref[...] = -RWKV_DECAY_SCALE * _sigmoid(w0_ref[...] + _dot(tw, w2_ref[...]))
    ta = jnp.dot(mix(5), a1_ref[...], preferred_element_type=F32)
    a_ref[...] = _sigmoid(a0_ref[...] + _dot(ta, a2_ref[...])).astype(BF16)


def _rwkv_front(h, tm, seq_len, grid_w, mu, w_rkvg, w0, w1, w2, a0, a1, a2):
    n, d = h.shape
    rk = RWKV_RANK
    cat1 = lambda w: jnp.concatenate([w[0], w[1]], axis=1)
    bd2 = lambda w: jnp.zeros((2 * rk, 2 * d), F32).at[:rk, :d].set(w[0]).at[rk:, d:].set(w[1])
    row = lambda i: (i, 0)
    const = lambda i: (0, 0)
    if grid_w is None:
        assert tm % seq_len == 0
        h_specs, h_args = [pl.BlockSpec((tm, d), row)], [h]
    else:
        assert tm % grid_w == 0 and seq_len % tm == 0
        rows_per_tile, last = tm // grid_w, n // grid_w - 1
        h_specs = [pl.BlockSpec((tm, d), row),
                   pl.BlockSpec((grid_w, d), lambda i: (jnp.maximum(i * rows_per_tile - 1, 0), 0)),
                   pl.BlockSpec((grid_w, d), lambda i: (jnp.minimum((i + 1) * rows_per_tile, last), 0))]
        h_args = [h, h, h]
    return pl.pallas_call(
        functools.partial(_rwkv_front_kernel, seq_len=seq_len, grid_w=grid_w),
        grid=(n // tm,),
        in_specs=h_specs + [
            pl.BlockSpec((8, d), const),
            pl.BlockSpec((4, d, d), lambda i: (0, 0, 0)),
            pl.BlockSpec((d, 2 * rk), const), pl.BlockSpec((2 * rk, 2 * d), const), pl.BlockSpec((1, 2 * d), const),
            pl.BlockSpec((d, 2 * rk), const), pl.BlockSpec((2 * rk, 2 * d), const), pl.BlockSpec((1, 2 * d), const)],
        out_specs=[pl.BlockSpec((tm, d), row)] * 4 + [pl.BlockSpec((tm, 2 * d), row)] * 2,
        out_shape=[jax.ShapeDtypeStruct((n, d), BF16)] * 4
        + [jax.ShapeDtypeStruct((n, 2 * d), F32), jax.ShapeDtypeStruct((n, 2 * d), BF16)],
        compiler_params=_params(1),
        name="rwkv_front",
    )(*h_args, jnp.zeros((8, d), F32).at[:6].set(mu), w_rkvg.astype(BF16),
      cat1(w1).astype(BF16), bd2(w2).astype(BF16), w0.reshape(1, 2 * d),
      cat1(a1).astype(BF16), bd2(a2).astype(BF16), a0.reshape(1, 2 * d))


def _seg_sum(x, head0):
    s0 = jnp.sum(jnp.where(head0, x, 0.0), axis=-1, keepdims=True)
    s1 = jnp.sum(jnp.where(head0, 0.0, x), axis=-1, keepdims=True)
    return jnp.where(head0, s0, s1)


def _stack(x, head0):
    return jnp.concatenate([jnp.where(head0, x, 0.0), jnp.where(head0, 0.0, x)], axis=0)


def _rwkv_chunk_terms(insts):
    n = len(insts)
    c = insts[0][0].shape[0]
    lane = lax.broadcasted_iota(jnp.int32, (c, LANES), 1)
    head0 = lane < RWKV_HEAD
    row = lax.broadcasted_iota(jnp.int32, (c, c), 0)
    col = lax.broadcasted_iota(jnp.int32, (c, c), 1)
    ti = lax.broadcasted_iota(jnp.int32, (c, LANES), 0)
    si = lane & (c - 1)
    prow = lax.broadcasted_iota(jnp.int32, (LANES, LANES), 0)
    pcol = lax.broadcasted_iota(jnp.int32, (LANES, LANES), 1)
    same_head = (prow >= RWKV_HEAD) == (pcol >= RWKV_HEAD)
    each = lambda f, *ls: [f(*xs) for xs in zip(*ls)]
    rev = [x[8] for x in insts]
    stk = lambda x: _stack(x, head0).astype(BF16)

    cum = [_tri_dot(((col >= row) if x[8] else (col <= row)).astype(F32), x[3]) for x in insts]
    tot = [cm[0:1] if rv else cm[c - 1:c] for cm, rv in zip(cum, rev)]

    a_t, r_t, b_ts, k_ts, b_h, k_h, v_s, vv, bonus = [], [], [], [], [], [], [], [], []
    for (r, k, v, lw, a, k_k, k_a, r_k, _), cm, tt in zip(insts, cum, tot):
        kk = k * k_k
        kk = kk * lax.rsqrt(jnp.maximum(_seg_sum(kk * kk, head0), 1e-24))
        kd = k * (1.0 + (a - 1.0) * k_a)
        bv = kk * a
        bonus.append(_seg_sum(r * kd * r_k, head0) * v)
        e_neg = jnp.exp(-cm)
        e_tot = jnp.exp(tt)
        b_neg, k_neg = bv * e_neg, kd * e_neg
        a_t.append(-kk * jnp.exp(cm - lw))
        r_t.append(r * jnp.exp(cm))
        b_ts.append(stk(b_neg))
        k_ts.append(stk(k_neg))
        b_h.append(b_neg * e_tot)
        k_h.append(k_neg * e_tot)
        v_s.append(stk(v))
        vv.append(v)

    strict = [(si > ti) if rv else (si < ti) for rv in rev]
    incl = [(si >= ti) if rv else (si <= ti) for rv in rev]
    ar = each(lambda x, y: jnp.concatenate([x, y], axis=0).astype(BF16), a_t, r_t)
    g_b = each(_dot_nt, ar, b_ts)
    g_k = each(_dot_nt, ar, k_ts)
    n_ab = each(lambda m_, x: jnp.where(m_, x[:c], 0.0), strict, g_b)
    l_ak = each(lambda m_, x: jnp.where(m_, x[:c], 0.0), strict, g_k)
    p_rb = each(lambda m_, x: jnp.where(m_, x[c:], 0.0), incl, g_b)
    p_rk = each(lambda m_, x: jnp.where(m_, x[c:], 0.0), incl, g_k)

    same_blk = lambda h: (ti // h) == (si // h)
    n_d = each(lambda x: jnp.where(same_blk(INV_BASE), x, 0.0), n_ab)
    steps = max(1, (INV_BASE - 1).bit_length())
    q = n_d
    tinv = n_d
    q = each(lambda x: _dot(x, stk(x)), q) if steps > 1 else q
    for j in range(1, steps):
        last = j == steps - 1
        qs = each(stk, q)
        if last:
            z = each(_dot, tinv, qs)
            tinv = each(lambda t_, x, y: t_ + x + y, tinv, q, z)
        else:
            both = each(lambda x, t_, w_: _dot(jnp.concatenate([x, t_], axis=0), w_), q, tinv, qs)
            tinv = each(lambda t_, x, y: t_ + x + y[c:], tinv, q, both)
            q = each(lambda y: y[:c], both)
    h = INV_BASE
    while h < c:
        link = same_blk(2 * h) & jnp.logical_not(same_blk(h))
        n_off = each(lambda x: jnp.where(link, x, 0.0), n_ab)
        y = each(lambda t_, x: x + _dot(t_, stk(x)), tinv, n_off)
        tinv = each(lambda t_, x: t_ + x + _dot(x, stk(t_)), tinv, y)
        h *= 2
    a_bar = each(lambda t_, x: x + _dot(t_, stk(x)), tinv, a_t)
    lv = each(lambda l_, p, y: _dot(jnp.concatenate([l_, p], axis=0), y), l_ak, p_rk, v_s)
    w = [x[:c] for x in lv]
    u0 = each(lambda t_, x: x + _dot(t_, stk(x)), tinv, w)
    r_bar = each(lambda x, p, y: x + _dot(p, stk(y)), r_t, p_rb, a_bar)
    y0 = each(lambda p, x, z: _dot(p, stk(x)) + z[c:], p_rb, u0, lv)
    zero = jnp.zeros((c, LANES), F32)
    ms = each(lambda ab, u, v_, bh, kh_: _dot_tn(
        jnp.concatenate([jnp.concatenate([ab, u], axis=1), jnp.concatenate([zero, v_], axis=1)], axis=0),
        jnp.concatenate([bh, kh_], axis=0)), a_bar, u0, vv, b_h, k_h)
    m = [jnp.where(same_head, x[:LANES], 0.0) for x in ms]
    s0p = [jnp.where(same_head, x[LANES:], 0.0) for x in ms]
    return [(r_bar[i], y0[i], m[i], s0p[i], jnp.exp(tot[i]), bonus[i]) for i in range(n)]


def _rwkv_scan_kernel(*refs, has_s0, emit_state):
    (r_ref, k_ref, v_ref, lwf_ref, lwb_ref, af_ref, ab_ref,
     kkf_ref, kkb_ref, kaf_ref, kab_ref, rkf_ref, rkb_ref, lnw_ref, lnb_ref) = refs[:15]
    s0_ref = refs[15] if has_s0 else None
    o_ref = refs[15 + has_s0]
    s_ref = refs[16 + has_s0] if emit_state else None
    rbar_ref, y0_ref, m_ref, s0p_ref, dec_ref, bonus_ref, yf_ref, yb_ref = refs[16 + has_s0 + emit_state:]
    bb, t = r_ref.shape[0], r_ref.shape[1]
    nc = t // CHUNK
    dirs = ((lwf_ref, af_ref, kkf_ref, kaf_ref, rkf_ref, False), (lwb_ref, ab_ref, kkb_ref, kab_ref, rkb_ref, True))
    group = min(bb * nc, PREP_GROUP)
    assert (bb * nc) % group == 0
    chunk_rows = lambda ci: pl.ds(pl.multiple_of(ci * CHUNK, CHUNK), CHUNK)

    def prep(gi, carry):
        insts, where = [], []
        at = [((gi * group + j) // nc, chunk_rows((gi * group + j) % nc)) for j in range(group)]
        for j, (e, sl) in enumerate(at):
            r, k, v = (ref[e, sl, :].astype(F32) for ref in (r_ref, k_ref, v_ref))
            for d, (lw_ref, a_ref, kk_ref, ka_ref, rk_ref, reverse) in enumerate(dirs):
                insts.append((r, k, v, lw_ref[e, sl, :], a_ref[e, sl, :].astype(F32), kk_ref[...], ka_ref[...],
                              rk_ref[...], reverse))
                where.append((d, gi * group + j, e, sl))
        terms = _rwkv_chunk_terms(insts)
        for (d, fi, e, sl), (r_bar, y0, m, s0p, dec, bon) in zip(where, terms):
            rbar_ref[d, fi] = r_bar.astype(BF16)
            y0_ref[d, fi] = y0
            m_ref[d, fi] = m.astype(BF16)
            s0p_ref[d, fi] = s0p
            dec_ref[d, fi] = jnp.broadcast_to(dec, (8, LANES))
        for j in range(group):
            _, _, e, sl = where[2 * j]
            bonus_ref[e, sl, :] = terms[2 * j][5] + terms[2 * j + 1][5]
        return carry

    lax.fori_loop(0, bb * nc // group, prep, 0)

    seq_steps = SEQ_STEPS if nc % SEQ_STEPS == 0 else 1
    idx = [(d, e) for d in range(2) for e in range(bb)]

    def seq(i, carry):
        for s in range(seq_steps):
            cis = (i * seq_steps + s, nc - 1 - i * seq_steps - s)
            sb16 = [st.astype(BF16) for st in carry]
            sm = [jnp.dot(sb16[n], m_ref[d, e * nc + cis[d]], preferred_element_type=F32)
                  for n, (d, e) in enumerate(idx)]
            ys = [_dot_nt(rbar_ref[d, e * nc + cis[d]], sb16[n]) + y0_ref[d, e * nc + cis[d]]
                  for n, (d, e) in enumerate(idx)]
            for n, (d, e) in enumerate(idx):
                (yf_ref, yb_ref)[d][e, chunk_rows(cis[d]), :] = ys[n]
            carry = tuple(carry[n] * dec_ref[d, e * nc + cis[d]][0:1] + sm[n] + s0p_ref[d, e * nc + cis[d]]
                          for n, (d, e) in enumerate(idx))
        return carry

    if has_s0:
        init = tuple(s0_ref[e, d, 0] for d in range(2) for e in range(bb))
    else:
        init = tuple(jnp.zeros((LANES, LANES), F32) for _ in range(2 * bb))
    final = lax.fori_loop(0, nc // seq_steps, seq, init)
    if emit_state:
        for d in range(2):
            for e in range(bb):
                s = final[d * bb + e]
                s_ref[e, d, 0] = s[:RWKV_HEAD, :RWKV_HEAD]
                s_ref[e, d, 1] = pltpu.roll(s, RWKV_HEAD, 1)[RWKV_HEAD:, :RWKV_HEAD]

    prow = lax.broadcasted_iota(jnp.int32, (LANES, LANES), 0)
    pcol = lax.broadcasted_iota(jnp.int32, (LANES, LANES), 1)
    head_ones = ((prow >= RWKV_HEAD) == (pcol >= RWKV_HEAD)).astype(BF16)
    inv_n = 1.0 / RWKV_HEAD

    def head_mean(x):
        hi = x.astype(BF16)
        lo = (x - hi.astype(F32)).astype(BF16)
        both = jnp.dot(jnp.concatenate([hi, lo], axis=0), head_ones, preferred_element_type=F32)
        return (both[:x.shape[0]] + both[x.shape[0]:]) * inv_n

    fgroup = min(bb * nc, FINISH_GROUP)
    assert (bb * nc) % fgroup == 0

    def finish(gi, carry):
        at = [((gi * fgroup + j) // nc, chunk_rows((gi * fgroup + j) % nc)) for j in range(fgroup)]
        y = [yf_ref[e, sl, :] + yb_ref[e, sl, :] for e, sl in at]
        yc = [x - head_mean(x) for x in y]
        var = [head_mean(x * x) for x in yc]
        for (e, sl), x, vr in zip(at, yc, var):
            o_ref[e, sl, :] = (x * lax.rsqrt(vr + LNX_EPS) * lnw_ref[...] + lnb_ref[...]
                               + bonus_ref[e, sl, :]).astype(BF16)
        return carry

    lax.fori_loop(0, bb * nc // fgroup, finish, 0)


def _rwkv_scan(r, k, v, lw, a, k_k, k_a, r_k, lnx_w, lnx_b, s0, bb, emit_state):
    bsz, t, d = r.shape
    nc = t // CHUNK
    seq = lambda b, p: (b, 0, p)
    seq_b = lambda b, p: (b, 0, RWKV_PAIRS + p)
    par = lambda b, p: (0, p)
    par_b = lambda b, p: (0, RWKV_PAIRS + p)
    sblk = pl.BlockSpec((bb, t, LANES), seq)
    sblk_b = pl.BlockSpec((bb, t, LANES), seq_b)
    pblk = pl.BlockSpec((1, LANES), par)
    pblk_b = pl.BlockSpec((1, LANES), par_b)
    s0blk = pl.BlockSpec((bb, 2, 1, LANES, LANES), lambda b, p: (b, 0, p, 0, 0))
    s_blk = pl.BlockSpec((bb, 2, 2, RWKV_HEAD, RWKV_HEAD), lambda b, p: (b, 0, p, 0, 0))
    s_shape = jax.ShapeDtypeStruct((bsz, 2, RWKV_HEADS, RWKV_HEAD, RWKV_HEAD), F32)
    has_s0 = s0 is not None
    k_k, k_a, r_k = (p.reshape(1, 2 * d) for p in (k_k, k_a, r_k))
    return pl.pallas_call(
        functools.partial(_rwkv_scan_kernel, has_s0=has_s0, emit_state=emit_state),
        grid=(bsz // bb, RWKV_PAIRS),
        in_specs=[sblk, sblk, sblk, sblk, sblk_b, sblk, sblk_b,
                  pblk, pblk_b, pblk, pblk_b, pblk, pblk_b, pblk, pblk] + [s0blk] * has_s0,
        out_specs=[sblk] + [s_blk] * emit_state,
        out_shape=[jax.ShapeDtypeStruct((bsz, t, d), BF16)] + [s_shape] * emit_state,
        scratch_shapes=[pltpu.VMEM((2, bb * nc, CHUNK, LANES), BF16), pltpu.VMEM((2, bb * nc, CHUNK, LANES), F32),
                        pltpu.VMEM((2, bb * nc, LANES, LANES), BF16), pltpu.VMEM((2, bb * nc, LANES, LANES), F32),
                        pltpu.VMEM((2, bb * nc, 8, LANES), F32), pltpu.VMEM((bb, t, LANES), F32),
                        pltpu.VMEM((bb, t, LANES), F32), pltpu.VMEM((bb, t, LANES), F32)],
        compiler_params=_params(2),
        name="rwkv_scan",
    )(r, k, v, lw, lw, a, a, k_k, k_k, k_a, k_a, r_k, r_k,
      lnx_w.reshape(1, d), lnx_b.reshape(1, d), *([s0] * has_s0))


def _rwkv_back_kernel(o_ref, g_ref, x_ref, mod_ref, wout_ref, nf_ref, y_ref):
    out = _dot(o_ref[...].astype(F32) * _silu(g_ref[...].astype(F32)), wout_ref[...])
    x2 = x_ref[...] + mod_ref[0][:, 2 * D_MODEL:] * out
    y_ref[...] = _rms(x2) * nf_ref[...]


def _rwkv_back(o, g, x, mod1, mod_idx, tm, w_out, norm_f):
    n, d = x.shape
    nmod = mod1.shape[0]
    row = lambda i: (i, 0)
    const = lambda i: (0, 0)
    return pl.pallas_call(
        _rwkv_back_kernel,
        grid=(n // tm,),
        in_specs=[pl.BlockSpec((tm, d), row), pl.BlockSpec((tm, d), row), pl.BlockSpec((tm, d), row),
                  pl.BlockSpec((1, 1, 3 * d), lambda i: (mod_idx(i), 0, 0)),
                  pl.BlockSpec((d, d), const), pl.BlockSpec((1, d), const)],
        out_specs=pl.BlockSpec((tm, d), row),
        out_shape=jax.ShapeDtypeStruct((n, d), F32),
        compiler_params=_params(1),
        name="rwkv_back",
    )(o, g, x, mod1.reshape(nmod, 1, 3 * d), w_out.astype(BF16), norm_f.reshape(1, d))


def _pair_blockdiag(s):
    b = s.shape[0]
    s = s.reshape(b, 2, RWKV_PAIRS, 2, RWKV_HEAD, RWKV_HEAD)
    z = jnp.zeros_like(s[:, :, :, 0])
    top = jnp.concatenate([s[:, :, :, 0], z], axis=-1)
    bot = jnp.concatenate([z, s[:, :, :, 1]], axis=-1)
    return jnp.concatenate([top, bot], axis=-2)


def _tile_rows(n_seq, t, may_span, rows):
    if t % rows == 0:
        return rows
    if may_span and rows % t == 0 and (n_seq * t) % rows == 0:
        return rows
    return t


def kernel(x_prompt, x_sample, state_gla, state_rwkv, c, c_ctx, w_mod, b_mod, norm_w, gla_w_in, gla_w_a1,
           gla_w_a2, gla_b_a, gla_norm, gla_w_out, rwkv_mu, rwkv_w_rkvg, rwkv_w0, rwkv_w1, rwkv_w2, rwkv_a0,
           rwkv_a1, rwkv_a2, rwkv_k_k, rwkv_k_a, rwkv_r_k, rwkv_lnx_w, rwkv_lnx_b, rwkv_w_out, norm_f):
    d = D_MODEL
    bp, tp, _ = x_prompt.shape
    bs, ts, _ = x_sample.shape
    assert tp % CHUNK == 0 and ts % CHUNK == 0 and ts % GRID_W == 0

    nrows = -(-(1 + bs) // 8) * 8
    cond = jnp.zeros((nrows, d), F32).at[0].set(c_ctx).at[1:1 + bs].set(c)
    cond = cond * (1.0 / (1.0 + jnp.exp(-cond)))
    mod0 = _matmul_bias(cond, w_mod[0], b_mod[0].reshape(1, 3 * d))
    mod1 = _matmul_bias(cond, w_mod[1], b_mod[1].reshape(1, 3 * d))

    def trunk(x3, shared_mod, grid_w, gla_s0, rwkv_s0, gla_bb, rwkv_bb, emit_state):
        b, t, _ = x3.shape
        n = b * t
        tm, tw = (_tile_rows(b, t, shared_mod, rows) for rows in (TOKEN_TILE, WIDE_TILE))
        mod_row = lambda rows: (lambda i: 0) if shared_mod else (lambda i: 1 + i // (t // rows))
        x = x3.reshape(n, d)
        seq = lambda arr: arr.reshape(b, t, arr.shape[-1])
        q, k, v, g, lg = _gla_front(x, mod0, mod_row(tw), tw, norm_w[0], gla_w_in[0], gla_w_a1[0], gla_w_a2[0], gla_b_a[0])
        o, *gla_s = _gla_scan(seq(q), seq(k), seq(v), seq(lg), gla_s0, gla_bb, GLA_HEADS, emit_state)
        x1, h1 = _gla_back(o.reshape(n, d), g, x, mod0, mod1, mod_row(tw), tw, gla_norm[0], gla_w_out[0], norm_w[1])
        r, k, v, g, lw, a = _rwkv_front(h1, tm, t, grid_w, rwkv_mu[0], rwkv_w_rkvg[0], rwkv_w0[0], rwkv_w1[0], rwkv_w2[0],
                                        rwkv_a0[0], rwkv_a1[0], rwkv_a2[0])
        o, *rwkv_s = _rwkv_scan(seq(r), seq(k), seq(v), seq(lw), seq(a), rwkv_k_k[0], rwkv_k_a[0], rwkv_r_k[0],
                                rwkv_lnx_w[0], rwkv_lnx_b[0], rwkv_s0, rwkv_bb, emit_state)
        y = _rwkv_back(o.reshape(n, d), g, x1, mod1, mod_row(tw), tw, rwkv_w_out[0], norm_f)
        return y.reshape(b, t, d), gla_s, rwkv_s

    y_prompt, (gla_s,), (rwkv_s,) = trunk(x_prompt, True, None, None, None, min(bp, 2), min(bp, 8), True)

    y_sample, _, _ = trunk(x_sample, False, GRID_W, state_gla[:, 0], _pair_blockdiag(state_rwkv[:, 0]),
                           1, min(bs, 4), False)
    return (y_prompt, y_sample, gla_s[:, None], rwkv_s[:, None])
```

```python
import functools

import jax
import jax.numpy as jnp
from jax import lax
from jax.experimental import pallas as pl
from jax.experimental.pallas import tpu as pltpu

F32 = jnp.float32
BF16 = jnp.bfloat16

D_MODEL = 1024
EPS = 1e-6
GRID_W = 64
GLA_HEADS = 4
GLA_DK = 512
GLA_DV = 1024
GLA_DKH = 128
GLA_DVH = 256
GLA_GATE_RANK = 16
GLA_GATE_NORM = 16.0
RWKV_HEAD = 64
RWKV_HEADS = 16
RWKV_PAIRS = 8
RWKV_RANK = 64
LNX_EPS = 64e-5
RWKV_DECAY_SCALE = 0.6065306597126334
CHUNK = 64
SEQ_STEPS = 4
GLA_STEPS = 2
INV_BASE = 8
TOKEN_TILE = 512
WIDE_TILE = 1024
FINISH_GROUP = 32
PREP_GROUP = 8
LANES = 128
VMEM_LIMIT = 56 * 1024 * 1024


def _dot(a, b):
    return jnp.dot(a.astype(BF16), b.astype(BF16), preferred_element_type=F32)


def _dot_nt(a, b):
    return lax.dot_general(a.astype(BF16), b.astype(BF16), (((1,), (1,)), ((), ())),
                           preferred_element_type=F32)


def _dot_tn(a, b):
    return lax.dot_general(a.astype(BF16), b.astype(BF16), (((0,), (0,)), ((), ())),
                           preferred_element_type=F32)


def _tri_dot(tri, x):
    hi = x.astype(BF16)
    lo = (x - hi.astype(F32)).astype(BF16)
    t = tri.astype(BF16)
    return jnp.dot(t, hi, preferred_element_type=F32) + jnp.dot(t, lo, preferred_element_type=F32)


def _log_sigmoid(z):
    return jnp.minimum(z, 0.0) - jnp.log(1.0 + jnp.exp(-jnp.abs(z)))


def _sigmoid(z):
    return 1.0 / (1.0 + jnp.exp(-z))


def _silu(z):
    return z * _sigmoid(z)


def _rms(x):
    return x * lax.rsqrt(jnp.mean(x * x, axis=-1, keepdims=True) + EPS)


def _params(n_grid_dims):
    return pltpu.CompilerParams(dimension_semantics=("arbitrary",) * n_grid_dims,
                                vmem_limit_bytes=VMEM_LIMIT)


def _mm_kernel(x_ref, w_ref, b_ref, o_ref):
    o_ref[...] = _dot(x_ref[...], w_ref[...]) + b_ref[...]


def _matmul_bias(x, w, b):
    m, k = x.shape
    n = w.shape[1]
    return pl.pallas_call(
        _mm_kernel,
        grid=(1,),
        in_specs=[pl.BlockSpec((m, k), lambda i: (0, 0)),
                  pl.BlockSpec((k, n), lambda i: (0, 0)),
                  pl.BlockSpec((1, n), lambda i: (0, 0))],
        out_specs=pl.BlockSpec((m, n), lambda i: (0, 0)),
        out_shape=jax.ShapeDtypeStruct((m, n), F32),
        compiler_params=_params(1),
        name="mod_matmul",
    )(x, w.astype(BF16), b)


def _gla_front_kernel(x_ref, mod_ref, nw_ref, win_ref, wa1_ref, wa2_ref, ba_ref,
                      q_ref, k_ref, v_ref, g_ref, lg_ref):
    d = D_MODEL
    mod = mod_ref[0]
    shift, scale = mod[:, :d], mod[:, d:2 * d]
    h = _rms(x_ref[...]) * nw_ref[...] * (1.0 + scale) + shift
    hb = h.astype(BF16)
    proj = lambda lo, hi: jnp.dot(hb, win_ref[:, lo:hi], preferred_element_type=F32)
    q_ref[...] = (proj(0, GLA_DK) * (GLA_DKH ** -0.5)).astype(BF16)
    k_ref[...] = proj(GLA_DK, 2 * GLA_DK).astype(BF16)
    v_ref[...] = proj(2 * GLA_DK, 2 * GLA_DK + GLA_DV).astype(BF16)
    g_ref[...] = proj(2 * GLA_DK + GLA_DV, 2 * GLA_DK + 2 * GLA_DV).astype(BF16)
    t = jnp.dot(hb, wa1_ref[...], preferred_element_type=F32)
    z = _dot(t, wa2_ref[...]) + ba_ref[...]
    lg_ref[...] = _log_sigmoid(z) * (1.0 / GLA_GATE_NORM)


def _gla_front(x, mod, mod_idx, tm, norm_w, w_in, w_a1, w_a2, b_a):
    n, d = x.shape
    nmod = mod.shape[0]
    wa1 = jnp.zeros((d, LANES), F32).at[:, :GLA_GATE_RANK].set(w_a1[0]).at[:, GLA_GATE_RANK:2 * GLA_GATE_RANK].set(w_a1[1])
    wa2 = jnp.zeros((LANES, 2 * GLA_DK), F32).at[:GLA_GATE_RANK, :GLA_DK].set(w_a2[0])
    wa2 = wa2.at[GLA_GATE_RANK:2 * GLA_GATE_RANK, GLA_DK:].set(w_a2[1])
    ba = b_a.reshape(1, 2 * GLA_DK)
    row = lambda i: (i, 0)
    const = lambda i: (0, 0)
    n_in = w_in.shape[1]
    outs = pl.pallas_call(
        _gla_front_kernel,
        grid=(n // tm,),
        in_specs=[pl.BlockSpec((tm, d), row),
                  pl.BlockSpec((1, 1, 3 * d), lambda i: (mod_idx(i), 0, 0)),
                  pl.BlockSpec((1, d), const),
                  pl.BlockSpec((d, n_in), const),
                  pl.BlockSpec((d, LANES), const),
                  pl.BlockSpec((LANES, 2 * GLA_DK), const),
                  pl.BlockSpec((1, 2 * GLA_DK), const)],
        out_specs=[pl.BlockSpec((tm, GLA_DK), row), pl.BlockSpec((tm, GLA_DK), row),
                   pl.BlockSpec((tm, GLA_DV), row), pl.BlockSpec((tm, GLA_DV), row),
                   pl.BlockSpec((tm, 2 * GLA_DK), row)],
        out_shape=[jax.ShapeDtypeStruct((n, GLA_DK), BF16), jax.ShapeDtypeStruct((n, GLA_DK), BF16),
                   jax.ShapeDtypeStruct((n, GLA_DV), BF16), jax.ShapeDtypeStruct((n, GLA_DV), BF16),
                   jax.ShapeDtypeStruct((n, 2 * GLA_DK), F32)],
        compiler_params=_params(1),
        name="gla_front",
    )(x, mod.reshape(nmod, 1, 3 * d), norm_w.reshape(1, d), w_in.astype(BF16), wa1.astype(BF16),
      wa2.astype(BF16), ba)
    return outs


def _gla_chunks(chains):
    flat = [(x, ch[2]) for ch in chains for x in ch[0]]
    c = flat[0][0][0].shape[0]
    dk = flat[0][0][1].shape[1]
    row = lax.broadcasted_iota(jnp.int32, (c, c), 0)
    col = lax.broadcasted_iota(jnp.int32, (c, c), 1)
    keep = [(col >= row) if rv else (col <= row) for _, rv in flat]
    b = [_tri_dot(kp.astype(F32), x[3]) for kp, (x, _) in zip(keep, flat)]
    b_end = [bb[0:1] if rv else bb[c - 1:c] for bb, (_, rv) in zip(b, flat)]
    qb = [(x[0] * jnp.exp(bb)).astype(BF16) for (x, _), bb in zip(flat, b)]
    kb = [x[1] * jnp.exp(-bb) for (x, _), bb in zip(flat, b)]
    kd = [x[1] * jnp.exp(be - bb) for (x, _), bb, be in zip(flat, b, b_end)]
    scores = [jnp.where(kp, _dot_nt(x, y), 0.0).astype(BF16) for kp, x, y in zip(keep, qb, kb)]
    upd = [_dot_tn(x, y[2]) for (y, _), x in zip(flat, kd)]
    decay = [jnp.broadcast_to(jnp.exp(be), (dk, dk)).T for be in b_end]
    n_steps = len(chains[0][0])
    st = [ch[1] for ch in chains]
    outs = [[] for _ in chains]
    for s in range(n_steps):
        at = [ci * n_steps + s for ci in range(len(chains))]
        for ci, f in enumerate(at):
            outs[ci].append(_dot(jnp.concatenate([scores[f], qb[f]], axis=1),
                                 jnp.concatenate([flat[f][0][2], st[ci].astype(BF16)], axis=0)))
        st = [st[ci] * jnp.concatenate([decay[f]] * (st[ci].shape[1] // dk), axis=1) + upd[f]
              for ci, f in enumerate(at)]
    return list(zip(outs, st))


def _gla_scan_kernel(*refs, has_s0, emit_state):
    q_ref, k_ref, v_ref, lf_ref, lb_ref = refs[:5]
    s0_ref = refs[5] if has_s0 else None
    o_ref = refs[5 + has_s0]
    s_ref = refs[6 + has_s0] if emit_state else None
    acc_ref = refs[-1]
    bb, t = q_ref.shape[0], q_ref.shape[1]
    hb = q_ref.shape[2] // GLA_DKH
    nc = t // CHUNK
    half = nc // 2
    kh = lambda h: slice(h * GLA_DKH, (h + 1) * GLA_DKH)
    vh = lambda h: slice(h * GLA_DVH, (h + 1) * GLA_DVH)

    steps = GLA_STEPS if half % GLA_STEPS == 0 else 1

    def both(accumulate):
        def body(i, carry):
            sls = [[pl.ds(pl.multiple_of(ci * CHUNK, CHUNK), CHUNK)
                    for ci in ((i * steps + s, nc - 1 - i * steps - s)[d] for s in range(steps))] for d in range(2)]
            chains = [([(q_ref[e, sl, kh(h)].astype(F32), k_ref[e, sl, kh(h)].astype(F32), v_ref[e, sl, vh(h)],
                         (lf_ref, lb_ref)[d][e, sl, kh(h)]) for sl in sls[d]], carry[n], d == 1)
                      for n, (e, d, h) in enumerate(idx)]
            res = _gla_chunks(chains)
            for n, (e, d, h) in enumerate(idx):
                for sl, o in zip(sls[d], res[n][0]):
                    if accumulate:
                        o_ref[e, sl, vh(h)] = (acc_ref[e, sl, vh(h)] + o).astype(BF16)
                    else:
                        acc_ref[e, sl, vh(h)] = o
            return tuple(st for _, st in res)
        return body

    idx = [(e, d, h) for e in range(bb) for d in range(2) for h in range(hb)]
    if has_s0:
        init = tuple(s0_ref[e, d, h] for e, d, h in idx)
    else:
        init = tuple(jnp.zeros((GLA_DKH, GLA_DVH), F32) for _ in idx)
    carry = lax.fori_loop(0, half // steps, both(False), init)
    carry = lax.fori_loop(half // steps, nc // steps, both(True), carry)
    if emit_state:
        for n, (e, d, h) in enumerate(idx):
            s_ref[e, d, h] = carry[n]


def _gla_scan(q, k, v, lg, s0, bb, hb, emit_state):
    bsz, t, _ = q.shape
    ng = GLA_HEADS // hb
    hd = lambda b, h: (b, 0, h)
    kblk = pl.BlockSpec((bb, t, hb * GLA_DKH), hd)
    vblk = pl.BlockSpec((bb, t, hb * GLA_DVH), hd)
    sblk = pl.BlockSpec((bb, 2, hb, GLA_DKH, GLA_DVH), lambda b, h: (b, 0, h, 0, 0))
    s_shape = jax.ShapeDtypeStruct((bsz, 2, GLA_HEADS, GLA_DKH, GLA_DVH), F32)
    has_s0 = s0 is not None
    return pl.pallas_call(
        functools.partial(_gla_scan_kernel, has_s0=has_s0, emit_state=emit_state),
        grid=(bsz // bb, ng),
        in_specs=[kblk, kblk, vblk, kblk, pl.BlockSpec((bb, t, hb * GLA_DKH), lambda b, h: (b, 0, ng + h))]
        + [sblk] * has_s0,
        out_specs=[vblk] + [sblk] * emit_state,
        out_shape=[jax.ShapeDtypeStruct((bsz, t, GLA_DV), BF16)] + [s_shape] * emit_state,
        scratch_shapes=[pltpu.VMEM((bb, t, hb * GLA_DVH), F32)],
        compiler_params=_params(2),
        name="gla_scan",
    )(q, k, v, lg, lg, *([s0] * has_s0))


def _gla_back_kernel(o_ref, g_ref, x_ref, mod0_ref, mod1_ref, gn_ref, wout_ref, nw1_ref, x1_ref, h1_ref):
    d = D_MODEL
    o = o_ref[...].astype(F32)
    gn = gn_ref[...]
    parts = [_rms(o[:, h * GLA_DVH:(h + 1) * GLA_DVH]) * gn for h in range(GLA_HEADS)]
    on = jnp.concatenate(parts, axis=-1) * _silu(g_ref[...].astype(F32))
    out = _dot(on, wout_ref[...])
    gate = mod0_ref[0][:, 2 * d:]
    x1 = x_ref[...] + gate * out
    x1_ref[...] = x1
    mod1 = mod1_ref[0]
    h1_ref[...] = (_rms(x1) * nw1_ref[...] * (1.0 + mod1[:, d:2 * d]) + mod1[:, :d]).astype(BF16)


def _gla_back(o, g, x, mod0, mod1, mod_idx, tm, gla_norm, w_out, norm_w1):
    n, d = x.shape
    nmod = mod0.shape[0]
    row = lambda i: (i, 0)
    const = lambda i: (0, 0)
    modspec = pl.BlockSpec((1, 1, 3 * d), lambda i: (mod_idx(i), 0, 0))
    return pl.pallas_call(
        _gla_back_kernel,
        grid=(n // tm,),
        in_specs=[pl.BlockSpec((tm, d), row), pl.BlockSpec((tm, d), row), pl.BlockSpec((tm, d), row),
                  modspec, modspec,
                  pl.BlockSpec((1, GLA_DVH), const), pl.BlockSpec((d, d), const), pl.BlockSpec((1, d), const)],
        out_specs=[pl.BlockSpec((tm, d), row), pl.BlockSpec((tm, d), row)],
        out_shape=[jax.ShapeDtypeStruct((n, d), F32), jax.ShapeDtypeStruct((n, d), BF16)],
        compiler_params=_params(1),
        name="gla_back",
    )(o, g, x, mod0.reshape(nmod, 1, 3 * d), mod1.reshape(nmod, 1, 3 * d), gla_norm.reshape(1, GLA_DVH),
      w_out.astype(BF16), norm_w1.reshape(1, d))


def _token_shift(h, up_ref, dn_ref, seq_len, grid_w):
    tm = h.shape[0]
    pos = lax.broadcasted_iota(jnp.int32, (tm, 1), 0) + pl.program_id(0) * tm
    prev = pltpu.roll(h, 1, 0)
    nxt = pltpu.roll(h, tm - 1, 0)
    t = pos % seq_len
    if grid_w is None:
        return 0.5 * (jnp.where(t == 0, 0.0, prev) + jnp.where(t == seq_len - 1, 0.0, nxt))
    col = pos % grid_w
    up = jnp.concatenate([up_ref[...].astype(F32), h[:tm - grid_w]], axis=0)
    dn = jnp.concatenate([h[grid_w:], dn_ref[...].astype(F32)], axis=0)
    up = jnp.where(t < grid_w, 0.0, up)
    dn = jnp.where(t >= seq_len - grid_w, 0.0, dn)
    left = jnp.where(col == 0, 0.0, prev)
    right = jnp.where(col == grid_w - 1, 0.0, nxt)
    return 0.25 * (up + dn + left + right)


def _rwkv_front_kernel(*refs, seq_len, grid_w):
    n_h = 1 if grid_w is None else 3
    h_ref, up_ref, dn_ref = (tuple(refs[:n_h]) + (None, None))[:3]
    (mu_ref, wrkvg_ref, w1_ref, w2_ref, w0_ref, a1_ref, a2_ref, a0_ref,
     r_ref, k_ref, v_ref, g_ref, lw_ref, a_ref) = refs[n_h:]
    h = h_ref[...].astype(F32)
    dh = _token_shift(h, up_ref, dn_ref, seq_len, grid_w) - h
    mix = lambda p: (h + dh * mu_ref[p:p + 1, :]).astype(BF16)
    r_ref[...] = jnp.dot(mix(0), wrkvg_ref[0], preferred_element_type=F32).astype(BF16)
    k_ref[...] = jnp.dot(mix(1), wrkvg_ref[1], preferred_element_type=F32).astype(BF16)
    v_ref[...] = jnp.dot(mix(2), wrkvg_ref[2], preferred_element_type=F32).astype(BF16)
    g_ref[...] = jnp.dot(mix(3), wrkvg_ref[3], preferred_element_type=F32).astype(BF16)
    tw = jnp.tanh(jnp.dot(mix(4), w1_ref[...], preferred_element_type=F32))
    lw_ref[...] = -RWKV_DECAY_SCALE * _sigmoid(w0_ref[...] + _dot(tw, w2_ref[...]))
    ta = jnp.dot(mix(5), a1_ref[...], preferred_element_type=F32)
    a_ref[...] = _sigmoid(a0_ref[...] + _dot(ta, a2_ref[...])).astype(BF16)


def _rwkv_front(h, tm, seq_len, grid_w, mu, w_rkvg, w0, w1, w2, a0, a1, a2):
    n, d = h.shape
    rk = RWKV_RANK
    cat1 = lambda w: jnp.concatenate([w[0], w[1]], axis=1)
    bd2 = lambda w: jnp.zeros((2 * rk, 2 * d), F32).at[:rk, :d].set(w[0]).at[rk:, d:].set(w[1])
    row = lambda i: (i, 0)
    const = lambda i: (0, 0)
    if grid_w is None:
        assert tm % seq_len == 0
        h_specs, h_args = [pl.BlockSpec((tm, d), row)], [h]
    else:
        assert tm % grid_w == 0 and seq_len % tm == 0
        rows_per_tile, last = tm // grid_w, n // grid_w - 1
        h_specs = [pl.BlockSpec((tm, d), row),
                   pl.BlockSpec((grid_w, d), lambda i: (jnp.maximum(i * rows_per_tile - 1, 0), 0)),
                   pl.BlockSpec((grid_w, d), lambda i: (jnp.minimum((i + 1) * rows_per_tile, last), 0))]
        h_args = [h, h, h]
    return pl.pallas_call(
        functools.partial(_rwkv_front_kernel, seq_len=seq_len, grid_w=grid_w),
        grid=(n // tm,),
        in_specs=h_specs + [
            pl.BlockSpec((8, d), const),
            pl.BlockSpec((4, d, d), lambda i: (0, 0, 0)),
            pl.BlockSpec((d, 2 * rk), const), pl.BlockSpec((2 * rk, 2 * d), const), pl.BlockSpec((1, 2 * d), const),
            pl.BlockSpec((d, 2 * rk), const), pl.BlockSpec((2 * rk, 2 * d), const), pl.BlockSpec((1, 2 * d), const)],
        out_specs=[pl.BlockSpec((tm, d), row)] * 4 + [pl.BlockSpec((tm, 2 * d), row)] * 2,
        out_shape=[jax.ShapeDtypeStruct((n, d), BF16)] * 4
        + [jax.ShapeDtypeStruct((n, 2 * d), F32), jax.ShapeDtypeStruct((n, 2 * d), BF16)],
        compiler_params=_params(1),
        name="rwkv_front",
    )(*h_args, jnp.zeros((8, d), F32).at[:6].set(mu), w_rkvg.astype(BF16),
      cat1(w1).astype(BF16), bd2(w2).astype(BF16), w0.reshape(1, 2 * d),
      cat1(a1).astype(BF16), bd2(a2).astype(BF16), a0.reshape(1, 2 * d))


def _seg_sum(x, head0):
    s0 = jnp.sum(jnp.where(head0, x, 0.0), axis=-1, keepdims=True)
    s1 = jnp.sum(jnp.where(head0, 0.0, x), axis=-1, keepdims=True)
    return jnp.where(head0, s0, s1)


def _stack(x, head0):
    return jnp.concatenate([jnp.where(head0, x, 0.0), jnp.where(head0, 0.0, x)], axis=0)


def _rwkv_chunk_terms(insts):
    n = len(insts)
    c = insts[0][0].shape[0]
    lane = lax.broadcasted_iota(jnp.int32, (c, LANES), 1)
    head0 = lane < RWKV_HEAD
    row = lax.broadcasted_iota(jnp.int32, (c, c), 0)
    col = lax.broadcasted_iota(jnp.int32, (c, c), 1)
    ti = lax.broadcasted_iota(jnp.int32, (c, LANES), 0)
    si = lane & (c - 1)
    prow = lax.broadcasted_iota(jnp.int32, (LANES, LANES), 0)
    pcol = lax.broadcasted_iota(jnp.int32, (LANES, LANES), 1)
    same_head = (prow >= RWKV_HEAD) == (pcol >= RWKV_HEAD)
    each = lambda f, *ls: [f(*xs) for xs in zip(*ls)]
    rev = [x[8] for x in insts]
    stk = lambda x: _stack(x, head0).astype(BF16)

    cum = [_tri_dot(((col >= row) if x[8] else (col <= row)).astype(F32), x[3]) for x in insts]
    tot = [cm[0:1] if rv else cm[c - 1:c] for cm, rv in zip(cum, rev)]

    a_t, r_t, b_ts, k_ts, b_h, k_h, v_s, vv, bonus = [], [], [], [], [], [], [], [], []
    for (r, k, v, lw, a, k_k, k_a, r_k, _), cm, tt in zip(insts, cum, tot):
        kk = k * k_k
        kk = kk * lax.rsqrt(jnp.maximum(_seg_sum(kk * kk, head0), 1e-24))
        kd = k * (1.0 + (a - 1.0) * k_a)
        bv = kk * a
        bonus.append(_seg_sum(r * kd * r_k, head0) * v)
        e_neg = jnp.exp(-cm)
        e_tot = jnp.exp(tt)
        b_neg, k_neg = bv * e_neg, kd * e_neg
        a_t.append(-kk * jnp.exp(cm - lw))
        r_t.append(r * jnp.exp(cm))
        b_ts.append(stk(b_neg))
        k_ts.append(stk(k_neg))
        b_h.append(b_neg * e_tot)
        k_h.append(k_neg * e_tot)
        v_s.append(stk(v))
        vv.append(v)

    strict = [(si > ti) if rv else (si < ti) for rv in rev]
    incl = [(si >= ti) if rv else (si <= ti) for rv in rev]
    ar = each(lambda x, y: jnp.concatenate([x, y], axis=0).astype(BF16), a_t, r_t)
    g_b = each(_dot_nt, ar, b_ts)
    g_k = each(_dot_nt, ar, k_ts)
    n_ab = each(lambda m_, x: jnp.where(m_, x[:c], 0.0), strict, g_b)
    l_ak = each(lambda m_, x: jnp.where(m_, x[:c], 0.0), strict, g_k)
    p_rb = each(lambda m_, x: jnp.where(m_, x[c:], 0.0), incl, g_b)
    p_rk = each(lambda m_, x: jnp.where(m_, x[c:], 0.0), incl, g_k)

    same_blk = lambda h: (ti // h) == (si // h)
    n_d = each(lambda x: jnp.where(same_blk(INV_BASE), x, 0.0), n_ab)
    steps = max(1, (INV_BASE - 1).bit_length())
    q = n_d
    tinv = n_d
    q = each(lambda x: _dot(x, stk(x)), q) if steps > 1 else q
    for j in range(1, steps):
        last = j == steps - 1
        qs = each(stk, q)
        if last:
            z = each(_dot, tinv, qs)
            tinv = each(lambda t_, x, y: t_ + x + y, tinv, q, z)
        else:
            both = each(lambda x, t_, w_: _dot(jnp.concatenate([x, t_], axis=0), w_), q, tinv, qs)
            tinv = each(lambda t_, x, y: t_ + x + y[c:], tinv, q, both)
            q = each(lambda y: y[:c], both)
    h = INV_BASE
    while h < c:
        link = same_blk(2 * h) & jnp.logical_not(same_blk(h))
        n_off = each(lambda x: jnp.where(link, x, 0.0), n_ab)
        y = each(lambda t_, x: x + _dot(t_, stk(x)), tinv, n_off)
        tinv = each(lambda t_, x: t_ + x + _dot(x, stk(t_)), tinv, y)
        h *= 2
    a_bar = each(lambda t_, x: x + _dot(t_, stk(x)), tinv, a_t)
    lv = each(lambda l_, p, y: _dot(jnp.concatenate([l_, p], axis=0), y), l_ak, p_rk, v_s)
    w = [x[:c] for x in lv]
    u0 = each(lambda t_, x: x + _dot(t_, stk(x)), tinv, w)
    r_bar = each(lambda x, p, y: x + _dot(p, stk(y)), r_t, p_rb, a_bar)
    y0 = each(lambda p, x, z: _dot(p, stk(x)) + z[c:], p_rb, u0, lv)
    zero = jnp.zeros((c, LANES), F32)
    ms = each(lambda ab, u, v_, bh, kh_: _dot_tn(
        jnp.concatenate([jnp.concatenate([ab, u], axis=1), jnp.concatenate([zero, v_], axis=1)], axis=0),
        jnp.concatenate([bh, kh_], axis=0)), a_bar, u0, vv, b_h, k_h)
    m = [jnp.where(same_head, x[:LANES], 0.0) for x in ms]
    s0p = [jnp.where(same_head, x[LANES:], 0.0) for x in ms]
    return [(r_bar[i], y0[i], m[i], s0p[i], jnp.exp(tot[i]), bonus[i]) for i in range(n)]


def _rwkv_scan_kernel(*refs, has_s0, emit_state):
    (r_ref, k_ref, v_ref, lwf_ref, lwb_ref, af_ref, ab_ref,
     kkf_ref, kkb_ref, kaf_ref, kab_ref, rkf_ref, rkb_ref, lnw_ref, lnb_ref) = refs[:15]
    s0_ref = refs[15] if has_s0 else None
    o_ref = refs[15 + has_s0]
    s_ref = refs[16 + has_s0] if emit_state else None
    w_ref, y0_ref, s0p_ref, dec_ref, bonus_ref, yf_ref, yb_ref = refs[16 + has_s0 + emit_state:]
    bb, t = r_ref.shape[0], r_ref.shape[1]
    nc = t // CHUNK
    dirs = ((lwf_ref, af_ref, kkf_ref, kaf_ref, rkf_ref, False), (lwb_ref, ab_ref, kkb_ref, kab_ref, rkb_ref, True))
    group = min(bb * nc, PREP_GROUP)
    assert (bb * nc) % group == 0
    chunk_rows = lambda ci: pl.ds(pl.multiple_of(ci * CHUNK, CHUNK), CHUNK)

    def prep(gi, carry):
        insts, where = [], []
        at = [((gi * group + j) // nc, chunk_rows((gi * group + j) % nc)) for j in range(group)]
        for j, (e, sl) in enumerate(at):
            r, k, v = (ref[e, sl, :].astype(F32) for ref in (r_ref, k_ref, v_ref))
            for d, (lw_ref, a_ref, kk_ref, ka_ref, rk_ref, reverse) in enumerate(dirs):
                insts.append((r, k, v, lw_ref[e, sl, :], a_ref[e, sl, :].astype(F32), kk_ref[...], ka_ref[...],
                              rk_ref[...], reverse))
                where.append((d, gi * group + j, e, sl))
        terms = _rwkv_chunk_terms(insts)
        for (d, fi, e, sl), (r_bar, y0, m, s0p, dec, bon) in zip(where, terms):
            rbar_t = jnp.concatenate([r_bar, jnp.zeros_like(r_bar)], axis=0).T
            w_ref[d, fi] = jnp.concatenate([m, rbar_t], axis=1).astype(BF16)
            y0_ref[d, fi] = y0
            s0p_ref[d, fi] = s0p
            dec_ref[d, fi] = jnp.broadcast_to(dec, (8, LANES))
        for j in range(group):
            _, _, e, sl = where[2 * j]
            bonus_ref[e, sl, :] = terms[2 * j][5] + terms[2 * j + 1][5]
        return carry

    lax.fori_loop(0, bb * nc // group, prep, 0)

    seq_steps = SEQ_STEPS if nc % SEQ_STEPS == 0 else 1
    idx = [(d, e) for d in range(2) for e in range(bb)]

    def seq(i, carry):
        for s in range(seq_steps):
            cis = (i * seq_steps + s, nc - 1 - i * seq_steps - s)
            sw = [jnp.dot(carry[n].astype(BF16), w_ref[d, e * nc + cis[d]], preferred_element_type=F32)
                  for n, (d, e) in enumerate(idx)]
            for n, (d, e) in enumerate(idx):
                y = sw[n][:, LANES:].T[:CHUNK] + y0_ref[d, e * nc + cis[d]]
                (yf_ref, yb_ref)[d][e, chunk_rows(cis[d]), :] = y
            carry = tuple(carry[n] * dec_ref[d, e * nc + cis[d]][0:1] + sw[n][:, :LANES] + s0p_ref[d, e * nc + cis[d]]
                          for n, (d, e) in enumerate(idx))
        return carry

    if has_s0:
        init = tuple(s0_ref[e, d, 0] for d in range(2) for e in range(bb))
    else:
        init = tuple(jnp.zeros((LANES, LANES), F32) for _ in range(2 * bb))
    final = lax.fori_loop(0, nc // seq_steps, seq, init)
    if emit_state:
        for d in range(2):
            for e in range(bb):
                s = final[d * bb + e]
                s_ref[e, d, 0] = s[:RWKV_HEAD, :RWKV_HEAD]
                s_ref[e, d, 1] = pltpu.roll(s, RWKV_HEAD, 1)[RWKV_HEAD:, :RWKV_HEAD]

    prow = lax.broadcasted_iota(jnp.int32, (LANES, LANES), 0)
    pcol = lax.broadcasted_iota(jnp.int32, (LANES, LANES), 1)
    head_ones = ((prow >= RWKV_HEAD) == (pcol >= RWKV_HEAD)).astype(BF16)
    inv_n = 1.0 / RWKV_HEAD

    def head_mean(x):
        hi = x.astype(BF16)
        lo = (x - hi.astype(F32)).astype(BF16)
        both = jnp.dot(jnp.concatenate([hi, lo], axis=0), head_ones, preferred_element_type=F32)
        return (both[:x.shape[0]] + both[x.shape[0]:]) * inv_n

    fgroup = min(bb * nc, FINISH_GROUP)
    assert (bb * nc) % fgroup == 0

    def finish(gi, carry):
        at = [((gi * fgroup + j) // nc, chunk_rows((gi * fgroup + j) % nc)) for j in range(fgroup)]
        y = [yf_ref[e, sl, :] + yb_ref[e, sl, :] for e, sl in at]
        yc = [x - head_mean(x) for x in y]
        var = [head_mean(x * x) for x in yc]
        for (e, sl), x, vr in zip(at, yc, var):
            o_ref[e, sl, :] = (x * lax.rsqrt(vr + LNX_EPS) * lnw_ref[...] + lnb_ref[...]
                               + bonus_ref[e, sl, :]).astype(BF16)
        return carry

    lax.fori_loop(0, bb * nc // fgroup, finish, 0)


def _rwkv_scan(r, k, v, lw, a, k_k, k_a, r_k, lnx_w, lnx_b, s0, bb, emit_state):
    bsz, t, d = r.shape
    nc = t // CHUNK
    seq = lambda b, p: (b, 0, p)
    seq_b = lambda b, p: (b, 0, RWKV_PAIRS + p)
    par = lambda b, p: (0, p)
    par_b = lambda b, p: (0, RWKV_PAIRS + p)
    sblk = pl.BlockSpec((bb, t, LANES), seq)
    sblk_b = pl.BlockSpec((bb, t, LANES), seq_b)
    pblk = pl.BlockSpec((1, LANES), par)
    pblk_b = pl.BlockSpec((1, LANES), par_b)
    s0blk = pl.BlockSpec((bb, 2, 1, LANES, LANES), lambda b, p: (b, 0, p, 0, 0))
    s_blk = pl.BlockSpec((bb, 2, 2, RWKV_HEAD, RWKV_HEAD), lambda b, p: (b, 0, p, 0, 0))
    s_shape = jax.ShapeDtypeStruct((bsz, 2, RWKV_HEADS, RWKV_HEAD, RWKV_HEAD), F32)
    has_s0 = s0 is not None
    k_k, k_a, r_k = (p.reshape(1, 2 * d) for p in (k_k, k_a, r_k))
    return pl.pallas_call(
        functools.partial(_rwkv_scan_kernel, has_s0=has_s0, emit_state=emit_state),
        grid=(bsz // bb, RWKV_PAIRS),
        in_specs=[sblk, sblk, sblk, sblk, sblk_b, sblk, sblk_b,
                  pblk, pblk_b, pblk, pblk_b, pblk, pblk_b, pblk, pblk] + [s0blk] * has_s0,
        out_specs=[sblk] + [s_blk] * emit_state,
        out_shape=[jax.ShapeDtypeStruct((bsz, t, d), BF16)] + [s_shape] * emit_state,
        scratch_shapes=[pltpu.VMEM((2, bb * nc, LANES, 2 * LANES), BF16), pltpu.VMEM((2, bb * nc, CHUNK, LANES), F32),
                        pltpu.VMEM((2, bb * nc, LANES, LANES), F32),
                        pltpu.VMEM((2, bb * nc, 8, LANES), F32), pltpu.VMEM((bb, t, LANES), F32),
                        pltpu.VMEM((bb, t, LANES), F32), pltpu.VMEM((bb, t, LANES), F32)],
        compiler_params=_params(2),
        name="rwkv_scan",
    )(r, k, v, lw, lw, a, a, k_k, k_k, k_a, k_a, r_k, r_k,
      lnx_w.reshape(1, d), lnx_b.reshape(1, d), *([s0] * has_s0))


def _rwkv_back_kernel(o_ref, g_ref, x_ref, mod_ref, wout_ref, nf_ref, y_ref):
    out = _dot(o_ref[...].astype(F32) * _silu(g_ref[...].astype(F32)), wout_ref[...])
    x2 = x_ref[...] + mod_ref[0][:, 2 * D_MODEL:] * out
    y_ref[...] = _rms(x2) * nf_ref[...]


def _rwkv_back(o, g, x, mod1, mod_idx, tm, w_out, norm_f):
    n, d = x.shape
    nmod = mod1.shape[0]
    row = lambda i: (i, 0)
    const = lambda i: (0, 0)
    return pl.pallas_call(
        _rwkv_back_kernel,
        grid=(n // tm,),
        in_specs=[pl.BlockSpec((tm, d), row), pl.BlockSpec((tm, d), row), pl.BlockSpec((tm, d), row),
                  pl.BlockSpec((1, 1, 3 * d), lambda i: (mod_idx(i), 0, 0)),
                  pl.BlockSpec((d, d), const), pl.BlockSpec((1, d), const)],
        out_specs=pl.BlockSpec((tm, d), row),
        out_shape=jax.ShapeDtypeStruct((n, d), F32),
        compiler_params=_params(1),
        name="rwkv_back",
    )(o, g, x, mod1.reshape(nmod, 1, 3 * d), w_out.astype(BF16), norm_f.reshape(1, d))


def _pair_blockdiag(s):
    b = s.shape[0]
    s = s.reshape(b, 2, RWKV_PAIRS, 2, RWKV_HEAD, RWKV_HEAD)
    z = jnp.zeros_like(s[:, :, :, 0])
    top = jnp.concatenate([s[:, :, :, 0], z], axis=-1)
    bot = jnp.concatenate([z, s[:, :, :, 1]], axis=-1)
    return jnp.concatenate([top, bot], axis=-2)


def _tile_rows(n_seq, t, may_span, rows):
    if t % rows == 0:
        return rows
    if may_span and rows % t == 0 and (n_seq * t) % rows == 0:
        return rows
    return t


def kernel(x_prompt, x_sample, state_gla, state_rwkv, c, c_ctx, w_mod, b_mod, norm_w, gla_w_in, gla_w_a1,
           gla_w_a2, gla_b_a, gla_norm, gla_w_out, rwkv_mu, rwkv_w_rkvg, rwkv_w0, rwkv_w1, rwkv_w2, rwkv_a0,
           rwkv_a1, rwkv_a2, rwkv_k_k, rwkv_k_a, rwkv_r_k, rwkv_lnx_w, rwkv_lnx_b, rwkv_w_out, norm_f):
    d = D_MODEL
    bp, tp, _ = x_prompt.shape
    bs, ts, _ = x_sample.shape
    assert tp % CHUNK == 0 and ts % CHUNK == 0 and ts % GRID_W == 0

    nrows = -(-(1 + bs) // 8) * 8
    cond = jnp.zeros((nrows, d), F32).at[0].set(c_ctx).at[1:1 + bs].set(c)
    cond = cond * (1.0 / (1.0 + jnp.exp(-cond)))
    mod0 = _matmul_bias(cond, w_mod[0], b_mod[0].reshape(1, 3 * d))
    mod1 = _matmul_bias(cond, w_mod[1], b_mod[1].reshape(1, 3 * d))

    def trunk(x3, shared_mod, grid_w, gla_s0, rwkv_s0, gla_bb, rwkv_bb, emit_state):
        b, t, _ = x3.shape
        n = b * t
        tm, tw = (_tile_rows(b, t, shared_mod, rows) for rows in (TOKEN_TILE, WIDE_TILE))
        mod_row = lambda rows: (lambda i: 0) if shared_mod else (lambda i: 1 + i // (t // rows))
        x = x3.reshape(n, d)
        seq = lambda arr: arr.reshape(b, t, arr.shape[-1])
        q, k, v, g, lg = _gla_front(x, mod0, mod_row(tw), tw, norm_w[0], gla_w_in[0], gla_w_a1[0], gla_w_a2[0], gla_b_a[0])
        o, *gla_s = _gla_scan(seq(q), seq(k), seq(v), seq(lg), gla_s0, gla_bb, GLA_HEADS, emit_state)
        x1, h1 = _gla_back(o.reshape(n, d), g, x, mod0, mod1, mod_row(tw), tw, gla_norm[0], gla_w_out[0], norm_w[1])
        r, k, v, g, lw, a = _rwkv_front(h1, tm, t, grid_w, rwkv_mu[0], rwkv_w_rkvg[0], rwkv_w0[0], rwkv_w1[0], rwkv_w2[0],
                                        rwkv_a0[0], rwkv_a1[0], rwkv_a2[0])
        o, *rwkv_s = _rwkv_scan(seq(r), seq(k), seq(v), seq(lw), seq(a), rwkv_k_k[0], rwkv_k_a[0], rwkv_r_k[0],
                                rwkv_lnx_w[0], rwkv_lnx_b[0], rwkv_s0, rwkv_bb, emit_state)
        y = _rwkv_back(o.reshape(n, d), g, x1, mod1, mod_row(tw), tw, rwkv_w_out[0], norm_f)
        return y.reshape(b, t, d), gla_s, rwkv_s

    y_prompt, (gla_s,), (rwkv_s,) = trunk(x_prompt, True, None, None, None, min(bp, 2), min(bp, 8), True)

    y_sample, _, _ = trunk(x_sample, False, GRID_W, state_gla[:, 0], _pair_blockdiag(state_rwkv[:, 0]),
                           1, min(bs, 4), False)
    return (y_prompt, y_sample, gla_s[:, None], rwkv_s[:, None])
```

```python
import functools

import jax
import jax.numpy as jnp
from jax import lax
from jax.experimental import pallas as pl
from jax.experimental.pallas import tpu as pltpu

F32 = jnp.float32
BF16 = jnp.bfloat16

D_MODEL = 1024
EPS = 1e-6
GRID_W = 64
GLA_HEADS = 4
GLA_DK = 512
GLA_DV = 1024
GLA_DKH = 128
GLA_DVH = 256
GLA_GATE_RANK = 16
GLA_GATE_NORM = 16.0
RWKV_HEAD = 64
RWKV_HEADS = 16
RWKV_PAIRS = 8
RWKV_RANK = 64
LNX_EPS = 64e-5
RWKV_DECAY_SCALE = 0.6065306597126334
CHUNK = 64
SEQ_STEPS = 4
GLA_STEPS = 4
INV_BASE = 8
TOKEN_TILE = 512
WIDE_TILE = 1024
FINISH_GROUP = 32
PREP_GROUP = 8
LANES = 128
VMEM_LIMIT = 56 * 1024 * 1024


def _dot(a, b):
    return jnp.dot(a.astype(BF16), b.astype(BF16), preferred_element_type=F32)


def _dot_nt(a, b):
    return lax.dot_general(a.astype(BF16), b.astype(BF16), (((1,), (1,)), ((), ())),
                           preferred_element_type=F32)


def _dot_tn(a, b):
    return lax.dot_general(a.astype(BF16), b.astype(BF16), (((0,), (0,)), ((), ())),
                           preferred_element_type=F32)


def _tri_dot(tri, x):
    hi = x.astype(BF16)
    lo = (x - hi.astype(F32)).astype(BF16)
    t = tri.astype(BF16)
    return jnp.dot(t, hi, preferred_element_type=F32) + jnp.dot(t, lo, preferred_element_type=F32)


def _log_sigmoid(z):
    return jnp.minimum(z, 0.0) - jnp.log(1.0 + jnp.exp(-jnp.abs(z)))


def _sigmoid(z):
    return 1.0 / (1.0 + jnp.exp(-z))


def _silu(z):
    return z * _sigmoid(z)


def _rms(x):
    return x * lax.rsqrt(jnp.mean(x * x, axis=-1, keepdims=True) + EPS)


def _params(n_grid_dims):
    return pltpu.CompilerParams(dimension_semantics=("arbitrary",) * n_grid_dims,
                                vmem_limit_bytes=VMEM_LIMIT)


def _mm_kernel(x_ref, w_ref, b_ref, o_ref):
    o_ref[...] = _dot(x_ref[...], w_ref[...]) + b_ref[...]


def _matmul_bias(x, w, b):
    m, k = x.shape
    n = w.shape[1]
    return pl.pallas_call(
        _mm_kernel,
        grid=(1,),
        in_specs=[pl.BlockSpec((m, k), lambda i: (0, 0)),
                  pl.BlockSpec((k, n), lambda i: (0, 0)),
                  pl.BlockSpec((1, n), lambda i: (0, 0))],
        out_specs=pl.BlockSpec((m, n), lambda i: (0, 0)),
        out_shape=jax.ShapeDtypeStruct((m, n), F32),
        compiler_params=_params(1),
        name="mod_matmul",
    )(x, w.astype(BF16), b)


def _gla_front_kernel(x_ref, mod_ref, nw_ref, win_ref, wa1_ref, wa2_ref, ba_ref,
                      q_ref, k_ref, v_ref, g_ref, lg_ref):
    d = D_MODEL
    mod = mod_ref[0]
    shift, scale = mod[:, :d], mod[:, d:2 * d]
    h = _rms(x_ref[...]) * nw_ref[...] * (1.0 + scale) + shift
    hb = h.astype(BF16)
    proj = lambda lo, hi: jnp.dot(hb, win_ref[:, lo:hi], preferred_element_type=F32)
    q_ref[...] = (proj(0, GLA_DK) * (GLA_DKH ** -0.5)).astype(BF16)
    k_ref[...] = proj(GLA_DK, 2 * GLA_DK).astype(BF16)
    v_ref[...] = proj(2 * GLA_DK, 2 * GLA_DK + GLA_DV).astype(BF16)
    g_ref[...] = proj(2 * GLA_DK + GLA_DV, 2 * GLA_DK + 2 * GLA_DV).astype(BF16)
    t = jnp.dot(hb, wa1_ref[...], preferred_element_type=F32)
    z = _dot(t, wa2_ref[...]) + ba_ref[...]
    lg_ref[...] = _log_sigmoid(z) * (1.0 / GLA_GATE_NORM)


def _gla_front(x, mod, mod_idx, tm, norm_w, w_in, w_a1, w_a2, b_a):
    n, d = x.shape
    nmod = mod.shape[0]
    wa1 = jnp.zeros((d, LANES), F32).at[:, :GLA_GATE_RANK].set(w_a1[0]).at[:, GLA_GATE_RANK:2 * GLA_GATE_RANK].set(w_a1[1])
    wa2 = jnp.zeros((LANES, 2 * GLA_DK), F32).at[:GLA_GATE_RANK, :GLA_DK].set(w_a2[0])
    wa2 = wa2.at[GLA_GATE_RANK:2 * GLA_GATE_RANK, GLA_DK:].set(w_a2[1])
    ba = b_a.reshape(1, 2 * GLA_DK)
    row = lambda i: (i, 0)
    const = lambda i: (0, 0)
    n_in = w_in.shape[1]
    outs = pl.pallas_call(
        _gla_front_kernel,
        grid=(n // tm,),
        in_specs=[pl.BlockSpec((tm, d), row),
                  pl.BlockSpec((1, 1, 3 * d), lambda i: (mod_idx(i), 0, 0)),
                  pl.BlockSpec((1, d), const),
                  pl.BlockSpec((d, n_in), const),
                  pl.BlockSpec((d, LANES), const),
                  pl.BlockSpec((LANES, 2 * GLA_DK), const),
                  pl.BlockSpec((1, 2 * GLA_DK), const)],
        out_specs=[pl.BlockSpec((tm, GLA_DK), row), pl.BlockSpec((tm, GLA_DK), row),
                   pl.BlockSpec((tm, GLA_DV), row), pl.BlockSpec((tm, GLA_DV), row),
                   pl.BlockSpec((tm, 2 * GLA_DK), row)],
        out_shape=[jax.ShapeDtypeStruct((n, GLA_DK), BF16), jax.ShapeDtypeStruct((n, GLA_DK), BF16),
                   jax.ShapeDtypeStruct((n, GLA_DV), BF16), jax.ShapeDtypeStruct((n, GLA_DV), BF16),
                   jax.ShapeDtypeStruct((n, 2 * GLA_DK), F32)],
        compiler_params=_params(1),
        name="gla_front",
    )(x, mod.reshape(nmod, 1, 3 * d), norm_w.reshape(1, d), w_in.astype(BF16), wa1.astype(BF16),
      wa2.astype(BF16), ba)
    return outs


def _gla_chunks(chains):
    flat = [(x, ch[2]) for ch in chains for x in ch[0]]
    c = flat[0][0][0].shape[0]
    dk = flat[0][0][1].shape[1]
    row = lax.broadcasted_iota(jnp.int32, (c, c), 0)
    col = lax.broadcasted_iota(jnp.int32, (c, c), 1)
    keep = [(col >= row) if rv else (col <= row) for _, rv in flat]
    b = [_tri_dot(kp.astype(F32), x[3]) for kp, (x, _) in zip(keep, flat)]
    b_end = [bb[0:1] if rv else bb[c - 1:c] for bb, (_, rv) in zip(b, flat)]
    qb = [(x[0] * jnp.exp(bb)).astype(BF16) for (x, _), bb in zip(flat, b)]
    kb = [x[1] * jnp.exp(-bb) for (x, _), bb in zip(flat, b)]
    kd = [x[1] * jnp.exp(be - bb) for (x, _), bb, be in zip(flat, b, b_end)]
    scores = [jnp.where(kp, _dot_nt(x, y), 0.0).astype(BF16) for kp, x, y in zip(keep, qb, kb)]
    upd = [_dot_tn(x, y[2]) for (y, _), x in zip(flat, kd)]
    decay = [jnp.broadcast_to(jnp.exp(be), (dk, dk)).T for be in b_end]
    n_steps = len(chains[0][0])
    st = [ch[1] for ch in chains]
    outs = [[] for _ in chains]
    for s in range(n_steps):
        at = [ci * n_steps + s for ci in range(len(chains))]
        for ci, f in enumerate(at):
            outs[ci].append(_dot(jnp.concatenate([scores[f], qb[f]], axis=1),
                                 jnp.concatenate([flat[f][0][2], st[ci].astype(BF16)], axis=0)))
        st = [st[ci] * jnp.concatenate([decay[f]] * (st[ci].shape[1] // dk), axis=1) + upd[f]
              for ci, f in enumerate(at)]
    return list(zip(outs, st))


def _gla_scan_kernel(*refs, has_s0, emit_state):
    q_ref, k_ref, v_ref, lf_ref, lb_ref = refs[:5]
    s0_ref = refs[5] if has_s0 else None
    o_ref = refs[5 + has_s0]
    s_ref = refs[6 + has_s0] if emit_state else None
    acc_ref, st_ref = refs[-2:]
    bb, t = q_ref.shape[0], q_ref.shape[1]
    hb = q_ref.shape[2] // GLA_DKH
    nc = t // CHUNK
    half = nc // 2
    kh = lambda h: slice(h * GLA_DKH, (h + 1) * GLA_DKH)
    vh = lambda h: slice(h * GLA_DVH, (h + 1) * GLA_DVH)

    steps = GLA_STEPS
    while half % steps:
        steps //= 2

    def both(accumulate):
        def body(i, carry):
            sls = [[pl.ds(pl.multiple_of(ci * CHUNK, CHUNK), CHUNK)
                    for ci in ((i * steps + s, nc - 1 - i * steps - s)[d] for s in range(steps))] for d in range(2)]
            chains = [([(q_ref[e, sl, kh(h)].astype(F32), k_ref[e, sl, kh(h)].astype(F32), v_ref[e, sl, vh(h)],
                         (lf_ref, lb_ref)[d][e, sl, kh(h)]) for sl in sls[d]], st_ref[n], d == 1)
                      for n, (e, d, h) in enumerate(idx)]
            res = _gla_chunks(chains)
            for n, (e, d, h) in enumerate(idx):
                for sl, o in zip(sls[d], res[n][0]):
                    if accumulate:
                        o_ref[e, sl, vh(h)] = (acc_ref[e, sl, vh(h)] + o).astype(BF16)
                    else:
                        acc_ref[e, sl, vh(h)] = o
            for n, (_, st) in enumerate(res):
                st_ref[n] = st
            return carry
        return body

    idx = [(e, d, h) for e in range(bb) for d in range(2) for h in range(hb)]
    for n, (e, d, h) in enumerate(idx):
        st_ref[n] = s0_ref[e, d, h] if has_s0 else jnp.zeros((GLA_DKH, GLA_DVH), F32)
    lax.fori_loop(0, half // steps, both(False), 0)
    lax.fori_loop(half // steps, nc // steps, both(True), 0)
    if emit_state:
        for n, (e, d, h) in enumerate(idx):
            s_ref[e, d, h] = st_ref[n]


def _gla_scan(q, k, v, lg, s0, bb, hb, emit_state):
    bsz, t, _ = q.shape
    ng = GLA_HEADS // hb
    hd = lambda b, h: (b, 0, h)
    kblk = pl.BlockSpec((bb, t, hb * GLA_DKH), hd)
    vblk = pl.BlockSpec((bb, t, hb * GLA_DVH), hd)
    sblk = pl.BlockSpec((bb, 2, hb, GLA_DKH, GLA_DVH), lambda b, h: (b, 0, h, 0, 0))
    s_shape = jax.ShapeDtypeStruct((bsz, 2, GLA_HEADS, GLA_DKH, GLA_DVH), F32)
    has_s0 = s0 is not None
    return pl.pallas_call(
        functools.partial(_gla_scan_kernel, has_s0=has_s0, emit_state=emit_state),
        grid=(bsz // bb, ng),
        in_specs=[kblk, kblk, vblk, kblk, pl.BlockSpec((bb, t, hb * GLA_DKH), lambda b, h: (b, 0, ng + h))]
        + [sblk] * has_s0,
        out_specs=[vblk] + [sblk] * emit_state,
        out_shape=[jax.ShapeDtypeStruct((bsz, t, GLA_DV), BF16)] + [s_shape] * emit_state,
        scratch_shapes=[pltpu.VMEM((bb, t, hb * GLA_DVH), F32), pltpu.VMEM((2 * bb * hb, GLA_DKH, GLA_DVH), F32)],
        compiler_params=_params(2),
        name="gla_scan",
    )(q, k, v, lg, lg, *([s0] * has_s0))


def _gla_back_kernel(o_ref, g_ref, x_ref, mod0_ref, mod1_ref, gn_ref, wout_ref, nw1_ref, x1_ref, h1_ref):
    d = D_MODEL
    o = o_ref[...].astype(F32)
    gn = gn_ref[...]
    parts = [_rms(o[:, h * GLA_DVH:(h + 1) * GLA_DVH]) * gn for h in range(GLA_HEADS)]
    on = jnp.concatenate(parts, axis=-1) * _silu(g_ref[...].astype(F32))
    out = _dot(on, wout_ref[...])
    gate = mod0_ref[0][:, 2 * d:]
    x1 = x_ref[...] + gate * out
    x1_ref[...] = x1
    mod1 = mod1_ref[0]
    h1_ref[...] = (_rms(x1) * nw1_ref[...] * (1.0 + mod1[:, d:2 * d]) + mod1[:, :d]).astype(BF16)


def _gla_back(o, g, x, mod0, mod1, mod_idx, tm, gla_norm, w_out, norm_w1):
    n, d = x.shape
    nmod = mod0.shape[0]
    row = lambda i: (i, 0)
    const = lambda i: (0, 0)
    modspec = pl.BlockSpec((1, 1, 3 * d), lambda i: (mod_idx(i), 0, 0))
    return pl.pallas_call(
        _gla_back_kernel,
        grid=(n // tm,),
        in_specs=[pl.BlockSpec((tm, d), row), pl.BlockSpec((tm, d), row), pl.BlockSpec((tm, d), row),
                  modspec, modspec,
                  pl.BlockSpec((1, GLA_DVH), const), pl.BlockSpec((d, d), const), pl.BlockSpec((1, d), const)],
        out_specs=[pl.BlockSpec((tm, d), row), pl.BlockSpec((tm, d), row)],
        out_shape=[jax.ShapeDtypeStruct((n, d), F32), jax.ShapeDtypeStruct((n, d), BF16)],
        compiler_params=_params(1),
        name="gla_back",
    )(o, g, x, mod0.reshape(nmod, 1, 3 * d), mod1.reshape(nmod, 1, 3 * d), gla_norm.reshape(1, GLA_DVH),
      w_out.astype(BF16), norm_w1.reshape(1, d))


def _token_shift(h, up_ref, dn_ref, seq_len, grid_w):
    tm = h.shape[0]
    pos = lax.broadcasted_iota(jnp.int32, (tm, 1), 0) + pl.program_id(0) * tm
    prev = pltpu.roll(h, 1, 0)
    nxt = pltpu.roll(h, tm - 1, 0)
    t = pos % seq_len
    if grid_w is None:
        return 0.5 * (jnp.where(t == 0, 0.0, prev) + jnp.where(t == seq_len - 1, 0.0, nxt))
    col = pos % grid_w
    up = jnp.concatenate([up_ref[...].astype(F32), h[:tm - grid_w]], axis=0)
    dn = jnp.concatenate([h[grid_w:], dn_ref[...].astype(F32)], axis=0)
    up = jnp.where(t < grid_w, 0.0, up)
    dn = jnp.where(t >= seq_len - grid_w, 0.0, dn)
    left = jnp.where(col == 0, 0.0, prev)
    right = jnp.where(col == grid_w - 1, 0.0, nxt)
    return 0.25 * (up + dn + left + right)


def _rwkv_front_kernel(*refs, seq_len, grid_w):
    n_h = 1 if grid_w is None else 3
    h_ref, up_ref, dn_ref = (tuple(refs[:n_h]) + (None, None))[:3]
    (mu_ref, wrkvg_ref, w1_ref, w2_ref, w0_ref, a1_ref, a2_ref, a0_ref,
     r_ref, k_ref, v_ref, g_ref, lw_ref, a_ref) = refs[n_h:]
    h = h_ref[...].astype(F32)
    dh = _token_shift(h, up_ref, dn_ref, seq_len, grid_w) - h
    mix = lambda p: (h + dh * mu_ref[p:p + 1, :]).astype(BF16)
    r_ref[...] = jnp.dot(mix(0), wrkvg_ref[0], preferred_element_type=F32).astype(BF16)
    k_ref[...] = jnp.dot(mix(1), wrkvg_ref[1], preferred_element_type=F32).astype(BF16)
    v_ref[...] = jnp.dot(mix(2), wrkvg_ref[2], preferred_element_type=F32).astype(BF16)
    g_ref[...] = jnp.dot(mix(3), wrkvg_ref[3], preferred_element_type=F32).astype(BF16)
    tw = jnp.tanh(jnp.dot(mix(4), w1_ref[...], preferred_element_type=F32))
    lw_ref[...] = -RWKV_DECAY_SCALE * _sigmoid(w0_ref[...] + _dot(tw, w2_ref[...]))
    ta = jnp.dot(mix(5), a1_ref[...], preferred_element_type=F32)
    a_ref[...] = _sigmoid(a0_ref[...] + _dot(ta, a2_ref[...])).astype(BF16)


def _rwkv_front(h, tm, seq_len, grid_w, mu, w_rkvg, w0, w1, w2, a0, a1, a2):
    n, d = h.shape
    rk = RWKV_RANK
    cat1 = lambda w: jnp.concatenate([w[0], w[1]], axis=1)
    bd2 = lambda w: jnp.zeros((2 * rk, 2 * d), F32).at[:rk, :d].set(w[0]).at[rk:, d:].set(w[1])
    row = lambda i: (i, 0)
    const = lambda i: (0, 0)
    if grid_w is None:
        assert tm % seq_len == 0
        h_specs, h_args = [pl.BlockSpec((tm, d), row)], [h]
    else:
        assert tm % grid_w == 0 and seq_len % tm == 0
        rows_per_tile, last = tm // grid_w, n // grid_w - 1
        h_specs = [pl.BlockSpec((tm, d), row),
                   pl.BlockSpec((grid_w, d), lambda i: (jnp.maximum(i * rows_per_tile - 1, 0), 0)),
                   pl.BlockSpec((grid_w, d), lambda i: (jnp.minimum((i + 1) * rows_per_tile, last), 0))]
        h_args = [h, h, h]
    return pl.pallas_call(
        functools.partial(_rwkv_front_kernel, seq_len=seq_len, grid_w=grid_w),
        grid=(n // tm,),
        in_specs=h_specs + [
            pl.BlockSpec((8, d), const),
            pl.BlockSpec((4, d, d), lambda i: (0, 0, 0)),
            pl.BlockSpec((d, 2 * rk), const), pl.BlockSpec((2 * rk, 2 * d), const), pl.BlockSpec((1, 2 * d), const),
            pl.BlockSpec((d, 2 * rk), const), pl.BlockSpec((2 * rk, 2 * d), const), pl.BlockSpec((1, 2 * d), const)],
        out_specs=[pl.BlockSpec((tm, d), row)] * 4 + [pl.BlockSpec((tm, 2 * d), row)] * 2,
        out_shape=[jax.ShapeDtypeStruct((n, d), BF16)] * 4
        + [jax.ShapeDtypeStruct((n, 2 * d), F32), jax.ShapeDtypeStruct((n, 2 * d), BF16)],
        compiler_params=_params(1),
        name="rwkv_front",
    )(*h_args, jnp.zeros((8, d), F32).at[:6].set(mu), w_rkvg.astype(BF16),
      cat1(w1).astype(BF16), bd2(w2).astype(BF16), w0.reshape(1, 2 * d),
      cat1(a1).astype(BF16), bd2(a2).astype(BF16), a0.reshape(1, 2 * d))


def _seg_sum(x, head0):
    s0 = jnp.sum(jnp.where(head0, x, 0.0), axis=-1, keepdims=True)
    s1 = jnp.sum(jnp.where(head0, 0.0, x), axis=-1, keepdims=True)
    return jnp.where(head0, s0, s1)


def _stack(x, head0):
    return jnp.concatenate([jnp.where(head0, x, 0.0), jnp.where(head0, 0.0, x)], axis=0)


def _rwkv_chunk_terms(insts):
    n = len(insts)
    c = insts[0][0].shape[0]
    lane = lax.broadcasted_iota(jnp.int32, (c, LANES), 1)
    head0 = lane < RWKV_HEAD
    row = lax.broadcasted_iota(jnp.int32, (c, c), 0)
    col = lax.broadcasted_iota(jnp.int32, (c, c), 1)
    ti = lax.broadcasted_iota(jnp.int32, (c, LANES), 0)
    si = lane & (c - 1)
    prow = lax.broadcasted_iota(jnp.int32, (LANES, LANES), 0)
    pcol = lax.broadcasted_iota(jnp.int32, (LANES, LANES), 1)
    same_head = (prow >= RWKV_HEAD) == (pcol >= RWKV_HEAD)
    each = lambda f, *ls: [f(*xs) for xs in zip(*ls)]
    rev = [x[8] for x in insts]
    stk = lambda x: _stack(x, head0).astype(BF16)

    cum = [_tri_dot(((col >= row) if x[8] else (col <= row)).astype(F32), x[3]) for x in insts]
    tot = [cm[0:1] if rv else cm[c - 1:c] for cm, rv in zip(cum, rev)]

    a_t, r_t, b_ts, k_ts, b_h, k_h, v_s, vv, bonus = [], [], [], [], [], [], [], [], []
    for (r, k, v, lw, a, k_k, k_a, r_k, _), cm, tt in zip(insts, cum, tot):
        kk = k * k_k
        kk = kk * lax.rsqrt(jnp.maximum(_seg_sum(kk * kk, head0), 1e-24))
        kd = k * (1.0 + (a - 1.0) * k_a)
        bv = kk * a
        bonus.append(_seg_sum(r * kd * r_k, head0) * v)
        e_neg = jnp.exp(-cm)
        e_tot = jnp.exp(tt)
        b_neg, k_neg = bv * e_neg, kd * e_neg
        a_t.append(-kk * jnp.exp(cm - lw))
        r_t.append(r * jnp.exp(cm))
        b_ts.append(stk(b_neg))
        k_ts.append(stk(k_neg))
        b_h.append(b_neg * e_tot)
        k_h.append(k_neg * e_tot)
        v_s.append(stk(v))
        vv.append(v)

    strict = [(si > ti) if rv else (si < ti) for rv in rev]
    incl = [(si >= ti) if rv else (si <= ti) for rv in rev]
    ar = each(lambda x, y: jnp.concatenate([x, y], axis=0).astype(BF16), a_t, r_t)
    g_b = each(_dot_nt, ar, b_ts)
    g_k = each(_dot_nt, ar, k_ts)
    n_ab = each(lambda m_, x: jnp.where(m_, x[:c], 0.0), strict, g_b)
    l_ak = each(lambda m_, x: jnp.where(m_, x[:c], 0.0), strict, g_k)
    p_rb = each(lambda m_, x: jnp.where(m_, x[c:], 0.0), incl, g_b)
    p_rk = each(lambda m_, x: jnp.where(m_, x[c:], 0.0), incl, g_k)

    same_blk = lambda h: (ti // h) == (si // h)
    n_d = each(lambda x: jnp.where(same_blk(INV_BASE), x, 0.0), n_ab)
    steps = max(1, (INV_BASE - 1).bit_length())
    q = n_d
    tinv = n_d
    q = each(lambda x: _dot(x, stk(x)), q) if steps > 1 else q
    for j in range(1, steps):
        last = j == steps - 1
        qs = each(stk, q)
        if last:
            z = each(_dot, tinv, qs)
            tinv = each(lambda t_, x, y: t_ + x + y, tinv, q, z)
        else:
            both = each(lambda x, t_, w_: _dot(jnp.concatenate([x, t_], axis=0), w_), q, tinv, qs)
            tinv = each(lambda t_, x, y: t_ + x + y[c:], tinv, q, both)
            q = each(lambda y: y[:c], both)
    h = INV_BASE
    while h < c:
        link = same_blk(2 * h) & jnp.logical_not(same_blk(h))
        n_off = each(lambda x: jnp.where(link, x, 0.0), n_ab)
        y = each(lambda t_, x: x + _dot(t_, stk(x)), tinv, n_off)
        tinv = each(lambda t_, x: t_ + x + _dot(x, stk(t_)), tinv, y)
        h *= 2
    a_bar = each(lambda t_, x: x + _dot(t_, stk(x)), tinv, a_t)
    lv = each(lambda l_, p, y: _dot(jnp.concatenate([l_, p], axis=0), y), l_ak, p_rk, v_s)
    w = [x[:c] for x in lv]
    u0 = each(lambda t_, x: x + _dot(t_, stk(x)), tinv, w)
    r_bar = each(lambda x, p, y: x + _dot(p, stk(y)), r_t, p_rb, a_bar)
    y0 = each(lambda p, x, z: _dot(p, stk(x)) + z[c:], p_rb, u0, lv)
    zero = jnp.zeros((c, LANES), F32)
    ms = each(lambda ab, u, v_, bh, kh_: _dot_tn(
        jnp.concatenate([jnp.concatenate([ab, u], axis=1), jnp.concatenate([zero, v_], axis=1)], axis=0),
        jnp.concatenate([bh, kh_], axis=0)), a_bar, u0, vv, b_h, k_h)
    m = [jnp.where(same_head, x[:LANES], 0.0) for x in ms]
    s0p = [jnp.where(same_head, x[LANES:], 0.0) for x in ms]
    return [(r_bar[i], y0[i], m[i], s0p[i], jnp.exp(tot[i]), bonus[i]) for i in range(n)]


def _rwkv_scan_kernel(*refs, has_s0, emit_state):
    (r_ref, k_ref, v_ref, lwf_ref, lwb_ref, af_ref, ab_ref,
     kkf_ref, kkb_ref, kaf_ref, kab_ref, rkf_ref, rkb_ref, lnw_ref, lnb_ref) = refs[:15]
    s0_ref = refs[15] if has_s0 else None
    o_ref = refs[15 + has_s0]
    s_ref = refs[16 + has_s0] if emit_state else None
    w_ref, y0_ref, s0p_ref, dec_ref, bonus_ref, yf_ref, yb_ref, st_ref = refs[16 + has_s0 + emit_state:]
    bb, t = r_ref.shape[0], r_ref.shape[1]
    nc = t // CHUNK
    dirs = ((lwf_ref, af_ref, kkf_ref, kaf_ref, rkf_ref, False), (lwb_ref, ab_ref, kkb_ref, kab_ref, rkb_ref, True))
    group = min(bb * nc, PREP_GROUP)
    assert (bb * nc) % group == 0
    chunk_rows = lambda ci: pl.ds(pl.multiple_of(ci * CHUNK, CHUNK), CHUNK)

    def prep(gi, carry):
        insts, where = [], []
        at = [((gi * group + j) // nc, chunk_rows((gi * group + j) % nc)) for j in range(group)]
        for j, (e, sl) in enumerate(at):
            r, k, v = (ref[e, sl, :].astype(F32) for ref in (r_ref, k_ref, v_ref))
            for d, (lw_ref, a_ref, kk_ref, ka_ref, rk_ref, reverse) in enumerate(dirs):
                insts.append((r, k, v, lw_ref[e, sl, :], a_ref[e, sl, :].astype(F32), kk_ref[...], ka_ref[...],
                              rk_ref[...], reverse))
                where.append((d, gi * group + j, e, sl))
        terms = _rwkv_chunk_terms(insts)
        for (d, fi, e, sl), (r_bar, y0, m, s0p, dec, bon) in zip(where, terms):
            rbar_t = jnp.concatenate([r_bar, jnp.zeros_like(r_bar)], axis=0).T
            w_ref[d, fi] = jnp.concatenate([m, rbar_t], axis=1).astype(BF16)
            y0_ref[d, fi] = y0
            s0p_ref[d, fi] = s0p
            dec_ref[d, fi] = jnp.broadcast_to(dec, (8, LANES))
        for j in range(group):
            _, _, e, sl = where[2 * j]
            bonus_ref[e, sl, :] = terms[2 * j][5] + terms[2 * j + 1][5]
        return carry

    lax.fori_loop(0, bb * nc // group, prep, 0)

    seq_steps = SEQ_STEPS if nc % SEQ_STEPS == 0 else 1
    idx = [(d, e) for d in range(2) for e in range(bb)]

    def seq(i, _):
        carry = tuple(st_ref[n] for n in range(len(idx)))
        for s in range(seq_steps):
            cis = (i * seq_steps + s, nc - 1 - i * seq_steps - s)
            sw = [jnp.dot(carry[n].astype(BF16), w_ref[d, e * nc + cis[d]], preferred_element_type=F32)
                  for n, (d, e) in enumerate(idx)]
            for n, (d, e) in enumerate(idx):
                y = sw[n][:, LANES:].T[:CHUNK] + y0_ref[d, e * nc + cis[d]]
                (yf_ref, yb_ref)[d][e, chunk_rows(cis[d]), :] = y
            carry = tuple(carry[n] * dec_ref[d, e * nc + cis[d]][0:1] + sw[n][:, :LANES] + s0p_ref[d, e * nc + cis[d]]
                          for n, (d, e) in enumerate(idx))
        for n, st in enumerate(carry):
            st_ref[n] = st
        return 0

    for n, (d, e) in enumerate(idx):
        st_ref[n] = s0_ref[e, d, 0] if has_s0 else jnp.zeros((LANES, LANES), F32)
    lax.fori_loop(0, nc // seq_steps, seq, 0)
    if emit_state:
        for d in range(2):
            for e in range(bb):
                s = st_ref[d * bb + e]
                s_ref[e, d, 0] = s[:RWKV_HEAD, :RWKV_HEAD]
                s_ref[e, d, 1] = pltpu.roll(s, RWKV_HEAD, 1)[RWKV_HEAD:, :RWKV_HEAD]

    prow = lax.broadcasted_iota(jnp.int32, (LANES, LANES), 0)
    pcol = lax.broadcasted_iota(jnp.int32, (LANES, LANES), 1)
    head_ones = ((prow >= RWKV_HEAD) == (pcol >= RWKV_HEAD)).astype(BF16)
    inv_n = 1.0 / RWKV_HEAD

    def head_mean(x):
        hi = x.astype(BF16)
        lo = (x - hi.astype(F32)).astype(BF16)
        both = jnp.dot(jnp.concatenate([hi, lo], axis=0), head_ones, preferred_element_type=F32)
        return (both[:x.shape[0]] + both[x.shape[0]:]) * inv_n

    fgroup = min(bb * nc, FINISH_GROUP)
    assert (bb * nc) % fgroup == 0

    def finish(gi, carry):
        at = [((gi * fgroup + j) // nc, chunk_rows((gi * fgroup + j) % nc)) for j in range(fgroup)]
        y = [yf_ref[e, sl, :] + yb_ref[e, sl, :] for e, sl in at]
        yc = [x - head_mean(x) for x in y]
        var = [head_mean(x * x) for x in yc]
        for (e, sl), x, vr in zip(at, yc, var):
            o_ref[e, sl, :] = (x * lax.rsqrt(vr + LNX_EPS) * lnw_ref[...] + lnb_ref[...]
                               + bonus_ref[e, sl, :]).astype(BF16)
        return carry

    lax.fori_loop(0, bb * nc // fgroup, finish, 0)


def _rwkv_scan(r, k, v, lw, a, k_k, k_a, r_k, lnx_w, lnx_b, s0, bb, emit_state):
    bsz, t, d = r.shape
    nc = t // CHUNK
    seq = lambda b, p: (b, 0, p)
    seq_b = lambda b, p: (b, 0, RWKV_PAIRS + p)
    par = lambda b, p: (0, p)
    par_b = lambda b, p: (0, RWKV_PAIRS + p)
    sblk = pl.BlockSpec((bb, t, LANES), seq)
    sblk_b = pl.BlockSpec((bb, t, LANES), seq_b)
    pblk = pl.BlockSpec((1, LANES), par)
    pblk_b = pl.BlockSpec((1, LANES), par_b)
    s0blk = pl.BlockSpec((bb, 2, 1, LANES, LANES), lambda b, p: (b, 0, p, 0, 0))
    s_blk = pl.BlockSpec((bb, 2, 2, RWKV_HEAD, RWKV_HEAD), lambda b, p: (b, 0, p, 0, 0))
    s_shape = jax.ShapeDtypeStruct((bsz, 2, RWKV_HEADS, RWKV_HEAD, RWKV_HEAD), F32)
    has_s0 = s0 is not None
    k_k, k_a, r_k = (p.reshape(1, 2 * d) for p in (k_k, k_a, r_k))
    return pl.pallas_call(
        functools.partial(_rwkv_scan_kernel, has_s0=has_s0, emit_state=emit_state),
        grid=(bsz // bb, RWKV_PAIRS),
        in_specs=[sblk, sblk, sblk, sblk, sblk_b, sblk, sblk_b,
                  pblk, pblk_b, pblk, pblk_b, pblk, pblk_b, pblk, pblk] + [s0blk] * has_s0,
        out_specs=[sblk] + [s_blk] * emit_state,
        out_shape=[jax.ShapeDtypeStruct((bsz, t, d), BF16)] + [s_shape] * emit_state,
        scratch_shapes=[pltpu.VMEM((2, bb * nc, LANES, 2 * LANES), BF16), pltpu.VMEM((2, bb * nc, CHUNK, LANES), F32),
                        pltpu.VMEM((2, bb * nc, LANES, LANES), F32),
                        pltpu.VMEM((2, bb * nc, 8, LANES), F32), pltpu.VMEM((bb, t, LANES), F32),
                        pltpu.VMEM((bb, t, LANES), F32), pltpu.VMEM((bb, t, LANES), F32),
                        pltpu.VMEM((2 * bb, LANES, LANES), F32)],
        compiler_params=_params(2),
        name="rwkv_scan",
    )(r, k, v, lw, lw, a, a, k_k, k_k, k_a, k_a, r_k, r_k,
      lnx_w.reshape(1, d), lnx_b.reshape(1, d), *([s0] * has_s0))


def _rwkv_back_kernel(o_ref, g_ref, x_ref, mod_ref, wout_ref, nf_ref, y_ref):
    out = _dot(o_ref[...].astype(F32) * _silu(g_ref[...].astype(F32)), wout_ref[...])
    x2 = x_ref[...] + mod_ref[0][:, 2 * D_MODEL:] * out
    y_ref[...] = _rms(x2) * nf_ref[...]


def _rwkv_back(o, g, x, mod1, mod_idx, tm, w_out, norm_f):
    n, d = x.shape
    nmod = mod1.shape[0]
    row = lambda i: (i, 0)
    const = lambda i: (0, 0)
    return pl.pallas_call(
        _rwkv_back_kernel,
        grid=(n // tm,),
        in_specs=[pl.BlockSpec((tm, d), row), pl.BlockSpec((tm, d), row), pl.BlockSpec((tm, d), row),
                  pl.BlockSpec((1, 1, 3 * d), lambda i: (mod_idx(i), 0, 0)),
                  pl.BlockSpec((d, d), const), pl.BlockSpec((1, d), const)],
        out_specs=pl.BlockSpec((tm, d), row),
        out_shape=jax.ShapeDtypeStruct((n, d), F32),
        compiler_params=_params(1),
        name="rwkv_back",
    )(o, g, x, mod1.reshape(nmod, 1, 3 * d), w_out.astype(BF16), norm_f.reshape(1, d))


def _pair_blockdiag(s):
    b = s.shape[0]
    s = s.reshape(b, 2, RWKV_PAIRS, 2, RWKV_HEAD, RWKV_HEAD)
    z = jnp.zeros_like(s[:, :, :, 0])
    top = jnp.concatenate([s[:, :, :, 0], z], axis=-1)
    bot = jnp.concatenate([z, s[:, :, :, 1]], axis=-1)
    return jnp.concatenate([top, bot], axis=-2)


def _tile_rows(n_seq, t, may_span, rows):
    if t % rows == 0:
        return rows
    if may_span and rows % t == 0 and (n_seq * t) % rows == 0:
        return rows
    return t


def kernel(x_prompt, x_sample, state_gla, state_rwkv, c, c_ctx, w_mod, b_mod, norm_w, gla_w_in, gla_w_a1,
           gla_w_a2, gla_b_a, gla_norm, gla_w_out, rwkv_mu, rwkv_w_rkvg, rwkv_w0, rwkv_w1, rwkv_w2, rwkv_a0,
           rwkv_a1, rwkv_a2, rwkv_k_k, rwkv_k_a, rwkv_r_k, rwkv_lnx_w, rwkv_lnx_b, rwkv_w_out, norm_f):
    d = D_MODEL
    bp, tp, _ = x_prompt.shape
    bs, ts, _ = x_sample.shape
    assert tp % CHUNK == 0 and ts % CHUNK == 0 and ts % GRID_W == 0

    nrows = -(-(1 + bs) // 8) * 8
    cond = jnp.zeros((nrows, d), F32).at[0].set(c_ctx).at[1:1 + bs].set(c)
    cond = cond * (1.0 / (1.0 + jnp.exp(-cond)))
    mod0 = _matmul_bias(cond, w_mod[0], b_mod[0].reshape(1, 3 * d))
    mod1 = _matmul_bias(cond, w_mod[1], b_mod[1].reshape(1, 3 * d))

    def trunk(x3, shared_mod, grid_w, gla_s0, rwkv_s0, gla_bb, rwkv_bb, emit_state):
        b, t, _ = x3.shape
        n = b * t
        tm, tw = (_tile_rows(b, t, shared_mod, rows) for rows in (TOKEN_TILE, WIDE_TILE))
        mod_row = lambda rows: (lambda i: 0) if shared_mod else (lambda i: 1 + i // (t // rows))
        x = x3.reshape(n, d)
        seq = lambda arr: arr.reshape(b, t, arr.shape[-1])
        q, k, v, g, lg = _gla_front(x, mod0, mod_row(tw), tw, norm_w[0], gla_w_in[0], gla_w_a1[0], gla_w_a2[0], gla_b_a[0])
        o, *gla_s = _gla_scan(seq(q), seq(k), seq(v), seq(lg), gla_s0, gla_bb, GLA_HEADS, emit_state)
        x1, h1 = _gla_back(o.reshape(n, d), g, x, mod0, mod1, mod_row(tw), tw, gla_norm[0], gla_w_out[0], norm_w[1])
        r, k, v, g, lw, a = _rwkv_front(h1, tm, t, grid_w, rwkv_mu[0], rwkv_w_rkvg[0], rwkv_w0[0], rwkv_w1[0], rwkv_w2[0],
                                        rwkv_a0[0], rwkv_a1[0], rwkv_a2[0])
        o, *rwkv_s = _rwkv_scan(seq(r), seq(k), seq(v), seq(lw), seq(a), rwkv_k_k[0], rwkv_k_a[0], rwkv_r_k[0],
                                rwkv_lnx_w[0], rwkv_lnx_b[0], rwkv_s0, rwkv_bb, emit_state)
        y = _rwkv_back(o.reshape(n, d), g, x1, mod1, mod_row(tw), tw, rwkv_w_out[0], norm_f)
        return y.reshape(b, t, d), gla_s, rwkv_s

    y_prompt, (gla_s,), (rwkv_s,) = trunk(x_prompt, True, None, None, None, min(bp, 2), min(bp, 8), True)

    y_sample, _, _ = trunk(x_sample, False, GRID_W, state_gla[:, 0], _pair_blockdiag(state_rwkv[:, 0]),
                           1, min(bs, 4), False)
    return (y_prompt, y_sample, gla_s[:, None], rwkv_s[:, None])
```

```python
import functools

import jax
import jax.numpy as jnp
from jax import lax
from jax.experimental import pallas as pl
from jax.experimental.pallas import tpu as pltpu

F32 = jnp.float32
BF16 = jnp.bfloat16

D_MODEL = 1024
EPS = 1e-6
GRID_W = 64
GLA_HEADS = 4
GLA_DK = 512
GLA_DV = 1024
GLA_DKH = 128
GLA_DVH = 256
GLA_GATE_RANK = 16
GLA_GATE_NORM = 16.0
RWKV_HEAD = 64
RWKV_HEADS = 16
RWKV_PAIRS = 8
RWKV_RANK = 64
LNX_EPS = 64e-5
RWKV_DECAY_SCALE = 0.6065306597126334
CHUNK = 64
SEQ_STEPS = 4
GLA_STEPS = 4
INV_BASE = 8
TOKEN_TILE = 1024
WIDE_TILE = 1024
FINISH_GROUP = 32
PREP_GROUP = 8
LANES = 128
VMEM_LIMIT = 56 * 1024 * 1024


def _dot(a, b):
    return jnp.dot(a.astype(BF16), b.astype(BF16), preferred_element_type=F32)


def _dot_nt(a, b):
    return lax.dot_general(a.astype(BF16), b.astype(BF16), (((1,), (1,)), ((), ())),
                           preferred_element_type=F32)


def _dot_tn(a, b):
    return lax.dot_general(a.astype(BF16), b.astype(BF16), (((0,), (0,)), ((), ())),
                           preferred_element_type=F32)


def _tri_dot(tri, x):
    hi = x.astype(BF16)
    lo = (x - hi.astype(F32)).astype(BF16)
    t = tri.astype(BF16)
    return jnp.dot(t, hi, preferred_element_type=F32) + jnp.dot(t, lo, preferred_element_type=F32)


def _log_sigmoid(z):
    return jnp.minimum(z, 0.0) - jnp.log(1.0 + jnp.exp(-jnp.abs(z)))


def _sigmoid(z):
    return 1.0 / (1.0 + jnp.exp(-z))


def _silu(z):
    return z * _sigmoid(z)


def _rms(x):
    return x * lax.rsqrt(jnp.mean(x * x, axis=-1, keepdims=True) + EPS)


def _params(n_grid_dims):
    return pltpu.CompilerParams(dimension_semantics=("arbitrary",) * n_grid_dims,
                                vmem_limit_bytes=VMEM_LIMIT)


def _mm_kernel(x_ref, w_ref, b_ref, o_ref):
    o_ref[...] = _dot(x_ref[...], w_ref[...]) + b_ref[...]


def _matmul_bias(x, w, b):
    m, k = x.shape
    n = w.shape[1]
    return pl.pallas_call(
        _mm_kernel,
        grid=(1,),
        in_specs=[pl.BlockSpec((m, k), lambda i: (0, 0)),
                  pl.BlockSpec((k, n), lambda i: (0, 0)),
                  pl.BlockSpec((1, n), lambda i: (0, 0))],
        out_specs=pl.BlockSpec((m, n), lambda i: (0, 0)),
        out_shape=jax.ShapeDtypeStruct((m, n), F32),
        compiler_params=_params(1),
        name="mod_matmul",
    )(x, w.astype(BF16), b)


def _gla_front_kernel(x_ref, mod_ref, nw_ref, win_ref, wa1_ref, wa2_ref, ba_ref,
                      q_ref, k_ref, v_ref, g_ref, lg_ref):
    d = D_MODEL
    mod = mod_ref[0]
    shift, scale = mod[:, :d], mod[:, d:2 * d]
    h = _rms(x_ref[...]) * nw_ref[...] * (1.0 + scale) + shift
    hb = h.astype(BF16)
    proj = lambda lo, hi: jnp.dot(hb, win_ref[:, lo:hi], preferred_element_type=F32)
    q_ref[...] = (proj(0, GLA_DK) * (GLA_DKH ** -0.5)).astype(BF16)
    k_ref[...] = proj(GLA_DK, 2 * GLA_DK).astype(BF16)
    v_ref[...] = proj(2 * GLA_DK, 2 * GLA_DK + GLA_DV).astype(BF16)
    g_ref[...] = proj(2 * GLA_DK + GLA_DV, 2 * GLA_DK + 2 * GLA_DV).astype(BF16)
    t = jnp.dot(hb, wa1_ref[...], preferred_element_type=F32)
    z = _dot(t, wa2_ref[...]) + ba_ref[...]
    lg_ref[...] = _log_sigmoid(z) * (1.0 / GLA_GATE_NORM)


def _gla_front(x, mod, mod_idx, tm, norm_w, w_in, w_a1, w_a2, b_a):
    n, d = x.shape
    nmod = mod.shape[0]
    wa1 = jnp.zeros((d, LANES), F32).at[:, :GLA_GATE_RANK].set(w_a1[0]).at[:, GLA_GATE_RANK:2 * GLA_GATE_RANK].set(w_a1[1])
    wa2 = jnp.zeros((LANES, 2 * GLA_DK), F32).at[:GLA_GATE_RANK, :GLA_DK].set(w_a2[0])
    wa2 = wa2.at[GLA_GATE_RANK:2 * GLA_GATE_RANK, GLA_DK:].set(w_a2[1])
    ba = b_a.reshape(1, 2 * GLA_DK)
    row = lambda i: (i, 0)
    const = lambda i: (0, 0)
    n_in = w_in.shape[1]
    outs = pl.pallas_call(
        _gla_front_kernel,
        grid=(n // tm,),
        in_specs=[pl.BlockSpec((tm, d), row),
                  pl.BlockSpec((1, 1, 3 * d), lambda i: (mod_idx(i), 0, 0)),
                  pl.BlockSpec((1, d), const),
                  pl.BlockSpec((d, n_in), const),
                  pl.BlockSpec((d, LANES), const),
                  pl.BlockSpec((LANES, 2 * GLA_DK), const),
                  pl.BlockSpec((1, 2 * GLA_DK), const)],
        out_specs=[pl.BlockSpec((tm, GLA_DK), row), pl.BlockSpec((tm, GLA_DK), row),
                   pl.BlockSpec((tm, GLA_DV), row), pl.BlockSpec((tm, GLA_DV), row),
                   pl.BlockSpec((tm, 2 * GLA_DK), row)],
        out_shape=[jax.ShapeDtypeStruct((n, GLA_DK), BF16), jax.ShapeDtypeStruct((n, GLA_DK), BF16),
                   jax.ShapeDtypeStruct((n, GLA_DV), BF16), jax.ShapeDtypeStruct((n, GLA_DV), BF16),
                   jax.ShapeDtypeStruct((n, 2 * GLA_DK), F32)],
        compiler_params=_params(1),
        name="gla_front",
    )(x, mod.reshape(nmod, 1, 3 * d), norm_w.reshape(1, d), w_in.astype(BF16), wa1.astype(BF16),
      wa2.astype(BF16), ba)
    return outs


def _gla_chunks(chains):
    flat = [(x, ch[2]) for ch in chains for x in ch[0]]
    c = flat[0][0][0].shape[0]
    dk = flat[0][0][1].shape[1]
    row = lax.broadcasted_iota(jnp.int32, (c, c), 0)
    col = lax.broadcasted_iota(jnp.int32, (c, c), 1)
    keep = [(col >= row) if rv else (col <= row) for _, rv in flat]
    b = [_tri_dot(kp.astype(F32), x[3]) for kp, (x, _) in zip(keep, flat)]
    b_end = [bb[0:1] if rv else bb[c - 1:c] for bb, (_, rv) in zip(b, flat)]
    qb = [(x[0] * jnp.exp(bb)).astype(BF16) for (x, _), bb in zip(flat, b)]
    kb = [x[1] * jnp.exp(-bb) for (x, _), bb in zip(flat, b)]
    kd = [x[1] * jnp.exp(be - bb) for (x, _), bb, be in zip(flat, b, b_end)]
    scores = [jnp.where(kp, _dot_nt(x, y), 0.0).astype(BF16) for kp, x, y in zip(keep, qb, kb)]
    upd = [_dot_tn(x, y[2]) for (y, _), x in zip(flat, kd)]
    decay = [jnp.broadcast_to(jnp.exp(be), (dk, dk)).T for be in b_end]
    n_steps = len(chains[0][0])
    st = [ch[1] for ch in chains]
    outs = [[] for _ in chains]
    for s in range(n_steps):
        at = [ci * n_steps + s for ci in range(len(chains))]
        for ci, f in enumerate(at):
            outs[ci].append(_dot(jnp.concatenate([scores[f], qb[f]], axis=1),
                                 jnp.concatenate([flat[f][0][2], st[ci].astype(BF16)], axis=0)))
        st = [st[ci] * jnp.concatenate([decay[f]] * (st[ci].shape[1] // dk), axis=1) + upd[f]
              for ci, f in enumerate(at)]
    return list(zip(outs, st))


def _gla_scan_kernel(*refs, has_s0, emit_state):
    q_ref, k_ref, v_ref, lf_ref, lb_ref = refs[:5]
    s0_ref = refs[5] if has_s0 else None
    o_ref = refs[5 + has_s0]
    s_ref = refs[6 + has_s0] if emit_state else None
    acc_ref, st_ref = refs[-2:]
    bb, t = q_ref.shape[0], q_ref.shape[1]
    hb = q_ref.shape[2] // GLA_DKH
    nc = t // CHUNK
    half = nc // 2
    kh = lambda h: slice(h * GLA_DKH, (h + 1) * GLA_DKH)
    vh = lambda h: slice(h * GLA_DVH, (h + 1) * GLA_DVH)

    steps = GLA_STEPS
    while half % steps:
        steps //= 2

    def both(accumulate):
        def body(i, carry):
            sls = [[pl.ds(pl.multiple_of(ci * CHUNK, CHUNK), CHUNK)
                    for ci in ((i * steps + s, nc - 1 - i * steps - s)[d] for s in range(steps))] for d in range(2)]
            chains = [([(q_ref[e, sl, kh(h)].astype(F32), k_ref[e, sl, kh(h)].astype(F32), v_ref[e, sl, vh(h)],
                         (lf_ref, lb_ref)[d][e, sl, kh(h)]) for sl in sls[d]], st_ref[n], d == 1)
                      for n, (e, d, h) in enumerate(idx)]
            res = _gla_chunks(chains)
            for n, (e, d, h) in enumerate(idx):
                for sl, o in zip(sls[d], res[n][0]):
                    if accumulate:
                        o_ref[e, sl, vh(h)] = (acc_ref[e, sl, vh(h)] + o).astype(BF16)
                    else:
                        acc_ref[e, sl, vh(h)] = o
            for n, (_, st) in enumerate(res):
                st_ref[n] = st
            return carry
        return body

    idx = [(e, d, h) for e in range(bb) for d in range(2) for h in range(hb)]
    for n, (e, d, h) in enumerate(idx):
        st_ref[n] = s0_ref[e, d, h] if has_s0 else jnp.zeros((GLA_DKH, GLA_DVH), F32)
    lax.fori_loop(0, half // steps, both(False), 0)
    lax.fori_loop(half // steps, nc // steps, both(True), 0)
    if emit_state:
        for n, (e, d, h) in enumerate(idx):
            s_ref[e, d, h] = st_ref[n]


def _gla_scan(q, k, v, lg, s0, bb, hb, emit_state):
    bsz, t, _ = q.shape
    ng = GLA_HEADS // hb
    hd = lambda b, h: (b, 0, h)
    kblk = pl.BlockSpec((bb, t, hb * GLA_DKH), hd)
    vblk = pl.BlockSpec((bb, t, hb * GLA_DVH), hd)
    sblk = pl.BlockSpec((bb, 2, hb, GLA_DKH, GLA_DVH), lambda b, h: (b, 0, h, 0, 0))
    s_shape = jax.ShapeDtypeStruct((bsz, 2, GLA_HEADS, GLA_DKH, GLA_DVH), F32)
    has_s0 = s0 is not None
    return pl.pallas_call(
        functools.partial(_gla_scan_kernel, has_s0=has_s0, emit_state=emit_state),
        grid=(bsz // bb, ng),
        in_specs=[kblk, kblk, vblk, kblk, pl.BlockSpec((bb, t, hb * GLA_DKH), lambda b, h: (b, 0, ng + h))]
        + [sblk] * has_s0,
        out_specs=[vblk] + [sblk] * emit_state,
        out_shape=[jax.ShapeDtypeStruct((bsz, t, GLA_DV), BF16)] + [s_shape] * emit_state,
        scratch_shapes=[pltpu.VMEM((bb, t, hb * GLA_DVH), F32), pltpu.VMEM((2 * bb * hb, GLA_DKH, GLA_DVH), F32)],
        compiler_params=_params(2),
        name="gla_scan",
    )(q, k, v, lg, lg, *([s0] * has_s0))


def _gla_back_kernel(o_ref, g_ref, x_ref, mod0_ref, mod1_ref, gn_ref, wout_ref, nw1_ref, x1_ref, h1_ref):
    d = D_MODEL
    o = o_ref[...].astype(F32)
    gn = gn_ref[...]
    parts = [_rms(o[:, h * GLA_DVH:(h + 1) * GLA_DVH]) * gn for h in range(GLA_HEADS)]
    on = jnp.concatenate(parts, axis=-1) * _silu(g_ref[...].astype(F32))
    out = _dot(on, wout_ref[...])
    gate = mod0_ref[0][:, 2 * d:]
    x1 = x_ref[...] + gate * out
    x1_ref[...] = x1
    mod1 = mod1_ref[0]
    h1_ref[...] = (_rms(x1) * nw1_ref[...] * (1.0 + mod1[:, d:2 * d]) + mod1[:, :d]).astype(BF16)


def _gla_back(o, g, x, mod0, mod1, mod_idx, tm, gla_norm, w_out, norm_w1):
    n, d = x.shape
    nmod = mod0.shape[0]
    row = lambda i: (i, 0)
    const = lambda i: (0, 0)
    modspec = pl.BlockSpec((1, 1, 3 * d), lambda i: (mod_idx(i), 0, 0))
    return pl.pallas_call(
        _gla_back_kernel,
        grid=(n // tm,),
        in_specs=[pl.BlockSpec((tm, d), row), pl.BlockSpec((tm, d), row), pl.BlockSpec((tm, d), row),
                  modspec, modspec,
                  pl.BlockSpec((1, GLA_DVH), const), pl.BlockSpec((d, d), const), pl.BlockSpec((1, d), const)],
        out_specs=[pl.BlockSpec((tm, d), row), pl.BlockSpec((tm, d), row)],
        out_shape=[jax.ShapeDtypeStruct((n, d), F32), jax.ShapeDtypeStruct((n, d), BF16)],
        compiler_params=_params(1),
        name="gla_back",
    )(o, g, x, mod0.reshape(nmod, 1, 3 * d), mod1.reshape(nmod, 1, 3 * d), gla_norm.reshape(1, GLA_DVH),
      w_out.astype(BF16), norm_w1.reshape(1, d))


def _token_shift(h, up_ref, dn_ref, seq_len, grid_w):
    tm = h.shape[0]
    pos = lax.broadcasted_iota(jnp.int32, (tm, 1), 0) + pl.program_id(0) * tm
    prev = pltpu.roll(h, 1, 0)
    nxt = pltpu.roll(h, tm - 1, 0)
    t = pos % seq_len
    if grid_w is None:
        return 0.5 * (jnp.where(t == 0, 0.0, prev) + jnp.where(t == seq_len - 1, 0.0, nxt))
    col = pos % grid_w
    up = jnp.concatenate([up_ref[...].astype(F32), h[:tm - grid_w]], axis=0)
    dn = jnp.concatenate([h[grid_w:], dn_ref[...].astype(F32)], axis=0)
    up = jnp.where(t < grid_w, 0.0, up)
    dn = jnp.where(t >= seq_len - grid_w, 0.0, dn)
    left = jnp.where(col == 0, 0.0, prev)
    right = jnp.where(col == grid_w - 1, 0.0, nxt)
    return 0.25 * (up + dn + left + right)


def _rwkv_front_kernel(*refs, seq_len, grid_w):
    n_h = 1 if grid_w is None else 3
    h_ref, up_ref, dn_ref = (tuple(refs[:n_h]) + (None, None))[:3]
    mu_ref, wrkvg_ref, w1_ref, a1_ref, r_ref, k_ref, v_ref, g_ref, tw_ref, ta_ref = refs[n_h:]
    h = h_ref[...].astype(F32)
    dh = _token_shift(h, up_ref, dn_ref, seq_len, grid_w) - h
    mix = lambda p: (h + dh * mu_ref[p:p + 1, :]).astype(BF16)
    r_ref[...] = jnp.dot(mix(0), wrkvg_ref[0], preferred_element_type=F32).astype(BF16)
    k_ref[...] = jnp.dot(mix(1), wrkvg_ref[1], preferred_element_type=F32).astype(BF16)
    v_ref[...] = jnp.dot(mix(2), wrkvg_ref[2], preferred_element_type=F32).astype(BF16)
    g_ref[...] = jnp.dot(mix(3), wrkvg_ref[3], preferred_element_type=F32).astype(BF16)
    tw_ref[...] = jnp.tanh(jnp.dot(mix(4), w1_ref[...], preferred_element_type=F32)).astype(BF16)
    ta_ref[...] = jnp.dot(mix(5), a1_ref[...], preferred_element_type=F32).astype(BF16)


def _rwkv_gates_kernel(tw_ref, ta_ref, w2_ref, w0_ref, a2_ref, a0_ref, lw_ref, a_ref):
    half_sig = lambda z: 0.5 * jnp.tanh(0.5 * z)
    zw = jnp.dot(tw_ref[...], w2_ref[...], preferred_element_type=F32)
    lw_ref[...] = -RWKV_DECAY_SCALE * (half_sig(w0_ref[...] + zw) + 0.5)
    za = jnp.dot(ta_ref[...], a2_ref[...], preferred_element_type=F32)
    a_ref[...] = (half_sig(a0_ref[...] + za) + 0.5).astype(BF16)


def _rwkv_gates(tw, ta, tm, w0, w2, a0, a2):
    n = tw.shape[0]
    d = D_MODEL
    rk = RWKV_RANK
    bd2 = lambda w: jnp.zeros((2 * rk, 2 * d), F32).at[:rk, :d].set(w[0]).at[rk:, d:].set(w[1])
    row = lambda i: (i, 0)
    const = lambda i: (0, 0)
    return pl.pallas_call(
        _rwkv_gates_kernel,
        grid=(n // tm,),
        in_specs=[pl.BlockSpec((tm, 2 * rk), row), pl.BlockSpec((tm, 2 * rk), row),
                  pl.BlockSpec((2 * rk, 2 * d), const), pl.BlockSpec((1, 2 * d), const),
                  pl.BlockSpec((2 * rk, 2 * d), const), pl.BlockSpec((1, 2 * d), const)],
        out_specs=[pl.BlockSpec((tm, 2 * d), row)] * 2,
        out_shape=[jax.ShapeDtypeStruct((n, 2 * d), F32), jax.ShapeDtypeStruct((n, 2 * d), BF16)],
        compiler_params=_params(1),
        name="rwkv_gates",
    )(tw, ta, bd2(w2).astype(BF16), w0.reshape(1, 2 * d), bd2(a2).astype(BF16), a0.reshape(1, 2 * d))


def _rwkv_front(h, tm, seq_len, grid_w, mu, w_rkvg, w1, a1):
    n, d = h.shape
    rk = RWKV_RANK
    cat1 = lambda w: jnp.concatenate([w[0], w[1]], axis=1)
    row = lambda i: (i, 0)
    const = lambda i: (0, 0)
    if grid_w is None:
        assert tm % seq_len == 0
        h_specs, h_args = [pl.BlockSpec((tm, d), row)], [h]
    else:
        assert tm % grid_w == 0 and seq_len % tm == 0
        rows_per_tile, last = tm // grid_w, n // grid_w - 1
        h_specs = [pl.BlockSpec((tm, d), row),
                   pl.BlockSpec((grid_w, d), lambda i: (jnp.maximum(i * rows_per_tile - 1, 0), 0)),
                   pl.BlockSpec((grid_w, d), lambda i: (jnp.minimum((i + 1) * rows_per_tile, last), 0))]
        h_args = [h, h, h]
    return pl.pallas_call(
        functools.partial(_rwkv_front_kernel, seq_len=seq_len, grid_w=grid_w),
        grid=(n // tm,),
        in_specs=h_specs + [
            pl.BlockSpec((8, d), const),
            pl.BlockSpec((4, d, d), lambda i: (0, 0, 0)),
            pl.BlockSpec((d, 2 * rk), const), pl.BlockSpec((d, 2 * rk), const)],
        out_specs=[pl.BlockSpec((tm, d), row)] * 4 + [pl.BlockSpec((tm, 2 * rk), row)] * 2,
        out_shape=[jax.ShapeDtypeStruct((n, d), BF16)] * 4 + [jax.ShapeDtypeStruct((n, 2 * rk), BF16)] * 2,
        compiler_params=_params(1),
        name="rwkv_front",
    )(*h_args, jnp.zeros((8, d), F32).at[:6].set(mu), w_rkvg.astype(BF16),
      cat1(w1).astype(BF16), cat1(a1).astype(BF16))


def _seg_sum(x, head0):
    s0 = jnp.sum(jnp.where(head0, x, 0.0), axis=-1, keepdims=True)
    s1 = jnp.sum(jnp.where(head0, 0.0, x), axis=-1, keepdims=True)
    return jnp.where(head0, s0, s1)


def _stack(x, head0):
    return jnp.concatenate([jnp.where(head0, x, 0.0), jnp.where(head0, 0.0, x)], axis=0)


def _rwkv_chunk_terms(insts):
    n = len(insts)
    c = insts[0][0].shape[0]
    lane = lax.broadcasted_iota(jnp.int32, (c, LANES), 1)
    head0 = lane < RWKV_HEAD
    row = lax.broadcasted_iota(jnp.int32, (c, c), 0)
    col = lax.broadcasted_iota(jnp.int32, (c, c), 1)
    ti = lax.broadcasted_iota(jnp.int32, (c, LANES), 0)
    si = lane & (c - 1)
    prow = lax.broadcasted_iota(jnp.int32, (LANES, LANES), 0)
    pcol = lax.broadcasted_iota(jnp.int32, (LANES, LANES), 1)
    same_head = (prow >= RWKV_HEAD) == (pcol >= RWKV_HEAD)
    each = lambda f, *ls: [f(*xs) for xs in zip(*ls)]
    rev = [x[8] for x in insts]
    stk = lambda x: _stack(x, head0).astype(BF16)

    cum = [_tri_dot(((col >= row) if x[8] else (col <= row)).astype(F32), x[3]) for x in insts]
    tot = [cm[0:1] if rv else cm[c - 1:c] for cm, rv in zip(cum, rev)]

    a_t, r_t, b_ts, k_ts, b_h, k_h, v_s, vv, bonus = [], [], [], [], [], [], [], [], []
    for (r, k, v, lw, a, k_k, k_a, r_k, _), cm, tt in zip(insts, cum, tot):
        kk = k * k_k
        kk = kk * lax.rsqrt(jnp.maximum(_seg_sum(kk * kk, head0), 1e-24))
        kd = k * (1.0 + (a - 1.0) * k_a)
        bv = kk * a
        bonus.append(_seg_sum(r * kd * r_k, head0) * v)
        e_neg = jnp.exp(-cm)
        e_tot = jnp.exp(tt)
        b_neg, k_neg = bv * e_neg, kd * e_neg
        a_t.append(-kk * jnp.exp(cm - lw))
        r_t.append(r * jnp.exp(cm))
        b_ts.append(stk(b_neg))
        k_ts.append(stk(k_neg))
        b_h.append(b_neg * e_tot)
        k_h.append(k_neg * e_tot)
        v_s.append(stk(v))
        vv.append(v)

    strict = [(si > ti) if rv else (si < ti) for rv in rev]
    incl = [(si >= ti) if rv else (si <= ti) for rv in rev]
    ar = each(lambda x, y: jnp.concatenate([x, y], axis=0).astype(BF16), a_t, r_t)
    g_b = each(_dot_nt, ar, b_ts)
    g_k = each(_dot_nt, ar, k_ts)
    n_ab = each(lambda m_, x: jnp.where(m_, x[:c], 0.0), strict, g_b)
    l_ak = each(lambda m_, x: jnp.where(m_, x[:c], 0.0), strict, g_k)
    p_rb = each(lambda m_, x: jnp.where(m_, x[c:], 0.0), incl, g_b)
    p_rk = each(lambda m_, x: jnp.where(m_, x[c:], 0.0), incl, g_k)

    same_blk = lambda h: (ti // h) == (si // h)
    n_d = each(lambda x: jnp.where(same_blk(INV_BASE), x, 0.0), n_ab)
    steps = max(1, (INV_BASE - 1).bit_length())
    q = n_d
    tinv = n_d
    q = each(lambda x: _dot(x, stk(x)), q) if steps > 1 else q
    for j in range(1, steps):
        last = j == steps - 1
        qs = each(stk, q)
        if last:
            z = each(_dot, tinv, qs)
            tinv = each(lambda t_, x, y: t_ + x + y, tinv, q, z)
        else:
            both = each(lambda x, t_, w_: _dot(jnp.concatenate([x, t_], axis=0), w_), q, tinv, qs)
            tinv = each(lambda t_, x, y: t_ + x + y[c:], tinv, q, both)
            q = each(lambda y: y[:c], both)
    h = INV_BASE
    while h < c:
        link = same_blk(2 * h) & jnp.logical_not(same_blk(h))
        n_off = each(lambda x: jnp.where(link, x, 0.0), n_ab)
        y = each(lambda t_, x: x + _dot(t_, stk(x)), tinv, n_off)
        tinv = each(lambda t_, x: t_ + x + _dot(x, stk(t_)), tinv, y)
        h *= 2
    a_bar = each(lambda t_, x: x + _dot(t_, stk(x)), tinv, a_t)
    lv = each(lambda l_, p, y: _dot(jnp.concatenate([l_, p], axis=0), y), l_ak, p_rk, v_s)
    w = [x[:c] for x in lv]
    u0 = each(lambda t_, x: x + _dot(t_, stk(x)), tinv, w)
    r_bar = each(lambda x, p, y: x + _dot(p, stk(y)), r_t, p_rb, a_bar)
    y0 = each(lambda p, x, z: _dot(p, stk(x)) + z[c:], p_rb, u0, lv)
    zero = jnp.zeros((c, LANES), F32)
    ms = each(lambda ab, u, v_, bh, kh_: _dot_tn(
        jnp.concatenate([jnp.concatenate([ab, u], axis=1), jnp.concatenate([zero, v_], axis=1)], axis=0),
        jnp.concatenate([bh, kh_], axis=0)), a_bar, u0, vv, b_h, k_h)
    m = [jnp.where(same_head, x[:LANES], 0.0) for x in ms]
    s0p = [jnp.where(same_head, x[LANES:], 0.0) for x in ms]
    return [(r_bar[i], y0[i], m[i], s0p[i], jnp.exp(tot[i]), bonus[i]) for i in range(n)]


def _rwkv_scan_kernel(*refs, has_s0, emit_state):
    (r_ref, k_ref, v_ref, lwf_ref, lwb_ref, af_ref, ab_ref,
     kkf_ref, kkb_ref, kaf_ref, kab_ref, rkf_ref, rkb_ref, lnw_ref, lnb_ref) = refs[:15]
    s0_ref = refs[15] if has_s0 else None
    o_ref = refs[15 + has_s0]
    s_ref = refs[16 + has_s0] if emit_state else None
    w_ref, y0_ref, s0p_ref, dec_ref, bonus_ref, yf_ref, yb_ref, st_ref = refs[16 + has_s0 + emit_state:]
    bb, t = r_ref.shape[0], r_ref.shape[1]
    nc = t // CHUNK
    dirs = ((lwf_ref, af_ref, kkf_ref, kaf_ref, rkf_ref, False), (lwb_ref, ab_ref, kkb_ref, kab_ref, rkb_ref, True))
    group = min(bb * nc, PREP_GROUP)
    assert (bb * nc) % group == 0
    chunk_rows = lambda ci: pl.ds(pl.multiple_of(ci * CHUNK, CHUNK), CHUNK)

    def prep(gi, carry):
        insts, where = [], []
        at = [((gi * group + j) // nc, chunk_rows((gi * group + j) % nc)) for j in range(group)]
        for j, (e, sl) in enumerate(at):
            r, k, v = (ref[e, sl, :].astype(F32) for ref in (r_ref, k_ref, v_ref))
            for d, (lw_ref, a_ref, kk_ref, ka_ref, rk_ref, reverse) in enumerate(dirs):
                insts.append((r, k, v, lw_ref[e, sl, :], a_ref[e, sl, :].astype(F32), kk_ref[...], ka_ref[...],
                              rk_ref[...], reverse))
                where.append((d, gi * group + j, e, sl))
        terms = _rwkv_chunk_terms(insts)
        for (d, fi, e, sl), (r_bar, y0, m, s0p, dec, bon) in zip(where, terms):
            rbar_t = jnp.concatenate([r_bar, jnp.zeros_like(r_bar)], axis=0).T
            w_ref[d, fi] = jnp.concatenate([m, rbar_t], axis=1).astype(BF16)
            y0_ref[d, fi] = y0
            s0p_ref[d, fi] = s0p
            dec_ref[d, fi] = jnp.broadcast_to(dec, (8, LANES))
        for j in range(group):
            _, _, e, sl = where[2 * j]
            bonus_ref[e, sl, :] = terms[2 * j][5] + terms[2 * j + 1][5]
        return carry

    lax.fori_loop(0, bb * nc // group, prep, 0)

    seq_steps = SEQ_STEPS if nc % SEQ_STEPS == 0 else 1
    idx = [(d, e) for d in range(2) for e in range(bb)]

    def seq(i, _):
        carry = tuple(st_ref[n] for n in range(len(idx)))
        for s in range(seq_steps):
            cis = (i * seq_steps + s, nc - 1 - i * seq_steps - s)
            sw = [jnp.dot(carry[n].astype(BF16), w_ref[d, e * nc + cis[d]], preferred_element_type=F32)
                  for n, (d, e) in enumerate(idx)]
            for n, (d, e) in enumerate(idx):
                y = sw[n][:, LANES:].T[:CHUNK] + y0_ref[d, e * nc + cis[d]]
                (yf_ref, yb_ref)[d][e, chunk_rows(cis[d]), :] = y
            carry = tuple(carry[n] * dec_ref[d, e * nc + cis[d]][0:1] + sw[n][:, :LANES] + s0p_ref[d, e * nc + cis[d]]
                          for n, (d, e) in enumerate(idx))
        for n, st in enumerate(carry):
            st_ref[n] = st
        return 0

    for n, (d, e) in enumerate(idx):
        st_ref[n] = s0_ref[e, d, 0] if has_s0 else jnp.zeros((LANES, LANES), F32)
    lax.fori_loop(0, nc // seq_steps, seq, 0)
    if emit_state:
        for d in range(2):
            for e in range(bb):
                s = st_ref[d * bb + e]
                s_ref[e, d, 0] = s[:RWKV_HEAD, :RWKV_HEAD]
                s_ref[e, d, 1] = pltpu.roll(s, RWKV_HEAD, 1)[RWKV_HEAD:, :RWKV_HEAD]

    prow = lax.broadcasted_iota(jnp.int32, (LANES, LANES), 0)
    pcol = lax.broadcasted_iota(jnp.int32, (LANES, LANES), 1)
    head_ones = ((prow >= RWKV_HEAD) == (pcol >= RWKV_HEAD)).astype(BF16)
    inv_n = 1.0 / RWKV_HEAD

    def head_mean(x):
        hi = x.astype(BF16)
        lo = (x - hi.astype(F32)).astype(BF16)
        both = jnp.dot(jnp.concatenate([hi, lo], axis=0), head_ones, preferred_element_type=F32)
        return (both[:x.shape[0]] + both[x.shape[0]:]) * inv_n

    fgroup = min(bb * nc, FINISH_GROUP)
    assert (bb * nc) % fgroup == 0

    def finish(gi, carry):
        at = [((gi * fgroup + j) // nc, chunk_rows((gi * fgroup + j) % nc)) for j in range(fgroup)]
        y = [yf_ref[e, sl, :] + yb_ref[e, sl, :] for e, sl in at]
        yc = [x - head_mean(x) for x in y]
        var = [head_mean(x * x) for x in yc]
        for (e, sl), x, vr in zip(at, yc, var):
            o_ref[e, sl, :] = (x * lax.rsqrt(vr + LNX_EPS) * lnw_ref[...] + lnb_ref[...]
                               + bonus_ref[e, sl, :]).astype(BF16)
        return carry

    lax.fori_loop(0, bb * nc // fgroup, finish, 0)


def _rwkv_scan(r, k, v, lw, a, k_k, k_a, r_k, lnx_w, lnx_b, s0, bb, emit_state):
    bsz, t, d = r.shape
    nc = t // CHUNK
    seq = lambda b, p: (b, 0, p)
    seq_b = lambda b, p: (b, 0, RWKV_PAIRS + p)
    par = lambda b, p: (0, p)
    par_b = lambda b, p: (0, RWKV_PAIRS + p)
    sblk = pl.BlockSpec((bb, t, LANES), seq)
    sblk_b = pl.BlockSpec((bb, t, LANES), seq_b)
    pblk = pl.BlockSpec((1, LANES), par)
    pblk_b = pl.BlockSpec((1, LANES), par_b)
    s0blk = pl.BlockSpec((bb, 2, 1, LANES, LANES), lambda b, p: (b, 0, p, 0, 0))
    s_blk = pl.BlockSpec((bb, 2, 2, RWKV_HEAD, RWKV_HEAD), lambda b, p: (b, 0, p, 0, 0))
    s_shape = jax.ShapeDtypeStruct((bsz, 2, RWKV_HEADS, RWKV_HEAD, RWKV_HEAD), F32)
    has_s0 = s0 is not None
    k_k, k_a, r_k = (p.reshape(1, 2 * d) for p in (k_k, k_a, r_k))
    return pl.pallas_call(
        functools.partial(_rwkv_scan_kernel, has_s0=has_s0, emit_state=emit_state),
        grid=(bsz // bb, RWKV_PAIRS),
        in_specs=[sblk, sblk, sblk, sblk, sblk_b, sblk, sblk_b,
                  pblk, pblk_b, pblk, pblk_b, pblk, pblk_b, pblk, pblk] + [s0blk] * has_s0,
        out_specs=[sblk] + [s_blk] * emit_state,
        out_shape=[jax.ShapeDtypeStruct((bsz, t, d), BF16)] + [s_shape] * emit_state,
        scratch_shapes=[pltpu.VMEM((2, bb * nc, LANES, 2 * LANES), BF16), pltpu.VMEM((2, bb * nc, CHUNK, LANES), F32),
                        pltpu.VMEM((2, bb * nc, LANES, LANES), F32),
                        pltpu.VMEM((2, bb * nc, 8, LANES), F32), pltpu.VMEM((bb, t, LANES), F32),
                        pltpu.VMEM((bb, t, LANES), F32), pltpu.VMEM((bb, t, LANES), F32),
                        pltpu.VMEM((2 * bb, LANES, LANES), F32)],
        compiler_params=_params(2),
        name="rwkv_scan",
    )(r, k, v, lw, lw, a, a, k_k, k_k, k_a, k_a, r_k, r_k,
      lnx_w.reshape(1, d), lnx_b.reshape(1, d), *([s0] * has_s0))


def _rwkv_back_kernel(o_ref, g_ref, x_ref, mod_ref, wout_ref, nf_ref, y_ref):
    out = _dot(o_ref[...].astype(F32) * _silu(g_ref[...].astype(F32)), wout_ref[...])
    x2 = x_ref[...] + mod_ref[0][:, 2 * D_MODEL:] * out
    y_ref[...] = _rms(x2) * nf_ref[...]


def _rwkv_back(o, g, x, mod1, mod_idx, tm, w_out, norm_f):
    n, d = x.shape
    nmod = mod1.shape[0]
    row = lambda i: (i, 0)
    const = lambda i: (0, 0)
    return pl.pallas_call(
        _rwkv_back_kernel,
        grid=(n // tm,),
        in_specs=[pl.BlockSpec((tm, d), row), pl.BlockSpec((tm, d), row), pl.BlockSpec((tm, d), row),
                  pl.BlockSpec((1, 1, 3 * d), lambda i: (mod_idx(i), 0, 0)),
                  pl.BlockSpec((d, d), const), pl.BlockSpec((1, d), const)],
        out_specs=pl.BlockSpec((tm, d), row),
        out_shape=jax.ShapeDtypeStruct((n, d), F32),
        compiler_params=_params(1),
        name="rwkv_back",
    )(o, g, x, mod1.reshape(nmod, 1, 3 * d), w_out.astype(BF16), norm_f.reshape(1, d))


def _pair_blockdiag(s):
    b = s.shape[0]
    s = s.reshape(b, 2, RWKV_PAIRS, 2, RWKV_HEAD, RWKV_HEAD)
    z = jnp.zeros_like(s[:, :, :, 0])
    top = jnp.concatenate([s[:, :, :, 0], z], axis=-1)
    bot = jnp.concatenate([z, s[:, :, :, 1]], axis=-1)
    return jnp.concatenate([top, bot], axis=-2)


def _tile_rows(n_seq, t, may_span, rows):
    if t % rows == 0:
        return rows
    if may_span and rows % t == 0 and (n_seq * t) % rows == 0:
        return rows
    return t


def kernel(x_prompt, x_sample, state_gla, state_rwkv, c, c_ctx, w_mod, b_mod, norm_w, gla_w_in, gla_w_a1,
           gla_w_a2, gla_b_a, gla_norm, gla_w_out, rwkv_mu, rwkv_w_rkvg, rwkv_w0, rwkv_w1, rwkv_w2, rwkv_a0,
           rwkv_a1, rwkv_a2, rwkv_k_k, rwkv_k_a, rwkv_r_k, rwkv_lnx_w, rwkv_lnx_b, rwkv_w_out, norm_f):
    d = D_MODEL
    bp, tp, _ = x_prompt.shape
    bs, ts, _ = x_sample.shape
    assert tp % CHUNK == 0 and ts % CHUNK == 0 and ts % GRID_W == 0

    nrows = -(-(1 + bs) // 8) * 8
    cond = jnp.zeros((nrows, d), F32).at[0].set(c_ctx).at[1:1 + bs].set(c)
    cond = cond * (1.0 / (1.0 + jnp.exp(-cond)))
    mod0 = _matmul_bias(cond, w_mod[0], b_mod[0].reshape(1, 3 * d))
    mod1 = _matmul_bias(cond, w_mod[1], b_mod[1].reshape(1, 3 * d))

    def trunk(x3, shared_mod, grid_w, gla_s0, rwkv_s0, gla_bb, rwkv_bb, emit_state):
        b, t, _ = x3.shape
        n = b * t
        tm, tw = (_tile_rows(b, t, shared_mod, rows) for rows in (TOKEN_TILE, WIDE_TILE))
        mod_row = lambda rows: (lambda i: 0) if shared_mod else (lambda i: 1 + i // (t // rows))
        x = x3.reshape(n, d)
        seq = lambda arr: arr.reshape(b, t, arr.shape[-1])
        q, k, v, g, lg = _gla_front(x, mod0, mod_row(tw), tw, norm_w[0], gla_w_in[0], gla_w_a1[0], gla_w_a2[0], gla_b_a[0])
        o, *gla_s = _gla_scan(seq(q), seq(k), seq(v), seq(lg), gla_s0, gla_bb, GLA_HEADS, emit_state)
        x1, h1 = _gla_back(o.reshape(n, d), g, x, mod0, mod1, mod_row(tw), tw, gla_norm[0], gla_w_out[0], norm_w[1])
        r, k, v, g, t_w, t_a = _rwkv_front(h1, tm, t, grid_w, rwkv_mu[0], rwkv_w_rkvg[0], rwkv_w1[0], rwkv_a1[0])
        lw, a = _rwkv_gates(t_w, t_a, tw, rwkv_w0[0], rwkv_w2[0], rwkv_a0[0], rwkv_a2[0])
        o, *rwkv_s = _rwkv_scan(seq(r), seq(k), seq(v), seq(lw), seq(a), rwkv_k_k[0], rwkv_k_a[0], rwkv_r_k[0],
                                rwkv_lnx_w[0], rwkv_lnx_b[0], rwkv_s0, rwkv_bb, emit_state)
        y = _rwkv_back(o.reshape(n, d), g, x1, mod1, mod_row(tw), tw, rwkv_w_out[0], norm_f)
        return y.reshape(b, t, d), gla_s, rwkv_s

    y_prompt, (gla_s,), (rwkv_s,) = trunk(x_prompt, True, None, None, None, min(bp, 2), min(bp, 8), True)

    y_sample, _, _ = trunk(x_sample, False, GRID_W, state_gla[:, 0], _pair_blockdiag(state_rwkv[:, 0]),
                           1, min(bs, 4), False)
    return (y_prompt, y_sample, gla_s[:, None], rwkv_s[:, None])
```

```python
import functools

import jax
import jax.numpy as jnp
from jax import lax
from jax.experimental import pallas as pl
from jax.experimental.pallas import tpu as pltpu

F32 = jnp.float32
BF16 = jnp.bfloat16

D_MODEL = 1024
EPS = 1e-6
GRID_W = 64
GLA_HEADS = 4
GLA_DK = 512
GLA_DV = 1024
GLA_DKH = 128
GLA_DVH = 256
GLA_GATE_RANK = 16
GLA_GATE_NORM = 16.0
RWKV_HEAD = 64
RWKV_HEADS = 16
RWKV_PAIRS = 8
RWKV_RANK = 64
LNX_EPS = 64e-5
RWKV_DECAY_SCALE = 0.6065306597126334
CHUNK = 64
SEQ_STEPS = 4
GLA_STEPS = 4
INV_BASE = 8
TOKEN_TILE = 512
WIDE_TILE = 1024
FINISH_GROUP = 32
PREP_GROUP = 8
LANES = 128
VMEM_LIMIT = 56 * 1024 * 1024


def _dot(a, b):
    return jnp.dot(a.astype(BF16), b.astype(BF16), preferred_element_type=F32)


def _dot_nt(a, b):
    return lax.dot_general(a.astype(BF16), b.astype(BF16), (((1,), (1,)), ((), ())),
                           preferred_element_type=F32)


def _dot_tn(a, b):
    return lax.dot_general(a.astype(BF16), b.astype(BF16), (((0,), (0,)), ((), ())),
                           preferred_element_type=F32)


def _tri_dot(tri, x):
    hi = x.astype(BF16)
    lo = (x - hi.astype(F32)).astype(BF16)
    t = tri.astype(BF16)
    return jnp.dot(t, hi, preferred_element_type=F32) + jnp.dot(t, lo, preferred_element_type=F32)


def _log_sigmoid(z):
    return jnp.minimum(z, 0.0) - jnp.log(1.0 + jnp.exp(-jnp.abs(z)))


def _sigmoid(z):
    return 0.5 * jnp.tanh(0.5 * z) + 0.5


def _silu(z):
    return z * _sigmoid(z)


def _rms(x):
    return x * lax.rsqrt(jnp.mean(x * x, axis=-1, keepdims=True) + EPS)


def _params(n_grid_dims):
    return pltpu.CompilerParams(dimension_semantics=("arbitrary",) * n_grid_dims,
                                vmem_limit_bytes=VMEM_LIMIT)


def _mm_kernel(x_ref, w_ref, b_ref, o_ref):
    o_ref[...] = _dot(x_ref[...], w_ref[...]) + b_ref[...]


def _matmul_bias(x, w, b):
    m, k = x.shape
    n = w.shape[1]
    return pl.pallas_call(
        _mm_kernel,
        grid=(1,),
        in_specs=[pl.BlockSpec((m, k), lambda i: (0, 0)),
                  pl.BlockSpec((k, n), lambda i: (0, 0)),
                  pl.BlockSpec((1, n), lambda i: (0, 0))],
        out_specs=pl.BlockSpec((m, n), lambda i: (0, 0)),
        out_shape=jax.ShapeDtypeStruct((m, n), F32),
        compiler_params=_params(1),
        name="mod_matmul",
    )(x, w.astype(BF16), b)


def _gla_front_kernel(x_ref, mod_ref, nw_ref, win_ref, wa1_ref, wa2_ref, ba_ref,
                      q_ref, k_ref, v_ref, g_ref, lg_ref):
    d = D_MODEL
    mod = mod_ref[0]
    shift, scale = mod[:, :d], mod[:, d:2 * d]
    h = _rms(x_ref[...]) * nw_ref[...] * (1.0 + scale) + shift
    hb = h.astype(BF16)
    proj = lambda lo, hi: jnp.dot(hb, win_ref[:, lo:hi], preferred_element_type=F32)
    q_ref[...] = (proj(0, GLA_DK) * (GLA_DKH ** -0.5)).astype(BF16)
    k_ref[...] = proj(GLA_DK, 2 * GLA_DK).astype(BF16)
    v_ref[...] = proj(2 * GLA_DK, 2 * GLA_DK + GLA_DV).astype(BF16)
    g_ref[...] = proj(2 * GLA_DK + GLA_DV, 2 * GLA_DK + 2 * GLA_DV).astype(BF16)
    t = jnp.dot(hb, wa1_ref[...], preferred_element_type=F32)
    z = _dot(t, wa2_ref[...]) + ba_ref[...]
    lg_ref[...] = _log_sigmoid(z) * (1.0 / GLA_GATE_NORM)


def _gla_front(x, mod, mod_idx, tm, norm_w, w_in, w_a1, w_a2, b_a):
    n, d = x.shape
    nmod = mod.shape[0]
    wa1 = jnp.zeros((d, LANES), F32).at[:, :GLA_GATE_RANK].set(w_a1[0]).at[:, GLA_GATE_RANK:2 * GLA_GATE_RANK].set(w_a1[1])
    wa2 = jnp.zeros((LANES, 2 * GLA_DK), F32).at[:GLA_GATE_RANK, :GLA_DK].set(w_a2[0])
    wa2 = wa2.at[GLA_GATE_RANK:2 * GLA_GATE_RANK, GLA_DK:].set(w_a2[1])
    ba = b_a.reshape(1, 2 * GLA_DK)
    row = lambda i: (i, 0)
    const = lambda i: (0, 0)
    n_in = w_in.shape[1]
    outs = pl.pallas_call(
        _gla_front_kernel,
        grid=(n // tm,),
        in_specs=[pl.BlockSpec((tm, d), row),
                  pl.BlockSpec((1, 1, 3 * d), lambda i: (mod_idx(i), 0, 0)),
                  pl.BlockSpec((1, d), const),
                  pl.BlockSpec((d, n_in), const),
                  pl.BlockSpec((d, LANES), const),
                  pl.BlockSpec((LANES, 2 * GLA_DK), const),
                  pl.BlockSpec((1, 2 * GLA_DK), const)],
        out_specs=[pl.BlockSpec((tm, GLA_DK), row), pl.BlockSpec((tm, GLA_DK), row),
                   pl.BlockSpec((tm, GLA_DV), row), pl.BlockSpec((tm, GLA_DV), row),
                   pl.BlockSpec((tm, 2 * GLA_DK), row)],
        out_shape=[jax.ShapeDtypeStruct((n, GLA_DK), BF16), jax.ShapeDtypeStruct((n, GLA_DK), BF16),
                   jax.ShapeDtypeStruct((n, GLA_DV), BF16), jax.ShapeDtypeStruct((n, GLA_DV), BF16),
                   jax.ShapeDtypeStruct((n, 2 * GLA_DK), F32)],
        compiler_params=_params(1),
        name="gla_front",
    )(x, mod.reshape(nmod, 1, 3 * d), norm_w.reshape(1, d), w_in.astype(BF16), wa1.astype(BF16),
      wa2.astype(BF16), ba)
    return outs


def _gla_chunks(chains):
    flat = [(x, ch[2]) for ch in chains for x in ch[0]]
    c = flat[0][0][0].shape[0]
    dk = flat[0][0][1].shape[1]
    row = lax.broadcasted_iota(jnp.int32, (c, c), 0)
    col = lax.broadcasted_iota(jnp.int32, (c, c), 1)
    keep = [(col >= row) if rv else (col <= row) for _, rv in flat]
    b = [_tri_dot(kp.astype(F32), x[3]) for kp, (x, _) in zip(keep, flat)]
    b_end = [bb[0:1] if rv else bb[c - 1:c] for bb, (_, rv) in zip(b, flat)]
    qb = [(x[0] * jnp.exp(bb)).astype(BF16) for (x, _), bb in zip(flat, b)]
    kb = [x[1] * jnp.exp(-bb) for (x, _), bb in zip(flat, b)]
    kd = [x[1] * jnp.exp(be - bb) for (x, _), bb, be in zip(flat, b, b_end)]
    scores = [jnp.where(kp, _dot_nt(x, y), 0.0).astype(BF16) for kp, x, y in zip(keep, qb, kb)]
    upd = [_dot_tn(x, y[2]) for (y, _), x in zip(flat, kd)]
    decay = [jnp.broadcast_to(jnp.exp(be), (dk, dk)).T for be in b_end]
    n_steps = len(chains[0][0])
    st = [ch[1] for ch in chains]
    outs = [[] for _ in chains]
    for s in range(n_steps):
        at = [ci * n_steps + s for ci in range(len(chains))]
        for ci, f in enumerate(at):
            outs[ci].append(_dot(jnp.concatenate([scores[f], qb[f]], axis=1),
                                 jnp.concatenate([flat[f][0][2], st[ci].astype(BF16)], axis=0)))
        st = [st[ci] * jnp.concatenate([decay[f]] * (st[ci].shape[1] // dk), axis=1) + upd[f]
              for ci, f in enumerate(at)]
    return list(zip(outs, st))


def _gla_scan_kernel(*refs, has_s0, emit_state):
    q_ref, k_ref, v_ref, lf_ref, lb_ref = refs[:5]
    s0_ref = refs[5] if has_s0 else None
    o_ref = refs[5 + has_s0]
    s_ref = refs[6 + has_s0] if emit_state else None
    acc_ref, st_ref = refs[-2:]
    bb, t = q_ref.shape[0], q_ref.shape[1]
    hb = q_ref.shape[2] // GLA_DKH
    nc = t // CHUNK
    half = nc // 2
    kh = lambda h: slice(h * GLA_DKH, (h + 1) * GLA_DKH)
    vh = lambda h: slice(h * GLA_DVH, (h + 1) * GLA_DVH)

    steps = GLA_STEPS
    while half % steps:
        steps //= 2

    def both(accumulate):
        def body(i, carry):
            sls = [[pl.ds(pl.multiple_of(ci * CHUNK, CHUNK), CHUNK)
                    for ci in ((i * steps + s, nc - 1 - i * steps - s)[d] for s in range(steps))] for d in range(2)]
            chains = [([(q_ref[e, sl, kh(h)].astype(F32), k_ref[e, sl, kh(h)].astype(F32), v_ref[e, sl, vh(h)],
                         (lf_ref, lb_ref)[d][e, sl, kh(h)]) for sl in sls[d]], st_ref[n], d == 1)
                      for n, (e, d, h) in enumerate(idx)]
            res = _gla_chunks(chains)
            for n, (e, d, h) in enumerate(idx):
                for sl, o in zip(sls[d], res[n][0]):
                    if accumulate:
                        o_ref[e, sl, vh(h)] = (acc_ref[e, sl, vh(h)] + o).astype(BF16)
                    else:
                        acc_ref[e, sl, vh(h)] = o
            for n, (_, st) in enumerate(res):
                st_ref[n] = st
            return carry
        return body

    idx = [(e, d, h) for e in range(bb) for d in range(2) for h in range(hb)]
    for n, (e, d, h) in enumerate(idx):
        st_ref[n] = s0_ref[e, d, h] if has_s0 else jnp.zeros((GLA_DKH, GLA_DVH), F32)
    lax.fori_loop(0, half // steps, both(False), 0)
    lax.fori_loop(half // steps, nc // steps, both(True), 0)
    if emit_state:
        for n, (e, d, h) in enumerate(idx):
            s_ref[e, d, h] = st_ref[n]


def _gla_scan(q, k, v, lg, s0, bb, hb, emit_state):
    bsz, t, _ = q.shape
    ng = GLA_HEADS // hb
    hd = lambda b, h: (b, 0, h)
    kblk = pl.BlockSpec((bb, t, hb * GLA_DKH), hd)
    vblk = pl.BlockSpec((bb, t, hb * GLA_DVH), hd)
    sblk = pl.BlockSpec((bb, 2, hb, GLA_DKH, GLA_DVH), lambda b, h: (b, 0, h, 0, 0))
    s_shape = jax.ShapeDtypeStruct((bsz, 2, GLA_HEADS, GLA_DKH, GLA_DVH), F32)
    has_s0 = s0 is not None
    return pl.pallas_call(
        functools.partial(_gla_scan_kernel, has_s0=has_s0, emit_state=emit_state),
        grid=(bsz // bb, ng),
        in_specs=[kblk, kblk, vblk, kblk, pl.BlockSpec((bb, t, hb * GLA_DKH), lambda b, h: (b, 0, ng + h))]
        + [sblk] * has_s0,
        out_specs=[vblk] + [sblk] * emit_state,
        out_shape=[jax.ShapeDtypeStruct((bsz, t, GLA_DV), BF16)] + [s_shape] * emit_state,
        scratch_shapes=[pltpu.VMEM((bb, t, hb * GLA_DVH), F32), pltpu.VMEM((2 * bb * hb, GLA_DKH, GLA_DVH), F32)],
        compiler_params=_params(2),
        name="gla_scan",
    )(q, k, v, lg, lg, *([s0] * has_s0))


def _gla_back_kernel(o_ref, g_ref, x_ref, mod0_ref, mod1_ref, gn_ref, wout_ref, nw1_ref, x1_ref, h1_ref):
    d = D_MODEL
    o = o_ref[...].astype(F32)
    gn = gn_ref[...]
    parts = [_rms(o[:, h * GLA_DVH:(h + 1) * GLA_DVH]) * gn for h in range(GLA_HEADS)]
    on = jnp.concatenate(parts, axis=-1) * _silu(g_ref[...].astype(F32))
    out = _dot(on, wout_ref[...])
    gate = mod0_ref[0][:, 2 * d:]
    x1 = x_ref[...] + gate * out
    x1_ref[...] = x1
    mod1 = mod1_ref[0]
    h1_ref[...] = (_rms(x1) * nw1_ref[...] * (1.0 + mod1[:, d:2 * d]) + mod1[:, :d]).astype(BF16)


def _gla_back(o, g, x, mod0, mod1, mod_idx, tm, gla_norm, w_out, norm_w1):
    n, d = x.shape
    nmod = mod0.shape[0]
    row = lambda i: (i, 0)
    const = lambda i: (0, 0)
    modspec = pl.BlockSpec((1, 1, 3 * d), lambda i: (mod_idx(i), 0, 0))
    return pl.pallas_call(
        _gla_back_kernel,
        grid=(n // tm,),
        in_specs=[pl.BlockSpec((tm, d), row), pl.BlockSpec((tm, d), row), pl.BlockSpec((tm, d), row),
                  modspec, modspec,
                  pl.BlockSpec((1, GLA_DVH), const), pl.BlockSpec((d, d), const), pl.BlockSpec((1, d), const)],
        out_specs=[pl.BlockSpec((tm, d), row), pl.BlockSpec((tm, d), row)],
        out_shape=[jax.ShapeDtypeStruct((n, d), F32), jax.ShapeDtypeStruct((n, d), BF16)],
        compiler_params=_params(1),
        name="gla_back",
    )(o, g, x, mod0.reshape(nmod, 1, 3 * d), mod1.reshape(nmod, 1, 3 * d), gla_norm.reshape(1, GLA_DVH),
      w_out.astype(BF16), norm_w1.reshape(1, d))


def _token_shift(h, up_ref, dn_ref, seq_len, grid_w):
    tm = h.shape[0]
    pos = lax.broadcasted_iota(jnp.int32, (tm, 1), 0) + pl.program_id(0) * tm
    prev = pltpu.roll(h, 1, 0)
    nxt = pltpu.roll(h, tm - 1, 0)
    t = pos % seq_len
    if grid_w is None:
        return 0.5 * (jnp.where(t == 0, 0.0, prev) + jnp.where(t == seq_len - 1, 0.0, nxt))
    col = pos % grid_w
    up = jnp.concatenate([up_ref[...].astype(F32), h[:tm - grid_w]], axis=0)
    dn = jnp.concatenate([h[grid_w:], dn_ref[...].astype(F32)], axis=0)
    up = jnp.where(t < grid_w, 0.0, up)
    dn = jnp.where(t >= seq_len - grid_w, 0.0, dn)
    left = jnp.where(col == 0, 0.0, prev)
    right = jnp.where(col == grid_w - 1, 0.0, nxt)
    return 0.25 * (up + dn + left + right)


def _rwkv_front_kernel(*refs, seq_len, grid_w):
    n_h = 1 if grid_w is None else 3
    h_ref, up_ref, dn_ref = (tuple(refs[:n_h]) + (None, None))[:3]
    (mu_ref, wrkvg_ref, w1_ref, w2_ref, w0_ref, a1_ref, a2_ref, a0_ref,
     r_ref, k_ref, v_ref, g_ref, lw_ref, a_ref) = refs[n_h:]
    h = h_ref[...].astype(F32)
    dh = _token_shift(h, up_ref, dn_ref, seq_len, grid_w) - h
    mix = lambda p: (h + dh * mu_ref[p:p + 1, :]).astype(BF16)
    r_ref[...] = jnp.dot(mix(0), wrkvg_ref[0], preferred_element_type=F32).astype(BF16)
    k_ref[...] = jnp.dot(mix(1), wrkvg_ref[1], preferred_element_type=F32).astype(BF16)
    v_ref[...] = jnp.dot(mix(2), wrkvg_ref[2], preferred_element_type=F32).astype(BF16)
    g_ref[...] = jnp.dot(mix(3), wrkvg_ref[3], preferred_element_type=F32).astype(BF16)
    tw = jnp.tanh(jnp.dot(mix(4), w1_ref[...], preferred_element_type=F32))
    lw_ref[...] = -RWKV_DECAY_SCALE * _sigmoid(w0_ref[...] + _dot(tw, w2_ref[...]))
    ta = jnp.dot(mix(5), a1_ref[...], preferred_element_type=F32)
    a_ref[...] = _sigmoid(a0_ref[...] + _dot(ta, a2_ref[...])).astype(BF16)


def _rwkv_front(h, tm, seq_len, grid_w, mu, w_rkvg, w0, w1, w2, a0, a1, a2):
    n, d = h.shape
    rk = RWKV_RANK
    cat1 = lambda w: jnp.concatenate([w[0], w[1]], axis=1)
    bd2 = lambda w: jnp.zeros((2 * rk, 2 * d), F32).at[:rk, :d].set(w[0]).at[rk:, d:].set(w[1])
    row = lambda i: (i, 0)
    const = lambda i: (0, 0)
    if grid_w is None:
        assert tm % seq_len == 0
        h_specs, h_args = [pl.BlockSpec((tm, d), row)], [h]
    else:
        assert tm % grid_w == 0 and seq_len % tm == 0
        rows_per_tile, last = tm // grid_w, n // grid_w - 1
        h_specs = [pl.BlockSpec((tm, d), row),
                   pl.BlockSpec((grid_w, d), lambda i: (jnp.maximum(i * rows_per_tile - 1, 0), 0)),
                   pl.BlockSpec((grid_w, d), lambda i: (jnp.minimum((i + 1) * rows_per_tile, last), 0))]
        h_args = [h, h, h]
    return pl.pallas_call(
        functools.partial(_rwkv_front_kernel, seq_len=seq_len, grid_w=grid_w),
        grid=(n // tm,),
        in_specs=h_specs + [
            pl.BlockSpec((8, d), const),
            pl.BlockSpec((4, d, d), lambda i: (0, 0, 0)),
            pl.BlockSpec((d, 2 * rk), const), pl.BlockSpec((2 * rk, 2 * d), const), pl.BlockSpec((1, 2 * d), const),
            pl.BlockSpec((d, 2 * rk), const), pl.BlockSpec((2 * rk, 2 * d), const), pl.BlockSpec((1, 2 * d), const)],
        out_specs=[pl.BlockSpec((tm, d), row)] * 4 + [pl.BlockSpec((tm, 2 * d), row)] * 2,
        out_shape=[jax.ShapeDtypeStruct((n, d), BF16)] * 4
        + [jax.ShapeDtypeStruct((n, 2 * d), F32), jax.ShapeDtypeStruct((n, 2 * d), BF16)],
        compiler_params=_params(1),
        name="rwkv_front",
    )(*h_args, jnp.zeros((8, d), F32).at[:6].set(mu), w_rkvg.astype(BF16),
      cat1(w1).astype(BF16), bd2(w2).astype(BF16), w0.reshape(1, 2 * d),
      cat1(a1).astype(BF16), bd2(a2).astype(BF16), a0.reshape(1, 2 * d))


def _seg_sum(x, head0):
    s0 = jnp.sum(jnp.where(head0, x, 0.0), axis=-1, keepdims=True)
    s1 = jnp.sum(jnp.where(head0, 0.0, x), axis=-1, keepdims=True)
    return jnp.where(head0, s0, s1)


def _stack(x, head0):
    return jnp.concatenate([jnp.where(head0, x, 0.0), jnp.where(head0, 0.0, x)], axis=0)


def _rwkv_chunk_terms(insts):
    n = len(insts)
    c = insts[0][0].shape[0]
    lane = lax.broadcasted_iota(jnp.int32, (c, LANES), 1)
    head0 = lane < RWKV_HEAD
    row = lax.broadcasted_iota(jnp.int32, (c, c), 0)
    col = lax.broadcasted_iota(jnp.int32, (c, c), 1)
    ti = lax.broadcasted_iota(jnp.int32, (c, LANES), 0)
    si = lane & (c - 1)
    prow = lax.broadcasted_iota(jnp.int32, (LANES, LANES), 0)
    pcol = lax.broadcasted_iota(jnp.int32, (LANES, LANES), 1)
    same_head = (prow >= RWKV_HEAD) == (pcol >= RWKV_HEAD)
    each = lambda f, *ls: [f(*xs) for xs in zip(*ls)]
    rev = [x[8] for x in insts]
    stk = lambda x: _stack(x, head0).astype(BF16)

    cum = [_tri_dot(((col >= row) if x[8] else (col <= row)).astype(F32), x[3]) for x in insts]
    tot = [cm[0:1] if rv else cm[c - 1:c] for cm, rv in zip(cum, rev)]

    a_t, r_t, b_ts, k_ts, b_h, k_h, v_s, vv, bonus = [], [], [], [], [], [], [], [], []
    for (r, k, v, lw, a, k_k, k_a, r_k, _), cm, tt in zip(insts, cum, tot):
        kk = k * k_k
        kk = kk * lax.rsqrt(jnp.maximum(_seg_sum(kk * kk, head0), 1e-24))
        kd = k * (1.0 + (a - 1.0) * k_a)
        bv = kk * a
        bonus.append(_seg_sum(r * kd * r_k, head0) * v)
        e_neg = jnp.exp(-cm)
        e_tot = jnp.exp(tt)
        b_neg, k_neg = bv * e_neg, kd * e_neg
        a_t.append(-kk * jnp.exp(cm - lw))
        r_t.append(r * jnp.exp(cm))
        b_ts.append(stk(b_neg))
        k_ts.append(stk(k_neg))
        b_h.append(b_neg * e_tot)
        k_h.append(k_neg * e_tot)
        v_s.append(stk(v))
        vv.append(v)

    strict = [(si > ti) if rv else (si < ti) for rv in rev]
    incl = [(si >= ti) if rv else (si <= ti) for rv in rev]
    ar = each(lambda x, y: jnp.concatenate([x, y], axis=0).astype(BF16), a_t, r_t)
    g_b = each(_dot_nt, ar, b_ts)
    g_k = each(_dot_nt, ar, k_ts)
    n_ab = each(lambda m_, x: jnp.where(m_, x[:c], 0.0), strict, g_b)
    l_ak = each(lambda m_, x: jnp.where(m_, x[:c], 0.0), strict, g_k)
    p_rb = each(lambda m_, x: jnp.where(m_, x[c:], 0.0), incl, g_b)
    p_rk = each(lambda m_, x: jnp.where(m_, x[c:], 0.0), incl, g_k)

    same_blk = lambda h: (ti // h) == (si // h)
    n_d = each(lambda x: jnp.where(same_blk(INV_BASE), x, 0.0), n_ab)
    steps = max(1, (INV_BASE - 1).bit_length())
    q = n_d
    tinv = n_d
    q = each(lambda x: _dot(x, stk(x)), q) if steps > 1 else q
    for j in range(1, steps):
        last = j == steps - 1
        qs = each(stk, q)
        if last:
            z = each(_dot, tinv, qs)
            tinv = each(lambda t_, x, y: t_ + x + y, tinv, q, z)
        else:
            both = each(lambda x, t_, w_: _dot(jnp.concatenate([x, t_], axis=0), w_), q, tinv, qs)
            tinv = each(lambda t_, x, y: t_ + x + y[c:], tinv, q, both)
            q = each(lambda y: y[:c], both)
    h = INV_BASE
    while h < c:
        link = same_blk(2 * h) & jnp.logical_not(same_blk(h))
        n_off = each(lambda x: jnp.where(link, x, 0.0), n_ab)
        y = each(lambda t_, x: x + _dot(t_, stk(x)), tinv, n_off)
        tinv = each(lambda t_, x: t_ + x + _dot(x, stk(t_)), tinv, y)
        h *= 2
    a_bar = each(lambda t_, x: x + _dot(t_, stk(x)), tinv, a_t)
    lv = each(lambda l_, p, y: _dot(jnp.concatenate([l_, p], axis=0), y), l_ak, p_rk, v_s)
    w = [x[:c] for x in lv]
    u0 = each(lambda t_, x: x + _dot(t_, stk(x)), tinv, w)
    r_bar = each(lambda x, p, y: x + _dot(p, stk(y)), r_t, p_rb, a_bar)
    y0 = each(lambda p, x, z: _dot(p, stk(x)) + z[c:], p_rb, u0, lv)
    zero = jnp.zeros((c, LANES), F32)
    ms = each(lambda ab, u, v_, bh, kh_: _dot_tn(
        jnp.concatenate([jnp.concatenate([ab, u], axis=1), jnp.concatenate([zero, v_], axis=1)], axis=0),
        jnp.concatenate([bh, kh_], axis=0)), a_bar, u0, vv, b_h, k_h)
    m = [jnp.where(same_head, x[:LANES], 0.0) for x in ms]
    s0p = [jnp.where(same_head, x[LANES:], 0.0) for x in ms]
    return [(r_bar[i], y0[i], m[i], s0p[i], jnp.exp(tot[i]), bonus[i]) for i in range(n)]


def _rwkv_scan_kernel(*refs, has_s0, emit_state):
    (r_ref, k_ref, v_ref, lwf_ref, lwb_ref, af_ref, ab_ref,
     kkf_ref, kkb_ref, kaf_ref, kab_ref, rkf_ref, rkb_ref, lnw_ref, lnb_ref) = refs[:15]
    s0_ref = refs[15] if has_s0 else None
    o_ref = refs[15 + has_s0]
    s_ref = refs[16 + has_s0] if emit_state else None
    w_ref, y0_ref, s0p_ref, dec_ref, bonus_ref, yf_ref, yb_ref, st_ref = refs[16 + has_s0 + emit_state:]
    bb, t = r_ref.shape[0], r_ref.shape[1]
    nc = t // CHUNK
    dirs = ((lwf_ref, af_ref, kkf_ref, kaf_ref, rkf_ref, False), (lwb_ref, ab_ref, kkb_ref, kab_ref, rkb_ref, True))
    group = min(bb * nc, PREP_GROUP)
    assert (bb * nc) % group == 0
    chunk_rows = lambda ci: pl.ds(pl.multiple_of(ci * CHUNK, CHUNK), CHUNK)

    def prep(gi, carry):
        insts, where = [], []
        at = [((gi * group + j) // nc, chunk_rows((gi * group + j) % nc)) for j in range(group)]
        for j, (e, sl) in enumerate(at):
            r, k, v = (ref[e, sl, :].astype(F32) for ref in (r_ref, k_ref, v_ref))
            for d, (lw_ref, a_ref, kk_ref, ka_ref, rk_ref, reverse) in enumerate(dirs):
                insts.append((r, k, v, lw_ref[e, sl, :], a_ref[e, sl, :].astype(F32), kk_ref[...], ka_ref[...],
                              rk_ref[...], reverse))
                where.append((d, gi * group + j, e, sl))
        terms = _rwkv_chunk_terms(insts)
        for (d, fi, e, sl), (r_bar, y0, m, s0p, dec, bon) in zip(where, terms):
            rbar_t = jnp.concatenate([r_bar, jnp.zeros_like(r_bar)], axis=0).T
            w_ref[d, fi] = jnp.concatenate([m, rbar_t], axis=1).astype(BF16)
            y0_ref[d, fi] = y0
            s0p_ref[d, fi] = s0p
            dec_ref[d, fi] = jnp.broadcast_to(dec, (8, LANES))
        for j in range(group):
            _, _, e, sl = where[2 * j]
            bonus_ref[e, sl, :] = terms[2 * j][5] + terms[2 * j + 1][5]
        return carry

    lax.fori_loop(0, bb * nc // group, prep, 0)

    seq_steps = SEQ_STEPS if nc % SEQ_STEPS == 0 else 1
    idx = [(d, e) for d in range(2) for e in range(bb)]

    def seq(i, _):
        carry = tuple(st_ref[n] for n in range(len(idx)))
        for s in range(seq_steps):
            cis = (i * seq_steps + s, nc - 1 - i * seq_steps - s)
            sw = [jnp.dot(carry[n].astype(BF16), w_ref[d, e * nc + cis[d]], preferred_element_type=F32)
                  for n, (d, e) in enumerate(idx)]
            for n, (d, e) in enumerate(idx):
                y = sw[n][:, LANES:].T[:CHUNK] + y0_ref[d, e * nc + cis[d]]
                (yf_ref, yb_ref)[d][e, chunk_rows(cis[d]), :] = y
            carry = tuple(carry[n] * dec_ref[d, e * nc + cis[d]][0:1] + sw[n][:, :LANES] + s0p_ref[d, e * nc + cis[d]]
                          for n, (d, e) in enumerate(idx))
        for n, st in enumerate(carry):
            st_ref[n] = st
        return 0

    for n, (d, e) in enumerate(idx):
        st_ref[n] = s0_ref[e, d, 0] if has_s0 else jnp.zeros((LANES, LANES), F32)
    lax.fori_loop(0, nc // seq_steps, seq, 0)
    if emit_state:
        for d in range(2):
            for e in range(bb):
                s = st_ref[d * bb + e]
                s_ref[e, d, 0] = s[:RWKV_HEAD, :RWKV_HEAD]
                s_ref[e, d, 1] = pltpu.roll(s, RWKV_HEAD, 1)[RWKV_HEAD:, :RWKV_HEAD]

    prow = lax.broadcasted_iota(jnp.int32, (LANES, LANES), 0)
    pcol = lax.broadcasted_iota(jnp.int32, (LANES, LANES), 1)
    head_ones = ((prow >= RWKV_HEAD) == (pcol >= RWKV_HEAD)).astype(BF16)
    inv_n = 1.0 / RWKV_HEAD

    def head_mean(x):
        hi = x.astype(BF16)
        lo = (x - hi.astype(F32)).astype(BF16)
        both = jnp.dot(jnp.concatenate([hi, lo], axis=0), head_ones, preferred_element_type=F32)
        return (both[:x.shape[0]] + both[x.shape[0]:]) * inv_n

    fgroup = min(bb * nc, FINISH_GROUP)
    assert (bb * nc) % fgroup == 0

    def finish(gi, carry):
        at = [((gi * fgroup + j) // nc, chunk_rows((gi * fgroup + j) % nc)) for j in range(fgroup)]
        y = [yf_ref[e, sl, :] + yb_ref[e, sl, :] for e, sl in at]
        yc = [x - head_mean(x) for x in y]
        var = [head_mean(x * x) for x in yc]
        for (e, sl), x, vr in zip(at, yc, var):
            o_ref[e, sl, :] = (x * lax.rsqrt(vr + LNX_EPS) * lnw_ref[...] + lnb_ref[...]
                               + bonus_ref[e, sl, :]).astype(BF16)
        return carry

    lax.fori_loop(0, bb * nc // fgroup, finish, 0)


def _rwkv_scan(r, k, v, lw, a, k_k, k_a, r_k, lnx_w, lnx_b, s0, bb, emit_state):
    bsz, t, d = r.shape
    nc = t // CHUNK
    seq = lambda b, p: (b, 0, p)
    seq_b = lambda b, p: (b, 0, RWKV_PAIRS + p)
    par = lambda b, p: (0, p)
    par_b = lambda b, p: (0, RWKV_PAIRS + p)
    sblk = pl.BlockSpec((bb, t, LANES), seq)
    sblk_b = pl.BlockSpec((bb, t, LANES), seq_b)
    pblk = pl.BlockSpec((1, LANES), par)
    pblk_b = pl.BlockSpec((1, LANES), par_b)
    s0blk = pl.BlockSpec((bb, 2, 1, LANES, LANES), lambda b, p: (b, 0, p, 0, 0))
    s_blk = pl.BlockSpec((bb, 2, 2, RWKV_HEAD, RWKV_HEAD), lambda b, p: (b, 0, p, 0, 0))
    s_shape = jax.ShapeDtypeStruct((bsz, 2, RWKV_HEADS, RWKV_HEAD, RWKV_HEAD), F32)
    has_s0 = s0 is not None
    k_k, k_a, r_k = (p.reshape(1, 2 * d) for p in (k_k, k_a, r_k))
    return pl.pallas_call(
        functools.partial(_rwkv_scan_kernel, has_s0=has_s0, emit_state=emit_state),
        grid=(bsz // bb, RWKV_PAIRS),
        in_specs=[sblk, sblk, sblk, sblk, sblk_b, sblk, sblk_b,
                  pblk, pblk_b, pblk, pblk_b, pblk, pblk_b, pblk, pblk] + [s0blk] * has_s0,
        out_specs=[sblk] + [s_blk] * emit_state,
        out_shape=[jax.ShapeDtypeStruct((bsz, t, d), BF16)] + [s_shape] * emit_state,
        scratch_shapes=[pltpu.VMEM((2, bb * nc, LANES, 2 * LANES), BF16), pltpu.VMEM((2, bb * nc, CHUNK, LANES), F32),
                        pltpu.VMEM((2, bb * nc, LANES, LANES), F32),
                        pltpu.VMEM((2, bb * nc, 8, LANES), F32), pltpu.VMEM((bb, t, LANES), F32),
                        pltpu.VMEM((bb, t, LANES), F32), pltpu.VMEM((bb, t, LANES), F32),
                        pltpu.VMEM((2 * bb, LANES, LANES), F32)],
        compiler_params=_params(2),
        name="rwkv_scan",
    )(r, k, v, lw, lw, a, a, k_k, k_k, k_a, k_a, r_k, r_k,
      lnx_w.reshape(1, d), lnx_b.reshape(1, d), *([s0] * has_s0))


def _rwkv_back_kernel(o_ref, g_ref, x_ref, mod_ref, wout_ref, nf_ref, y_ref):
    out = _dot(o_ref[...].astype(F32) * _silu(g_ref[...].astype(F32)), wout_ref[...])
    x2 = x_ref[...] + mod_ref[0][:, 2 * D_MODEL:] * out
    y_ref[...] = _rms(x2) * nf_ref[...]


def _rwkv_back(o, g, x, mod1, mod_idx, tm, w_out, norm_f):
    n, d = x.shape
    nmod = mod1.shape[0]
    row = lambda i: (i, 0)
    const = lambda i: (0, 0)
    return pl.pallas_call(
        _rwkv_back_kernel,
        grid=(n // tm,),
        in_specs=[pl.BlockSpec((tm, d), row), pl.BlockSpec((tm, d), row), pl.BlockSpec((tm, d), row),
                  pl.BlockSpec((1, 1, 3 * d), lambda i: (mod_idx(i), 0, 0)),
                  pl.BlockSpec((d, d), const), pl.BlockSpec((1, d), const)],
        out_specs=pl.BlockSpec((tm, d), row),
        out_shape=jax.ShapeDtypeStruct((n, d), F32),
        compiler_params=_params(1),
        name="rwkv_back",
    )(o, g, x, mod1.reshape(nmod, 1, 3 * d), w_out.astype(BF16), norm_f.reshape(1, d))


def _pair_blockdiag(s):
    b = s.shape[0]
    s = s.reshape(b, 2, RWKV_PAIRS, 2, RWKV_HEAD, RWKV_HEAD)
    z = jnp.zeros_like(s[:, :, :, 0])
    top = jnp.concatenate([s[:, :, :, 0], z], axis=-1)
    bot = jnp.concatenate([z, s[:, :, :, 1]], axis=-1)
    return jnp.concatenate([top, bot], axis=-2)


def _tile_rows(n_seq, t, may_span, rows):
    if t % rows == 0:
        return rows
    if may_span and rows % t == 0 and (n_seq * t) % rows == 0:
        return rows
    return t


def kernel(x_prompt, x_sample, state_gla, state_rwkv, c, c_ctx, w_mod, b_mod, norm_w, gla_w_in, gla_w_a1,
           gla_w_a2, gla_b_a, gla_norm, gla_w_out, rwkv_mu, rwkv_w_rkvg, rwkv_w0, rwkv_w1, rwkv_w2, rwkv_a0,
           rwkv_a1, rwkv_a2, rwkv_k_k, rwkv_k_a, rwkv_r_k, rwkv_lnx_w, rwkv_lnx_b, rwkv_w_out, norm_f):
    d = D_MODEL
    bp, tp, _ = x_prompt.shape
    bs, ts, _ = x_sample.shape
    assert tp % CHUNK == 0 and ts % CHUNK == 0 and ts % GRID_W == 0

    nrows = -(-(1 + bs) // 8) * 8
    cond = jnp.zeros((nrows, d), F32).at[0].set(c_ctx).at[1:1 + bs].set(c)
    cond = cond * (1.0 / (1.0 + jnp.exp(-cond)))
    mod0 = _matmul_bias(cond, w_mod[0], b_mod[0].reshape(1, 3 * d))
    mod1 = _matmul_bias(cond, w_mod[1], b_mod[1].reshape(1, 3 * d))

    def trunk(x3, shared_mod, grid_w, gla_s0, rwkv_s0, gla_bb, rwkv_bb, emit_state):
        b, t, _ = x3.shape
        n = b * t
        tm, tw = (_tile_rows(b, t, shared_mod, rows) for rows in (TOKEN_TILE, WIDE_TILE))
        mod_row = lambda rows: (lambda i: 0) if shared_mod else (lambda i: 1 + i // (t // rows))
        x = x3.reshape(n, d)
        seq = lambda arr: arr.reshape(b, t, arr.shape[-1])
        q, k, v, g, lg = _gla_front(x, mod0, mod_row(tw), tw, norm_w[0], gla_w_in[0], gla_w_a1[0], gla_w_a2[0], gla_b_a[0])
        o, *gla_s = _gla_scan(seq(q), seq(k), seq(v), seq(lg), gla_s0, gla_bb, GLA_HEADS, emit_state)
        x1, h1 = _gla_back(o.reshape(n, d), g, x, mod0, mod1, mod_row(tw), tw, gla_norm[0], gla_w_out[0], norm_w[1])
        r, k, v, g, lw, a = _rwkv_front(h1, tm, t, grid_w, rwkv_mu[0], rwkv_w_rkvg[0], rwkv_w0[0], rwkv_w1[0], rwkv_w2[0],
                                        rwkv_a0[0], rwkv_a1[0], rwkv_a2[0])
        o, *rwkv_s = _rwkv_scan(seq(r), seq(k), seq(v), seq(lw), seq(a), rwkv_k_k[0], rwkv_k_a[0], rwkv_r_k[0],
                                rwkv_lnx_w[0], rwkv_lnx_b[0], rwkv_s0, rwkv_bb, emit_state)
        y = _rwkv_back(o.reshape(n, d), g, x1, mod1, mod_row(tw), tw, rwkv_w_out[0], norm_f)
        return y.reshape(b, t, d), gla_s, rwkv_s

    y_prompt, (gla_s,), (rwkv_s,) = trunk(x_prompt, True, None, None, None, min(bp, 2), min(bp, 8), True)

    y_sample, _, _ = trunk(x_sample, False, GRID_W, state_gla[:, 0], _pair_blockdiag(state_rwkv[:, 0]),
                           1, min(bs, 4), False)
    return (y_prompt, y_sample, gla_s[:, None], rwkv_s[:, None])
```

```python
import functools

import jax
import jax.numpy as jnp
from jax import lax
from jax.experimental import pallas as pl
from jax.experimental.pallas import tpu as pltpu

F32 = jnp.float32
BF16 = jnp.bfloat16

D_MODEL = 1024
EPS = 1e-6
GRID_W = 64
GLA_HEADS = 4
GLA_DK = 512
GLA_DV = 1024
GLA_DKH = 128
GLA_DVH = 256
GLA_GATE_RANK = 16
GLA_GATE_NORM = 16.0
RWKV_HEAD = 64
RWKV_HEADS = 16
RWKV_PAIRS = 8
RWKV_RANK = 64
LNX_EPS = 64e-5
RWKV_DECAY_SCALE = 0.6065306597126334
CHUNK = 64
SEQ_STEPS = 4
GLA_STEPS = 4
INV_BASE = 8
TOKEN_TILE = 512
WIDE_TILE = 1024
FINISH_GROUP = 32
PREP_GROUP = 8
LANES = 128
VMEM_LIMIT = 56 * 1024 * 1024


def _dot(a, b):
    return jnp.dot(a.astype(BF16), b.astype(BF16), preferred_element_type=F32)


def _dot_nt(a, b):
    return lax.dot_general(a.astype(BF16), b.astype(BF16), (((1,), (1,)), ((), ())),
                           preferred_element_type=F32)


def _dot_tn(a, b):
    return lax.dot_general(a.astype(BF16), b.astype(BF16), (((0,), (0,)), ((), ())),
                           preferred_element_type=F32)


def _tri_dot(tri, x):
    hi = x.astype(BF16)
    lo = (x - hi.astype(F32)).astype(BF16)
    t = tri.astype(BF16)
    return jnp.dot(t, hi, preferred_element_type=F32) + jnp.dot(t, lo, preferred_element_type=F32)


def _log_sigmoid(z):
    return jnp.minimum(z, 0.0) - jnp.log(1.0 + jnp.exp(-jnp.abs(z)))


def _sigmoid(z):
    return 0.5 * jnp.tanh(0.5 * z) + 0.5


def _silu(z):
    return z * _sigmoid(z)


def _rms(x):
    return x * lax.rsqrt(jnp.mean(x * x, axis=-1, keepdims=True) + EPS)


def _params(n_grid_dims):
    return pltpu.CompilerParams(dimension_semantics=("arbitrary",) * n_grid_dims,
                                vmem_limit_bytes=VMEM_LIMIT)


def _mm_kernel(x_ref, w_ref, b_ref, o_ref):
    o_ref[...] = _dot(x_ref[...], w_ref[...]) + b_ref[...]


def _matmul_bias(x, w, b):
    m, k = x.shape
    n = w.shape[1]
    return pl.pallas_call(
        _mm_kernel,
        grid=(1,),
        in_specs=[pl.BlockSpec((m, k), lambda i: (0, 0)),
                  pl.BlockSpec((k, n), lambda i: (0, 0)),
                  pl.BlockSpec((1, n), lambda i: (0, 0))],
        out_specs=pl.BlockSpec((m, n), lambda i: (0, 0)),
        out_shape=jax.ShapeDtypeStruct((m, n), F32),
        compiler_params=_params(1),
        name="mod_matmul",
    )(x, w.astype(BF16), b)


def _gla_front_kernel(x_ref, mod_ref, nw_ref, win_ref, wa1_ref, wa2_ref, ba_ref,
                      q_ref, k_ref, v_ref, g_ref, lg_ref):
    d = D_MODEL
    mod = mod_ref[0]
    shift, scale = mod[:, :d], mod[:, d:2 * d]
    h = _rms(x_ref[...]) * nw_ref[...] * (1.0 + scale) + shift
    hb = h.astype(BF16)
    proj = lambda lo, hi: jnp.dot(hb, win_ref[:, lo:hi], preferred_element_type=F32)
    q_ref[...] = (proj(0, GLA_DK) * (GLA_DKH ** -0.5)).astype(BF16)
    k_ref[...] = proj(GLA_DK, 2 * GLA_DK).astype(BF16)
    v_ref[...] = proj(2 * GLA_DK, 2 * GLA_DK + GLA_DV).astype(BF16)
    g_ref[...] = proj(2 * GLA_DK + GLA_DV, 2 * GLA_DK + 2 * GLA_DV).astype(BF16)
    t = jnp.dot(hb, wa1_ref[...], preferred_element_type=F32)
    z = _dot(t, wa2_ref[...]) + ba_ref[...]
    lg_ref[...] = _log_sigmoid(z) * (1.0 / GLA_GATE_NORM)


def _gla_front(x, mod, mod_idx, tm, norm_w, w_in, w_a1, w_a2, b_a):
    n, d = x.shape
    nmod = mod.shape[0]
    wa1 = jnp.zeros((d, LANES), F32).at[:, :GLA_GATE_RANK].set(w_a1[0]).at[:, GLA_GATE_RANK:2 * GLA_GATE_RANK].set(w_a1[1])
    wa2 = jnp.zeros((LANES, 2 * GLA_DK), F32).at[:GLA_GATE_RANK, :GLA_DK].set(w_a2[0])
    wa2 = wa2.at[GLA_GATE_RANK:2 * GLA_GATE_RANK, GLA_DK:].set(w_a2[1])
    ba = b_a.reshape(1, 2 * GLA_DK)
    row = lambda i: (i, 0)
    const = lambda i: (0, 0)
    n_in = w_in.shape[1]
    outs = pl.pallas_call(
        _gla_front_kernel,
        grid=(n // tm,),
        in_specs=[pl.BlockSpec((tm, d), row),
                  pl.BlockSpec((1, 1, 3 * d), lambda i: (mod_idx(i), 0, 0)),
                  pl.BlockSpec((1, d), const),
                  pl.BlockSpec((d, n_in), const),
                  pl.BlockSpec((d, LANES), const),
                  pl.BlockSpec((LANES, 2 * GLA_DK), const),
                  pl.BlockSpec((1, 2 * GLA_DK), const)],
        out_specs=[pl.BlockSpec((tm, GLA_DK), row), pl.BlockSpec((tm, GLA_DK), row),
                   pl.BlockSpec((tm, GLA_DV), row), pl.BlockSpec((tm, GLA_DV), row),
                   pl.BlockSpec((tm, 2 * GLA_DK), row)],
        out_shape=[jax.ShapeDtypeStruct((n, GLA_DK), BF16), jax.ShapeDtypeStruct((n, GLA_DK), BF16),
                   jax.ShapeDtypeStruct((n, GLA_DV), BF16), jax.ShapeDtypeStruct((n, GLA_DV), BF16),
                   jax.ShapeDtypeStruct((n, 2 * GLA_DK), F32)],
        compiler_params=_params(1),
        name="gla_front",
    )(x, mod.reshape(nmod, 1, 3 * d), norm_w.reshape(1, d), w_in.astype(BF16), wa1.astype(BF16),
      wa2.astype(BF16), ba)
    return outs


def _gla_chunks(chains):
    flat = [(x, ch[2]) for ch in chains for x in ch[0]]
    c = flat[0][0][0].shape[0]
    dk = flat[0][0][1].shape[1]
    row = lax.broadcasted_iota(jnp.int32, (c, c), 0)
    col = lax.broadcasted_iota(jnp.int32, (c, c), 1)
    keep = [(col >= row) if rv else (col <= row) for _, rv in flat]
    b = [_tri_dot(kp.astype(F32), x[3]) for kp, (x, _) in zip(keep, flat)]
    b_end = [bb[0:1] if rv else bb[c - 1:c] for bb, (_, rv) in zip(b, flat)]
    qb = [(x[0] * jnp.exp(bb)).astype(BF16) for (x, _), bb in zip(flat, b)]
    kb = [x[1] * jnp.exp(-bb) for (x, _), bb in zip(flat, b)]
    kd = [x[1] * jnp.exp(be - bb) for (x, _), bb, be in zip(flat, b, b_end)]
    scores = [jnp.where(kp, _dot_nt(x, y), 0.0).astype(BF16) for kp, x, y in zip(keep, qb, kb)]
    upd = [_dot_tn(x, y[2]) for (y, _), x in zip(flat, kd)]
    decay = [jnp.broadcast_to(jnp.exp(be), (dk, dk)).T for be in b_end]
    n_steps = len(chains[0][0])
    st = [ch[1] for ch in chains]
    outs = [[] for _ in chains]
    for s in range(n_steps):
        at = [ci * n_steps + s for ci in range(len(chains))]
        for ci, f in enumerate(at):
            outs[ci].append(_dot(jnp.concatenate([scores[f], qb[f]], axis=1),
                                 jnp.concatenate([flat[f][0][2], st[ci].astype(BF16)], axis=0)))
        st = [st[ci] * jnp.concatenate([decay[f]] * (st[ci].shape[1] // dk), axis=1) + upd[f]
              for ci, f in enumerate(at)]
    return list(zip(outs, st))


def _gla_scan_kernel(*refs, has_s0, emit_state):
    q_ref, k_ref, v_ref, lf_ref, lb_ref = refs[:5]
    s0_ref = refs[5] if has_s0 else None
    o_ref = refs[5 + has_s0]
    s_ref = refs[6 + has_s0] if emit_state else None
    acc_ref, st_ref = refs[-2:]
    bb, t = q_ref.shape[0], q_ref.shape[1]
    hb = q_ref.shape[2] // GLA_DKH
    nc = t // CHUNK
    half = nc // 2
    kh = lambda h: slice(h * GLA_DKH, (h + 1) * GLA_DKH)
    vh = lambda h: slice(h * GLA_DVH, (h + 1) * GLA_DVH)

    steps = GLA_STEPS
    while half % steps:
        steps //= 2

    def both(accumulate):
        def body(i, carry):
            sls = [[pl.ds(pl.multiple_of(ci * CHUNK, CHUNK), CHUNK)
                    for ci in ((i * steps + s, nc - 1 - i * steps - s)[d] for s in range(steps))] for d in range(2)]
            chains = [([(q_ref[e, sl, kh(h)].astype(F32), k_ref[e, sl, kh(h)].astype(F32), v_ref[e, sl, vh(h)],
                         (lf_ref, lb_ref)[d][e, sl, kh(h)]) for sl in sls[d]], st_ref[n], d == 1)
                      for n, (e, d, h) in enumerate(idx)]
            res = _gla_chunks(chains)
            for n, (e, d, h) in enumerate(idx):
                for sl, o in zip(sls[d], res[n][0]):
                    if accumulate:
                        o_ref[e, sl, vh(h)] = (acc_ref[e, sl, vh(h)] + o).astype(BF16)
                    else:
                        acc_ref[e, sl, vh(h)] = o
            for n, (_, st) in enumerate(res):
                st_ref[n] = st
            return carry
        return body

    idx = [(e, d, h) for e in range(bb) for d in range(2) for h in range(hb)]
    for n, (e, d, h) in enumerate(idx):
        st_ref[n] = s0_ref[e, d, h] if has_s0 else jnp.zeros((GLA_DKH, GLA_DVH), F32)
    lax.fori_loop(0, half // steps, both(False), 0)
    lax.fori_loop(half // steps, nc // steps, both(True), 0)
    if emit_state:
        for n, (e, d, h) in enumerate(idx):
            s_ref[e, d, h] = st_ref[n]


def _gla_scan(q, k, v, lg, s0, bb, hb, emit_state):
    bsz, t, _ = q.shape
    ng = GLA_HEADS // hb
    hd = lambda b, h: (b, 0, h)
    kblk = pl.BlockSpec((bb, t, hb * GLA_DKH), hd)
    vblk = pl.BlockSpec((bb, t, hb * GLA_DVH), hd)
    sblk = pl.BlockSpec((bb, 2, hb, GLA_DKH, GLA_DVH), lambda b, h: (b, 0, h, 0, 0))
    s_shape = jax.ShapeDtypeStruct((bsz, 2, GLA_HEADS, GLA_DKH, GLA_DVH), F32)
    has_s0 = s0 is not None
    return pl.pallas_call(
        functools.partial(_gla_scan_kernel, has_s0=has_s0, emit_state=emit_state),
        grid=(bsz // bb, ng),
        in_specs=[kblk, kblk, vblk, kblk, pl.BlockSpec((bb, t, hb * GLA_DKH), lambda b, h: (b, 0, ng + h))]
        + [sblk] * has_s0,
        out_specs=[vblk] + [sblk] * emit_state,
        out_shape=[jax.ShapeDtypeStruct((bsz, t, GLA_DV), BF16)] + [s_shape] * emit_state,
        scratch_shapes=[pltpu.VMEM((bb, t, hb * GLA_DVH), F32), pltpu.VMEM((2 * bb * hb, GLA_DKH, GLA_DVH), F32)],
        compiler_params=_params(2),
        name="gla_scan",
    )(q, k, v, lg, lg, *([s0] * has_s0))


def _gla_back_kernel(o_ref, g_ref, x_ref, mod0_ref, mod1_ref, gn_ref, wout_ref, nw1_ref, x1_ref, h1_ref):
    d = D_MODEL
    o = o_ref[...].astype(F32)
    gn = gn_ref[...]
    parts = [_rms(o[:, h * GLA_DVH:(h + 1) * GLA_DVH]) * gn for h in range(GLA_HEADS)]
    on = jnp.concatenate(parts, axis=-1) * _silu(g_ref[...].astype(F32))
    out = _dot(on, wout_ref[...])
    gate = mod0_ref[0][:, 2 * d:]
    x1 = x_ref[...] + gate * out
    x1_ref[...] = x1
    mod1 = mod1_ref[0]
    h1_ref[...] = (_rms(x1) * nw1_ref[...] * (1.0 + mod1[:, d:2 * d]) + mod1[:, :d]).astype(BF16)


def _gla_back(o, g, x, mod0, mod1, mod_idx, tm, gla_norm, w_out, norm_w1):
    n, d = x.shape
    nmod = mod0.shape[0]
    row = lambda i: (i, 0)
    const = lambda i: (0, 0)
    modspec = pl.BlockSpec((1, 1, 3 * d), lambda i: (mod_idx(i), 0, 0))
    return pl.pallas_call(
        _gla_back_kernel,
        grid=(n // tm,),
        in_specs=[pl.BlockSpec((tm, d), row), pl.BlockSpec((tm, d), row), pl.BlockSpec((tm, d), row),
                  modspec, modspec,
                  pl.BlockSpec((1, GLA_DVH), const), pl.BlockSpec((d, d), const), pl.BlockSpec((1, d), const)],
        out_specs=[pl.BlockSpec((tm, d), row), pl.BlockSpec((tm, d), row)],
        out_shape=[jax.ShapeDtypeStruct((n, d), F32), jax.ShapeDtypeStruct((n, d), BF16)],
        compiler_params=_params(1),
        name="gla_back",
    )(o, g, x, mod0.reshape(nmod, 1, 3 * d), mod1.reshape(nmod, 1, 3 * d), gla_norm.reshape(1, GLA_DVH),
      w_out.astype(BF16), norm_w1.reshape(1, d))


def _token_shift(h, up_ref, dn_ref, seq_len, grid_w):
    tm = h.shape[0]
    pos = lax.broadcasted_iota(jnp.int32, (tm, 1), 0) + pl.program_id(0) * tm
    prev = pltpu.roll(h, 1, 0)
    nxt = pltpu.roll(h, tm - 1, 0)
    t = pos % seq_len
    if grid_w is None:
        return 0.5 * (jnp.where(t == 0, 0.0, prev) + jnp.where(t == seq_len - 1, 0.0, nxt))
    col = pos % grid_w
    up = jnp.concatenate([up_ref[...].astype(F32), h[:tm - grid_w]], axis=0)
    dn = jnp.concatenate([h[grid_w:], dn_ref[...].astype(F32)], axis=0)
    up = jnp.where(t < grid_w, 0.0, up)
    dn = jnp.where(t >= seq_len - grid_w, 0.0, dn)
    left = jnp.where(col == 0, 0.0, prev)
    right = jnp.where(col == grid_w - 1, 0.0, nxt)
    return 0.25 * (up + dn + left + right)


def _rwkv_front_kernel(*refs, seq_len, grid_w):
    n_h = 1 if grid_w is None else 3
    h_ref, up_ref, dn_ref = (tuple(refs[:n_h]) + (None, None))[:3]
    (mu_ref, wrkvg_ref, w1_ref, w2_ref, w0_ref, a1_ref, a2_ref, a0_ref,
     r_ref, k_ref, v_ref, g_ref, lw_ref, a_ref) = refs[n_h:]
    h = h_ref[...].astype(F32)
    dh = _token_shift(h, up_ref, dn_ref, seq_len, grid_w) - h
    mix = lambda p: (h + dh * mu_ref[p:p + 1, :]).astype(BF16)
    r_ref[...] = jnp.dot(mix(0), wrkvg_ref[0], preferred_element_type=F32).astype(BF16)
    k_ref[...] = jnp.dot(mix(1), wrkvg_ref[1], preferred_element_type=F32).astype(BF16)
    v_ref[...] = jnp.dot(mix(2), wrkvg_ref[2], preferred_element_type=F32).astype(BF16)
    g_ref[...] = jnp.dot(mix(3), wrkvg_ref[3], preferred_element_type=F32).astype(BF16)
    tw = jnp.tanh(jnp.dot(mix(4), w1_ref[...], preferred_element_type=F32))
    lw_ref[...] = -RWKV_DECAY_SCALE * _sigmoid(w0_ref[...] + _dot(tw, w2_ref[...]))
    ta = jnp.dot(mix(5), a1_ref[...], preferred_element_type=F32)
    a_ref[...] = _sigmoid(a0_ref[...] + _dot(ta, a2_ref[...])).astype(BF16)


def _rwkv_front(h, tm, seq_len, grid_w, mu, w_rkvg, w0, w1, w2, a0, a1, a2):
    n, d = h.shape
    rk = RWKV_RANK
    cat1 = lambda w: jnp.concatenate([w[0], w[1]], axis=1)
    bd2 = lambda w: jnp.zeros((2 * rk, 2 * d), F32).at[:rk, :d].set(w[0]).at[rk:, d:].set(w[1])
    row = lambda i: (i, 0)
    const = lambda i: (0, 0)
    if grid_w is None:
        assert tm % seq_len == 0
        h_specs, h_args = [pl.BlockSpec((tm, d), row)], [h]
    else:
        assert tm % grid_w == 0 and seq_len % tm == 0
        rows_per_tile, last = tm // grid_w, n // grid_w - 1
        h_specs = [pl.BlockSpec((tm, d), row),
                   pl.BlockSpec((grid_w, d), lambda i: (jnp.maximum(i * rows_per_tile - 1, 0), 0)),
                   pl.BlockSpec((grid_w, d), lambda i: (jnp.minimum((i + 1) * rows_per_tile, last), 0))]
        h_args = [h, h, h]
    return pl.pallas_call(
        functools.partial(_rwkv_front_kernel, seq_len=seq_len, grid_w=grid_w),
        grid=(n // tm,),
        in_specs=h_specs + [
            pl.BlockSpec((8, d), const),
            pl.BlockSpec((4, d, d), lambda i: (0, 0, 0)),
            pl.BlockSpec((d, 2 * rk), const), pl.BlockSpec((2 * rk, 2 * d), const), pl.BlockSpec((1, 2 * d), const),
            pl.BlockSpec((d, 2 * rk), const), pl.BlockSpec((2 * rk, 2 * d), const), pl.BlockSpec((1, 2 * d), const)],
        out_specs=[pl.BlockSpec((tm, d), row)] * 4 + [pl.BlockSpec((tm, 2 * d), row)] * 2,
        out_shape=[jax.ShapeDtypeStruct((n, d), BF16)] * 4
        + [jax.ShapeDtypeStruct((n, 2 * d), F32), jax.ShapeDtypeStruct((n, 2 * d), BF16)],
        compiler_params=_params(1),
        name="rwkv_front",
    )(*h_args, jnp.zeros((8, d), F32).at[:6].set(mu), w_rkvg.astype(BF16),
      cat1(w1).astype(BF16), bd2(w2).astype(BF16), w0.reshape(1, 2 * d),
      cat1(a1).astype(BF16), bd2(a2).astype(BF16), a0.reshape(1, 2 * d))


def _seg_sum(x, head0):
    s0 = jnp.sum(jnp.where(head0, x, 0.0), axis=-1, keepdims=True)
    s1 = jnp.sum(jnp.where(head0, 0.0, x), axis=-1, keepdims=True)
    return jnp.where(head0, s0, s1)


def _stack(x, head0):
    return jnp.concatenate([jnp.where(head0, x, 0.0), jnp.where(head0, 0.0, x)], axis=0)


def _rwkv_chunk_terms(insts):
    n = len(insts)
    c = insts[0][0].shape[0]
    lane = lax.broadcasted_iota(jnp.int32, (c, LANES), 1)
    head0 = lane < RWKV_HEAD
    row = lax.broadcasted_iota(jnp.int32, (c, c), 0)
    col = lax.broadcasted_iota(jnp.int32, (c, c), 1)
    ti = lax.broadcasted_iota(jnp.int32, (c, LANES), 0)
    si = lane & (c - 1)
    prow = lax.broadcasted_iota(jnp.int32, (LANES, LANES), 0)
    pcol = lax.broadcasted_iota(jnp.int32, (LANES, LANES), 1)
    same_head = (prow >= RWKV_HEAD) == (pcol >= RWKV_HEAD)
    each = lambda f, *ls: [f(*xs) for xs in zip(*ls)]
    rev = [x[8] for x in insts]
    stk = lambda x: _stack(x, head0).astype(BF16)

    cum = [_tri_dot(((col >= row) if x[8] else (col <= row)).astype(F32), x[3]) for x in insts]
    tot = [cm[0:1] if rv else cm[c - 1:c] for cm, rv in zip(cum, rev)]

    a_t, r_t, b_ts, k_ts, b_h, k_h, v_s, vv, bonus = [], [], [], [], [], [], [], [], []
    for (r, k, v, lw, a, k_k, k_a, r_k, _), cm, tt in zip(insts, cum, tot):
        kk = k * k_k
        kk = kk * lax.rsqrt(jnp.maximum(_seg_sum(kk * kk, head0), 1e-24))
        kd = k * (1.0 + (a - 1.0) * k_a)
        bv = kk * a
        bonus.append(_seg_sum(r * kd * r_k, head0) * v)
        e_neg = jnp.exp(-cm)
        e_tot = jnp.exp(tt)
        b_neg, k_neg = bv * e_neg, kd * e_neg
        a_t.append(-kk * jnp.exp(cm - lw))
        r_t.append(r * jnp.exp(cm))
        b_ts.append(stk(b_neg))
        k_ts.append(stk(k_neg))
        b_h.append(b_neg * e_tot)
        k_h.append(k_neg * e_tot)
        v_s.append(stk(v))
        vv.append(v)

    strict = [(si > ti) if rv else (si < ti) for rv in rev]
    incl = [(si >= ti) if rv else (si <= ti) for rv in rev]
    ar = each(lambda x, y: jnp.concatenate([x, y], axis=0).astype(BF16), a_t, r_t)
    g_bk = each(lambda x, y, z: _dot_nt(x, jnp.concatenate([y, z], axis=0)), ar, b_ts, k_ts)
    g_b = [x[:, :LANES] for x in g_bk]
    g_k = [x[:, LANES:] for x in g_bk]
    n_ab = each(lambda m_, x: jnp.where(m_, x[:c], 0.0), strict, g_b)
    l_ak = each(lambda m_, x: jnp.where(m_, x[:c], 0.0), strict, g_k)
    p_rb = each(lambda m_, x: jnp.where(m_, x[c:], 0.0), incl, g_b)
    p_rk = each(lambda m_, x: jnp.where(m_, x[c:], 0.0), incl, g_k)

    same_blk = lambda h: (ti // h) == (si // h)
    n_d = each(lambda x: jnp.where(same_blk(INV_BASE), x, 0.0), n_ab)
    steps = max(1, (INV_BASE - 1).bit_length())
    q = n_d
    tinv = n_d
    q = each(lambda x: _dot(x, stk(x)), q) if steps > 1 else q
    for j in range(1, steps):
        last = j == steps - 1
        qs = each(stk, q)
        if last:
            z = each(_dot, tinv, qs)
            tinv = each(lambda t_, x, y: t_ + x + y, tinv, q, z)
        else:
            both = each(lambda x, t_, w_: _dot(jnp.concatenate([x, t_], axis=0), w_), q, tinv, qs)
            tinv = each(lambda t_, x, y: t_ + x + y[c:], tinv, q, both)
            q = each(lambda y: y[:c], both)
    h = INV_BASE
    while h < c:
        link = same_blk(2 * h) & jnp.logical_not(same_blk(h))
        n_off = each(lambda x: jnp.where(link, x, 0.0), n_ab)
        y = each(lambda t_, x: x + _dot(t_, stk(x)), tinv, n_off)
        tinv = each(lambda t_, x: t_ + x + _dot(x, stk(t_)), tinv, y)
        h *= 2
    lv = each(lambda l_, p, y: _dot(jnp.concatenate([l_, p], axis=0), y), l_ak, p_rk, v_s)
    w = [x[:c] for x in lv]
    pair = lambda f, x, y: _dot(f, jnp.concatenate([stk(x), stk(y)], axis=1))
    au = each(pair, tinv, a_t, w)
    a_bar = each(lambda x, z: x + z[:, :LANES], a_t, au)
    u0 = each(lambda x, z: x + z[:, LANES:], w, au)
    ry = each(pair, p_rb, a_bar, u0)
    r_bar = each(lambda x, z: x + z[:, :LANES], r_t, ry)
    y0 = each(lambda z, l_: z[:, LANES:] + l_[c:], ry, lv)
    zero = jnp.zeros((c, LANES), F32)
    ms = each(lambda ab, u, v_, bh, kh_: _dot_tn(
        jnp.concatenate([jnp.concatenate([ab, u], axis=1), jnp.concatenate([zero, v_], axis=1)], axis=0),
        jnp.concatenate([bh, kh_], axis=0)), a_bar, u0, vv, b_h, k_h)
    m = [jnp.where(same_head, x[:LANES], 0.0) for x in ms]
    s0p = [jnp.where(same_head, x[LANES:], 0.0) for x in ms]
    return [(r_bar[i], y0[i], m[i], s0p[i], jnp.exp(tot[i]), bonus[i]) for i in range(n)]


def _rwkv_scan_kernel(*refs, has_s0, emit_state):
    (r_ref, k_ref, v_ref, lwf_ref, lwb_ref, af_ref, ab_ref,
     kkf_ref, kkb_ref, kaf_ref, kab_ref, rkf_ref, rkb_ref, lnw_ref, lnb_ref) = refs[:15]
    s0_ref = refs[15] if has_s0 else None
    o_ref = refs[15 + has_s0]
    s_ref = refs[16 + has_s0] if emit_state else None
    w_ref, y0_ref, s0p_ref, dec_ref, bonus_ref, yf_ref, yb_ref, st_ref = refs[16 + has_s0 + emit_state:]
    bb, t = r_ref.shape[0], r_ref.shape[1]
    nc = t // CHUNK
    dirs = ((lwf_ref, af_ref, kkf_ref, kaf_ref, rkf_ref, False), (lwb_ref, ab_ref, kkb_ref, kab_ref, rkb_ref, True))
    group = min(bb * nc, PREP_GROUP)
    assert (bb * nc) % group == 0
    chunk_rows = lambda ci: pl.ds(pl.multiple_of(ci * CHUNK, CHUNK), CHUNK)

    def prep(gi, carry):
        insts, where = [], []
        at = [((gi * group + j) // nc, chunk_rows((gi * group + j) % nc)) for j in range(group)]
        for j, (e, sl) in enumerate(at):
            r, k, v = (ref[e, sl, :].astype(F32) for ref in (r_ref, k_ref, v_ref))
            for d, (lw_ref, a_ref, kk_ref, ka_ref, rk_ref, reverse) in enumerate(dirs):
                insts.append((r, k, v, lw_ref[e, sl, :], a_ref[e, sl, :].astype(F32), kk_ref[...], ka_ref[...],
                              rk_ref[...], reverse))
                where.append((d, gi * group + j, e, sl))
        terms = _rwkv_chunk_terms(insts)
        for (d, fi, e, sl), (r_bar, y0, m, s0p, dec, bon) in zip(where, terms):
            rbar_t = jnp.concatenate([r_bar, jnp.zeros_like(r_bar)], axis=0).T
            w_ref[d, fi] = jnp.concatenate([m, rbar_t], axis=1).astype(BF16)
            y0_ref[d, fi] = y0
            s0p_ref[d, fi] = s0p
            dec_ref[d, fi] = jnp.broadcast_to(dec, (8, LANES))
        for j in range(group):
            _, _, e, sl = where[2 * j]
            bonus_ref[e, sl, :] = terms[2 * j][5] + terms[2 * j + 1][5]
        return carry

    lax.fori_loop(0, bb * nc // group, prep, 0)

    seq_steps = SEQ_STEPS if nc % SEQ_STEPS == 0 else 1
    idx = [(d, e) for d in range(2) for e in range(bb)]

    def seq(i, _):
        carry = tuple(st_ref[n] for n in range(len(idx)))
        for s in range(seq_steps):
            cis = (i * seq_steps + s, nc - 1 - i * seq_steps - s)
            sw = [jnp.dot(carry[n].astype(BF16), w_ref[d, e * nc + cis[d]], preferred_element_type=F32)
                  for n, (d, e) in enumerate(idx)]
            for n, (d, e) in enumerate(idx):
                y = sw[n][:, LANES:].T[:CHUNK] + y0_ref[d, e * nc + cis[d]]
                (yf_ref, yb_ref)[d][e, chunk_rows(cis[d]), :] = y
            carry = tuple(carry[n] * dec_ref[d, e * nc + cis[d]][0:1] + sw[n][:, :LANES] + s0p_ref[d, e * nc + cis[d]]
                          for n, (d, e) in enumerate(idx))
        for n, st in enumerate(carry):
            st_ref[n] = st
        return 0

    for n, (d, e) in enumerate(idx):
        st_ref[n] = s0_ref[e, d, 0] if has_s0 else jnp.zeros((LANES, LANES), F32)
    lax.fori_loop(0, nc // seq_steps, seq, 0)
    if emit_state:
        for d in range(2):
            for e in range(bb):
                s = st_ref[d * bb + e]
                s_ref[e, d, 0] = s[:RWKV_HEAD, :RWKV_HEAD]
                s_ref[e, d, 1] = pltpu.roll(s, RWKV_HEAD, 1)[RWKV_HEAD:, :RWKV_HEAD]

    prow = lax.broadcasted_iota(jnp.int32, (LANES, LANES), 0)
    pcol = lax.broadcasted_iota(jnp.int32, (LANES, LANES), 1)
    head_ones = ((prow >= RWKV_HEAD) == (pcol >= RWKV_HEAD)).astype(BF16)
    inv_n = 1.0 / RWKV_HEAD

    def head_mean(x):
        hi = x.astype(BF16)
        lo = (x - hi.astype(F32)).astype(BF16)
        both = jnp.dot(jnp.concatenate([hi, lo], axis=0), head_ones, preferred_element_type=F32)
        return (both[:x.shape[0]] + both[x.shape[0]:]) * inv_n

    fgroup = min(bb * nc, FINISH_GROUP)
    assert (bb * nc) % fgroup == 0

    def finish(gi, carry):
        at = [((gi * fgroup + j) // nc, chunk_rows((gi * fgroup + j) % nc)) for j in range(fgroup)]
        y = [yf_ref[e, sl, :] + yb_ref[e, sl, :] for e, sl in at]
        yc = [x - head_mean(x) for x in y]
        var = [head_mean(x * x) for x in yc]
        for (e, sl), x, vr in zip(at, yc, var):
            o_ref[e, sl, :] = (x * lax.rsqrt(vr + LNX_EPS) * lnw_ref[...] + lnb_ref[...]
                               + bonus_ref[e, sl, :]).astype(BF16)
        return carry

    lax.fori_loop(0, bb * nc // fgroup, finish, 0)


def _rwkv_scan(r, k, v, lw, a, k_k, k_a, r_k, lnx_w, lnx_b, s0, bb, emit_state):
    bsz, t, d = r.shape
    nc = t // CHUNK
    seq = lambda b, p: (b, 0, p)
    seq_b = lambda b, p: (b, 0, RWKV_PAIRS + p)
    par = lambda b, p: (0, p)
    par_b = lambda b, p: (0, RWKV_PAIRS + p)
    sblk = pl.BlockSpec((bb, t, LANES), seq)
    sblk_b = pl.BlockSpec((bb, t, LANES), seq_b)
    pblk = pl.BlockSpec((1, LANES), par)
    pblk_b = pl.BlockSpec((1, LANES), par_b)
    s0blk = pl.BlockSpec((bb, 2, 1, LANES, LANES), lambda b, p: (b, 0, p, 0, 0))
    s_blk = pl.BlockSpec((bb, 2, 2, RWKV_HEAD, RWKV_HEAD), lambda b, p: (b, 0, p, 0, 0))
    s_shape = jax.ShapeDtypeStruct((bsz, 2, RWKV_HEADS, RWKV_HEAD, RWKV_HEAD), F32)
    has_s0 = s0 is not None
    k_k, k_a, r_k = (p.reshape(1, 2 * d) for p in (k_k, k_a, r_k))
    return pl.pallas_call(
        functools.partial(_rwkv_scan_kernel, has_s0=has_s0, emit_state=emit_state),
        grid=(bsz // bb, RWKV_PAIRS),
        in_specs=[sblk, sblk, sblk, sblk, sblk_b, sblk, sblk_b,
                  pblk, pblk_b, pblk, pblk_b, pblk, pblk_b, pblk, pblk] + [s0blk] * has_s0,
        out_specs=[sblk] + [s_blk] * emit_state,
        out_shape=[jax.ShapeDtypeStruct((bsz, t, d), BF16)] + [s_shape] * emit_state,
        scratch_shapes=[pltpu.VMEM((2, bb * nc, LANES, 2 * LANES), BF16), pltpu.VMEM((2, bb * nc, CHUNK, LANES), F32),
                        pltpu.VMEM((2, bb * nc, LANES, LANES), F32),
                        pltpu.VMEM((2, bb * nc, 8, LANES), F32), pltpu.VMEM((bb, t, LANES), F32),
                        pltpu.VMEM((bb, t, LANES), F32), pltpu.VMEM((bb, t, LANES), F32),
                        pltpu.VMEM((2 * bb, LANES, LANES), F32)],
        compiler_params=_params(2),
        name="rwkv_scan",
    )(r, k, v, lw, lw, a, a, k_k, k_k, k_a, k_a, r_k, r_k,
      lnx_w.reshape(1, d), lnx_b.reshape(1, d), *([s0] * has_s0))


def _rwkv_back_kernel(o_ref, g_ref, x_ref, mod_ref, wout_ref, nf_ref, y_ref):
    out = _dot(o_ref[...].astype(F32) * _silu(g_ref[...].astype(F32)), wout_ref[...])
    x2 = x_ref[...] + mod_ref[0][:, 2 * D_MODEL:] * out
    y_ref[...] = _rms(x2) * nf_ref[...]


def _rwkv_back(o, g, x, mod1, mod_idx, tm, w_out, norm_f):
    n, d = x.shape
    nmod = mod1.shape[0]
    row = lambda i: (i, 0)
    const = lambda i: (0, 0)
    return pl.pallas_call(
        _rwkv_back_kernel,
        grid=(n // tm,),
        in_specs=[pl.BlockSpec((tm, d), row), pl.BlockSpec((tm, d), row), pl.BlockSpec((tm, d), row),
                  pl.BlockSpec((1, 1, 3 * d), lambda i: (mod_idx(i), 0, 0)),
                  pl.BlockSpec((d, d), const), pl.BlockSpec((1, d), const)],
        out_specs=pl.BlockSpec((tm, d), row),
        out_shape=jax.ShapeDtypeStruct((n, d), F32),
        compiler_params=_params(1),
        name="rwkv_back",
    )(o, g, x, mod1.reshape(nmod, 1, 3 * d), w_out.astype(BF16), norm_f.reshape(1, d))


def _pair_blockdiag(s):
    b = s.shape[0]
    s = s.reshape(b, 2, RWKV_PAIRS, 2, RWKV_HEAD, RWKV_HEAD)
    z = jnp.zeros_like(s[:, :, :, 0])
    top = jnp.concatenate([s[:, :, :, 0], z], axis=-1)
    bot = jnp.concatenate([z, s[:, :, :, 1]], axis=-1)
    return jnp.concatenate([top, bot], axis=-2)


def _tile_rows(n_seq, t, may_span, rows):
    if t % rows == 0:
        return rows
    if may_span and rows % t == 0 and (n_seq * t) % rows == 0:
        return rows
    return t


def kernel(x_prompt, x_sample, state_gla, state_rwkv, c, c_ctx, w_mod, b_mod, norm_w, gla_w_in, gla_w_a1,
           gla_w_a2, gla_b_a, gla_norm, gla_w_out, rwkv_mu, rwkv_w_rkvg, rwkv_w0, rwkv_w1, rwkv_w2, rwkv_a0,
           rwkv_a1, rwkv_a2, rwkv_k_k, rwkv_k_a, rwkv_r_k, rwkv_lnx_w, rwkv_lnx_b, rwkv_w_out, norm_f):
    d = D_MODEL
    bp, tp, _ = x_prompt.shape
    bs, ts, _ = x_sample.shape
    assert tp % CHUNK == 0 and ts % CHUNK == 0 and ts % GRID_W == 0

    nrows = -(-(1 + bs) // 8) * 8
    cond = jnp.zeros((nrows, d), F32).at[0].set(c_ctx).at[1:1 + bs].set(c)
    cond = cond * (1.0 / (1.0 + jnp.exp(-cond)))
    mod0 = _matmul_bias(cond, w_mod[0], b_mod[0].reshape(1, 3 * d))
    mod1 = _matmul_bias(cond, w_mod[1], b_mod[1].reshape(1, 3 * d))

    def trunk(x3, shared_mod, grid_w, gla_s0, rwkv_s0, gla_bb, rwkv_bb, emit_state):
        b, t, _ = x3.shape
        n = b * t
        tm, tw = (_tile_rows(b, t, shared_mod, rows) for rows in (TOKEN_TILE, WIDE_TILE))
        mod_row = lambda rows: (lambda i: 0) if shared_mod else (lambda i: 1 + i // (t // rows))
        x = x3.reshape(n, d)
        seq = lambda arr: arr.reshape(b, t, arr.shape[-1])
        q, k, v, g, lg = _gla_front(x, mod0, mod_row(tw), tw, norm_w[0], gla_w_in[0], gla_w_a1[0], gla_w_a2[0], gla_b_a[0])
        o, *gla_s = _gla_scan(seq(q), seq(k), seq(v), seq(lg), gla_s0, gla_bb, GLA_HEADS, emit_state)
        x1, h1 = _gla_back(o.reshape(n, d), g, x, mod0, mod1, mod_row(tw), tw, gla_norm[0], gla_w_out[0], norm_w[1])
        r, k, v, g, lw, a = _rwkv_front(h1, tm, t, grid_w, rwkv_mu[0], rwkv_w_rkvg[0], rwkv_w0[0], rwkv_w1[0], rwkv_w2[0],
                                        rwkv_a0[0], rwkv_a1[0], rwkv_a2[0])
        o, *rwkv_s = _rwkv_scan(seq(r), seq(k), seq(v), seq(lw), seq(a), rwkv_k_k[0], rwkv_k_a[0], rwkv_r_k[0],
                                rwkv_lnx_w[0], rwkv_lnx_b[0], rwkv_s0, rwkv_bb, emit_state)
        y = _rwkv_back(o.reshape(n, d), g, x1, mod1, mod_row(tw), tw, rwkv_w_out[0], norm_f)
        return y.reshape(b, t, d), gla_s, rwkv_s

    y_prompt, (gla_s,), (rwkv_s,) = trunk(x_prompt, True, None, None, None, min(bp, 2), min(bp, 8), True)

    y_sample, _, _ = trunk(x_sample, False, GRID_W, state_gla[:, 0], _pair_blockdiag(state_rwkv[:, 0]),
                           1, min(bs, 4), False)
    return (y_prompt, y_sample, gla_s[:, None], rwkv_s[:, None])
```

```python
import functools

import jax
import jax.numpy as jnp
from jax import lax
from jax.experimental import pallas as pl
from jax.experimental.pallas import tpu as pltpu

F32 = jnp.float32
BF16 = jnp.bfloat16

D_MODEL = 1024
EPS = 1e-6
GRID_W = 64
GLA_HEADS = 4
GLA_DK = 512
GLA_DV = 1024
GLA_DKH = 128
GLA_DVH = 256
GLA_GATE_RANK = 16
GLA_GATE_NORM = 16.0
RWKV_HEAD = 64
RWKV_HEADS = 16
RWKV_PAIRS = 8
RWKV_RANK = 64
LNX_EPS = 64e-5
RWKV_DECAY_SCALE = 0.6065306597126334
CHUNK = 64
SEQ_STEPS = 4
GLA_STEPS = 4
INV_BASE = 8
TOKEN_TILE = 512
WIDE_TILE = 1024
FINISH_GROUP = 32
PREP_GROUP = 8
LANES = 128
VMEM_LIMIT = 56 * 1024 * 1024


def _dot(a, b):
    return jnp.dot(a.astype(BF16), b.astype(BF16), preferred_element_type=F32)


def _dot_nt(a, b):
    return lax.dot_general(a.astype(BF16), b.astype(BF16), (((1,), (1,)), ((), ())),
                           preferred_element_type=F32)


def _dot_tn(a, b):
    return lax.dot_general(a.astype(BF16), b.astype(BF16), (((0,), (0,)), ((), ())),
                           preferred_element_type=F32)


def _tri_dot(tri, x):
    hi = x.astype(BF16)
    lo = (x - hi.astype(F32)).astype(BF16)
    t = tri.astype(BF16)
    return jnp.dot(t, hi, preferred_element_type=F32) + jnp.dot(t, lo, preferred_element_type=F32)


def _log_sigmoid(z):
    return jnp.minimum(z, 0.0) - jnp.log(1.0 + jnp.exp(-jnp.abs(z)))


def _sigmoid(z):
    return 0.5 * jnp.tanh(0.5 * z) + 0.5


def _silu(z):
    return z * _sigmoid(z)


def _rms(x):
    return x * lax.rsqrt(jnp.mean(x * x, axis=-1, keepdims=True) + EPS)


def _params(n_grid_dims):
    return pltpu.CompilerParams(dimension_semantics=("arbitrary",) * n_grid_dims,
                                vmem_limit_bytes=VMEM_LIMIT)


def _mm_kernel(x_ref, w_ref, b_ref, o_ref):
    o_ref[...] = _dot(x_ref[...], w_ref[...]) + b_ref[...]


def _matmul_bias(x, w, b):
    m, k = x.shape
    n = w.shape[1]
    return pl.pallas_call(
        _mm_kernel,
        grid=(1,),
        in_specs=[pl.BlockSpec((m, k), lambda i: (0, 0)),
                  pl.BlockSpec((k, n), lambda i: (0, 0)),
                  pl.BlockSpec((1, n), lambda i: (0, 0))],
        out_specs=pl.BlockSpec((m, n), lambda i: (0, 0)),
        out_shape=jax.ShapeDtypeStruct((m, n), F32),
        compiler_params=_params(1),
        name="mod_matmul",
    )(x, w.astype(BF16), b)


def _gla_front_kernel(x_ref, mod_ref, nw_ref, win_ref, wa1_ref, wa2_ref, ba_ref,
                      q_ref, k_ref, v_ref, g_ref, lg_ref):
    d = D_MODEL
    mod = mod_ref[0]
    shift, scale = mod[:, :d], mod[:, d:2 * d]
    h = _rms(x_ref[...]) * nw_ref[...] * (1.0 + scale) + shift
    hb = h.astype(BF16)
    proj = lambda lo, hi: jnp.dot(hb, win_ref[:, lo:hi], preferred_element_type=F32)
    q_ref[...] = (proj(0, GLA_DK) * (GLA_DKH ** -0.5)).astype(BF16)
    k_ref[...] = proj(GLA_DK, 2 * GLA_DK).astype(BF16)
    v_ref[...] = proj(2 * GLA_DK, 2 * GLA_DK + GLA_DV).astype(BF16)
    g_ref[...] = proj(2 * GLA_DK + GLA_DV, 2 * GLA_DK + 2 * GLA_DV).astype(BF16)
    t = jnp.dot(hb, wa1_ref[...], preferred_element_type=F32)
    z = _dot(t, wa2_ref[...]) + ba_ref[...]
    lg_ref[...] = _log_sigmoid(z) * (1.0 / GLA_GATE_NORM)


def _gla_front(x, mod, mod_idx, tm, norm_w, w_in, w_a1, w_a2, b_a):
    n, d = x.shape
    nmod = mod.shape[0]
    wa1 = jnp.zeros((d, LANES), F32).at[:, :GLA_GATE_RANK].set(w_a1[0]).at[:, GLA_GATE_RANK:2 * GLA_GATE_RANK].set(w_a1[1])
    wa2 = jnp.zeros((LANES, 2 * GLA_DK), F32).at[:GLA_GATE_RANK, :GLA_DK].set(w_a2[0])
    wa2 = wa2.at[GLA_GATE_RANK:2 * GLA_GATE_RANK, GLA_DK:].set(w_a2[1])
    ba = b_a.reshape(1, 2 * GLA_DK)
    row = lambda i: (i, 0)
    const = lambda i: (0, 0)
    n_in = w_in.shape[1]
    outs = pl.pallas_call(
        _gla_front_kernel,
        grid=(n // tm,),
        in_specs=[pl.BlockSpec((tm, d), row),
                  pl.BlockSpec((1, 1, 3 * d), lambda i: (mod_idx(i), 0, 0)),
                  pl.BlockSpec((1, d), const),
                  pl.BlockSpec((d, n_in), const),
                  pl.BlockSpec((d, LANES), const),
                  pl.BlockSpec((LANES, 2 * GLA_DK), const),
                  pl.BlockSpec((1, 2 * GLA_DK), const)],
        out_specs=[pl.BlockSpec((tm, GLA_DK), row), pl.BlockSpec((tm, GLA_DK), row),
                   pl.BlockSpec((tm, GLA_DV), row), pl.BlockSpec((tm, GLA_DV), row),
                   pl.BlockSpec((tm, 2 * GLA_DK), row)],
        out_shape=[jax.ShapeDtypeStruct((n, GLA_DK), BF16), jax.ShapeDtypeStruct((n, GLA_DK), BF16),
                   jax.ShapeDtypeStruct((n, GLA_DV), BF16), jax.ShapeDtypeStruct((n, GLA_DV), BF16),
                   jax.ShapeDtypeStruct((n, 2 * GLA_DK), F32)],
        compiler_params=_params(1),
        name="gla_front",
    )(x, mod.reshape(nmod, 1, 3 * d), norm_w.reshape(1, d), w_in.astype(BF16), wa1.astype(BF16),
      wa2.astype(BF16), ba)
    return outs


def _gla_chunks(chains):
    flat = [(x, ch[2]) for ch in chains for x in ch[0]]
    c = flat[0][0][0].shape[0]
    dk = flat[0][0][1].shape[1]
    row = lax.broadcasted_iota(jnp.int32, (c, c), 0)
    col = lax.broadcasted_iota(jnp.int32, (c, c), 1)
    keep = [(col >= row) if rv else (col <= row) for _, rv in flat]
    b = [_tri_dot(kp.astype(F32), x[3]) for kp, (x, _) in zip(keep, flat)]
    b_end = [bb[0:1] if rv else bb[c - 1:c] for bb, (_, rv) in zip(b, flat)]
    qb = [(x[0] * jnp.exp(bb)).astype(BF16) for (x, _), bb in zip(flat, b)]
    kb = [x[1] * jnp.exp(-bb) for (x, _), bb in zip(flat, b)]
    kd = [x[1] * jnp.exp(be - bb) for (x, _), bb, be in zip(flat, b, b_end)]
    scores = [jnp.where(kp, _dot_nt(x, y), 0.0).astype(BF16) for kp, x, y in zip(keep, qb, kb)]
    upd = [_dot_tn(x, y[2]) for (y, _), x in zip(flat, kd)]
    decay = [jnp.broadcast_to(jnp.exp(be), (dk, dk)).T for be in b_end]
    n_steps = len(chains[0][0])
    st = [ch[1] for ch in chains]
    outs = [[] for _ in chains]
    for s in range(n_steps):
        at = [ci * n_steps + s for ci in range(len(chains))]
        for ci, f in enumerate(at):
            outs[ci].append(_dot(jnp.concatenate([scores[f], qb[f]], axis=1),
                                 jnp.concatenate([flat[f][0][2], st[ci].astype(BF16)], axis=0)))
        st = [st[ci] * jnp.concatenate([decay[f]] * (st[ci].shape[1] // dk), axis=1) + upd[f]
              for ci, f in enumerate(at)]
    return list(zip(outs, st))


def _gla_scan_kernel(*refs, has_s0, emit_state):
    q_ref, k_ref, v_ref, lf_ref, lb_ref = refs[:5]
    s0_ref = refs[5] if has_s0 else None
    o_ref = refs[5 + has_s0]
    s_ref = refs[6 + has_s0] if emit_state else None
    acc_ref, st_ref = refs[-2:]
    bb, t = q_ref.shape[0], q_ref.shape[1]
    hb = q_ref.shape[2] // GLA_DKH
    nc = t // CHUNK
    half = nc // 2
    kh = lambda h: slice(h * GLA_DKH, (h + 1) * GLA_DKH)
    vh = lambda h: slice(h * GLA_DVH, (h + 1) * GLA_DVH)

    steps = GLA_STEPS
    while half % steps:
        steps //= 2

    def both(accumulate):
        def body(i, carry):
            sls = [[pl.ds(pl.multiple_of(ci * CHUNK, CHUNK), CHUNK)
                    for ci in ((i * steps + s, nc - 1 - i * steps - s)[d] for s in range(steps))] for d in range(2)]
            chains = [([(q_ref[e, sl, kh(h)].astype(F32), k_ref[e, sl, kh(h)].astype(F32), v_ref[e, sl, vh(h)],
                         (lf_ref, lb_ref)[d][e, sl, kh(h)]) for sl in sls[d]], st_ref[n], d == 1)
                      for n, (e, d, h) in enumerate(idx)]
            res = _gla_chunks(chains)
            for n, (e, d, h) in enumerate(idx):
                for sl, o in zip(sls[d], res[n][0]):
                    if accumulate:
                        o_ref[e, sl, vh(h)] = (acc_ref[e, sl, vh(h)] + o).astype(BF16)
                    else:
                        acc_ref[e, sl, vh(h)] = o
            for n, (_, st) in enumerate(res):
                st_ref[n] = st
            return carry
        return body

    idx = [(e, d, h) for e in range(bb) for d in range(2) for h in range(hb)]
    for n, (e, d, h) in enumerate(idx):
        st_ref[n] = s0_ref[e, d, h] if has_s0 else jnp.zeros((GLA_DKH, GLA_DVH), F32)
    lax.fori_loop(0, half // steps, both(False), 0)
    lax.fori_loop(half // steps, nc // steps, both(True), 0)
    if emit_state:
        for n, (e, d, h) in enumerate(idx):
            s_ref[e, d, h] = st_ref[n]


def _gla_scan(q, k, v, lg, s0, bb, hb, emit_state):
    bsz, t, _ = q.shape
    ng = GLA_HEADS // hb
    hd = lambda b, h: (b, 0, h)
    kblk = pl.BlockSpec((bb, t, hb * GLA_DKH), hd)
    vblk = pl.BlockSpec((bb, t, hb * GLA_DVH), hd)
    sblk = pl.BlockSpec((bb, 2, hb, GLA_DKH, GLA_DVH), lambda b, h: (b, 0, h, 0, 0))
    s_shape = jax.ShapeDtypeStruct((bsz, 2, GLA_HEADS, GLA_DKH, GLA_DVH), F32)
    has_s0 = s0 is not None
    return pl.pallas_call(
        functools.partial(_gla_scan_kernel, has_s0=has_s0, emit_state=emit_state),
        grid=(bsz // bb, ng),
        in_specs=[kblk, kblk, vblk, kblk, pl.BlockSpec((bb, t, hb * GLA_DKH), lambda b, h: (b, 0, ng + h))]
        + [sblk] * has_s0,
        out_specs=[vblk] + [sblk] * emit_state,
        out_shape=[jax.ShapeDtypeStruct((bsz, t, GLA_DV), BF16)] + [s_shape] * emit_state,
        scratch_shapes=[pltpu.VMEM((bb, t, hb * GLA_DVH), F32), pltpu.VMEM((2 * bb * hb, GLA_DKH, GLA_DVH), F32)],
        compiler_params=_params(2),
        name="gla_scan",
    )(q, k, v, lg, lg, *([s0] * has_s0))


def _gla_back_kernel(o_ref, g_ref, x_ref, mod0_ref, mod1_ref, gn_ref, wout_ref, nw1_ref, x1_ref, h1_ref):
    d = D_MODEL
    o = o_ref[...].astype(F32)
    gn = gn_ref[...]
    parts = [_rms(o[:, h * GLA_DVH:(h + 1) * GLA_DVH]) * gn for h in range(GLA_HEADS)]
    on = jnp.concatenate(parts, axis=-1) * _silu(g_ref[...].astype(F32))
    out = _dot(on, wout_ref[...])
    gate = mod0_ref[0][:, 2 * d:]
    x1 = x_ref[...] + gate * out
    x1_ref[...] = x1
    mod1 = mod1_ref[0]
    h1_ref[...] = (_rms(x1) * nw1_ref[...] * (1.0 + mod1[:, d:2 * d]) + mod1[:, :d]).astype(BF16)


def _gla_back(o, g, x, mod0, mod1, mod_idx, tm, gla_norm, w_out, norm_w1):
    n, d = x.shape
    nmod = mod0.shape[0]
    row = lambda i: (i, 0)
    const = lambda i: (0, 0)
    modspec = pl.BlockSpec((1, 1, 3 * d), lambda i: (mod_idx(i), 0, 0))
    return pl.pallas_call(
        _gla_back_kernel,
        grid=(n // tm,),
        in_specs=[pl.BlockSpec((tm, d), row), pl.BlockSpec((tm, d), row), pl.BlockSpec((tm, d), row),
                  modspec, modspec,
                  pl.BlockSpec((1, GLA_DVH), const), pl.BlockSpec((d, d), const), pl.BlockSpec((1, d), const)],
        out_specs=[pl.BlockSpec((tm, d), row), pl.BlockSpec((tm, d), row)],
        out_shape=[jax.ShapeDtypeStruct((n, d), F32), jax.ShapeDtypeStruct((n, d), BF16)],
        compiler_params=_params(1),
        name="gla_back",
    )(o, g, x, mod0.reshape(nmod, 1, 3 * d), mod1.reshape(nmod, 1, 3 * d), gla_norm.reshape(1, GLA_DVH),
      w_out.astype(BF16), norm_w1.reshape(1, d))


def _token_shift(h, up_ref, dn_ref, seq_len, grid_w):
    tm = h.shape[0]
    pos = lax.broadcasted_iota(jnp.int32, (tm, 1), 0) + pl.program_id(0) * tm
    prev = pltpu.roll(h, 1, 0)
    nxt = pltpu.roll(h, tm - 1, 0)
    t = pos % seq_len
    if grid_w is None:
        return 0.5 * (jnp.where(t == 0, 0.0, prev) + jnp.where(t == seq_len - 1, 0.0, nxt))
    col = pos % grid_w
    up = jnp.concatenate([up_ref[...].astype(F32), h[:tm - grid_w]], axis=0)
    dn = jnp.concatenate([h[grid_w:], dn_ref[...].astype(F32)], axis=0)
    up = jnp.where(t < grid_w, 0.0, up)
    dn = jnp.where(t >= seq_len - grid_w, 0.0, dn)
    left = jnp.where(col == 0, 0.0, prev)
    right = jnp.where(col == grid_w - 1, 0.0, nxt)
    return 0.25 * (up + dn + left + right)


def _rwkv_front_kernel(*refs, seq_len, grid_w):
    n_h = 1 if grid_w is None else 3
    h_ref, up_ref, dn_ref = (tuple(refs[:n_h]) + (None, None))[:3]
    (mu_ref, wrkvg_ref, w1_ref, w2_ref, w0_ref, a1_ref, a2_ref, a0_ref,
     r_ref, k_ref, v_ref, g_ref, lw_ref, a_ref) = refs[n_h:]
    h = h_ref[...].astype(F32)
    dh = _token_shift(h, up_ref, dn_ref, seq_len, grid_w) - h
    mix = lambda p: (h + dh * mu_ref[p:p + 1, :]).astype(BF16)
    r_ref[...] = jnp.dot(mix(0), wrkvg_ref[0], preferred_element_type=F32).astype(BF16)
    k_ref[...] = jnp.dot(mix(1), wrkvg_ref[1], preferred_element_type=F32).astype(BF16)
    v_ref[...] = jnp.dot(mix(2), wrkvg_ref[2], preferred_element_type=F32).astype(BF16)
    g_ref[...] = jnp.dot(mix(3), wrkvg_ref[3], preferred_element_type=F32).astype(BF16)
    tw = jnp.tanh(jnp.dot(mix(4), w1_ref[...], preferred_element_type=F32))
    lw_ref[...] = -RWKV_DECAY_SCALE * _sigmoid(w0_ref[...] + _dot(tw, w2_ref[...]))
    ta = jnp.dot(mix(5), a1_ref[...], preferred_element_type=F32)
    a_ref[...] = _sigmoid(a0_ref[...] + _dot(ta, a2_ref[...])).astype(BF16)


def _rwkv_front(h, tm, seq_len, grid_w, mu, w_rkvg, w0, w1, w2, a0, a1, a2):
    n, d = h.shape
    rk = RWKV_RANK
    cat1 = lambda w: jnp.concatenate([w[0], w[1]], axis=1)
    bd2 = lambda w: jnp.zeros((2 * rk, 2 * d), F32).at[:rk, :d].set(w[0]).at[rk:, d:].set(w[1])
    row = lambda i: (i, 0)
    const = lambda i: (0, 0)
    if grid_w is None:
        assert tm % seq_len == 0
        h_specs, h_args = [pl.BlockSpec((tm, d), row)], [h]
    else:
        assert tm % grid_w == 0 and seq_len % tm == 0
        rows_per_tile, last = tm // grid_w, n // grid_w - 1
        h_specs = [pl.BlockSpec((tm, d), row),
                   pl.BlockSpec((grid_w, d), lambda i: (jnp.maximum(i * rows_per_tile - 1, 0), 0)),
                   pl.BlockSpec((grid_w, d), lambda i: (jnp.minimum((i + 1) * rows_per_tile, last), 0))]
        h_args = [h, h, h]
    return pl.pallas_call(
        functools.partial(_rwkv_front_kernel, seq_len=seq_len, grid_w=grid_w),
        grid=(n // tm,),
        in_specs=h_specs + [
            pl.BlockSpec((8, d), const),
            pl.BlockSpec((4, d, d), lambda i: (0, 0, 0)),
            pl.BlockSpec((d, 2 * rk), const), pl.BlockSpec((2 * rk, 2 * d), const), pl.BlockSpec((1, 2 * d), const),
            pl.BlockSpec((d, 2 * rk), const), pl.BlockSpec((2 * rk, 2 * d), const), pl.BlockSpec((1, 2 * d), const)],
        out_specs=[pl.BlockSpec((tm, d), row)] * 4 + [pl.BlockSpec((tm, 2 * d), row)] * 2,
        out_shape=[jax.ShapeDtypeStruct((n, d), BF16)] * 4
        + [jax.ShapeDtypeStruct((n, 2 * d), F32), jax.ShapeDtypeStruct((n, 2 * d), BF16)],
        compiler_params=_params(1),
        name="rwkv_front",
    )(*h_args, jnp.zeros((8, d), F32).at[:6].set(mu), w_rkvg.astype(BF16),
      cat1(w1).astype(BF16), bd2(w2).astype(BF16), w0.reshape(1, 2 * d),
      cat1(a1).astype(BF16), bd2(a2).astype(BF16), a0.reshape(1, 2 * d))


def _seg_sum(x, head0):
    s0 = jnp.sum(jnp.where(head0, x, 0.0), axis=-1, keepdims=True)
    s1 = jnp.sum(jnp.where(head0, 0.0, x), axis=-1, keepdims=True)
    return jnp.where(head0, s0, s1)


def _stack(x, head0):
    return jnp.concatenate([jnp.where(head0, x, 0.0), jnp.where(head0, 0.0, x)], axis=0)


def _rwkv_chunk_terms(insts):
    n = len(insts)
    c = insts[0][0].shape[0]
    lane = lax.broadcasted_iota(jnp.int32, (c, LANES), 1)
    head0 = lane < RWKV_HEAD
    row = lax.broadcasted_iota(jnp.int32, (c, c), 0)
    col = lax.broadcasted_iota(jnp.int32, (c, c), 1)
    ti = lax.broadcasted_iota(jnp.int32, (c, LANES), 0)
    si = lane & (c - 1)
    prow = lax.broadcasted_iota(jnp.int32, (LANES, LANES), 0)
    pcol = lax.broadcasted_iota(jnp.int32, (LANES, LANES), 1)
    same_head = (prow >= RWKV_HEAD) == (pcol >= RWKV_HEAD)
    each = lambda f, *ls: [f(*xs) for xs in zip(*ls)]
    rev = [x[8] for x in insts]
    stk = lambda x: _stack(x, head0).astype(BF16)

    cum = [_tri_dot(((col >= row) if x[8] else (col <= row)).astype(F32), x[3]) for x in insts]
    tot = [cm[0:1] if rv else cm[c - 1:c] for cm, rv in zip(cum, rev)]

    a_t, r_t, b_ts, k_ts, b_h, k_h, v_s, vv, bonus = [], [], [], [], [], [], [], [], []
    for (r, k, v, lw, a, k_k, k_a, r_k, _), cm, tt in zip(insts, cum, tot):
        kk = k * k_k
        kk = kk * lax.rsqrt(jnp.maximum(_seg_sum(kk * kk, head0), 1e-24))
        kd = k * (1.0 + (a - 1.0) * k_a)
        bv = kk * a
        bonus.append(_seg_sum(r * kd * r_k, head0) * v)
        e_neg = jnp.exp(-cm)
        e_tot = jnp.exp(tt)
        b_neg, k_neg = bv * e_neg, kd * e_neg
        a_t.append(-kk * jnp.exp(cm - lw))
        r_t.append(r * jnp.exp(cm))
        b_ts.append(stk(b_neg))
        k_ts.append(stk(k_neg))
        b_h.append(b_neg * e_tot)
        k_h.append(k_neg * e_tot)
        v_s.append(stk(v))
        vv.append(v)

    strict = [(si > ti) if rv else (si < ti) for rv in rev]
    incl = [(si >= ti) if rv else (si <= ti) for rv in rev]
    ar = each(lambda x, y: jnp.concatenate([x, y], axis=0).astype(BF16), a_t, r_t)
    g_bk = each(lambda x, y, z: _dot_nt(x, jnp.concatenate([y, z], axis=0)), ar, b_ts, k_ts)
    g_b = [x[:, :LANES] for x in g_bk]
    g_k = [x[:, LANES:] for x in g_bk]
    n_ab = each(lambda m_, x: jnp.where(m_, x[:c], 0.0), strict, g_b)
    l_ak = each(lambda m_, x: jnp.where(m_, x[:c], 0.0), strict, g_k)
    p_rb = each(lambda m_, x: jnp.where(m_, x[c:], 0.0), incl, g_b)
    p_rk = each(lambda m_, x: jnp.where(m_, x[c:], 0.0), incl, g_k)

    same_blk = lambda h: (ti // h) == (si // h)
    n_d = each(lambda x: jnp.where(same_blk(INV_BASE), x, 0.0), n_ab)
    steps = max(1, (INV_BASE - 1).bit_length())
    q = n_d
    tinv = n_d
    q = each(lambda x: _dot(x, stk(x)), q) if steps > 1 else q
    for j in range(1, steps):
        last = j == steps - 1
        qs = each(stk, q)
        if last:
            z = each(_dot, tinv, qs)
            tinv = each(lambda t_, x, y: t_ + x + y, tinv, q, z)
        else:
            both = each(lambda x, t_, w_: _dot(jnp.concatenate([x, t_], axis=0), w_), q, tinv, qs)
            tinv = each(lambda t_, x, y: t_ + x + y[c:], tinv, q, both)
            q = each(lambda y: y[:c], both)
    pair = lambda f, x, y: _dot(f, jnp.concatenate([stk(x), stk(y)], axis=1))
    link = lambda h: same_blk(2 * h) & jnp.logical_not(same_blk(h))
    n_off = lambda h: [jnp.where(link(h), x, 0.0) for x in n_ab]
    grow = lambda t_, y, z: t_ + y + z
    h = INV_BASE
    while h < c:
        if 4 * h <= c:
            n_1, n_2 = n_off(h), n_off(2 * h)
            p_1 = each(pair, tinv, n_1, n_2)
            y_1 = each(lambda x, z: x + z[:, :LANES], n_1, p_1)
            b_2 = each(lambda x, z: x + z[:, LANES:], n_2, p_1)
            p_2 = each(pair, y_1, tinv, b_2)
            tinv = each(lambda t_, y, z: grow(t_, y, z[:, :LANES]), tinv, y_1, p_2)
            y_2 = each(lambda x, z: x + z[:, LANES:], b_2, p_2)
            tinv = each(lambda t_, y: grow(t_, y, _dot(y, stk(t_))), tinv, y_2)
            h *= 4
        else:
            y = each(lambda t_, x: x + _dot(t_, stk(x)), tinv, n_off(h))
            tinv = each(lambda t_, y_: grow(t_, y_, _dot(y_, stk(t_))), tinv, y)
            h *= 2
    lv = each(lambda l_, p, y: _dot(jnp.concatenate([l_, p], axis=0), y), l_ak, p_rk, v_s)
    w = [x[:c] for x in lv]
    au = each(pair, tinv, a_t, w)
    a_bar = each(lambda x, z: x + z[:, :LANES], a_t, au)
    u0 = each(lambda x, z: x + z[:, LANES:], w, au)
    ry = each(pair, p_rb, a_bar, u0)
    r_bar = each(lambda x, z: x + z[:, :LANES], r_t, ry)
    y0 = each(lambda z, l_: z[:, LANES:] + l_[c:], ry, lv)
    zero = jnp.zeros((c, LANES), F32)
    ms = each(lambda ab, u, v_, bh, kh_: _dot_tn(
        jnp.concatenate([jnp.concatenate([ab, u], axis=1), jnp.concatenate([zero, v_], axis=1)], axis=0),
        jnp.concatenate([bh, kh_], axis=0)), a_bar, u0, vv, b_h, k_h)
    m = [jnp.where(same_head, x[:LANES], 0.0) for x in ms]
    s0p = [jnp.where(same_head, x[LANES:], 0.0) for x in ms]
    return [(r_bar[i], y0[i], m[i], s0p[i], jnp.exp(tot[i]), bonus[i]) for i in range(n)]


def _rwkv_scan_kernel(*refs, has_s0, emit_state):
    (r_ref, k_ref, v_ref, lwf_ref, lwb_ref, af_ref, ab_ref,
     kkf_ref, kkb_ref, kaf_ref, kab_ref, rkf_ref, rkb_ref, lnw_ref, lnb_ref) = refs[:15]
    s0_ref = refs[15] if has_s0 else None
    o_ref = refs[15 + has_s0]
    s_ref = refs[16 + has_s0] if emit_state else None
    w_ref, y0_ref, s0p_ref, dec_ref, bonus_ref, yf_ref, yb_ref, st_ref = refs[16 + has_s0 + emit_state:]
    bb, t = r_ref.shape[0], r_ref.shape[1]
    nc = t // CHUNK
    dirs = ((lwf_ref, af_ref, kkf_ref, kaf_ref, rkf_ref, False), (lwb_ref, ab_ref, kkb_ref, kab_ref, rkb_ref, True))
    group = min(bb * nc, PREP_GROUP)
    assert (bb * nc) % group == 0
    chunk_rows = lambda ci: pl.ds(pl.multiple_of(ci * CHUNK, CHUNK), CHUNK)

    def prep(gi, carry):
        insts, where = [], []
        at = [((gi * group + j) // nc, chunk_rows((gi * group + j) % nc)) for j in range(group)]
        for j, (e, sl) in enumerate(at):
            r, k, v = (ref[e, sl, :].astype(F32) for ref in (r_ref, k_ref, v_ref))
            for d, (lw_ref, a_ref, kk_ref, ka_ref, rk_ref, reverse) in enumerate(dirs):
                insts.append((r, k, v, lw_ref[e, sl, :], a_ref[e, sl, :].astype(F32), kk_ref[...], ka_ref[...],
                              rk_ref[...], reverse))
                where.append((d, gi * group + j, e, sl))
        terms = _rwkv_chunk_terms(insts)
        for (d, fi, e, sl), (r_bar, y0, m, s0p, dec, bon) in zip(where, terms):
            rbar_t = jnp.concatenate([r_bar, jnp.zeros_like(r_bar)], axis=0).T
            w_ref[d, fi] = jnp.concatenate([m, rbar_t], axis=1).astype(BF16)
            y0_ref[d, fi] = y0
            s0p_ref[d, fi] = s0p
            dec_ref[d, fi] = jnp.broadcast_to(dec, (8, LANES))
        for j in range(group):
            _, _, e, sl = where[2 * j]
            bonus_ref[e, sl, :] = terms[2 * j][5] + terms[2 * j + 1][5]
        return carry

    lax.fori_loop(0, bb * nc // group, prep, 0)

    seq_steps = SEQ_STEPS if nc % SEQ_STEPS == 0 else 1
    idx = [(d, e) for d in range(2) for e in range(bb)]

    def seq(i, _):
        carry = tuple(st_ref[n] for n in range(len(idx)))
        for s in range(seq_steps):
            cis = (i * seq_steps + s, nc - 1 - i * seq_steps - s)
            sw = [jnp.dot(carry[n].astype(BF16), w_ref[d, e * nc + cis[d]], preferred_element_type=F32)
                  for n, (d, e) in enumerate(idx)]
            for n, (d, e) in enumerate(idx):
                y = sw[n][:, LANES:].T[:CHUNK] + y0_ref[d, e * nc + cis[d]]
                (yf_ref, yb_ref)[d][e, chunk_rows(cis[d]), :] = y
            carry = tuple(carry[n] * dec_ref[d, e * nc + cis[d]][0:1] + sw[n][:, :LANES] + s0p_ref[d, e * nc + cis[d]]
                          for n, (d, e) in enumerate(idx))
        for n, st in enumerate(carry):
            st_ref[n] = st
        return 0

    for n, (d, e) in enumerate(idx):
        st_ref[n] = s0_ref[e, d, 0] if has_s0 else jnp.zeros((LANES, LANES), F32)
    lax.fori_loop(0, nc // seq_steps, seq, 0)
    if emit_state:
        for d in range(2):
            for e in range(bb):
                s = st_ref[d * bb + e]
                s_ref[e, d, 0] = s[:RWKV_HEAD, :RWKV_HEAD]
                s_ref[e, d, 1] = pltpu.roll(s, RWKV_HEAD, 1)[RWKV_HEAD:, :RWKV_HEAD]

    prow = lax.broadcasted_iota(jnp.int32, (LANES, LANES), 0)
    pcol = lax.broadcasted_iota(jnp.int32, (LANES, LANES), 1)
    head_ones = ((prow >= RWKV_HEAD) == (pcol >= RWKV_HEAD)).astype(BF16)
    inv_n = 1.0 / RWKV_HEAD

    def head_mean(x):
        hi = x.astype(BF16)
        lo = (x - hi.astype(F32)).astype(BF16)
        both = jnp.dot(jnp.concatenate([hi, lo], axis=0), head_ones, preferred_element_type=F32)
        return (both[:x.shape[0]] + both[x.shape[0]:]) * inv_n

    fgroup = min(bb * nc, FINISH_GROUP)
    assert (bb * nc) % fgroup == 0

    def finish(gi, carry):
        at = [((gi * fgroup + j) // nc, chunk_rows((gi * fgroup + j) % nc)) for j in range(fgroup)]
        y = [yf_ref[e, sl, :] + yb_ref[e, sl, :] for e, sl in at]
        yc = [x - head_mean(x) for x in y]
        var = [head_mean(x * x) for x in yc]
        for (e, sl), x, vr in zip(at, yc, var):
            o_ref[e, sl, :] = (x * lax.rsqrt(vr + LNX_EPS) * lnw_ref[...] + lnb_ref[...]
                               + bonus_ref[e, sl, :]).astype(BF16)
        return carry

    lax.fori_loop(0, bb * nc // fgroup, finish, 0)


def _rwkv_scan(r, k, v, lw, a, k_k, k_a, r_k, lnx_w, lnx_b, s0, bb, emit_state):
    bsz, t, d = r.shape
    nc = t // CHUNK
    seq = lambda b, p: (b, 0, p)
    seq_b = lambda b, p: (b, 0, RWKV_PAIRS + p)
    par = lambda b, p: (0, p)
    par_b = lambda b, p: (0, RWKV_PAIRS + p)
    sblk = pl.BlockSpec((bb, t, LANES), seq)
    sblk_b = pl.BlockSpec((bb, t, LANES), seq_b)
    pblk = pl.BlockSpec((1, LANES), par)
    pblk_b = pl.BlockSpec((1, LANES), par_b)
    s0blk = pl.BlockSpec((bb, 2, 1, LANES, LANES), lambda b, p: (b, 0, p, 0, 0))
    s_blk = pl.BlockSpec((bb, 2, 2, RWKV_HEAD, RWKV_HEAD), lambda b, p: (b, 0, p, 0, 0))
    s_shape = jax.ShapeDtypeStruct((bsz, 2, RWKV_HEADS, RWKV_HEAD, RWKV_HEAD), F32)
    has_s0 = s0 is not None
    k_k, k_a, r_k = (p.reshape(1, 2 * d) for p in (k_k, k_a, r_k))
    return pl.pallas_call(
        functools.partial(_rwkv_scan_kernel, has_s0=has_s0, emit_state=emit_state),
        grid=(bsz // bb, RWKV_PAIRS),
        in_specs=[sblk, sblk, sblk, sblk, sblk_b, sblk, sblk_b,
                  pblk, pblk_b, pblk, pblk_b, pblk, pblk_b, pblk, pblk] + [s0blk] * has_s0,
        out_specs=[sblk] + [s_blk] * emit_state,
        out_shape=[jax.ShapeDtypeStruct((bsz, t, d), BF16)] + [s_shape] * emit_state,
        scratch_shapes=[pltpu.VMEM((2, bb * nc, LANES, 2 * LANES), BF16), pltpu.VMEM((2, bb * nc, CHUNK, LANES), F32),
                        pltpu.VMEM((2, bb * nc, LANES, LANES), F32),
                        pltpu.VMEM((2, bb * nc, 8, LANES), F32), pltpu.VMEM((bb, t, LANES), F32),
                        pltpu.VMEM((bb, t, LANES), F32), pltpu.VMEM((bb, t, LANES), F32),
                        pltpu.VMEM((2 * bb, LANES, LANES), F32)],
        compiler_params=_params(2),
        name="rwkv_scan",
    )(r, k, v, lw, lw, a, a, k_k, k_k, k_a, k_a, r_k, r_k,
      lnx_w.reshape(1, d), lnx_b.reshape(1, d), *([s0] * has_s0))


def _rwkv_back_kernel(o_ref, g_ref, x_ref, mod_ref, wout_ref, nf_ref, y_ref):
    out = _dot(o_ref[...].astype(F32) * _silu(g_ref[...].astype(F32)), wout_ref[...])
    x2 = x_ref[...] + mod_ref[0][:, 2 * D_MODEL:] * out
    y_ref[...] = _rms(x2) * nf_ref[...]


def _rwkv_back(o, g, x, mod1, mod_idx, tm, w_out, norm_f):
    n, d = x.shape
    nmod = mod1.shape[0]
    row = lambda i: (i, 0)
    const = lambda i: (0, 0)
    return pl.pallas_call(
        _rwkv_back_kernel,
        grid=(n // tm,),
        in_specs=[pl.BlockSpec((tm, d), row), pl.BlockSpec((tm, d), row), pl.BlockSpec((tm, d), row),
                  pl.BlockSpec((1, 1, 3 * d), lambda i: (mod_idx(i), 0, 0)),
                  pl.BlockSpec((d, d), const), pl.BlockSpec((1, d), const)],
        out_specs=pl.BlockSpec((tm, d), row),
        out_shape=jax.ShapeDtypeStruct((n, d), F32),
        compiler_params=_params(1),
        name="rwkv_back",
    )(o, g, x, mod1.reshape(nmod, 1, 3 * d), w_out.astype(BF16), norm_f.reshape(1, d))


def _pair_blockdiag(s):
    b = s.shape[0]
    s = s.reshape(b, 2, RWKV_PAIRS, 2, RWKV_HEAD, RWKV_HEAD)
    z = jnp.zeros_like(s[:, :, :, 0])
    top = jnp.concatenate([s[:, :, :, 0], z], axis=-1)
    bot = jnp.concatenate([z, s[:, :, :, 1]], axis=-1)
    return jnp.concatenate([top, bot], axis=-2)


def _tile_rows(n_seq, t, may_span, rows):
    if t % rows == 0:
        return rows
    if may_span and rows % t == 0 and (n_seq * t) % rows == 0:
        return rows
    return t


def kernel(x_prompt, x_sample, state_gla, state_rwkv, c, c_ctx, w_mod, b_mod, norm_w, gla_w_in, gla_w_a1,
           gla_w_a2, gla_b_a, gla_norm, gla_w_out, rwkv_mu, rwkv_w_rkvg, rwkv_w0, rwkv_w1, rwkv_w2, rwkv_a0,
           rwkv_a1, rwkv_a2, rwkv_k_k, rwkv_k_a, rwkv_r_k, rwkv_lnx_w, rwkv_lnx_b, rwkv_w_out, norm_f):
    d = D_MODEL
    bp, tp, _ = x_prompt.shape
    bs, ts, _ = x_sample.shape
    assert tp % CHUNK == 0 and ts % CHUNK == 0 and ts % GRID_W == 0

    nrows = -(-(1 + bs) // 8) * 8
    cond = jnp.zeros((nrows, d), F32).at[0].set(c_ctx).at[1:1 + bs].set(c)
    cond = cond * (1.0 / (1.0 + jnp.exp(-cond)))
    mod0 = _matmul_bias(cond, w_mod[0], b_mod[0].reshape(1, 3 * d))
    mod1 = _matmul_bias(cond, w_mod[1], b_mod[1].reshape(1, 3 * d))

    def trunk(x3, shared_mod, grid_w, gla_s0, rwkv_s0, gla_bb, rwkv_bb, emit_state):
        b, t, _ = x3.shape
        n = b * t
        tm, tw = (_tile_rows(b, t, shared_mod, rows) for rows in (TOKEN_TILE, WIDE_TILE))
        mod_row = lambda rows: (lambda i: 0) if shared_mod else (lambda i: 1 + i // (t // rows))
        x = x3.reshape(n, d)
        seq = lambda arr: arr.reshape(b, t, arr.shape[-1])
        q, k, v, g, lg = _gla_front(x, mod0, mod_row(tw), tw, norm_w[0], gla_w_in[0], gla_w_a1[0], gla_w_a2[0], gla_b_a[0])
        o, *gla_s = _gla_scan(seq(q), seq(k), seq(v), seq(lg), gla_s0, gla_bb, GLA_HEADS, emit_state)
        x1, h1 = _gla_back(o.reshape(n, d), g, x, mod0, mod1, mod_row(tw), tw, gla_norm[0], gla_w_out[0], norm_w[1])
        r, k, v, g, lw, a = _rwkv_front(h1, tm, t, grid_w, rwkv_mu[0], rwkv_w_rkvg[0], rwkv_w0[0], rwkv_w1[0], rwkv_w2[0],
                                        rwkv_a0[0], rwkv_a1[0], rwkv_a2[0])
        o, *rwkv_s = _rwkv_scan(seq(r), seq(k), seq(v), seq(lw), seq(a), rwkv_k_k[0], rwkv_k_a[0], rwkv_r_k[0],
                                rwkv_lnx_w[0], rwkv_lnx_b[0], rwkv_s0, rwkv_bb, emit_state)
        y = _rwkv_back(o.reshape(n, d), g, x1, mod1, mod_row(tw), tw, rwkv_w_out[0], norm_f)
        return y.reshape(b, t, d), gla_s, rwkv_s

    y_prompt, (gla_s,), (rwkv_s,) = trunk(x_prompt, True, None, None, None, min(bp, 2), min(bp, 8), True)

    y_sample, _, _ = trunk(x_sample, False, GRID_W, state_gla[:, 0], _pair_blockdiag(state_rwkv[:, 0]),
                           1, min(bs, 4), False)
    return (y_prompt, y_sample, gla_s[:, None], rwkv_s[:, None])
```

```python
import functools

import jax
import jax.numpy as jnp
from jax import lax
from jax.experimental import pallas as pl
from jax.experimental.pallas import tpu as pltpu

F32 = jnp.float32
BF16 = jnp.bfloat16

D_MODEL = 1024
EPS = 1e-6
GRID_W = 64
GLA_HEADS = 4
GLA_DK = 512
GLA_DV = 1024
GLA_DKH = 128
GLA_DVH = 256
GLA_GATE_RANK = 16
GLA_GATE_NORM = 16.0
RWKV_HEAD = 64
RWKV_HEADS = 16
RWKV_PAIRS = 8
RWKV_RANK = 64
LNX_EPS = 64e-5
RWKV_DECAY_SCALE = 0.6065306597126334
CHUNK = 64
SEQ_STEPS = 4
GLA_STEPS = 4
INV_BASE = 8
TOKEN_TILE = 512
WIDE_TILE = 1024
FINISH_GROUP = 32
PREP_GROUP = 8
LANES = 128
VMEM_LIMIT = 56 * 1024 * 1024


def _dot(a, b):
    return jnp.dot(a.astype(BF16), b.astype(BF16), preferred_element_type=F32)


def _dot_nt(a, b):
    return lax.dot_general(a.astype(BF16), b.astype(BF16), (((1,), (1,)), ((), ())),
                           preferred_element_type=F32)


def _dot_tn(a, b):
    return lax.dot_general(a.astype(BF16), b.astype(BF16), (((0,), (0,)), ((), ())),
                           preferred_element_type=F32)


def _tri_dot(tri, x):
    hi = x.astype(BF16)
    lo = (x - hi.astype(F32)).astype(BF16)
    t = tri.astype(BF16)
    return jnp.dot(t, hi, preferred_element_type=F32) + jnp.dot(t, lo, preferred_element_type=F32)


def _log_sigmoid(z):
    return jnp.minimum(z, 0.0) - jnp.log(1.0 + jnp.exp(-jnp.abs(z)))


def _sigmoid(z):
    return 0.5 * jnp.tanh(0.5 * z) + 0.5


def _silu(z):
    return z * _sigmoid(z)


def _rms(x):
    return x * lax.rsqrt(jnp.mean(x * x, axis=-1, keepdims=True) + EPS)


def _params(n_grid_dims):
    return pltpu.CompilerParams(dimension_semantics=("arbitrary",) * n_grid_dims,
                                vmem_limit_bytes=VMEM_LIMIT)


def _mm_kernel(x_ref, w_ref, b_ref, o_ref):
    o_ref[...] = _dot(x_ref[...], w_ref[...]) + b_ref[...]


def _matmul_bias(x, w, b):
    m, k = x.shape
    n = w.shape[1]
    return pl.pallas_call(
        _mm_kernel,
        grid=(1,),
        in_specs=[pl.BlockSpec((m, k), lambda i: (0, 0)),
                  pl.BlockSpec((k, n), lambda i: (0, 0)),
                  pl.BlockSpec((1, n), lambda i: (0, 0))],
        out_specs=pl.BlockSpec((m, n), lambda i: (0, 0)),
        out_shape=jax.ShapeDtypeStruct((m, n), F32),
        compiler_params=_params(1),
        name="mod_matmul",
    )(x, w.astype(BF16), b)


def _gla_front_kernel(x_ref, mod_ref, nw_ref, win_ref, wa1_ref, wa2_ref, ba_ref,
                      q_ref, k_ref, v_ref, g_ref, lg_ref):
    d = D_MODEL
    mod = mod_ref[0]
    shift, scale = mod[:, :d], mod[:, d:2 * d]
    h = _rms(x_ref[...]) * nw_ref[...] * (1.0 + scale) + shift
    hb = h.astype(BF16)
    proj = lambda lo, hi: jnp.dot(hb, win_ref[:, lo:hi], preferred_element_type=F32)
    q_ref[...] = (proj(0, GLA_DK) * (GLA_DKH ** -0.5)).astype(BF16)
    k_ref[...] = proj(GLA_DK, 2 * GLA_DK).astype(BF16)
    v_ref[...] = proj(2 * GLA_DK, 2 * GLA_DK + GLA_DV).astype(BF16)
    g_ref[...] = proj(2 * GLA_DK + GLA_DV, 2 * GLA_DK + 2 * GLA_DV).astype(BF16)
    t = jnp.dot(hb, wa1_ref[...], preferred_element_type=F32)
    z = _dot(t, wa2_ref[...]) + ba_ref[...]
    lg_ref[...] = _log_sigmoid(z) * (1.0 / GLA_GATE_NORM)


def _gla_front(x, mod, mod_idx, tm, norm_w, w_in, w_a1, w_a2, b_a):
    n, d = x.shape
    nmod = mod.shape[0]
    wa1 = jnp.zeros((d, LANES), F32).at[:, :GLA_GATE_RANK].set(w_a1[0]).at[:, GLA_GATE_RANK:2 * GLA_GATE_RANK].set(w_a1[1])
    wa2 = jnp.zeros((LANES, 2 * GLA_DK), F32).at[:GLA_GATE_RANK, :GLA_DK].set(w_a2[0])
    wa2 = wa2.at[GLA_GATE_RANK:2 * GLA_GATE_RANK, GLA_DK:].set(w_a2[1])
    ba = b_a.reshape(1, 2 * GLA_DK)
    row = lambda i: (i, 0)
    const = lambda i: (0, 0)
    n_in = w_in.shape[1]
    outs = pl.pallas_call(
        _gla_front_kernel,
        grid=(n // tm,),
        in_specs=[pl.BlockSpec((tm, d), row),
                  pl.BlockSpec((1, 1, 3 * d), lambda i: (mod_idx(i), 0, 0)),
                  pl.BlockSpec((1, d), const),
                  pl.BlockSpec((d, n_in), const),
                  pl.BlockSpec((d, LANES), const),
                  pl.BlockSpec((LANES, 2 * GLA_DK), const),
                  pl.BlockSpec((1, 2 * GLA_DK), const)],
        out_specs=[pl.BlockSpec((tm, GLA_DK), row), pl.BlockSpec((tm, GLA_DK), row),
                   pl.BlockSpec((tm, GLA_DV), row), pl.BlockSpec((tm, GLA_DV), row),
                   pl.BlockSpec((tm, 2 * GLA_DK), row)],
        out_shape=[jax.ShapeDtypeStruct((n, GLA_DK), BF16), jax.ShapeDtypeStruct((n, GLA_DK), BF16),
                   jax.ShapeDtypeStruct((n, GLA_DV), BF16), jax.ShapeDtypeStruct((n, GLA_DV), BF16),
                   jax.ShapeDtypeStruct((n, 2 * GLA_DK), F32)],
        compiler_params=_params(1),
        name="gla_front",
    )(x, mod.reshape(nmod, 1, 3 * d), norm_w.reshape(1, d), w_in.astype(BF16), wa1.astype(BF16),
      wa2.astype(BF16), ba)
    return outs


def _gla_chunks(chains):
    flat = [(x, ch[2]) for ch in chains for x in ch[0]]
    c = flat[0][0][0].shape[0]
    dk = flat[0][0][1].shape[1]
    row = lax.broadcasted_iota(jnp.int32, (c, c), 0)
    col = lax.broadcasted_iota(jnp.int32, (c, c), 1)
    keep = [(col >= row) if rv else (col <= row) for _, rv in flat]
    b = [_tri_dot(kp.astype(F32), x[3]) for kp, (x, _) in zip(keep, flat)]
    b_end = [bb[0:1] if rv else bb[c - 1:c] for bb, (_, rv) in zip(b, flat)]
    qb = [(x[0] * jnp.exp(bb)).astype(BF16) for (x, _), bb in zip(flat, b)]
    kb = [x[1] * jnp.exp(-bb) for (x, _), bb in zip(flat, b)]
    kd = [x[1] * jnp.exp(be - bb) for (x, _), bb, be in zip(flat, b, b_end)]
    scores = [jnp.where(kp, _dot_nt(x, y), 0.0).astype(BF16) for kp, x, y in zip(keep, qb, kb)]
    decay = [jnp.broadcast_to(jnp.exp(be), (dk, dk)).T for be in b_end]
    n_steps = len(chains[0][0])
    st = [ch[1] for ch in chains]
    outs = [[] for _ in chains]
    for s in range(n_steps):
        at = [ci * n_steps + s for ci in range(len(chains))]
        for ci, f in enumerate(at):
            outs[ci].append(_dot(jnp.concatenate([scores[f], qb[f]], axis=1),
                                 jnp.concatenate([flat[f][0][2], st[ci].astype(BF16)], axis=0)))
        st = [st[ci] * jnp.concatenate([decay[f]] * (st[ci].shape[1] // dk), axis=1) + _dot_tn(kd[f], flat[f][0][2])
              for ci, f in enumerate(at)]
    return list(zip(outs, st))


def _gla_scan_kernel(*refs, has_s0, emit_state):
    q_ref, k_ref, v_ref, lf_ref, lb_ref = refs[:5]
    s0_ref = refs[5] if has_s0 else None
    o_ref = refs[5 + has_s0]
    s_ref = refs[6 + has_s0] if emit_state else None
    acc_ref, st_ref = refs[-2:]
    bb, t = q_ref.shape[0], q_ref.shape[1]
    hb = q_ref.shape[2] // GLA_DKH
    nc = t // CHUNK
    half = nc // 2
    kh = lambda h: slice(h * GLA_DKH, (h + 1) * GLA_DKH)
    vh = lambda h: slice(h * GLA_DVH, (h + 1) * GLA_DVH)

    steps = GLA_STEPS
    while half % steps:
        steps //= 2

    def both(accumulate):
        def body(i, carry):
            sls = [[pl.ds(pl.multiple_of(ci * CHUNK, CHUNK), CHUNK)
                    for ci in ((i * steps + s, nc - 1 - i * steps - s)[d] for s in range(steps))] for d in range(2)]
            chains = [([(q_ref[e, sl, kh(h)].astype(F32), k_ref[e, sl, kh(h)].astype(F32), v_ref[e, sl, vh(h)],
                         (lf_ref, lb_ref)[d][e, sl, kh(h)]) for sl in sls[d]], st_ref[n], d == 1)
                      for n, (e, d, h) in enumerate(idx)]
            res = _gla_chunks(chains)
            for n, (e, d, h) in enumerate(idx):
                for sl, o in zip(sls[d], res[n][0]):
                    if accumulate:
                        o_ref[e, sl, vh(h)] = (acc_ref[e, sl, vh(h)] + o).astype(BF16)
                    else:
                        acc_ref[e, sl, vh(h)] = o
            for n, (_, st) in enumerate(res):
                st_ref[n] = st
            return carry
        return body

    idx = [(e, d, h) for e in range(bb) for d in range(2) for h in range(hb)]
    for n, (e, d, h) in enumerate(idx):
        st_ref[n] = s0_ref[e, d, h] if has_s0 else jnp.zeros((GLA_DKH, GLA_DVH), F32)
    lax.fori_loop(0, half // steps, both(False), 0)
    lax.fori_loop(half // steps, nc // steps, both(True), 0)
    if emit_state:
        for n, (e, d, h) in enumerate(idx):
            s_ref[e, d, h] = st_ref[n]


def _gla_scan(q, k, v, lg, s0, bb, hb, emit_state):
    bsz, t, _ = q.shape
    ng = GLA_HEADS // hb
    hd = lambda b, h: (b, 0, h)
    kblk = pl.BlockSpec((bb, t, hb * GLA_DKH), hd)
    vblk = pl.BlockSpec((bb, t, hb * GLA_DVH), hd)
    sblk = pl.BlockSpec((bb, 2, hb, GLA_DKH, GLA_DVH), lambda b, h: (b, 0, h, 0, 0))
    s_shape = jax.ShapeDtypeStruct((bsz, 2, GLA_HEADS, GLA_DKH, GLA_DVH), F32)
    has_s0 = s0 is not None
    return pl.pallas_call(
        functools.partial(_gla_scan_kernel, has_s0=has_s0, emit_state=emit_state),
        grid=(bsz // bb, ng),
        in_specs=[kblk, kblk, vblk, kblk, pl.BlockSpec((bb, t, hb * GLA_DKH), lambda b, h: (b, 0, ng + h))]
        + [sblk] * has_s0,
        out_specs=[vblk] + [sblk] * emit_state,
        out_shape=[jax.ShapeDtypeStruct((bsz, t, GLA_DV), BF16)] + [s_shape] * emit_state,
        scratch_shapes=[pltpu.VMEM((bb, t, hb * GLA_DVH), F32), pltpu.VMEM((2 * bb * hb, GLA_DKH, GLA_DVH), F32)],
        compiler_params=_params(2),
        name="gla_scan",
    )(q, k, v, lg, lg, *([s0] * has_s0))


def _gla_back_kernel(o_ref, g_ref, x_ref, mod0_ref, mod1_ref, gn_ref, wout_ref, nw1_ref, x1_ref, h1_ref):
    d = D_MODEL
    o = o_ref[...].astype(F32)
    gn = gn_ref[...]
    parts = [_rms(o[:, h * GLA_DVH:(h + 1) * GLA_DVH]) * gn for h in range(GLA_HEADS)]
    on = jnp.concatenate(parts, axis=-1) * _silu(g_ref[...].astype(F32))
    out = _dot(on, wout_ref[...])
    gate = mod0_ref[0][:, 2 * d:]
    x1 = x_ref[...] + gate * out
    x1_ref[...] = x1
    mod1 = mod1_ref[0]
    h1_ref[...] = (_rms(x1) * nw1_ref[...] * (1.0 + mod1[:, d:2 * d]) + mod1[:, :d]).astype(BF16)


def _gla_back(o, g, x, mod0, mod1, mod_idx, tm, gla_norm, w_out, norm_w1):
    n, d = x.shape
    nmod = mod0.shape[0]
    row = lambda i: (i, 0)
    const = lambda i: (0, 0)
    modspec = pl.BlockSpec((1, 1, 3 * d), lambda i: (mod_idx(i), 0, 0))
    return pl.pallas_call(
        _gla_back_kernel,
        grid=(n // tm,),
        in_specs=[pl.BlockSpec((tm, d), row), pl.BlockSpec((tm, d), row), pl.BlockSpec((tm, d), row),
                  modspec, modspec,
                  pl.BlockSpec((1, GLA_DVH), const), pl.BlockSpec((d, d), const), pl.BlockSpec((1, d), const)],
        out_specs=[pl.BlockSpec((tm, d), row), pl.BlockSpec((tm, d), row)],
        out_shape=[jax.ShapeDtypeStruct((n, d), F32), jax.ShapeDtypeStruct((n, d), BF16)],
        compiler_params=_params(1),
        name="gla_back",
    )(o, g, x, mod0.reshape(nmod, 1, 3 * d), mod1.reshape(nmod, 1, 3 * d), gla_norm.reshape(1, GLA_DVH),
      w_out.astype(BF16), norm_w1.reshape(1, d))


def _token_shift(h, up_ref, dn_ref, seq_len, grid_w):
    tm = h.shape[0]
    pos = lax.broadcasted_iota(jnp.int32, (tm, 1), 0) + pl.program_id(0) * tm
    prev = pltpu.roll(h, 1, 0)
    nxt = pltpu.roll(h, tm - 1, 0)
    t = pos % seq_len
    if grid_w is None:
        return 0.5 * (jnp.where(t == 0, 0.0, prev) + jnp.where(t == seq_len - 1, 0.0, nxt))
    col = pos % grid_w
    up = jnp.concatenate([up_ref[...].astype(F32), h[:tm - grid_w]], axis=0)
    dn = jnp.concatenate([h[grid_w:], dn_ref[...].astype(F32)], axis=0)
    up = jnp.where(t < grid_w, 0.0, up)
    dn = jnp.where(t >= seq_len - grid_w, 0.0, dn)
    left = jnp.where(col == 0, 0.0, prev)
    right = jnp.where(col == grid_w - 1, 0.0, nxt)
    return 0.25 * (up + dn + left + right)


def _rwkv_front_kernel(*refs, seq_len, grid_w):
    n_h = 1 if grid_w is None else 3
    h_ref, up_ref, dn_ref = (tuple(refs[:n_h]) + (None, None))[:3]
    (mu_ref, wrkvg_ref, w1_ref, w2_ref, w0_ref, a1_ref, a2_ref, a0_ref,
     r_ref, k_ref, v_ref, g_ref, lw_ref, a_ref) = refs[n_h:]
    h = h_ref[...].astype(F32)
    dh = _token_shift(h, up_ref, dn_ref, seq_len, grid_w) - h
    mix = lambda p: (h + dh * mu_ref[p:p + 1, :]).astype(BF16)
    r_ref[...] = jnp.dot(mix(0), wrkvg_ref[0], preferred_element_type=F32).astype(BF16)
    k_ref[...] = jnp.dot(mix(1), wrkvg_ref[1], preferred_element_type=F32).astype(BF16)
    v_ref[...] = jnp.dot(mix(2), wrkvg_ref[2], preferred_element_type=F32).astype(BF16)
    g_ref[...] = jnp.dot(mix(3), wrkvg_ref[3], preferred_element_type=F32).astype(BF16)
    tw = jnp.tanh(jnp.dot(mix(4), w1_ref[...], preferred_element_type=F32))
    lw_ref[...] = -RWKV_DECAY_SCALE * _sigmoid(w0_ref[...] + _dot(tw, w2_ref[...]))
    ta = jnp.dot(mix(5), a1_ref[...], preferred_element_type=F32)
    a_ref[...] = _sigmoid(a0_ref[...] + _dot(ta, a2_ref[...])).astype(BF16)


def _rwkv_front(h, tm, seq_len, grid_w, mu, w_rkvg, w0, w1, w2, a0, a1, a2):
    n, d = h.shape
    rk = RWKV_RANK
    cat1 = lambda w: jnp.concatenate([w[0], w[1]], axis=1)
    bd2 = lambda w: jnp.zeros((2 * rk, 2 * d), F32).at[:rk, :d].set(w[0]).at[rk:, d:].set(w[1])
    row = lambda i: (i, 0)
    const = lambda i: (0, 0)
    if grid_w is None:
        assert tm % seq_len == 0
        h_specs, h_args = [pl.BlockSpec((tm, d), row)], [h]
    else:
        assert tm % grid_w == 0 and seq_len % tm == 0
        rows_per_tile, last = tm // grid_w, n // grid_w - 1
        h_specs = [pl.BlockSpec((tm, d), row),
                   pl.BlockSpec((grid_w, d), lambda i: (jnp.maximum(i * rows_per_tile - 1, 0), 0)),
                   pl.BlockSpec((grid_w, d), lambda i: (jnp.minimum((i + 1) * rows_per_tile, last), 0))]
        h_args = [h, h, h]
    return pl.pallas_call(
        functools.partial(_rwkv_front_kernel, seq_len=seq_len, grid_w=grid_w),
        grid=(n // tm,),
        in_specs=h_specs + [
            pl.BlockSpec((8, d), const),
            pl.BlockSpec((4, d, d), lambda i: (0, 0, 0)),
            pl.BlockSpec((d, 2 * rk), const), pl.BlockSpec((2 * rk, 2 * d), const), pl.BlockSpec((1, 2 * d), const),
            pl.BlockSpec((d, 2 * rk), const), pl.BlockSpec((2 * rk, 2 * d), const), pl.BlockSpec((1, 2 * d), const)],
        out_specs=[pl.BlockSpec((tm, d), row)] * 4 + [pl.BlockSpec((tm, 2 * d), row)] * 2,
        out_shape=[jax.ShapeDtypeStruct((n, d), BF16)] * 4
        + [jax.ShapeDtypeStruct((n, 2 * d), F32), jax.ShapeDtypeStruct((n, 2 * d), BF16)],
        compiler_params=_params(1),
        name="rwkv_front",
    )(*h_args, jnp.zeros((8, d), F32).at[:6].set(mu), w_rkvg.astype(BF16),
      cat1(w1).astype(BF16), bd2(w2).astype(BF16), w0.reshape(1, 2 * d),
      cat1(a1).astype(BF16), bd2(a2).astype(BF16), a0.reshape(1, 2 * d))


def _seg_sum(x, head0):
    s0 = jnp.sum(jnp.where(head0, x, 0.0), axis=-1, keepdims=True)
    s1 = jnp.sum(jnp.where(head0, 0.0, x), axis=-1, keepdims=True)
    return jnp.where(head0, s0, s1)


def _stack(x, head0):
    return jnp.concatenate([jnp.where(head0, x, 0.0), jnp.where(head0, 0.0, x)], axis=0)


def _rwkv_chunk_terms(insts):
    n = len(insts)
    c = insts[0][0].shape[0]
    lane = lax.broadcasted_iota(jnp.int32, (c, LANES), 1)
    head0 = lane < RWKV_HEAD
    row = lax.broadcasted_iota(jnp.int32, (c, c), 0)
    col = lax.broadcasted_iota(jnp.int32, (c, c), 1)
    ti = lax.broadcasted_iota(jnp.int32, (c, LANES), 0)
    si = lane & (c - 1)
    prow = lax.broadcasted_iota(jnp.int32, (LANES, LANES), 0)
    pcol = lax.broadcasted_iota(jnp.int32, (LANES, LANES), 1)
    same_head = (prow >= RWKV_HEAD) == (pcol >= RWKV_HEAD)
    each = lambda f, *ls: [f(*xs) for xs in zip(*ls)]
    rev = [x[8] for x in insts]
    stk = lambda x: _stack(x, head0).astype(BF16)

    cum = [_tri_dot(((col >= row) if x[8] else (col <= row)).astype(F32), x[3]) for x in insts]
    tot = [cm[0:1] if rv else cm[c - 1:c] for cm, rv in zip(cum, rev)]

    a_t, r_t, b_ts, k_ts, b_h, k_h, v_s, vv, bonus = [], [], [], [], [], [], [], [], []
    for (r, k, v, lw, a, k_k, k_a, r_k, _), cm, tt in zip(insts, cum, tot):
        kk = k * k_k
        kk = kk * lax.rsqrt(jnp.maximum(_seg_sum(kk * kk, head0), 1e-24))
        kd = k * (1.0 + (a - 1.0) * k_a)
        bv = kk * a
        bonus.append(_seg_sum(r * kd * r_k, head0) * v)
        e_neg = jnp.exp(-cm)
        e_tot = jnp.exp(tt)
        b_neg, k_neg = bv * e_neg, kd * e_neg
        a_t.append(-kk * jnp.exp(cm - lw))
        r_t.append(r * jnp.exp(cm))
        b_ts.append(stk(b_neg))
        k_ts.append(stk(k_neg))
        b_h.append(b_neg * e_tot)
        k_h.append(k_neg * e_tot)
        v_s.append(stk(v))
        vv.append(v)

    strict = [(si > ti) if rv else (si < ti) for rv in rev]
    incl = [(si >= ti) if rv else (si <= ti) for rv in rev]
    ar = each(lambda x, y: jnp.concatenate([x, y], axis=0).astype(BF16), a_t, r_t)
    g_bk = each(lambda x, y, z: _dot_nt(x, jnp.concatenate([y, z], axis=0)), ar, b_ts, k_ts)
    g_b = [x[:, :LANES] for x in g_bk]
    g_k = [x[:, LANES:] for x in g_bk]
    n_ab = each(lambda m_, x: jnp.where(m_, x[:c], 0.0), strict, g_b)
    l_ak = each(lambda m_, x: jnp.where(m_, x[:c], 0.0), strict, g_k)
    p_rb = each(lambda m_, x: jnp.where(m_, x[c:], 0.0), incl, g_b)
    p_rk = each(lambda m_, x: jnp.where(m_, x[c:], 0.0), incl, g_k)

    same_blk = lambda h: (ti // h) == (si // h)
    n_d = each(lambda x: jnp.where(same_blk(INV_BASE), x, 0.0), n_ab)
    steps = max(1, (INV_BASE - 1).bit_length())
    q = n_d
    tinv = n_d
    q = each(lambda x: _dot(x, stk(x)), q) if steps > 1 else q
    for j in range(1, steps):
        last = j == steps - 1
        qs = each(stk, q)
        if last:
            z = each(_dot, tinv, qs)
            tinv = each(lambda t_, x, y: t_ + x + y, tinv, q, z)
        else:
            both = each(lambda x, t_, w_: _dot(jnp.concatenate([x, t_], axis=0), w_), q, tinv, qs)
            tinv = each(lambda t_, x, y: t_ + x + y[c:], tinv, q, both)
            q = each(lambda y: y[:c], both)
    pair = lambda f, x, y: _dot(f, jnp.concatenate([stk(x), stk(y)], axis=1))
    link = lambda h: same_blk(2 * h) & jnp.logical_not(same_blk(h))
    n_off = lambda h: [jnp.where(link(h), x, 0.0) for x in n_ab]
    grow = lambda t_, y, z: t_ + y + z
    h = INV_BASE
    while h < c:
        if 4 * h <= c:
            n_1, n_2 = n_off(h), n_off(2 * h)
            p_1 = each(pair, tinv, n_1, n_2)
            y_1 = each(lambda x, z: x + z[:, :LANES], n_1, p_1)
            b_2 = each(lambda x, z: x + z[:, LANES:], n_2, p_1)
            p_2 = each(pair, y_1, tinv, b_2)
            tinv = each(lambda t_, y, z: grow(t_, y, z[:, :LANES]), tinv, y_1, p_2)
            y_2 = each(lambda x, z: x + z[:, LANES:], b_2, p_2)
            tinv = each(lambda t_, y: grow(t_, y, _dot(y, stk(t_))), tinv, y_2)
            h *= 4
        else:
            y = each(lambda t_, x: x + _dot(t_, stk(x)), tinv, n_off(h))
            tinv = each(lambda t_, y_: grow(t_, y_, _dot(y_, stk(t_))), tinv, y)
            h *= 2
    lv = each(lambda l_, p, y: _dot(jnp.concatenate([l_, p], axis=0), y), l_ak, p_rk, v_s)
    w = [x[:c] for x in lv]
    au = each(pair, tinv, a_t, w)
    a_bar = each(lambda x, z: x + z[:, :LANES], a_t, au)
    u0 = each(lambda x, z: x + z[:, LANES:], w, au)
    ry = each(pair, p_rb, a_bar, u0)
    r_bar = each(lambda x, z: x + z[:, :LANES], r_t, ry)
    y0 = each(lambda z, l_: z[:, LANES:] + l_[c:], ry, lv)
    zero = jnp.zeros((c, LANES), F32)
    ms = each(lambda ab, u, v_, bh, kh_: _dot_tn(
        jnp.concatenate([jnp.concatenate([ab, u], axis=1), jnp.concatenate([zero, v_], axis=1)], axis=0),
        jnp.concatenate([bh, kh_], axis=0)), a_bar, u0, vv, b_h, k_h)
    m = [jnp.where(same_head, x[:LANES], 0.0) for x in ms]
    s0p = [jnp.where(same_head, x[LANES:], 0.0) for x in ms]
    return [(r_bar[i], y0[i], m[i], s0p[i], jnp.exp(tot[i]), bonus[i]) for i in range(n)]


def _rwkv_scan_kernel(*refs, has_s0, emit_state):
    (r_ref, k_ref, v_ref, lwf_ref, lwb_ref, af_ref, ab_ref,
     kkf_ref, kkb_ref, kaf_ref, kab_ref, rkf_ref, rkb_ref, lnw_ref, lnb_ref) = refs[:15]
    s0_ref = refs[15] if has_s0 else None
    o_ref = refs[15 + has_s0]
    s_ref = refs[16 + has_s0] if emit_state else None
    w_ref, y0_ref, s0p_ref, dec_ref, bonus_ref, yf_ref, yb_ref, st_ref = refs[16 + has_s0 + emit_state:]
    bb, t = r_ref.shape[0], r_ref.shape[1]
    nc = t // CHUNK
    dirs = ((lwf_ref, af_ref, kkf_ref, kaf_ref, rkf_ref, False), (lwb_ref, ab_ref, kkb_ref, kab_ref, rkb_ref, True))
    group = min(bb * nc, PREP_GROUP)
    assert (bb * nc) % group == 0
    chunk_rows = lambda ci: pl.ds(pl.multiple_of(ci * CHUNK, CHUNK), CHUNK)

    def prep(gi, carry):
        insts, where = [], []
        at = [((gi * group + j) // nc, chunk_rows((gi * group + j) % nc)) for j in range(group)]
        for j, (e, sl) in enumerate(at):
            r, k, v = (ref[e, sl, :].astype(F32) for ref in (r_ref, k_ref, v_ref))
            for d, (lw_ref, a_ref, kk_ref, ka_ref, rk_ref, reverse) in enumerate(dirs):
                insts.append((r, k, v, lw_ref[e, sl, :], a_ref[e, sl, :].astype(F32), kk_ref[...], ka_ref[...],
                              rk_ref[...], reverse))
                where.append((d, gi * group + j, e, sl))
        terms = _rwkv_chunk_terms(insts)
        for (d, fi, e, sl), (r_bar, y0, m, s0p, dec, bon) in zip(where, terms):
            rbar_t = jnp.concatenate([r_bar, jnp.zeros_like(r_bar)], axis=0).T
            w_ref[d, fi] = jnp.concatenate([m, rbar_t], axis=1).astype(BF16)
            y0_ref[d, fi] = y0
            s0p_ref[d, fi] = s0p
            dec_ref[d, fi] = jnp.broadcast_to(dec, (8, LANES))
        for j in range(group):
            _, _, e, sl = where[2 * j]
            bonus_ref[e, sl, :] = terms[2 * j][5] + terms[2 * j + 1][5]
        return carry

    lax.fori_loop(0, bb * nc // group, prep, 0)

    seq_steps = SEQ_STEPS if nc % SEQ_STEPS == 0 else 1
    idx = [(d, e) for d in range(2) for e in range(bb)]

    def seq(i, _):
        carry = tuple(st_ref[n] for n in range(len(idx)))
        for s in range(seq_steps):
            cis = (i * seq_steps + s, nc - 1 - i * seq_steps - s)
            sw = [jnp.dot(carry[n].astype(BF16), w_ref[d, e * nc + cis[d]], preferred_element_type=F32)
                  for n, (d, e) in enumerate(idx)]
            for n, (d, e) in enumerate(idx):
                y = sw[n][:, LANES:].T[:CHUNK] + y0_ref[d, e * nc + cis[d]]
                (yf_ref, yb_ref)[d][e, chunk_rows(cis[d]), :] = y
            carry = tuple(carry[n] * dec_ref[d, e * nc + cis[d]][0:1] + sw[n][:, :LANES] + s0p_ref[d, e * nc + cis[d]]
                          for n, (d, e) in enumerate(idx))
        for n, st in enumerate(carry):
            st_ref[n] = st
        return 0

    for n, (d, e) in enumerate(idx):
        st_ref[n] = s0_ref[e, d, 0] if has_s0 else jnp.zeros((LANES, LANES), F32)
    lax.fori_loop(0, nc // seq_steps, seq, 0)
    if emit_state:
        for d in range(2):
            for e in range(bb):
                s = st_ref[d * bb + e]
                s_ref[e, d, 0] = s[:RWKV_HEAD, :RWKV_HEAD]
                s_ref[e, d, 1] = pltpu.roll(s, RWKV_HEAD, 1)[RWKV_HEAD:, :RWKV_HEAD]

    prow = lax.broadcasted_iota(jnp.int32, (LANES, LANES), 0)
    pcol = lax.broadcasted_iota(jnp.int32, (LANES, LANES), 1)
    head_ones = ((prow >= RWKV_HEAD) == (pcol >= RWKV_HEAD)).astype(BF16)
    inv_n = 1.0 / RWKV_HEAD

    def head_mean(x):
        hi = x.astype(BF16)
        lo = (x - hi.astype(F32)).astype(BF16)
        both = jnp.dot(jnp.concatenate([hi, lo], axis=0), head_ones, preferred_element_type=F32)
        return (both[:x.shape[0]] + both[x.shape[0]:]) * inv_n

    fgroup = min(bb * nc, FINISH_GROUP)
    assert (bb * nc) % fgroup == 0

    def finish(gi, carry):
        at = [((gi * fgroup + j) // nc, chunk_rows((gi * fgroup + j) % nc)) for j in range(fgroup)]
        y = [yf_ref[e, sl, :] + yb_ref[e, sl, :] for e, sl in at]
        yc = [x - head_mean(x) for x in y]
        var = [head_mean(x * x) for x in yc]
        for (e, sl), x, vr in zip(at, yc, var):
            o_ref[e, sl, :] = (x * lax.rsqrt(vr + LNX_EPS) * lnw_ref[...] + lnb_ref[...]
                               + bonus_ref[e, sl, :]).astype(BF16)
        return carry

    lax.fori_loop(0, bb * nc // fgroup, finish, 0)


def _rwkv_scan(r, k, v, lw, a, k_k, k_a, r_k, lnx_w, lnx_b, s0, bb, emit_state):
    bsz, t, d = r.shape
    nc = t // CHUNK
    seq = lambda b, p: (b, 0, p)
    seq_b = lambda b, p: (b, 0, RWKV_PAIRS + p)
    par = lambda b, p: (0, p)
    par_b = lambda b, p: (0, RWKV_PAIRS + p)
    sblk = pl.BlockSpec((bb, t, LANES), seq)
    sblk_b = pl.BlockSpec((bb, t, LANES), seq_b)
    pblk = pl.BlockSpec((1, LANES), par)
    pblk_b = pl.BlockSpec((1, LANES), par_b)
    s0blk = pl.BlockSpec((bb, 2, 1, LANES, LANES), lambda b, p: (b, 0, p, 0, 0))
    s_blk = pl.BlockSpec((bb, 2, 2, RWKV_HEAD, RWKV_HEAD), lambda b, p: (b, 0, p, 0, 0))
    s_shape = jax.ShapeDtypeStruct((bsz, 2, RWKV_HEADS, RWKV_HEAD, RWKV_HEAD), F32)
    has_s0 = s0 is not None
    k_k, k_a, r_k = (p.reshape(1, 2 * d) for p in (k_k, k_a, r_k))
    return pl.pallas_call(
        functools.partial(_rwkv_scan_kernel, has_s0=has_s0, emit_state=emit_state),
        grid=(bsz // bb, RWKV_PAIRS),
        in_specs=[sblk, sblk, sblk, sblk, sblk_b, sblk, sblk_b,
                  pblk, pblk_b, pblk, pblk_b, pblk, pblk_b, pblk, pblk] + [s0blk] * has_s0,
        out_specs=[sblk] + [s_blk] * emit_state,
        out_shape=[jax.ShapeDtypeStruct((bsz, t, d), BF16)] + [s_shape] * emit_state,
        scratch_shapes=[pltpu.VMEM((2, bb * nc, LANES, 2 * LANES), BF16), pltpu.VMEM((2, bb * nc, CHUNK, LANES), F32),
                        pltpu.VMEM((2, bb * nc, LANES, LANES), F32),
                        pltpu.VMEM((2, bb * nc, 8, LANES), F32), pltpu.VMEM((bb, t, LANES), F32),
                        pltpu.VMEM((bb, t, LANES), F32), pltpu.VMEM((bb, t, LANES), F32),
                        pltpu.VMEM((2 * bb, LANES, LANES), F32)],
        compiler_params=_params(2),
        name="rwkv_scan",
    )(r, k, v, lw, lw, a, a, k_k, k_k, k_a, k_a, r_k, r_k,
      lnx_w.reshape(1, d), lnx_b.reshape(1, d), *([s0] * has_s0))


def _rwkv_back_kernel(o_ref, g_ref, x_ref, mod_ref, wout_ref, nf_ref, y_ref):
    out = _dot(o_ref[...].astype(F32) * _silu(g_ref[...].astype(F32)), wout_ref[...])
    x2 = x_ref[...] + mod_ref[0][:, 2 * D_MODEL:] * out
    y_ref[...] = _rms(x2) * nf_ref[...]


def _rwkv_back(o, g, x, mod1, mod_idx, tm, w_out, norm_f):
    n, d = x.shape
    nmod = mod1.shape[0]
    row = lambda i: (i, 0)
    const = lambda i: (0, 0)
    return pl.pallas_call(
        _rwkv_back_kernel,
        grid=(n // tm,),
        in_specs=[pl.BlockSpec((tm, d), row), pl.BlockSpec((tm, d), row), pl.BlockSpec((tm, d), row),
                  pl.BlockSpec((1, 1, 3 * d), lambda i: (mod_idx(i), 0, 0)),
                  pl.BlockSpec((d, d), const), pl.BlockSpec((1, d), const)],
        out_specs=pl.BlockSpec((tm, d), row),
        out_shape=jax.ShapeDtypeStruct((n, d), F32),
        compiler_params=_params(1),
        name="rwkv_back",
    )(o, g, x, mod1.reshape(nmod, 1, 3 * d), w_out.astype(BF16), norm_f.reshape(1, d))


def _pair_blockdiag(s):
    b = s.shape[0]
    s = s.reshape(b, 2, RWKV_PAIRS, 2, RWKV_HEAD, RWKV_HEAD)
    z = jnp.zeros_like(s[:, :, :, 0])
    top = jnp.concatenate([s[:, :, :, 0], z], axis=-1)
    bot = jnp.concatenate([z, s[:, :, :, 1]], axis=-1)
    return jnp.concatenate([top, bot], axis=-2)


def _tile_rows(n_seq, t, may_span, rows):
    if t % rows == 0:
        return rows
    if may_span and rows % t == 0 and (n_seq * t) % rows == 0:
        return rows
    return t


def kernel(x_prompt, x_sample, state_gla, state_rwkv, c, c_ctx, w_mod, b_mod, norm_w, gla_w_in, gla_w_a1,
           gla_w_a2, gla_b_a, gla_norm, gla_w_out, rwkv_mu, rwkv_w_rkvg, rwkv_w0, rwkv_w1, rwkv_w2, rwkv_a0,
           rwkv_a1, rwkv_a2, rwkv_k_k, rwkv_k_a, rwkv_r_k, rwkv_lnx_w, rwkv_lnx_b, rwkv_w_out, norm_f):
    d = D_MODEL
    bp, tp, _ = x_prompt.shape
    bs, ts, _ = x_sample.shape
    assert tp % CHUNK == 0 and ts % CHUNK == 0 and ts % GRID_W == 0

    nrows = -(-(1 + bs) // 8) * 8
    cond = jnp.zeros((nrows, d), F32).at[0].set(c_ctx).at[1:1 + bs].set(c)
    cond = cond * (1.0 / (1.0 + jnp.exp(-cond)))
    mod0 = _matmul_bias(cond, w_mod[0], b_mod[0].reshape(1, 3 * d))
    mod1 = _matmul_bias(cond, w_mod[1], b_mod[1].reshape(1, 3 * d))

    def trunk(x3, shared_mod, grid_w, gla_s0, rwkv_s0, gla_bb, rwkv_bb, emit_state):
        b, t, _ = x3.shape
        n = b * t
        tm, tw = (_tile_rows(b, t, shared_mod, rows) for rows in (TOKEN_TILE, WIDE_TILE))
        mod_row = lambda rows: (lambda i: 0) if shared_mod else (lambda i: 1 + i // (t // rows))
        x = x3.reshape(n, d)
        seq = lambda arr: arr.reshape(b, t, arr.shape[-1])
        q, k, v, g, lg = _gla_front(x, mod0, mod_row(tw), tw, norm_w[0], gla_w_in[0], gla_w_a1[0], gla_w_a2[0], gla_b_a[0])
        o, *gla_s = _gla_scan(seq(q), seq(k), seq(v), seq(lg), gla_s0, gla_bb, GLA_HEADS, emit_state)
        x1, h1 = _gla_back(o.reshape(n, d), g, x, mod0, mod1, mod_row(tw), tw, gla_norm[0], gla_w_out[0], norm_w[1])
        r, k, v, g, lw, a = _rwkv_front(h1, tm, t, grid_w, rwkv_mu[0], rwkv_w_rkvg[0], rwkv_w0[0], rwkv_w1[0], rwkv_w2[0],
                                        rwkv_a0[0], rwkv_a1[0], rwkv_a2[0])
        o, *rwkv_s = _rwkv_scan(seq(r), seq(k), seq(v), seq(lw), seq(a), rwkv_k_k[0], rwkv_k_a[0], rwkv_r_k[0],
                                rwkv_lnx_w[0], rwkv_lnx_b[0], rwkv_s0, rwkv_bb, emit_state)
        y = _rwkv_back(o.reshape(n, d), g, x1, mod1, mod_row(tw), tw, rwkv_w_out[0], norm_f)
        return y.reshape(b, t, d), gla_s, rwkv_s

    y_prompt, (gla_s,), (rwkv_s,) = trunk(x_prompt, True, None, None, None, min(bp, 2), min(bp, 8), True)

    y_sample, _, _ = trunk(x_sample, False, GRID_W, state_gla[:, 0], _pair_blockdiag(state_rwkv[:, 0]),
                           1, min(bs, 4), False)
    return (y_prompt, y_sample, gla_s[:, None], rwkv_s[:, None])
```

```python
import functools

import jax
import jax.numpy as jnp
from jax import lax
from jax.experimental import pallas as pl
from jax.experimental.pallas import tpu as pltpu

F32 = jnp.float32
BF16 = jnp.bfloat16

D_MODEL = 1024
EPS = 1e-6
GRID_W = 64
GLA_HEADS = 4
GLA_DK = 512
GLA_DV = 1024
GLA_DKH = 128
GLA_DVH = 256
GLA_GATE_RANK = 16
GLA_GATE_NORM = 16.0
RWKV_HEAD = 64
RWKV_HEADS = 16
RWKV_PAIRS = 8
RWKV_RANK = 64
LNX_EPS = 64e-5
RWKV_DECAY_SCALE = 0.6065306597126334
CHUNK = 64
SEQ_STEPS = 16
GLA_STEPS = 8
INV_BASE = 8
TOKEN_TILE = 512
WIDE_TILE = 1024
FINISH_GROUP = 32
PREP_GROUP = 8
LANES = 128
VMEM_LIMIT = 56 * 1024 * 1024


def _dot(a, b):
    return jnp.dot(a.astype(BF16), b.astype(BF16), preferred_element_type=F32)


def _dot_nt(a, b):
    return lax.dot_general(a.astype(BF16), b.astype(BF16), (((1,), (1,)), ((), ())),
                           preferred_element_type=F32)


def _dot_tn(a, b):
    return lax.dot_general(a.astype(BF16), b.astype(BF16), (((0,), (0,)), ((), ())),
                           preferred_element_type=F32)


def _tri_dot(tri, x):
    hi = x.astype(BF16)
    lo = (x - hi.astype(F32)).astype(BF16)
    t = tri.astype(BF16)
    return jnp.dot(t, hi, preferred_element_type=F32) + jnp.dot(t, lo, preferred_element_type=F32)


def _log_sigmoid(z):
    return jnp.minimum(z, 0.0) - jnp.log(1.0 + jnp.exp(-jnp.abs(z)))


def _sigmoid(z):
    return 0.5 * jnp.tanh(0.5 * z) + 0.5


def _silu(z):
    return z * _sigmoid(z)


def _rms(x):
    return x * lax.rsqrt(jnp.mean(x * x, axis=-1, keepdims=True) + EPS)


def _params(n_grid_dims):
    return pltpu.CompilerParams(dimension_semantics=("arbitrary",) * n_grid_dims,
                                vmem_limit_bytes=VMEM_LIMIT)


def _mm_kernel(x_ref, w_ref, b_ref, o_ref):
    o_ref[...] = _dot(x_ref[...], w_ref[...]) + b_ref[...]


def _matmul_bias(x, w, b):
    m, k = x.shape
    n = w.shape[1]
    return pl.pallas_call(
        _mm_kernel,
        grid=(1,),
        in_specs=[pl.BlockSpec((m, k), lambda i: (0, 0)),
                  pl.BlockSpec((k, n), lambda i: (0, 0)),
                  pl.BlockSpec((1, n), lambda i: (0, 0))],
        out_specs=pl.BlockSpec((m, n), lambda i: (0, 0)),
        out_shape=jax.ShapeDtypeStruct((m, n), F32),
        compiler_params=_params(1),
        name="mod_matmul",
    )(x, w.astype(BF16), b)


def _gla_front_kernel(x_ref, mod_ref, nw_ref, win_ref, wa1_ref, wa2_ref, ba_ref,
                      q_ref, k_ref, v_ref, g_ref, lg_ref):
    d = D_MODEL
    mod = mod_ref[0]
    shift, scale = mod[:, :d], mod[:, d:2 * d]
    h = _rms(x_ref[...]) * nw_ref[...] * (1.0 + scale) + shift
    hb = h.astype(BF16)
    proj = lambda lo, hi: jnp.dot(hb, win_ref[:, lo:hi], preferred_element_type=F32)
    q_ref[...] = (proj(0, GLA_DK) * (GLA_DKH ** -0.5)).astype(BF16)
    k_ref[...] = proj(GLA_DK, 2 * GLA_DK).astype(BF16)
    v_ref[...] = proj(2 * GLA_DK, 2 * GLA_DK + GLA_DV).astype(BF16)
    g_ref[...] = proj(2 * GLA_DK + GLA_DV, 2 * GLA_DK + 2 * GLA_DV).astype(BF16)
    t = jnp.dot(hb, wa1_ref[...], preferred_element_type=F32)
    z = _dot(t, wa2_ref[...]) + ba_ref[...]
    lg_ref[...] = _log_sigmoid(z) * (1.0 / GLA_GATE_NORM)


def _gla_front(x, mod, mod_idx, tm, norm_w, w_in, w_a1, w_a2, b_a):
    n, d = x.shape
    nmod = mod.shape[0]
    wa1 = jnp.zeros((d, LANES), F32).at[:, :GLA_GATE_RANK].set(w_a1[0]).at[:, GLA_GATE_RANK:2 * GLA_GATE_RANK].set(w_a1[1])
    wa2 = jnp.zeros((LANES, 2 * GLA_DK), F32).at[:GLA_GATE_RANK, :GLA_DK].set(w_a2[0])
    wa2 = wa2.at[GLA_GATE_RANK:2 * GLA_GATE_RANK, GLA_DK:].set(w_a2[1])
    ba = b_a.reshape(1, 2 * GLA_DK)
    row = lambda i: (i, 0)
    const = lambda i: (0, 0)
    n_in = w_in.shape[1]
    outs = pl.pallas_call(
        _gla_front_kernel,
        grid=(n // tm,),
        in_specs=[pl.BlockSpec((tm, d), row),
                  pl.BlockSpec((1, 1, 3 * d), lambda i: (mod_idx(i), 0, 0)),
                  pl.BlockSpec((1, d), const),
                  pl.BlockSpec((d, n_in), const),
                  pl.BlockSpec((d, LANES), const),
                  pl.BlockSpec((LANES, 2 * GLA_DK), const),
                  pl.BlockSpec((1, 2 * GLA_DK), const)],
        out_specs=[pl.BlockSpec((tm, GLA_DK), row), pl.BlockSpec((tm, GLA_DK), row),
                   pl.BlockSpec((tm, GLA_DV), row), pl.BlockSpec((tm, GLA_DV), row),
                   pl.BlockSpec((tm, 2 * GLA_DK), row)],
        out_shape=[jax.ShapeDtypeStruct((n, GLA_DK), BF16), jax.ShapeDtypeStruct((n, GLA_DK), BF16),
                   jax.ShapeDtypeStruct((n, GLA_DV), BF16), jax.ShapeDtypeStruct((n, GLA_DV), BF16),
                   jax.ShapeDtypeStruct((n, 2 * GLA_DK), F32)],
        compiler_params=_params(1),
        name="gla_front",
    )(x, mod.reshape(nmod, 1, 3 * d), norm_w.reshape(1, d), w_in.astype(BF16), wa1.astype(BF16),
      wa2.astype(BF16), ba)
    return outs


def _gla_chunks(chains):
    flat = [(x, ch[2]) for ch in chains for x in ch[0]]
    c = flat[0][0][0].shape[0]
    dk = flat[0][0][1].shape[1]
    row = lax.broadcasted_iota(jnp.int32, (c, c), 0)
    col = lax.broadcasted_iota(jnp.int32, (c, c), 1)
    keep = [(col >= row) if rv else (col <= row) for _, rv in flat]
    b = [_tri_dot(kp.astype(F32), x[3]) for kp, (x, _) in zip(keep, flat)]
    b_end = [bb[0:1] if rv else bb[c - 1:c] for bb, (_, rv) in zip(b, flat)]
    qb = [(x[0] * jnp.exp(bb)).astype(BF16) for (x, _), bb in zip(flat, b)]
    kb = [x[1] * jnp.exp(-bb) for (x, _), bb in zip(flat, b)]
    kd = [x[1] * jnp.exp(be - bb) for (x, _), bb, be in zip(flat, b, b_end)]
    scores = [jnp.where(kp, _dot_nt(x, y), 0.0).astype(BF16) for kp, x, y in zip(keep, qb, kb)]
    decay = [jnp.broadcast_to(jnp.exp(be), (dk, dk)).T for be in b_end]
    n_steps = len(chains[0][0])
    st = [ch[1] for ch in chains]
    outs = [[] for _ in chains]
    for s in range(n_steps):
        at = [ci * n_steps + s for ci in range(len(chains))]
        for ci, f in enumerate(at):
            outs[ci].append(_dot(jnp.concatenate([scores[f], qb[f]], axis=1),
                                 jnp.concatenate([flat[f][0][2], st[ci].astype(BF16)], axis=0)))
        st = [st[ci] * jnp.concatenate([decay[f]] * (st[ci].shape[1] // dk), axis=1) + _dot_tn(kd[f], flat[f][0][2])
              for ci, f in enumerate(at)]
    return list(zip(outs, st))


def _gla_scan_kernel(*refs, has_s0, emit_state):
    q_ref, k_ref, v_ref, lf_ref, lb_ref = refs[:5]
    s0_ref = refs[5] if has_s0 else None
    o_ref = refs[5 + has_s0]
    s_ref = refs[6 + has_s0] if emit_state else None
    acc_ref, st_ref = refs[-2:]
    bb, t = q_ref.shape[0], q_ref.shape[1]
    hb = q_ref.shape[2] // GLA_DKH
    nc = t // CHUNK
    half = nc // 2
    kh = lambda h: slice(h * GLA_DKH, (h + 1) * GLA_DKH)
    vh = lambda h: slice(h * GLA_DVH, (h + 1) * GLA_DVH)

    steps = GLA_STEPS
    while half % steps:
        steps //= 2

    def both(accumulate):
        def body(i, carry):
            sls = [[pl.ds(pl.multiple_of(ci * CHUNK, CHUNK), CHUNK)
                    for ci in ((i * steps + s, nc - 1 - i * steps - s)[d] for s in range(steps))] for d in range(2)]
            chains = [([(q_ref[e, sl, kh(h)].astype(F32), k_ref[e, sl, kh(h)].astype(F32), v_ref[e, sl, vh(h)],
                         (lf_ref, lb_ref)[d][e, sl, kh(h)]) for sl in sls[d]], st_ref[n], d == 1)
                      for n, (e, d, h) in enumerate(idx)]
            res = _gla_chunks(chains)
            for n, (e, d, h) in enumerate(idx):
                for sl, o in zip(sls[d], res[n][0]):
                    if accumulate:
                        o_ref[e, sl, vh(h)] = (acc_ref[e, sl, vh(h)] + o).astype(BF16)
                    else:
                        acc_ref[e, sl, vh(h)] = o
            for n, (_, st) in enumerate(res):
                st_ref[n] = st
            return carry
        return body

    idx = [(e, d, h) for e in range(bb) for d in range(2) for h in range(hb)]
    for n, (e, d, h) in enumerate(idx):
        st_ref[n] = s0_ref[e, d, h] if has_s0 else jnp.zeros((GLA_DKH, GLA_DVH), F32)
    lax.fori_loop(0, half // steps, both(False), 0)
    lax.fori_loop(half // steps, nc // steps, both(True), 0)
    if emit_state:
        for n, (e, d, h) in enumerate(idx):
            s_ref[e, d, h] = st_ref[n]


def _gla_scan(q, k, v, lg, s0, bb, hb, emit_state):
    bsz, t, _ = q.shape
    ng = GLA_HEADS // hb
    hd = lambda b, h: (b, 0, h)
    kblk = pl.BlockSpec((bb, t, hb * GLA_DKH), hd)
    vblk = pl.BlockSpec((bb, t, hb * GLA_DVH), hd)
    sblk = pl.BlockSpec((bb, 2, hb, GLA_DKH, GLA_DVH), lambda b, h: (b, 0, h, 0, 0))
    s_shape = jax.ShapeDtypeStruct((bsz, 2, GLA_HEADS, GLA_DKH, GLA_DVH), F32)
    has_s0 = s0 is not None
    return pl.pallas_call(
        functools.partial(_gla_scan_kernel, has_s0=has_s0, emit_state=emit_state),
        grid=(bsz // bb, ng),
        in_specs=[kblk, kblk, vblk, kblk, pl.BlockSpec((bb, t, hb * GLA_DKH), lambda b, h: (b, 0, ng + h))]
        + [sblk] * has_s0,
        out_specs=[vblk] + [sblk] * emit_state,
        out_shape=[jax.ShapeDtypeStruct((bsz, t, GLA_DV), BF16)] + [s_shape] * emit_state,
        scratch_shapes=[pltpu.VMEM((bb, t, hb * GLA_DVH), F32), pltpu.VMEM((2 * bb * hb, GLA_DKH, GLA_DVH), F32)],
        compiler_params=_params(2),
        name="gla_scan",
    )(q, k, v, lg, lg, *([s0] * has_s0))


def _gla_back_kernel(o_ref, g_ref, x_ref, mod0_ref, mod1_ref, gn_ref, wout_ref, nw1_ref, x1_ref, h1_ref):
    d = D_MODEL
    o = o_ref[...].astype(F32)
    gn = gn_ref[...]
    parts = [_rms(o[:, h * GLA_DVH:(h + 1) * GLA_DVH]) * gn for h in range(GLA_HEADS)]
    on = jnp.concatenate(parts, axis=-1) * _silu(g_ref[...].astype(F32))
    out = _dot(on, wout_ref[...])
    gate = mod0_ref[0][:, 2 * d:]
    x1 = x_ref[...] + gate * out
    x1_ref[...] = x1
    mod1 = mod1_ref[0]
    h1_ref[...] = (_rms(x1) * nw1_ref[...] * (1.0 + mod1[:, d:2 * d]) + mod1[:, :d]).astype(BF16)


def _gla_back(o, g, x, mod0, mod1, mod_idx, tm, gla_norm, w_out, norm_w1):
    n, d = x.shape
    nmod = mod0.shape[0]
    row = lambda i: (i, 0)
    const = lambda i: (0, 0)
    modspec = pl.BlockSpec((1, 1, 3 * d), lambda i: (mod_idx(i), 0, 0))
    return pl.pallas_call(
        _gla_back_kernel,
        grid=(n // tm,),
        in_specs=[pl.BlockSpec((tm, d), row), pl.BlockSpec((tm, d), row), pl.BlockSpec((tm, d), row),
                  modspec, modspec,
                  pl.BlockSpec((1, GLA_DVH), const), pl.BlockSpec((d, d), const), pl.BlockSpec((1, d), const)],
        out_specs=[pl.BlockSpec((tm, d), row), pl.BlockSpec((tm, d), row)],
        out_shape=[jax.ShapeDtypeStruct((n, d), F32), jax.ShapeDtypeStruct((n, d), BF16)],
        compiler_params=_params(1),
        name="gla_back",
    )(o, g, x, mod0.reshape(nmod, 1, 3 * d), mod1.reshape(nmod, 1, 3 * d), gla_norm.reshape(1, GLA_DVH),
      w_out.astype(BF16), norm_w1.reshape(1, d))


def _token_shift(h, up_ref, dn_ref, seq_len, grid_w):
    tm = h.shape[0]
    pos = lax.broadcasted_iota(jnp.int32, (tm, 1), 0) + pl.program_id(0) * tm
    prev = pltpu.roll(h, 1, 0)
    nxt = pltpu.roll(h, tm - 1, 0)
    t = pos % seq_len
    if grid_w is None:
        return 0.5 * (jnp.where(t == 0, 0.0, prev) + jnp.where(t == seq_len - 1, 0.0, nxt))
    col = pos % grid_w
    up = jnp.concatenate([up_ref[...].astype(F32), h[:tm - grid_w]], axis=0)
    dn = jnp.concatenate([h[grid_w:], dn_ref[...].astype(F32)], axis=0)
    up = jnp.where(t < grid_w, 0.0, up)
    dn = jnp.where(t >= seq_len - grid_w, 0.0, dn)
    left = jnp.where(col == 0, 0.0, prev)
    right = jnp.where(col == grid_w - 1, 0.0, nxt)
    return 0.25 * (up + dn + left + right)


def _rwkv_front_kernel(*refs, seq_len, grid_w):
    n_h = 1 if grid_w is None else 3
    h_ref, up_ref, dn_ref = (tuple(refs[:n_h]) + (None, None))[:3]
    (mu_ref, wrkvg_ref, w1_ref, w2_ref, w0_ref, a1_ref, a2_ref, a0_ref,
     r_ref, k_ref, v_ref, g_ref, lw_ref, a_ref) = refs[n_h:]
    h = h_ref[...].astype(F32)
    dh = _token_shift(h, up_ref, dn_ref, seq_len, grid_w) - h
    mix = lambda p: (h + dh * mu_ref[p:p + 1, :]).astype(BF16)
    r_ref[...] = jnp.dot(mix(0), wrkvg_ref[0], preferred_element_type=F32).astype(BF16)
    k_ref[...] = jnp.dot(mix(1), wrkvg_ref[1], preferred_element_type=F32).astype(BF16)
    v_ref[...] = jnp.dot(mix(2), wrkvg_ref[2], preferred_element_type=F32).astype(BF16)
    g_ref[...] = jnp.dot(mix(3), wrkvg_ref[3], preferred_element_type=F32).astype(BF16)
    tw = jnp.tanh(jnp.dot(mix(4), w1_ref[...], preferred_element_type=F32))
    lw_ref[...] = -RWKV_DECAY_SCALE * _sigmoid(w0_ref[...] + _dot(tw, w2_ref[...]))
    ta = jnp.dot(mix(5), a1_ref[...], preferred_element_type=F32)
    a_ref[...] = _sigmoid(a0_ref[...] + _dot(ta, a2_ref[...])).astype(BF16)


def _rwkv_front(h, tm, seq_len, grid_w, mu, w_rkvg, w0, w1, w2, a0, a1, a2):
    n, d = h.shape
    rk = RWKV_RANK
    cat1 = lambda w: jnp.concatenate([w[0], w[1]], axis=1)
    bd2 = lambda w: jnp.zeros((2 * rk, 2 * d), F32).at[:rk, :d].set(w[0]).at[rk:, d:].set(w[1])
    row = lambda i: (i, 0)
    const = lambda i: (0, 0)
    if grid_w is None:
        assert tm % seq_len == 0
        h_specs, h_args = [pl.BlockSpec((tm, d), row)], [h]
    else:
        assert tm % grid_w == 0 and seq_len % tm == 0
        rows_per_tile, last = tm // grid_w, n // grid_w - 1
        h_specs = [pl.BlockSpec((tm, d), row),
                   pl.BlockSpec((grid_w, d), lambda i: (jnp.maximum(i * rows_per_tile - 1, 0), 0)),
                   pl.BlockSpec((grid_w, d), lambda i: (jnp.minimum((i + 1) * rows_per_tile, last), 0))]
        h_args = [h, h, h]
    return pl.pallas_call(
        functools.partial(_rwkv_front_kernel, seq_len=seq_len, grid_w=grid_w),
        grid=(n // tm,),
        in_specs=h_specs + [
            pl.BlockSpec((8, d), const),
            pl.BlockSpec((4, d, d), lambda i: (0, 0, 0)),
            pl.BlockSpec((d, 2 * rk), const), pl.BlockSpec((2 * rk, 2 * d), const), pl.BlockSpec((1, 2 * d), const),
            pl.BlockSpec((d, 2 * rk), const), pl.BlockSpec((2 * rk, 2 * d), const), pl.BlockSpec((1, 2 * d), const)],
        out_specs=[pl.BlockSpec((tm, d), row)] * 4 + [pl.BlockSpec((tm, 2 * d), row)] * 2,
        out_shape=[jax.ShapeDtypeStruct((n, d), BF16)] * 4
        + [jax.ShapeDtypeStruct((n, 2 * d), F32), jax.ShapeDtypeStruct((n, 2 * d), BF16)],
        compiler_params=_params(1),
        name="rwkv_front",
    )(*h_args, jnp.zeros((8, d), F32).at[:6].set(mu), w_rkvg.astype(BF16),
      cat1(w1).astype(BF16), bd2(w2).astype(BF16), w0.reshape(1, 2 * d),
      cat1(a1).astype(BF16), bd2(a2).astype(BF16), a0.reshape(1, 2 * d))


def _seg_sum(x, head0):
    s0 = jnp.sum(jnp.where(head0, x, 0.0), axis=-1, keepdims=True)
    s1 = jnp.sum(jnp.where(head0, 0.0, x), axis=-1, keepdims=True)
    return jnp.where(head0, s0, s1)


def _stack(x, head0):
    return jnp.concatenate([jnp.where(head0, x, 0.0), jnp.where(head0, 0.0, x)], axis=0)


def _rwkv_chunk_terms(insts):
    n = len(insts)
    c = insts[0][0].shape[0]
    lane = lax.broadcasted_iota(jnp.int32, (c, LANES), 1)
    head0 = lane < RWKV_HEAD
    row = lax.broadcasted_iota(jnp.int32, (c, c), 0)
    col = lax.broadcasted_iota(jnp.int32, (c, c), 1)
    ti = lax.broadcasted_iota(jnp.int32, (c, LANES), 0)
    si = lane & (c - 1)
    prow = lax.broadcasted_iota(jnp.int32, (LANES, LANES), 0)
    pcol = lax.broadcasted_iota(jnp.int32, (LANES, LANES), 1)
    same_head = (prow >= RWKV_HEAD) == (pcol >= RWKV_HEAD)
    each = lambda f, *ls: [f(*xs) for xs in zip(*ls)]
    rev = [x[8] for x in insts]
    stk = lambda x: _stack(x, head0).astype(BF16)

    cum = [_tri_dot(((col >= row) if x[8] else (col <= row)).astype(F32), x[3]) for x in insts]
    tot = [cm[0:1] if rv else cm[c - 1:c] for cm, rv in zip(cum, rev)]

    a_t, r_t, b_ts, k_ts, b_h, k_h, v_s, vv, bonus = [], [], [], [], [], [], [], [], []
    for (r, k, v, lw, a, k_k, k_a, r_k, _), cm, tt in zip(insts, cum, tot):
        kk = k * k_k
        kk = kk * lax.rsqrt(jnp.maximum(_seg_sum(kk * kk, head0), 1e-24))
        kd = k * (1.0 + (a - 1.0) * k_a)
        bv = kk * a
        bonus.append(_seg_sum(r * kd * r_k, head0) * v)
        e_neg = jnp.exp(-cm)
        e_tot = jnp.exp(tt)
        b_neg, k_neg = bv * e_neg, kd * e_neg
        a_t.append(-kk * jnp.exp(cm - lw))
        r_t.append(r * jnp.exp(cm))
        b_ts.append(stk(b_neg))
        k_ts.append(stk(k_neg))
        b_h.append(b_neg * e_tot)
        k_h.append(k_neg * e_tot)
        v_s.append(stk(v))
        vv.append(v)

    strict = [(si > ti) if rv else (si < ti) for rv in rev]
    incl = [(si >= ti) if rv else (si <= ti) for rv in rev]
    ar = each(lambda x, y: jnp.concatenate([x, y], axis=0).astype(BF16), a_t, r_t)
    g_bk = each(lambda x, y, z: _dot_nt(x, jnp.concatenate([y, z], axis=0)), ar, b_ts, k_ts)
    g_b = [x[:, :LANES] for x in g_bk]
    g_k = [x[:, LANES:] for x in g_bk]
    n_ab = each(lambda m_, x: jnp.where(m_, x[:c], 0.0), strict, g_b)
    l_ak = each(lambda m_, x: jnp.where(m_, x[:c], 0.0), strict, g_k)
    p_rb = each(lambda m_, x: jnp.where(m_, x[c:], 0.0), incl, g_b)
    p_rk = each(lambda m_, x: jnp.where(m_, x[c:], 0.0), incl, g_k)

    same_blk = lambda h: (ti // h) == (si // h)
    n_d = each(lambda x: jnp.where(same_blk(INV_BASE), x, 0.0), n_ab)
    steps = max(1, (INV_BASE - 1).bit_length())
    q = n_d
    tinv = n_d
    q = each(lambda x: _dot(x, stk(x)), q) if steps > 1 else q
    for j in range(1, steps):
        last = j == steps - 1
        qs = each(stk, q)
        if last:
            z = each(_dot, tinv, qs)
            tinv = each(lambda t_, x, y: t_ + x + y, tinv, q, z)
        else:
            both = each(lambda x, t_, w_: _dot(jnp.concatenate([x, t_], axis=0), w_), q, tinv, qs)
            tinv = each(lambda t_, x, y: t_ + x + y[c:], tinv, q, both)
            q = each(lambda y: y[:c], both)
    pair = lambda f, x, y: _dot(f, jnp.concatenate([stk(x), stk(y)], axis=1))
    link = lambda h: same_blk(2 * h) & jnp.logical_not(same_blk(h))
    n_off = lambda h: [jnp.where(link(h), x, 0.0) for x in n_ab]
    grow = lambda t_, y, z: t_ + y + z
    h = INV_BASE
    while h < c:
        if 4 * h <= c:
            n_1, n_2 = n_off(h), n_off(2 * h)
            p_1 = each(pair, tinv, n_1, n_2)
            y_1 = each(lambda x, z: x + z[:, :LANES], n_1, p_1)
            b_2 = each(lambda x, z: x + z[:, LANES:], n_2, p_1)
            p_2 = each(pair, y_1, tinv, b_2)
            tinv = each(lambda t_, y, z: grow(t_, y, z[:, :LANES]), tinv, y_1, p_2)
            y_2 = each(lambda x, z: x + z[:, LANES:], b_2, p_2)
            tinv = each(lambda t_, y: grow(t_, y, _dot(y, stk(t_))), tinv, y_2)
            h *= 4
        else:
            y = each(lambda t_, x: x + _dot(t_, stk(x)), tinv, n_off(h))
            tinv = each(lambda t_, y_: grow(t_, y_, _dot(y_, stk(t_))), tinv, y)
            h *= 2
    lv = each(lambda l_, p, y: _dot(jnp.concatenate([l_, p], axis=0), y), l_ak, p_rk, v_s)
    w = [x[:c] for x in lv]
    au = each(pair, tinv, a_t, w)
    a_bar = each(lambda x, z: x + z[:, :LANES], a_t, au)
    u0 = each(lambda x, z: x + z[:, LANES:], w, au)
    ry = each(pair, p_rb, a_bar, u0)
    r_bar = each(lambda x, z: x + z[:, :LANES], r_t, ry)
    y0 = each(lambda z, l_: z[:, LANES:] + l_[c:], ry, lv)
    zero = jnp.zeros((c, LANES), F32)
    ms = each(lambda ab, u, v_, bh, kh_: _dot_tn(
        jnp.concatenate([jnp.concatenate([ab, u], axis=1), jnp.concatenate([zero, v_], axis=1)], axis=0),
        jnp.concatenate([bh, kh_], axis=0)), a_bar, u0, vv, b_h, k_h)
    m = [jnp.where(same_head, x[:LANES], 0.0) for x in ms]
    s0p = [jnp.where(same_head, x[LANES:], 0.0) for x in ms]
    return [(r_bar[i], y0[i], m[i], s0p[i], jnp.exp(tot[i]), bonus[i]) for i in range(n)]


def _rwkv_scan_kernel(*refs, has_s0, emit_state):
    (r_ref, k_ref, v_ref, lwf_ref, lwb_ref, af_ref, ab_ref,
     kkf_ref, kkb_ref, kaf_ref, kab_ref, rkf_ref, rkb_ref, lnw_ref, lnb_ref) = refs[:15]
    s0_ref = refs[15] if has_s0 else None
    o_ref = refs[15 + has_s0]
    s_ref = refs[16 + has_s0] if emit_state else None
    w_ref, y0_ref, s0p_ref, dec_ref, bonus_ref, yf_ref, yb_ref, st_ref = refs[16 + has_s0 + emit_state:]
    bb, t = r_ref.shape[0], r_ref.shape[1]
    nc = t // CHUNK
    dirs = ((lwf_ref, af_ref, kkf_ref, kaf_ref, rkf_ref, False), (lwb_ref, ab_ref, kkb_ref, kab_ref, rkb_ref, True))
    group = min(bb * nc, PREP_GROUP)
    assert (bb * nc) % group == 0
    chunk_rows = lambda ci: pl.ds(pl.multiple_of(ci * CHUNK, CHUNK), CHUNK)

    def prep(gi, carry):
        insts, where = [], []
        at = [((gi * group + j) // nc, chunk_rows((gi * group + j) % nc)) for j in range(group)]
        for j, (e, sl) in enumerate(at):
            r, k, v = (ref[e, sl, :].astype(F32) for ref in (r_ref, k_ref, v_ref))
            for d, (lw_ref, a_ref, kk_ref, ka_ref, rk_ref, reverse) in enumerate(dirs):
                insts.append((r, k, v, lw_ref[e, sl, :], a_ref[e, sl, :].astype(F32), kk_ref[...], ka_ref[...],
                              rk_ref[...], reverse))
                where.append((d, gi * group + j, e, sl))
        terms = _rwkv_chunk_terms(insts)
        for (d, fi, e, sl), (r_bar, y0, m, s0p, dec, bon) in zip(where, terms):
            rbar_t = jnp.concatenate([r_bar, jnp.zeros_like(r_bar)], axis=0).T
            w_ref[d, fi] = jnp.concatenate([m, rbar_t], axis=1).astype(BF16)
            y0_ref[d, fi] = y0
            s0p_ref[d, fi] = s0p
            dec_ref[d, fi] = jnp.broadcast_to(dec, (8, LANES))
        for j in range(group):
            _, _, e, sl = where[2 * j]
            bonus_ref[e, sl, :] = terms[2 * j][5] + terms[2 * j + 1][5]
        return carry

    lax.fori_loop(0, bb * nc // group, prep, 0)

    seq_steps = SEQ_STEPS
    while nc % seq_steps:
        seq_steps //= 2
    idx = [(d, e) for d in range(2) for e in range(bb)]

    def seq(i, _):
        carry = tuple(st_ref[n] for n in range(len(idx)))
        for s in range(seq_steps):
            cis = (i * seq_steps + s, nc - 1 - i * seq_steps - s)
            sw = [jnp.dot(carry[n].astype(BF16), w_ref[d, e * nc + cis[d]], preferred_element_type=F32)
                  for n, (d, e) in enumerate(idx)]
            for n, (d, e) in enumerate(idx):
                y = sw[n][:, LANES:].T[:CHUNK] + y0_ref[d, e * nc + cis[d]]
                (yf_ref, yb_ref)[d][e, chunk_rows(cis[d]), :] = y
            carry = tuple(carry[n] * dec_ref[d, e * nc + cis[d]][0:1] + sw[n][:, :LANES] + s0p_ref[d, e * nc + cis[d]]
                          for n, (d, e) in enumerate(idx))
        for n, st in enumerate(carry):
            st_ref[n] = st
        return 0

    for n, (d, e) in enumerate(idx):
        st_ref[n] = s0_ref[e, d, 0] if has_s0 else jnp.zeros((LANES, LANES), F32)
    lax.fori_loop(0, nc // seq_steps, seq, 0)
    if emit_state:
        for d in range(2):
            for e in range(bb):
                s = st_ref[d * bb + e]
                s_ref[e, d, 0] = s[:RWKV_HEAD, :RWKV_HEAD]
                s_ref[e, d, 1] = pltpu.roll(s, RWKV_HEAD, 1)[RWKV_HEAD:, :RWKV_HEAD]

    prow = lax.broadcasted_iota(jnp.int32, (LANES, LANES), 0)
    pcol = lax.broadcasted_iota(jnp.int32, (LANES, LANES), 1)
    head_ones = ((prow >= RWKV_HEAD) == (pcol >= RWKV_HEAD)).astype(BF16)
    inv_n = 1.0 / RWKV_HEAD

    def head_mean(x):
        hi = x.astype(BF16)
        lo = (x - hi.astype(F32)).astype(BF16)
        both = jnp.dot(jnp.concatenate([hi, lo], axis=0), head_ones, preferred_element_type=F32)
        return (both[:x.shape[0]] + both[x.shape[0]:]) * inv_n

    fgroup = min(bb * nc, FINISH_GROUP)
    assert (bb * nc) % fgroup == 0

    def finish(gi, carry):
        at = [((gi * fgroup + j) // nc, chunk_rows((gi * fgroup + j) % nc)) for j in range(fgroup)]
        y = [yf_ref[e, sl, :] + yb_ref[e, sl, :] for e, sl in at]
        yc = [x - head_mean(x) for x in y]
        var = [head_mean(x * x) for x in yc]
        for (e, sl), x, vr in zip(at, yc, var):
            o_ref[e, sl, :] = (x * lax.rsqrt(vr + LNX_EPS) * lnw_ref[...] + lnb_ref[...]
                               + bonus_ref[e, sl, :]).astype(BF16)
        return carry

    lax.fori_loop(0, bb * nc // fgroup, finish, 0)


def _rwkv_scan(r, k, v, lw, a, k_k, k_a, r_k, lnx_w, lnx_b, s0, bb, emit_state):
    bsz, t, d = r.shape
    nc = t // CHUNK
    seq = lambda b, p: (b, 0, p)
    seq_b = lambda b, p: (b, 0, RWKV_PAIRS + p)
    par = lambda b, p: (0, p)
    par_b = lambda b, p: (0, RWKV_PAIRS + p)
    sblk = pl.BlockSpec((bb, t, LANES), seq)
    sblk_b = pl.BlockSpec((bb, t, LANES), seq_b)
    pblk = pl.BlockSpec((1, LANES), par)
    pblk_b = pl.BlockSpec((1, LANES), par_b)
    s0blk = pl.BlockSpec((bb, 2, 1, LANES, LANES), lambda b, p: (b, 0, p, 0, 0))
    s_blk = pl.BlockSpec((bb, 2, 2, RWKV_HEAD, RWKV_HEAD), lambda b, p: (b, 0, p, 0, 0))
    s_shape = jax.ShapeDtypeStruct((bsz, 2, RWKV_HEADS, RWKV_HEAD, RWKV_HEAD), F32)
    has_s0 = s0 is not None
    k_k, k_a, r_k = (p.reshape(1, 2 * d) for p in (k_k, k_a, r_k))
    return pl.pallas_call(
        functools.partial(_rwkv_scan_kernel, has_s0=has_s0, emit_state=emit_state),
        grid=(bsz // bb, RWKV_PAIRS),
        in_specs=[sblk, sblk, sblk, sblk, sblk_b, sblk, sblk_b,
                  pblk, pblk_b, pblk, pblk_b, pblk, pblk_b, pblk, pblk] + [s0blk] * has_s0,
        out_specs=[sblk] + [s_blk] * emit_state,
        out_shape=[jax.ShapeDtypeStruct((bsz, t, d), BF16)] + [s_shape] * emit_state,
        scratch_shapes=[pltpu.VMEM((2, bb * nc, LANES, 2 * LANES), BF16), pltpu.VMEM((2, bb * nc, CHUNK, LANES), F32),
                        pltpu.VMEM((2, bb * nc, LANES, LANES), F32),
                        pltpu.VMEM((2, bb * nc, 8, LANES), F32), pltpu.VMEM((bb, t, LANES), F32),
                        pltpu.VMEM((bb, t, LANES), F32), pltpu.VMEM((bb, t, LANES), F32),
                        pltpu.VMEM((2 * bb, LANES, LANES), F32)],
        compiler_params=_params(2),
        name="rwkv_scan",
    )(r, k, v, lw, lw, a, a, k_k, k_k, k_a, k_a, r_k, r_k,
      lnx_w.reshape(1, d), lnx_b.reshape(1, d), *([s0] * has_s0))


def _rwkv_back_kernel(o_ref, g_ref, x_ref, mod_ref, wout_ref, nf_ref, y_ref):
    out = _dot(o_ref[...].astype(F32) * _silu(g_ref[...].astype(F32)), wout_ref[...])
    x2 = x_ref[...] + mod_ref[0][:, 2 * D_MODEL:] * out
    y_ref[...] = _rms(x2) * nf_ref[...]


def _rwkv_back(o, g, x, mod1, mod_idx, tm, w_out, norm_f):
    n, d = x.shape
    nmod = mod1.shape[0]
    row = lambda i: (i, 0)
    const = lambda i: (0, 0)
    return pl.pallas_call(
        _rwkv_back_kernel,
        grid=(n // tm,),
        in_specs=[pl.BlockSpec((tm, d), row), pl.BlockSpec((tm, d), row), pl.BlockSpec((tm, d), row),
                  pl.BlockSpec((1, 1, 3 * d), lambda i: (mod_idx(i), 0, 0)),
                  pl.BlockSpec((d, d), const), pl.BlockSpec((1, d), const)],
        out_specs=pl.BlockSpec((tm, d), row),
        out_shape=jax.ShapeDtypeStruct((n, d), F32),
        compiler_params=_params(1),
        name="rwkv_back",
    )(o, g, x, mod1.reshape(nmod, 1, 3 * d), w_out.astype(BF16), norm_f.reshape(1, d))


def _pair_blockdiag(s):
    b = s.shape[0]
    s = s.reshape(b, 2, RWKV_PAIRS, 2, RWKV_HEAD, RWKV_HEAD)
    z = jnp.zeros_like(s[:, :, :, 0])
    top = jnp.concatenate([s[:, :, :, 0], z], axis=-1)
    bot = jnp.concatenate([z, s[:, :, :, 1]], axis=-1)
    return jnp.concatenate([top, bot], axis=-2)


def _tile_rows(n_seq, t, may_span, rows):
    if t % rows == 0:
        return rows
    if may_span and rows % t == 0 and (n_seq * t) % rows == 0:
        return rows
    return t


def kernel(x_prompt, x_sample, state_gla, state_rwkv, c, c_ctx, w_mod, b_mod, norm_w, gla_w_in, gla_w_a1,
           gla_w_a2, gla_b_a, gla_norm, gla_w_out, rwkv_mu, rwkv_w_rkvg, rwkv_w0, rwkv_w1, rwkv_w2, rwkv_a0,
           rwkv_a1, rwkv_a2, rwkv_k_k, rwkv_k_a, rwkv_r_k, rwkv_lnx_w, rwkv_lnx_b, rwkv_w_out, norm_f):
    d = D_MODEL
    bp, tp, _ = x_prompt.shape
    bs, ts, _ = x_sample.shape
    assert tp % CHUNK == 0 and ts % CHUNK == 0 and ts % GRID_W == 0

    nrows = -(-(1 + bs) // 8) * 8
    cond = jnp.zeros((nrows, d), F32).at[0].set(c_ctx).at[1:1 + bs].set(c)
    cond = cond * (1.0 / (1.0 + jnp.exp(-cond)))
    mod0 = _matmul_bias(cond, w_mod[0], b_mod[0].reshape(1, 3 * d))
    mod1 = _matmul_bias(cond, w_mod[1], b_mod[1].reshape(1, 3 * d))

    def trunk(x3, shared_mod, grid_w, gla_s0, rwkv_s0, gla_bb, rwkv_bb, emit_state):
        b, t, _ = x3.shape
        n = b * t
        tm, tw = (_tile_rows(b, t, shared_mod, rows) for rows in (TOKEN_TILE, WIDE_TILE))
        mod_row = lambda rows: (lambda i: 0) if shared_mod else (lambda i: 1 + i // (t // rows))
        x = x3.reshape(n, d)
        seq = lambda arr: arr.reshape(b, t, arr.shape[-1])
        q, k, v, g, lg = _gla_front(x, mod0, mod_row(tw), tw, norm_w[0], gla_w_in[0], gla_w_a1[0], gla_w_a2[0], gla_b_a[0])
        o, *gla_s = _gla_scan(seq(q), seq(k), seq(v), seq(lg), gla_s0, gla_bb, GLA_HEADS, emit_state)
        x1, h1 = _gla_back(o.reshape(n, d), g, x, mod0, mod1, mod_row(tw), tw, gla_norm[0], gla_w_out[0], norm_w[1])
        r, k, v, g, lw, a = _rwkv_front(h1, tm, t, grid_w, rwkv_mu[0], rwkv_w_rkvg[0], rwkv_w0[0], rwkv_w1[0], rwkv_w2[0],
                                        rwkv_a0[0], rwkv_a1[0], rwkv_a2[0])
        o, *rwkv_s = _rwkv_scan(seq(r), seq(k), seq(v), seq(lw), seq(a), rwkv_k_k[0], rwkv_k_a[0], rwkv_r_k[0],
                                rwkv_lnx_w[0], rwkv_lnx_b[0], rwkv_s0, rwkv_bb, emit_state)
        y = _rwkv_back(o.reshape(n, d), g, x1, mod1, mod_row(tw), tw, rwkv_w_out[0], norm_f)
        return y.reshape(b, t, d), gla_s, rwkv_s

    y_prompt, (gla_s,), (rwkv_s,) = trunk(x_prompt, True, None, None, None, min(bp, 2), min(bp, 8), True)

    y_sample, _, _ = trunk(x_sample, False, GRID_W, state_gla[:, 0], _pair_blockdiag(state_rwkv[:, 0]),
                           1, min(bs, 4), False)
    return (y_prompt, y_sample, gla_s[:, None], rwkv_s[:, None])
```

```python
import functools

import jax
import jax.numpy as jnp
from jax import lax
from jax.experimental import pallas as pl
from jax.experimental.pallas import tpu as pltpu

F32 = jnp.float32
BF16 = jnp.bfloat16

D_MODEL = 1024
EPS = 1e-6
GRID_W = 64
GLA_HEADS = 4
GLA_DK = 512
GLA_DV = 1024
GLA_DKH = 128
GLA_DVH = 256
GLA_GATE_RANK = 16
GLA_GATE_NORM = 16.0
RWKV_HEAD = 64
RWKV_HEADS = 16
RWKV_PAIRS = 8
RWKV_RANK = 64
LNX_EPS = 64e-5
RWKV_DECAY_SCALE = 0.6065306597126334
CHUNK = 64
SEQ_STEPS = 16
GLA_STEPS = 8
INV_BASE = 8
TOKEN_TILE = 512
WIDE_TILE = 1024
MOD_COLS = 512
FINISH_GROUP = 32
PREP_GROUP = 8
LANES = 128
VMEM_LIMIT = 56 * 1024 * 1024


def _dot(a, b):
    return jnp.dot(a.astype(BF16), b.astype(BF16), preferred_element_type=F32)


def _dot_nt(a, b):
    return lax.dot_general(a.astype(BF16), b.astype(BF16), (((1,), (1,)), ((), ())),
                           preferred_element_type=F32)


def _dot_tn(a, b):
    return lax.dot_general(a.astype(BF16), b.astype(BF16), (((0,), (0,)), ((), ())),
                           preferred_element_type=F32)


def _tri_dot(tri, x):
    hi = x.astype(BF16)
    lo = (x - hi.astype(F32)).astype(BF16)
    t = tri.astype(BF16)
    return jnp.dot(t, hi, preferred_element_type=F32) + jnp.dot(t, lo, preferred_element_type=F32)


def _log_sigmoid(z):
    return jnp.minimum(z, 0.0) - jnp.log(1.0 + jnp.exp(-jnp.abs(z)))


def _sigmoid(z):
    return 0.5 * jnp.tanh(0.5 * z) + 0.5


def _silu(z):
    return z * _sigmoid(z)


def _rms(x):
    return x * lax.rsqrt(jnp.mean(x * x, axis=-1, keepdims=True) + EPS)


def _params(n_grid_dims):
    return pltpu.CompilerParams(dimension_semantics=("arbitrary",) * n_grid_dims,
                                vmem_limit_bytes=VMEM_LIMIT)


def _mm_kernel(x_ref, w_ref, b_ref, o_ref):
    o_ref[...] = _dot(x_ref[...], w_ref[0]) + b_ref[...]


def _matmul_bias(x, w, layer, b):
    m, k = x.shape
    n = w.shape[2]
    tn = min(n, MOD_COLS)
    assert n % tn == 0
    return pl.pallas_call(
        _mm_kernel,
        grid=(n // tn,),
        in_specs=[pl.BlockSpec((m, k), lambda j: (0, 0)),
                  pl.BlockSpec((1, k, tn), lambda j: (layer, 0, j)),
                  pl.BlockSpec((1, tn), lambda j: (0, j))],
        out_specs=pl.BlockSpec((m, tn), lambda j: (0, j)),
        out_shape=jax.ShapeDtypeStruct((m, n), F32),
        compiler_params=_params(1),
        name="mod_matmul",
    )(x, w, b)


def _gla_front_kernel(x_ref, mod_ref, nw_ref, win_ref, wa1_ref, wa2_ref, ba_ref,
                      q_ref, k_ref, v_ref, g_ref, lg_ref):
    d = D_MODEL
    mod = mod_ref[0]
    shift, scale = mod[:, :d], mod[:, d:2 * d]
    h = _rms(x_ref[...]) * nw_ref[...] * (1.0 + scale) + shift
    hb = h.astype(BF16)
    proj = lambda lo, hi: jnp.dot(hb, win_ref[:, lo:hi], preferred_element_type=F32)
    q_ref[...] = (proj(0, GLA_DK) * (GLA_DKH ** -0.5)).astype(BF16)
    k_ref[...] = proj(GLA_DK, 2 * GLA_DK).astype(BF16)
    v_ref[...] = proj(2 * GLA_DK, 2 * GLA_DK + GLA_DV).astype(BF16)
    g_ref[...] = proj(2 * GLA_DK + GLA_DV, 2 * GLA_DK + 2 * GLA_DV).astype(BF16)
    t = jnp.dot(hb, wa1_ref[...], preferred_element_type=F32)
    z = _dot(t, wa2_ref[...]) + ba_ref[...]
    lg_ref[...] = _log_sigmoid(z) * (1.0 / GLA_GATE_NORM)


def _gla_front(x, mod, mod_idx, tm, norm_w, w_in, w_a1, w_a2, b_a):
    n, d = x.shape
    nmod = mod.shape[0]
    wa1 = jnp.zeros((d, LANES), F32).at[:, :GLA_GATE_RANK].set(w_a1[0]).at[:, GLA_GATE_RANK:2 * GLA_GATE_RANK].set(w_a1[1])
    wa2 = jnp.zeros((LANES, 2 * GLA_DK), F32).at[:GLA_GATE_RANK, :GLA_DK].set(w_a2[0])
    wa2 = wa2.at[GLA_GATE_RANK:2 * GLA_GATE_RANK, GLA_DK:].set(w_a2[1])
    ba = b_a.reshape(1, 2 * GLA_DK)
    row = lambda i: (i, 0)
    const = lambda i: (0, 0)
    n_in = w_in.shape[1]
    outs = pl.pallas_call(
        _gla_front_kernel,
        grid=(n // tm,),
        in_specs=[pl.BlockSpec((tm, d), row),
                  pl.BlockSpec((1, 1, 3 * d), lambda i: (mod_idx(i), 0, 0)),
                  pl.BlockSpec((1, d), const),
                  pl.BlockSpec((d, n_in), const),
                  pl.BlockSpec((d, LANES), const),
                  pl.BlockSpec((LANES, 2 * GLA_DK), const),
                  pl.BlockSpec((1, 2 * GLA_DK), const)],
        out_specs=[pl.BlockSpec((tm, GLA_DK), row), pl.BlockSpec((tm, GLA_DK), row),
                   pl.BlockSpec((tm, GLA_DV), row), pl.BlockSpec((tm, GLA_DV), row),
                   pl.BlockSpec((tm, 2 * GLA_DK), row)],
        out_shape=[jax.ShapeDtypeStruct((n, GLA_DK), BF16), jax.ShapeDtypeStruct((n, GLA_DK), BF16),
                   jax.ShapeDtypeStruct((n, GLA_DV), BF16), jax.ShapeDtypeStruct((n, GLA_DV), BF16),
                   jax.ShapeDtypeStruct((n, 2 * GLA_DK), F32)],
        compiler_params=_params(1),
        name="gla_front",
    )(x, mod.reshape(nmod, 1, 3 * d), norm_w.reshape(1, d), w_in.astype(BF16), wa1.astype(BF16),
      wa2.astype(BF16), ba)
    return outs


def _gla_chunks(chains):
    flat = [(x, ch[2]) for ch in chains for x in ch[0]]
    c = flat[0][0][0].shape[0]
    dk = flat[0][0][1].shape[1]
    row = lax.broadcasted_iota(jnp.int32, (c, c), 0)
    col = lax.broadcasted_iota(jnp.int32, (c, c), 1)
    keep = [(col >= row) if rv else (col <= row) for _, rv in flat]
    b = [_tri_dot(kp.astype(F32), x[3]) for kp, (x, _) in zip(keep, flat)]
    b_end = [bb[0:1] if rv else bb[c - 1:c] for bb, (_, rv) in zip(b, flat)]
    qb = [(x[0] * jnp.exp(bb)).astype(BF16) for (x, _), bb in zip(flat, b)]
    kb = [x[1] * jnp.exp(-bb) for (x, _), bb in zip(flat, b)]
    kd = [x[1] * jnp.exp(be - bb) for (x, _), bb, be in zip(flat, b, b_end)]
    scores = [jnp.where(kp, _dot_nt(x, y), 0.0).astype(BF16) for kp, x, y in zip(keep, qb, kb)]
    decay = [jnp.broadcast_to(jnp.exp(be), (dk, dk)).T for be in b_end]
    n_steps = len(chains[0][0])
    st = [ch[1] for ch in chains]
    outs = [[] for _ in chains]
    for s in range(n_steps):
        at = [ci * n_steps + s for ci in range(len(chains))]
        for ci, f in enumerate(at):
            outs[ci].append(_dot(jnp.concatenate([scores[f], qb[f]], axis=1),
                                 jnp.concatenate([flat[f][0][2], st[ci].astype(BF16)], axis=0)))
        st = [st[ci] * jnp.concatenate([decay[f]] * (st[ci].shape[1] // dk), axis=1) + _dot_tn(kd[f], flat[f][0][2])
              for ci, f in enumerate(at)]
    return list(zip(outs, st))


def _gla_scan_kernel(*refs, has_s0, emit_state):
    q_ref, k_ref, v_ref, lf_ref, lb_ref = refs[:5]
    s0_ref = refs[5] if has_s0 else None
    o_ref = refs[5 + has_s0]
    s_ref = refs[6 + has_s0] if emit_state else None
    acc_ref, st_ref = refs[-2:]
    bb, t = q_ref.shape[0], q_ref.shape[1]
    hb = q_ref.shape[2] // GLA_DKH
    nc = t // CHUNK
    half = nc // 2
    kh = lambda h: slice(h * GLA_DKH, (h + 1) * GLA_DKH)
    vh = lambda h: slice(h * GLA_DVH, (h + 1) * GLA_DVH)

    steps = GLA_STEPS
    while half % steps:
        steps //= 2

    def both(accumulate):
        def body(i, carry):
            sls = [[pl.ds(pl.multiple_of(ci * CHUNK, CHUNK), CHUNK)
                    for ci in ((i * steps + s, nc - 1 - i * steps - s)[d] for s in range(steps))] for d in range(2)]
            chains = [([(q_ref[e, sl, kh(h)].astype(F32), k_ref[e, sl, kh(h)].astype(F32), v_ref[e, sl, vh(h)],
                         (lf_ref, lb_ref)[d][e, sl, kh(h)]) for sl in sls[d]], st_ref[n], d == 1)
                      for n, (e, d, h) in enumerate(idx)]
            res = _gla_chunks(chains)
            for n, (e, d, h) in enumerate(idx):
                for sl, o in zip(sls[d], res[n][0]):
                    if accumulate:
                        o_ref[e, sl, vh(h)] = (acc_ref[e, sl, vh(h)] + o).astype(BF16)
                    else:
                        acc_ref[e, sl, vh(h)] = o
            for n, (_, st) in enumerate(res):
                st_ref[n] = st
            return carry
        return body

    idx = [(e, d, h) for e in range(bb) for d in range(2) for h in range(hb)]
    for n, (e, d, h) in enumerate(idx):
        st_ref[n] = s0_ref[e, d, h] if has_s0 else jnp.zeros((GLA_DKH, GLA_DVH), F32)
    lax.fori_loop(0, half // steps, both(False), 0)
    lax.fori_loop(half // steps, nc // steps, both(True), 0)
    if emit_state:
        for n, (e, d, h) in enumerate(idx):
            s_ref[e, d, h] = st_ref[n]


def _gla_scan(q, k, v, lg, s0, bb, hb, emit_state):
    bsz, t, _ = q.shape
    ng = GLA_HEADS // hb
    hd = lambda b, h: (b, 0, h)
    kblk = pl.BlockSpec((bb, t, hb * GLA_DKH), hd)
    vblk = pl.BlockSpec((bb, t, hb * GLA_DVH), hd)
    sblk = pl.BlockSpec((bb, 2, hb, GLA_DKH, GLA_DVH), lambda b, h: (b, 0, h, 0, 0))
    s_shape = jax.ShapeDtypeStruct((bsz, 2, GLA_HEADS, GLA_DKH, GLA_DVH), F32)
    has_s0 = s0 is not None
    return pl.pallas_call(
        functools.partial(_gla_scan_kernel, has_s0=has_s0, emit_state=emit_state),
        grid=(bsz // bb, ng),
        in_specs=[kblk, kblk, vblk, kblk, pl.BlockSpec((bb, t, hb * GLA_DKH), lambda b, h: (b, 0, ng + h))]
        + [sblk] * has_s0,
        out_specs=[vblk] + [sblk] * emit_state,
        out_shape=[jax.ShapeDtypeStruct((bsz, t, GLA_DV), BF16)] + [s_shape] * emit_state,
        scratch_shapes=[pltpu.VMEM((bb, t, hb * GLA_DVH), F32), pltpu.VMEM((2 * bb * hb, GLA_DKH, GLA_DVH), F32)],
        compiler_params=_params(2),
        name="gla_scan",
    )(q, k, v, lg, lg, *([s0] * has_s0))


def _gla_back_kernel(o_ref, g_ref, x_ref, mod0_ref, mod1_ref, gn_ref, wout_ref, nw1_ref, x1_ref, h1_ref):
    d = D_MODEL
    o = o_ref[...].astype(F32)
    gn = gn_ref[...]
    parts = [_rms(o[:, h * GLA_DVH:(h + 1) * GLA_DVH]) * gn for h in range(GLA_HEADS)]
    on = jnp.concatenate(parts, axis=-1) * _silu(g_ref[...].astype(F32))
    out = _dot(on, wout_ref[...])
    gate = mod0_ref[0][:, 2 * d:]
    x1 = x_ref[...] + gate * out
    x1_ref[...] = x1
    mod1 = mod1_ref[0]
    h1_ref[...] = (_rms(x1) * nw1_ref[...] * (1.0 + mod1[:, d:2 * d]) + mod1[:, :d]).astype(BF16)


def _gla_back(o, g, x, mod0, mod1, mod_idx, tm, gla_norm, w_out, norm_w1):
    n, d = x.shape
    nmod = mod0.shape[0]
    row = lambda i: (i, 0)
    const = lambda i: (0, 0)
    modspec = pl.BlockSpec((1, 1, 3 * d), lambda i: (mod_idx(i), 0, 0))
    return pl.pallas_call(
        _gla_back_kernel,
        grid=(n // tm,),
        in_specs=[pl.BlockSpec((tm, d), row), pl.BlockSpec((tm, d), row), pl.BlockSpec((tm, d), row),
                  modspec, modspec,
                  pl.BlockSpec((1, GLA_DVH), const), pl.BlockSpec((d, d), const), pl.BlockSpec((1, d), const)],
        out_specs=[pl.BlockSpec((tm, d), row), pl.BlockSpec((tm, d), row)],
        out_shape=[jax.ShapeDtypeStruct((n, d), F32), jax.ShapeDtypeStruct((n, d), BF16)],
        compiler_params=_params(1),
        name="gla_back",
    )(o, g, x, mod0.reshape(nmod, 1, 3 * d), mod1.reshape(nmod, 1, 3 * d), gla_norm.reshape(1, GLA_DVH),
      w_out.astype(BF16), norm_w1.reshape(1, d))


def _token_shift(h, up_ref, dn_ref, seq_len, grid_w):
    tm = h.shape[0]
    pos = lax.broadcasted_iota(jnp.int32, (tm, 1), 0) + pl.program_id(0) * tm
    prev = pltpu.roll(h, 1, 0)
    nxt = pltpu.roll(h, tm - 1, 0)
    t = pos % seq_len
    if grid_w is None:
        return 0.5 * (jnp.where(t == 0, 0.0, prev) + jnp.where(t == seq_len - 1, 0.0, nxt))
    col = pos % grid_w
    up = jnp.concatenate([up_ref[...].astype(F32), h[:tm - grid_w]], axis=0)
    dn = jnp.concatenate([h[grid_w:], dn_ref[...].astype(F32)], axis=0)
    up = jnp.where(t < grid_w, 0.0, up)
    dn = jnp.where(t >= seq_len - grid_w, 0.0, dn)
    left = jnp.where(col == 0, 0.0, prev)
    right = jnp.where(col == grid_w - 1, 0.0, nxt)
    return 0.25 * (up + dn + left + right)


def _rwkv_front_kernel(*refs, seq_len, grid_w):
    n_h = 1 if grid_w is None else 3
    h_ref, up_ref, dn_ref = (tuple(refs[:n_h]) + (None, None))[:3]
    (mu_ref, wrkvg_ref, w1_ref, w2_ref, w0_ref, a1_ref, a2_ref, a0_ref,
     r_ref, k_ref, v_ref, g_ref, lw_ref, a_ref) = refs[n_h:]
    h = h_ref[...].astype(F32)
    dh = _token_shift(h, up_ref, dn_ref, seq_len, grid_w) - h
    mix = lambda p: (h + dh * mu_ref[p:p + 1, :]).astype(BF16)
    r_ref[...] = jnp.dot(mix(0), wrkvg_ref[0], preferred_element_type=F32).astype(BF16)
    k_ref[...] = jnp.dot(mix(1), wrkvg_ref[1], preferred_element_type=F32).astype(BF16)
    v_ref[...] = jnp.dot(mix(2), wrkvg_ref[2], preferred_element_type=F32).astype(BF16)
    g_ref[...] = jnp.dot(mix(3), wrkvg_ref[3], preferred_element_type=F32).astype(BF16)
    tw = jnp.tanh(jnp.dot(mix(4), w1_ref[...], preferred_element_type=F32))
    lw_ref[...] = -RWKV_DECAY_SCALE * _sigmoid(w0_ref[...] + _dot(tw, w2_ref[...]))
    ta = jnp.dot(mix(5), a1_ref[...], preferred_element_type=F32)
    a_ref[...] = _sigmoid(a0_ref[...] + _dot(ta, a2_ref[...])).astype(BF16)


def _rwkv_front(h, tm, seq_len, grid_w, mu, w_rkvg, w0, w1, w2, a0, a1, a2):
    n, d = h.shape
    rk = RWKV_RANK
    cat1 = lambda w: jnp.concatenate([w[0], w[1]], axis=1)
    bd2 = lambda w: jnp.zeros((2 * rk, 2 * d), F32).at[:rk, :d].set(w[0]).at[rk:, d:].set(w[1])
    row = lambda i: (i, 0)
    const = lambda i: (0, 0)
    if grid_w is None:
        assert tm % seq_len == 0
        h_specs, h_args = [pl.BlockSpec((tm, d), row)], [h]
    else:
        assert tm % grid_w == 0 and seq_len % tm == 0
        rows_per_tile, last = tm // grid_w, n // grid_w - 1
        h_specs = [pl.BlockSpec((tm, d), row),
                   pl.BlockSpec((grid_w, d), lambda i: (jnp.maximum(i * rows_per_tile - 1, 0), 0)),
                   pl.BlockSpec((grid_w, d), lambda i: (jnp.minimum((i + 1) * rows_per_tile, last), 0))]
        h_args = [h, h, h]
    return pl.pallas_call(
        functools.partial(_rwkv_front_kernel, seq_len=seq_len, grid_w=grid_w),
        grid=(n // tm,),
        in_specs=h_specs + [
            pl.BlockSpec((8, d), const),
            pl.BlockSpec((4, d, d), lambda i: (0, 0, 0)),
            pl.BlockSpec((d, 2 * rk), const), pl.BlockSpec((2 * rk, 2 * d), const), pl.BlockSpec((1, 2 * d), const),
            pl.BlockSpec((d, 2 * rk), const), pl.BlockSpec((2 * rk, 2 * d), const), pl.BlockSpec((1, 2 * d), const)],
        out_specs=[pl.BlockSpec((tm, d), row)] * 4 + [pl.BlockSpec((tm, 2 * d), row)] * 2,
        out_shape=[jax.ShapeDtypeStruct((n, d), BF16)] * 4
        + [jax.ShapeDtypeStruct((n, 2 * d), F32), jax.ShapeDtypeStruct((n, 2 * d), BF16)],
        compiler_params=_params(1),
        name="rwkv_front",
    )(*h_args, jnp.zeros((8, d), F32).at[:6].set(mu), w_rkvg.astype(BF16),
      cat1(w1).astype(BF16), bd2(w2).astype(BF16), w0.reshape(1, 2 * d),
      cat1(a1).astype(BF16), bd2(a2).astype(BF16), a0.reshape(1, 2 * d))


def _seg_sum(x, head0):
    s0 = jnp.sum(jnp.where(head0, x, 0.0), axis=-1, keepdims=True)
    s1 = jnp.sum(jnp.where(head0, 0.0, x), axis=-1, keepdims=True)
    return jnp.where(head0, s0, s1)


def _stack(x, head0):
    return jnp.concatenate([jnp.where(head0, x, 0.0), jnp.where(head0, 0.0, x)], axis=0)


def _rwkv_chunk_terms(insts):
    n = len(insts)
    c = insts[0][0].shape[0]
    lane = lax.broadcasted_iota(jnp.int32, (c, LANES), 1)
    head0 = lane < RWKV_HEAD
    row = lax.broadcasted_iota(jnp.int32, (c, c), 0)
    col = lax.broadcasted_iota(jnp.int32, (c, c), 1)
    ti = lax.broadcasted_iota(jnp.int32, (c, LANES), 0)
    si = lane & (c - 1)
    prow = lax.broadcasted_iota(jnp.int32, (LANES, LANES), 0)
    pcol = lax.broadcasted_iota(jnp.int32, (LANES, LANES), 1)
    same_head = (prow >= RWKV_HEAD) == (pcol >= RWKV_HEAD)
    each = lambda f, *ls: [f(*xs) for xs in zip(*ls)]
    rev = [x[8] for x in insts]
    stk = lambda x: _stack(x, head0).astype(BF16)

    cum = [_tri_dot(((col >= row) if x[8] else (col <= row)).astype(F32), x[3]) for x in insts]
    tot = [cm[0:1] if rv else cm[c - 1:c] for cm, rv in zip(cum, rev)]

    a_t, r_t, b_ts, k_ts, b_h, k_h, v_s, vv, bonus = [], [], [], [], [], [], [], [], []
    for (r, k, v, lw, a, k_k, k_a, r_k, _), cm, tt in zip(insts, cum, tot):
        kk = k * k_k
        kk = kk * lax.rsqrt(jnp.maximum(_seg_sum(kk * kk, head0), 1e-24))
        kd = k * (1.0 + (a - 1.0) * k_a)
        bv = kk * a
        bonus.append(_seg_sum(r * kd * r_k, head0) * v)
        e_neg = jnp.exp(-cm)
        e_tot = jnp.exp(tt)
        b_neg, k_neg = bv * e_neg, kd * e_neg
        a_t.append(-kk * jnp.exp(cm - lw))
        r_t.append(r * jnp.exp(cm))
        b_ts.append(stk(b_neg))
        k_ts.append(stk(k_neg))
        b_h.append(b_neg * e_tot)
        k_h.append(k_neg * e_tot)
        v_s.append(stk(v))
        vv.append(v)

    strict = [(si > ti) if rv else (si < ti) for rv in rev]
    incl = [(si >= ti) if rv else (si <= ti) for rv in rev]
    ar = each(lambda x, y: jnp.concatenate([x, y], axis=0).astype(BF16), a_t, r_t)
    g_bk = each(lambda x, y, z: _dot_nt(x, jnp.concatenate([y, z], axis=0)), ar, b_ts, k_ts)
    g_b = [x[:, :LANES] for x in g_bk]
    g_k = [x[:, LANES:] for x in g_bk]
    n_ab = each(lambda m_, x: jnp.where(m_, x[:c], 0.0), strict, g_b)
    l_ak = each(lambda m_, x: jnp.where(m_, x[:c], 0.0), strict, g_k)
    p_rb = each(lambda m_, x: jnp.where(m_, x[c:], 0.0), incl, g_b)
    p_rk = each(lambda m_, x: jnp.where(m_, x[c:], 0.0), incl, g_k)

    same_blk = lambda h: (ti // h) == (si // h)
    n_d = each(lambda x: jnp.where(same_blk(INV_BASE), x, 0.0), n_ab)
    steps = max(1, (INV_BASE - 1).bit_length())
    q = n_d
    tinv = n_d
    q = each(lambda x: _dot(x, stk(x)), q) if steps > 1 else q
    for j in range(1, steps):
        last = j == steps - 1
        qs = each(stk, q)
        if last:
            z = each(_dot, tinv, qs)
            tinv = each(lambda t_, x, y: t_ + x + y, tinv, q, z)
        else:
            both = each(lambda x, t_, w_: _dot(jnp.concatenate([x, t_], axis=0), w_), q, tinv, qs)
            tinv = each(lambda t_, x, y: t_ + x + y[c:], tinv, q, both)
            q = each(lambda y: y[:c], both)
    pair = lambda f, x, y: _dot(f, jnp.concatenate([stk(x), stk(y)], axis=1))
    link = lambda h: same_blk(2 * h) & jnp.logical_not(same_blk(h))
    n_off = lambda h: [jnp.where(link(h), x, 0.0) for x in n_ab]
    grow = lambda t_, y, z: t_ + y + z
    h = INV_BASE
    while h < c:
        if 4 * h <= c:
            n_1, n_2 = n_off(h), n_off(2 * h)
            p_1 = each(pair, tinv, n_1, n_2)
            y_1 = each(lambda x, z: x + z[:, :LANES], n_1, p_1)
            b_2 = each(lambda x, z: x + z[:, LANES:], n_2, p_1)
            p_2 = each(pair, y_1, tinv, b_2)
            tinv = each(lambda t_, y, z: grow(t_, y, z[:, :LANES]), tinv, y_1, p_2)
            y_2 = each(lambda x, z: x + z[:, LANES:], b_2, p_2)
            tinv = each(lambda t_, y: grow(t_, y, _dot(y, stk(t_))), tinv, y_2)
            h *= 4
        else:
            y = each(lambda t_, x: x + _dot(t_, stk(x)), tinv, n_off(h))
            tinv = each(lambda t_, y_: grow(t_, y_, _dot(y_, stk(t_))), tinv, y)
            h *= 2
    lv = each(lambda l_, p, y: _dot(jnp.concatenate([l_, p], axis=0), y), l_ak, p_rk, v_s)
    w = [x[:c] for x in lv]
    au = each(pair, tinv, a_t, w)
    a_bar = each(lambda x, z: x + z[:, :LANES], a_t, au)
    u0 = each(lambda x, z: x + z[:, LANES:], w, au)
    ry = each(pair, p_rb, a_bar, u0)
    r_bar = each(lambda x, z: x + z[:, :LANES], r_t, ry)
    y0 = each(lambda z, l_: z[:, LANES:] + l_[c:], ry, lv)
    zero = jnp.zeros((c, LANES), F32)
    ms = each(lambda ab, u, v_, bh, kh_: _dot_tn(
        jnp.concatenate([jnp.concatenate([ab, u], axis=1), jnp.concatenate([zero, v_], axis=1)], axis=0),
        jnp.concatenate([bh, kh_], axis=0)), a_bar, u0, vv, b_h, k_h)
    m = [jnp.where(same_head, x[:LANES], 0.0) for x in ms]
    s0p = [jnp.where(same_head, x[LANES:], 0.0) for x in ms]
    return [(r_bar[i], y0[i], m[i], s0p[i], jnp.exp(tot[i]), bonus[i]) for i in range(n)]


def _rwkv_scan_kernel(*refs, has_s0, emit_state):
    (r_ref, k_ref, v_ref, lwf_ref, lwb_ref, af_ref, ab_ref,
     kkf_ref, kkb_ref, kaf_ref, kab_ref, rkf_ref, rkb_ref, lnw_ref, lnb_ref) = refs[:15]
    s0_ref = refs[15] if has_s0 else None
    o_ref = refs[15 + has_s0]
    s_ref = refs[16 + has_s0] if emit_state else None
    w_ref, y0_ref, s0p_ref, dec_ref, bonus_ref, yf_ref, yb_ref, st_ref = refs[16 + has_s0 + emit_state:]
    bb, t = r_ref.shape[0], r_ref.shape[1]
    nc = t // CHUNK
    dirs = ((lwf_ref, af_ref, kkf_ref, kaf_ref, rkf_ref, False), (lwb_ref, ab_ref, kkb_ref, kab_ref, rkb_ref, True))
    group = min(bb * nc, PREP_GROUP)
    assert (bb * nc) % group == 0
    chunk_rows = lambda ci: pl.ds(pl.multiple_of(ci * CHUNK, CHUNK), CHUNK)

    def prep(gi, carry):
        insts, where = [], []
        at = [((gi * group + j) // nc, chunk_rows((gi * group + j) % nc)) for j in range(group)]
        for j, (e, sl) in enumerate(at):
            r, k, v = (ref[e, sl, :].astype(F32) for ref in (r_ref, k_ref, v_ref))
            for d, (lw_ref, a_ref, kk_ref, ka_ref, rk_ref, reverse) in enumerate(dirs):
                insts.append((r, k, v, lw_ref[e, sl, :], a_ref[e, sl, :].astype(F32), kk_ref[...], ka_ref[...],
                              rk_ref[...], reverse))
                where.append((d, gi * group + j, e, sl))
        terms = _rwkv_chunk_terms(insts)
        for (d, fi, e, sl), (r_bar, y0, m, s0p, dec, bon) in zip(where, terms):
            rbar_t = jnp.concatenate([r_bar, jnp.zeros_like(r_bar)], axis=0).T
            w_ref[d, fi] = jnp.concatenate([m, rbar_t], axis=1).astype(BF16)
            y0_ref[d, fi] = y0
            s0p_ref[d, fi] = s0p
            dec_ref[d, fi] = jnp.broadcast_to(dec, (8, LANES))
        for j in range(group):
            _, _, e, sl = where[2 * j]
            bonus_ref[e, sl, :] = terms[2 * j][5] + terms[2 * j + 1][5]
        return carry

    lax.fori_loop(0, bb * nc // group, prep, 0)

    seq_steps = SEQ_STEPS
    while nc % seq_steps:
        seq_steps //= 2
    idx = [(d, e) for d in range(2) for e in range(bb)]

    def seq(i, _):
        carry = tuple(st_ref[n] for n in range(len(idx)))
        for s in range(seq_steps):
            cis = (i * seq_steps + s, nc - 1 - i * seq_steps - s)
            sw = [jnp.dot(carry[n].astype(BF16), w_ref[d, e * nc + cis[d]], preferred_element_type=F32)
                  for n, (d, e) in enumerate(idx)]
            for n, (d, e) in enumerate(idx):
                y = sw[n][:, LANES:].T[:CHUNK] + y0_ref[d, e * nc + cis[d]]
                (yf_ref, yb_ref)[d][e, chunk_rows(cis[d]), :] = y
            carry = tuple(carry[n] * dec_ref[d, e * nc + cis[d]][0:1] + sw[n][:, :LANES] + s0p_ref[d, e * nc + cis[d]]
                          for n, (d, e) in enumerate(idx))
        for n, st in enumerate(carry):
            st_ref[n] = st
        return 0

    for n, (d, e) in enumerate(idx):
        st_ref[n] = s0_ref[e, d, 0] if has_s0 else jnp.zeros((LANES, LANES), F32)
    lax.fori_loop(0, nc // seq_steps, seq, 0)
    if emit_state:
        for d in range(2):
            for e in range(bb):
                s = st_ref[d * bb + e]
                s_ref[e, d, 0] = s[:RWKV_HEAD, :RWKV_HEAD]
                s_ref[e, d, 1] = pltpu.roll(s, RWKV_HEAD, 1)[RWKV_HEAD:, :RWKV_HEAD]

    prow = lax.broadcasted_iota(jnp.int32, (LANES, LANES), 0)
    pcol = lax.broadcasted_iota(jnp.int32, (LANES, LANES), 1)
    head_ones = ((prow >= RWKV_HEAD) == (pcol >= RWKV_HEAD)).astype(BF16)
    inv_n = 1.0 / RWKV_HEAD

    def head_mean(x):
        hi = x.astype(BF16)
        lo = (x - hi.astype(F32)).astype(BF16)
        both = jnp.dot(jnp.concatenate([hi, lo], axis=0), head_ones, preferred_element_type=F32)
        return (both[:x.shape[0]] + both[x.shape[0]:]) * inv_n

    fgroup = min(bb * nc, FINISH_GROUP)
    assert (bb * nc) % fgroup == 0

    def finish(gi, carry):
        at = [((gi * fgroup + j) // nc, chunk_rows((gi * fgroup + j) % nc)) for j in range(fgroup)]
        y = [yf_ref[e, sl, :] + yb_ref[e, sl, :] for e, sl in at]
        yc = [x - head_mean(x) for x in y]
        var = [head_mean(x * x) for x in yc]
        for (e, sl), x, vr in zip(at, yc, var):
            o_ref[e, sl, :] = (x * lax.rsqrt(vr + LNX_EPS) * lnw_ref[...] + lnb_ref[...]
                               + bonus_ref[e, sl, :]).astype(BF16)
        return carry

    lax.fori_loop(0, bb * nc // fgroup, finish, 0)


def _rwkv_scan(r, k, v, lw, a, k_k, k_a, r_k, lnx_w, lnx_b, s0, bb, emit_state):
    bsz, t, d = r.shape
    nc = t // CHUNK
    seq = lambda b, p: (b, 0, p)
    seq_b = lambda b, p: (b, 0, RWKV_PAIRS + p)
    par = lambda b, p: (0, p)
    par_b = lambda b, p: (0, RWKV_PAIRS + p)
    sblk = pl.BlockSpec((bb, t, LANES), seq)
    sblk_b = pl.BlockSpec((bb, t, LANES), seq_b)
    pblk = pl.BlockSpec((1, LANES), par)
    pblk_b = pl.BlockSpec((1, LANES), par_b)
    s0blk = pl.BlockSpec((bb, 2, 1, LANES, LANES), lambda b, p: (b, 0, p, 0, 0))
    s_blk = pl.BlockSpec((bb, 2, 2, RWKV_HEAD, RWKV_HEAD), lambda b, p: (b, 0, p, 0, 0))
    s_shape = jax.ShapeDtypeStruct((bsz, 2, RWKV_HEADS, RWKV_HEAD, RWKV_HEAD), F32)
    has_s0 = s0 is not None
    k_k, k_a, r_k = (p.reshape(1, 2 * d) for p in (k_k, k_a, r_k))
    return pl.pallas_call(
        functools.partial(_rwkv_scan_kernel, has_s0=has_s0, emit_state=emit_state),
        grid=(bsz // bb, RWKV_PAIRS),
        in_specs=[sblk, sblk, sblk, sblk, sblk_b, sblk, sblk_b,
                  pblk, pblk_b, pblk, pblk_b, pblk, pblk_b, pblk, pblk] + [s0blk] * has_s0,
        out_specs=[sblk] + [s_blk] * emit_state,
        out_shape=[jax.ShapeDtypeStruct((bsz, t, d), BF16)] + [s_shape] * emit_state,
        scratch_shapes=[pltpu.VMEM((2, bb * nc, LANES, 2 * LANES), BF16), pltpu.VMEM((2, bb * nc, CHUNK, LANES), F32),
                        pltpu.VMEM((2, bb * nc, LANES, LANES), F32),
                        pltpu.VMEM((2, bb * nc, 8, LANES), F32), pltpu.VMEM((bb, t, LANES), F32),
                        pltpu.VMEM((bb, t, LANES), F32), pltpu.VMEM((bb, t, LANES), F32),
                        pltpu.VMEM((2 * bb, LANES, LANES), F32)],
        compiler_params=_params(2),
        name="rwkv_scan",
    )(r, k, v, lw, lw, a, a, k_k, k_k, k_a, k_a, r_k, r_k,
      lnx_w.reshape(1, d), lnx_b.reshape(1, d), *([s0] * has_s0))


def _rwkv_back_kernel(o_ref, g_ref, x_ref, mod_ref, wout_ref, nf_ref, y_ref):
    out = _dot(o_ref[...].astype(F32) * _silu(g_ref[...].astype(F32)), wout_ref[...])
    x2 = x_ref[...] + mod_ref[0][:, 2 * D_MODEL:] * out
    y_ref[...] = _rms(x2) * nf_ref[...]


def _rwkv_back(o, g, x, mod1, mod_idx, tm, w_out, norm_f):
    n, d = x.shape
    nmod = mod1.shape[0]
    row = lambda i: (i, 0)
    const = lambda i: (0, 0)
    return pl.pallas_call(
        _rwkv_back_kernel,
        grid=(n // tm,),
        in_specs=[pl.BlockSpec((tm, d), row), pl.BlockSpec((tm, d), row), pl.BlockSpec((tm, d), row),
                  pl.BlockSpec((1, 1, 3 * d), lambda i: (mod_idx(i), 0, 0)),
                  pl.BlockSpec((d, d), const), pl.BlockSpec((1, d), const)],
        out_specs=pl.BlockSpec((tm, d), row),
        out_shape=jax.ShapeDtypeStruct((n, d), F32),
        compiler_params=_params(1),
        name="rwkv_back",
    )(o, g, x, mod1.reshape(nmod, 1, 3 * d), w_out.astype(BF16), norm_f.reshape(1, d))


def _pair_blockdiag(s):
    b = s.shape[0]
    s = s.reshape(b, 2, RWKV_PAIRS, 2, RWKV_HEAD, RWKV_HEAD)
    z = jnp.zeros_like(s[:, :, :, 0])
    top = jnp.concatenate([s[:, :, :, 0], z], axis=-1)
    bot = jnp.concatenate([z, s[:, :, :, 1]], axis=-1)
    return jnp.concatenate([top, bot], axis=-2)


def _tile_rows(n_seq, t, may_span, rows):
    if t % rows == 0:
        return rows
    if may_span and rows % t == 0 and (n_seq * t) % rows == 0:
        return rows
    return t


def kernel(x_prompt, x_sample, state_gla, state_rwkv, c, c_ctx, w_mod, b_mod, norm_w, gla_w_in, gla_w_a1,
           gla_w_a2, gla_b_a, gla_norm, gla_w_out, rwkv_mu, rwkv_w_rkvg, rwkv_w0, rwkv_w1, rwkv_w2, rwkv_a0,
           rwkv_a1, rwkv_a2, rwkv_k_k, rwkv_k_a, rwkv_r_k, rwkv_lnx_w, rwkv_lnx_b, rwkv_w_out, norm_f):
    d = D_MODEL
    bp, tp, _ = x_prompt.shape
    bs, ts, _ = x_sample.shape
    assert tp % CHUNK == 0 and ts % CHUNK == 0 and ts % GRID_W == 0

    nrows = -(-(1 + bs) // 8) * 8
    cond = jnp.zeros((nrows, d), F32).at[0].set(c_ctx).at[1:1 + bs].set(c)
    cond = cond * (1.0 / (1.0 + jnp.exp(-cond)))
    mod0 = _matmul_bias(cond, w_mod, 0, b_mod[0].reshape(1, 3 * d))
    mod1 = _matmul_bias(cond, w_mod, 1, b_mod[1].reshape(1, 3 * d))

    def trunk(x3, shared_mod, grid_w, gla_s0, rwkv_s0, gla_bb, rwkv_bb, emit_state):
        b, t, _ = x3.shape
        n = b * t
        tm, tw = (_tile_rows(b, t, shared_mod, rows) for rows in (TOKEN_TILE, WIDE_TILE))
        mod_row = lambda rows: (lambda i: 0) if shared_mod else (lambda i: 1 + i // (t // rows))
        x = x3.reshape(n, d)
        seq = lambda arr: arr.reshape(b, t, arr.shape[-1])
        q, k, v, g, lg = _gla_front(x, mod0, mod_row(tw), tw, norm_w[0], gla_w_in[0], gla_w_a1[0], gla_w_a2[0], gla_b_a[0])
        o, *gla_s = _gla_scan(seq(q), seq(k), seq(v), seq(lg), gla_s0, gla_bb, GLA_HEADS, emit_state)
        x1, h1 = _gla_back(o.reshape(n, d), g, x, mod0, mod1, mod_row(tw), tw, gla_norm[0], gla_w_out[0], norm_w[1])
        r, k, v, g, lw, a = _rwkv_front(h1, tm, t, grid_w, rwkv_mu[0], rwkv_w_rkvg[0], rwkv_w0[0], rwkv_w1[0], rwkv_w2[0],
                                        rwkv_a0[0], rwkv_a1[0], rwkv_a2[0])
        o, *rwkv_s = _rwkv_scan(seq(r), seq(k), seq(v), seq(lw), seq(a), rwkv_k_k[0], rwkv_k_a[0], rwkv_r_k[0],
                                rwkv_lnx_w[0], rwkv_lnx_b[0], rwkv_s0, rwkv_bb, emit_state)
        y = _rwkv_back(o.reshape(n, d), g, x1, mod1, mod_row(tw), tw, rwkv_w_out[0], norm_f)
        return y.reshape(b, t, d), gla_s, rwkv_s

    y_prompt, (gla_s,), (rwkv_s,) = trunk(x_prompt, True, None, None, None, min(bp, 2), min(bp, 8), True)

    y_sample, _, _ = trunk(x_sample, False, GRID_W, state_gla[:, 0], _pair_blockdiag(state_rwkv[:, 0]),
                           1, min(bs, 4), False)
    return (y_prompt, y_sample, gla_s[:, None], rwkv_s[:, None])
```

```python
import functools

import jax
import jax.numpy as jnp
from jax import lax
from jax.experimental import pallas as pl
from jax.experimental.pallas import tpu as pltpu

F32 = jnp.float32
BF16 = jnp.bfloat16

D_MODEL = 1024
EPS = 1e-6
GRID_W = 64
GLA_HEADS = 4
GLA_DK = 512
GLA_DV = 1024
GLA_DKH = 128
GLA_DVH = 256
GLA_GATE_RANK = 16
GLA_GATE_NORM = 16.0
RWKV_HEAD = 64
RWKV_HEADS = 16
RWKV_PAIRS = 8
RWKV_RANK = 64
LNX_EPS = 64e-5
RWKV_DECAY_SCALE = 0.6065306597126334
CHUNK = 64
SEQ_STEPS = 16
GLA_STEPS = 8
INV_BASE = 8
TOKEN_TILE = 512
WIDE_TILE = 1024
MOD_ROWS = 256
FINISH_GROUP = 32
PREP_GROUP = 8
LANES = 128
VMEM_LIMIT = 56 * 1024 * 1024


def _dot(a, b):
    return jnp.dot(a.astype(BF16), b.astype(BF16), preferred_element_type=F32)


def _dot_nt(a, b):
    return lax.dot_general(a.astype(BF16), b.astype(BF16), (((1,), (1,)), ((), ())),
                           preferred_element_type=F32)


def _dot_tn(a, b):
    return lax.dot_general(a.astype(BF16), b.astype(BF16), (((0,), (0,)), ((), ())),
                           preferred_element_type=F32)


def _tri_dot(tri, x):
    hi = x.astype(BF16)
    lo = (x - hi.astype(F32)).astype(BF16)
    t = tri.astype(BF16)
    return jnp.dot(t, hi, preferred_element_type=F32) + jnp.dot(t, lo, preferred_element_type=F32)


def _log_sigmoid(z):
    return jnp.minimum(z, 0.0) - jnp.log(1.0 + jnp.exp(-jnp.abs(z)))


def _sigmoid(z):
    return 0.5 * jnp.tanh(0.5 * z) + 0.5


def _silu(z):
    return z * _sigmoid(z)


def _rms(x):
    return x * lax.rsqrt(jnp.mean(x * x, axis=-1, keepdims=True) + EPS)


def _params(n_grid_dims):
    return pltpu.CompilerParams(dimension_semantics=("arbitrary",) * n_grid_dims,
                                vmem_limit_bytes=VMEM_LIMIT)


def _mm_kernel(x_ref, w_ref, b_ref, o_ref):
    @pl.when(pl.program_id(0) == 0)
    def _():
        o_ref[...] = jnp.broadcast_to(b_ref[...], o_ref.shape)

    o_ref[...] += _dot(x_ref[...], w_ref[0])


def _matmul_bias(x, w, layer, b):
    m, k = x.shape
    n = w.shape[2]
    tk = min(k, MOD_ROWS)
    assert k % tk == 0
    return pl.pallas_call(
        _mm_kernel,
        grid=(k // tk,),
        in_specs=[pl.BlockSpec((m, tk), lambda j: (0, j)),
                  pl.BlockSpec((1, tk, n), lambda j: (layer, j, 0)),
                  pl.BlockSpec((1, n), lambda j: (0, 0))],
        out_specs=pl.BlockSpec((m, n), lambda j: (0, 0)),
        out_shape=jax.ShapeDtypeStruct((m, n), F32),
        compiler_params=_params(1),
        name="mod_matmul",
    )(x, w, b)


def _gla_front_kernel(x_ref, mod_ref, nw_ref, win_ref, wa1_ref, wa2_ref, ba_ref,
                      q_ref, k_ref, v_ref, g_ref, lg_ref):
    d = D_MODEL
    mod = mod_ref[0]
    shift, scale = mod[:, :d], mod[:, d:2 * d]
    h = _rms(x_ref[...]) * nw_ref[...] * (1.0 + scale) + shift
    hb = h.astype(BF16)
    proj = lambda lo, hi: jnp.dot(hb, win_ref[:, lo:hi], preferred_element_type=F32)
    q_ref[...] = (proj(0, GLA_DK) * (GLA_DKH ** -0.5)).astype(BF16)
    k_ref[...] = proj(GLA_DK, 2 * GLA_DK).astype(BF16)
    v_ref[...] = proj(2 * GLA_DK, 2 * GLA_DK + GLA_DV).astype(BF16)
    g_ref[...] = proj(2 * GLA_DK + GLA_DV, 2 * GLA_DK + 2 * GLA_DV).astype(BF16)
    t = jnp.dot(hb, wa1_ref[...], preferred_element_type=F32)
    z = _dot(t, wa2_ref[...]) + ba_ref[...]
    lg_ref[...] = _log_sigmoid(z) * (1.0 / GLA_GATE_NORM)


def _gla_front(x, mod, mod_idx, tm, norm_w, w_in, w_a1, w_a2, b_a):
    n, d = x.shape
    nmod = mod.shape[0]
    wa1 = jnp.zeros((d, LANES), F32).at[:, :GLA_GATE_RANK].set(w_a1[0]).at[:, GLA_GATE_RANK:2 * GLA_GATE_RANK].set(w_a1[1])
    wa2 = jnp.zeros((LANES, 2 * GLA_DK), F32).at[:GLA_GATE_RANK, :GLA_DK].set(w_a2[0])
    wa2 = wa2.at[GLA_GATE_RANK:2 * GLA_GATE_RANK, GLA_DK:].set(w_a2[1])
    ba = b_a.reshape(1, 2 * GLA_DK)
    row = lambda i: (i, 0)
    const = lambda i: (0, 0)
    n_in = w_in.shape[1]
    outs = pl.pallas_call(
        _gla_front_kernel,
        grid=(n // tm,),
        in_specs=[pl.BlockSpec((tm, d), row),
                  pl.BlockSpec((1, 1, 3 * d), lambda i: (mod_idx(i), 0, 0)),
                  pl.BlockSpec((1, d), const),
                  pl.BlockSpec((d, n_in), const),
                  pl.BlockSpec((d, LANES), const),
                  pl.BlockSpec((LANES, 2 * GLA_DK), const),
                  pl.BlockSpec((1, 2 * GLA_DK), const)],
        out_specs=[pl.BlockSpec((tm, GLA_DK), row), pl.BlockSpec((tm, GLA_DK), row),
                   pl.BlockSpec((tm, GLA_DV), row), pl.BlockSpec((tm, GLA_DV), row),
                   pl.BlockSpec((tm, 2 * GLA_DK), row)],
        out_shape=[jax.ShapeDtypeStruct((n, GLA_DK), BF16), jax.ShapeDtypeStruct((n, GLA_DK), BF16),
                   jax.ShapeDtypeStruct((n, GLA_DV), BF16), jax.ShapeDtypeStruct((n, GLA_DV), BF16),
                   jax.ShapeDtypeStruct((n, 2 * GLA_DK), F32)],
        compiler_params=_params(1),
        name="gla_front",
    )(x, mod.reshape(nmod, 1, 3 * d), norm_w.reshape(1, d), w_in.astype(BF16), wa1.astype(BF16),
      wa2.astype(BF16), ba)
    return outs


def _gla_chunks(chains):
    flat = [(x, ch[2]) for ch in chains for x in ch[0]]
    c = flat[0][0][0].shape[0]
    dk = flat[0][0][1].shape[1]
    row = lax.broadcasted_iota(jnp.int32, (c, c), 0)
    col = lax.broadcasted_iota(jnp.int32, (c, c), 1)
    keep = [(col >= row) if rv else (col <= row) for _, rv in flat]
    b = [_tri_dot(kp.astype(F32), x[3]) for kp, (x, _) in zip(keep, flat)]
    b_end = [bb[0:1] if rv else bb[c - 1:c] for bb, (_, rv) in zip(b, flat)]
    qb = [(x[0] * jnp.exp(bb)).astype(BF16) for (x, _), bb in zip(flat, b)]
    kb = [x[1] * jnp.exp(-bb) for (x, _), bb in zip(flat, b)]
    kd = [x[1] * jnp.exp(be - bb) for (x, _), bb, be in zip(flat, b, b_end)]
    scores = [jnp.where(kp, _dot_nt(x, y), 0.0).astype(BF16) for kp, x, y in zip(keep, qb, kb)]
    decay = [jnp.broadcast_to(jnp.exp(be), (dk, dk)).T for be in b_end]
    n_steps = len(chains[0][0])
    st = [ch[1] for ch in chains]
    outs = [[] for _ in chains]
    for s in range(n_steps):
        at = [ci * n_steps + s for ci in range(len(chains))]
        for ci, f in enumerate(at):
            outs[ci].append(_dot(jnp.concatenate([scores[f], qb[f]], axis=1),
                                 jnp.concatenate([flat[f][0][2], st[ci].astype(BF16)], axis=0)))
        st = [st[ci] * jnp.concatenate([decay[f]] * (st[ci].shape[1] // dk), axis=1) + _dot_tn(kd[f], flat[f][0][2])
              for ci, f in enumerate(at)]
    return list(zip(outs, st))


def _gla_scan_kernel(*refs, has_s0, emit_state):
    q_ref, k_ref, v_ref, lf_ref, lb_ref = refs[:5]
    s0_ref = refs[5] if has_s0 else None
    o_ref = refs[5 + has_s0]
    s_ref = refs[6 + has_s0] if emit_state else None
    acc_ref, st_ref = refs[-2:]
    bb, t = q_ref.shape[0], q_ref.shape[1]
    hb = q_ref.shape[2] // GLA_DKH
    nc = t // CHUNK
    half = nc // 2
    kh = lambda h: slice(h * GLA_DKH, (h + 1) * GLA_DKH)
    vh = lambda h: slice(h * GLA_DVH, (h + 1) * GLA_DVH)

    steps = GLA_STEPS
    while half % steps:
        steps //= 2

    def both(accumulate):
        def body(i, carry):
            sls = [[pl.ds(pl.multiple_of(ci * CHUNK, CHUNK), CHUNK)
                    for ci in ((i * steps + s, nc - 1 - i * steps - s)[d] for s in range(steps))] for d in range(2)]
            chains = [([(q_ref[e, sl, kh(h)].astype(F32), k_ref[e, sl, kh(h)].astype(F32), v_ref[e, sl, vh(h)],
                         (lf_ref, lb_ref)[d][e, sl, kh(h)]) for sl in sls[d]], st_ref[n], d == 1)
                      for n, (e, d, h) in enumerate(idx)]
            res = _gla_chunks(chains)
            for n, (e, d, h) in enumerate(idx):
                for sl, o in zip(sls[d], res[n][0]):
                    if accumulate:
                        o_ref[e, sl, vh(h)] = (acc_ref[e, sl, vh(h)] + o).astype(BF16)
                    else:
                        acc_ref[e, sl, vh(h)] = o
            for n, (_, st) in enumerate(res):
                st_ref[n] = st
            return carry
        return body

    idx = [(e, d, h) for e in range(bb) for d in range(2) for h in range(hb)]
    for n, (e, d, h) in enumerate(idx):
        st_ref[n] = s0_ref[e, d, h] if has_s0 else jnp.zeros((GLA_DKH, GLA_DVH), F32)
    lax.fori_loop(0, half // steps, both(False), 0)
    lax.fori_loop(half // steps, nc // steps, both(True), 0)
    if emit_state:
        for n, (e, d, h) in enumerate(idx):
            s_ref[e, d, h] = st_ref[n]


def _gla_scan(q, k, v, lg, s0, bb, hb, emit_state):
    bsz, t, _ = q.shape
    ng = GLA_HEADS // hb
    hd = lambda b, h: (b, 0, h)
    kblk = pl.BlockSpec((bb, t, hb * GLA_DKH), hd)
    vblk = pl.BlockSpec((bb, t, hb * GLA_DVH), hd)
    sblk = pl.BlockSpec((bb, 2, hb, GLA_DKH, GLA_DVH), lambda b, h: (b, 0, h, 0, 0))
    s_shape = jax.ShapeDtypeStruct((bsz, 2, GLA_HEADS, GLA_DKH, GLA_DVH), F32)
    has_s0 = s0 is not None
    return pl.pallas_call(
        functools.partial(_gla_scan_kernel, has_s0=has_s0, emit_state=emit_state),
        grid=(bsz // bb, ng),
        in_specs=[kblk, kblk, vblk, kblk, pl.BlockSpec((bb, t, hb * GLA_DKH), lambda b, h: (b, 0, ng + h))]
        + [sblk] * has_s0,
        out_specs=[vblk] + [sblk] * emit_state,
        out_shape=[jax.ShapeDtypeStruct((bsz, t, GLA_DV), BF16)] + [s_shape] * emit_state,
        scratch_shapes=[pltpu.VMEM((bb, t, hb * GLA_DVH), F32), pltpu.VMEM((2 * bb * hb, GLA_DKH, GLA_DVH), F32)],
        compiler_params=_params(2),
        name="gla_scan",
    )(q, k, v, lg, lg, *([s0] * has_s0))


def _gla_back_kernel(o_ref, g_ref, x_ref, mod0_ref, mod1_ref, gn_ref, wout_ref, nw1_ref, x1_ref, h1_ref):
    d = D_MODEL
    o = o_ref[...].astype(F32)
    gn = gn_ref[...]
    parts = [_rms(o[:, h * GLA_DVH:(h + 1) * GLA_DVH]) * gn for h in range(GLA_HEADS)]
    on = jnp.concatenate(parts, axis=-1) * _silu(g_ref[...].astype(F32))
    out = _dot(on, wout_ref[...])
    gate = mod0_ref[0][:, 2 * d:]
    x1 = x_ref[...] + gate * out
    x1_ref[...] = x1
    mod1 = mod1_ref[0]
    h1_ref[...] = (_rms(x1) * nw1_ref[...] * (1.0 + mod1[:, d:2 * d]) + mod1[:, :d]).astype(BF16)


def _gla_back(o, g, x, mod0, mod1, mod_idx, tm, gla_norm, w_out, norm_w1):
    n, d = x.shape
    nmod = mod0.shape[0]
    row = lambda i: (i, 0)
    const = lambda i: (0, 0)
    modspec = pl.BlockSpec((1, 1, 3 * d), lambda i: (mod_idx(i), 0, 0))
    return pl.pallas_call(
        _gla_back_kernel,
        grid=(n // tm,),
        in_specs=[pl.BlockSpec((tm, d), row), pl.BlockSpec((tm, d), row), pl.BlockSpec((tm, d), row),
                  modspec, modspec,
                  pl.BlockSpec((1, GLA_DVH), const), pl.BlockSpec((d, d), const), pl.BlockSpec((1, d), const)],
        out_specs=[pl.BlockSpec((tm, d), row), pl.BlockSpec((tm, d), row)],
        out_shape=[jax.ShapeDtypeStruct((n, d), F32), jax.ShapeDtypeStruct((n, d), BF16)],
        compiler_params=_params(1),
        name="gla_back",
    )(o, g, x, mod0.reshape(nmod, 1, 3 * d), mod1.reshape(nmod, 1, 3 * d), gla_norm.reshape(1, GLA_DVH),
      w_out.astype(BF16), norm_w1.reshape(1, d))


def _token_shift(h, up_ref, dn_ref, seq_len, grid_w):
    tm = h.shape[0]
    pos = lax.broadcasted_iota(jnp.int32, (tm, 1), 0) + pl.program_id(0) * tm
    prev = pltpu.roll(h, 1, 0)
    nxt = pltpu.roll(h, tm - 1, 0)
    t = pos % seq_len
    if grid_w is None:
        return 0.5 * (jnp.where(t == 0, 0.0, prev) + jnp.where(t == seq_len - 1, 0.0, nxt))
    col = pos % grid_w
    up = jnp.concatenate([up_ref[...].astype(F32), h[:tm - grid_w]], axis=0)
    dn = jnp.concatenate([h[grid_w:], dn_ref[...].astype(F32)], axis=0)
    up = jnp.where(t < grid_w, 0.0, up)
    dn = jnp.where(t >= seq_len - grid_w, 0.0, dn)
    left = jnp.where(col == 0, 0.0, prev)
    right = jnp.where(col == grid_w - 1, 0.0, nxt)
    return 0.25 * (up + dn + left + right)


def _rwkv_front_kernel(*refs, seq_len, grid_w):
    n_h = 1 if grid_w is None else 3
    h_ref, up_ref, dn_ref = (tuple(refs[:n_h]) + (None, None))[:3]
    (mu_ref, wrkvg_ref, w1_ref, w2_ref, w0_ref, a1_ref, a2_ref, a0_ref,
     r_ref, k_ref, v_ref, g_ref, lw_ref, a_ref) = refs[n_h:]
    h = h_ref[...].astype(F32)
    dh = _token_shift(h, up_ref, dn_ref, seq_len, grid_w) - h
    mix = lambda p: (h + dh * mu_ref[p:p + 1, :]).astype(BF16)
    r_ref[...] = jnp.dot(mix(0), wrkvg_ref[0], preferred_element_type=F32).astype(BF16)
    k_ref[...] = jnp.dot(mix(1), wrkvg_ref[1], preferred_element_type=F32).astype(BF16)
    v_ref[...] = jnp.dot(mix(2), wrkvg_ref[2], preferred_element_type=F32).astype(BF16)
    g_ref[...] = jnp.dot(mix(3), wrkvg_ref[3], preferred_element_type=F32).astype(BF16)
    tw = jnp.tanh(jnp.dot(mix(4), w1_ref[...], preferred_element_type=F32))
    lw_ref[...] = -RWKV_DECAY_SCALE * _sigmoid(w0_ref[...] + _dot(tw, w2_ref[...]))
    ta = jnp.dot(mix(5), a1_ref[...], preferred_element_type=F32)
    a_ref[...] = _sigmoid(a0_ref[...] + _dot(ta, a2_ref[...])).astype(BF16)


def _rwkv_front(h, tm, seq_len, grid_w, mu, w_rkvg, w0, w1, w2, a0, a1, a2):
    n, d = h.shape
    rk = RWKV_RANK
    cat1 = lambda w: jnp.concatenate([w[0], w[1]], axis=1)
    bd2 = lambda w: jnp.zeros((2 * rk, 2 * d), F32).at[:rk, :d].set(w[0]).at[rk:, d:].set(w[1])
    row = lambda i: (i, 0)
    const = lambda i: (0, 0)
    if grid_w is None:
        assert tm % seq_len == 0
        h_specs, h_args = [pl.BlockSpec((tm, d), row)], [h]
    else:
        assert tm % grid_w == 0 and seq_len % tm == 0
        rows_per_tile, last = tm // grid_w, n // grid_w - 1
        h_specs = [pl.BlockSpec((tm, d), row),
                   pl.BlockSpec((grid_w, d), lambda i: (jnp.maximum(i * rows_per_tile - 1, 0), 0)),
                   pl.BlockSpec((grid_w, d), lambda i: (jnp.minimum((i + 1) * rows_per_tile, last), 0))]
        h_args = [h, h, h]
    return pl.pallas_call(
        functools.partial(_rwkv_front_kernel, seq_len=seq_len, grid_w=grid_w),
        grid=(n // tm,),
        in_specs=h_specs + [
            pl.BlockSpec((8, d), const),
            pl.BlockSpec((4, d, d), lambda i: (0, 0, 0)),
            pl.BlockSpec((d, 2 * rk), const), pl.BlockSpec((2 * rk, 2 * d), const), pl.BlockSpec((1, 2 * d), const),
            pl.BlockSpec((d, 2 * rk), const), pl.BlockSpec((2 * rk, 2 * d), const), pl.BlockSpec((1, 2 * d), const)],
        out_specs=[pl.BlockSpec((tm, d), row)] * 4 + [pl.BlockSpec((tm, 2 * d), row)] * 2,
        out_shape=[jax.ShapeDtypeStruct((n, d), BF16)] * 4
        + [jax.ShapeDtypeStruct((n, 2 * d), F32), jax.ShapeDtypeStruct((n, 2 * d), BF16)],
        compiler_params=_params(1),
        name="rwkv_front",
    )(*h_args, jnp.zeros((8, d), F32).at[:6].set(mu), w_rkvg.astype(BF16),
      cat1(w1).astype(BF16), bd2(w2).astype(BF16), w0.reshape(1, 2 * d),
      cat1(a1).astype(BF16), bd2(a2).astype(BF16), a0.reshape(1, 2 * d))


def _seg_sum(x, head0):
    s0 = jnp.sum(jnp.where(head0, x, 0.0), axis=-1, keepdims=True)
    s1 = jnp.sum(jnp.where(head0, 0.0, x), axis=-1, keepdims=True)
    return jnp.where(head0, s0, s1)


def _stack(x, head0):
    return jnp.concatenate([jnp.where(head0, x, 0.0), jnp.where(head0, 0.0, x)], axis=0)


def _rwkv_chunk_terms(insts):
    n = len(insts)
    c = insts[0][0].shape[0]
    lane = lax.broadcasted_iota(jnp.int32, (c, LANES), 1)
    head0 = lane < RWKV_HEAD
    row = lax.broadcasted_iota(jnp.int32, (c, c), 0)
    col = lax.broadcasted_iota(jnp.int32, (c, c), 1)
    ti = lax.broadcasted_iota(jnp.int32, (c, LANES), 0)
    si = lane & (c - 1)
    prow = lax.broadcasted_iota(jnp.int32, (LANES, LANES), 0)
    pcol = lax.broadcasted_iota(jnp.int32, (LANES, LANES), 1)
    same_head = (prow >= RWKV_HEAD) == (pcol >= RWKV_HEAD)
    each = lambda f, *ls: [f(*xs) for xs in zip(*ls)]
    rev = [x[8] for x in insts]
    stk = lambda x: _stack(x, head0).astype(BF16)

    cum = [_tri_dot(((col >= row) if x[8] else (col <= row)).astype(F32), x[3]) for x in insts]
    tot = [cm[0:1] if rv else cm[c - 1:c] for cm, rv in zip(cum, rev)]

    a_t, r_t, b_ts, k_ts, b_h, k_h, v_s, vv, bonus = [], [], [], [], [], [], [], [], []
    for (r, k, v, lw, a, k_k, k_a, r_k, _), cm, tt in zip(insts, cum, tot):
        kk = k * k_k
        kk = kk * lax.rsqrt(jnp.maximum(_seg_sum(kk * kk, head0), 1e-24))
        kd = k * (1.0 + (a - 1.0) * k_a)
        bv = kk * a
        bonus.append(_seg_sum(r * kd * r_k, head0) * v)
        e_neg = jnp.exp(-cm)
        e_tot = jnp.exp(tt)
        b_neg, k_neg = bv * e_neg, kd * e_neg
        a_t.append(-kk * jnp.exp(cm - lw))
        r_t.append(r * jnp.exp(cm))
        b_ts.append(stk(b_neg))
        k_ts.append(stk(k_neg))
        b_h.append(b_neg * e_tot)
        k_h.append(k_neg * e_tot)
        v_s.append(stk(v))
        vv.append(v)

    strict = [(si > ti) if rv else (si < ti) for rv in rev]
    incl = [(si >= ti) if rv else (si <= ti) for rv in rev]
    ar = each(lambda x, y: jnp.concatenate([x, y], axis=0).astype(BF16), a_t, r_t)
    g_bk = each(lambda x, y, z: _dot_nt(x, jnp.concatenate([y, z], axis=0)), ar, b_ts, k_ts)
    g_b = [x[:, :LANES] for x in g_bk]
    g_k = [x[:, LANES:] for x in g_bk]
    n_ab = each(lambda m_, x: jnp.where(m_, x[:c], 0.0), strict, g_b)
    l_ak = each(lambda m_, x: jnp.where(m_, x[:c], 0.0), strict, g_k)
    p_rb = each(lambda m_, x: jnp.where(m_, x[c:], 0.0), incl, g_b)
    p_rk = each(lambda m_, x: jnp.where(m_, x[c:], 0.0), incl, g_k)

    same_blk = lambda h: (ti // h) == (si // h)
    n_d = each(lambda x: jnp.where(same_blk(INV_BASE), x, 0.0), n_ab)
    steps = max(1, (INV_BASE - 1).bit_length())
    q = n_d
    tinv = n_d
    q = each(lambda x: _dot(x, stk(x)), q) if steps > 1 else q
    for j in range(1, steps):
        last = j == steps - 1
        qs = each(stk, q)
        if last:
            z = each(_dot, tinv, qs)
            tinv = each(lambda t_, x, y: t_ + x + y, tinv, q, z)
        else:
            both = each(lambda x, t_, w_: _dot(jnp.concatenate([x, t_], axis=0), w_), q, tinv, qs)
            tinv = each(lambda t_, x, y: t_ + x + y[c:], tinv, q, both)
            q = each(lambda y: y[:c], both)
    pair = lambda f, x, y: _dot(f, jnp.concatenate([stk(x), stk(y)], axis=1))
    link = lambda h: same_blk(2 * h) & jnp.logical_not(same_blk(h))
    n_off = lambda h: [jnp.where(link(h), x, 0.0) for x in n_ab]
    grow = lambda t_, y, z: t_ + y + z
    h = INV_BASE
    while h < c:
        if 4 * h <= c:
            n_1, n_2 = n_off(h), n_off(2 * h)
            p_1 = each(pair, tinv, n_1, n_2)
            y_1 = each(lambda x, z: x + z[:, :LANES], n_1, p_1)
            b_2 = each(lambda x, z: x + z[:, LANES:], n_2, p_1)
            p_2 = each(pair, y_1, tinv, b_2)
            tinv = each(lambda t_, y, z: grow(t_, y, z[:, :LANES]), tinv, y_1, p_2)
            y_2 = each(lambda x, z: x + z[:, LANES:], b_2, p_2)
            tinv = each(lambda t_, y: grow(t_, y, _dot(y, stk(t_))), tinv, y_2)
            h *= 4
        else:
            y = each(lambda t_, x: x + _dot(t_, stk(x)), tinv, n_off(h))
            tinv = each(lambda t_, y_: grow(t_, y_, _dot(y_, stk(t_))), tinv, y)
            h *= 2
    lv = each(lambda l_, p, y: _dot(jnp.concatenate([l_, p], axis=0), y), l_ak, p_rk, v_s)
    w = [x[:c] for x in lv]
    au = each(pair, tinv, a_t, w)
    a_bar = each(lambda x, z: x + z[:, :LANES], a_t, au)
    u0 = each(lambda x, z: x + z[:, LANES:], w, au)
    ry = each(pair, p_rb, a_bar, u0)
    r_bar = each(lambda x, z: x + z[:, :LANES], r_t, ry)
    y0 = each(lambda z, l_: z[:, LANES:] + l_[c:], ry, lv)
    zero = jnp.zeros((c, LANES), F32)
    ms = each(lambda ab, u, v_, bh, kh_: _dot_tn(
        jnp.concatenate([jnp.concatenate([ab, u], axis=1), jnp.concatenate([zero, v_], axis=1)], axis=0),
        jnp.concatenate([bh, kh_], axis=0)), a_bar, u0, vv, b_h, k_h)
    m = [jnp.where(same_head, x[:LANES], 0.0) for x in ms]
    s0p = [jnp.where(same_head, x[LANES:], 0.0) for x in ms]
    return [(r_bar[i], y0[i], m[i], s0p[i], jnp.exp(tot[i]), bonus[i]) for i in range(n)]


def _rwkv_scan_kernel(*refs, has_s0, emit_state):
    (r_ref, k_ref, v_ref, lwf_ref, lwb_ref, af_ref, ab_ref,
     kkf_ref, kkb_ref, kaf_ref, kab_ref, rkf_ref, rkb_ref, lnw_ref, lnb_ref) = refs[:15]
    s0_ref = refs[15] if has_s0 else None
    o_ref = refs[15 + has_s0]
    s_ref = refs[16 + has_s0] if emit_state else None
    w_ref, y0_ref, s0p_ref, dec_ref, bonus_ref, yf_ref, yb_ref, st_ref = refs[16 + has_s0 + emit_state:]
    bb, t = r_ref.shape[0], r_ref.shape[1]
    nc = t // CHUNK
    dirs = ((lwf_ref, af_ref, kkf_ref, kaf_ref, rkf_ref, False), (lwb_ref, ab_ref, kkb_ref, kab_ref, rkb_ref, True))
    group = min(bb * nc, PREP_GROUP)
    assert (bb * nc) % group == 0
    chunk_rows = lambda ci: pl.ds(pl.multiple_of(ci * CHUNK, CHUNK), CHUNK)

    def prep(gi, carry):
        insts, where = [], []
        at = [((gi * group + j) // nc, chunk_rows((gi * group + j) % nc)) for j in range(group)]
        for j, (e, sl) in enumerate(at):
            r, k, v = (ref[e, sl, :].astype(F32) for ref in (r_ref, k_ref, v_ref))
            for d, (lw_ref, a_ref, kk_ref, ka_ref, rk_ref, reverse) in enumerate(dirs):
                insts.append((r, k, v, lw_ref[e, sl, :], a_ref[e, sl, :].astype(F32), kk_ref[...], ka_ref[...],
                              rk_ref[...], reverse))
                where.append((d, gi * group + j, e, sl))
        terms = _rwkv_chunk_terms(insts)
        for (d, fi, e, sl), (r_bar, y0, m, s0p, dec, bon) in zip(where, terms):
            rbar_t = jnp.concatenate([r_bar, jnp.zeros_like(r_bar)], axis=0).T
            w_ref[d, fi] = jnp.concatenate([m, rbar_t], axis=1).astype(BF16)
            y0_ref[d, fi] = y0
            s0p_ref[d, fi] = s0p
            dec_ref[d, fi] = jnp.broadcast_to(dec, (8, LANES))
        for j in range(group):
            _, _, e, sl = where[2 * j]
            bonus_ref[e, sl, :] = terms[2 * j][5] + terms[2 * j + 1][5]
        return carry

    lax.fori_loop(0, bb * nc // group, prep, 0)

    seq_steps = SEQ_STEPS
    while nc % seq_steps:
        seq_steps //= 2
    idx = [(d, e) for d in range(2) for e in range(bb)]

    def seq(i, _):
        carry = tuple(st_ref[n] for n in range(len(idx)))
        for s in range(seq_steps):
            cis = (i * seq_steps + s, nc - 1 - i * seq_steps - s)
            sw = [jnp.dot(carry[n].astype(BF16), w_ref[d, e * nc + cis[d]], preferred_element_type=F32)
                  for n, (d, e) in enumerate(idx)]
            for n, (d, e) in enumerate(idx):
                y = sw[n][:, LANES:].T[:CHUNK] + y0_ref[d, e * nc + cis[d]]
                (yf_ref, yb_ref)[d][e, chunk_rows(cis[d]), :] = y
            carry = tuple(carry[n] * dec_ref[d, e * nc + cis[d]][0:1] + sw[n][:, :LANES] + s0p_ref[d, e * nc + cis[d]]
                          for n, (d, e) in enumerate(idx))
        for n, st in enumerate(carry):
            st_ref[n] = st
        return 0

    for n, (d, e) in enumerate(idx):
        st_ref[n] = s0_ref[e, d, 0] if has_s0 else jnp.zeros((LANES, LANES), F32)
    lax.fori_loop(0, nc // seq_steps, seq, 0)
    if emit_state:
        for d in range(2):
            for e in range(bb):
                s = st_ref[d * bb + e]
                s_ref[e, d, 0] = s[:RWKV_HEAD, :RWKV_HEAD]
                s_ref[e, d, 1] = pltpu.roll(s, RWKV_HEAD, 1)[RWKV_HEAD:, :RWKV_HEAD]

    prow = lax.broadcasted_iota(jnp.int32, (LANES, LANES), 0)
    pcol = lax.broadcasted_iota(jnp.int32, (LANES, LANES), 1)
    head_ones = ((prow >= RWKV_HEAD) == (pcol >= RWKV_HEAD)).astype(BF16)
    inv_n = 1.0 / RWKV_HEAD

    def head_mean(x):
        hi = x.astype(BF16)
        lo = (x - hi.astype(F32)).astype(BF16)
        both = jnp.dot(jnp.concatenate([hi, lo], axis=0), head_ones, preferred_element_type=F32)
        return (both[:x.shape[0]] + both[x.shape[0]:]) * inv_n

    fgroup = min(bb * nc, FINISH_GROUP)
    assert (bb * nc) % fgroup == 0

    def finish(gi, carry):
        at = [((gi * fgroup + j) // nc, chunk_rows((gi * fgroup + j) % nc)) for j in range(fgroup)]
        y = [yf_ref[e, sl, :] + yb_ref[e, sl, :] for e, sl in at]
        yc = [x - head_mean(x) for x in y]
        var = [head_mean(x * x) for x in yc]
        for (e, sl), x, vr in zip(at, yc, var):
            o_ref[e, sl, :] = (x * lax.rsqrt(vr + LNX_EPS) * lnw_ref[...] + lnb_ref[...]
                               + bonus_ref[e, sl, :]).astype(BF16)
        return carry

    lax.fori_loop(0, bb * nc // fgroup, finish, 0)


def _rwkv_scan(r, k, v, lw, a, k_k, k_a, r_k, lnx_w, lnx_b, s0, bb, emit_state):
    bsz, t, d = r.shape
    nc = t // CHUNK
    seq = lambda b, p: (b, 0, p)
    seq_b = lambda b, p: (b, 0, RWKV_PAIRS + p)
    par = lambda b, p: (0, p)
    par_b = lambda b, p: (0, RWKV_PAIRS + p)
    sblk = pl.BlockSpec((bb, t, LANES), seq)
    sblk_b = pl.BlockSpec((bb, t, LANES), seq_b)
    pblk = pl.BlockSpec((1, LANES), par)
    pblk_b = pl.BlockSpec((1, LANES), par_b)
    s0blk = pl.BlockSpec((bb, 2, 1, LANES, LANES), lambda b, p: (b, 0, p, 0, 0))
    s_blk = pl.BlockSpec((bb, 2, 2, RWKV_HEAD, RWKV_HEAD), lambda b, p: (b, 0, p, 0, 0))
    s_shape = jax.ShapeDtypeStruct((bsz, 2, RWKV_HEADS, RWKV_HEAD, RWKV_HEAD), F32)
    has_s0 = s0 is not None
    k_k, k_a, r_k = (p.reshape(1, 2 * d) for p in (k_k, k_a, r_k))
    return pl.pallas_call(
        functools.partial(_rwkv_scan_kernel, has_s0=has_s0, emit_state=emit_state),
        grid=(bsz // bb, RWKV_PAIRS),
        in_specs=[sblk, sblk, sblk, sblk, sblk_b, sblk, sblk_b,
                  pblk, pblk_b, pblk, pblk_b, pblk, pblk_b, pblk, pblk] + [s0blk] * has_s0,
        out_specs=[sblk] + [s_blk] * emit_state,
        out_shape=[jax.ShapeDtypeStruct((bsz, t, d), BF16)] + [s_shape] * emit_state,
        scratch_shapes=[pltpu.VMEM((2, bb * nc, LANES, 2 * LANES), BF16), pltpu.VMEM((2, bb * nc, CHUNK, LANES), F32),
                        pltpu.VMEM((2, bb * nc, LANES, LANES), F32),
                        pltpu.VMEM((2, bb * nc, 8, LANES), F32), pltpu.VMEM((bb, t, LANES), F32),
                        pltpu.VMEM((bb, t, LANES), F32), pltpu.VMEM((bb, t, LANES), F32),
                        pltpu.VMEM((2 * bb, LANES, LANES), F32)],
        compiler_params=_params(2),
        name="rwkv_scan",
    )(r, k, v, lw, lw, a, a, k_k, k_k, k_a, k_a, r_k, r_k,
      lnx_w.reshape(1, d), lnx_b.reshape(1, d), *([s0] * has_s0))


def _rwkv_back_kernel(o_ref, g_ref, x_ref, mod_ref, wout_ref, nf_ref, y_ref):
    out = _dot(o_ref[...].astype(F32) * _silu(g_ref[...].astype(F32)), wout_ref[...])
    x2 = x_ref[...] + mod_ref[0][:, 2 * D_MODEL:] * out
    y_ref[...] = _rms(x2) * nf_ref[...]


def _rwkv_back(o, g, x, mod1, mod_idx, tm, w_out, norm_f):
    n, d = x.shape
    nmod = mod1.shape[0]
    row = lambda i: (i, 0)
    const = lambda i: (0, 0)
    return pl.pallas_call(
        _rwkv_back_kernel,
        grid=(n // tm,),
        in_specs=[pl.BlockSpec((tm, d), row), pl.BlockSpec((tm, d), row), pl.BlockSpec((tm, d), row),
                  pl.BlockSpec((1, 1, 3 * d), lambda i: (mod_idx(i), 0, 0)),
                  pl.BlockSpec((d, d), const), pl.BlockSpec((1, d), const)],
        out_specs=pl.BlockSpec((tm, d), row),
        out_shape=jax.ShapeDtypeStruct((n, d), F32),
        compiler_params=_params(1),
        name="rwkv_back",
    )(o, g, x, mod1.reshape(nmod, 1, 3 * d), w_out.astype(BF16), norm_f.reshape(1, d))


def _pair_blockdiag(s):
    b = s.shape[0]
    s = s.reshape(b, 2, RWKV_PAIRS, 2, RWKV_HEAD, RWKV_HEAD)
    z = jnp.zeros_like(s[:, :, :, 0])
    top = jnp.concatenate([s[:, :, :, 0], z], axis=-1)
    bot = jnp.concatenate([z, s[:, :, :, 1]], axis=-1)
    return jnp.concatenate([top, bot], axis=-2)


def _tile_rows(n_seq, t, may_span, rows):
    if t % rows == 0:
        return rows
    if may_span and rows % t == 0 and (n_seq * t) % rows == 0:
        return rows
    return t


def kernel(x_prompt, x_sample, state_gla, state_rwkv, c, c_ctx, w_mod, b_mod, norm_w, gla_w_in, gla_w_a1,
           gla_w_a2, gla_b_a, gla_norm, gla_w_out, rwkv_mu, rwkv_w_rkvg, rwkv_w0, rwkv_w1, rwkv_w2, rwkv_a0,
           rwkv_a1, rwkv_a2, rwkv_k_k, rwkv_k_a, rwkv_r_k, rwkv_lnx_w, rwkv_lnx_b, rwkv_w_out, norm_f):
    d = D_MODEL
    bp, tp, _ = x_prompt.shape
    bs, ts, _ = x_sample.shape
    assert tp % CHUNK == 0 and ts % CHUNK == 0 and ts % GRID_W == 0

    nrows = -(-(1 + bs) // 8) * 8
    cond = jnp.zeros((nrows, d), F32).at[0].set(c_ctx).at[1:1 + bs].set(c)
    cond = cond * (1.0 / (1.0 + jnp.exp(-cond)))
    mod0 = _matmul_bias(cond, w_mod, 0, b_mod[0].reshape(1, 3 * d))
    mod1 = _matmul_bias(cond, w_mod, 1, b_mod[1].reshape(1, 3 * d))

    def trunk(x3, shared_mod, grid_w, gla_s0, rwkv_s0, gla_bb, rwkv_bb, emit_state):
        b, t, _ = x3.shape
        n = b * t
        tm, tw = (_tile_rows(b, t, shared_mod, rows) for rows in (TOKEN_TILE, WIDE_TILE))
        mod_row = lambda rows: (lambda i: 0) if shared_mod else (lambda i: 1 + i // (t // rows))
        x = x3.reshape(n, d)
        seq = lambda arr: arr.reshape(b, t, arr.shape[-1])
        q, k, v, g, lg = _gla_front(x, mod0, mod_row(tw), tw, norm_w[0], gla_w_in[0], gla_w_a1[0], gla_w_a2[0], gla_b_a[0])
        o, *gla_s = _gla_scan(seq(q), seq(k), seq(v), seq(lg), gla_s0, gla_bb, GLA_HEADS, emit_state)
        x1, h1 = _gla_back(o.reshape(n, d), g, x, mod0, mod1, mod_row(tw), tw, gla_norm[0], gla_w_out[0], norm_w[1])
        r, k, v, g, lw, a = _rwkv_front(h1, tm, t, grid_w, rwkv_mu[0], rwkv_w_rkvg[0], rwkv_w0[0], rwkv_w1[0], rwkv_w2[0],
                                        rwkv_a0[0], rwkv_a1[0], rwkv_a2[0])
        o, *rwkv_s = _rwkv_scan(seq(r), seq(k), seq(v), seq(lw), seq(a), rwkv_k_k[0], rwkv_k_a[0], rwkv_r_k[0],
                                rwkv_lnx_w[0], rwkv_lnx_b[0], rwkv_s0, rwkv_bb, emit_state)
        y = _rwkv_back(o.reshape(n, d), g, x1, mod1, mod_row(tw), tw, rwkv_w_out[0], norm_f)
        return y.reshape(b, t, d), gla_s, rwkv_s

    y_prompt, (gla_s,), (rwkv_s,) = trunk(x_prompt, True, None, None, None, min(bp, 2), min(bp, 8), True)

    y_sample, _, _ = trunk(x_sample, False, GRID_W, state_gla[:, 0], _pair_blockdiag(state_rwkv[:, 0]),
                           1, min(bs, 4), False)
    return (y_prompt, y_sample, gla_s[:, None], rwkv_s[:, None])
```
